```python
import jax, jax.numpy as jnp
from jax import lax
import numpy as np

D_MODEL = 2048
BATCH = 16
SEQ = 2048
DEPTH = 4

CHUNK = 64
N_MIXERS = 3
N_A = (DEPTH + 2) // 3
N_B = (DEPTH + 1) // 3
N_C = DEPTH // 3

A_HEADS = 16
A_HEAD_DIM = D_MODEL // A_HEADS
A_WIDTH = A_HEADS * A_HEAD_DIM
LEFT_CHUNKS = 8
BAND = (LEFT_CHUNKS + 1) * CHUNK
REL_CLIP = 256
N_REL = 2 * REL_CLIP + 1

RG_WIDTH = 5 * D_MODEL // 4
RG_BLOCKS = 16
RG_BLOCK = RG_WIDTH // RG_BLOCKS
CONV_WIDTH = 4
RG_C = 8.0

C_HEADS = 16
C_HEAD_DIM = D_MODEL // C_HEADS
C_WIDTH = C_HEADS * C_HEAD_DIM
Q_BLOCK = 128

RMS_EPS = 1e-6
NEG_INF = -1e30

kernel_name = "hybrid_chunked_attn_rglru_fox_trunk"


def rmsnorm(x, g):
    xf = x.astype(jnp.float32)
    y = xf * lax.rsqrt(jnp.mean(xf * xf, axis=-1, keepdims=True) + RMS_EPS)
    return (y * g.astype(jnp.float32)).astype(x.dtype)


def chunked_rel_attention(q, k, v, rel_bias):
    B, S, H, dh = q.shape
    n_chunks = S // CHUNK
    pad = LEFT_CHUNKS * CHUNK
    kp = jnp.pad(k, ((0, 0), (pad, 0), (0, 0), (0, 0)))
    vp = jnp.pad(v, ((0, 0), (pad, 0), (0, 0), (0, 0)))
    dist = pad + jnp.arange(CHUNK)[:, None] - jnp.arange(BAND)[None, :]
    idx = jnp.clip(dist, -REL_CLIP, REL_CLIP) + REL_CLIP
    bias = rel_bias.astype(jnp.float32)[:, idx]
    scale = A_HEAD_DIM ** -0.5
    band_offsets = jnp.arange(BAND)

    def one_chunk(c):
        start = c * CHUNK
        qb = lax.dynamic_slice_in_dim(q, start, CHUNK, axis=1)
        kb = lax.dynamic_slice_in_dim(kp, start, BAND, axis=1)
        vb = lax.dynamic_slice_in_dim(vp, start, BAND, axis=1)
        s = jnp.einsum('bqhd,bkhd->bhqk', qb, kb).astype(jnp.float32) * scale + bias
        valid = (start - pad + band_offsets) >= 0
        s = jnp.where(valid[None, None, None, :], s, NEG_INF)
        p = jax.nn.softmax(s, axis=-1).astype(vb.dtype)
        return jnp.einsum('bhqk,bkhd->bqhd', p, vb)

    out = lax.map(one_chunk, jnp.arange(n_chunks))
    return out.transpose(1, 0, 2, 3, 4).reshape(B, S, H * dh)


def mixer_a(h, w_in, rel_bias, w_out):
    B, S, _ = h.shape
    proj = jnp.einsum('bsd,de->bse', h, w_in)
    q, k, v, g = jnp.split(proj, 4, axis=-1)
    shp = (B, S, A_HEADS, A_HEAD_DIM)
    o = chunked_rel_attention(q.reshape(shp), k.reshape(shp), v.reshape(shp), rel_bias)
    return jnp.einsum('bse,ed->bsd', o * jax.nn.silu(g), w_out)


def causal_depthwise_conv(x, w, b):
    S = x.shape[1]
    xp = jnp.pad(x, ((0, 0), (CONV_WIDTH - 1, 0), (0, 0)))
    y = xp[:, 0:S] * w[0]
    for t in range(1, CONV_WIDTH):
        y = y + xp[:, t:t + S] * w[t]
    return y + b


def rg_lru(x, w_a, b_a, w_x, b_x, lam):
    B, S, _ = x.shape
    xf = x.astype(jnp.float32)
    xb = xf.reshape(B, S, RG_BLOCKS, RG_BLOCK)
    r = jax.nn.sigmoid(jnp.einsum('bsnd,nde->bsne', xb, w_a.astype(jnp.float32)) + b_a.astype(jnp.float32))
    i = jax.nn.sigmoid(jnp.einsum('bsnd,nde->bsne', xb, w_x.astype(jnp.float32)) + b_x.astype(jnp.float32))
    r = r.reshape(B, S, RG_WIDTH)
    i = i.reshape(B, S, RG_WIDTH)
    log_a = -RG_C * jax.nn.softplus(-lam.astype(jnp.float32)) * r
    a = jnp.exp(log_a)
    u = jnp.sqrt(-jnp.expm1(2.0 * log_a)) * (i * xf)

    def combine(left, right):
        a1, b1 = left
        a2, b2 = right
        return a1 * a2, a2 * b1 + b2

    _, hs = lax.associative_scan(combine, (a, u), axis=1)
    return hs.astype(x.dtype)


def mixer_b(h, w_in, conv_w, conv_b, gate_a_w, gate_a_b, gate_x_w, gate_x_b, lam, w_out):
    proj = jnp.einsum('bsd,de->bse', h, w_in)
    xr, g = jnp.split(proj, 2, axis=-1)
    xr = causal_depthwise_conv(xr, conv_w, conv_b)
    hr = rg_lru(xr, gate_a_w, gate_a_b, gate_x_w, gate_x_b, lam)
    return jnp.einsum('bse,ed->bsd', hr * jax.nn.silu(g), w_out)


def forgetting_attention(q, k, v, log_f):
    B, S, H, dh = q.shape
    n_blocks = S // Q_BLOCK
    cum = jnp.cumsum(log_f, axis=1).transpose(0, 2, 1)
    kpos = jnp.arange(S)
    scale = C_HEAD_DIM ** -0.5

    def one_block(blk):
        start = blk * Q_BLOCK
        qb = lax.dynamic_slice_in_dim(q, start, Q_BLOCK, axis=1)
        cq = lax.dynamic_slice_in_dim(cum, start, Q_BLOCK, axis=2)
        s = jnp.einsum('bqhd,bkhd->bhqk', qb, k).astype(jnp.float32) * scale
        s = s + (cq[..., :, None] - cum[..., None, :])
        qpos = start + jnp.arange(Q_BLOCK)
        s = jnp.where((kpos[None, :] <= qpos[:, None])[None, None], s, NEG_INF)
        p = jax.nn.softmax(s, axis=-1).astype(v.dtype)
        return jnp.einsum('bhqk,bkhd->bqhd', p, v)

    out = lax.map(one_block, jnp.arange(n_blocks))
    return out.transpose(1, 0, 2, 3, 4).reshape(B, S, H * dh)


def mixer_c(h, w_in, f_bias, w_out):
    B, S, _ = h.shape
    proj = jnp.einsum('bsd,de->bse', h, w_in)
    q, k, v, g, f_logit = jnp.split(proj, [C_WIDTH, 2 * C_WIDTH, 3 * C_WIDTH, 4 * C_WIDTH], axis=-1)
    shp = (B, S, C_HEADS, C_HEAD_DIM)
    log_f = jax.nn.log_sigmoid(f_logit.astype(jnp.float32) + f_bias.astype(jnp.float32))
    o = forgetting_attention(q.reshape(shp), k.reshape(shp), v.reshape(shp), log_f)
    return jnp.einsum('bse,ed->bsd', o * jax.nn.silu(g), w_out)


def _dense(k, shape, fan_in):
    return jax.random.normal(k, shape, jnp.float32) * (fan_in ** -0.5)


def _fwd_setup_inputs(seed: int = 0) -> dict:
    key = jax.random.key(seed)
    ks = jax.random.split(key, 20)
    x = jax.random.normal(ks[0], (BATCH, SEQ, D_MODEL), jnp.float32)
    norm_pre = 1.0 + 0.02 * jax.random.normal(ks[1], (DEPTH, D_MODEL), jnp.float32)
    norm_post = 1.0 + 0.02 * jax.random.normal(ks[2], (DEPTH, D_MODEL), jnp.float32)
    a_w_in = _dense(ks[3], (N_A, D_MODEL, 4 * A_WIDTH), D_MODEL)
    a_rel_bias = 0.1 * jax.random.normal(ks[4], (N_A, A_HEADS, N_REL), jnp.float32)
    a_w_out = _dense(ks[5], (N_A, A_WIDTH, D_MODEL), A_WIDTH)
    b_w_in = _dense(ks[6], (N_B, D_MODEL, 2 * RG_WIDTH), D_MODEL)
    b_conv_w = _dense(ks[7], (N_B, CONV_WIDTH, RG_WIDTH), CONV_WIDTH)
    b_conv_b = 0.01 * jax.random.normal(ks[8], (N_B, RG_WIDTH), jnp.float32)
    b_gate_a_w = _dense(ks[9], (N_B, RG_BLOCKS, RG_BLOCK, RG_BLOCK), RG_BLOCK)
    b_gate_a_b = 0.01 * jax.random.normal(ks[10], (N_B, RG_BLOCKS, RG_BLOCK), jnp.float32)
    b_gate_x_w = _dense(ks[11], (N_B, RG_BLOCKS, RG_BLOCK, RG_BLOCK), RG_BLOCK)
    b_gate_x_b = 0.01 * jax.random.normal(ks[12], (N_B, RG_BLOCKS, RG_BLOCK), jnp.float32)
    u = jax.random.uniform(ks[13], (N_B, RG_WIDTH), jnp.float32, minval=0.9, maxval=0.999)
    a0 = u ** (1.0 / RG_C)
    b_lambda = jnp.log(a0) - jnp.log1p(-a0)
    b_w_out = _dense(ks[14], (N_B, RG_WIDTH, D_MODEL), RG_WIDTH)
    c_w_in = _dense(ks[15], (N_C, D_MODEL, 4 * C_WIDTH + C_HEADS), D_MODEL)
    c_f_bias = 3.0 + 0.5 * jax.random.normal(ks[16], (N_C, C_HEADS), jnp.float32)
    c_w_out = _dense(ks[17], (N_C, C_WIDTH, D_MODEL), C_WIDTH)
    return {"x": x, "norm_pre": norm_pre, "norm_post": norm_post,
            "a_w_in": a_w_in, "a_rel_bias": a_rel_bias, "a_w_out": a_w_out,
            "b_w_in": b_w_in, "b_conv_w": b_conv_w, "b_conv_b": b_conv_b,
            "b_gate_a_w": b_gate_a_w, "b_gate_a_b": b_gate_a_b,
            "b_gate_x_w": b_gate_x_w, "b_gate_x_b": b_gate_x_b,
            "b_lambda": b_lambda, "b_w_out": b_w_out,
            "c_w_in": c_w_in, "c_f_bias": c_f_bias, "c_w_out": c_w_out}


def _fwd_reference(x, norm_pre, norm_post, a_w_in, a_rel_bias, a_w_out, b_w_in, b_conv_w, b_conv_b,
              b_gate_a_w, b_gate_a_b, b_gate_x_w, b_gate_x_b, b_lambda, b_w_out,
              c_w_in, c_f_bias, c_w_out):
    for i in range(DEPTH):
        m = i % N_MIXERS
        j = i // N_MIXERS
        h = rmsnorm(x, norm_pre[i])
        if m == 0:
            y = mixer_a(h, a_w_in[j], a_rel_bias[j], a_w_out[j])
        elif m == 1:
            y = mixer_b(h, b_w_in[j], b_conv_w[j], b_conv_b[j], b_gate_a_w[j], b_gate_a_b[j],
                        b_gate_x_w[j], b_gate_x_b[j], b_lambda[j], b_w_out[j])
        else:
            y = mixer_c(h, c_w_in[j], c_f_bias[j], c_w_out[j])
        x = x + rmsnorm(y, norm_post[i])
    return x


import jax as _jax
import jax.numpy as _jnp

TWIN_FORMAT = 'train_step'
FWD_PARAMS = ['x', 'norm_pre', 'norm_post', 'a_w_in', 'a_rel_bias', 'a_w_out', 'b_w_in', 'b_conv_w', 'b_conv_b', 'b_gate_a_w', 'b_gate_a_b', 'b_gate_x_w', 'b_gate_x_b', 'b_lambda', 'b_w_out', 'c_w_in', 'c_f_bias', 'c_w_out']
TWIN_WEIGHTS = ['norm_pre', 'norm_post', 'a_w_in', 'a_rel_bias', 'a_w_out', 'b_w_in', 'b_conv_w', 'b_conv_b', 'b_gate_a_w', 'b_gate_a_b', 'b_gate_x_w', 'b_gate_x_b', 'b_lambda', 'b_w_out', 'c_w_in', 'c_f_bias', 'c_w_out']
TWIN_DIFF_INPUT = 'x'
TWIN_INPUTS = ['x', 'norm_pre', 'norm_post', 'a_w_in', 'a_rel_bias', 'a_w_out', 'b_w_in', 'b_conv_w', 'b_conv_b', 'b_gate_a_w', 'b_gate_a_b', 'b_gate_x_w', 'b_gate_x_b', 'b_lambda', 'b_w_out', 'c_w_in', 'c_f_bias', 'c_w_out', 'loss_target', 'm_norm_pre', 'm_norm_post', 'm_a_w_in', 'm_a_rel_bias', 'm_a_w_out', 'm_b_w_in', 'm_b_conv_w', 'm_b_conv_b', 'm_b_gate_a_w', 'm_b_gate_a_b', 'm_b_gate_x_w', 'm_b_gate_x_b', 'm_b_lambda', 'm_b_w_out', 'm_c_w_in', 'm_c_f_bias', 'm_c_w_out', 'v_norm_pre', 'v_norm_post', 'v_a_w_in', 'v_a_rel_bias', 'v_a_w_out', 'v_b_w_in', 'v_b_conv_w', 'v_b_conv_b', 'v_b_gate_a_w', 'v_b_gate_a_b', 'v_b_gate_x_w', 'v_b_gate_x_b', 'v_b_lambda', 'v_b_w_out', 'v_c_w_in', 'v_c_f_bias', 'v_c_w_out']
TWIN_OUTPUTS = ['loss', 'grad_x', 'grad_norm_pre', 'grad_norm_post', 'grad_a_w_in', 'grad_a_rel_bias', 'grad_a_w_out', 'grad_b_w_in', 'grad_b_conv_w', 'grad_b_conv_b', 'grad_b_gate_a_w', 'grad_b_gate_a_b', 'grad_b_gate_x_w', 'grad_b_gate_x_b', 'grad_b_lambda', 'grad_b_w_out', 'grad_c_w_in', 'grad_c_f_bias', 'grad_c_w_out', 'delta_norm_pre', 'delta_norm_post', 'delta_a_w_in', 'delta_a_rel_bias', 'delta_a_w_out', 'delta_b_w_in', 'delta_b_conv_w', 'delta_b_conv_b', 'delta_b_gate_a_w', 'delta_b_gate_a_b', 'delta_b_gate_x_w', 'delta_b_gate_x_b', 'delta_b_lambda', 'delta_b_w_out', 'delta_c_w_in', 'delta_c_f_bias', 'delta_c_w_out', 'new_m_norm_pre', 'new_m_norm_post', 'new_m_a_w_in', 'new_m_a_rel_bias', 'new_m_a_w_out', 'new_m_b_w_in', 'new_m_b_conv_w', 'new_m_b_conv_b', 'new_m_b_gate_a_w', 'new_m_b_gate_a_b', 'new_m_b_gate_x_w', 'new_m_b_gate_x_b', 'new_m_b_lambda', 'new_m_b_w_out', 'new_m_c_w_in', 'new_m_c_f_bias', 'new_m_c_w_out', 'new_v_norm_pre', 'new_v_norm_post', 'new_v_a_w_in', 'new_v_a_rel_bias', 'new_v_a_w_out', 'new_v_b_w_in', 'new_v_b_conv_w', 'new_v_b_conv_b', 'new_v_b_gate_a_w', 'new_v_b_gate_a_b', 'new_v_b_gate_x_w', 'new_v_b_gate_x_b', 'new_v_b_lambda', 'new_v_b_w_out', 'new_v_c_w_in', 'new_v_c_f_bias', 'new_v_c_w_out']
TWIN_LEAF_KINDS = {'loss': 'loss', 'grad_x': 'grad_x', 'grad_norm_pre': 'grad_w', 'grad_norm_post': 'grad_w', 'grad_a_w_in': 'grad_w', 'grad_a_rel_bias': 'grad_w', 'grad_a_w_out': 'grad_w', 'grad_b_w_in': 'grad_w', 'grad_b_conv_w': 'grad_w', 'grad_b_conv_b': 'grad_w', 'grad_b_gate_a_w': 'grad_w', 'grad_b_gate_a_b': 'grad_w', 'grad_b_gate_x_w': 'grad_w', 'grad_b_gate_x_b': 'grad_w', 'grad_b_lambda': 'grad_w', 'grad_b_w_out': 'grad_w', 'grad_c_w_in': 'grad_w', 'grad_c_f_bias': 'grad_w', 'grad_c_w_out': 'grad_w', 'delta_norm_pre': 'delta_w', 'delta_norm_post': 'delta_w', 'delta_a_w_in': 'delta_w', 'delta_a_rel_bias': 'delta_w', 'delta_a_w_out': 'delta_w', 'delta_b_w_in': 'delta_w', 'delta_b_conv_w': 'delta_w', 'delta_b_conv_b': 'delta_w', 'delta_b_gate_a_w': 'delta_w', 'delta_b_gate_a_b': 'delta_w', 'delta_b_gate_x_w': 'delta_w', 'delta_b_gate_x_b': 'delta_w', 'delta_b_lambda': 'delta_w', 'delta_b_w_out': 'delta_w', 'delta_c_w_in': 'delta_w', 'delta_c_f_bias': 'delta_w', 'delta_c_w_out': 'delta_w', 'new_m_norm_pre': 'new_m', 'new_m_norm_post': 'new_m', 'new_m_a_w_in': 'new_m', 'new_m_a_rel_bias': 'new_m', 'new_m_a_w_out': 'new_m', 'new_m_b_w_in': 'new_m', 'new_m_b_conv_w': 'new_m', 'new_m_b_conv_b': 'new_m', 'new_m_b_gate_a_w': 'new_m', 'new_m_b_gate_a_b': 'new_m', 'new_m_b_gate_x_w': 'new_m', 'new_m_b_gate_x_b': 'new_m', 'new_m_b_lambda': 'new_m', 'new_m_b_w_out': 'new_m', 'new_m_c_w_in': 'new_m', 'new_m_c_f_bias': 'new_m', 'new_m_c_w_out': 'new_m', 'new_v_norm_pre': 'new_v', 'new_v_norm_post': 'new_v', 'new_v_a_w_in': 'new_v', 'new_v_a_rel_bias': 'new_v', 'new_v_a_w_out': 'new_v', 'new_v_b_w_in': 'new_v', 'new_v_b_conv_w': 'new_v', 'new_v_b_conv_b': 'new_v', 'new_v_b_gate_a_w': 'new_v', 'new_v_b_gate_a_b': 'new_v', 'new_v_b_gate_x_w': 'new_v', 'new_v_b_gate_x_b': 'new_v', 'new_v_b_lambda': 'new_v', 'new_v_b_w_out': 'new_v', 'new_v_c_w_in': 'new_v', 'new_v_c_f_bias': 'new_v', 'new_v_c_w_out': 'new_v'}


def _forward(args):
    return _fwd_reference(*[args[k] for k in FWD_PARAMS])


def _output_shape():
    out = _jax.eval_shape(lambda: _forward(_fwd_setup_inputs(0)))
    return out.shape, out.dtype

N_MICROBATCH = 1
ADAM_LR = 0.001
ADAM_B1 = 0.9
ADAM_B2 = 0.999
ADAM_EPS = 1e-08
ADAM_WD = 0.01
ADAM_STEP = 10
PER_EXAMPLE_BATCH_AXIS = {'x': 0, 'loss_target': 0}
SHARED_INPUTS = []
_WEIGHT_DTYPES = {'norm_pre': _jnp.float32, 'norm_post': _jnp.float32, 'a_w_in': _jnp.float32, 'a_rel_bias': _jnp.float32, 'a_w_out': _jnp.float32, 'b_w_in': _jnp.float32, 'b_conv_w': _jnp.float32, 'b_conv_b': _jnp.float32, 'b_gate_a_w': _jnp.float32, 'b_gate_a_b': _jnp.float32, 'b_gate_x_w': _jnp.float32, 'b_gate_x_b': _jnp.float32, 'b_lambda': _jnp.float32, 'b_w_out': _jnp.float32, 'c_w_in': _jnp.float32, 'c_f_bias': _jnp.float32, 'c_w_out': _jnp.float32}
MOMENT_SCALE = {'norm_pre': 9.204742e-01, 'norm_post': 1.584163e+01, 'a_w_in': 4.968576e-01, 'a_rel_bias': 1.585552e-01, 'a_w_out': 8.588186e-01, 'b_w_in': 5.840635e-01, 'b_conv_w': 9.134049e-01, 'b_conv_b': 1.009943e+01, 'b_gate_a_w': 1.755467e-01, 'b_gate_a_b': 2.055918e-01, 'b_gate_x_w': 3.350246e-01, 'b_gate_x_b': 4.660959e-01, 'b_lambda': 4.636912e-01, 'b_w_out': 9.510132e-01, 'c_w_in': 3.294506e-01, 'c_f_bias': 1.660279e+00, 'c_w_out': 5.661842e-01}


def _to_microbatches(a, axis):
    t = _jnp.moveaxis(a, axis, 0)
    t = t.reshape((N_MICROBATCH, t.shape[0] // N_MICROBATCH) + t.shape[1:])
    return _jnp.moveaxis(t, 1, axis + 1)


def setup_inputs(seed: int = 0) -> dict:
    inp = _fwd_setup_inputs(seed)
    key = _jax.random.fold_in(_jax.random.key(seed), 7919)
    shape, _ = _output_shape()
    out = dict(inp)
    out["loss_target"] = _jax.random.normal(_jax.random.fold_in(key, 0), shape, _jnp.float32)
    for i, name in enumerate(TWIN_WEIGHTS):
        w = inp[name].astype(_jnp.float32)
        if MOMENT_SCALE is None:
            s = _jnp.sqrt(_jnp.mean(_jnp.square(w)) + 1e-30)
        else:
            s = MOMENT_SCALE[name]
        km, kv = _jax.random.split(_jax.random.fold_in(key, i + 1))
        out[name] = w
        out["m_" + name] = s * _jax.random.normal(km, w.shape, _jnp.float32)
        out["v_" + name] = (s * s) * _jax.random.uniform(kv, w.shape, _jnp.float32, 0.5, 1.5)
    if N_MICROBATCH > 1:
        for name, axis in PER_EXAMPLE_BATCH_AXIS.items():
            out[name] = _to_microbatches(out[name], axis)
    return {'x': out['x'], 'norm_pre': out['norm_pre'], 'norm_post': out['norm_post'], 'a_w_in': out['a_w_in'], 'a_rel_bias': out['a_rel_bias'], 'a_w_out': out['a_w_out'], 'b_w_in': out['b_w_in'], 'b_conv_w': out['b_conv_w'], 'b_conv_b': out['b_conv_b'], 'b_gate_a_w': out['b_gate_a_w'], 'b_gate_a_b': out['b_gate_a_b'], 'b_gate_x_w': out['b_gate_x_w'], 'b_gate_x_b': out['b_gate_x_b'], 'b_lambda': out['b_lambda'], 'b_w_out': out['b_w_out'], 'c_w_in': out['c_w_in'], 'c_f_bias': out['c_f_bias'], 'c_w_out': out['c_w_out'], 'loss_target': out['loss_target'], 'm_norm_pre': out['m_norm_pre'], 'm_norm_post': out['m_norm_post'], 'm_a_w_in': out['m_a_w_in'], 'm_a_rel_bias': out['m_a_rel_bias'], 'm_a_w_out': out['m_a_w_out'], 'm_b_w_in': out['m_b_w_in'], 'm_b_conv_w': out['m_b_conv_w'], 'm_b_conv_b': out['m_b_conv_b'], 'm_b_gate_a_w': out['m_b_gate_a_w'], 'm_b_gate_a_b': out['m_b_gate_a_b'], 'm_b_gate_x_w': out['m_b_gate_x_w'], 'm_b_gate_x_b': out['m_b_gate_x_b'], 'm_b_lambda': out['m_b_lambda'], 'm_b_w_out': out['m_b_w_out'], 'm_c_w_in': out['m_c_w_in'], 'm_c_f_bias': out['m_c_f_bias'], 'm_c_w_out': out['m_c_w_out'], 'v_norm_pre': out['v_norm_pre'], 'v_norm_post': out['v_norm_post'], 'v_a_w_in': out['v_a_w_in'], 'v_a_rel_bias': out['v_a_rel_bias'], 'v_a_w_out': out['v_a_w_out'], 'v_b_w_in': out['v_b_w_in'], 'v_b_conv_w': out['v_b_conv_w'], 'v_b_conv_b': out['v_b_conv_b'], 'v_b_gate_a_w': out['v_b_gate_a_w'], 'v_b_gate_a_b': out['v_b_gate_a_b'], 'v_b_gate_x_w': out['v_b_gate_x_w'], 'v_b_gate_x_b': out['v_b_gate_x_b'], 'v_b_lambda': out['v_b_lambda'], 'v_b_w_out': out['v_b_w_out'], 'v_c_w_in': out['v_c_w_in'], 'v_c_f_bias': out['v_c_f_bias'], 'v_c_w_out': out['v_c_w_out']}


def _loss(weights, diff, rest, loss_target):
    with _jax.named_scope("forward"):
        args = {**rest, TWIN_DIFF_INPUT: diff, **{k: w.astype(_WEIGHT_DTYPES[k]) for k, w in weights.items()}}
        y = _forward(args)
    with _jax.named_scope("loss_head"):
        err = _jnp.square(y.astype(_jnp.float32) - loss_target)
        return 0.5 * _jnp.sum(_jnp.mean(err, axis=-1)) if err.ndim else 0.5 * err


def _adamw(w, g, m, v):
    m = ADAM_B1 * m + (1.0 - ADAM_B1) * g
    v = ADAM_B2 * v + (1.0 - ADAM_B2) * _jnp.square(g)
    m_hat = m / (1.0 - ADAM_B1 ** ADAM_STEP)
    v_hat = v / (1.0 - ADAM_B2 ** ADAM_STEP)
    delta = -ADAM_LR * (m_hat / (_jnp.sqrt(v_hat) + ADAM_EPS) + ADAM_WD * w)
    return delta, m, v


def reference(x, norm_pre, norm_post, a_w_in, a_rel_bias, a_w_out, b_w_in, b_conv_w, b_conv_b, b_gate_a_w, b_gate_a_b, b_gate_x_w, b_gate_x_b, b_lambda, b_w_out, c_w_in, c_f_bias, c_w_out, loss_target, m_norm_pre, m_norm_post, m_a_w_in, m_a_rel_bias, m_a_w_out, m_b_w_in, m_b_conv_w, m_b_conv_b, m_b_gate_a_w, m_b_gate_a_b, m_b_gate_x_w, m_b_gate_x_b, m_b_lambda, m_b_w_out, m_c_w_in, m_c_f_bias, m_c_w_out, v_norm_pre, v_norm_post, v_a_w_in, v_a_rel_bias, v_a_w_out, v_b_w_in, v_b_conv_w, v_b_conv_b, v_b_gate_a_w, v_b_gate_a_b, v_b_gate_x_w, v_b_gate_x_b, v_b_lambda, v_b_w_out, v_c_w_in, v_c_f_bias, v_c_w_out):
    given = dict(x=x, norm_pre=norm_pre, norm_post=norm_post, a_w_in=a_w_in, a_rel_bias=a_rel_bias, a_w_out=a_w_out, b_w_in=b_w_in, b_conv_w=b_conv_w, b_conv_b=b_conv_b, b_gate_a_w=b_gate_a_w, b_gate_a_b=b_gate_a_b, b_gate_x_w=b_gate_x_w, b_gate_x_b=b_gate_x_b, b_lambda=b_lambda, b_w_out=b_w_out, c_w_in=c_w_in, c_f_bias=c_f_bias, c_w_out=c_w_out, loss_target=loss_target, m_norm_pre=m_norm_pre, m_norm_post=m_norm_post, m_a_w_in=m_a_w_in, m_a_rel_bias=m_a_rel_bias, m_a_w_out=m_a_w_out, m_b_w_in=m_b_w_in, m_b_conv_w=m_b_conv_w, m_b_conv_b=m_b_conv_b, m_b_gate_a_w=m_b_gate_a_w, m_b_gate_a_b=m_b_gate_a_b, m_b_gate_x_w=m_b_gate_x_w, m_b_gate_x_b=m_b_gate_x_b, m_b_lambda=m_b_lambda, m_b_w_out=m_b_w_out, m_c_w_in=m_c_w_in, m_c_f_bias=m_c_f_bias, m_c_w_out=m_c_w_out, v_norm_pre=v_norm_pre, v_norm_post=v_norm_post, v_a_w_in=v_a_w_in, v_a_rel_bias=v_a_rel_bias, v_a_w_out=v_a_w_out, v_b_w_in=v_b_w_in, v_b_conv_w=v_b_conv_w, v_b_conv_b=v_b_conv_b, v_b_gate_a_w=v_b_gate_a_w, v_b_gate_a_b=v_b_gate_a_b, v_b_gate_x_w=v_b_gate_x_w, v_b_gate_x_b=v_b_gate_x_b, v_b_lambda=v_b_lambda, v_b_w_out=v_b_w_out, v_c_w_in=v_c_w_in, v_c_f_bias=v_c_f_bias, v_c_w_out=v_c_w_out)
    weights = {n: given[n] for n in TWIN_WEIGHTS}
    shared = {n: given[n] for n in SHARED_INPUTS}
    per_example = {n: given[n] for n in ['x']}
    grad_fn = _jax.value_and_grad(_loss, argnums=(0, 1))

    def one_microbatch(ex, loss_target):
        ex = dict(ex)
        diff = ex.pop(TWIN_DIFF_INPUT)
        return grad_fn(weights, diff, {**shared, **ex}, loss_target)

    if N_MICROBATCH == 1:
        loss, (grad_w, grad_x) = one_microbatch(per_example, given["loss_target"])
    else:
        def body(carry, xs):
            loss_sum, grad_sum = carry
            l_k, (gw_k, gx_k) = one_microbatch(xs[0], xs[1])
            with _jax.named_scope("update"):
                return (loss_sum + l_k, _jax.tree.map(_jnp.add, grad_sum, gw_k)), gx_k

        init = (_jnp.zeros((), _jnp.float32), _jax.tree.map(_jnp.zeros_like, weights))
        (loss, grad_w), grad_x = _jax.lax.scan(body, init, (per_example, given["loss_target"]))
    with _jax.named_scope("update"):
        delta_w, new_m, new_v = {}, {}, {}
        for n in TWIN_WEIGHTS:
            delta_w[n], new_m[n], new_v[n] = _adamw(weights[n], grad_w[n], given["m_" + n], given["v_" + n])
    return (loss, grad_x, *[grad_w[n] for n in TWIN_WEIGHTS], *[delta_w[n] for n in TWIN_WEIGHTS],
            *[new_m[n] for n in TWIN_WEIGHTS], *[new_v[n] for n in TWIN_WEIGHTS])
```

```python
import functools

import jax
import jax.numpy as jnp
from jax import lax
from jax.experimental import pallas as pl
from jax.experimental.pallas import tpu as pltpu

F32 = jnp.float32
BF16 = jnp.bfloat16

N_DEV = 8
D_MODEL = 2048
HEADS = 16
HEAD_DIM = 128
HEAD_COLS = 4 * HEAD_DIM
CHUNK = 64
LEFT_CHUNKS = 8
REL_CLIP = 256
N_REL = 2 * REL_CLIP + 1
TQ = 256
A_PAD = LEFT_CHUNKS * CHUNK
A_KW = A_PAD + TQ
RG_WIDTH = 2560
RG_BLOCKS = 16
RG_BLOCK = 160
RG_COLS = 640
RG_GROUPS = RG_WIDTH // RG_COLS
RG_C = 8.0
CONV_WIDTH = 4
RMS_EPS = 1e-6
NEG_INF = -1e30
ADAM_LR = 0.001
ADAM_B1 = 0.9
ADAM_B2 = 0.999
ADAM_EPS = 1e-08
ADAM_WD = 0.01
ADAM_STEP = 10
VMEM_LIMIT = 56 * 1024 * 1024
MESH = pl.DeviceIdType.MESH


def _params(sem, vmem=VMEM_LIMIT):
    return pltpu.CompilerParams(dimension_semantics=sem, vmem_limit_bytes=vmem)


def _sigmoid(x):
    return 1.0 / (1.0 + jnp.exp(-x))


def _log1p(y):
    u = 1.0 + y
    return jnp.where(u == 1.0, y, jnp.log(u) * (y / jnp.where(u == 1.0, 1.0, u - 1.0)))


def _softplus(x):
    return jnp.maximum(x, 0.0) + _log1p(jnp.exp(-jnp.abs(x)))


def _dot(a, b, dims):
    return lax.dot_general(a, b, (dims, ((), ())), preferred_element_type=F32)


def _dot_nn(a, b):
    return _dot(a, b, ((1,), (0,)))


def _dot_nt(a, b):
    return _dot(a, b, ((1,), (1,)))


def _dot_tn(a, b):
    return _dot(a, b, ((0,), (0,)))


LANES = 128


def _fit(dim, want):
    if dim <= want:
        return dim
    best = max(t for t in range(LANES, want + 1, LANES) if dim % t == 0)
    return best


def matmul(a, b, *, mode, out_dtype, name, bm=512, bn=1024, bk=2048):
    if mode == "tn":
        K, M = a.shape
    else:
        M, K = a.shape
    N = b.shape[0] if mode == "nt" else b.shape[1]
    bm, bn, bk = _fit(M, bm), _fit(N, bn), _fit(K, bk)
    nk = K // bk
    dims = {"nn": ((1,), (0,)), "nt": ((1,), (1,)), "tn": ((0,), (0,))}[mode]

    def body(a_ref, b_ref, o_ref, *scratch):
        if nk == 1:
            o_ref[...] = _dot(a_ref[...], b_ref[...], dims).astype(o_ref.dtype)
            return
        acc_ref, = scratch
        k = pl.program_id(2)

        @pl.when(k == 0)
        def _():
            acc_ref[...] = jnp.zeros_like(acc_ref)

        acc_ref[...] += _dot(a_ref[...], b_ref[...], dims)

        @pl.when(k == nk - 1)
        def _():
            o_ref[...] = acc_ref[...].astype(o_ref.dtype)

    if mode == "tn":
        a_spec = pl.BlockSpec((bk, bm), lambda j, i, k: (k, i))
    else:
        a_spec = pl.BlockSpec((bm, bk), lambda j, i, k: (i, k))
    if mode == "nt":
        b_spec = pl.BlockSpec((bn, bk), lambda j, i, k: (j, k))
    else:
        b_spec = pl.BlockSpec((bk, bn), lambda j, i, k: (k, j))
    return pl.pallas_call(
        body, name=name,
        grid=(N // bn, M // bm, nk),
        in_specs=[a_spec, b_spec],
        out_specs=pl.BlockSpec((bm, bn), lambda j, i, k: (i, j)),
        out_shape=jax.ShapeDtypeStruct((M, N), out_dtype),
        scratch_shapes=[] if nk == 1 else [pltpu.VMEM((bm, bn), F32)],
        compiler_params=_params(("parallel", "parallel", "arbitrary")),
    )(a, b)


ROW_TILE = 256


def _rms_stats(z):
    r = lax.rsqrt(jnp.mean(z * z, axis=-1, keepdims=True) + RMS_EPS)
    return r, z * r


def _rms_bwd(n, r, g, dout):
    dn = dout * g
    return r * (dn - n * jnp.mean(dn * n, axis=-1, keepdims=True))


def _row_spec(T, Dm):
    bt = min(ROW_TILE, T)
    return bt, pl.BlockSpec((bt, Dm), lambda i: (i, 0)), pl.BlockSpec((1, Dm), lambda i: (0, 0))


def prenorm_fwd(x, g, name):
    T, Dm = x.shape
    bt, row, vec = _row_spec(T, Dm)

    def body(x_ref, g_ref, h_ref):
        _, n = _rms_stats(x_ref[...])
        h_ref[...] = (n * g_ref[...]).astype(BF16)

    return pl.pallas_call(
        body, name=name, grid=(T // bt,), in_specs=[row, vec], out_specs=row,
        out_shape=jax.ShapeDtypeStruct((T, Dm), BF16), compiler_params=_params(("parallel",)),
    )(x, g)


def postnorm_fwd(x, y, g, name):
    T, Dm = x.shape
    bt, row, vec = _row_spec(T, Dm)

    def body(x_ref, y_ref, g_ref, o_ref):
        _, n = _rms_stats(y_ref[...])
        o_ref[...] = x_ref[...] + n * g_ref[...]

    return pl.pallas_call(
        body, name=name, grid=(T // bt,), in_specs=[row, row, vec], out_specs=row,
        out_shape=jax.ShapeDtypeStruct((T, Dm), F32), compiler_params=_params(("parallel",)),
    )(x, y, g)


def loss_fwd_bwd(xf, target, name):
    T, Dm = xf.shape
    bt, row, _ = _row_spec(T, Dm)

    def body(x_ref, t_ref, l_ref, d_ref):
        @pl.when(pl.program_id(0) == 0)
        def _():
            l_ref[...] = jnp.zeros_like(l_ref)

        err = x_ref[...] - t_ref[...]
        per_tok = jnp.mean(err * err, axis=-1, keepdims=True)
        l_ref[...] += 0.5 * jnp.sum(per_tok, axis=0, keepdims=True)
        d_ref[...] = err * (1.0 / Dm)

    return pl.pallas_call(
        body, name=name, grid=(T // bt,), in_specs=[row, row],
        out_specs=[pl.BlockSpec((1, 1), lambda i: (0, 0)), row],
        out_shape=[jax.ShapeDtypeStruct((1, 1), F32), jax.ShapeDtypeStruct((T, Dm), F32)],
        compiler_params=_params(("arbitrary",)),
    )(xf, target)


def postnorm_bwd(y, g, dout, name):
    T, Dm = y.shape
    bt, row, vec = _row_spec(T, Dm)

    def body(y_ref, g_ref, d_ref, dy_ref, dg_ref):
        @pl.when(pl.program_id(0) == 0)
        def _():
            dg_ref[...] = jnp.zeros_like(dg_ref)

        r, n = _rms_stats(y_ref[...])
        dout_v = d_ref[...]
        dg_ref[...] += jnp.sum(dout_v * n, axis=0, keepdims=True)
        dy_ref[...] = _rms_bwd(n, r, g_ref[...], dout_v).astype(BF16)

    return pl.pallas_call(
        body, name=name, grid=(T // bt,), in_specs=[row, vec, row], out_specs=[row, vec],
        out_shape=[jax.ShapeDtypeStruct((T, Dm), BF16), jax.ShapeDtypeStruct((1, Dm), F32)],
        compiler_params=_params(("arbitrary",)),
    )(y, g, dout)


def prenorm_bwd(x, g, dhs, dres, name):
    T, Dm = x.shape
    bt, row, vec = _row_spec(T, Dm)
    n_dh = len(dhs)

    def body(x_ref, g_ref, *refs):
        dh_refs, (dr_ref, dx_ref, dg_ref) = refs[:n_dh], refs[n_dh:]

        @pl.when(pl.program_id(0) == 0)
        def _():
            dg_ref[...] = jnp.zeros_like(dg_ref)

        r, n = _rms_stats(x_ref[...])
        dh_v = dh_refs[0][...]
        for extra in dh_refs[1:]:
            dh_v = dh_v + extra[...]
        dg_ref[...] += jnp.sum(dh_v * n, axis=0, keepdims=True)
        dx_ref[...] = dr_ref[...] + _rms_bwd(n, r, g_ref[...], dh_v)

    return pl.pallas_call(
        body, name=name, grid=(T // bt,), in_specs=[row, vec] + [row] * (n_dh + 1), out_specs=[row, vec],
        out_shape=[jax.ShapeDtypeStruct((T, Dm), F32), jax.ShapeDtypeStruct((1, Dm), F32)],
        compiler_params=_params(("arbitrary",)),
    )(x, g, *dhs, dres)


def _silu(g):
    return g * _sigmoid(g)


def _gate_bwd(dgated, core, g):
    sg = _sigmoid(g)
    return dgated * (g * sg), dgated * core * (sg * (1.0 + g * (1.0 - sg)))


def _softmax_rows(s):
    e = jnp.exp(s - jnp.max(s, axis=-1, keepdims=True))
    return e / jnp.sum(e, axis=-1, keepdims=True)


def _band_scores(q, kw, bias, r0):
    s = _dot_nt(q, kw) * (HEAD_DIM ** -0.5) + bias
    i = lax.broadcasted_iota(jnp.int32, s.shape, 0)
    j = lax.broadcasted_iota(jnp.int32, s.shape, 1)
    lo = (i >> 6) << 6
    vis = (j >= lo) & (j < lo + (LEFT_CHUNKS + 1) * CHUNK) & (j >= A_PAD - r0)
    return jnp.where(vis, s, NEG_INF)


def _fill_padded_kv(p_ref, kp_ref, vp_ref):
    zeros = jnp.zeros((A_PAD, HEAD_DIM), BF16)
    kp_ref[0:A_PAD, :] = zeros
    vp_ref[0:A_PAD, :] = zeros
    kp_ref[A_PAD:, :] = p_ref[0, :, HEAD_DIM:2 * HEAD_DIM].astype(BF16)
    vp_ref[A_PAD:, :] = p_ref[0, :, 2 * HEAD_DIM:3 * HEAD_DIM].astype(BF16)


def attn_a_fwd(proj, bias, name):
    B, S, _ = proj.shape
    nt = S // TQ

    def body(p_ref, b_ref, o_ref, gt_ref, kp_ref, vp_ref):
        t = pl.program_id(2)

        @pl.when(t == 0)
        def _():
            _fill_padded_kv(p_ref, kp_ref, vp_ref)

        r0 = pl.multiple_of(t * TQ, TQ)
        q = p_ref[0, pl.ds(r0, TQ), 0:HEAD_DIM].astype(BF16)
        g = p_ref[0, pl.ds(r0, TQ), 3 * HEAD_DIM:4 * HEAD_DIM]
        p = _softmax_rows(_band_scores(q, kp_ref[pl.ds(r0, A_KW), :], b_ref[0], r0))
        o = _dot_nn(p.astype(BF16), vp_ref[pl.ds(r0, A_KW), :])
        o_ref[0] = o
        gt_ref[0] = (o * _silu(g)).astype(BF16)

    tile = pl.BlockSpec((1, TQ, HEAD_DIM), lambda b, h, t: (b, t, h))
    return pl.pallas_call(
        body, name=name, grid=(B, HEADS, nt),
        in_specs=[pl.BlockSpec((1, S, HEAD_COLS), lambda b, h, t: (b, 0, h)),
                  pl.BlockSpec((1, TQ, A_KW), lambda b, h, t: (h, 0, 0))],
        out_specs=[tile, tile],
        out_shape=[jax.ShapeDtypeStruct((B, S, HEADS * HEAD_DIM), F32),
                   jax.ShapeDtypeStruct((B, S, HEADS * HEAD_DIM), BF16)],
        scratch_shapes=[pltpu.VMEM((A_PAD + S, HEAD_DIM), BF16), pltpu.VMEM((A_PAD + S, HEAD_DIM), BF16)],
        compiler_params=_params(("parallel", "parallel", "arbitrary")),
    )(proj, bias)


def attn_a_bwd(proj, bias, o, dgated, name):
    B, S, _ = proj.shape
    nt = S // TQ

    def body(p_ref, b_ref, o_ref, dgt_ref, dp_ref, db_ref, kp_ref, vp_ref, dk_ref, dv_ref):
        b_, t = pl.program_id(1), pl.program_id(2)

        @pl.when(t == 0)
        def _():
            _fill_padded_kv(p_ref, kp_ref, vp_ref)
            dk_ref[...] = jnp.zeros_like(dk_ref)
            dv_ref[...] = jnp.zeros_like(dv_ref)

        @pl.when((t == 0) & (b_ == 0))
        def _():
            db_ref[...] = jnp.zeros_like(db_ref)

        r0 = pl.multiple_of(t * TQ, TQ)
        rows, win = pl.ds(r0, TQ), pl.ds(r0, A_KW)
        q = p_ref[0, rows, 0:HEAD_DIM].astype(BF16)
        g = p_ref[0, rows, 3 * HEAD_DIM:4 * HEAD_DIM]
        kw, vw = kp_ref[win, :], vp_ref[win, :]
        p = _softmax_rows(_band_scores(q, kw, b_ref[0], r0))
        do, dg = _gate_bwd(dgt_ref[0], o_ref[0], g)
        do = do.astype(BF16)
        dv_ref[win, :] += _dot_tn(p.astype(BF16), do)
        dpr = _dot_nt(do, vw)
        ds = p * (dpr - jnp.sum(p * dpr, axis=-1, keepdims=True))
        db_ref[0] += ds
        ds = (ds * (HEAD_DIM ** -0.5)).astype(BF16)
        dk_ref[win, :] += _dot_tn(ds, q)
        dp_ref[0, rows, 0:HEAD_DIM] = _dot_nn(ds, kw).astype(BF16)
        dp_ref[0, rows, 3 * HEAD_DIM:4 * HEAD_DIM] = dg.astype(BF16)

        @pl.when(t == nt - 1)
        def _():
            dp_ref[0, :, HEAD_DIM:2 * HEAD_DIM] = dk_ref[A_PAD:, :].astype(BF16)
            dp_ref[0, :, 2 * HEAD_DIM:3 * HEAD_DIM] = dv_ref[A_PAD:, :].astype(BF16)

    tile = pl.BlockSpec((1, TQ, HEAD_DIM), lambda h, b, t: (b, t, h))
    seq = pl.BlockSpec((1, S, HEAD_COLS), lambda h, b, t: (b, 0, h))
    bias_spec = pl.BlockSpec((1, TQ, A_KW), lambda h, b, t: (h, 0, 0))
    return pl.pallas_call(
        body, name=name, grid=(HEADS, B, nt),
        in_specs=[seq, bias_spec, tile, tile],
        out_specs=[seq, bias_spec],
        out_shape=[jax.ShapeDtypeStruct(proj.shape, BF16), jax.ShapeDtypeStruct(bias.shape, F32)],
        scratch_shapes=[pltpu.VMEM((A_PAD + S, HEAD_DIM), BF16), pltpu.VMEM((A_PAD + S, HEAD_DIM), BF16),
                        pltpu.VMEM((A_PAD + S, HEAD_DIM), F32), pltpu.VMEM((A_PAD + S, HEAD_DIM), F32)],
        compiler_params=_params(("arbitrary", "arbitrary", "arbitrary")),
    )(proj, bias, o, dgated)


def band_bias(rel_bias):
    gen = jnp.concatenate([rel_bias[:, 1:2 * REL_CLIP],
                           jnp.broadcast_to(rel_bias[:, 2 * REL_CLIP:], (HEADS, A_KW - REL_CLIP))], axis=1)
    length = TQ + A_KW - 1
    assert gen.shape[1] == length
    rev = jnp.concatenate([gen[:, ::-1], jnp.zeros((HEADS, 1), rel_bias.dtype)], axis=1)
    sheared = jnp.tile(rev, (1, TQ))[:, :TQ * length].reshape(HEADS, TQ, length)
    return sheared[:, :, TQ - 1:]


def _group_scan(a, u, carry, reverse=False):
    row = lax.broadcasted_iota(jnp.int32, u.shape, 0)
    for k in (1, 2, 4):
        shift = 8 - k if reverse else k
        valid = (row < 8 - k) if reverse else (row >= k)
        u_sh = pltpu.roll(u, shift, 0)
        if a is None:
            u = jnp.where(valid, u + u_sh, u)
        else:
            a_sh = pltpu.roll(a, shift, 0)
            u = jnp.where(valid, a * u_sh + u, u)
            a = jnp.where(valid, a * a_sh, a)
    return (u + carry) if a is None else (a * carry + u)


def _scan_rows(n_rows, step, carry0, reverse=False):
    groups = n_rows // 8

    def loop(i, carry):
        gi = (groups - 1 - i) if reverse else i
        return step(pl.multiple_of(gi * 8, 8), carry)

    return lax.fori_loop(0, groups, loop, carry0)


def fox_cum_fwd(f_logit, f_bias, name):
    B, S, L = f_logit.shape

    def body(f_ref, b_ref, c_ref):
        z = f_ref[0] + b_ref[...]
        c_ref[0] = jnp.minimum(z, 0.0) - _log1p(jnp.exp(-jnp.abs(z)))

        def step(r0, carry):
            h = _group_scan(None, c_ref[0, pl.ds(r0, 8), :], carry)
            c_ref[0, pl.ds(r0, 8), :] = h
            return h[7:8, :]

        _scan_rows(S, step, jnp.zeros((1, L), F32))

    return pl.pallas_call(
        body, name=name, grid=(B,),
        in_specs=[pl.BlockSpec((1, S, L), lambda b: (b, 0, 0)), pl.BlockSpec((1, L), lambda b: (0, 0))],
        out_specs=pl.BlockSpec((1, S, L), lambda b: (b, 0, 0)),
        out_shape=jax.ShapeDtypeStruct((B, S, L), F32), compiler_params=_params(("parallel",)),
    )(f_logit, f_bias)


def fox_cum_bwd(f_logit, f_bias, dcum, name):
    B, S, L = f_logit.shape

    def body(f_ref, b_ref, d_ref, df_ref, db_ref):
        @pl.when(pl.program_id(0) == 0)
        def _():
            db_ref[...] = jnp.zeros_like(db_ref)

        def step(r0, carry):
            h = _group_scan(None, d_ref[0, pl.ds(r0, 8), :], carry, reverse=True)
            df_ref[0, pl.ds(r0, 8), :] = h
            return h[0:1, :]

        _scan_rows(S, step, jnp.zeros((1, L), F32), reverse=True)
        df = df_ref[0] * _sigmoid(-(f_ref[0] + b_ref[...]))
        df_ref[0] = df
        db_ref[...] += jnp.sum(df, axis=0, keepdims=True)

    seq = pl.BlockSpec((1, S, L), lambda b: (b, 0, 0))
    vec = pl.BlockSpec((1, L), lambda b: (0, 0))
    return pl.pallas_call(
        body, name=name, grid=(B,), in_specs=[seq, vec, seq], out_specs=[seq, vec],
        out_shape=[jax.ShapeDtypeStruct((B, S, L), F32), jax.ShapeDtypeStruct((1, L), F32)],
        compiler_params=_params(("arbitrary",)),
    )(f_logit, f_bias, dcum)


def _fox_scores(q, k, cc, cr, h, r0):
    lane = lax.broadcasted_iota(jnp.int32, cc.shape, 1)
    cq = jnp.sum(jnp.where(lane == h, cc, 0.0), axis=1, keepdims=True)
    sub = lax.broadcasted_iota(jnp.int32, cr.shape, 0)
    ck = jnp.sum(jnp.where(sub == h, cr, 0.0), axis=0, keepdims=True)
    s = _dot_nt(q, k) * (HEAD_DIM ** -0.5) + (cq - ck)
    qpos = r0 + lax.broadcasted_iota(jnp.int32, s.shape, 0)
    kpos = lax.broadcasted_iota(jnp.int32, s.shape, 1)
    return jnp.where(kpos <= qpos, s, NEG_INF)


def fox_fwd(proj, cum_col, cum_row, name):
    B, S, _ = proj.shape
    nt = S // TQ

    def body(p_ref, cc_ref, cr_ref, o_ref, gt_ref, k_ref, v_ref):
        h, t = pl.program_id(1), pl.program_id(2)

        @pl.when(t == 0)
        def _():
            k_ref[...] = p_ref[0, :, HEAD_DIM:2 * HEAD_DIM].astype(BF16)
            v_ref[...] = p_ref[0, :, 2 * HEAD_DIM:3 * HEAD_DIM].astype(BF16)

        r0 = pl.multiple_of(t * TQ, TQ)
        q = p_ref[0, pl.ds(r0, TQ), 0:HEAD_DIM].astype(BF16)
        g = p_ref[0, pl.ds(r0, TQ), 3 * HEAD_DIM:4 * HEAD_DIM]
        p = _softmax_rows(_fox_scores(q, k_ref[...], cc_ref[0], cr_ref[0], h, r0))
        o = _dot_nn(p.astype(BF16), v_ref[...])
        o_ref[0] = o
        gt_ref[0] = (o * _silu(g)).astype(BF16)

    tile = pl.BlockSpec((1, TQ, HEAD_DIM), lambda b, h, t: (b, t, h))
    return pl.pallas_call(
        body, name=name, grid=(B, HEADS, nt),
        in_specs=[pl.BlockSpec((1, S, HEAD_COLS), lambda b, h, t: (b, 0, h)),
                  pl.BlockSpec((1, TQ, cum_col.shape[2]), lambda b, h, t: (b, t, 0)),
                  pl.BlockSpec((1, HEADS, S), lambda b, h, t: (b, 0, 0))],
        out_specs=[tile, tile],
        out_shape=[jax.ShapeDtypeStruct((B, S, HEADS * HEAD_DIM), F32),
                   jax.ShapeDtypeStruct((B, S, HEADS * HEAD_DIM), BF16)],
        scratch_shapes=[pltpu.VMEM((S, HEAD_DIM), BF16), pltpu.VMEM((S, HEAD_DIM), BF16)],
        compiler_params=_params(("parallel", "parallel", "arbitrary")),
    )(proj, cum_col, cum_row)


def fox_bwd(proj, cum_col, cum_row, o, dgated, name):
    B, S, _ = proj.shape
    nt = S // TQ

    def body(p_ref, cc_ref, cr_ref, o_ref, dgt_ref, dp_ref, dc_ref, k_ref, v_ref, dk_ref, dv_ref):
        h, t = pl.program_id(1), pl.program_id(2)

        @pl.when(t == 0)
        def _():
            k_ref[...] = p_ref[0, :, HEAD_DIM:2 * HEAD_DIM].astype(BF16)
            v_ref[...] = p_ref[0, :, 2 * HEAD_DIM:3 * HEAD_DIM].astype(BF16)
            dk_ref[...] = jnp.zeros_like(dk_ref)
            dv_ref[...] = jnp.zeros_like(dv_ref)
            dc_ref[...] = jnp.zeros_like(dc_ref)

        r0 = pl.multiple_of(t * TQ, TQ)
        rows = pl.ds(r0, TQ)
        q = p_ref[0, rows, 0:HEAD_DIM].astype(BF16)
        g = p_ref[0, rows, 3 * HEAD_DIM:4 * HEAD_DIM]
        k, v = k_ref[...], v_ref[...]
        p = _softmax_rows(_fox_scores(q, k, cc_ref[0], cr_ref[0], h, r0))
        do, dg = _gate_bwd(dgt_ref[0], o_ref[0], g)
        do = do.astype(BF16)
        dv_ref[...] += _dot_tn(p.astype(BF16), do)
        dpr = _dot_nt(do, v)
        ds = p * (dpr - jnp.sum(p * dpr, axis=-1, keepdims=True))
        dc_ref[0, 0] += jnp.sum(ds, axis=0, keepdims=True)
        ds = (ds * (HEAD_DIM ** -0.5)).astype(BF16)
        dk_ref[...] += _dot_tn(ds, q)
        dp_ref[0, rows, 0:HEAD_DIM] = _dot_nn(ds, k).astype(BF16)
        dp_ref[0, rows, 3 * HEAD_DIM:4 * HEAD_DIM] = dg.astype(BF16)

        @pl.when(t == nt - 1)
        def _():
            dp_ref[0, :, HEAD_DIM:2 * HEAD_DIM] = dk_ref[...].astype(BF16)
            dp_ref[0, :, 2 * HEAD_DIM:3 * HEAD_DIM] = dv_ref[...].astype(BF16)

    tile = pl.BlockSpec((1, TQ, HEAD_DIM), lambda b, h, t: (b, t, h))
    seq = pl.BlockSpec((1, S, HEAD_COLS), lambda b, h, t: (b, 0, h))
    return pl.pallas_call(
        body, name=name, grid=(B, HEADS, nt),
        in_specs=[seq,
                  pl.BlockSpec((1, TQ, cum_col.shape[2]), lambda b, h, t: (b, t, 0)),
                  pl.BlockSpec((1, HEADS, S), lambda b, h, t: (b, 0, 0)),
                  tile, tile],
        out_specs=[seq, pl.BlockSpec((1, 1, 1, S), lambda b, h, t: (b, h, 0, 0))],
        out_shape=[jax.ShapeDtypeStruct(proj.shape, BF16), jax.ShapeDtypeStruct((B, HEADS, 1, S), F32)],
        scratch_shapes=[pltpu.VMEM((S, HEAD_DIM), BF16), pltpu.VMEM((S, HEAD_DIM), BF16),
                        pltpu.VMEM((S, HEAD_DIM), F32), pltpu.VMEM((S, HEAD_DIM), F32)],
        compiler_params=_params(("parallel", "parallel", "arbitrary")),
    )(proj, cum_col, cum_row, o, dgated)


RG_ROWS = 512


def _rg_gates(xc, wa_ref, ba_ref, wx_ref, bx_ref, lam_ref):
    xcb = xc.astype(BF16)
    r = _sigmoid(_dot_nn(xcb, wa_ref[0]) + ba_ref[...])
    i = _sigmoid(_dot_nn(xcb, wx_ref[0]) + bx_ref[...])
    sp = _softplus(-lam_ref[...])
    log_a = (-RG_C * sp) * r
    a = jnp.exp(log_a)
    m = jnp.sqrt(-jnp.tanh(log_a) * (a * a + 1.0))
    return xcb, r, i, sp, a, m


def _rg_specs(B, S, rows, order):
    nc = S // rows

    def idx(fn):
        def index_map(*ids):
            v = dict(zip(order.lower(), ids))
            c = (nc - 1 - v["c"]) if "C" in order else v["c"]
            return fn(v["b"], v["d"], c)
        return index_map

    return dict(
        proj=pl.BlockSpec((1, rows, 2 * RG_COLS), idx(lambda b, d, c: (b, c, d))),
        act=pl.BlockSpec((1, rows, RG_COLS), idx(lambda b, d, c: (b, c, d))),
        taps=pl.BlockSpec((CONV_WIDTH, RG_COLS), idx(lambda b, d, c: (0, d))),
        vec=pl.BlockSpec((1, RG_COLS), idx(lambda b, d, c: (0, d))),
        gate=pl.BlockSpec((1, RG_COLS, RG_COLS), idx(lambda b, d, c: (d, 0, 0))),
    )


def rglru_fwd(proj, conv_w, conv_b, wa, ba, wx, bx, lam, name, rows=RG_ROWS):
    B, S, _ = proj.shape
    rows = min(rows, S)
    sp_ = _rg_specs(B, S, rows, "bdc")

    def body(p_ref, cw_ref, cb_ref, wa_ref, ba_ref, wx_ref, bx_ref, lam_ref,
             xc_ref, hs_ref, hp_ref, gt_ref, ext_ref, a_ref, u_ref, xcar_ref, hcar_ref):
        @pl.when(pl.program_id(2) == 0)
        def _():
            xcar_ref[...] = jnp.zeros_like(xcar_ref)
            hcar_ref[...] = jnp.zeros_like(hcar_ref)

        xr = p_ref[0, :, 0:RG_COLS]
        ext_ref[0:8, :] = xcar_ref[...]
        ext_ref[8:, :] = xr
        xcar_ref[...] = xr[rows - 8:, :]
        xc = ext_ref[pl.ds(5, rows), :] * cw_ref[0:1, :]
        xc = xc + ext_ref[pl.ds(6, rows), :] * cw_ref[1:2, :]
        xc = xc + ext_ref[pl.ds(7, rows), :] * cw_ref[2:3, :]
        xc = xc + xr * cw_ref[3:4, :] + cb_ref[...]
        xc_ref[0] = xc
        _, _, i, _, a, m = _rg_gates(xc, wa_ref, ba_ref, wx_ref, bx_ref, lam_ref)
        a_ref[...] = a
        u_ref[...] = m * (i * xc)

        def step(r0, carry):
            h = _group_scan(a_ref[pl.ds(r0, 8), :], u_ref[pl.ds(r0, 8), :], carry)
            row = lax.broadcasted_iota(jnp.int32, h.shape, 0)
            hs_ref[0, pl.ds(r0, 8), :] = h
            hp_ref[0, pl.ds(r0, 8), :] = jnp.where(row == 0, carry, pltpu.roll(h, 1, 0))
            return h[7:8, :]

        hcar_ref[0:1, :] = _scan_rows(rows, step, hcar_ref[0:1, :])
        gt_ref[0] = (hs_ref[0] * _silu(p_ref[0, :, RG_COLS:])).astype(BF16)

    act = jax.ShapeDtypeStruct((B, S, RG_WIDTH), F32)
    return pl.pallas_call(
        body, name=name, grid=(B, RG_GROUPS, S // rows),
        in_specs=[sp_["proj"], sp_["taps"], sp_["vec"], sp_["gate"], sp_["vec"], sp_["gate"], sp_["vec"], sp_["vec"]],
        out_specs=[sp_["act"]] * 4,
        out_shape=[act, act, act, jax.ShapeDtypeStruct((B, S, RG_WIDTH), BF16)],
        scratch_shapes=[pltpu.VMEM((rows + 8, RG_COLS), F32), pltpu.VMEM((rows, RG_COLS), F32),
                        pltpu.VMEM((rows, RG_COLS), F32), pltpu.VMEM((8, RG_COLS), F32), pltpu.VMEM((8, RG_COLS), F32)],
        compiler_params=_params(("parallel", "parallel", "arbitrary")),
    )(proj, conv_w, conv_b, wa, ba, wx, bx, lam)


def rglru_bwd(proj, xc, hs, hprev, dgated, conv_w, wa, ba, wx, bx, lam, name, rows=RG_ROWS):
    B, S, _ = proj.shape
    rows = min(rows, S)
    sp_ = _rg_specs(B, S, rows, "dbC")

    def body(p_ref, xc_ref, hs_ref, hp_ref, dgt_ref, cw_ref, wa_ref, ba_ref, wx_ref, bx_ref, lam_ref,
             dp_ref, dcw_ref, dcb_ref, dwa_ref, dba_ref, dwx_ref, dbx_ref, dlam_ref,
             ext_ref, c_ref, l_ref, acar_ref, lcar_ref, dcar_ref):
        b_, c_ = pl.program_id(1), pl.program_id(2)

        @pl.when(c_ == 0)
        def _():
            acar_ref[...] = jnp.zeros_like(acar_ref)
            lcar_ref[...] = jnp.zeros_like(lcar_ref)
            dcar_ref[...] = jnp.zeros_like(dcar_ref)

        @pl.when((c_ == 0) & (b_ == 0))
        def _():
            for ref in (dcw_ref, dcb_ref, dwa_ref, dba_ref, dwx_ref, dbx_ref, dlam_ref):
                ref[...] = jnp.zeros_like(ref)

        xr, g = p_ref[0, :, 0:RG_COLS], p_ref[0, :, RG_COLS:]
        xc_v = xc_ref[0]
        xcb, r, i, sp, a, m = _rg_gates(xc_v, wa_ref, ba_ref, wx_ref, bx_ref, lam_ref)
        dhs, dg = _gate_bwd(dgt_ref[0], hs_ref[0], g)
        dp_ref[0, :, RG_COLS:] = dg.astype(BF16)

        ext_ref[0:rows, :] = a
        ext_ref[rows:, :] = acar_ref[...]
        acar_ref[...] = a[0:8, :]
        c_ref[...] = ext_ref[pl.ds(1, rows), :]
        l_ref[...] = dhs

        def step(r0, carry):
            lam_g = _group_scan(c_ref[pl.ds(r0, 8), :], l_ref[pl.ds(r0, 8), :], carry, reverse=True)
            l_ref[pl.ds(r0, 8), :] = lam_g
            return lam_g[0:1, :]

        lcar_ref[0:1, :] = _scan_rows(rows, step, lcar_ref[0:1, :], reverse=True)
        du = l_ref[...]
        da = du * hp_ref[0]
        dlog_a = da * a - (du * (i * xc_v)) * (a * a / m)
        dr = dlog_a * (-RG_C * sp)
        dsp = jnp.sum(dlog_a * (-RG_C * r), axis=0, keepdims=True)
        dlam_ref[...] += dsp * (-_sigmoid(-lam_ref[...]))
        dpa = dr * (r * (1.0 - r))
        dpx = (du * (m * xc_v)) * (i * (1.0 - i))
        dba_ref[...] += jnp.sum(dpa, axis=0, keepdims=True)
        dbx_ref[...] += jnp.sum(dpx, axis=0, keepdims=True)
        dpa, dpx = dpa.astype(BF16), dpx.astype(BF16)
        dwa_ref[0] += _dot_tn(xcb, dpa)
        dwx_ref[0] += _dot_tn(xcb, dpx)
        dxc = du * (m * i) + _dot_nt(dpa, wa_ref[0]) + _dot_nt(dpx, wx_ref[0])

        dcb_ref[...] += jnp.sum(dxc, axis=0, keepdims=True)
        ext_ref[0:rows, :] = dxc
        ext_ref[rows:, :] = dcar_ref[...]
        dcar_ref[...] = dxc[0:8, :]
        dxr = jnp.zeros_like(dxc)
        for k in range(CONV_WIDTH):
            tap = CONV_WIDTH - 1 - k
            ahead = dxc if k == 0 else ext_ref[pl.ds(k, rows), :]
            dxr = dxr + ahead * cw_ref[tap:tap + 1, :]
            dcw_ref[tap:tap + 1, :] += jnp.sum(xr * ahead, axis=0, keepdims=True)
        dp_ref[0, :, 0:RG_COLS] = dxr.astype(BF16)

    vec = jax.ShapeDtypeStruct((1, RG_WIDTH), F32)
    gate = jax.ShapeDtypeStruct((RG_GROUPS, RG_COLS, RG_COLS), F32)
    return pl.pallas_call(
        body, name=name, grid=(RG_GROUPS, B, S // rows),
        in_specs=[sp_["proj"], sp_["act"], sp_["act"], sp_["act"], sp_["act"], sp_["taps"],
                  sp_["gate"], sp_["vec"], sp_["gate"], sp_["vec"], sp_["vec"]],
        out_specs=[sp_["proj"], sp_["taps"], sp_["vec"], sp_["gate"], sp_["vec"], sp_["gate"], sp_["vec"], sp_["vec"]],
        out_shape=[jax.ShapeDtypeStruct(proj.shape, BF16), jax.ShapeDtypeStruct((CONV_WIDTH, RG_WIDTH), F32), vec,
                   gate, vec, gate, vec, vec],
        scratch_shapes=[pltpu.VMEM((rows + 8, RG_COLS), F32), pltpu.VMEM((rows, RG_COLS), F32),
                        pltpu.VMEM((rows, RG_COLS), F32), pltpu.VMEM((8, RG_COLS), F32),
                        pltpu.VMEM((8, RG_COLS), F32), pltpu.VMEM((8, RG_COLS), F32)],
        compiler_params=_params(("arbitrary", "arbitrary", "arbitrary")),
    )(proj, xc, hs, hprev, dgated, conv_w, wa, ba, wx, bx, lam)


def block_diag_gates(w):
    per = RG_COLS // RG_BLOCK
    w4 = w.reshape(RG_GROUPS, per, RG_BLOCK, RG_BLOCK)
    return jnp.einsum("dipq,ij->dipjq", w4, jnp.eye(per, dtype=w.dtype)).reshape(RG_GROUPS, RG_COLS, RG_COLS)


def block_diag_gates_t(dw):
    per = RG_COLS // RG_BLOCK
    dw6 = dw.reshape(RG_GROUPS, per, RG_BLOCK, per, RG_BLOCK)
    return jnp.stack([dw6[:, i, :, i, :] for i in range(per)], axis=1).reshape(RG_BLOCKS, RG_BLOCK, RG_BLOCK)


def adamw(w, parts, m, v, name, row_tile=256):
    R, C = w.shape
    br = row_tile if R % row_tile == 0 else R

    def body(w_ref, p_ref, m_ref, v_ref, g_ref, d_ref, nm_ref, nv_ref):
        g = p_ref[0].astype(F32)
        for k in range(1, N_DEV):
            g = g + p_ref[k].astype(F32)
        nm = ADAM_B1 * m_ref[...] + (1.0 - ADAM_B1) * g
        nv = ADAM_B2 * v_ref[...] + (1.0 - ADAM_B2) * (g * g)
        m_hat = nm / (1.0 - ADAM_B1 ** ADAM_STEP)
        v_hat = nv / (1.0 - ADAM_B2 ** ADAM_STEP)
        g_ref[...] = g
        d_ref[...] = -ADAM_LR * (m_hat / (jnp.sqrt(v_hat) + ADAM_EPS) + ADAM_WD * w_ref[...])
        nm_ref[...] = nm
        nv_ref[...] = nv

    row = pl.BlockSpec((br, C), lambda i: (i, 0))
    out = jax.ShapeDtypeStruct((R, C), F32)
    return pl.pallas_call(
        body, name=name, grid=(R // br,),
        in_specs=[row, pl.BlockSpec((N_DEV, br, C), lambda i: (0, i, 0)), row, row],
        out_specs=[row] * 4, out_shape=[out] * 4, compiler_params=_params(("parallel",)),
    )(w, parts, m, v)


def _peers():
    x, y, c = lax.axis_index("x"), lax.axis_index("y"), lax.axis_index("c")
    me = 4 * x + 2 * y + c
    peers = []
    for k in range(1, N_DEV):
        px = 1 - x if k & 4 else x
        py = 1 - y if k & 2 else y
        pc = 1 - c if k & 1 else c
        peers.append(((px, py, pc), 4 * px + 2 * py + pc))
    return me, peers


def exchange(arrs, name, scatter):
    n = len(arrs)

    def body(*refs):
        ins, outs = refs[:n], refs[n:2 * n]
        send_sems, recv_sems, local_sems = refs[2 * n:]
        me, peers = _peers()
        started = []
        for a in range(n):
            own = ins[a].at[me] if scatter else ins[a]
            local = pltpu.make_async_copy(own, outs[a].at[me], local_sems.at[a])
            local.start()
            started.append(local)
            for k, (peer, peer_idx) in enumerate(peers):
                src = ins[a].at[peer_idx] if scatter else ins[a]
                pltpu.make_async_remote_copy(
                    src_ref=src, dst_ref=outs[a].at[me], send_sem=send_sems.at[a, k], recv_sem=recv_sems.at[a, k],
                    device_id=peer, device_id_type=MESH).start()
        for a in range(n):
            for k, (peer, peer_idx) in enumerate(peers):
                src = ins[a].at[peer_idx] if scatter else ins[a]
                pltpu.make_async_remote_copy(
                    src_ref=src, dst_ref=outs[a].at[peer_idx], send_sem=send_sems.at[a, k],
                    recv_sem=recv_sems.at[a, k], device_id=peer, device_id_type=MESH).wait()
        for local in started:
            local.wait()

    any_spec = pl.BlockSpec(memory_space=pl.ANY)
    out_shape = [jax.ShapeDtypeStruct(a.shape if scatter else (N_DEV,) + a.shape, a.dtype) for a in arrs]
    return pl.pallas_call(
        body, name=name, in_specs=[any_spec] * n, out_specs=[any_spec] * n, out_shape=out_shape,
        scratch_shapes=[pltpu.SemaphoreType.DMA((n, N_DEV - 1)), pltpu.SemaphoreType.DMA((n, N_DEV - 1)),
                        pltpu.SemaphoreType.DMA((n,))],
    )(*arrs)


def heads_in(gathered):
    g = gathered.reshape(4, 2, D_MODEL, 8, HEAD_DIM)
    return g.transpose(2, 1, 3, 0, 4).reshape(D_MODEL, HEADS * HEAD_COLS)


def heads_out(dw):
    g = dw.reshape(D_MODEL, 2, 8, 4, HEAD_DIM)
    return g.transpose(3, 1, 0, 2, 4).reshape(N_DEV, D_MODEL, 1024)


def rg_in(gathered):
    g = gathered.reshape(2, RG_GROUPS, D_MODEL, RG_COLS)
    return g.transpose(2, 1, 0, 3).reshape(D_MODEL, 2 * RG_WIDTH)


def rg_out(dw):
    g = dw.reshape(D_MODEL, RG_GROUPS, 2, RG_COLS)
    return g.transpose(2, 1, 0, 3).reshape(N_DEV, D_MODEL, RG_COLS)


def fox_in(gathered):
    full = gathered.transpose(1, 0, 2).reshape(D_MODEL, N_DEV * 1026)
    main = full[:, :4 * D_MODEL].reshape(D_MODEL, 4, HEADS, HEAD_DIM).transpose(0, 2, 1, 3)
    f = jnp.pad(full[:, 4 * D_MODEL:], ((0, 0), (0, HEAD_DIM - HEADS)))
    return main.reshape(D_MODEL, HEADS * HEAD_COLS), f


def fox_out(dmain, df):
    main = dmain.reshape(D_MODEL, HEADS, 4, HEAD_DIM).transpose(0, 2, 1, 3).reshape(D_MODEL, 4 * D_MODEL)
    full = jnp.concatenate([main, df[:, :HEADS].astype(main.dtype)], axis=1)
    return full.reshape(D_MODEL, N_DEV, 1026).transpose(1, 0, 2)


def _seq(a, B):
    return a.reshape(B, a.shape[0] // B, a.shape[1])


def _flat(a):
    return a.reshape(a.shape[0] * a.shape[1], a.shape[2])


def mixer_a_fwd(h, w, B, tag):
    proj = matmul(h, w["w_in"], mode="nn", out_dtype=F32, name=f"{tag}_proj")
    o, gated = attn_a_fwd(_seq(proj, B), w["bias"], f"{tag}_attn")
    return _flat(gated), dict(proj=proj, o=o)


def mixer_a_bwd(dgated, h, w, saved, B, tag):
    dproj, dbias = attn_a_bwd(_seq(saved["proj"], B), w["bias"], saved["o"], _seq(dgated, B), f"{tag}_attn_bwd")
    return _flat(dproj), dict(bias=dbias)


def mixer_b_fwd(h, w, B, tag):
    proj = matmul(h, w["w_in"], mode="nn", out_dtype=F32, name=f"{tag}_proj")
    xc, hs, hprev, gated = rglru_fwd(_seq(proj, B), w["conv_w"], w["conv_b"], w["wa"], w["ba"], w["wx"], w["bx"],
                                     w["lam"], f"{tag}_rglru")
    return _flat(gated), dict(proj=proj, xc=xc, hs=hs, hprev=hprev)


def mixer_b_bwd(dgated, h, w, saved, B, tag):
    dproj, dcw, dcb, dwa, dba, dwx, dbx, dlam = rglru_bwd(
        _seq(saved["proj"], B), saved["xc"], saved["hs"], saved["hprev"], _seq(dgated, B),
        w["conv_w"], w["wa"], w["ba"], w["wx"], w["bx"], w["lam"], f"{tag}_rglru_bwd")
    return _flat(dproj), dict(conv_w=dcw, conv_b=dcb, wa=dwa, ba=dba, wx=dwx, bx=dbx, lam=dlam)


def mixer_c_fwd(h, w, B, tag):
    proj = matmul(h, w["w_in"], mode="nn", out_dtype=F32, name=f"{tag}_proj")
    f_logit = matmul(h, w["w_f"], mode="nn", out_dtype=F32, name=f"{tag}_fproj")
    cum = fox_cum_fwd(_seq(f_logit, B), w["f_bias"], f"{tag}_cum")
    cum_row = cum[:, :, :HEADS].transpose(0, 2, 1)
    o, gated = fox_fwd(_seq(proj, B), cum, cum_row, f"{tag}_attn")
    return _flat(gated), dict(proj=proj, o=o, f_logit=f_logit, cum=cum, cum_row=cum_row)


def mixer_c_bwd(dgated, h, w, saved, B, tag):
    dproj, dck = fox_bwd(_seq(saved["proj"], B), saved["cum"], saved["cum_row"], saved["o"], _seq(dgated, B),
                         f"{tag}_attn_bwd")
    S = dck.shape[-1]
    dcum = jnp.pad(-dck.reshape(B, HEADS, S).transpose(0, 2, 1), ((0, 0), (0, 0), (0, HEAD_DIM - HEADS)))
    df, dfb = fox_cum_bwd(_seq(saved["f_logit"], B), w["f_bias"], dcum, f"{tag}_cum_bwd")
    return _flat(dproj), dict(f_bias=dfb, df=_flat(df).astype(BF16))


MIXERS = {"a": (mixer_a_fwd, mixer_a_bwd), "b": (mixer_b_fwd, mixer_b_bwd), "c": (mixer_c_fwd, mixer_c_bwd)}
LAYER_KINDS = "abca"


def local_step(x, target, norm_pre, norm_post, layers):
    B, S, Dm = x.shape
    xs = [x.reshape(B * S, Dm)]
    saved = []
    for li, (kind, w) in enumerate(zip(LAYER_KINDS, layers)):
        tag = f"l{li}{kind}"
        h = prenorm_fwd(xs[-1], norm_pre[li:li + 1], f"{tag}_prenorm")
        gated, sv = MIXERS[kind][0](h, w, B, tag)
        y = matmul(gated, w["w_out"], mode="nn", out_dtype=F32, name=f"{tag}_out")
        xs.append(postnorm_fwd(xs[-1], y, norm_post[li:li + 1], f"{tag}_postnorm"))
        saved.append(dict(sv, h=h, gated=gated, y=y))
    loss, dx = loss_fwd_bwd(xs[-1], target.reshape(B * S, Dm), "loss")

    grads = [None] * len(layers)
    for li in reversed(range(len(layers))):
        kind, w, sv = LAYER_KINDS[li], layers[li], saved[li]
        tag = f"l{li}{kind}"
        dy, dg_post = postnorm_bwd(sv["y"], norm_post[li:li + 1], dx, f"{tag}_postnorm_bwd")
        dw_out = matmul(sv["gated"], dy, mode="tn", out_dtype=BF16, name=f"{tag}_dwout", bm=512, bn=1024, bk=1024)
        dgated = matmul(dy, w["w_out"], mode="nt", out_dtype=F32, name=f"{tag}_dgated", bn=512)
        dproj, gw = MIXERS[kind][1](dgated, sv["h"], w, sv, B, tag)
        dw_in = matmul(sv["h"], dproj, mode="tn", out_dtype=BF16, name=f"{tag}_dwin", bm=512, bn=1024, bk=1024)
        dhs = [matmul(dproj, w["w_in"], mode="nt", out_dtype=F32, name=f"{tag}_dh", bn=1024, bk=1024)]
        if kind == "c":
            gw["w_f"] = matmul(sv["h"], gw["df"], mode="tn", out_dtype=BF16, name=f"{tag}_dwf", bm=512, bk=1024)
            dhs.append(matmul(gw.pop("df"), w["w_f"], mode="nt", out_dtype=F32, name=f"{tag}_dhf", bn=1024))
        dx, dg_pre = prenorm_bwd(xs[li], norm_pre[li:li + 1], dhs, dx, f"{tag}_prenorm_bwd")
        grads[li] = dict(gw, w_in=dw_in, w_out=dw_out, norm_pre=dg_pre, norm_post=dg_post)
    return loss, dx.reshape(B, S, Dm), grads


SHARDED = ("a_w_in", "a_w_out", "b_w_in", "b_conv_w", "b_w_out", "c_w_in", "c_w_out")
REPLICATED = ("norm_pre", "norm_post", "a_rel_bias", "b_conv_b", "b_gate_a_w", "b_gate_a_b", "b_gate_x_w",
              "b_gate_x_b", "b_lambda", "c_f_bias")
WEIGHTS = ("norm_pre", "norm_post", "a_w_in", "a_rel_bias", "a_w_out", "b_w_in", "b_conv_w", "b_conv_b",
           "b_gate_a_w", "b_gate_a_b", "b_gate_x_w", "b_gate_x_b", "b_lambda", "b_w_out", "c_w_in", "c_f_bias",
           "c_w_out")


def build_layers(gathered, small):
    def rows(name, j):
        g = gathered[name][:, j]
        return g.reshape(N_DEV * g.shape[1], g.shape[2])

    def layer_a(j):
        return dict(w_in=heads_in(gathered["a_w_in"][:, j]), w_out=rows("a_w_out", j),
                    bias=band_bias(small["a_rel_bias"][j]))

    c_in, c_f = fox_in(gathered["c_w_in"][:, 0])
    layer_b = dict(
        w_in=rg_in(gathered["b_w_in"][:, 0]), w_out=rows("b_w_out", 0),
        conv_w=gathered["b_conv_w"][:, 0].transpose(1, 0, 2).reshape(CONV_WIDTH, RG_WIDTH),
        conv_b=small["b_conv_b"], lam=small["b_lambda"],
        wa=block_diag_gates(small["b_gate_a_w"][0]).astype(BF16), ba=small["b_gate_a_b"].reshape(1, RG_WIDTH),
        wx=block_diag_gates(small["b_gate_x_w"][0]).astype(BF16), bx=small["b_gate_x_b"].reshape(1, RG_WIDTH))
    layer_c = dict(w_in=c_in, w_f=c_f, w_out=rows("c_w_out", 0),
                   f_bias=jnp.pad(small["c_f_bias"], ((0, 0), (0, HEAD_DIM - HEADS))))
    return [layer_a(0), layer_b, layer_c, layer_a(1)]


def gradient_blocks(grads, small):
    ga, gb, gc = (grads[0], grads[3]), grads[1], grads[2]

    def rows(g):
        return g.reshape(N_DEV, g.shape[0] // N_DEV, g.shape[1])

    blocks = dict(
        a_w_in=jnp.stack([heads_out(g["w_in"]) for g in ga], axis=1),
        a_w_out=jnp.stack([rows(g["w_out"]) for g in ga], axis=1),
        b_w_in=rg_out(gb["w_in"])[:, None],
        b_conv_w=gb["conv_w"].reshape(CONV_WIDTH, N_DEV, RG_WIDTH // N_DEV).transpose(1, 0, 2)[:, None],
        b_w_out=rows(gb["w_out"])[:, None],
        c_w_in=fox_out(gc["w_in"], gc["w_f"])[:, None],
        c_w_out=rows(gc["w_out"])[:, None])

    def rel_bias_grad(j, dbias):
        return jax.vjp(band_bias, small["a_rel_bias"][j])[1](dbias)[0]

    partial = dict(
        norm_pre=jnp.concatenate([g["norm_pre"] for g in grads], axis=0),
        norm_post=jnp.concatenate([g["norm_post"] for g in grads], axis=0),
        a_rel_bias=jnp.stack([rel_bias_grad(j, g["bias"]) for j, g in enumerate(ga)], axis=0),
        b_conv_b=gb["conv_b"], b_lambda=gb["lam"],
        b_gate_a_w=block_diag_gates_t(gb["wa"])[None], b_gate_a_b=gb["ba"].reshape(1, RG_BLOCKS, RG_BLOCK),
        b_gate_x_w=block_diag_gates_t(gb["wx"])[None], b_gate_x_b=gb["bx"].reshape(1, RG_BLOCKS, RG_BLOCK),
        c_f_bias=gc["f_bias"][:, :HEADS])
    return blocks, partial


PACK_LANES = 128
PACK_ALIGN = 8 * PACK_LANES


def pack(parts):
    flat = []
    for p in parts:
        n = p.size
        flat.append(jnp.pad(p.reshape(n), (0, -n % PACK_ALIGN)).reshape(-1, PACK_LANES))
    return jnp.concatenate(flat, axis=0)


def unpack(packed, shapes):
    out, row = [], 0
    for shape in shapes:
        n = 1
        for s in shape:
            n *= s
        n_rows = (n + PACK_ALIGN - 1) // PACK_ALIGN * 8
        out.append(packed[row:row + n_rows].reshape(-1)[:n].reshape(shape))
        row += n_rows
    return out


def _as_rows(a):
    return a.reshape(-1, a.shape[-1])


def kernel(x, norm_pre, norm_post, a_w_in, a_rel_bias, a_w_out, b_w_in, b_conv_w, b_conv_b, b_gate_a_w, b_gate_a_b, b_gate_x_w, b_gate_x_b, b_lambda, b_w_out, c_w_in, c_f_bias, c_w_out, loss_target, m_norm_pre, m_norm_post, m_a_w_in, m_a_rel_bias, m_a_w_out, m_b_w_in, m_b_conv_w, m_b_conv_b, m_b_gate_a_w, m_b_gate_a_b, m_b_gate_x_w, m_b_gate_x_b, m_b_lambda, m_b_w_out, m_c_w_in, m_c_f_bias, m_c_w_out, v_norm_pre, v_norm_post, v_a_w_in, v_a_rel_bias, v_a_w_out, v_b_w_in, v_b_conv_w, v_b_conv_b, v_b_gate_a_w, v_b_gate_a_b, v_b_gate_x_w, v_b_gate_x_b, v_b_lambda, v_b_w_out, v_c_w_in, v_c_f_bias, v_c_w_out):
    args = dict(locals())
    w = {n: args[n] for n in WEIGHTS}
    m = {n: args["m_" + n] for n in WEIGHTS}
    v = {n: args["v_" + n] for n in WEIGHTS}

    shards = [w[n] if n == "b_conv_w" else w[n].astype(BF16) for n in SHARDED]
    gathered = dict(zip(SHARDED, exchange(shards, "gather_weights", scatter=False)))
    layers = build_layers(gathered, w)
    loss, grad_x, grads = local_step(x, loss_target, norm_pre, norm_post, layers)
    blocks, partial = gradient_blocks(grads, w)

    received = dict(zip(SHARDED, exchange([blocks[n] for n in SHARDED], "scatter_grads", scatter=True)))
    small_parts = exchange([pack([partial[n] for n in REPLICATED])], "gather_small_grads", scatter=False)[0]

    grad, delta, new_m, new_v = {}, {}, {}, {}
    for n in SHARDED:
        parts = received[n].reshape((N_DEV,) + _as_rows(w[n]).shape)
        outs = adamw(_as_rows(w[n]), parts, _as_rows(m[n]), _as_rows(v[n]), f"adamw_{n}")
        grad[n], delta[n], new_m[n], new_v[n] = (o.reshape(w[n].shape) for o in outs)
    shapes = [w[n].shape for n in REPLICATED]
    outs = adamw(pack([w[n] for n in REPLICATED]), small_parts, pack([m[n] for n in REPLICATED]),
                 pack([v[n] for n in REPLICATED]), "adamw_replicated")
    for res, packed in zip((grad, delta, new_m, new_v), outs):
        res.update(zip(REPLICATED, unpack(packed, shapes)))

    total = lax.psum(loss[0, 0], ("x", "y", "c"))
    return (total, grad_x, *[grad[n] for n in WEIGHTS], *[delta[n] for n in WEIGHTS],
            *[new_m[n] for n in WEIGHTS], *[new_v[n] for n in WEIGHTS])
```

```python
import functools

import jax
import jax.numpy as jnp
from jax import lax
from jax.experimental import pallas as pl
from jax.experimental.pallas import tpu as pltpu

F32 = jnp.float32
BF16 = jnp.bfloat16

N_DEV = 8
D_MODEL = 2048
HEADS = 16
HEAD_DIM = 128
CHUNK = 64
LEFT_CHUNKS = 8
REL_CLIP = 256
N_REL = 2 * REL_CLIP + 1
TQ = 256
A_PAD = LEFT_CHUNKS * CHUNK
A_KW = A_PAD + TQ
RG_WIDTH = 2560
RG_BLOCKS = 16
RG_BLOCK = 160
RG_COLS = 640
RG_GROUPS = RG_WIDTH // RG_COLS
RG_C = 8.0
CONV_WIDTH = 4
RMS_EPS = 1e-6
NEG_INF = -1e30
ADAM_LR = 0.001
ADAM_B1 = 0.9
ADAM_B2 = 0.999
ADAM_EPS = 1e-08
ADAM_WD = 0.01
ADAM_STEP = 10
VMEM_LIMIT = 56 * 1024 * 1024
MESH = pl.DeviceIdType.MESH


def _params(sem, vmem=VMEM_LIMIT):
    return pltpu.CompilerParams(dimension_semantics=sem, vmem_limit_bytes=vmem)


def _sigmoid(x):
    return 1.0 / (1.0 + jnp.exp(-x))


def _log1p(y):
    u = 1.0 + y
    return jnp.where(u == 1.0, y, jnp.log(u) * (y / jnp.where(u == 1.0, 1.0, u - 1.0)))


def _softplus(x):
    return jnp.maximum(x, 0.0) + _log1p(jnp.exp(-jnp.abs(x)))


def _dot(a, b, dims):
    return lax.dot_general(a, b, (dims, ((), ())), preferred_element_type=F32)


def _dot_nn(a, b):
    return _dot(a, b, ((1,), (0,)))


def _dot_nt(a, b):
    return _dot(a, b, ((1,), (1,)))


def _dot_tn(a, b):
    return _dot(a, b, ((0,), (0,)))


def _peers():
    x, y, c = lax.axis_index("x"), lax.axis_index("y"), lax.axis_index("c")
    me = 4 * x + 2 * y + c
    peers = []
    for k in range(1, N_DEV):
        px = 1 - x if k & 4 else x
        py = 1 - y if k & 2 else y
        pc = 1 - c if k & 1 else c
        peers.append(((px, py, pc), 4 * px + 2 * py + pc))
    return me, peers


class Ride:
    def __init__(self, arrs, scatter):
        self.arrs, self.scatter, self.out = list(arrs), scatter, None

    def out_shapes(self):
        return [jax.ShapeDtypeStruct(a.shape if self.scatter else (N_DEV,) + a.shape, a.dtype) for a in self.arrs]

    def sem_shapes(self):
        n = len(self.arrs)
        return [pltpu.SemaphoreType.DMA((n, N_DEV - 1)), pltpu.SemaphoreType.DMA((n, N_DEV - 1)),
                pltpu.SemaphoreType.DMA((n,))]

    def _copies(self, ins, outs, sems, landing):
        send_sems, recv_sems, local_sems = sems
        me, peers = _peers()
        local, remote = [], []
        for a, (src, dst) in enumerate(zip(ins, outs)):
            local.append(pltpu.make_async_copy(src.at[me] if self.scatter else src, dst.at[me], local_sems.at[a]))
            for k, (peer, peer_idx) in enumerate(peers):
                remote.append(pltpu.make_async_remote_copy(
                    src_ref=src.at[peer_idx] if self.scatter else src, dst_ref=dst.at[peer_idx if landing else me],
                    send_sem=send_sems.at[a, k], recv_sem=recv_sems.at[a, k], device_id=peer, device_id_type=MESH))
        return local, remote

    def start(self, ins, outs, sems):
        local, remote = self._copies(ins, outs, sems, landing=False)
        for cp in local + remote:
            cp.start()

    def wait(self, ins, outs, sems):
        local, remote = self._copies(ins, outs, sems, landing=True)
        for cp in remote + local:
            cp.wait()


def _call(body, *, name, grid, in_specs, out_specs, out_shape, args, scratch_shapes=(), semantics=None, ride=None,
          aliases=None):
    scratch_shapes = list(scratch_shapes)
    if ride is None:
        return pl.pallas_call(
            body, name=name, grid=grid, in_specs=in_specs, out_specs=out_specs, out_shape=out_shape,
            scratch_shapes=scratch_shapes, input_output_aliases=aliases or {},
            compiler_params=_params(semantics if grid else None))(*args)
    assert not aliases
    n_in, n_out, n_sc, n_r = len(in_specs), len(out_specs), len(scratch_shapes), len(ride.arrs)

    def riding(*refs):
        ins, r_ins = refs[:n_in], refs[n_in:n_in + n_r]
        outs, r_outs = refs[n_in + n_r:n_in + n_r + n_out], refs[n_in + n_r + n_out:n_in + 2 * n_r + n_out]
        rest = refs[n_in + 2 * n_r + n_out:]
        scratch, sems = rest[:n_sc], rest[n_sc:]
        first = last = None
        for axis, size in enumerate(grid):
            pid = pl.program_id(axis)
            first = (pid == 0) if first is None else first & (pid == 0)
            last = (pid == size - 1) if last is None else last & (pid == size - 1)
        if grid:
            pl.when(first)(lambda: ride.start(r_ins, r_outs, sems))
        else:
            ride.start(r_ins, r_outs, sems)
        body(*ins, *outs, *scratch)
        if grid:
            pl.when(last)(lambda: ride.wait(r_ins, r_outs, sems))
        else:
            ride.wait(r_ins, r_outs, sems)

    any_spec = pl.BlockSpec(memory_space=pl.ANY)
    res = pl.pallas_call(
        riding, name=name, grid=grid, in_specs=list(in_specs) + [any_spec] * n_r,
        out_specs=list(out_specs) + [any_spec] * n_r, out_shape=list(out_shape) + ride.out_shapes(),
        scratch_shapes=scratch_shapes + ride.sem_shapes(),
        compiler_params=_params(("arbitrary",) * len(grid) if grid else None))(*args, *ride.arrs)
    ride.out = list(res[n_out:])
    return list(res[:n_out])


def exchange(ride, name):
    _call(lambda: None, name=name, grid=(), in_specs=[], out_specs=[], out_shape=[], args=[], ride=ride)
    return ride.out


LANES = 128


def _fit(dims, want):
    dims = tuple(dims)
    if len(set(dims)) == 1 and dims[0] <= want:
        return dims[0]
    return max(t for t in range(LANES, want + 1, LANES) if all(d % t == 0 for d in dims))


def _cols(arr):
    return arr.shape[-1] * (arr.shape[0] if len(arr.shape) == 3 else 1)


def _tile_spec(shape, rblk, cblk, rc):
    if len(shape) == 2:
        return pl.BlockSpec((rblk, cblk), rc)
    per = shape[2] // cblk

    def index_map(*ids):
        r, c = rc(*ids)
        return (c // per, r, c % per)

    return pl.BlockSpec((1, rblk, cblk), index_map)


def matmul(a, b, *, mode, out_dtype, name, bm=512, bn=1024, bk=2048, out_slabs=1, ride=None):
    a_rows, a_cols, b_rows, b_cols = a.shape[-2], _cols(a), b.shape[-2], _cols(b)
    (K, M) = (a_rows, a_cols) if mode == "tn" else (a_cols, a_rows)
    N = b_rows if mode == "nt" else b_cols
    assert K == (b_cols if mode == "nt" else b_rows), (name, a.shape, b.shape)
    out_shape = (M, N) if out_slabs == 1 else (out_slabs, M, N // out_slabs)
    widths = dict(m=[M], n=[N, out_shape[-1]], k=[K])
    widths["m" if mode == "tn" else "k"].append(a.shape[-1])
    widths["k" if mode == "nt" else "n"].append(b.shape[-1])
    bm, bn, bk = _fit(widths["m"], bm), _fit(widths["n"], bn), _fit(widths["k"], bk)
    nk = K // bk
    dims = {"nn": ((1,), (0,)), "nt": ((1,), (1,)), "tn": ((0,), (0,))}[mode]

    def val(ref):
        return ref[0] if len(ref.shape) == 3 else ref[...]

    def put(ref, x):
        if len(ref.shape) == 3:
            ref[0] = x.astype(ref.dtype)
        else:
            ref[...] = x.astype(ref.dtype)

    def body(a_ref, b_ref, o_ref, *scratch):
        if nk == 1:
            put(o_ref, _dot(val(a_ref), val(b_ref), dims))
            return
        acc_ref, = scratch
        k = pl.program_id(2)

        @pl.when(k == 0)
        def _():
            acc_ref[...] = jnp.zeros_like(acc_ref)

        acc_ref[...] += _dot(val(a_ref), val(b_ref), dims)

        @pl.when(k == nk - 1)
        def _():
            put(o_ref, acc_ref[...])

    if mode == "tn":
        a_spec = _tile_spec(a.shape, bk, bm, lambda j, i, k: (k, i))
    else:
        a_spec = _tile_spec(a.shape, bm, bk, lambda j, i, k: (i, k))
    if mode == "nt":
        b_spec = _tile_spec(b.shape, bn, bk, lambda j, i, k: (j, k))
    else:
        b_spec = _tile_spec(b.shape, bk, bn, lambda j, i, k: (k, j))
    return _call(
        body, name=name, grid=(N // bn, M // bm, nk), in_specs=[a_spec, b_spec],
        out_specs=[_tile_spec(out_shape, bm, bn, lambda j, i, k: (i, j))],
        out_shape=[jax.ShapeDtypeStruct(out_shape, out_dtype)],
        scratch_shapes=[] if nk == 1 else [pltpu.VMEM((bm, bn), F32)],
        semantics=("parallel", "parallel", "arbitrary"), args=[a, b], ride=ride)[0]


ROW_TILE = 256


def _rms_stats(z):
    r = lax.rsqrt(jnp.mean(z * z, axis=-1, keepdims=True) + RMS_EPS)
    return r, z * r


def _rms_bwd(n, r, g, dout):
    dn = dout * g
    return r * (dn - n * jnp.mean(dn * n, axis=-1, keepdims=True))


def _row_spec(T, Dm):
    bt = min(ROW_TILE, T)
    return bt, pl.BlockSpec((bt, Dm), lambda i: (i, 0)), pl.BlockSpec((1, Dm), lambda i: (0, 0))


def prenorm_fwd(x, g, name):
    T, Dm = x.shape
    bt, row, vec = _row_spec(T, Dm)

    def body(x_ref, g_ref, h_ref):
        _, n = _rms_stats(x_ref[...])
        h_ref[...] = (n * g_ref[...]).astype(BF16)

    return pl.pallas_call(
        body, name=name, grid=(T // bt,), in_specs=[row, vec], out_specs=row,
        out_shape=jax.ShapeDtypeStruct((T, Dm), BF16), compiler_params=_params(("parallel",)),
    )(x, g)


def postnorm_fwd(x, y, g, name):
    T, Dm = x.shape
    bt, row, vec = _row_spec(T, Dm)

    def body(x_ref, y_ref, g_ref, o_ref):
        _, n = _rms_stats(y_ref[...])
        o_ref[...] = x_ref[...] + n * g_ref[...]

    return pl.pallas_call(
        body, name=name, grid=(T // bt,), in_specs=[row, row, vec], out_specs=row,
        out_shape=jax.ShapeDtypeStruct((T, Dm), F32), compiler_params=_params(("parallel",)),
    )(x, y, g)


def loss_fwd_bwd(xf, target, name):
    T, Dm = xf.shape
    bt, row, _ = _row_spec(T, Dm)

    def body(x_ref, t_ref, l_ref, d_ref):
        @pl.when(pl.program_id(0) == 0)
        def _():
            l_ref[...] = jnp.zeros_like(l_ref)

        err = x_ref[...] - t_ref[...]
        per_tok = jnp.mean(err * err, axis=-1, keepdims=True)
        l_ref[...] += 0.5 * jnp.sum(per_tok, axis=0, keepdims=True)
        d_ref[...] = err * (1.0 / Dm)

    return pl.pallas_call(
        body, name=name, grid=(T // bt,), in_specs=[row, row],
        out_specs=[pl.BlockSpec((1, 1), lambda i: (0, 0)), row],
        out_shape=[jax.ShapeDtypeStruct((1, 1), F32), jax.ShapeDtypeStruct((T, Dm), F32)],
        compiler_params=_params(("arbitrary",)),
    )(xf, target)


def postnorm_bwd(y, g, dout, name):
    T, Dm = y.shape
    bt, row, vec = _row_spec(T, Dm)

    def body(y_ref, g_ref, d_ref, dy_ref, dg_ref):
        @pl.when(pl.program_id(0) == 0)
        def _():
            dg_ref[...] = jnp.zeros_like(dg_ref)

        r, n = _rms_stats(y_ref[...])
        dout_v = d_ref[...]
        dg_ref[...] += jnp.sum(dout_v * n, axis=0, keepdims=True)
        dy_ref[...] = _rms_bwd(n, r, g_ref[...], dout_v).astype(BF16)

    return pl.pallas_call(
        body, name=name, grid=(T // bt,), in_specs=[row, vec, row], out_specs=[row, vec],
        out_shape=[jax.ShapeDtypeStruct((T, Dm), BF16), jax.ShapeDtypeStruct((1, Dm), F32)],
        compiler_params=_params(("arbitrary",)),
    )(y, g, dout)


def prenorm_bwd(x, g, dhs, dres, name):
    T, Dm = x.shape
    bt, row, vec = _row_spec(T, Dm)
    n_dh = len(dhs)

    def body(x_ref, g_ref, *refs):
        dh_refs, (dr_ref, dx_ref, dg_ref) = refs[:n_dh], refs[n_dh:]

        @pl.when(pl.program_id(0) == 0)
        def _():
            dg_ref[...] = jnp.zeros_like(dg_ref)

        r, n = _rms_stats(x_ref[...])
        dh_v = dh_refs[0][...]
        for extra in dh_refs[1:]:
            dh_v = dh_v + extra[...]
        dg_ref[...] += jnp.sum(dh_v * n, axis=0, keepdims=True)
        dx_ref[...] = dr_ref[...] + _rms_bwd(n, r, g_ref[...], dh_v)

    return pl.pallas_call(
        body, name=name, grid=(T // bt,), in_specs=[row, vec] + [row] * (n_dh + 1), out_specs=[row, vec],
        out_shape=[jax.ShapeDtypeStruct((T, Dm), F32), jax.ShapeDtypeStruct((1, Dm), F32)],
        compiler_params=_params(("arbitrary",)),
    )(x, g, *dhs, dres)


def _silu(g):
    return g * _sigmoid(g)


def _gate_bwd(dgated, core, g):
    sg = _sigmoid(g)
    return dgated * (g * sg), dgated * core * (sg * (1.0 + g * (1.0 - sg)))


def _softmax_rows(s):
    e = jnp.exp(s - jnp.max(s, axis=-1, keepdims=True))
    return e / jnp.sum(e, axis=-1, keepdims=True)


def _band_scores(q, kw, bias, r0):
    s = _dot_nt(q, kw) * (HEAD_DIM ** -0.5) + bias
    i = lax.broadcasted_iota(jnp.int32, s.shape, 0)
    j = lax.broadcasted_iota(jnp.int32, s.shape, 1)
    lo = (i >> 6) << 6
    vis = (j >= lo) & (j < lo + (LEFT_CHUNKS + 1) * CHUNK) & (j >= A_PAD - r0)
    return jnp.where(vis, s, NEG_INF)


def _fill_padded_kv(p_ref, kp_ref, vp_ref):
    zeros = jnp.zeros((A_PAD, HEAD_DIM), BF16)
    kp_ref[0:A_PAD, :] = zeros
    vp_ref[0:A_PAD, :] = zeros
    kp_ref[A_PAD:, :] = p_ref[1, 0].astype(BF16)
    vp_ref[A_PAD:, :] = p_ref[2, 0].astype(BF16)


def _head_specs(S, order):
    def idx(fn):
        return lambda *ids: fn(**dict(zip(order, ids)))

    return (pl.BlockSpec((4, 1, S, HEAD_DIM), idx(lambda b, h, t: (0, b, 0, h))),
            pl.BlockSpec((1, TQ, HEAD_DIM), idx(lambda b, h, t: (b, t, h))))


def attn_a_fwd(proj, bias, name, ride=None):
    _, B, S, W = proj.shape
    nt = S // TQ

    def body(p_ref, b_ref, o_ref, gt_ref, kp_ref, vp_ref):
        t = pl.program_id(2)

        @pl.when(t == 0)
        def _():
            _fill_padded_kv(p_ref, kp_ref, vp_ref)

        r0 = pl.multiple_of(t * TQ, TQ)
        q = p_ref[0, 0, pl.ds(r0, TQ), :].astype(BF16)
        g = p_ref[3, 0, pl.ds(r0, TQ), :]
        p = _softmax_rows(_band_scores(q, kp_ref[pl.ds(r0, A_KW), :], b_ref[0], r0))
        o = _dot_nn(p.astype(BF16), vp_ref[pl.ds(r0, A_KW), :])
        o_ref[0] = o
        gt_ref[0] = (o * _silu(g)).astype(BF16)

    seq, tile = _head_specs(S, "bht")
    return _call(
        body, name=name, grid=(B, HEADS, nt),
        in_specs=[seq, pl.BlockSpec((1, TQ, A_KW), lambda b, h, t: (h, 0, 0))], out_specs=[tile, tile],
        out_shape=[jax.ShapeDtypeStruct((B, S, W), F32), jax.ShapeDtypeStruct((B, S, W), BF16)],
        scratch_shapes=[pltpu.VMEM((A_PAD + S, HEAD_DIM), BF16), pltpu.VMEM((A_PAD + S, HEAD_DIM), BF16)],
        semantics=("parallel", "parallel", "arbitrary"), args=[proj, bias], ride=ride)


def attn_a_bwd(proj, bias, o, dgated, name, ride=None):
    _, B, S, W = proj.shape
    nt = S // TQ

    def body(p_ref, b_ref, o_ref, dgt_ref, dp_ref, db_ref, kp_ref, vp_ref, dk_ref, dv_ref):
        b_, t = pl.program_id(1), pl.program_id(2)

        @pl.when(t == 0)
        def _():
            _fill_padded_kv(p_ref, kp_ref, vp_ref)
            dk_ref[...] = jnp.zeros_like(dk_ref)
            dv_ref[...] = jnp.zeros_like(dv_ref)

        @pl.when((t == 0) & (b_ == 0))
        def _():
            db_ref[...] = jnp.zeros_like(db_ref)

        r0 = pl.multiple_of(t * TQ, TQ)
        rows, win = pl.ds(r0, TQ), pl.ds(r0, A_KW)
        q = p_ref[0, 0, rows, :].astype(BF16)
        g = p_ref[3, 0, rows, :]
        kw, vw = kp_ref[win, :], vp_ref[win, :]
        p = _softmax_rows(_band_scores(q, kw, b_ref[0], r0))
        do, dg = _gate_bwd(dgt_ref[0], o_ref[0], g)
        do = do.astype(BF16)
        dv_ref[win, :] += _dot_tn(p.astype(BF16), do)
        dpr = _dot_nt(do, vw)
        ds = p * (dpr - jnp.sum(p * dpr, axis=-1, keepdims=True))
        db_ref[0] += ds
        ds = (ds * (HEAD_DIM ** -0.5)).astype(BF16)
        dk_ref[win, :] += _dot_tn(ds, q)
        dp_ref[0, 0, rows, :] = _dot_nn(ds, kw).astype(BF16)
        dp_ref[3, 0, rows, :] = dg.astype(BF16)

        @pl.when(t == nt - 1)
        def _():
            dp_ref[1, 0] = dk_ref[A_PAD:, :].astype(BF16)
            dp_ref[2, 0] = dv_ref[A_PAD:, :].astype(BF16)

    seq, tile = _head_specs(S, "hbt")
    bias_spec = pl.BlockSpec((1, TQ, A_KW), lambda h, b, t: (h, 0, 0))
    return _call(
        body, name=name, grid=(HEADS, B, nt), in_specs=[seq, bias_spec, tile, tile], out_specs=[seq, bias_spec],
        out_shape=[jax.ShapeDtypeStruct(proj.shape, BF16), jax.ShapeDtypeStruct(bias.shape, F32)],
        scratch_shapes=[pltpu.VMEM((A_PAD + S, HEAD_DIM), BF16), pltpu.VMEM((A_PAD + S, HEAD_DIM), BF16),
                        pltpu.VMEM((A_PAD + S, HEAD_DIM), F32), pltpu.VMEM((A_PAD + S, HEAD_DIM), F32)],
        semantics=("arbitrary", "arbitrary", "arbitrary"), args=[proj, bias, o, dgated], ride=ride)


def band_bias(rel_bias):
    gen = jnp.concatenate([rel_bias[:, 1:2 * REL_CLIP],
                           jnp.broadcast_to(rel_bias[:, 2 * REL_CLIP:], (HEADS, A_KW - REL_CLIP))], axis=1)
    length = TQ + A_KW - 1
    assert gen.shape[1] == length
    rev = jnp.concatenate([gen[:, ::-1], jnp.zeros((HEADS, 1), rel_bias.dtype)], axis=1)
    sheared = jnp.tile(rev, (1, TQ))[:, :TQ * length].reshape(HEADS, TQ, length)
    return sheared[:, :, TQ - 1:]


def _group_scan(a, u, carry, reverse=False):
    row = lax.broadcasted_iota(jnp.int32, u.shape, 0)
    for k in (1, 2, 4):
        shift = 8 - k if reverse else k
        valid = (row < 8 - k) if reverse else (row >= k)
        u_sh = pltpu.roll(u, shift, 0)
        if a is None:
            u = jnp.where(valid, u + u_sh, u)
        else:
            a_sh = pltpu.roll(a, shift, 0)
            u = jnp.where(valid, a * u_sh + u, u)
            a = jnp.where(valid, a * a_sh, a)
    return (u + carry) if a is None else (a * carry + u)


def _scan_rows(n_rows, step, carry0, reverse=False):
    groups = n_rows // 8

    def loop(i, carry):
        gi = (groups - 1 - i) if reverse else i
        return step(pl.multiple_of(gi * 8, 8), carry)

    return lax.fori_loop(0, groups, loop, carry0)


def fox_cum_fwd(f_logit, f_bias, name):
    B, S, L = f_logit.shape

    def body(f_ref, b_ref, c_ref):
        z = f_ref[0] + b_ref[...]
        c_ref[0] = jnp.minimum(z, 0.0) - _log1p(jnp.exp(-jnp.abs(z)))

        def step(r0, carry):
            h = _group_scan(None, c_ref[0, pl.ds(r0, 8), :], carry)
            c_ref[0, pl.ds(r0, 8), :] = h
            return h[7:8, :]

        _scan_rows(S, step, jnp.zeros((1, L), F32))

    return pl.pallas_call(
        body, name=name, grid=(B,),
        in_specs=[pl.BlockSpec((1, S, L), lambda b: (b, 0, 0)), pl.BlockSpec((1, L), lambda b: (0, 0))],
        out_specs=pl.BlockSpec((1, S, L), lambda b: (b, 0, 0)),
        out_shape=jax.ShapeDtypeStruct((B, S, L), F32), compiler_params=_params(("parallel",)),
    )(f_logit, f_bias)


def fox_cum_bwd(f_logit, f_bias, dcum, name):
    B, S, L = f_logit.shape

    def body(f_ref, b_ref, d_ref, df_ref, db_ref):
        @pl.when(pl.program_id(0) == 0)
        def _():
            db_ref[...] = jnp.zeros_like(db_ref)

        def step(r0, carry):
            h = _group_scan(None, d_ref[0, pl.ds(r0, 8), :], carry, reverse=True)
            df_ref[0, pl.ds(r0, 8), :] = h
            return h[0:1, :]

        _scan_rows(S, step, jnp.zeros((1, L), F32), reverse=True)
        df = df_ref[0] * _sigmoid(-(f_ref[0] + b_ref[...]))
        df_ref[0] = df
        db_ref[...] += jnp.sum(df, axis=0, keepdims=True)

    seq = pl.BlockSpec((1, S, L), lambda b: (b, 0, 0))
    vec = pl.BlockSpec((1, L), lambda b: (0, 0))
    return pl.pallas_call(
        body, name=name, grid=(B,), in_specs=[seq, vec, seq], out_specs=[seq, vec],
        out_shape=[jax.ShapeDtypeStruct((B, S, L), F32), jax.ShapeDtypeStruct((1, L), F32)],
        compiler_params=_params(("arbitrary",)),
    )(f_logit, f_bias, dcum)


def _fox_scores(q, k, cc, cr, h, r0):
    lane = lax.broadcasted_iota(jnp.int32, cc.shape, 1)
    cq = jnp.sum(jnp.where(lane == h, cc, 0.0), axis=1, keepdims=True)
    sub = lax.broadcasted_iota(jnp.int32, cr.shape, 0)
    ck = jnp.sum(jnp.where(sub == h, cr, 0.0), axis=0, keepdims=True)
    s = _dot_nt(q, k) * (HEAD_DIM ** -0.5) + (cq - ck)
    qpos = r0 + lax.broadcasted_iota(jnp.int32, s.shape, 0)
    kpos = lax.broadcasted_iota(jnp.int32, s.shape, 1)
    return jnp.where(kpos <= qpos, s, NEG_INF)


KEY_STEP = 512


def _by_causal_width(t, S, fn):
    per = KEY_STEP // TQ
    for c in range(S // KEY_STEP):
        pl.when(t // per == c)(functools.partial(fn, (c + 1) * KEY_STEP))


def fox_fwd(proj, cum_col, cum_row, name, ride=None):
    _, B, S, W = proj.shape
    nt = S // TQ

    def body(p_ref, cc_ref, cr_ref, o_ref, gt_ref, k_ref, v_ref):
        h, t = pl.program_id(1), pl.program_id(2)

        @pl.when(t == 0)
        def _():
            k_ref[...] = p_ref[1, 0].astype(BF16)
            v_ref[...] = p_ref[2, 0].astype(BF16)

        r0 = pl.multiple_of(t * TQ, TQ)
        q = p_ref[0, 0, pl.ds(r0, TQ), :].astype(BF16)
        g = p_ref[3, 0, pl.ds(r0, TQ), :]

        def tile_out(width):
            p = _softmax_rows(_fox_scores(q, k_ref[0:width, :], cc_ref[0], cr_ref[0, :, 0:width], h, r0))
            o = _dot_nn(p.astype(BF16), v_ref[0:width, :])
            o_ref[0] = o
            gt_ref[0] = (o * _silu(g)).astype(BF16)

        _by_causal_width(t, S, tile_out)

    seq, tile = _head_specs(S, "bht")
    return _call(
        body, name=name, grid=(B, HEADS, nt),
        in_specs=[seq, pl.BlockSpec((1, TQ, cum_col.shape[2]), lambda b, h, t: (b, t, 0)),
                  pl.BlockSpec((1, HEADS, S), lambda b, h, t: (b, 0, 0))],
        out_specs=[tile, tile],
        out_shape=[jax.ShapeDtypeStruct((B, S, W), F32), jax.ShapeDtypeStruct((B, S, W), BF16)],
        scratch_shapes=[pltpu.VMEM((S, HEAD_DIM), BF16), pltpu.VMEM((S, HEAD_DIM), BF16)],
        semantics=("parallel", "parallel", "arbitrary"), args=[proj, cum_col, cum_row], ride=ride)


def fox_bwd(proj, cum_col, cum_row, o, dgated, name, ride=None):
    _, B, S, W = proj.shape
    nt = S // TQ

    def body(p_ref, cc_ref, cr_ref, o_ref, dgt_ref, dp_ref, dc_ref, k_ref, v_ref, dk_ref, dv_ref):
        h, t = pl.program_id(1), pl.program_id(2)

        @pl.when(t == 0)
        def _():
            k_ref[...] = p_ref[1, 0].astype(BF16)
            v_ref[...] = p_ref[2, 0].astype(BF16)
            dk_ref[...] = jnp.zeros_like(dk_ref)
            dv_ref[...] = jnp.zeros_like(dv_ref)
            dc_ref[...] = jnp.zeros_like(dc_ref)

        r0 = pl.multiple_of(t * TQ, TQ)
        rows = pl.ds(r0, TQ)
        q = p_ref[0, 0, rows, :].astype(BF16)
        g = p_ref[3, 0, rows, :]
        do, dg = _gate_bwd(dgt_ref[0], o_ref[0], g)
        do = do.astype(BF16)
        dp_ref[3, 0, rows, :] = dg.astype(BF16)

        def tile_grads(width):
            k, v = k_ref[0:width, :], v_ref[0:width, :]
            p = _softmax_rows(_fox_scores(q, k, cc_ref[0], cr_ref[0, :, 0:width], h, r0))
            dv_ref[0:width, :] += _dot_tn(p.astype(BF16), do)
            dpr = _dot_nt(do, v)
            ds = p * (dpr - jnp.sum(p * dpr, axis=-1, keepdims=True))
            dc_ref[0, 0, :, 0:width] += jnp.sum(ds, axis=0, keepdims=True)
            ds = (ds * (HEAD_DIM ** -0.5)).astype(BF16)
            dk_ref[0:width, :] += _dot_tn(ds, q)
            dp_ref[0, 0, rows, :] = _dot_nn(ds, k).astype(BF16)

        _by_causal_width(t, S, tile_grads)

        @pl.when(t == nt - 1)
        def _():
            dp_ref[1, 0] = dk_ref[...].astype(BF16)
            dp_ref[2, 0] = dv_ref[...].astype(BF16)

    seq, tile = _head_specs(S, "bht")
    return _call(
        body, name=name, grid=(B, HEADS, nt),
        in_specs=[seq, pl.BlockSpec((1, TQ, cum_col.shape[2]), lambda b, h, t: (b, t, 0)),
                  pl.BlockSpec((1, HEADS, S), lambda b, h, t: (b, 0, 0)), tile, tile],
        out_specs=[seq, pl.BlockSpec((1, 1, 1, S), lambda b, h, t: (b, h, 0, 0))],
        out_shape=[jax.ShapeDtypeStruct(proj.shape, BF16), jax.ShapeDtypeStruct((B, HEADS, 1, S), F32)],
        scratch_shapes=[pltpu.VMEM((S, HEAD_DIM), BF16), pltpu.VMEM((S, HEAD_DIM), BF16),
                        pltpu.VMEM((S, HEAD_DIM), F32), pltpu.VMEM((S, HEAD_DIM), F32)],
        semantics=("parallel", "parallel", "arbitrary"), args=[proj, cum_col, cum_row, o, dgated], ride=ride)


RG_ROWS = 512


def _rg_gates(xc, wa_ref, ba_ref, wx_ref, bx_ref, lam_ref):
    xcb = xc.astype(BF16)
    r = _sigmoid(_dot_nn(xcb, wa_ref[0]) + ba_ref[...])
    i = _sigmoid(_dot_nn(xcb, wx_ref[0]) + bx_ref[...])
    sp = _softplus(-lam_ref[...])
    log_a = (-RG_C * sp) * r
    a = jnp.exp(log_a)
    m = jnp.sqrt(-jnp.tanh(log_a) * (a * a + 1.0))
    return xcb, r, i, sp, a, m


def _rg_specs(B, S, rows, order):
    nc = S // rows

    def idx(fn):
        def index_map(*ids):
            v = dict(zip(order.lower(), ids))
            c = (nc - 1 - v["c"]) if "C" in order else v["c"]
            return fn(v["b"], v["d"], c)
        return index_map

    return dict(
        proj=pl.BlockSpec((2, 1, rows, RG_COLS), idx(lambda b, d, c: (0, b, c, d))),
        act=pl.BlockSpec((1, rows, RG_COLS), idx(lambda b, d, c: (b, c, d))),
        taps=pl.BlockSpec((CONV_WIDTH, RG_COLS), idx(lambda b, d, c: (0, d))),
        vec=pl.BlockSpec((1, RG_COLS), idx(lambda b, d, c: (0, d))),
        gate=pl.BlockSpec((1, RG_COLS, RG_COLS), idx(lambda b, d, c: (d, 0, 0))),
    )


def rglru_fwd(proj, conv_w, conv_b, wa, ba, wx, bx, lam, name, rows=RG_ROWS, ride=None):
    _, B, S, _ = proj.shape
    rows = min(rows, S)
    sp_ = _rg_specs(B, S, rows, "bdc")

    def body(p_ref, cw_ref, cb_ref, wa_ref, ba_ref, wx_ref, bx_ref, lam_ref,
             xc_ref, hs_ref, hp_ref, gt_ref, ext_ref, a_ref, u_ref, xcar_ref, hcar_ref):
        @pl.when(pl.program_id(2) == 0)
        def _():
            xcar_ref[...] = jnp.zeros_like(xcar_ref)
            hcar_ref[...] = jnp.zeros_like(hcar_ref)

        xr = p_ref[0, 0]
        ext_ref[0:8, :] = xcar_ref[...]
        ext_ref[8:, :] = xr
        xcar_ref[...] = xr[rows - 8:, :]
        xc = ext_ref[pl.ds(5, rows), :] * cw_ref[0:1, :]
        xc = xc + ext_ref[pl.ds(6, rows), :] * cw_ref[1:2, :]
        xc = xc + ext_ref[pl.ds(7, rows), :] * cw_ref[2:3, :]
        xc = xc + xr * cw_ref[3:4, :] + cb_ref[...]
        xc_ref[0] = xc
        _, _, i, _, a, m = _rg_gates(xc, wa_ref, ba_ref, wx_ref, bx_ref, lam_ref)
        a_ref[...] = a
        u_ref[...] = m * (i * xc)

        def step(r0, carry):
            h = _group_scan(a_ref[pl.ds(r0, 8), :], u_ref[pl.ds(r0, 8), :], carry)
            row = lax.broadcasted_iota(jnp.int32, h.shape, 0)
            hs_ref[0, pl.ds(r0, 8), :] = h
            hp_ref[0, pl.ds(r0, 8), :] = jnp.where(row == 0, carry, pltpu.roll(h, 1, 0))
            return h[7:8, :]

        hcar_ref[0:1, :] = _scan_rows(rows, step, hcar_ref[0:1, :])
        gt_ref[0] = (hs_ref[0] * _silu(p_ref[1, 0])).astype(BF16)

    act = jax.ShapeDtypeStruct((B, S, RG_WIDTH), F32)
    return _call(
        body, name=name, grid=(B, RG_GROUPS, S // rows),
        in_specs=[sp_["proj"], sp_["taps"], sp_["vec"], sp_["gate"], sp_["vec"], sp_["gate"], sp_["vec"], sp_["vec"]],
        out_specs=[sp_["act"]] * 4,
        out_shape=[act, act, act, jax.ShapeDtypeStruct((B, S, RG_WIDTH), BF16)],
        scratch_shapes=[pltpu.VMEM((rows + 8, RG_COLS), F32), pltpu.VMEM((rows, RG_COLS), F32),
                        pltpu.VMEM((rows, RG_COLS), F32), pltpu.VMEM((8, RG_COLS), F32), pltpu.VMEM((8, RG_COLS), F32)],
        semantics=("parallel", "parallel", "arbitrary"), args=[proj, conv_w, conv_b, wa, ba, wx, bx, lam], ride=ride)


def rglru_bwd(proj, xc, hs, hprev, dgated, conv_w, wa, ba, wx, bx, lam, name, rows=RG_ROWS, ride=None):
    _, B, S, _ = proj.shape
    rows = min(rows, S)
    sp_ = _rg_specs(B, S, rows, "dbC")

    def body(p_ref, xc_ref, hs_ref, hp_ref, dgt_ref, cw_ref, wa_ref, ba_ref, wx_ref, bx_ref, lam_ref,
             dp_ref, dcw_ref, dcb_ref, dwa_ref, dba_ref, dwx_ref, dbx_ref, dlam_ref,
             ext_ref, c_ref, l_ref, acar_ref, lcar_ref, dcar_ref):
        b_, c_ = pl.program_id(1), pl.program_id(2)

        @pl.when(c_ == 0)
        def _():
            acar_ref[...] = jnp.zeros_like(acar_ref)
            lcar_ref[...] = jnp.zeros_like(lcar_ref)
            dcar_ref[...] = jnp.zeros_like(dcar_ref)

        @pl.when((c_ == 0) & (b_ == 0))
        def _():
            for ref in (dcw_ref, dcb_ref, dwa_ref, dba_ref, dwx_ref, dbx_ref, dlam_ref):
                ref[...] = jnp.zeros_like(ref)

        xr, g = p_ref[0, 0], p_ref[1, 0]
        xc_v = xc_ref[0]
        xcb, r, i, sp, a, m = _rg_gates(xc_v, wa_ref, ba_ref, wx_ref, bx_ref, lam_ref)
        dhs, dg = _gate_bwd(dgt_ref[0], hs_ref[0], g)
        dp_ref[1, 0] = dg.astype(BF16)

        ext_ref[0:rows, :] = a
        ext_ref[rows:, :] = acar_ref[...]
        acar_ref[...] = a[0:8, :]
        c_ref[...] = ext_ref[pl.ds(1, rows), :]
        l_ref[...] = dhs

        def step(r0, carry):
            lam_g = _group_scan(c_ref[pl.ds(r0, 8), :], l_ref[pl.ds(r0, 8), :], carry, reverse=True)
            l_ref[pl.ds(r0, 8), :] = lam_g
            return lam_g[0:1, :]

        lcar_ref[0:1, :] = _scan_rows(rows, step, lcar_ref[0:1, :], reverse=True)
        du = l_ref[...]
        da = du * hp_ref[0]
        dlog_a = da * a - (du * (i * xc_v)) * (a * a / m)
        dr = dlog_a * (-RG_C * sp)
        dsp = jnp.sum(dlog_a * (-RG_C * r), axis=0, keepdims=True)
        dlam_ref[...] += dsp * (-_sigmoid(-lam_ref[...]))
        dpa = dr * (r * (1.0 - r))
        dpx = (du * (m * xc_v)) * (i * (1.0 - i))
        dba_ref[...] += jnp.sum(dpa, axis=0, keepdims=True)
        dbx_ref[...] += jnp.sum(dpx, axis=0, keepdims=True)
        dpa, dpx = dpa.astype(BF16), dpx.astype(BF16)
        dwa_ref[0] += _dot_tn(xcb, dpa)
        dwx_ref[0] += _dot_tn(xcb, dpx)
        dxc = du * (m * i) + _dot_nt(dpa, wa_ref[0]) + _dot_nt(dpx, wx_ref[0])

        dcb_ref[...] += jnp.sum(dxc, axis=0, keepdims=True)
        ext_ref[0:rows, :] = dxc
        ext_ref[rows:, :] = dcar_ref[...]
        dcar_ref[...] = dxc[0:8, :]
        dxr = jnp.zeros_like(dxc)
        for k in range(CONV_WIDTH):
            tap = CONV_WIDTH - 1 - k
            ahead = dxc if k == 0 else ext_ref[pl.ds(k, rows), :]
            dxr = dxr + ahead * cw_ref[tap:tap + 1, :]
            dcw_ref[tap:tap + 1, :] += jnp.sum(xr * ahead, axis=0, keepdims=True)
        dp_ref[0, 0] = dxr.astype(BF16)

    vec = jax.ShapeDtypeStruct((1, RG_WIDTH), F32)
    gate = jax.ShapeDtypeStruct((RG_GROUPS, RG_COLS, RG_COLS), F32)
    return _call(
        body, name=name, grid=(RG_GROUPS, B, S // rows),
        in_specs=[sp_["proj"], sp_["act"], sp_["act"], sp_["act"], sp_["act"], sp_["taps"],
                  sp_["gate"], sp_["vec"], sp_["gate"], sp_["vec"], sp_["vec"]],
        out_specs=[sp_["proj"], sp_["taps"], sp_["vec"], sp_["gate"], sp_["vec"], sp_["gate"], sp_["vec"], sp_["vec"]],
        out_shape=[jax.ShapeDtypeStruct(proj.shape, BF16), jax.ShapeDtypeStruct((CONV_WIDTH, RG_WIDTH), F32), vec,
                   gate, vec, gate, vec, vec],
        scratch_shapes=[pltpu.VMEM((rows + 8, RG_COLS), F32), pltpu.VMEM((rows, RG_COLS), F32),
                        pltpu.VMEM((rows, RG_COLS), F32), pltpu.VMEM((8, RG_COLS), F32),
                        pltpu.VMEM((8, RG_COLS), F32), pltpu.VMEM((8, RG_COLS), F32)],
        semantics=("arbitrary", "arbitrary", "arbitrary"),
        args=[proj, xc, hs, hprev, dgated, conv_w, wa, ba, wx, bx, lam], ride=ride)


def block_diag_gates(w):
    per = RG_COLS // RG_BLOCK
    w4 = w.reshape(RG_GROUPS, per, RG_BLOCK, RG_BLOCK)
    return jnp.einsum("dipq,ij->dipjq", w4, jnp.eye(per, dtype=w.dtype)).reshape(RG_GROUPS, RG_COLS, RG_COLS)


def block_diag_gates_t(dw):
    per = RG_COLS // RG_BLOCK
    dw6 = dw.reshape(RG_GROUPS, per, RG_BLOCK, per, RG_BLOCK)
    return jnp.stack([dw6[:, i, :, i, :] for i in range(per)], axis=1).reshape(RG_BLOCKS, RG_BLOCK, RG_BLOCK)


def adamw(w, parts, m, v, name, layer=0, prev=None, row_tile=ROW_TILE):
    L, R, C = w.shape
    n_parts = parts.shape[0]
    br = row_tile if R % row_tile == 0 else R

    def body(w_ref, p_ref, m_ref, v_ref, *refs):
        g_ref, d_ref, nm_ref, nv_ref = refs[-4:]
        g = p_ref[0].astype(F32)
        for k in range(1, n_parts):
            g = g + p_ref[k].astype(F32)
        nm = ADAM_B1 * m_ref[0] + (1.0 - ADAM_B1) * g
        nv = ADAM_B2 * v_ref[0] + (1.0 - ADAM_B2) * (g * g)
        m_hat = nm / (1.0 - ADAM_B1 ** ADAM_STEP)
        v_hat = nv / (1.0 - ADAM_B2 ** ADAM_STEP)
        g_ref[0] = g
        d_ref[0] = -ADAM_LR * (m_hat / (jnp.sqrt(v_hat) + ADAM_EPS) + ADAM_WD * w_ref[0])
        nm_ref[0] = nm
        nv_ref[0] = nv

    slab = pl.BlockSpec((1, br, C), lambda i: (layer, i, 0))
    out = jax.ShapeDtypeStruct((L, R, C), F32)
    carried = [] if prev is None else list(prev)
    return _call(
        body, name=name, grid=(R // br,),
        in_specs=[slab, pl.BlockSpec((n_parts, br, C), lambda i: (0, i, 0)), slab, slab]
        + [pl.BlockSpec(memory_space=pl.ANY)] * len(carried),
        out_specs=[slab] * 4, out_shape=[out] * 4, semantics=("parallel",), args=[w, parts, m, v] + carried,
        aliases={4 + k: k for k in range(len(carried))})


def _seq(a, B):
    return a.reshape(a.shape[:-2] + (B, a.shape[-2] // B, a.shape[-1]))


def _flat(a):
    return a.reshape(a.shape[:-3] + (a.shape[-3] * a.shape[-2], a.shape[-1]))


def mixer_a_fwd(h, w, B, tag, rides):
    proj = matmul(h, w["w_in"], mode="nn", out_dtype=F32, name=f"{tag}_proj", out_slabs=4,
                  ride=rides.pop(f"{tag}_proj", None))
    o, gated = attn_a_fwd(_seq(proj, B), w["bias"], f"{tag}_attn", ride=rides.pop(f"{tag}_attn", None))
    return _flat(gated), dict(proj=proj, o=o)


def mixer_a_bwd(dgated, w, saved, B, tag, rides):
    dproj, dbias = attn_a_bwd(_seq(saved["proj"], B), w["bias"], saved["o"], _seq(dgated, B), f"{tag}_attn_bwd",
                              ride=rides.pop(f"{tag}_attn_bwd", None))
    return _flat(dproj), dict(bias=dbias)


def mixer_b_fwd(h, w, B, tag, rides):
    proj = matmul(h, w["w_in"], mode="nn", out_dtype=F32, name=f"{tag}_proj", out_slabs=2, bn=RG_COLS,
                  ride=rides.pop(f"{tag}_proj", None))
    xc, hs, hprev, gated = rglru_fwd(_seq(proj, B), w["conv_w"], w["conv_b"], w["wa"], w["ba"], w["wx"], w["bx"],
                                     w["lam"], f"{tag}_rglru", ride=rides.pop(f"{tag}_rglru", None))
    return _flat(gated), dict(proj=proj, xc=xc, hs=hs, hprev=hprev)


def mixer_b_bwd(dgated, w, saved, B, tag, rides):
    dproj, dcw, dcb, dwa, dba, dwx, dbx, dlam = rglru_bwd(
        _seq(saved["proj"], B), saved["xc"], saved["hs"], saved["hprev"], _seq(dgated, B),
        w["conv_w"], w["wa"], w["ba"], w["wx"], w["bx"], w["lam"], f"{tag}_rglru_bwd",
        ride=rides.pop(f"{tag}_rglru_bwd", None))
    return _flat(dproj), dict(conv_w=dcw, conv_b=dcb, wa=dwa, ba=dba, wx=dwx, bx=dbx, lam=dlam)


def mixer_c_fwd(h, w, B, tag, rides):
    proj = matmul(h, w["w_in"], mode="nn", out_dtype=F32, name=f"{tag}_proj", out_slabs=4,
                  ride=rides.pop(f"{tag}_proj", None))
    f_logit = matmul(h, w["w_f"], mode="nn", out_dtype=F32, name=f"{tag}_fproj")
    cum = fox_cum_fwd(_seq(f_logit, B), w["f_bias"], f"{tag}_cum")
    cum_row = cum[:, :, :HEADS].transpose(0, 2, 1)
    o, gated = fox_fwd(_seq(proj, B), cum, cum_row, f"{tag}_attn", ride=rides.pop(f"{tag}_attn", None))
    return _flat(gated), dict(proj=proj, o=o, f_logit=f_logit, cum=cum, cum_row=cum_row)


def mixer_c_bwd(dgated, w, saved, B, tag, rides):
    dproj, dck = fox_bwd(_seq(saved["proj"], B), saved["cum"], saved["cum_row"], saved["o"], _seq(dgated, B),
                         f"{tag}_attn_bwd", ride=rides.pop(f"{tag}_attn_bwd", None))
    S = dck.shape[-1]
    dcum = jnp.pad(-dck.reshape(B, HEADS, S).transpose(0, 2, 1), ((0, 0), (0, 0), (0, HEAD_DIM - HEADS)))
    df, dfb = fox_cum_bwd(_seq(saved["f_logit"], B), w["f_bias"], dcum, f"{tag}_cum_bwd")
    return _flat(dproj), dict(f_bias=dfb, df=_flat(df).astype(BF16))


MIXERS = {"a": (mixer_a_fwd, mixer_a_bwd), "b": (mixer_b_fwd, mixer_b_bwd), "c": (mixer_c_fwd, mixer_c_bwd)}
LAYER_KINDS = "abca"


def local_step(x, target, norm_pre, norm_post, get_layer, rides, on_grads):
    B, S, Dm = x.shape
    n_layers = len(LAYER_KINDS)
    xs = [x.reshape(B * S, Dm)]
    saved, layers = [], []
    for li, kind in enumerate(LAYER_KINDS):
        tag = f"l{li}{kind}"
        h = prenorm_fwd(xs[-1], norm_pre[li:li + 1], f"{tag}_prenorm")
        w = get_layer(li)
        gated, sv = MIXERS[kind][0](h, w, B, tag, rides)
        y = matmul(gated, w["w_out"], mode="nn", out_dtype=F32, name=f"{tag}_out", ride=rides.pop(f"{tag}_out", None))
        xs.append(postnorm_fwd(xs[-1], y, norm_post[li:li + 1], f"{tag}_postnorm"))
        saved.append(dict(sv, h=h, gated=gated, y=y))
        layers.append(w)
    loss, dx = loss_fwd_bwd(xs[-1], target.reshape(B * S, Dm), "loss")

    for li in reversed(range(n_layers)):
        kind, w, sv = LAYER_KINDS[li], layers[li], saved[li]
        tag = f"l{li}{kind}"
        dy, dg_post = postnorm_bwd(sv["y"], norm_post[li:li + 1], dx, f"{tag}_postnorm_bwd")
        on_grads(li, "norm_post", dg_post)
        on_grads(li, "w_out", matmul(sv["gated"], dy, mode="tn", out_dtype=BF16, name=f"{tag}_dwout", bk=1024,
                                     ride=rides.pop(f"{tag}_dwout", None)))
        dgated = matmul(dy, w["w_out"], mode="nt", out_dtype=F32, name=f"{tag}_dgated", bn=512,
                        ride=rides.pop(f"{tag}_dgated", None))
        dproj, gw = MIXERS[kind][1](dgated, w, sv, B, tag, rides)
        df = gw.pop("df", None)
        for name, value in gw.items():
            on_grads(li, name, value)
        on_grads(li, "w_in", matmul(sv["h"], dproj, mode="tn", out_dtype=BF16, name=f"{tag}_dwin", bk=1024,
                                    out_slabs=w["grad_slabs"], ride=rides.pop(f"{tag}_dwin", None)))
        dhs = [matmul(dproj, w["w_in"], mode="nt", out_dtype=F32, name=f"{tag}_dh", bn=1024, bk=1024,
                      ride=rides.pop(f"{tag}_dh", None))]
        if df is not None:
            on_grads(li, "w_f", matmul(sv["h"], df, mode="tn", out_dtype=BF16, name=f"{tag}_dwf", bk=1024))
            dhs.append(matmul(df, w["w_f"], mode="nt", out_dtype=F32, name=f"{tag}_dhf", bn=1024))
        dx, dg_pre = prenorm_bwd(xs[li], norm_pre[li:li + 1], dhs, dx, f"{tag}_prenorm_bwd")
        on_grads(li, "norm_pre", dg_pre)
    assert not rides, list(rides)
    return loss, dx.reshape(B, S, Dm)


WEIGHTS = ("norm_pre", "norm_post", "a_w_in", "a_rel_bias", "a_w_out", "b_w_in", "b_conv_w", "b_conv_b",
           "b_gate_a_w", "b_gate_a_b", "b_gate_x_w", "b_gate_x_b", "b_lambda", "b_w_out", "c_w_in", "c_f_bias",
           "c_w_out")
C_SHARD = (4 * D_MODEL + HEADS) // N_DEV


def _rows(gathered):
    return gathered.reshape(gathered.shape[0] * gathered.shape[1], gathered.shape[2])


def layer_a(w_in, w_out, rel_bias):
    return dict(w_in=w_in, w_out=_rows(w_out), bias=band_bias(rel_bias), grad_slabs=N_DEV)


def layer_b(w_in, w_out, conv_w, small):
    return dict(
        w_in=w_in, w_out=_rows(w_out), grad_slabs=N_DEV,
        conv_w=conv_w.transpose(1, 0, 2).reshape(CONV_WIDTH, RG_WIDTH),
        conv_b=small["b_conv_b"], lam=small["b_lambda"],
        wa=block_diag_gates(small["b_gate_a_w"][0]).astype(BF16), ba=small["b_gate_a_b"].reshape(1, RG_WIDTH),
        wx=block_diag_gates(small["b_gate_x_w"][0]).astype(BF16), bx=small["b_gate_x_b"].reshape(1, RG_WIDTH))


def layer_c(w_in, w_out, small):
    full = w_in.transpose(1, 0, 2).reshape(D_MODEL, N_DEV * C_SHARD)
    return dict(w_in=full[:, :4 * D_MODEL], w_f=jnp.pad(full[:, 4 * D_MODEL:], ((0, 0), (0, HEAD_DIM - HEADS))),
                w_out=_rows(w_out), grad_slabs=1,
                f_bias=jnp.pad(small["c_f_bias"], ((0, 0), (0, HEAD_DIM - HEADS))))


def c_w_in_blocks(dmain, df):
    full = jnp.concatenate([dmain, df[:, :HEADS].astype(dmain.dtype)], axis=1)
    return full.reshape(D_MODEL, N_DEV, C_SHARD).transpose(1, 0, 2)


def _row_blocks(g):
    return g.reshape(N_DEV, g.shape[0] // N_DEV, g.shape[1])


PACK_LANES = 128
PACK_ALIGN = 8 * PACK_LANES


def pack(parts):
    flat = []
    for p in parts:
        n = p.size
        flat.append(jnp.pad(p.reshape(n), (0, -n % PACK_ALIGN)).reshape(-1, PACK_LANES))
    rows = sum(f.shape[0] for f in flat)
    flat.append(jnp.zeros((-rows % ROW_TILE, PACK_LANES), F32))
    return jnp.concatenate(flat, axis=0)


def unpack(packed, shapes):
    out, row = [], 0
    for shape in shapes:
        n = 1
        for s in shape:
            n *= s
        n_rows = (n + PACK_ALIGN - 1) // PACK_ALIGN * 8
        out.append(packed[row:row + n_rows].reshape(-1)[:n].reshape(shape))
        row += n_rows
    return out


LATE = (("a_rel_bias", slice(0, 1)), ("norm_pre", slice(0, 1)), ("norm_post", slice(0, 1)))
EARLY = (("a_rel_bias", slice(1, 2)), ("norm_pre", slice(1, 4)), ("norm_post", slice(1, 4)),
         ("b_conv_b", slice(None)), ("b_gate_a_w", slice(None)), ("b_gate_a_b", slice(None)),
         ("b_gate_x_w", slice(None)), ("b_gate_x_b", slice(None)), ("b_lambda", slice(None)),
         ("c_f_bias", slice(None)))


def _pieces(tree, pieces):
    return [tree[name][sl] for name, sl in pieces]


def kernel(x, norm_pre, norm_post, a_w_in, a_rel_bias, a_w_out, b_w_in, b_conv_w, b_conv_b, b_gate_a_w, b_gate_a_b, b_gate_x_w, b_gate_x_b, b_lambda, b_w_out, c_w_in, c_f_bias, c_w_out, loss_target, m_norm_pre, m_norm_post, m_a_w_in, m_a_rel_bias, m_a_w_out, m_b_w_in, m_b_conv_w, m_b_conv_b, m_b_gate_a_w, m_b_gate_a_b, m_b_gate_x_w, m_b_gate_x_b, m_b_lambda, m_b_w_out, m_c_w_in, m_c_f_bias, m_c_w_out, v_norm_pre, v_norm_post, v_a_w_in, v_a_rel_bias, v_a_w_out, v_b_w_in, v_b_conv_w, v_b_conv_b, v_b_gate_a_w, v_b_gate_a_b, v_b_gate_x_w, v_b_gate_x_b, v_b_lambda, v_b_w_out, v_c_w_in, v_c_f_bias, v_c_w_out):
    args = dict(locals())
    w = {n: args[n] for n in WEIGHTS}
    m = {n: args["m_" + n] for n in WEIGHTS}
    v = {n: args["v_" + n] for n in WEIGHTS}

    a_in, a_out = a_w_in.astype(BF16), a_w_out.astype(BF16)
    gather_a0 = Ride([a_in[0], a_out[0]], scatter=False)
    gather_b = Ride([b_w_in[0].astype(BF16), b_w_out[0].astype(BF16), b_conv_w[0]], scatter=False)
    gather_c_in = Ride([c_w_in[0].astype(BF16)], scatter=False)
    gather_c_out = Ride([c_w_out[0].astype(BF16)], scatter=False)
    gather_a1 = Ride([a_in[1], a_out[1]], scatter=False)
    exchange(gather_a0, "gather_l0")
    rides = {"l0a_proj": gather_b, "l0a_attn": gather_c_in, "l1b_rglru": gather_c_out, "l2c_attn": gather_a1}

    def get_layer(li):
        if li == 0:
            return layer_a(*gather_a0.out, a_rel_bias[0])
        if li == 1:
            return layer_b(*gather_b.out, w)
        if li == 2:
            return layer_c(gather_c_in.out[0], gather_c_out.out[0], w)
        return layer_a(*gather_a1.out, a_rel_bias[1])

    grads = [dict() for _ in LAYER_KINDS]
    scatters = {}

    def rel_bias_grad(j, dbias):
        return jax.vjp(band_bias, a_rel_bias[j])[1](dbias)[0][None]

    def early_partial():
        gb, gc = grads[1], grads[2]
        tree = dict(
            a_rel_bias=jnp.concatenate([jnp.zeros((1, HEADS, N_REL), F32), rel_bias_grad(1, grads[3]["bias"])]),
            norm_pre=jnp.concatenate([jnp.zeros((1, D_MODEL), F32)] + [grads[li]["norm_pre"] for li in (1, 2, 3)]),
            norm_post=jnp.concatenate([jnp.zeros((1, D_MODEL), F32)] + [grads[li]["norm_post"] for li in (1, 2, 3)]),
            b_conv_b=gb["conv_b"], b_lambda=gb["lam"],
            b_gate_a_w=block_diag_gates_t(gb["wa"])[None], b_gate_a_b=gb["ba"].reshape(1, RG_BLOCKS, RG_BLOCK),
            b_gate_x_w=block_diag_gates_t(gb["wx"])[None], b_gate_x_b=gb["bx"].reshape(1, RG_BLOCKS, RG_BLOCK),
            c_f_bias=gc["f_bias"][:, :HEADS])
        return pack(_pieces(tree, EARLY))

    def on_grads(li, name, value):
        g = grads[li]
        g[name] = value
        if (li, name) == (3, "w_in"):
            scatters["a1"] = rides["l2c_attn_bwd"] = Ride([g["w_in"], _row_blocks(g["w_out"])], scatter=True)
        elif (li, name) == (2, "w_f"):
            scatters["c"] = rides["l1b_rglru_bwd"] = Ride(
                [c_w_in_blocks(g["w_in"], g["w_f"]), _row_blocks(g["w_out"])], scatter=True)
        elif (li, name) == (1, "w_in"):
            conv = g["conv_w"].reshape(CONV_WIDTH, N_DEV, RG_WIDTH // N_DEV).transpose(1, 0, 2)
            scatters["b"] = rides["l0a_attn_bwd"] = Ride([g["w_in"], _row_blocks(g["w_out"]), conv], scatter=True)
        elif (li, name) == (1, "norm_pre"):
            scatters["early"] = rides["l0a_dwin"] = Ride([early_partial()], scatter=False)
        elif (li, name) == (0, "w_out"):
            scatters["a0_out"] = rides["l0a_dgated"] = Ride([_row_blocks(value)], scatter=True)
        elif (li, name) == (0, "w_in"):
            scatters["a0_in"] = rides["l0a_dh"] = Ride([value], scatter=True)

    loss, grad_x = local_step(x, loss_target, norm_pre, norm_post, get_layer, rides, on_grads)
    late_tree = dict(a_rel_bias=rel_bias_grad(0, grads[0]["bias"]), norm_pre=grads[0]["norm_pre"],
                     norm_post=grads[0]["norm_post"])
    late_parts = exchange(Ride([pack([late_tree[n] for n, _ in LATE])], scatter=False), "gather_late_grads")[0]

    def sharded(name, layer_parts):
        outs = None
        for j, parts in enumerate(layer_parts):
            outs = adamw(w[name], parts, m[name], v[name], f"adamw_{name}_{j}", layer=j, prev=outs)
        return outs

    res = dict(
        a_w_in=sharded("a_w_in", [scatters["a0_in"].out[0], scatters["a1"].out[0]]),
        a_w_out=sharded("a_w_out", [scatters["a0_out"].out[0], scatters["a1"].out[1]]),
        b_w_in=sharded("b_w_in", [scatters["b"].out[0]]),
        b_w_out=sharded("b_w_out", [scatters["b"].out[1]]),
        b_conv_w=sharded("b_conv_w", [scatters["b"].out[2]]),
        c_w_in=sharded("c_w_in", [scatters["c"].out[0]]),
        c_w_out=sharded("c_w_out", [scatters["c"].out[1]]))

    packed = {}
    for label, pieces, parts in (("early", EARLY, scatters["early"].out[0]), ("late", LATE, late_parts)):
        outs = adamw(pack(_pieces(w, pieces))[None], parts, pack(_pieces(m, pieces))[None],
                     pack(_pieces(v, pieces))[None], f"adamw_replicated_{label}")
        shapes = [w[n][sl].shape for n, sl in pieces]
        packed[label] = [dict(zip([n for n, _ in pieces], unpack(o[0], shapes))) for o in outs]
    for n in ("b_conv_b", "b_gate_a_w", "b_gate_a_b", "b_gate_x_w", "b_gate_x_b", "b_lambda", "c_f_bias"):
        res[n] = [packed["early"][k][n] for k in range(4)]
    for n in ("a_rel_bias", "norm_pre", "norm_post"):
        res[n] = [jnp.concatenate([packed["late"][k][n], packed["early"][k][n]]) for k in range(4)]

    total = lax.psum(loss[0, 0], ("x", "y", "c"))
    return (total, grad_x, *[res[n][0] for n in WEIGHTS], *[res[n][1] for n in WEIGHTS],
            *[res[n][2] for n in WEIGHTS], *[res[n][3] for n in WEIGHTS])
```

```python
import functools

import jax
import jax.numpy as jnp
from jax import lax
from jax.experimental import pallas as pl
from jax.experimental.pallas import tpu as pltpu

F32 = jnp.float32
BF16 = jnp.bfloat16

N_DEV = 8
D_MODEL = 2048
HEADS = 16
HEAD_DIM = 128
CHUNK = 64
LEFT_CHUNKS = 8
REL_CLIP = 256
N_REL = 2 * REL_CLIP + 1
TQ = 256
A_PAD = LEFT_CHUNKS * CHUNK
A_BAND = A_PAD + CHUNK
RG_WIDTH = 2560
RG_BLOCKS = 16
RG_BLOCK = 160
RG_COLS = 640
RG_GROUPS = RG_WIDTH // RG_COLS
RG_C = 8.0
CONV_WIDTH = 4
RMS_EPS = 1e-6
NEG_INF = -1e30
ADAM_LR = 0.001
ADAM_B1 = 0.9
ADAM_B2 = 0.999
ADAM_EPS = 1e-08
ADAM_WD = 0.01
ADAM_STEP = 10
VMEM_LIMIT = 56 * 1024 * 1024
MESH = pl.DeviceIdType.MESH


def _params(sem, vmem=VMEM_LIMIT):
    return pltpu.CompilerParams(dimension_semantics=sem, vmem_limit_bytes=vmem)


def _sigmoid(x):
    return 1.0 / (1.0 + jnp.exp(-x))


def _log1p(y):
    u = 1.0 + y
    return jnp.where(u == 1.0, y, jnp.log(u) * (y / jnp.where(u == 1.0, 1.0, u - 1.0)))


def _softplus(x):
    return jnp.maximum(x, 0.0) + _log1p(jnp.exp(-jnp.abs(x)))


def _dot(a, b, dims):
    return lax.dot_general(a, b, (dims, ((), ())), preferred_element_type=F32)


def _dot_nn(a, b):
    return _dot(a, b, ((1,), (0,)))


def _dot_nt(a, b):
    return _dot(a, b, ((1,), (1,)))


def _dot_tn(a, b):
    return _dot(a, b, ((0,), (0,)))


def _peers():
    x, y, c = lax.axis_index("x"), lax.axis_index("y"), lax.axis_index("c")
    me = 4 * x + 2 * y + c
    peers = []
    for k in range(1, N_DEV):
        px = 1 - x if k & 4 else x
        py = 1 - y if k & 2 else y
        pc = 1 - c if k & 1 else c
        peers.append(((px, py, pc), 4 * px + 2 * py + pc))
    return me, peers


class Ride:
    def __init__(self, arrs, scatter):
        self.arrs, self.scatter, self.out = list(arrs), scatter, None

    def out_shapes(self):
        return [jax.ShapeDtypeStruct(a.shape if self.scatter else (N_DEV,) + a.shape, a.dtype) for a in self.arrs]

    def sem_shapes(self):
        n = len(self.arrs)
        return [pltpu.SemaphoreType.DMA((n, N_DEV - 1)), pltpu.SemaphoreType.DMA((n, N_DEV - 1)),
                pltpu.SemaphoreType.DMA((n,))]

    def _copies(self, ins, outs, sems, landing):
        send_sems, recv_sems, local_sems = sems
        me, peers = _peers()
        local, remote = [], []
        for a, (src, dst) in enumerate(zip(ins, outs)):
            local.append(pltpu.make_async_copy(src.at[me] if self.scatter else src, dst.at[me], local_sems.at[a]))
            for k, (peer, peer_idx) in enumerate(peers):
                remote.append(pltpu.make_async_remote_copy(
                    src_ref=src.at[peer_idx] if self.scatter else src, dst_ref=dst.at[peer_idx if landing else me],
                    send_sem=send_sems.at[a, k], recv_sem=recv_sems.at[a, k], device_id=peer, device_id_type=MESH))
        return local, remote

    def start(self, ins, outs, sems):
        local, remote = self._copies(ins, outs, sems, landing=False)
        for cp in local + remote:
            cp.start()

    def wait(self, ins, outs, sems):
        local, remote = self._copies(ins, outs, sems, landing=True)
        for cp in remote + local:
            cp.wait()


def _call(body, *, name, grid, in_specs, out_specs, out_shape, args, scratch_shapes=(), semantics=None, ride=None,
          aliases=None):
    scratch_shapes = list(scratch_shapes)
    if ride is None:
        return pl.pallas_call(
            body, name=name, grid=grid, in_specs=in_specs, out_specs=out_specs, out_shape=out_shape,
            scratch_shapes=scratch_shapes, input_output_aliases=aliases or {},
            compiler_params=_params(semantics if grid else None))(*args)
    assert not aliases
    n_in, n_out, n_sc, n_r = len(in_specs), len(out_specs), len(scratch_shapes), len(ride.arrs)

    def riding(*refs):
        ins, r_ins = refs[:n_in], refs[n_in:n_in + n_r]
        outs, r_outs = refs[n_in + n_r:n_in + n_r + n_out], refs[n_in + n_r + n_out:n_in + 2 * n_r + n_out]
        rest = refs[n_in + 2 * n_r + n_out:]
        scratch, sems = rest[:n_sc], rest[n_sc:]
        first = last = None
        for axis, size in enumerate(grid):
            pid = pl.program_id(axis)
            first = (pid == 0) if first is None else first & (pid == 0)
            last = (pid == size - 1) if last is None else last & (pid == size - 1)
        if grid:
            pl.when(first)(lambda: ride.start(r_ins, r_outs, sems))
        else:
            ride.start(r_ins, r_outs, sems)
        body(*ins, *outs, *scratch)
        if grid:
            pl.when(last)(lambda: ride.wait(r_ins, r_outs, sems))
        else:
            ride.wait(r_ins, r_outs, sems)

    any_spec = pl.BlockSpec(memory_space=pl.ANY)
    res = pl.pallas_call(
        riding, name=name, grid=grid, in_specs=list(in_specs) + [any_spec] * n_r,
        out_specs=list(out_specs) + [any_spec] * n_r, out_shape=list(out_shape) + ride.out_shapes(),
        scratch_shapes=scratch_shapes + ride.sem_shapes(),
        compiler_params=_params(("arbitrary",) * len(grid) if grid else None))(*args, *ride.arrs)
    ride.out = list(res[n_out:])
    return list(res[:n_out])


def exchange(ride, name):
    _call(lambda: None, name=name, grid=(), in_specs=[], out_specs=[], out_shape=[], args=[], ride=ride)
    return ride.out


LANES = 128


def _fit(dims, want):
    dims = tuple(dims)
    if len(set(dims)) == 1 and dims[0] <= want:
        return dims[0]
    return max(t for t in range(LANES, want + 1, LANES) if all(d % t == 0 for d in dims))


def _cols(arr):
    return arr.shape[-1] * (arr.shape[0] if len(arr.shape) == 3 else 1)


def _tile_spec(shape, rblk, cblk, rc):
    if len(shape) == 2:
        return pl.BlockSpec((rblk, cblk), rc)
    per = shape[2] // cblk

    def index_map(*ids):
        r, c = rc(*ids)
        return (c // per, r, c % per)

    return pl.BlockSpec((1, rblk, cblk), index_map)


def matmul(a, b, *, mode, out_dtype, name, bm=512, bn=1024, bk=2048, out_slabs=1, ride=None):
    a_rows, a_cols, b_rows, b_cols = a.shape[-2], _cols(a), b.shape[-2], _cols(b)
    (K, M) = (a_rows, a_cols) if mode == "tn" else (a_cols, a_rows)
    N = b_rows if mode == "nt" else b_cols
    assert K == (b_cols if mode == "nt" else b_rows), (name, a.shape, b.shape)
    out_shape = (M, N) if out_slabs == 1 else (out_slabs, M, N // out_slabs)
    widths = dict(m=[M], n=[N, out_shape[-1]], k=[K])
    widths["m" if mode == "tn" else "k"].append(a.shape[-1])
    widths["k" if mode == "nt" else "n"].append(b.shape[-1])
    bm, bn, bk = _fit(widths["m"], bm), _fit(widths["n"], bn), _fit(widths["k"], bk)
    nk = K // bk
    dims = {"nn": ((1,), (0,)), "nt": ((1,), (1,)), "tn": ((0,), (0,))}[mode]

    def val(ref):
        return ref[0] if len(ref.shape) == 3 else ref[...]

    def put(ref, x):
        if len(ref.shape) == 3:
            ref[0] = x.astype(ref.dtype)
        else:
            ref[...] = x.astype(ref.dtype)

    def body(a_ref, b_ref, o_ref, *scratch):
        if nk == 1:
            put(o_ref, _dot(val(a_ref), val(b_ref), dims))
            return
        acc_ref, = scratch
        k = pl.program_id(2)

        @pl.when(k == 0)
        def _():
            acc_ref[...] = jnp.zeros_like(acc_ref)

        acc_ref[...] += _dot(val(a_ref), val(b_ref), dims)

        @pl.when(k == nk - 1)
        def _():
            put(o_ref, acc_ref[...])

    if mode == "tn":
        a_spec = _tile_spec(a.shape, bk, bm, lambda j, i, k: (k, i))
    else:
        a_spec = _tile_spec(a.shape, bm, bk, lambda j, i, k: (i, k))
    if mode == "nt":
        b_spec = _tile_spec(b.shape, bn, bk, lambda j, i, k: (j, k))
    else:
        b_spec = _tile_spec(b.shape, bk, bn, lambda j, i, k: (k, j))
    return _call(
        body, name=name, grid=(N // bn, M // bm, nk), in_specs=[a_spec, b_spec],
        out_specs=[_tile_spec(out_shape, bm, bn, lambda j, i, k: (i, j))],
        out_shape=[jax.ShapeDtypeStruct(out_shape, out_dtype)],
        scratch_shapes=[] if nk == 1 else [pltpu.VMEM((bm, bn), F32)],
        semantics=("parallel", "parallel", "arbitrary"), args=[a, b], ride=ride)[0]


ROW_TILE = 256


def _rms_stats(z):
    r = lax.rsqrt(jnp.mean(z * z, axis=-1, keepdims=True) + RMS_EPS)
    return r, z * r


def _rms_bwd(n, r, g, dout):
    dn = dout * g
    return r * (dn - n * jnp.mean(dn * n, axis=-1, keepdims=True))


def _row_spec(T, Dm):
    bt = min(ROW_TILE, T)
    return bt, pl.BlockSpec((bt, Dm), lambda i: (i, 0)), pl.BlockSpec((1, Dm), lambda i: (0, 0))


def prenorm_fwd(x, g, name):
    T, Dm = x.shape
    bt, row, vec = _row_spec(T, Dm)

    def body(x_ref, g_ref, h_ref):
        _, n = _rms_stats(x_ref[...])
        h_ref[...] = (n * g_ref[...]).astype(BF16)

    return pl.pallas_call(
        body, name=name, grid=(T // bt,), in_specs=[row, vec], out_specs=row,
        out_shape=jax.ShapeDtypeStruct((T, Dm), BF16), compiler_params=_params(("parallel",)),
    )(x, g)


def postnorm_fwd(x, y, g, name):
    T, Dm = x.shape
    bt, row, vec = _row_spec(T, Dm)

    def body(x_ref, y_ref, g_ref, o_ref):
        _, n = _rms_stats(y_ref[...])
        o_ref[...] = x_ref[...] + n * g_ref[...]

    return pl.pallas_call(
        body, name=name, grid=(T // bt,), in_specs=[row, row, vec], out_specs=row,
        out_shape=jax.ShapeDtypeStruct((T, Dm), F32), compiler_params=_params(("parallel",)),
    )(x, y, g)


def loss_fwd_bwd(xf, target, name):
    T, Dm = xf.shape
    bt, row, _ = _row_spec(T, Dm)

    def body(x_ref, t_ref, l_ref, d_ref):
        @pl.when(pl.program_id(0) == 0)
        def _():
            l_ref[...] = jnp.zeros_like(l_ref)

        err = x_ref[...] - t_ref[...]
        per_tok = jnp.mean(err * err, axis=-1, keepdims=True)
        l_ref[...] += 0.5 * jnp.sum(per_tok, axis=0, keepdims=True)
        d_ref[...] = err * (1.0 / Dm)

    return pl.pallas_call(
        body, name=name, grid=(T // bt,), in_specs=[row, row],
        out_specs=[pl.BlockSpec((1, 1), lambda i: (0, 0)), row],
        out_shape=[jax.ShapeDtypeStruct((1, 1), F32), jax.ShapeDtypeStruct((T, Dm), F32)],
        compiler_params=_params(("arbitrary",)),
    )(xf, target)


def postnorm_bwd(y, g, dout, name):
    T, Dm = y.shape
    bt, row, vec = _row_spec(T, Dm)

    def body(y_ref, g_ref, d_ref, dy_ref, dg_ref):
        @pl.when(pl.program_id(0) == 0)
        def _():
            dg_ref[...] = jnp.zeros_like(dg_ref)

        r, n = _rms_stats(y_ref[...])
        dout_v = d_ref[...]
        dg_ref[...] += jnp.sum(dout_v * n, axis=0, keepdims=True)
        dy_ref[...] = _rms_bwd(n, r, g_ref[...], dout_v).astype(BF16)

    return pl.pallas_call(
        body, name=name, grid=(T // bt,), in_specs=[row, vec, row], out_specs=[row, vec],
        out_shape=[jax.ShapeDtypeStruct((T, Dm), BF16), jax.ShapeDtypeStruct((1, Dm), F32)],
        compiler_params=_params(("arbitrary",)),
    )(y, g, dout)


def prenorm_bwd(x, g, dhs, dres, name):
    T, Dm = x.shape
    bt, row, vec = _row_spec(T, Dm)
    n_dh = len(dhs)

    def body(x_ref, g_ref, *refs):
        dh_refs, (dr_ref, dx_ref, dg_ref) = refs[:n_dh], refs[n_dh:]

        @pl.when(pl.program_id(0) == 0)
        def _():
            dg_ref[...] = jnp.zeros_like(dg_ref)

        r, n = _rms_stats(x_ref[...])
        dh_v = dh_refs[0][...]
        for extra in dh_refs[1:]:
            dh_v = dh_v + extra[...]
        dg_ref[...] += jnp.sum(dh_v * n, axis=0, keepdims=True)
        dx_ref[...] = dr_ref[...] + _rms_bwd(n, r, g_ref[...], dh_v)

    return pl.pallas_call(
        body, name=name, grid=(T // bt,), in_specs=[row, vec] + [row] * (n_dh + 1), out_specs=[row, vec],
        out_shape=[jax.ShapeDtypeStruct((T, Dm), F32), jax.ShapeDtypeStruct((1, Dm), F32)],
        compiler_params=_params(("arbitrary",)),
    )(x, g, *dhs, dres)


def _silu(g):
    return g * _sigmoid(g)


def _gate_bwd(dgated, core, g):
    sg = _sigmoid(g)
    return dgated * (g * sg), dgated * core * (sg * (1.0 + g * (1.0 - sg)))


def _softmax_rows(s):
    e = jnp.exp(s - jnp.max(s, axis=-1, keepdims=True))
    return e * (1.0 / jnp.sum(e, axis=-1, keepdims=True))


def _band_scores(q, kw, bias, r0):
    s = _dot_nt(q, kw) * (HEAD_DIM ** -0.5) + bias
    j = lax.broadcasted_iota(jnp.int32, s.shape, 1)
    return jnp.where(j >= A_PAD - r0, s, NEG_INF)


def _fill_padded_kv(p_ref, kp_ref, vp_ref):
    zeros = jnp.zeros((A_PAD, HEAD_DIM), BF16)
    kp_ref[0:A_PAD, :] = zeros
    vp_ref[0:A_PAD, :] = zeros
    kp_ref[A_PAD:, :] = p_ref[1, 0].astype(BF16)
    vp_ref[A_PAD:, :] = p_ref[2, 0].astype(BF16)


def _head_specs(S, order):
    def idx(fn):
        return lambda *ids: fn(**dict(zip(order, ids)))

    return (pl.BlockSpec((4, 1, S, HEAD_DIM), idx(lambda b, h, t: (0, b, 0, h))),
            pl.BlockSpec((1, TQ, HEAD_DIM), idx(lambda b, h, t: (b, t, h))))


def attn_a_fwd(proj, bias, name, ride=None):
    _, B, S, W = proj.shape
    nt = S // TQ

    def body(p_ref, b_ref, o_ref, gt_ref, kp_ref, vp_ref):
        t = pl.program_id(2)

        @pl.when(t == 0)
        def _():
            _fill_padded_kv(p_ref, kp_ref, vp_ref)

        for c in range(TQ // CHUNK):
            r0 = pl.multiple_of(t * TQ + c * CHUNK, CHUNK)
            rows, band, part = pl.ds(r0, CHUNK), pl.ds(r0, A_BAND), slice(c * CHUNK, (c + 1) * CHUNK)
            q = p_ref[0, 0, rows, :].astype(BF16)
            p = _softmax_rows(_band_scores(q, kp_ref[band, :], b_ref[0], r0))
            o = _dot_nn(p.astype(BF16), vp_ref[band, :])
            o_ref[0, part, :] = o
            gt_ref[0, part, :] = (o * _silu(p_ref[3, 0, rows, :])).astype(BF16)

    seq, tile = _head_specs(S, "bht")
    return _call(
        body, name=name, grid=(B, HEADS, nt),
        in_specs=[seq, pl.BlockSpec((1, CHUNK, A_BAND), lambda b, h, t: (h, 0, 0))], out_specs=[tile, tile],
        out_shape=[jax.ShapeDtypeStruct((B, S, W), F32), jax.ShapeDtypeStruct((B, S, W), BF16)],
        scratch_shapes=[pltpu.VMEM((A_PAD + S, HEAD_DIM), BF16), pltpu.VMEM((A_PAD + S, HEAD_DIM), BF16)],
        semantics=("parallel", "parallel", "arbitrary"), args=[proj, bias], ride=ride)


def attn_a_bwd(proj, bias, o, dgated, name, ride=None):
    _, B, S, W = proj.shape
    nt = S // TQ

    def body(p_ref, b_ref, o_ref, dgt_ref, dp_ref, db_ref, kp_ref, vp_ref, dk_ref, dv_ref):
        b_, t = pl.program_id(1), pl.program_id(2)

        @pl.when(t == 0)
        def _():
            _fill_padded_kv(p_ref, kp_ref, vp_ref)
            dk_ref[...] = jnp.zeros_like(dk_ref)
            dv_ref[...] = jnp.zeros_like(dv_ref)

        @pl.when((t == 0) & (b_ == 0))
        def _():
            db_ref[...] = jnp.zeros_like(db_ref)

        for c in range(TQ // CHUNK):
            r0 = pl.multiple_of(t * TQ + c * CHUNK, CHUNK)
            rows, band, part = pl.ds(r0, CHUNK), pl.ds(r0, A_BAND), slice(c * CHUNK, (c + 1) * CHUNK)
            q = p_ref[0, 0, rows, :].astype(BF16)
            kw, vw = kp_ref[band, :], vp_ref[band, :]
            p = _softmax_rows(_band_scores(q, kw, b_ref[0], r0))
            do, dg = _gate_bwd(dgt_ref[0, part, :], o_ref[0, part, :], p_ref[3, 0, rows, :])
            do = do.astype(BF16)
            dv_ref[band, :] += _dot_tn(p.astype(BF16), do)
            dpr = _dot_nt(do, vw)
            ds = p * (dpr - jnp.sum(p * dpr, axis=-1, keepdims=True))
            db_ref[0] += ds
            ds = (ds * (HEAD_DIM ** -0.5)).astype(BF16)
            dk_ref[band, :] += _dot_tn(ds, q)
            dp_ref[0, 0, rows, :] = _dot_nn(ds, kw).astype(BF16)
            dp_ref[3, 0, rows, :] = dg.astype(BF16)

        @pl.when(t == nt - 1)
        def _():
            dp_ref[1, 0] = dk_ref[A_PAD:, :].astype(BF16)
            dp_ref[2, 0] = dv_ref[A_PAD:, :].astype(BF16)

    seq, tile = _head_specs(S, "hbt")
    bias_spec = pl.BlockSpec((1, CHUNK, A_BAND), lambda h, b, t: (h, 0, 0))
    return _call(
        body, name=name, grid=(HEADS, B, nt), in_specs=[seq, bias_spec, tile, tile], out_specs=[seq, bias_spec],
        out_shape=[jax.ShapeDtypeStruct(proj.shape, BF16), jax.ShapeDtypeStruct(bias.shape, F32)],
        scratch_shapes=[pltpu.VMEM((A_PAD + S, HEAD_DIM), BF16), pltpu.VMEM((A_PAD + S, HEAD_DIM), BF16),
                        pltpu.VMEM((A_PAD + S, HEAD_DIM), F32), pltpu.VMEM((A_PAD + S, HEAD_DIM), F32)],
        semantics=("arbitrary", "arbitrary", "arbitrary"), args=[proj, bias, o, dgated], ride=ride)


def band_bias(rel_bias):
    length = CHUNK + A_BAND - 1
    first = REL_CLIP + 1 - CHUNK
    gen = jnp.concatenate([rel_bias[:, first:],
                           jnp.broadcast_to(rel_bias[:, 2 * REL_CLIP:], (HEADS, length - (N_REL - first)))], axis=1)
    rev = jnp.concatenate([gen[:, ::-1], jnp.zeros((HEADS, 1), rel_bias.dtype)], axis=1)
    sheared = jnp.tile(rev, (1, CHUNK))[:, :CHUNK * length].reshape(HEADS, CHUNK, length)
    return sheared[:, :, CHUNK - 1:]


def _group_scan(a, u, carry, reverse=False):
    row = lax.broadcasted_iota(jnp.int32, u.shape, 0)
    for k in (1, 2, 4):
        shift = 8 - k if reverse else k
        valid = (row < 8 - k) if reverse else (row >= k)
        u_sh = pltpu.roll(u, shift, 0)
        if a is None:
            u = jnp.where(valid, u + u_sh, u)
        else:
            a_sh = pltpu.roll(a, shift, 0)
            u = jnp.where(valid, a * u_sh + u, u)
            a = jnp.where(valid, a * a_sh, a)
    return (u + carry) if a is None else (a * carry + u)


def _scan_rows(n_rows, step, carry0, reverse=False):
    groups = n_rows // 8

    def loop(i, carry):
        gi = (groups - 1 - i) if reverse else i
        return step(pl.multiple_of(gi * 8, 8), carry)

    return lax.fori_loop(0, groups, loop, carry0)


def fox_cum_fwd(f_logit, f_bias, name):
    B, S, L = f_logit.shape

    def body(f_ref, b_ref, c_ref):
        z = f_ref[0] + b_ref[...]
        c_ref[0] = jnp.minimum(z, 0.0) - _log1p(jnp.exp(-jnp.abs(z)))

        def step(r0, carry):
            h = _group_scan(None, c_ref[0, pl.ds(r0, 8), :], carry)
            c_ref[0, pl.ds(r0, 8), :] = h
            return h[7:8, :]

        _scan_rows(S, step, jnp.zeros((1, L), F32))

    return pl.pallas_call(
        body, name=name, grid=(B,),
        in_specs=[pl.BlockSpec((1, S, L), lambda b: (b, 0, 0)), pl.BlockSpec((1, L), lambda b: (0, 0))],
        out_specs=pl.BlockSpec((1, S, L), lambda b: (b, 0, 0)),
        out_shape=jax.ShapeDtypeStruct((B, S, L), F32), compiler_params=_params(("parallel",)),
    )(f_logit, f_bias)


def fox_cum_bwd(f_logit, f_bias, dcum, name):
    B, S, L = f_logit.shape

    def body(f_ref, b_ref, d_ref, df_ref, db_ref):
        @pl.when(pl.program_id(0) == 0)
        def _():
            db_ref[...] = jnp.zeros_like(db_ref)

        def step(r0, carry):
            h = _group_scan(None, d_ref[0, pl.ds(r0, 8), :], carry, reverse=True)
            df_ref[0, pl.ds(r0, 8), :] = h
            return h[0:1, :]

        _scan_rows(S, step, jnp.zeros((1, L), F32), reverse=True)
        df = df_ref[0] * _sigmoid(-(f_ref[0] + b_ref[...]))
        df_ref[0] = df
        db_ref[...] += jnp.sum(df, axis=0, keepdims=True)

    seq = pl.BlockSpec((1, S, L), lambda b: (b, 0, 0))
    vec = pl.BlockSpec((1, L), lambda b: (0, 0))
    return pl.pallas_call(
        body, name=name, grid=(B,), in_specs=[seq, vec, seq], out_specs=[seq, vec],
        out_shape=[jax.ShapeDtypeStruct((B, S, L), F32), jax.ShapeDtypeStruct((1, L), F32)],
        compiler_params=_params(("arbitrary",)),
    )(f_logit, f_bias, dcum)


def _fox_scores(q, k, cc, cr, h, r0):
    lane = lax.broadcasted_iota(jnp.int32, cc.shape, 1)
    cq = jnp.sum(jnp.where(lane == h, cc, 0.0), axis=1, keepdims=True)
    sub = lax.broadcasted_iota(jnp.int32, cr.shape, 0)
    ck = jnp.sum(jnp.where(sub == h, cr, 0.0), axis=0, keepdims=True)
    s = _dot_nt(q, k) * (HEAD_DIM ** -0.5) + (cq - ck)
    qpos = r0 + lax.broadcasted_iota(jnp.int32, s.shape, 0)
    kpos = lax.broadcasted_iota(jnp.int32, s.shape, 1)
    return jnp.where(kpos <= qpos, s, NEG_INF)


KEY_STEP = 512


def _by_causal_width(t, S, fn):
    per = KEY_STEP // TQ
    for c in range(S // KEY_STEP):
        pl.when(t // per == c)(functools.partial(fn, (c + 1) * KEY_STEP))


def fox_fwd(proj, cum_col, cum_row, name, ride=None):
    _, B, S, W = proj.shape
    nt = S // TQ

    def body(p_ref, cc_ref, cr_ref, o_ref, gt_ref, k_ref, v_ref):
        h, t = pl.program_id(1), pl.program_id(2)

        @pl.when(t == 0)
        def _():
            k_ref[...] = p_ref[1, 0].astype(BF16)
            v_ref[...] = p_ref[2, 0].astype(BF16)

        r0 = pl.multiple_of(t * TQ, TQ)
        q = p_ref[0, 0, pl.ds(r0, TQ), :].astype(BF16)
        g = p_ref[3, 0, pl.ds(r0, TQ), :]

        def tile_out(width):
            p = _softmax_rows(_fox_scores(q, k_ref[0:width, :], cc_ref[0], cr_ref[0, :, 0:width], h, r0))
            o = _dot_nn(p.astype(BF16), v_ref[0:width, :])
            o_ref[0] = o
            gt_ref[0] = (o * _silu(g)).astype(BF16)

        _by_causal_width(t, S, tile_out)

    seq, tile = _head_specs(S, "bht")
    return _call(
        body, name=name, grid=(B, HEADS, nt),
        in_specs=[seq, pl.BlockSpec((1, TQ, cum_col.shape[2]), lambda b, h, t: (b, t, 0)),
                  pl.BlockSpec((1, HEADS, S), lambda b, h, t: (b, 0, 0))],
        out_specs=[tile, tile],
        out_shape=[jax.ShapeDtypeStruct((B, S, W), F32), jax.ShapeDtypeStruct((B, S, W), BF16)],
        scratch_shapes=[pltpu.VMEM((S, HEAD_DIM), BF16), pltpu.VMEM((S, HEAD_DIM), BF16)],
        semantics=("parallel", "parallel", "arbitrary"), args=[proj, cum_col, cum_row], ride=ride)


def fox_bwd(proj, cum_col, cum_row, o, dgated, name, ride=None):
    _, B, S, W = proj.shape
    nt = S // TQ

    def body(p_ref, cc_ref, cr_ref, o_ref, dgt_ref, dp_ref, dc_ref, k_ref, v_ref, dk_ref, dv_ref):
        h, t = pl.program_id(1), pl.program_id(2)

        @pl.when(t == 0)
        def _():
            k_ref[...] = p_ref[1, 0].astype(BF16)
            v_ref[...] = p_ref[2, 0].astype(BF16)
            dk_ref[...] = jnp.zeros_like(dk_ref)
            dv_ref[...] = jnp.zeros_like(dv_ref)
            dc_ref[...] = jnp.zeros_like(dc_ref)

        r0 = pl.multiple_of(t * TQ, TQ)
        rows = pl.ds(r0, TQ)
        q = p_ref[0, 0, rows, :].astype(BF16)
        g = p_ref[3, 0, rows, :]
        do, dg = _gate_bwd(dgt_ref[0], o_ref[0], g)
        do = do.astype(BF16)
        dp_ref[3, 0, rows, :] = dg.astype(BF16)

        def tile_grads(width):
            k, v = k_ref[0:width, :], v_ref[0:width, :]
            p = _softmax_rows(_fox_scores(q, k, cc_ref[0], cr_ref[0, :, 0:width], h, r0))
            dv_ref[0:width, :] += _dot_tn(p.astype(BF16), do)
            dpr = _dot_nt(do, v)
            ds = p * (dpr - jnp.sum(p * dpr, axis=-1, keepdims=True))
            dc_ref[0, 0, :, 0:width] += jnp.sum(ds, axis=0, keepdims=True)
            ds = (ds * (HEAD_DIM ** -0.5)).astype(BF16)
            dk_ref[0:width, :] += _dot_tn(ds, q)
            dp_ref[0, 0, rows, :] = _dot_nn(ds, k).astype(BF16)

        _by_causal_width(t, S, tile_grads)

        @pl.when(t == nt - 1)
        def _():
            dp_ref[1, 0] = dk_ref[...].astype(BF16)
            dp_ref[2, 0] = dv_ref[...].astype(BF16)

    seq, tile = _head_specs(S, "bht")
    return _call(
        body, name=name, grid=(B, HEADS, nt),
        in_specs=[seq, pl.BlockSpec((1, TQ, cum_col.shape[2]), lambda b, h, t: (b, t, 0)),
                  pl.BlockSpec((1, HEADS, S), lambda b, h, t: (b, 0, 0)), tile, tile],
        out_specs=[seq, pl.BlockSpec((1, 1, 1, S), lambda b, h, t: (b, h, 0, 0))],
        out_shape=[jax.ShapeDtypeStruct(proj.shape, BF16), jax.ShapeDtypeStruct((B, HEADS, 1, S), F32)],
        scratch_shapes=[pltpu.VMEM((S, HEAD_DIM), BF16), pltpu.VMEM((S, HEAD_DIM), BF16),
                        pltpu.VMEM((S, HEAD_DIM), F32), pltpu.VMEM((S, HEAD_DIM), F32)],
        semantics=("parallel", "parallel", "arbitrary"), args=[proj, cum_col, cum_row, o, dgated], ride=ride)


RG_ROWS = 512


def _rg_gates(xc, wa_ref, ba_ref, wx_ref, bx_ref, lam_ref):
    xcb = xc.astype(BF16)
    r = _sigmoid(_dot_nn(xcb, wa_ref[0]) + ba_ref[...])
    i = _sigmoid(_dot_nn(xcb, wx_ref[0]) + bx_ref[...])
    sp = _softplus(-lam_ref[...])
    log_a = (-RG_C * sp) * r
    a = jnp.exp(log_a)
    m = jnp.sqrt(-jnp.tanh(log_a) * (a * a + 1.0))
    return xcb, r, i, sp, a, m


def _rg_specs(B, S, rows, order):
    nc = S // rows

    def idx(fn):
        def index_map(*ids):
            v = dict(zip(order.lower(), ids))
            c = (nc - 1 - v["c"]) if "C" in order else v["c"]
            return fn(v["b"], v["d"], c)
        return index_map

    return dict(
        proj=pl.BlockSpec((2, 1, rows, RG_COLS), idx(lambda b, d, c: (0, b, c, d))),
        act=pl.BlockSpec((1, rows, RG_COLS), idx(lambda b, d, c: (b, c, d))),
        taps=pl.BlockSpec((CONV_WIDTH, RG_COLS), idx(lambda b, d, c: (0, d))),
        vec=pl.BlockSpec((1, RG_COLS), idx(lambda b, d, c: (0, d))),
        gate=pl.BlockSpec((1, RG_COLS, RG_COLS), idx(lambda b, d, c: (d, 0, 0))),
    )


def rglru_fwd(proj, conv_w, conv_b, wa, ba, wx, bx, lam, name, rows=RG_ROWS, ride=None):
    _, B, S, _ = proj.shape
    rows = min(rows, S)
    sp_ = _rg_specs(B, S, rows, "bdc")

    def body(p_ref, cw_ref, cb_ref, wa_ref, ba_ref, wx_ref, bx_ref, lam_ref,
             xc_ref, hs_ref, hp_ref, gt_ref, ext_ref, a_ref, u_ref, xcar_ref, hcar_ref):
        @pl.when(pl.program_id(2) == 0)
        def _():
            xcar_ref[...] = jnp.zeros_like(xcar_ref)
            hcar_ref[...] = jnp.zeros_like(hcar_ref)

        xr = p_ref[0, 0]
        ext_ref[0:8, :] = xcar_ref[...]
        ext_ref[8:, :] = xr
        xcar_ref[...] = xr[rows - 8:, :]
        xc = ext_ref[pl.ds(5, rows), :] * cw_ref[0:1, :]
        xc = xc + ext_ref[pl.ds(6, rows), :] * cw_ref[1:2, :]
        xc = xc + ext_ref[pl.ds(7, rows), :] * cw_ref[2:3, :]
        xc = xc + xr * cw_ref[3:4, :] + cb_ref[...]
        xc_ref[0] = xc
        _, _, i, _, a, m = _rg_gates(xc, wa_ref, ba_ref, wx_ref, bx_ref, lam_ref)
        a_ref[...] = a
        u_ref[...] = m * (i * xc)

        def step(r0, carry):
            h = _group_scan(a_ref[pl.ds(r0, 8), :], u_ref[pl.ds(r0, 8), :], carry)
            row = lax.broadcasted_iota(jnp.int32, h.shape, 0)
            hs_ref[0, pl.ds(r0, 8), :] = h
            hp_ref[0, pl.ds(r0, 8), :] = jnp.where(row == 0, carry, pltpu.roll(h, 1, 0))
            return h[7:8, :]

        hcar_ref[0:1, :] = _scan_rows(rows, step, hcar_ref[0:1, :])
        gt_ref[0] = (hs_ref[0] * _silu(p_ref[1, 0])).astype(BF16)

    act = jax.ShapeDtypeStruct((B, S, RG_WIDTH), F32)
    return _call(
        body, name=name, grid=(B, RG_GROUPS, S // rows),
        in_specs=[sp_["proj"], sp_["taps"], sp_["vec"], sp_["gate"], sp_["vec"], sp_["gate"], sp_["vec"], sp_["vec"]],
        out_specs=[sp_["act"]] * 4,
        out_shape=[act, act, act, jax.ShapeDtypeStruct((B, S, RG_WIDTH), BF16)],
        scratch_shapes=[pltpu.VMEM((rows + 8, RG_COLS), F32), pltpu.VMEM((rows, RG_COLS), F32),
                        pltpu.VMEM((rows, RG_COLS), F32), pltpu.VMEM((8, RG_COLS), F32), pltpu.VMEM((8, RG_COLS), F32)],
        semantics=("parallel", "parallel", "arbitrary"), args=[proj, conv_w, conv_b, wa, ba, wx, bx, lam], ride=ride)


def rglru_bwd(proj, xc, hs, hprev, dgated, conv_w, wa, ba, wx, bx, lam, name, rows=RG_ROWS, ride=None):
    _, B, S, _ = proj.shape
    rows = min(rows, S)
    sp_ = _rg_specs(B, S, rows, "dbC")

    def body(p_ref, xc_ref, hs_ref, hp_ref, dgt_ref, cw_ref, wa_ref, ba_ref, wx_ref, bx_ref, lam_ref,
             dp_ref, dcw_ref, dcb_ref, dwa_ref, dba_ref, dwx_ref, dbx_ref, dlam_ref,
             ext_ref, c_ref, l_ref, acar_ref, lcar_ref, dcar_ref):
        b_, c_ = pl.program_id(1), pl.program_id(2)

        @pl.when(c_ == 0)
        def _():
            acar_ref[...] = jnp.zeros_like(acar_ref)
            lcar_ref[...] = jnp.zeros_like(lcar_ref)
            dcar_ref[...] = jnp.zeros_like(dcar_ref)

        @pl.when((c_ == 0) & (b_ == 0))
        def _():
            for ref in (dcw_ref, dcb_ref, dwa_ref, dba_ref, dwx_ref, dbx_ref, dlam_ref):
                ref[...] = jnp.zeros_like(ref)

        xr, g = p_ref[0, 0], p_ref[1, 0]
        xc_v = xc_ref[0]
        xcb, r, i, sp, a, m = _rg_gates(xc_v, wa_ref, ba_ref, wx_ref, bx_ref, lam_ref)
        dhs, dg = _gate_bwd(dgt_ref[0], hs_ref[0], g)
        dp_ref[1, 0] = dg.astype(BF16)

        ext_ref[0:rows, :] = a
        ext_ref[rows:, :] = acar_ref[...]
        acar_ref[...] = a[0:8, :]
        c_ref[...] = ext_ref[pl.ds(1, rows), :]
        l_ref[...] = dhs

        def step(r0, carry):
            lam_g = _group_scan(c_ref[pl.ds(r0, 8), :], l_ref[pl.ds(r0, 8), :], carry, reverse=True)
            l_ref[pl.ds(r0, 8), :] = lam_g
            return lam_g[0:1, :]

        lcar_ref[0:1, :] = _scan_rows(rows, step, lcar_ref[0:1, :], reverse=True)
        du = l_ref[...]
        da = du * hp_ref[0]
        dlog_a = da * a - (du * (i * xc_v)) * (a * a / m)
        dr = dlog_a * (-RG_C * sp)
        dsp = jnp.sum(dlog_a * (-RG_C * r), axis=0, keepdims=True)
        dlam_ref[...] += dsp * (-_sigmoid(-lam_ref[...]))
        dpa = dr * (r * (1.0 - r))
        dpx = (du * (m * xc_v)) * (i * (1.0 - i))
        dba_ref[...] += jnp.sum(dpa, axis=0, keepdims=True)
        dbx_ref[...] += jnp.sum(dpx, axis=0, keepdims=True)
        dpa, dpx = dpa.astype(BF16), dpx.astype(BF16)
        dwa_ref[0] += _dot_tn(xcb, dpa)
        dwx_ref[0] += _dot_tn(xcb, dpx)
        dxc = du * (m * i) + _dot_nt(dpa, wa_ref[0]) + _dot_nt(dpx, wx_ref[0])

        dcb_ref[...] += jnp.sum(dxc, axis=0, keepdims=True)
        ext_ref[0:rows, :] = dxc
        ext_ref[rows:, :] = dcar_ref[...]
        dcar_ref[...] = dxc[0:8, :]
        dxr = jnp.zeros_like(dxc)
        for k in range(CONV_WIDTH):
            tap = CONV_WIDTH - 1 - k
            ahead = dxc if k == 0 else ext_ref[pl.ds(k, rows), :]
            dxr = dxr + ahead * cw_ref[tap:tap + 1, :]
            dcw_ref[tap:tap + 1, :] += jnp.sum(xr * ahead, axis=0, keepdims=True)
        dp_ref[0, 0] = dxr.astype(BF16)

    vec = jax.ShapeDtypeStruct((1, RG_WIDTH), F32)
    gate = jax.ShapeDtypeStruct((RG_GROUPS, RG_COLS, RG_COLS), F32)
    return _call(
        body, name=name, grid=(RG_GROUPS, B, S // rows),
        in_specs=[sp_["proj"], sp_["act"], sp_["act"], sp_["act"], sp_["act"], sp_["taps"],
                  sp_["gate"], sp_["vec"], sp_["gate"], sp_["vec"], sp_["vec"]],
        out_specs=[sp_["proj"], sp_["taps"], sp_["vec"], sp_["gate"], sp_["vec"], sp_["gate"], sp_["vec"], sp_["vec"]],
        out_shape=[jax.ShapeDtypeStruct(proj.shape, BF16), jax.ShapeDtypeStruct((CONV_WIDTH, RG_WIDTH), F32), vec,
                   gate, vec, gate, vec, vec],
        scratch_shapes=[pltpu.VMEM((rows + 8, RG_COLS), F32), pltpu.VMEM((rows, RG_COLS), F32),
                        pltpu.VMEM((rows, RG_COLS), F32), pltpu.VMEM((8, RG_COLS), F32),
                        pltpu.VMEM((8, RG_COLS), F32), pltpu.VMEM((8, RG_COLS), F32)],
        semantics=("arbitrary", "arbitrary", "arbitrary"),
        args=[proj, xc, hs, hprev, dgated, conv_w, wa, ba, wx, bx, lam], ride=ride)


def block_diag_gates(w):
    per = RG_COLS // RG_BLOCK
    w4 = w.reshape(RG_GROUPS, per, RG_BLOCK, RG_BLOCK)
    return jnp.einsum("dipq,ij->dipjq", w4, jnp.eye(per, dtype=w.dtype)).reshape(RG_GROUPS, RG_COLS, RG_COLS)


def block_diag_gates_t(dw):
    per = RG_COLS // RG_BLOCK
    dw6 = dw.reshape(RG_GROUPS, per, RG_BLOCK, per, RG_BLOCK)
    return jnp.stack([dw6[:, i, :, i, :] for i in range(per)], axis=1).reshape(RG_BLOCKS, RG_BLOCK, RG_BLOCK)


def adamw(w, parts, m, v, name, layer=0, prev=None, row_tile=ROW_TILE):
    L, R, C = w.shape
    n_parts = parts.shape[0]
    br = row_tile if R % row_tile == 0 else R

    def body(w_ref, p_ref, m_ref, v_ref, *refs):
        g_ref, d_ref, nm_ref, nv_ref = refs[-4:]
        g = p_ref[0].astype(F32)
        for k in range(1, n_parts):
            g = g + p_ref[k].astype(F32)
        nm = ADAM_B1 * m_ref[0] + (1.0 - ADAM_B1) * g
        nv = ADAM_B2 * v_ref[0] + (1.0 - ADAM_B2) * (g * g)
        m_hat = nm / (1.0 - ADAM_B1 ** ADAM_STEP)
        v_hat = nv / (1.0 - ADAM_B2 ** ADAM_STEP)
        g_ref[0] = g
        d_ref[0] = -ADAM_LR * (m_hat / (jnp.sqrt(v_hat) + ADAM_EPS) + ADAM_WD * w_ref[0])
        nm_ref[0] = nm
        nv_ref[0] = nv

    slab = pl.BlockSpec((1, br, C), lambda i: (layer, i, 0))
    out = jax.ShapeDtypeStruct((L, R, C), F32)
    carried = [] if prev is None else list(prev)
    return _call(
        body, name=name, grid=(R // br,),
        in_specs=[slab, pl.BlockSpec((n_parts, br, C), lambda i: (0, i, 0)), slab, slab]
        + [pl.BlockSpec(memory_space=pl.ANY)] * len(carried),
        out_specs=[slab] * 4, out_shape=[out] * 4, semantics=("parallel",), args=[w, parts, m, v] + carried,
        aliases={4 + k: k for k in range(len(carried))})


def _seq(a, B):
    return a.reshape(a.shape[:-2] + (B, a.shape[-2] // B, a.shape[-1]))


def _flat(a):
    return a.reshape(a.shape[:-3] + (a.shape[-3] * a.shape[-2], a.shape[-1]))


def _tiles(w, which, **default):
    return dict(default, **w.get("tiles", {}).get(which, {}))


def mixer_a_fwd(h, w, B, tag, rides):
    proj = matmul(h, w["w_in"], mode="nn", out_dtype=F32, name=f"{tag}_proj", out_slabs=4,
                  ride=rides.pop(f"{tag}_proj", None), **_tiles(w, "proj"))
    o, gated = attn_a_fwd(_seq(proj, B), w["bias"], f"{tag}_attn", ride=rides.pop(f"{tag}_attn", None))
    return _flat(gated), dict(proj=proj, o=o)


def mixer_a_bwd(dgated, w, saved, B, tag, rides):
    dproj, dbias = attn_a_bwd(_seq(saved["proj"], B), w["bias"], saved["o"], _seq(dgated, B), f"{tag}_attn_bwd",
                              ride=rides.pop(f"{tag}_attn_bwd", None))
    return _flat(dproj), dict(bias=dbias)


def mixer_b_fwd(h, w, B, tag, rides):
    proj = matmul(h, w["w_in"], mode="nn", out_dtype=F32, name=f"{tag}_proj", out_slabs=2, bn=RG_COLS,
                  ride=rides.pop(f"{tag}_proj", None))
    xc, hs, hprev, gated = rglru_fwd(_seq(proj, B), w["conv_w"], w["conv_b"], w["wa"], w["ba"], w["wx"], w["bx"],
                                     w["lam"], f"{tag}_rglru", ride=rides.pop(f"{tag}_rglru", None))
    return _flat(gated), dict(proj=proj, xc=xc, hs=hs, hprev=hprev)


def mixer_b_bwd(dgated, w, saved, B, tag, rides):
    dproj, dcw, dcb, dwa, dba, dwx, dbx, dlam = rglru_bwd(
        _seq(saved["proj"], B), saved["xc"], saved["hs"], saved["hprev"], _seq(dgated, B),
        w["conv_w"], w["wa"], w["ba"], w["wx"], w["bx"], w["lam"], f"{tag}_rglru_bwd",
        ride=rides.pop(f"{tag}_rglru_bwd", None))
    return _flat(dproj), dict(conv_w=dcw, conv_b=dcb, wa=dwa, ba=dba, wx=dwx, bx=dbx, lam=dlam)


def mixer_c_fwd(h, w, B, tag, rides):
    proj = matmul(h, w["w_in"], mode="nn", out_dtype=F32, name=f"{tag}_proj", out_slabs=4,
                  ride=rides.pop(f"{tag}_proj", None), **_tiles(w, "proj"))
    f_logit = matmul(h, w["w_f"], mode="nn", out_dtype=F32, name=f"{tag}_fproj")
    cum = fox_cum_fwd(_seq(f_logit, B), w["f_bias"], f"{tag}_cum")
    cum_row = cum[:, :, :HEADS].transpose(0, 2, 1)
    o, gated = fox_fwd(_seq(proj, B), cum, cum_row, f"{tag}_attn", ride=rides.pop(f"{tag}_attn", None))
    return _flat(gated), dict(proj=proj, o=o, f_logit=f_logit, cum=cum, cum_row=cum_row)


def mixer_c_bwd(dgated, w, saved, B, tag, rides):
    dproj, dck = fox_bwd(_seq(saved["proj"], B), saved["cum"], saved["cum_row"], saved["o"], _seq(dgated, B),
                         f"{tag}_attn_bwd", ride=rides.pop(f"{tag}_attn_bwd", None))
    S = dck.shape[-1]
    dcum = jnp.pad(-dck.reshape(B, HEADS, S).transpose(0, 2, 1), ((0, 0), (0, 0), (0, HEAD_DIM - HEADS)))
    df, dfb = fox_cum_bwd(_seq(saved["f_logit"], B), w["f_bias"], dcum, f"{tag}_cum_bwd")
    return _flat(dproj), dict(f_bias=dfb, df=_flat(df).astype(BF16))


MIXERS = {"a": (mixer_a_fwd, mixer_a_bwd), "b": (mixer_b_fwd, mixer_b_bwd), "c": (mixer_c_fwd, mixer_c_bwd)}
LAYER_KINDS = "abca"


def local_step(x, target, norm_pre, norm_post, get_layer, rides, on_grads):
    B, S, Dm = x.shape
    n_layers = len(LAYER_KINDS)
    xs = [x.reshape(B * S, Dm)]
    saved, layers = [], []
    for li, kind in enumerate(LAYER_KINDS):
        tag = f"l{li}{kind}"
        h = prenorm_fwd(xs[-1], norm_pre[li:li + 1], f"{tag}_prenorm")
        w = get_layer(li)
        gated, sv = MIXERS[kind][0](h, w, B, tag, rides)
        y = matmul(gated, w["w_out"], mode="nn", out_dtype=F32, name=f"{tag}_out", ride=rides.pop(f"{tag}_out", None))
        xs.append(postnorm_fwd(xs[-1], y, norm_post[li:li + 1], f"{tag}_postnorm"))
        saved.append(dict(sv, h=h, gated=gated, y=y))
        layers.append(w)
    loss, dx = loss_fwd_bwd(xs[-1], target.reshape(B * S, Dm), "loss")

    for li in reversed(range(n_layers)):
        kind, w, sv = LAYER_KINDS[li], layers[li], saved[li]
        tag = f"l{li}{kind}"
        dy, dg_post = postnorm_bwd(sv["y"], norm_post[li:li + 1], dx, f"{tag}_postnorm_bwd")
        on_grads(li, "norm_post", dg_post)
        on_grads(li, "w_out", matmul(sv["gated"], dy, mode="tn", out_dtype=BF16, name=f"{tag}_dwout", bk=1024,
                                     ride=rides.pop(f"{tag}_dwout", None)))
        dgated = matmul(dy, w["w_out"], mode="nt", out_dtype=F32, name=f"{tag}_dgated", bn=512,
                        ride=rides.pop(f"{tag}_dgated", None))
        dproj, gw = MIXERS[kind][1](dgated, w, sv, B, tag, rides)
        df = gw.pop("df", None)
        for name, value in gw.items():
            on_grads(li, name, value)
        on_grads(li, "w_in", matmul(sv["h"], dproj, mode="tn", out_dtype=BF16, name=f"{tag}_dwin",
                                    out_slabs=w["grad_slabs"], ride=rides.pop(f"{tag}_dwin", None),
                                    **_tiles(w, "dwin", bk=1024)))
        dhs = [matmul(dproj, w["w_in"], mode="nt", out_dtype=F32, name=f"{tag}_dh",
                      ride=rides.pop(f"{tag}_dh", None), **_tiles(w, "dh", bn=1024, bk=1024))]
        if df is not None:
            on_grads(li, "w_f", matmul(sv["h"], df, mode="tn", out_dtype=BF16, name=f"{tag}_dwf", bk=1024))
            dhs.append(matmul(df, w["w_f"], mode="nt", out_dtype=F32, name=f"{tag}_dhf", bn=1024))
        dx, dg_pre = prenorm_bwd(xs[li], norm_pre[li:li + 1], dhs, dx, f"{tag}_prenorm_bwd")
        on_grads(li, "norm_pre", dg_pre)
    assert not rides, list(rides)
    return loss, dx.reshape(B, S, Dm)


WEIGHTS = ("norm_pre", "norm_post", "a_w_in", "a_rel_bias", "a_w_out", "b_w_in", "b_conv_w", "b_conv_b",
           "b_gate_a_w", "b_gate_a_b", "b_gate_x_w", "b_gate_x_b", "b_lambda", "b_w_out", "c_w_in", "c_f_bias",
           "c_w_out")
C_SHARD = (4 * D_MODEL + HEADS) // N_DEV


def _rows(gathered):
    return gathered.reshape(gathered.shape[0] * gathered.shape[1], gathered.shape[2])


def layer_a(w_in, w_out, rel_bias):
    return dict(w_in=w_in, w_out=_rows(w_out), bias=band_bias(rel_bias), grad_slabs=N_DEV)


def layer_b(w_in, w_out, conv_w, small):
    return dict(
        w_in=w_in, w_out=_rows(w_out), grad_slabs=N_DEV,
        conv_w=conv_w.transpose(1, 0, 2).reshape(CONV_WIDTH, RG_WIDTH),
        conv_b=small["b_conv_b"], lam=small["b_lambda"],
        wa=block_diag_gates(small["b_gate_a_w"][0]).astype(BF16), ba=small["b_gate_a_b"].reshape(1, RG_WIDTH),
        wx=block_diag_gates(small["b_gate_x_w"][0]).astype(BF16), bx=small["b_gate_x_b"].reshape(1, RG_WIDTH))


def layer_c(w_in, w_out, small):
    full = w_in.transpose(1, 0, 2).reshape(D_MODEL, N_DEV * C_SHARD)
    return dict(w_in=full[:, :4 * D_MODEL], w_f=jnp.pad(full[:, 4 * D_MODEL:], ((0, 0), (0, HEAD_DIM - HEADS))),
                w_out=_rows(w_out), grad_slabs=1,
                f_bias=jnp.pad(small["c_f_bias"], ((0, 0), (0, HEAD_DIM - HEADS))))


def c_w_in_blocks(dmain, df):
    full = jnp.concatenate([dmain, df[:, :HEADS].astype(dmain.dtype)], axis=1)
    return full.reshape(D_MODEL, N_DEV, C_SHARD).transpose(1, 0, 2)


def _row_blocks(g):
    return g.reshape(N_DEV, g.shape[0] // N_DEV, g.shape[1])


PACK_LANES = 128
PACK_ALIGN = 8 * PACK_LANES


def pack(parts):
    flat = []
    for p in parts:
        n = p.size
        flat.append(jnp.pad(p.reshape(n), (0, -n % PACK_ALIGN)).reshape(-1, PACK_LANES))
    rows = sum(f.shape[0] for f in flat)
    flat.append(jnp.zeros((-rows % ROW_TILE, PACK_LANES), F32))
    return jnp.concatenate(flat, axis=0)


def unpack(packed, shapes):
    out, row = [], 0
    for shape in shapes:
        n = 1
        for s in shape:
            n *= s
        n_rows = (n + PACK_ALIGN - 1) // PACK_ALIGN * 8
        out.append(packed[row:row + n_rows].reshape(-1)[:n].reshape(shape))
        row += n_rows
    return out


LATE = (("a_rel_bias", slice(0, 1)), ("norm_pre", slice(0, 1)), ("norm_post", slice(0, 1)))
EARLY = (("a_rel_bias", slice(1, 2)), ("norm_pre", slice(1, 4)), ("norm_post", slice(1, 4)),
         ("b_conv_b", slice(None)), ("b_gate_a_w", slice(None)), ("b_gate_a_b", slice(None)),
         ("b_gate_x_w", slice(None)), ("b_gate_x_b", slice(None)), ("b_lambda", slice(None)),
         ("c_f_bias", slice(None)))


def _pieces(tree, pieces):
    return [tree[name][sl] for name, sl in pieces]


def kernel(x, norm_pre, norm_post, a_w_in, a_rel_bias, a_w_out, b_w_in, b_conv_w, b_conv_b, b_gate_a_w, b_gate_a_b, b_gate_x_w, b_gate_x_b, b_lambda, b_w_out, c_w_in, c_f_bias, c_w_out, loss_target, m_norm_pre, m_norm_post, m_a_w_in, m_a_rel_bias, m_a_w_out, m_b_w_in, m_b_conv_w, m_b_conv_b, m_b_gate_a_w, m_b_gate_a_b, m_b_gate_x_w, m_b_gate_x_b, m_b_lambda, m_b_w_out, m_c_w_in, m_c_f_bias, m_c_w_out, v_norm_pre, v_norm_post, v_a_w_in, v_a_rel_bias, v_a_w_out, v_b_w_in, v_b_conv_w, v_b_conv_b, v_b_gate_a_w, v_b_gate_a_b, v_b_gate_x_w, v_b_gate_x_b, v_b_lambda, v_b_w_out, v_c_w_in, v_c_f_bias, v_c_w_out):
    args = dict(locals())
    w = {n: args[n] for n in WEIGHTS}
    m = {n: args["m_" + n] for n in WEIGHTS}
    v = {n: args["v_" + n] for n in WEIGHTS}

    a_in, a_out = a_w_in.astype(BF16), a_w_out.astype(BF16)
    gather_a0 = Ride([a_in[0], a_out[0]], scatter=False)
    gather_b = Ride([b_w_in[0].astype(BF16), b_w_out[0].astype(BF16), b_conv_w[0]], scatter=False)
    gather_c_in = Ride([c_w_in[0].astype(BF16)], scatter=False)
    gather_c_out = Ride([c_w_out[0].astype(BF16)], scatter=False)
    gather_a1 = Ride([a_in[1], a_out[1]], scatter=False)
    exchange(gather_a0, "gather_l0")
    rides = {"l0a_proj": gather_b, "l0a_attn": gather_c_in, "l1b_rglru": gather_c_out, "l2c_attn": gather_a1}

    def get_layer(li):
        if li == 0:
            return layer_a(*gather_a0.out, a_rel_bias[0])
        if li == 1:
            return layer_b(*gather_b.out, w)
        if li == 2:
            return dict(layer_c(gather_c_in.out[0], gather_c_out.out[0], w),
                        tiles=dict(proj=dict(bn=2048), dh=dict(bk=2048), dwin=dict(bk=2048)))
        big_rows = dict(bm=1024)
        return dict(layer_a(*gather_a1.out, a_rel_bias[1]), tiles=dict(proj=big_rows, dh=big_rows, dwin=big_rows))

    grads = [dict() for _ in LAYER_KINDS]
    scatters = {}

    def rel_bias_grad(j, dbias):
        return jax.vjp(band_bias, a_rel_bias[j])[1](dbias)[0][None]

    def early_partial():
        gb, gc = grads[1], grads[2]
        tree = dict(
            a_rel_bias=jnp.concatenate([jnp.zeros((1, HEADS, N_REL), F32), rel_bias_grad(1, grads[3]["bias"])]),
            norm_pre=jnp.concatenate([jnp.zeros((1, D_MODEL), F32)] + [grads[li]["norm_pre"] for li in (1, 2, 3)]),
            norm_post=jnp.concatenate([jnp.zeros((1, D_MODEL), F32)] + [grads[li]["norm_post"] for li in (1, 2, 3)]),
            b_conv_b=gb["conv_b"], b_lambda=gb["lam"],
            b_gate_a_w=block_diag_gates_t(gb["wa"])[None], b_gate_a_b=gb["ba"].reshape(1, RG_BLOCKS, RG_BLOCK),
            b_gate_x_w=block_diag_gates_t(gb["wx"])[None], b_gate_x_b=gb["bx"].reshape(1, RG_BLOCKS, RG_BLOCK),
            c_f_bias=gc["f_bias"][:, :HEADS])
        return pack(_pieces(tree, EARLY))

    def on_grads(li, name, value):
        g = grads[li]
        g[name] = value
        if (li, name) == (3, "w_in"):
            scatters["a1"] = rides["l2c_attn_bwd"] = Ride([g["w_in"], _row_blocks(g["w_out"])], scatter=True)
        elif (li, name) == (2, "w_f"):
            scatters["c"] = rides["l1b_rglru_bwd"] = Ride(
                [c_w_in_blocks(g["w_in"], g["w_f"]), _row_blocks(g["w_out"])], scatter=True)
        elif (li, name) == (1, "w_in"):
            conv = g["conv_w"].reshape(CONV_WIDTH, N_DEV, RG_WIDTH // N_DEV).transpose(1, 0, 2)
            scatters["b"] = rides["l0a_attn_bwd"] = Ride([g["w_in"], _row_blocks(g["w_out"]), conv], scatter=True)
        elif (li, name) == (1, "norm_pre"):
            scatters["early"] = rides["l0a_dwin"] = Ride([early_partial()], scatter=False)
        elif (li, name) == (0, "w_out"):
            scatters["a0_out"] = rides["l0a_dgated"] = Ride([_row_blocks(value)], scatter=True)
        elif (li, name) == (0, "w_in"):
            scatters["a0_in"] = rides["l0a_dh"] = Ride([value], scatter=True)

    loss, grad_x = local_step(x, loss_target, norm_pre, norm_post, get_layer, rides, on_grads)
    late_tree = dict(a_rel_bias=rel_bias_grad(0, grads[0]["bias"]), norm_pre=grads[0]["norm_pre"],
                     norm_post=grads[0]["norm_post"])
    late_parts = exchange(Ride([pack([late_tree[n] for n, _ in LATE])], scatter=False), "gather_late_grads")[0]

    def sharded(name, layer_parts):
        outs = None
        for j, parts in enumerate(layer_parts):
            outs = adamw(w[name], parts, m[name], v[name], f"adamw_{name}_{j}", layer=j, prev=outs)
        return outs

    res = dict(
        a_w_in=sharded("a_w_in", [scatters["a0_in"].out[0], scatters["a1"].out[0]]),
        a_w_out=sharded("a_w_out", [scatters["a0_out"].out[0], scatters["a1"].out[1]]),
        b_w_in=sharded("b_w_in", [scatters["b"].out[0]]),
        b_w_out=sharded("b_w_out", [scatters["b"].out[1]]),
        b_conv_w=sharded("b_conv_w", [scatters["b"].out[2]]),
        c_w_in=sharded("c_w_in", [scatters["c"].out[0]]),
        c_w_out=sharded("c_w_out", [scatters["c"].out[1]]))

    packed = {}
    for label, pieces, parts in (("early", EARLY, scatters["early"].out[0]), ("late", LATE, late_parts)):
        outs = adamw(pack(_pieces(w, pieces))[None], parts, pack(_pieces(m, pieces))[None],
                     pack(_pieces(v, pieces))[None], f"adamw_replicated_{label}")
        shapes = [w[n][sl].shape for n, sl in pieces]
        packed[label] = [dict(zip([n for n, _ in pieces], unpack(o[0], shapes))) for o in outs]
    for n in ("b_conv_b", "b_gate_a_w", "b_gate_a_b", "b_gate_x_w", "b_gate_x_b", "b_lambda", "c_f_bias"):
        res[n] = [packed["early"][k][n] for k in range(4)]
    for n in ("a_rel_bias", "norm_pre", "norm_post"):
        res[n] = [jnp.concatenate([packed["late"][k][n], packed["early"][k][n]]) for k in range(4)]

    total = lax.psum(loss[0, 0], ("x", "y", "c"))
    return (total, grad_x, *[res[n][0] for n in WEIGHTS], *[res[n][1] for n in WEIGHTS],
            *[res[n][2] for n in WEIGHTS], *[res[n][3] for n in WEIGHTS])
```

```python
import functools

import jax
import jax.numpy as jnp
from jax import lax
from jax.experimental import pallas as pl
from jax.experimental.pallas import tpu as pltpu

F32 = jnp.float32
BF16 = jnp.bfloat16

N_DEV = 8
D_MODEL = 2048
HEADS = 16
HEAD_DIM = 128
CHUNK = 64
LEFT_CHUNKS = 8
REL_CLIP = 256
N_REL = 2 * REL_CLIP + 1
TQ = 256
A_PAD = LEFT_CHUNKS * CHUNK
A_KW = A_PAD + TQ
RG_WIDTH = 2560
RG_BLOCKS = 16
RG_BLOCK = 160
RG_COLS = 640
RG_GROUPS = RG_WIDTH // RG_COLS
RG_C = 8.0
CONV_WIDTH = 4
RMS_EPS = 1e-6
NEG_INF = -1e30
ADAM_LR = 0.001
ADAM_B1 = 0.9
ADAM_B2 = 0.999
ADAM_EPS = 1e-08
ADAM_WD = 0.01
ADAM_STEP = 10
VMEM_LIMIT = 56 * 1024 * 1024
MESH = pl.DeviceIdType.MESH


def _params(sem, vmem=VMEM_LIMIT):
    return pltpu.CompilerParams(dimension_semantics=sem, vmem_limit_bytes=vmem)


def _sigmoid(x):
    return 1.0 / (1.0 + jnp.exp(-x))


def _log1p(y):
    u = 1.0 + y
    return jnp.where(u == 1.0, y, jnp.log(u) * (y / jnp.where(u == 1.0, 1.0, u - 1.0)))


def _softplus(x):
    return jnp.maximum(x, 0.0) + _log1p(jnp.exp(-jnp.abs(x)))


def _dot(a, b, dims):
    return lax.dot_general(a, b, (dims, ((), ())), preferred_element_type=F32)


def _dot_nn(a, b):
    return _dot(a, b, ((1,), (0,)))


def _dot_nt(a, b):
    return _dot(a, b, ((1,), (1,)))


def _dot_tn(a, b):
    return _dot(a, b, ((0,), (0,)))


def _peers():
    x, y, c = lax.axis_index("x"), lax.axis_index("y"), lax.axis_index("c")
    me = 4 * x + 2 * y + c
    peers = []
    for k in range(1, N_DEV):
        px = 1 - x if k & 4 else x
        py = 1 - y if k & 2 else y
        pc = 1 - c if k & 1 else c
        peers.append(((px, py, pc), 4 * px + 2 * py + pc))
    return me, peers


class Ride:
    def __init__(self, arrs, scatter):
        self.arrs, self.scatter, self.out = list(arrs), scatter, None

    def out_shapes(self):
        return [jax.ShapeDtypeStruct(a.shape if self.scatter else (N_DEV,) + a.shape, a.dtype) for a in self.arrs]

    def sem_shapes(self):
        n = len(self.arrs)
        return [pltpu.SemaphoreType.DMA((n, N_DEV - 1)), pltpu.SemaphoreType.DMA((n, N_DEV - 1)),
                pltpu.SemaphoreType.DMA((n,))]

    def _copies(self, ins, outs, sems, landing):
        send_sems, recv_sems, local_sems = sems
        me, peers = _peers()
        local, remote = [], []
        for a, (src, dst) in enumerate(zip(ins, outs)):
            local.append(pltpu.make_async_copy(src.at[me] if self.scatter else src, dst.at[me], local_sems.at[a]))
            for k, (peer, peer_idx) in enumerate(peers):
                remote.append(pltpu.make_async_remote_copy(
                    src_ref=src.at[peer_idx] if self.scatter else src, dst_ref=dst.at[peer_idx if landing else me],
                    send_sem=send_sems.at[a, k], recv_sem=recv_sems.at[a, k], device_id=peer, device_id_type=MESH))
        return local, remote

    def start(self, ins, outs, sems):
        local, remote = self._copies(ins, outs, sems, landing=False)
        for cp in local + remote:
            cp.start()

    def wait(self, ins, outs, sems):
        local, remote = self._copies(ins, outs, sems, landing=True)
        for cp in remote + local:
            cp.wait()


def _call(body, *, name, grid, in_specs, out_specs, out_shape, args, scratch_shapes=(), semantics=None, ride=None,
          aliases=None):
    scratch_shapes = list(scratch_shapes)
    if ride is None:
        return pl.pallas_call(
            body, name=name, grid=grid, in_specs=in_specs, out_specs=out_specs, out_shape=out_shape,
            scratch_shapes=scratch_shapes, input_output_aliases=aliases or {},
            compiler_params=_params(semantics if grid else None))(*args)
    assert not aliases
    n_in, n_out, n_sc, n_r = len(in_specs), len(out_specs), len(scratch_shapes), len(ride.arrs)

    def riding(*refs):
        ins, r_ins = refs[:n_in], refs[n_in:n_in + n_r]
        outs, r_outs = refs[n_in + n_r:n_in + n_r + n_out], refs[n_in + n_r + n_out:n_in + 2 * n_r + n_out]
        rest = refs[n_in + 2 * n_r + n_out:]
        scratch, sems = rest[:n_sc], rest[n_sc:]
        first = last = None
        for axis, size in enumerate(grid):
            pid = pl.program_id(axis)
            first = (pid == 0) if first is None else first & (pid == 0)
            last = (pid == size - 1) if last is None else last & (pid == size - 1)
        if grid:
            pl.when(first)(lambda: ride.start(r_ins, r_outs, sems))
        else:
            ride.start(r_ins, r_outs, sems)
        body(*ins, *outs, *scratch)
        if grid:
            pl.when(last)(lambda: ride.wait(r_ins, r_outs, sems))
        else:
            ride.wait(r_ins, r_outs, sems)

    any_spec = pl.BlockSpec(memory_space=pl.ANY)
    res = pl.pallas_call(
        riding, name=name, grid=grid, in_specs=list(in_specs) + [any_spec] * n_r,
        out_specs=list(out_specs) + [any_spec] * n_r, out_shape=list(out_shape) + ride.out_shapes(),
        scratch_shapes=scratch_shapes + ride.sem_shapes(),
        compiler_params=_params(("arbitrary",) * len(grid) if grid else None))(*args, *ride.arrs)
    ride.out = list(res[n_out:])
    return list(res[:n_out])


def exchange(ride, name):
    _call(lambda: None, name=name, grid=(), in_specs=[], out_specs=[], out_shape=[], args=[], ride=ride)
    return ride.out


LANES = 128


def _fit(dims, want):
    dims = tuple(dims)
    if len(set(dims)) == 1 and dims[0] <= want:
        return dims[0]
    return max(t for t in range(LANES, want + 1, LANES) if all(d % t == 0 for d in dims))


def _cols(arr):
    return arr.shape[-1] * (arr.shape[0] if len(arr.shape) == 3 else 1)


def _tile_spec(shape, rblk, cblk, rc):
    if len(shape) == 2:
        return pl.BlockSpec((rblk, cblk), rc)
    per = shape[2] // cblk

    def index_map(*ids):
        r, c = rc(*ids)
        return (c // per, r, c % per)

    return pl.BlockSpec((1, rblk, cblk), index_map)


def matmul(a, b, *, mode, out_dtype, name, bm=1024, bn=1024, bk=2048, out_slabs=1, ride=None):
    a_rows, a_cols, b_rows, b_cols = a.shape[-2], _cols(a), b.shape[-2], _cols(b)
    (K, M) = (a_rows, a_cols) if mode == "tn" else (a_cols, a_rows)
    N = b_rows if mode == "nt" else b_cols
    assert K == (b_cols if mode == "nt" else b_rows), (name, a.shape, b.shape)
    out_shape = (M, N) if out_slabs == 1 else (out_slabs, M, N // out_slabs)
    widths = dict(m=[M], n=[N, out_shape[-1]], k=[K])
    widths["m" if mode == "tn" else "k"].append(a.shape[-1])
    widths["k" if mode == "nt" else "n"].append(b.shape[-1])
    bm, bn, bk = _fit(widths["m"], bm), _fit(widths["n"], bn), _fit(widths["k"], bk)
    nk = K // bk
    dims = {"nn": ((1,), (0,)), "nt": ((1,), (1,)), "tn": ((0,), (0,))}[mode]

    def val(ref):
        return ref[0] if len(ref.shape) == 3 else ref[...]

    def put(ref, x):
        if len(ref.shape) == 3:
            ref[0] = x.astype(ref.dtype)
        else:
            ref[...] = x.astype(ref.dtype)

    def body(a_ref, b_ref, o_ref, *scratch):
        if nk == 1:
            put(o_ref, _dot(val(a_ref), val(b_ref), dims))
            return
        acc_ref, = scratch
        k = pl.program_id(2)

        @pl.when(k == 0)
        def _():
            acc_ref[...] = jnp.zeros_like(acc_ref)

        acc_ref[...] += _dot(val(a_ref), val(b_ref), dims)

        @pl.when(k == nk - 1)
        def _():
            put(o_ref, acc_ref[...])

    if mode == "tn":
        a_spec = _tile_spec(a.shape, bk, bm, lambda j, i, k: (k, i))
    else:
        a_spec = _tile_spec(a.shape, bm, bk, lambda j, i, k: (i, k))
    if mode == "nt":
        b_spec = _tile_spec(b.shape, bn, bk, lambda j, i, k: (j, k))
    else:
        b_spec = _tile_spec(b.shape, bk, bn, lambda j, i, k: (k, j))
    return _call(
        body, name=name, grid=(N // bn, M // bm, nk), in_specs=[a_spec, b_spec],
        out_specs=[_tile_spec(out_shape, bm, bn, lambda j, i, k: (i, j))],
        out_shape=[jax.ShapeDtypeStruct(out_shape, out_dtype)],
        scratch_shapes=[] if nk == 1 else [pltpu.VMEM((bm, bn), F32)],
        semantics=("parallel", "parallel", "arbitrary"), args=[a, b], ride=ride)[0]


ROW_TILE = 256


def _rms_stats(z):
    r = lax.rsqrt(jnp.mean(z * z, axis=-1, keepdims=True) + RMS_EPS)
    return r, z * r


def _rms_bwd(n, r, g, dout):
    dn = dout * g
    return r * (dn - n * jnp.mean(dn * n, axis=-1, keepdims=True))


def _row_spec(T, Dm):
    bt = min(ROW_TILE, T)
    return bt, pl.BlockSpec((bt, Dm), lambda i: (i, 0)), pl.BlockSpec((1, Dm), lambda i: (0, 0))


def prenorm_fwd(x, g, name):
    T, Dm = x.shape
    bt, row, vec = _row_spec(T, Dm)

    def body(x_ref, g_ref, h_ref):
        _, n = _rms_stats(x_ref[...])
        h_ref[...] = (n * g_ref[...]).astype(BF16)

    return pl.pallas_call(
        body, name=name, grid=(T // bt,), in_specs=[row, vec], out_specs=row,
        out_shape=jax.ShapeDtypeStruct((T, Dm), BF16), compiler_params=_params(("parallel",)),
    )(x, g)


def postnorm_fwd(x, y, g, name):
    T, Dm = x.shape
    bt, row, vec = _row_spec(T, Dm)

    def body(x_ref, y_ref, g_ref, o_ref):
        _, n = _rms_stats(y_ref[...])
        o_ref[...] = x_ref[...] + n * g_ref[...]

    return pl.pallas_call(
        body, name=name, grid=(T // bt,), in_specs=[row, row, vec], out_specs=row,
        out_shape=jax.ShapeDtypeStruct((T, Dm), F32), compiler_params=_params(("parallel",)),
    )(x, y, g)


def loss_fwd_bwd(xf, target, name):
    T, Dm = xf.shape
    bt, row, _ = _row_spec(T, Dm)

    def body(x_ref, t_ref, l_ref, d_ref):
        @pl.when(pl.program_id(0) == 0)
        def _():
            l_ref[...] = jnp.zeros_like(l_ref)

        err = x_ref[...] - t_ref[...]
        per_tok = jnp.mean(err * err, axis=-1, keepdims=True)
        l_ref[...] += 0.5 * jnp.sum(per_tok, axis=0, keepdims=True)
        d_ref[...] = err * (1.0 / Dm)

    return pl.pallas_call(
        body, name=name, grid=(T // bt,), in_specs=[row, row],
        out_specs=[pl.BlockSpec((1, 1), lambda i: (0, 0)), row],
        out_shape=[jax.ShapeDtypeStruct((1, 1), F32), jax.ShapeDtypeStruct((T, Dm), F32)],
        compiler_params=_params(("arbitrary",)),
    )(xf, target)


def postnorm_bwd(y, g, dout, name):
    T, Dm = y.shape
    bt, row, vec = _row_spec(T, Dm)

    def body(y_ref, g_ref, d_ref, dy_ref, dg_ref):
        @pl.when(pl.program_id(0) == 0)
        def _():
            dg_ref[...] = jnp.zeros_like(dg_ref)

        r, n = _rms_stats(y_ref[...])
        dout_v = d_ref[...]
        dg_ref[...] += jnp.sum(dout_v * n, axis=0, keepdims=True)
        dy_ref[...] = _rms_bwd(n, r, g_ref[...], dout_v).astype(BF16)

    return pl.pallas_call(
        body, name=name, grid=(T // bt,), in_specs=[row, vec, row], out_specs=[row, vec],
        out_shape=[jax.ShapeDtypeStruct((T, Dm), BF16), jax.ShapeDtypeStruct((1, Dm), F32)],
        compiler_params=_params(("arbitrary",)),
    )(y, g, dout)


def prenorm_bwd(x, g, dhs, dres, name):
    T, Dm = x.shape
    bt, row, vec = _row_spec(T, Dm)
    n_dh = len(dhs)

    def body(x_ref, g_ref, *refs):
        dh_refs, (dr_ref, dx_ref, dg_ref) = refs[:n_dh], refs[n_dh:]

        @pl.when(pl.program_id(0) == 0)
        def _():
            dg_ref[...] = jnp.zeros_like(dg_ref)

        r, n = _rms_stats(x_ref[...])
        dh_v = dh_refs[0][...]
        for extra in dh_refs[1:]:
            dh_v = dh_v + extra[...]
        dg_ref[...] += jnp.sum(dh_v * n, axis=0, keepdims=True)
        dx_ref[...] = dr_ref[...] + _rms_bwd(n, r, g_ref[...], dh_v)

    return pl.pallas_call(
        body, name=name, grid=(T // bt,), in_specs=[row, vec] + [row] * (n_dh + 1), out_specs=[row, vec],
        out_shape=[jax.ShapeDtypeStruct((T, Dm), F32), jax.ShapeDtypeStruct((1, Dm), F32)],
        compiler_params=_params(("arbitrary",)),
    )(x, g, *dhs, dres)


def _silu(g):
    return g * _sigmoid(g)


def _gate_bwd(dgated, core, g):
    sg = _sigmoid(g)
    return dgated * (g * sg), dgated * core * (sg * (1.0 + g * (1.0 - sg)))


def _softmax_rows(s):
    e = jnp.exp(s - jnp.max(s, axis=-1, keepdims=True))
    return e * (1.0 / jnp.sum(e, axis=-1, keepdims=True))


def _band_scores(q, kw, bias, r0):
    s = _dot_nt(q, kw) * (HEAD_DIM ** -0.5) + bias
    j = lax.broadcasted_iota(jnp.int32, s.shape, 1)
    return jnp.where(j >= A_PAD - r0, s, NEG_INF)


def _fill_padded_kv(p_ref, kp_ref, vp_ref):
    zeros = jnp.zeros((A_PAD, HEAD_DIM), BF16)
    kp_ref[0:A_PAD, :] = zeros
    vp_ref[0:A_PAD, :] = zeros
    kp_ref[A_PAD:, :] = p_ref[1, 0].astype(BF16)
    vp_ref[A_PAD:, :] = p_ref[2, 0].astype(BF16)


def _head_specs(S, order):
    def idx(fn):
        return lambda *ids: fn(**dict(zip(order, ids)))

    return (pl.BlockSpec((4, 1, S, HEAD_DIM), idx(lambda b, h, t: (0, b, 0, h))),
            pl.BlockSpec((1, TQ, HEAD_DIM), idx(lambda b, h, t: (b, t, h))))


def attn_a_fwd(proj, bias, name, ride=None):
    _, B, S, W = proj.shape
    nt = S // TQ

    def body(p_ref, b_ref, o_ref, gt_ref, kp_ref, vp_ref):
        t = pl.program_id(2)

        @pl.when(t == 0)
        def _():
            _fill_padded_kv(p_ref, kp_ref, vp_ref)

        r0 = pl.multiple_of(t * TQ, TQ)
        q = p_ref[0, 0, pl.ds(r0, TQ), :].astype(BF16)
        g = p_ref[3, 0, pl.ds(r0, TQ), :]
        p = _softmax_rows(_band_scores(q, kp_ref[pl.ds(r0, A_KW), :], b_ref[0], r0))
        o = _dot_nn(p.astype(BF16), vp_ref[pl.ds(r0, A_KW), :])
        o_ref[0] = o
        gt_ref[0] = (o * _silu(g)).astype(BF16)

    seq, tile = _head_specs(S, "bht")
    return _call(
        body, name=name, grid=(B, HEADS, nt),
        in_specs=[seq, pl.BlockSpec((1, TQ, A_KW), lambda b, h, t: (h, 0, 0))], out_specs=[tile, tile],
        out_shape=[jax.ShapeDtypeStruct((B, S, W), F32), jax.ShapeDtypeStruct((B, S, W), BF16)],
        scratch_shapes=[pltpu.VMEM((A_PAD + S, HEAD_DIM), BF16), pltpu.VMEM((A_PAD + S, HEAD_DIM), BF16)],
        semantics=("parallel", "parallel", "arbitrary"), args=[proj, bias], ride=ride)


def attn_a_bwd(proj, bias, o, dgated, name, ride=None):
    _, B, S, W = proj.shape
    nt = S // TQ

    def body(p_ref, b_ref, o_ref, dgt_ref, dp_ref, db_ref, kp_ref, vp_ref, dk_ref, dv_ref):
        b_, t = pl.program_id(1), pl.program_id(2)

        @pl.when(t == 0)
        def _():
            _fill_padded_kv(p_ref, kp_ref, vp_ref)
            dk_ref[...] = jnp.zeros_like(dk_ref)
            dv_ref[...] = jnp.zeros_like(dv_ref)

        @pl.when((t == 0) & (b_ == 0))
        def _():
            db_ref[...] = jnp.zeros_like(db_ref)

        r0 = pl.multiple_of(t * TQ, TQ)
        rows, win = pl.ds(r0, TQ), pl.ds(r0, A_KW)
        q = p_ref[0, 0, rows, :].astype(BF16)
        g = p_ref[3, 0, rows, :]
        kw, vw = kp_ref[win, :], vp_ref[win, :]
        p = _softmax_rows(_band_scores(q, kw, b_ref[0], r0))
        do, dg = _gate_bwd(dgt_ref[0], o_ref[0], g)
        do = do.astype(BF16)
        dv_ref[win, :] += _dot_tn(p.astype(BF16), do)
        dpr = _dot_nt(do, vw)
        ds = p * (dpr - jnp.sum(p * dpr, axis=-1, keepdims=True))
        db_ref[0] += ds
        ds = (ds * (HEAD_DIM ** -0.5)).astype(BF16)
        dk_ref[win, :] += _dot_tn(ds, q)
        dp_ref[0, 0, rows, :] = _dot_nn(ds, kw).astype(BF16)
        dp_ref[3, 0, rows, :] = dg.astype(BF16)

        @pl.when(t == nt - 1)
        def _():
            dp_ref[1, 0] = dk_ref[A_PAD:, :].astype(BF16)
            dp_ref[2, 0] = dv_ref[A_PAD:, :].astype(BF16)

    seq, tile = _head_specs(S, "hbt")
    bias_spec = pl.BlockSpec((1, TQ, A_KW), lambda h, b, t: (h, 0, 0))
    return _call(
        body, name=name, grid=(HEADS, B, nt), in_specs=[seq, bias_spec, tile, tile], out_specs=[seq, bias_spec],
        out_shape=[jax.ShapeDtypeStruct(proj.shape, BF16), jax.ShapeDtypeStruct(bias.shape, F32)],
        scratch_shapes=[pltpu.VMEM((A_PAD + S, HEAD_DIM), BF16), pltpu.VMEM((A_PAD + S, HEAD_DIM), BF16),
                        pltpu.VMEM((A_PAD + S, HEAD_DIM), F32), pltpu.VMEM((A_PAD + S, HEAD_DIM), F32)],
        semantics=("arbitrary", "arbitrary", "arbitrary"), args=[proj, bias, o, dgated], ride=ride)


def band_bias(rel_bias):
    length = TQ + A_KW - 1
    first = REL_CLIP + 1 - TQ
    gen = jnp.concatenate([rel_bias[:, first:],
                           jnp.broadcast_to(rel_bias[:, 2 * REL_CLIP:], (HEADS, length - (N_REL - first)))], axis=1)
    rev = jnp.concatenate([gen[:, ::-1], jnp.zeros((HEADS, 1), rel_bias.dtype)], axis=1)
    sheared = jnp.tile(rev, (1, TQ))[:, :TQ * length].reshape(HEADS, TQ, length)
    i = lax.broadcasted_iota(jnp.int32, (TQ, A_KW), 0)
    j = lax.broadcasted_iota(jnp.int32, (TQ, A_KW), 1)
    first_key = (i // CHUNK) * CHUNK
    in_band = (j >= first_key) & (j < first_key + (LEFT_CHUNKS + 1) * CHUNK)
    return jnp.where(in_band, sheared[:, :, TQ - 1:], NEG_INF)


def _group_scan(a, u, carry, reverse=False):
    row = lax.broadcasted_iota(jnp.int32, u.shape, 0)
    for k in (1, 2, 4):
        shift = 8 - k if reverse else k
        valid = (row < 8 - k) if reverse else (row >= k)
        u_sh = pltpu.roll(u, shift, 0)
        if a is None:
            u = jnp.where(valid, u + u_sh, u)
        else:
            a_sh = pltpu.roll(a, shift, 0)
            u = jnp.where(valid, a * u_sh + u, u)
            a = jnp.where(valid, a * a_sh, a)
    return (u + carry) if a is None else (a * carry + u)


def _scan_rows(n_rows, step, carry0, reverse=False):
    groups = n_rows // 8

    def loop(i, carry):
        gi = (groups - 1 - i) if reverse else i
        return step(pl.multiple_of(gi * 8, 8), carry)

    return lax.fori_loop(0, groups, loop, carry0)


def fox_cum_fwd(f_logit, f_bias, name):
    B, S, L = f_logit.shape

    def body(f_ref, b_ref, c_ref):
        z = f_ref[0] + b_ref[...]
        c_ref[0] = jnp.minimum(z, 0.0) - _log1p(jnp.exp(-jnp.abs(z)))

        def step(r0, carry):
            h = _group_scan(None, c_ref[0, pl.ds(r0, 8), :], carry)
            c_ref[0, pl.ds(r0, 8), :] = h
            return h[7:8, :]

        _scan_rows(S, step, jnp.zeros((1, L), F32))

    return pl.pallas_call(
        body, name=name, grid=(B,),
        in_specs=[pl.BlockSpec((1, S, L), lambda b: (b, 0, 0)), pl.BlockSpec((1, L), lambda b: (0, 0))],
        out_specs=pl.BlockSpec((1, S, L), lambda b: (b, 0, 0)),
        out_shape=jax.ShapeDtypeStruct((B, S, L), F32), compiler_params=_params(("parallel",)),
    )(f_logit, f_bias)


def fox_cum_bwd(f_logit, f_bias, dcum, name):
    B, S, L = f_logit.shape

    def body(f_ref, b_ref, d_ref, df_ref, db_ref):
        @pl.when(pl.program_id(0) == 0)
        def _():
            db_ref[...] = jnp.zeros_like(db_ref)

        def step(r0, carry):
            h = _group_scan(None, d_ref[0, pl.ds(r0, 8), :], carry, reverse=True)
            df_ref[0, pl.ds(r0, 8), :] = h
            return h[0:1, :]

        _scan_rows(S, step, jnp.zeros((1, L), F32), reverse=True)
        df = df_ref[0] * _sigmoid(-(f_ref[0] + b_ref[...]))
        df_ref[0] = df
        db_ref[...] += jnp.sum(df, axis=0, keepdims=True)

    seq = pl.BlockSpec((1, S, L), lambda b: (b, 0, 0))
    vec = pl.BlockSpec((1, L), lambda b: (0, 0))
    return pl.pallas_call(
        body, name=name, grid=(B,), in_specs=[seq, vec, seq], out_specs=[seq, vec],
        out_shape=[jax.ShapeDtypeStruct((B, S, L), F32), jax.ShapeDtypeStruct((1, L), F32)],
        compiler_params=_params(("arbitrary",)),
    )(f_logit, f_bias, dcum)


def _fox_scores(q, k, cc, cr, h, r0):
    lane = lax.broadcasted_iota(jnp.int32, cc.shape, 1)
    cq = jnp.sum(jnp.where(lane == h, cc, 0.0), axis=1, keepdims=True)
    sub = lax.broadcasted_iota(jnp.int32, cr.shape, 0)
    ck = jnp.sum(jnp.where(sub == h, cr, 0.0), axis=0, keepdims=True)
    s = _dot_nt(q, k) * (HEAD_DIM ** -0.5) + (cq - ck)
    qpos = r0 + lax.broadcasted_iota(jnp.int32, s.shape, 0)
    kpos = lax.broadcasted_iota(jnp.int32, s.shape, 1)
    return jnp.where(kpos <= qpos, s, NEG_INF)


KEY_STEP = 512


def _by_causal_width(t, S, fn):
    per = KEY_STEP // TQ
    for c in range(S // KEY_STEP):
        pl.when(t // per == c)(functools.partial(fn, (c + 1) * KEY_STEP))


def fox_fwd(proj, cum_col, cum_row, name, ride=None):
    _, B, S, W = proj.shape
    nt = S // TQ

    def body(p_ref, cc_ref, cr_ref, o_ref, gt_ref, k_ref, v_ref):
        h, t = pl.program_id(1), pl.program_id(2)

        @pl.when(t == 0)
        def _():
            k_ref[...] = p_ref[1, 0].astype(BF16)
            v_ref[...] = p_ref[2, 0].astype(BF16)

        r0 = pl.multiple_of(t * TQ, TQ)
        q = p_ref[0, 0, pl.ds(r0, TQ), :].astype(BF16)
        g = p_ref[3, 0, pl.ds(r0, TQ), :]

        def tile_out(width):
            p = _softmax_rows(_fox_scores(q, k_ref[0:width, :], cc_ref[0], cr_ref[0, :, 0:width], h, r0))
            o = _dot_nn(p.astype(BF16), v_ref[0:width, :])
            o_ref[0] = o
            gt_ref[0] = (o * _silu(g)).astype(BF16)

        _by_causal_width(t, S, tile_out)

    seq, tile = _head_specs(S, "bht")
    return _call(
        body, name=name, grid=(B, HEADS, nt),
        in_specs=[seq, pl.BlockSpec((1, TQ, cum_col.shape[2]), lambda b, h, t: (b, t, 0)),
                  pl.BlockSpec((1, HEADS, S), lambda b, h, t: (b, 0, 0))],
        out_specs=[tile, tile],
        out_shape=[jax.ShapeDtypeStruct((B, S, W), F32), jax.ShapeDtypeStruct((B, S, W), BF16)],
        scratch_shapes=[pltpu.VMEM((S, HEAD_DIM), BF16), pltpu.VMEM((S, HEAD_DIM), BF16)],
        semantics=("parallel", "parallel", "arbitrary"), args=[proj, cum_col, cum_row], ride=ride)


def fox_bwd(proj, cum_col, cum_row, o, dgated, name, ride=None):
    _, B, S, W = proj.shape
    nt = S // TQ

    def body(p_ref, cc_ref, cr_ref, o_ref, dgt_ref, dp_ref, dc_ref, k_ref, v_ref, dk_ref, dv_ref):
        h, t = pl.program_id(1), pl.program_id(2)

        @pl.when(t == 0)
        def _():
            k_ref[...] = p_ref[1, 0].astype(BF16)
            v_ref[...] = p_ref[2, 0].astype(BF16)
            dk_ref[...] = jnp.zeros_like(dk_ref)
            dv_ref[...] = jnp.zeros_like(dv_ref)
            dc_ref[...] = jnp.zeros_like(dc_ref)

        r0 = pl.multiple_of(t * TQ, TQ)
        rows = pl.ds(r0, TQ)
        q = p_ref[0, 0, rows, :].astype(BF16)
        g = p_ref[3, 0, rows, :]
        do, dg = _gate_bwd(dgt_ref[0], o_ref[0], g)
        do = do.astype(BF16)
        dp_ref[3, 0, rows, :] = dg.astype(BF16)

        def tile_grads(width):
            k, v = k_ref[0:width, :], v_ref[0:width, :]
            p = _softmax_rows(_fox_scores(q, k, cc_ref[0], cr_ref[0, :, 0:width], h, r0))
            dv_ref[0:width, :] += _dot_tn(p.astype(BF16), do)
            dpr = _dot_nt(do, v)
            ds = p * (dpr - jnp.sum(p * dpr, axis=-1, keepdims=True))
            dc_ref[0, 0, :, 0:width] += jnp.sum(ds, axis=0, keepdims=True)
            ds = (ds * (HEAD_DIM ** -0.5)).astype(BF16)
            dk_ref[0:width, :] += _dot_tn(ds, q)
            dp_ref[0, 0, rows, :] = _dot_nn(ds, k).astype(BF16)

        _by_causal_width(t, S, tile_grads)

        @pl.when(t == nt - 1)
        def _():
            dp_ref[1, 0] = dk_ref[...].astype(BF16)
            dp_ref[2, 0] = dv_ref[...].astype(BF16)

    seq, tile = _head_specs(S, "bht")
    return _call(
        body, name=name, grid=(B, HEADS, nt),
        in_specs=[seq, pl.BlockSpec((1, TQ, cum_col.shape[2]), lambda b, h, t: (b, t, 0)),
                  pl.BlockSpec((1, HEADS, S), lambda b, h, t: (b, 0, 0)), tile, tile],
        out_specs=[seq, pl.BlockSpec((1, 1, 1, S), lambda b, h, t: (b, h, 0, 0))],
        out_shape=[jax.ShapeDtypeStruct(proj.shape, BF16), jax.ShapeDtypeStruct((B, HEADS, 1, S), F32)],
        scratch_shapes=[pltpu.VMEM((S, HEAD_DIM), BF16), pltpu.VMEM((S, HEAD_DIM), BF16),
                        pltpu.VMEM((S, HEAD_DIM), F32), pltpu.VMEM((S, HEAD_DIM), F32)],
        semantics=("parallel", "parallel", "arbitrary"), args=[proj, cum_col, cum_row, o, dgated], ride=ride)


RG_ROWS = 512


def _rg_gates(xc, wa_ref, ba_ref, wx_ref, bx_ref, lam_ref):
    xcb = xc.astype(BF16)
    r = _sigmoid(_dot_nn(xcb, wa_ref[0]) + ba_ref[...])
    i = _sigmoid(_dot_nn(xcb, wx_ref[0]) + bx_ref[...])
    sp = _softplus(-lam_ref[...])
    log_a = (-RG_C * sp) * r
    a = jnp.exp(log_a)
    m = jnp.sqrt(-jnp.tanh(log_a) * (a * a + 1.0))
    return xcb, r, i, sp, a, m


def _rg_specs(B, S, rows, order):
    nc = S // rows

    def idx(fn):
        def index_map(*ids):
            v = dict(zip(order.lower(), ids))
            c = (nc - 1 - v["c"]) if "C" in order else v["c"]
            return fn(v["b"], v["d"], c)
        return index_map

    return dict(
        proj=pl.BlockSpec((2, 1, rows, RG_COLS), idx(lambda b, d, c: (0, b, c, d))),
        act=pl.BlockSpec((1, rows, RG_COLS), idx(lambda b, d, c: (b, c, d))),
        taps=pl.BlockSpec((CONV_WIDTH, RG_COLS), idx(lambda b, d, c: (0, d))),
        vec=pl.BlockSpec((1, RG_COLS), idx(lambda b, d, c: (0, d))),
        gate=pl.BlockSpec((1, RG_COLS, RG_COLS), idx(lambda b, d, c: (d, 0, 0))),
    )


def rglru_fwd(proj, conv_w, conv_b, wa, ba, wx, bx, lam, name, rows=RG_ROWS, ride=None):
    _, B, S, _ = proj.shape
    rows = min(rows, S)
    sp_ = _rg_specs(B, S, rows, "bdc")

    def body(p_ref, cw_ref, cb_ref, wa_ref, ba_ref, wx_ref, bx_ref, lam_ref,
             xc_ref, hs_ref, hp_ref, gt_ref, ext_ref, a_ref, u_ref, xcar_ref, hcar_ref):
        @pl.when(pl.program_id(2) == 0)
        def _():
            xcar_ref[...] = jnp.zeros_like(xcar_ref)
            hcar_ref[...] = jnp.zeros_like(hcar_ref)

        xr = p_ref[0, 0]
        ext_ref[0:8, :] = xcar_ref[...]
        ext_ref[8:, :] = xr
        xcar_ref[...] = xr[rows - 8:, :]
        xc = ext_ref[pl.ds(5, rows), :] * cw_ref[0:1, :]
        xc = xc + ext_ref[pl.ds(6, rows), :] * cw_ref[1:2, :]
        xc = xc + ext_ref[pl.ds(7, rows), :] * cw_ref[2:3, :]
        xc = xc + xr * cw_ref[3:4, :] + cb_ref[...]
        xc_ref[0] = xc
        _, _, i, _, a, m = _rg_gates(xc, wa_ref, ba_ref, wx_ref, bx_ref, lam_ref)
        a_ref[...] = a
        u_ref[...] = m * (i * xc)

        def step(r0, carry):
            h = _group_scan(a_ref[pl.ds(r0, 8), :], u_ref[pl.ds(r0, 8), :], carry)
            row = lax.broadcasted_iota(jnp.int32, h.shape, 0)
            hs_ref[0, pl.ds(r0, 8), :] = h
            hp_ref[0, pl.ds(r0, 8), :] = jnp.where(row == 0, carry, pltpu.roll(h, 1, 0))
            return h[7:8, :]

        hcar_ref[0:1, :] = _scan_rows(rows, step, hcar_ref[0:1, :])
        gt_ref[0] = (hs_ref[0] * _silu(p_ref[1, 0])).astype(BF16)

    act = jax.ShapeDtypeStruct((B, S, RG_WIDTH), F32)
    return _call(
        body, name=name, grid=(B, RG_GROUPS, S // rows),
        in_specs=[sp_["proj"], sp_["taps"], sp_["vec"], sp_["gate"], sp_["vec"], sp_["gate"], sp_["vec"], sp_["vec"]],
        out_specs=[sp_["act"]] * 4,
        out_shape=[act, act, act, jax.ShapeDtypeStruct((B, S, RG_WIDTH), BF16)],
        scratch_shapes=[pltpu.VMEM((rows + 8, RG_COLS), F32), pltpu.VMEM((rows, RG_COLS), F32),
                        pltpu.VMEM((rows, RG_COLS), F32), pltpu.VMEM((8, RG_COLS), F32), pltpu.VMEM((8, RG_COLS), F32)],
        semantics=("parallel", "parallel", "arbitrary"), args=[proj, conv_w, conv_b, wa, ba, wx, bx, lam], ride=ride)


def rglru_bwd(proj, xc, hs, hprev, dgated, conv_w, wa, ba, wx, bx, lam, name, rows=RG_ROWS, ride=None):
    _, B, S, _ = proj.shape
    rows = min(rows, S)
    sp_ = _rg_specs(B, S, rows, "dbC")

    def body(p_ref, xc_ref, hs_ref, hp_ref, dgt_ref, cw_ref, wa_ref, ba_ref, wx_ref, bx_ref, lam_ref,
             dp_ref, dcw_ref, dcb_ref, dwa_ref, dba_ref, dwx_ref, dbx_ref, dlam_ref,
             ext_ref, c_ref, l_ref, acar_ref, lcar_ref, dcar_ref):
        b_, c_ = pl.program_id(1), pl.program_id(2)

        @pl.when(c_ == 0)
        def _():
            acar_ref[...] = jnp.zeros_like(acar_ref)
            lcar_ref[...] = jnp.zeros_like(lcar_ref)
            dcar_ref[...] = jnp.zeros_like(dcar_ref)

        @pl.when((c_ == 0) & (b_ == 0))
        def _():
            for ref in (dcw_ref, dcb_ref, dwa_ref, dba_ref, dwx_ref, dbx_ref, dlam_ref):
                ref[...] = jnp.zeros_like(ref)

        xr, g = p_ref[0, 0], p_ref[1, 0]
        xc_v = xc_ref[0]
        xcb, r, i, sp, a, m = _rg_gates(xc_v, wa_ref, ba_ref, wx_ref, bx_ref, lam_ref)
        dhs, dg = _gate_bwd(dgt_ref[0], hs_ref[0], g)
        dp_ref[1, 0] = dg.astype(BF16)

        ext_ref[0:rows, :] = a
        ext_ref[rows:, :] = acar_ref[...]
        acar_ref[...] = a[0:8, :]
        c_ref[...] = ext_ref[pl.ds(1, rows), :]
        l_ref[...] = dhs

        def step(r0, carry):
            lam_g = _group_scan(c_ref[pl.ds(r0, 8), :], l_ref[pl.ds(r0, 8), :], carry, reverse=True)
            l_ref[pl.ds(r0, 8), :] = lam_g
            return lam_g[0:1, :]

        lcar_ref[0:1, :] = _scan_rows(rows, step, lcar_ref[0:1, :], reverse=True)
        du = l_ref[...]
        da = du * hp_ref[0]
        dlog_a = da * a - (du * (i * xc_v)) * (a * a / m)
        dr = dlog_a * (-RG_C * sp)
        dsp = jnp.sum(dlog_a * (-RG_C * r), axis=0, keepdims=True)
        dlam_ref[...] += dsp * (-_sigmoid(-lam_ref[...]))
        dpa = dr * (r * (1.0 - r))
        dpx = (du * (m * xc_v)) * (i * (1.0 - i))
        dba_ref[...] += jnp.sum(dpa, axis=0, keepdims=True)
        dbx_ref[...] += jnp.sum(dpx, axis=0, keepdims=True)
        dpa, dpx = dpa.astype(BF16), dpx.astype(BF16)
        dwa_ref[0] += _dot_tn(xcb, dpa)
        dwx_ref[0] += _dot_tn(xcb, dpx)
        dxc = du * (m * i) + _dot_nt(dpa, wa_ref[0]) + _dot_nt(dpx, wx_ref[0])

        dcb_ref[...] += jnp.sum(dxc, axis=0, keepdims=True)
        ext_ref[0:rows, :] = dxc
        ext_ref[rows:, :] = dcar_ref[...]
        dcar_ref[...] = dxc[0:8, :]
        dxr = jnp.zeros_like(dxc)
        for k in range(CONV_WIDTH):
            tap = CONV_WIDTH - 1 - k
            ahead = dxc if k == 0 else ext_ref[pl.ds(k, rows), :]
            dxr = dxr + ahead * cw_ref[tap:tap + 1, :]
            dcw_ref[tap:tap + 1, :] += jnp.sum(xr * ahead, axis=0, keepdims=True)
        dp_ref[0, 0] = dxr.astype(BF16)

    vec = jax.ShapeDtypeStruct((1, RG_WIDTH), F32)
    gate = jax.ShapeDtypeStruct((RG_GROUPS, RG_COLS, RG_COLS), F32)
    return _call(
        body, name=name, grid=(RG_GROUPS, B, S // rows),
        in_specs=[sp_["proj"], sp_["act"], sp_["act"], sp_["act"], sp_["act"], sp_["taps"],
                  sp_["gate"], sp_["vec"], sp_["gate"], sp_["vec"], sp_["vec"]],
        out_specs=[sp_["proj"], sp_["taps"], sp_["vec"], sp_["gate"], sp_["vec"], sp_["gate"], sp_["vec"], sp_["vec"]],
        out_shape=[jax.ShapeDtypeStruct(proj.shape, BF16), jax.ShapeDtypeStruct((CONV_WIDTH, RG_WIDTH), F32), vec,
                   gate, vec, gate, vec, vec],
        scratch_shapes=[pltpu.VMEM((rows + 8, RG_COLS), F32), pltpu.VMEM((rows, RG_COLS), F32),
                        pltpu.VMEM((rows, RG_COLS), F32), pltpu.VMEM((8, RG_COLS), F32),
                        pltpu.VMEM((8, RG_COLS), F32), pltpu.VMEM((8, RG_COLS), F32)],
        semantics=("arbitrary", "arbitrary", "arbitrary"),
        args=[proj, xc, hs, hprev, dgated, conv_w, wa, ba, wx, bx, lam], ride=ride)


def block_diag_gates(w):
    per = RG_COLS // RG_BLOCK
    w4 = w.reshape(RG_GROUPS, per, RG_BLOCK, RG_BLOCK)
    return jnp.einsum("dipq,ij->dipjq", w4, jnp.eye(per, dtype=w.dtype)).reshape(RG_GROUPS, RG_COLS, RG_COLS)


def block_diag_gates_t(dw):
    per = RG_COLS // RG_BLOCK
    dw6 = dw.reshape(RG_GROUPS, per, RG_BLOCK, per, RG_BLOCK)
    return jnp.stack([dw6[:, i, :, i, :] for i in range(per)], axis=1).reshape(RG_BLOCKS, RG_BLOCK, RG_BLOCK)


def adamw(w, parts, m, v, name, layer=0, prev=None, row_tile=ROW_TILE):
    L, R, C = w.shape
    n_parts = parts.shape[0]
    br = row_tile if R % row_tile == 0 else R

    def body(w_ref, p_ref, m_ref, v_ref, *refs):
        g_ref, d_ref, nm_ref, nv_ref = refs[-4:]
        g = p_ref[0].astype(F32)
        for k in range(1, n_parts):
            g = g + p_ref[k].astype(F32)
        nm = ADAM_B1 * m_ref[0] + (1.0 - ADAM_B1) * g
        nv = ADAM_B2 * v_ref[0] + (1.0 - ADAM_B2) * (g * g)
        m_hat = nm / (1.0 - ADAM_B1 ** ADAM_STEP)
        v_hat = nv / (1.0 - ADAM_B2 ** ADAM_STEP)
        g_ref[0] = g
        d_ref[0] = -ADAM_LR * (m_hat / (jnp.sqrt(v_hat) + ADAM_EPS) + ADAM_WD * w_ref[0])
        nm_ref[0] = nm
        nv_ref[0] = nv

    slab = pl.BlockSpec((1, br, C), lambda i: (layer, i, 0))
    out = jax.ShapeDtypeStruct((L, R, C), F32)
    carried = [] if prev is None else list(prev)
    return _call(
        body, name=name, grid=(R // br,),
        in_specs=[slab, pl.BlockSpec((n_parts, br, C), lambda i: (0, i, 0)), slab, slab]
        + [pl.BlockSpec(memory_space=pl.ANY)] * len(carried),
        out_specs=[slab] * 4, out_shape=[out] * 4, semantics=("parallel",), args=[w, parts, m, v] + carried,
        aliases={4 + k: k for k in range(len(carried))})


def _seq(a, B):
    return a.reshape(a.shape[:-2] + (B, a.shape[-2] // B, a.shape[-1]))


def _flat(a):
    return a.reshape(a.shape[:-3] + (a.shape[-3] * a.shape[-2], a.shape[-1]))


def _tiles(w, which, **default):
    return dict(default, **w.get("tiles", {}).get(which, {}))


def mixer_a_fwd(h, w, B, tag, rides):
    proj = matmul(h, w["w_in"], mode="nn", out_dtype=F32, name=f"{tag}_proj", out_slabs=4,
                  ride=rides.pop(f"{tag}_proj", None), **_tiles(w, "proj"))
    o, gated = attn_a_fwd(_seq(proj, B), w["bias"], f"{tag}_attn", ride=rides.pop(f"{tag}_attn", None))
    return _flat(gated), dict(proj=proj, o=o)


def mixer_a_bwd(dgated, w, saved, B, tag, rides):
    dproj, dbias = attn_a_bwd(_seq(saved["proj"], B), w["bias"], saved["o"], _seq(dgated, B), f"{tag}_attn_bwd",
                              ride=rides.pop(f"{tag}_attn_bwd", None))
    return _flat(dproj), dict(bias=dbias)


def mixer_b_fwd(h, w, B, tag, rides):
    proj = matmul(h, w["w_in"], mode="nn", out_dtype=F32, name=f"{tag}_proj", out_slabs=2, bn=RG_COLS,
                  ride=rides.pop(f"{tag}_proj", None))
    xc, hs, hprev, gated = rglru_fwd(_seq(proj, B), w["conv_w"], w["conv_b"], w["wa"], w["ba"], w["wx"], w["bx"],
                                     w["lam"], f"{tag}_rglru", ride=rides.pop(f"{tag}_rglru", None))
    return _flat(gated), dict(proj=proj, xc=xc, hs=hs, hprev=hprev)


def mixer_b_bwd(dgated, w, saved, B, tag, rides):
    dproj, dcw, dcb, dwa, dba, dwx, dbx, dlam = rglru_bwd(
        _seq(saved["proj"], B), saved["xc"], saved["hs"], saved["hprev"], _seq(dgated, B),
        w["conv_w"], w["wa"], w["ba"], w["wx"], w["bx"], w["lam"], f"{tag}_rglru_bwd",
        ride=rides.pop(f"{tag}_rglru_bwd", None))
    return _flat(dproj), dict(conv_w=dcw, conv_b=dcb, wa=dwa, ba=dba, wx=dwx, bx=dbx, lam=dlam)


def mixer_c_fwd(h, w, B, tag, rides):
    proj = matmul(h, w["w_in"], mode="nn", out_dtype=F32, name=f"{tag}_proj", out_slabs=4,
                  ride=rides.pop(f"{tag}_proj", None), **_tiles(w, "proj"))
    f_logit = matmul(h, w["w_f"], mode="nn", out_dtype=F32, name=f"{tag}_fproj")
    cum = fox_cum_fwd(_seq(f_logit, B), w["f_bias"], f"{tag}_cum")
    cum_row = cum[:, :, :HEADS].transpose(0, 2, 1)
    o, gated = fox_fwd(_seq(proj, B), cum, cum_row, f"{tag}_attn", ride=rides.pop(f"{tag}_attn", None))
    return _flat(gated), dict(proj=proj, o=o, f_logit=f_logit, cum=cum, cum_row=cum_row)


def mixer_c_bwd(dgated, w, saved, B, tag, rides):
    dproj, dck = fox_bwd(_seq(saved["proj"], B), saved["cum"], saved["cum_row"], saved["o"], _seq(dgated, B),
                         f"{tag}_attn_bwd", ride=rides.pop(f"{tag}_attn_bwd", None))
    S = dck.shape[-1]
    dcum = jnp.pad(-dck.reshape(B, HEADS, S).transpose(0, 2, 1), ((0, 0), (0, 0), (0, HEAD_DIM - HEADS)))
    df, dfb = fox_cum_bwd(_seq(saved["f_logit"], B), w["f_bias"], dcum, f"{tag}_cum_bwd")
    return _flat(dproj), dict(f_bias=dfb, df=_flat(df).astype(BF16))


MIXERS = {"a": (mixer_a_fwd, mixer_a_bwd), "b": (mixer_b_fwd, mixer_b_bwd), "c": (mixer_c_fwd, mixer_c_bwd)}
LAYER_KINDS = "abca"


def local_step(x, target, norm_pre, norm_post, get_layer, rides, on_grads):
    B, S, Dm = x.shape
    n_layers = len(LAYER_KINDS)
    xs = [x.reshape(B * S, Dm)]
    saved, layers = [], []
    for li, kind in enumerate(LAYER_KINDS):
        tag = f"l{li}{kind}"
        h = prenorm_fwd(xs[-1], norm_pre[li:li + 1], f"{tag}_prenorm")
        w = get_layer(li)
        gated, sv = MIXERS[kind][0](h, w, B, tag, rides)
        y = matmul(gated, w["w_out"], mode="nn", out_dtype=F32, name=f"{tag}_out", ride=rides.pop(f"{tag}_out", None))
        xs.append(postnorm_fwd(xs[-1], y, norm_post[li:li + 1], f"{tag}_postnorm"))
        saved.append(dict(sv, h=h, gated=gated, y=y))
        layers.append(w)
    loss, dx = loss_fwd_bwd(xs[-1], target.reshape(B * S, Dm), "loss")

    for li in reversed(range(n_layers)):
        kind, w, sv = LAYER_KINDS[li], layers[li], saved[li]
        tag = f"l{li}{kind}"
        dy, dg_post = postnorm_bwd(sv["y"], norm_post[li:li + 1], dx, f"{tag}_postnorm_bwd")
        on_grads(li, "norm_post", dg_post)
        on_grads(li, "w_out", matmul(sv["gated"], dy, mode="tn", out_dtype=BF16, name=f"{tag}_dwout",
                                     ride=rides.pop(f"{tag}_dwout", None)))
        dgated = matmul(dy, w["w_out"], mode="nt", out_dtype=F32, name=f"{tag}_dgated",
                        ride=rides.pop(f"{tag}_dgated", None))
        dproj, gw = MIXERS[kind][1](dgated, w, sv, B, tag, rides)
        df = gw.pop("df", None)
        for name, value in gw.items():
            on_grads(li, name, value)
        on_grads(li, "w_in", matmul(sv["h"], dproj, mode="tn", out_dtype=BF16, name=f"{tag}_dwin",
                                    out_slabs=w["grad_slabs"], ride=rides.pop(f"{tag}_dwin", None),
                                    **_tiles(w, "dwin")))
        dhs = [matmul(dproj, w["w_in"], mode="nt", out_dtype=F32, name=f"{tag}_dh",
                      ride=rides.pop(f"{tag}_dh", None), **_tiles(w, "dh"))]
        if df is not None:
            on_grads(li, "w_f", matmul(sv["h"], df, mode="tn", out_dtype=BF16, name=f"{tag}_dwf"))
            dhs.append(matmul(df, w["w_f"], mode="nt", out_dtype=F32, name=f"{tag}_dhf"))
        dx, dg_pre = prenorm_bwd(xs[li], norm_pre[li:li + 1], dhs, dx, f"{tag}_prenorm_bwd")
        on_grads(li, "norm_pre", dg_pre)
    assert not rides, list(rides)
    return loss, dx.reshape(B, S, Dm)


WEIGHTS = ("norm_pre", "norm_post", "a_w_in", "a_rel_bias", "a_w_out", "b_w_in", "b_conv_w", "b_conv_b",
           "b_gate_a_w", "b_gate_a_b", "b_gate_x_w", "b_gate_x_b", "b_lambda", "b_w_out", "c_w_in", "c_f_bias",
           "c_w_out")
C_SHARD = (4 * D_MODEL + HEADS) // N_DEV


def _rows(gathered):
    return gathered.reshape(gathered.shape[0] * gathered.shape[1], gathered.shape[2])


def layer_a(w_in, w_out, rel_bias):
    return dict(w_in=w_in, w_out=_rows(w_out), bias=band_bias(rel_bias), grad_slabs=N_DEV)


def layer_b(w_in, w_out, conv_w, small):
    return dict(
        w_in=w_in, w_out=_rows(w_out), grad_slabs=N_DEV,
        conv_w=conv_w.transpose(1, 0, 2).reshape(CONV_WIDTH, RG_WIDTH),
        conv_b=small["b_conv_b"], lam=small["b_lambda"],
        wa=block_diag_gates(small["b_gate_a_w"][0]).astype(BF16), ba=small["b_gate_a_b"].reshape(1, RG_WIDTH),
        wx=block_diag_gates(small["b_gate_x_w"][0]).astype(BF16), bx=small["b_gate_x_b"].reshape(1, RG_WIDTH))


def layer_c(w_in, w_out, small):
    full = w_in.transpose(1, 0, 2).reshape(D_MODEL, N_DEV * C_SHARD)
    return dict(w_in=full[:, :4 * D_MODEL], w_f=jnp.pad(full[:, 4 * D_MODEL:], ((0, 0), (0, HEAD_DIM - HEADS))),
                w_out=_rows(w_out), grad_slabs=1,
                f_bias=jnp.pad(small["c_f_bias"], ((0, 0), (0, HEAD_DIM - HEADS))))


def c_w_in_blocks(dmain, df):
    full = jnp.concatenate([dmain, df[:, :HEADS].astype(dmain.dtype)], axis=1)
    return full.reshape(D_MODEL, N_DEV, C_SHARD).transpose(1, 0, 2)


def _row_blocks(g):
    return g.reshape(N_DEV, g.shape[0] // N_DEV, g.shape[1])


PACK_LANES = 128
PACK_ALIGN = 8 * PACK_LANES


def pack(parts):
    flat = []
    for p in parts:
        n = p.size
        flat.append(jnp.pad(p.reshape(n), (0, -n % PACK_ALIGN)).reshape(-1, PACK_LANES))
    rows = sum(f.shape[0] for f in flat)
    flat.append(jnp.zeros((-rows % ROW_TILE, PACK_LANES), F32))
    return jnp.concatenate(flat, axis=0)


def unpack(packed, shapes):
    out, row = [], 0
    for shape in shapes:
        n = 1
        for s in shape:
            n *= s
        n_rows = (n + PACK_ALIGN - 1) // PACK_ALIGN * 8
        out.append(packed[row:row + n_rows].reshape(-1)[:n].reshape(shape))
        row += n_rows
    return out


LATE = (("a_rel_bias", slice(0, 1)), ("norm_pre", slice(0, 1)), ("norm_post", slice(0, 1)))
EARLY = (("a_rel_bias", slice(1, 2)), ("norm_pre", slice(1, 4)), ("norm_post", slice(1, 4)),
         ("b_conv_b", slice(None)), ("b_gate_a_w", slice(None)), ("b_gate_a_b", slice(None)),
         ("b_gate_x_w", slice(None)), ("b_gate_x_b", slice(None)), ("b_lambda", slice(None)),
         ("c_f_bias", slice(None)))


def _pieces(tree, pieces):
    return [tree[name][sl] for name, sl in pieces]


def kernel(x, norm_pre, norm_post, a_w_in, a_rel_bias, a_w_out, b_w_in, b_conv_w, b_conv_b, b_gate_a_w, b_gate_a_b, b_gate_x_w, b_gate_x_b, b_lambda, b_w_out, c_w_in, c_f_bias, c_w_out, loss_target, m_norm_pre, m_norm_post, m_a_w_in, m_a_rel_bias, m_a_w_out, m_b_w_in, m_b_conv_w, m_b_conv_b, m_b_gate_a_w, m_b_gate_a_b, m_b_gate_x_w, m_b_gate_x_b, m_b_lambda, m_b_w_out, m_c_w_in, m_c_f_bias, m_c_w_out, v_norm_pre, v_norm_post, v_a_w_in, v_a_rel_bias, v_a_w_out, v_b_w_in, v_b_conv_w, v_b_conv_b, v_b_gate_a_w, v_b_gate_a_b, v_b_gate_x_w, v_b_gate_x_b, v_b_lambda, v_b_w_out, v_c_w_in, v_c_f_bias, v_c_w_out):
    args = dict(locals())
    w = {n: args[n] for n in WEIGHTS}
    m = {n: args["m_" + n] for n in WEIGHTS}
    v = {n: args["v_" + n] for n in WEIGHTS}

    a_in, a_out = a_w_in.astype(BF16), a_w_out.astype(BF16)
    gather_a0 = Ride([a_in[0], a_out[0]], scatter=False)
    gather_b = Ride([b_w_in[0].astype(BF16), b_w_out[0].astype(BF16), b_conv_w[0]], scatter=False)
    gather_c_in = Ride([c_w_in[0].astype(BF16)], scatter=False)
    gather_c_out = Ride([c_w_out[0].astype(BF16)], scatter=False)
    gather_a1 = Ride([a_in[1], a_out[1]], scatter=False)
    exchange(gather_a0, "gather_l0")
    rides = {"l0a_proj": gather_b, "l0a_attn": gather_c_in, "l1b_rglru": gather_c_out, "l2c_attn": gather_a1}

    def get_layer(li):
        if li == 0:
            return layer_a(*gather_a0.out, a_rel_bias[0])
        if li == 1:
            return layer_b(*gather_b.out, w)
        if li == 2:
            return dict(layer_c(gather_c_in.out[0], gather_c_out.out[0], w),
                        tiles=dict(proj=dict(bn=2048)))
        return dict(layer_a(*gather_a1.out, a_rel_bias[1]), tiles=dict(dwin=dict(bk=1024)))

    grads = [dict() for _ in LAYER_KINDS]
    scatters = {}

    def rel_bias_grad(j, dbias):
        return jax.vjp(band_bias, a_rel_bias[j])[1](dbias)[0][None]

    def early_partial():
        gb, gc = grads[1], grads[2]
        tree = dict(
            a_rel_bias=jnp.concatenate([jnp.zeros((1, HEADS, N_REL), F32), rel_bias_grad(1, grads[3]["bias"])]),
            norm_pre=jnp.concatenate([jnp.zeros((1, D_MODEL), F32)] + [grads[li]["norm_pre"] for li in (1, 2, 3)]),
            norm_post=jnp.concatenate([jnp.zeros((1, D_MODEL), F32)] + [grads[li]["norm_post"] for li in (1, 2, 3)]),
            b_conv_b=gb["conv_b"], b_lambda=gb["lam"],
            b_gate_a_w=block_diag_gates_t(gb["wa"])[None], b_gate_a_b=gb["ba"].reshape(1, RG_BLOCKS, RG_BLOCK),
            b_gate_x_w=block_diag_gates_t(gb["wx"])[None], b_gate_x_b=gb["bx"].reshape(1, RG_BLOCKS, RG_BLOCK),
            c_f_bias=gc["f_bias"][:, :HEADS])
        return pack(_pieces(tree, EARLY))

    def on_grads(li, name, value):
        g = grads[li]
        g[name] = value
        if (li, name) == (3, "w_in"):
            scatters["a1"] = rides["l2c_attn_bwd"] = Ride([g["w_in"], _row_blocks(g["w_out"])], scatter=True)
        elif (li, name) == (2, "w_f"):
            scatters["c"] = rides["l1b_rglru_bwd"] = Ride(
                [c_w_in_blocks(g["w_in"], g["w_f"]), _row_blocks(g["w_out"])], scatter=True)
        elif (li, name) == (1, "w_in"):
            conv = g["conv_w"].reshape(CONV_WIDTH, N_DEV, RG_WIDTH // N_DEV).transpose(1, 0, 2)
            scatters["b"] = rides["l0a_attn_bwd"] = Ride([g["w_in"], _row_blocks(g["w_out"]), conv], scatter=True)
        elif (li, name) == (1, "norm_pre"):
            scatters["early"] = rides["l0a_dwin"] = Ride([early_partial()], scatter=False)
        elif (li, name) == (0, "w_out"):
            scatters["a0_out"] = rides["l0a_dgated"] = Ride([_row_blocks(value)], scatter=True)
        elif (li, name) == (0, "w_in"):
            scatters["a0_in"] = rides["l0a_dh"] = Ride([value], scatter=True)

    loss, grad_x = local_step(x, loss_target, norm_pre, norm_post, get_layer, rides, on_grads)
    late_tree = dict(a_rel_bias=rel_bias_grad(0, grads[0]["bias"]), norm_pre=grads[0]["norm_pre"],
                     norm_post=grads[0]["norm_post"])
    late_parts = exchange(Ride([pack([late_tree[n] for n, _ in LATE])], scatter=False), "gather_late_grads")[0]

    def sharded(name, layer_parts):
        outs = None
        for j, parts in enumerate(layer_parts):
            outs = adamw(w[name], parts, m[name], v[name], f"adamw_{name}_{j}", layer=j, prev=outs)
        return outs

    res = dict(
        a_w_in=sharded("a_w_in", [scatters["a0_in"].out[0], scatters["a1"].out[0]]),
        a_w_out=sharded("a_w_out", [scatters["a0_out"].out[0], scatters["a1"].out[1]]),
        b_w_in=sharded("b_w_in", [scatters["b"].out[0]]),
        b_w_out=sharded("b_w_out", [scatters["b"].out[1]]),
        b_conv_w=sharded("b_conv_w", [scatters["b"].out[2]]),
        c_w_in=sharded("c_w_in", [scatters["c"].out[0]]),
        c_w_out=sharded("c_w_out", [scatters["c"].out[1]]))

    packed = {}
    for label, pieces, parts in (("early", EARLY, scatters["early"].out[0]), ("late", LATE, late_parts)):
        outs = adamw(pack(_pieces(w, pieces))[None], parts, pack(_pieces(m, pieces))[None],
                     pack(_pieces(v, pieces))[None], f"adamw_replicated_{label}")
        shapes = [w[n][sl].shape for n, sl in pieces]
        packed[label] = [dict(zip([n for n, _ in pieces], unpack(o[0], shapes))) for o in outs]
    for n in ("b_conv_b", "b_gate_a_w", "b_gate_a_b", "b_gate_x_w", "b_gate_x_b", "b_lambda", "c_f_bias"):
        res[n] = [packed["early"][k][n] for k in range(4)]
    for n in ("a_rel_bias", "norm_pre", "norm_post"):
        res[n] = [jnp.concatenate([packed["late"][k][n], packed["early"][k][n]]) for k in range(4)]

    total = lax.psum(loss[0, 0], ("x", "y", "c"))
    return (total, grad_x, *[res[n][0] for n in WEIGHTS], *[res[n][1] for n in WEIGHTS],
            *[res[n][2] for n in WEIGHTS], *[res[n][3] for n in WEIGHTS])
```

```python
import functools

import jax
import jax.numpy as jnp
from jax import lax
from jax.experimental import pallas as pl
from jax.experimental.pallas import tpu as pltpu

F32 = jnp.float32
BF16 = jnp.bfloat16

N_DEV = 8
D_MODEL = 2048
HEADS = 16
HEAD_DIM = 128
CHUNK = 64
LEFT_CHUNKS = 8
REL_CLIP = 256
N_REL = 2 * REL_CLIP + 1
TQ = 256
A_PAD = LEFT_CHUNKS * CHUNK
A_KW = A_PAD + TQ
RG_WIDTH = 2560
RG_BLOCKS = 16
RG_BLOCK = 160
RG_COLS = 640
RG_GROUPS = RG_WIDTH // RG_COLS
RG_C = 8.0
CONV_WIDTH = 4
RMS_EPS = 1e-6
NEG_INF = -1e30
ADAM_LR = 0.001
ADAM_B1 = 0.9
ADAM_B2 = 0.999
ADAM_EPS = 1e-08
ADAM_WD = 0.01
ADAM_STEP = 10
VMEM_LIMIT = 56 * 1024 * 1024
MESH = pl.DeviceIdType.MESH


def _params(sem, vmem=VMEM_LIMIT):
    return pltpu.CompilerParams(dimension_semantics=sem, vmem_limit_bytes=vmem)


def _sigmoid(x):
    return 1.0 / (1.0 + jnp.exp(-x))


def _log1p(y):
    u = 1.0 + y
    return jnp.where(u == 1.0, y, jnp.log(u) * (y / jnp.where(u == 1.0, 1.0, u - 1.0)))


def _softplus(x):
    return jnp.maximum(x, 0.0) + _log1p(jnp.exp(-jnp.abs(x)))


def _dot(a, b, dims):
    return lax.dot_general(a, b, (dims, ((), ())), preferred_element_type=F32)


def _dot_nn(a, b):
    return _dot(a, b, ((1,), (0,)))


def _dot_nt(a, b):
    return _dot(a, b, ((1,), (1,)))


def _dot_tn(a, b):
    return _dot(a, b, ((0,), (0,)))


def _peers():
    x, y, c = lax.axis_index("x"), lax.axis_index("y"), lax.axis_index("c")
    me = 4 * x + 2 * y + c
    peers = []
    for k in range(1, N_DEV):
        px = 1 - x if k & 4 else x
        py = 1 - y if k & 2 else y
        pc = 1 - c if k & 1 else c
        peers.append(((px, py, pc), 4 * px + 2 * py + pc))
    return me, peers


class Ride:
    def __init__(self, arrs, scatter, via_sibling=False):
        assert not (scatter and via_sibling)
        self.arrs, self.scatter, self.via_sibling, self.out = list(arrs), scatter, via_sibling, None

    def out_shapes(self):
        return [jax.ShapeDtypeStruct(a.shape if self.scatter else (N_DEV,) + a.shape, a.dtype) for a in self.arrs]

    def sem_shapes(self):
        n = len(self.arrs)
        return [pltpu.SemaphoreType.DMA((n, N_DEV - 1)), pltpu.SemaphoreType.DMA((n, N_DEV - 1)),
                pltpu.SemaphoreType.DMA((n,))]

    def _copies(self, ins, outs, sems, landing):
        send_sems, recv_sems, local_sems = sems
        me, peers = _peers()
        local, remote = [], []
        for a, (src, dst) in enumerate(zip(ins, outs)):
            local.append(pltpu.make_async_copy(src.at[me] if self.scatter else src, dst.at[me], local_sems.at[a]))
            for k, (peer, peer_idx) in enumerate(peers):
                remote.append(pltpu.make_async_remote_copy(
                    src_ref=src.at[peer_idx] if self.scatter else src, dst_ref=dst.at[peer_idx if landing else me],
                    send_sem=send_sems.at[a, k], recv_sem=recv_sems.at[a, k], device_id=peer, device_id_type=MESH))
        return local, remote

    def _direct(self, k):
        return not self.via_sibling or k == 0 or (k + 1) % 2 == 0

    def start(self, ins, outs, sems):
        local, remote = self._copies(ins, outs, sems, landing=False)
        n_peers = N_DEV - 1
        for cp in local + [cp for i, cp in enumerate(remote) if self._direct(i % n_peers)]:
            cp.start()

    def wait(self, ins, outs, sems):
        local, remote = self._copies(ins, outs, sems, landing=True)
        n_peers = N_DEV - 1
        for i, cp in enumerate(remote):
            if self._direct(i % n_peers):
                cp.wait()
        if self.via_sibling:
            send_sems, recv_sems, _ = sems
            me, peers = _peers()
            sibling = peers[0][0]
            passed = []
            for a, dst in enumerate(outs):
                for j in range(1, n_peers, 2):
                    came, lands = peers[j][1], peers[j + 1][1]
                    pltpu.make_async_remote_copy(
                        src_ref=dst.at[came], dst_ref=dst.at[came], send_sem=send_sems.at[a, j + 1],
                        recv_sem=recv_sems.at[a, j + 1], device_id=sibling, device_id_type=MESH).start()
                    passed.append(pltpu.make_async_remote_copy(
                        src_ref=dst.at[came], dst_ref=dst.at[lands], send_sem=send_sems.at[a, j + 1],
                        recv_sem=recv_sems.at[a, j + 1], device_id=sibling, device_id_type=MESH))
            for cp in passed:
                cp.wait()
        for cp in local:
            cp.wait()


def _call(body, *, name, grid, in_specs, out_specs, out_shape, args, scratch_shapes=(), semantics=None, ride=None,
          aliases=None):
    scratch_shapes = list(scratch_shapes)
    if ride is None:
        return pl.pallas_call(
            body, name=name, grid=grid, in_specs=in_specs, out_specs=out_specs, out_shape=out_shape,
            scratch_shapes=scratch_shapes, input_output_aliases=aliases or {},
            compiler_params=_params(semantics if grid else None))(*args)
    assert not aliases
    n_in, n_out, n_sc, n_r = len(in_specs), len(out_specs), len(scratch_shapes), len(ride.arrs)

    def riding(*refs):
        ins, r_ins = refs[:n_in], refs[n_in:n_in + n_r]
        outs, r_outs = refs[n_in + n_r:n_in + n_r + n_out], refs[n_in + n_r + n_out:n_in + 2 * n_r + n_out]
        rest = refs[n_in + 2 * n_r + n_out:]
        scratch, sems = rest[:n_sc], rest[n_sc:]
        first = last = None
        for axis, size in enumerate(grid):
            pid = pl.program_id(axis)
            first = (pid == 0) if first is None else first & (pid == 0)
            last = (pid == size - 1) if last is None else last & (pid == size - 1)
        if grid:
            pl.when(first)(lambda: ride.start(r_ins, r_outs, sems))
        else:
            ride.start(r_ins, r_outs, sems)
        body(*ins, *outs, *scratch)
        if grid:
            pl.when(last)(lambda: ride.wait(r_ins, r_outs, sems))
        else:
            ride.wait(r_ins, r_outs, sems)

    any_spec = pl.BlockSpec(memory_space=pl.ANY)
    res = pl.pallas_call(
        riding, name=name, grid=grid, in_specs=list(in_specs) + [any_spec] * n_r,
        out_specs=list(out_specs) + [any_spec] * n_r, out_shape=list(out_shape) + ride.out_shapes(),
        scratch_shapes=scratch_shapes + ride.sem_shapes(),
        compiler_params=_params(("arbitrary",) * len(grid) if grid else None))(*args, *ride.arrs)
    ride.out = list(res[n_out:])
    return list(res[:n_out])


def exchange(ride, name):
    _call(lambda: None, name=name, grid=(), in_specs=[], out_specs=[], out_shape=[], args=[], ride=ride)
    return ride.out


LANES = 128


def _fit(dims, want):
    dims = tuple(dims)
    if len(set(dims)) == 1 and dims[0] <= want:
        return dims[0]
    return max(t for t in range(LANES, want + 1, LANES) if all(d % t == 0 for d in dims))


def _cols(arr):
    return arr.shape[-1] * (arr.shape[0] if len(arr.shape) == 3 else 1)


def _tile_spec(shape, rblk, cblk, rc):
    if len(shape) == 2:
        return pl.BlockSpec((rblk, cblk), rc)
    per = shape[2] // cblk

    def index_map(*ids):
        r, c = rc(*ids)
        return (c // per, r, c % per)

    return pl.BlockSpec((1, rblk, cblk), index_map)


def matmul(a, b, *, mode, out_dtype, name, bm=1024, bn=1024, bk=2048, out_slabs=1, ride=None):
    a_rows, a_cols, b_rows, b_cols = a.shape[-2], _cols(a), b.shape[-2], _cols(b)
    (K, M) = (a_rows, a_cols) if mode == "tn" else (a_cols, a_rows)
    N = b_rows if mode == "nt" else b_cols
    assert K == (b_cols if mode == "nt" else b_rows), (name, a.shape, b.shape)
    out_shape = (M, N) if out_slabs == 1 else (out_slabs, M, N // out_slabs)
    widths = dict(m=[M], n=[N, out_shape[-1]], k=[K])
    widths["m" if mode == "tn" else "k"].append(a.shape[-1])
    widths["k" if mode == "nt" else "n"].append(b.shape[-1])
    bm, bn, bk = _fit(widths["m"], bm), _fit(widths["n"], bn), _fit(widths["k"], bk)
    nk = K // bk
    dims = {"nn": ((1,), (0,)), "nt": ((1,), (1,)), "tn": ((0,), (0,))}[mode]

    def val(ref):
        return ref[0] if len(ref.shape) == 3 else ref[...]

    def put(ref, x):
        if len(ref.shape) == 3:
            ref[0] = x.astype(ref.dtype)
        else:
            ref[...] = x.astype(ref.dtype)

    def body(a_ref, b_ref, o_ref, *scratch):
        if nk == 1:
            put(o_ref, _dot(val(a_ref), val(b_ref), dims))
            return
        acc_ref, = scratch
        k = pl.program_id(2)

        @pl.when(k == 0)
        def _():
            acc_ref[...] = jnp.zeros_like(acc_ref)

        acc_ref[...] += _dot(val(a_ref), val(b_ref), dims)

        @pl.when(k == nk - 1)
        def _():
            put(o_ref, acc_ref[...])

    if mode == "tn":
        a_spec = _tile_spec(a.shape, bk, bm, lambda j, i, k: (k, i))
    else:
        a_spec = _tile_spec(a.shape, bm, bk, lambda j, i, k: (i, k))
    if mode == "nt":
        b_spec = _tile_spec(b.shape, bn, bk, lambda j, i, k: (j, k))
    else:
        b_spec = _tile_spec(b.shape, bk, bn, lambda j, i, k: (k, j))
    return _call(
        body, name=name, grid=(N // bn, M // bm, nk), in_specs=[a_spec, b_spec],
        out_specs=[_tile_spec(out_shape, bm, bn, lambda j, i, k: (i, j))],
        out_shape=[jax.ShapeDtypeStruct(out_shape, out_dtype)],
        scratch_shapes=[] if nk == 1 else [pltpu.VMEM((bm, bn), F32)],
        semantics=("parallel", "parallel", "arbitrary"), args=[a, b], ride=ride)[0]


ROW_TILE = 256


def _rms_stats(z):
    r = lax.rsqrt(jnp.mean(z * z, axis=-1, keepdims=True) + RMS_EPS)
    return r, z * r


def _rms_bwd(n, r, g, dout):
    dn = dout * g
    return r * (dn - n * jnp.mean(dn * n, axis=-1, keepdims=True))


def _row_spec(T, Dm):
    bt = min(ROW_TILE, T)
    return bt, pl.BlockSpec((bt, Dm), lambda i: (i, 0)), pl.BlockSpec((1, Dm), lambda i: (0, 0))


def prenorm_fwd(x, g, name):
    T, Dm = x.shape
    bt, row, vec = _row_spec(T, Dm)

    def body(x_ref, g_ref, h_ref):
        _, n = _rms_stats(x_ref[...])
        h_ref[...] = (n * g_ref[...]).astype(BF16)

    return pl.pallas_call(
        body, name=name, grid=(T // bt,), in_specs=[row, vec], out_specs=row,
        out_shape=jax.ShapeDtypeStruct((T, Dm), BF16), compiler_params=_params(("parallel",)),
    )(x, g)


def postnorm_fwd(x, y, g, name):
    T, Dm = x.shape
    bt, row, vec = _row_spec(T, Dm)

    def body(x_ref, y_ref, g_ref, o_ref):
        _, n = _rms_stats(y_ref[...])
        o_ref[...] = x_ref[...] + n * g_ref[...]

    return pl.pallas_call(
        body, name=name, grid=(T // bt,), in_specs=[row, row, vec], out_specs=row,
        out_shape=jax.ShapeDtypeStruct((T, Dm), F32), compiler_params=_params(("parallel",)),
    )(x, y, g)


def loss_fwd_bwd(xf, target, name):
    T, Dm = xf.shape
    bt, row, _ = _row_spec(T, Dm)

    def body(x_ref, t_ref, l_ref, d_ref):
        @pl.when(pl.program_id(0) == 0)
        def _():
            l_ref[...] = jnp.zeros_like(l_ref)

        err = x_ref[...] - t_ref[...]
        per_tok = jnp.mean(err * err, axis=-1, keepdims=True)
        l_ref[...] += 0.5 * jnp.sum(per_tok, axis=0, keepdims=True)
        d_ref[...] = err * (1.0 / Dm)

    return pl.pallas_call(
        body, name=name, grid=(T // bt,), in_specs=[row, row],
        out_specs=[pl.BlockSpec((1, 1), lambda i: (0, 0)), row],
        out_shape=[jax.ShapeDtypeStruct((1, 1), F32), jax.ShapeDtypeStruct((T, Dm), F32)],
        compiler_params=_params(("arbitrary",)),
    )(xf, target)


def postnorm_bwd(y, g, dout, name):
    T, Dm = y.shape
    bt, row, vec = _row_spec(T, Dm)

    def body(y_ref, g_ref, d_ref, dy_ref, dg_ref):
        @pl.when(pl.program_id(0) == 0)
        def _():
            dg_ref[...] = jnp.zeros_like(dg_ref)

        r, n = _rms_stats(y_ref[...])
        dout_v = d_ref[...]
        dg_ref[...] += jnp.sum(dout_v * n, axis=0, keepdims=True)
        dy_ref[...] = _rms_bwd(n, r, g_ref[...], dout_v).astype(BF16)

    return pl.pallas_call(
        body, name=name, grid=(T // bt,), in_specs=[row, vec, row], out_specs=[row, vec],
        out_shape=[jax.ShapeDtypeStruct((T, Dm), BF16), jax.ShapeDtypeStruct((1, Dm), F32)],
        compiler_params=_params(("arbitrary",)),
    )(y, g, dout)


def prenorm_bwd(x, g, dhs, dres, name):
    T, Dm = x.shape
    bt, row, vec = _row_spec(T, Dm)
    n_dh = len(dhs)

    def body(x_ref, g_ref, *refs):
        dh_refs, (dr_ref, dx_ref, dg_ref) = refs[:n_dh], refs[n_dh:]

        @pl.when(pl.program_id(0) == 0)
        def _():
            dg_ref[...] = jnp.zeros_like(dg_ref)

        r, n = _rms_stats(x_ref[...])
        dh_v = dh_refs[0][...]
        for extra in dh_refs[1:]:
            dh_v = dh_v + extra[...]
        dg_ref[...] += jnp.sum(dh_v * n, axis=0, keepdims=True)
        dx_ref[...] = dr_ref[...] + _rms_bwd(n, r, g_ref[...], dh_v)

    return pl.pallas_call(
        body, name=name, grid=(T // bt,), in_specs=[row, vec] + [row] * (n_dh + 1), out_specs=[row, vec],
        out_shape=[jax.ShapeDtypeStruct((T, Dm), F32), jax.ShapeDtypeStruct((1, Dm), F32)],
        compiler_params=_params(("arbitrary",)),
    )(x, g, *dhs, dres)


def _silu(g):
    return g * _sigmoid(g)


def _gate_bwd(dgated, core, g):
    sg = _sigmoid(g)
    return dgated * (g * sg), dgated * core * (sg * (1.0 + g * (1.0 - sg)))


def _softmax_rows(s):
    e = jnp.exp(s - jnp.max(s, axis=-1, keepdims=True))
    return e * (1.0 / jnp.sum(e, axis=-1, keepdims=True))


def _band_scores(q, kw, bias, r0):
    s = _dot_nt(q, kw) * (HEAD_DIM ** -0.5) + bias
    j = lax.broadcasted_iota(jnp.int32, s.shape, 1)
    return jnp.where(j >= A_PAD - r0, s, NEG_INF)


def _fill_padded_kv(p_ref, kp_ref, vp_ref):
    zeros = jnp.zeros((A_PAD, HEAD_DIM), BF16)
    kp_ref[0:A_PAD, :] = zeros
    vp_ref[0:A_PAD, :] = zeros
    kp_ref[A_PAD:, :] = p_ref[1, 0].astype(BF16)
    vp_ref[A_PAD:, :] = p_ref[2, 0].astype(BF16)


def _head_specs(S, order):
    def idx(fn):
        return lambda *ids: fn(**dict(zip(order, ids)))

    return (pl.BlockSpec((4, 1, S, HEAD_DIM), idx(lambda b, h, t: (0, b, 0, h))),
            pl.BlockSpec((1, TQ, HEAD_DIM), idx(lambda b, h, t: (b, t, h))))


def attn_a_fwd(proj, bias, name, ride=None):
    _, B, S, W = proj.shape
    nt = S // TQ

    def body(p_ref, b_ref, o_ref, gt_ref, kp_ref, vp_ref):
        t = pl.program_id(2)

        @pl.when(t == 0)
        def _():
            _fill_padded_kv(p_ref, kp_ref, vp_ref)

        r0 = pl.multiple_of(t * TQ, TQ)
        q = p_ref[0, 0, pl.ds(r0, TQ), :].astype(BF16)
        g = p_ref[3, 0, pl.ds(r0, TQ), :]
        p = _softmax_rows(_band_scores(q, kp_ref[pl.ds(r0, A_KW), :], b_ref[0], r0))
        o = _dot_nn(p.astype(BF16), vp_ref[pl.ds(r0, A_KW), :])
        o_ref[0] = o
        gt_ref[0] = (o * _silu(g)).astype(BF16)

    seq, tile = _head_specs(S, "bht")
    return _call(
        body, name=name, grid=(B, HEADS, nt),
        in_specs=[seq, pl.BlockSpec((1, TQ, A_KW), lambda b, h, t: (h, 0, 0))], out_specs=[tile, tile],
        out_shape=[jax.ShapeDtypeStruct((B, S, W), F32), jax.ShapeDtypeStruct((B, S, W), BF16)],
        scratch_shapes=[pltpu.VMEM((A_PAD + S, HEAD_DIM), BF16), pltpu.VMEM((A_PAD + S, HEAD_DIM), BF16)],
        semantics=("parallel", "parallel", "arbitrary"), args=[proj, bias], ride=ride)


def attn_a_bwd(proj, bias, o, dgated, name, ride=None):
    _, B, S, W = proj.shape
    nt = S // TQ

    def body(p_ref, b_ref, o_ref, dgt_ref, dp_ref, db_ref, kp_ref, vp_ref, dk_ref, dv_ref):
        b_, t = pl.program_id(1), pl.program_id(2)

        @pl.when(t == 0)
        def _():
            _fill_padded_kv(p_ref, kp_ref, vp_ref)
            dk_ref[...] = jnp.zeros_like(dk_ref)
            dv_ref[...] = jnp.zeros_like(dv_ref)

        @pl.when((t == 0) & (b_ == 0))
        def _():
            db_ref[...] = jnp.zeros_like(db_ref)

        r0 = pl.multiple_of(t * TQ, TQ)
        rows, win = pl.ds(r0, TQ), pl.ds(r0, A_KW)
        q = p_ref[0, 0, rows, :].astype(BF16)
        g = p_ref[3, 0, rows, :]
        kw, vw = kp_ref[win, :], vp_ref[win, :]
        p = _softmax_rows(_band_scores(q, kw, b_ref[0], r0))
        do, dg = _gate_bwd(dgt_ref[0], o_ref[0], g)
        do = do.astype(BF16)
        dv_ref[win, :] += _dot_tn(p.astype(BF16), do)
        dpr = _dot_nt(do, vw)
        ds = p * (dpr - jnp.sum(p * dpr, axis=-1, keepdims=True))
        db_ref[0] += ds
        ds = (ds * (HEAD_DIM ** -0.5)).astype(BF16)
        dk_ref[win, :] += _dot_tn(ds, q)
        dp_ref[0, 0, rows, :] = _dot_nn(ds, kw).astype(BF16)
        dp_ref[3, 0, rows, :] = dg.astype(BF16)

        @pl.when(t == nt - 1)
        def _():
            dp_ref[1, 0] = dk_ref[A_PAD:, :].astype(BF16)
            dp_ref[2, 0] = dv_ref[A_PAD:, :].astype(BF16)

    seq, tile = _head_specs(S, "hbt")
    bias_spec = pl.BlockSpec((1, TQ, A_KW), lambda h, b, t: (h, 0, 0))
    return _call(
        body, name=name, grid=(HEADS, B, nt), in_specs=[seq, bias_spec, tile, tile], out_specs=[seq, bias_spec],
        out_shape=[jax.ShapeDtypeStruct(proj.shape, BF16), jax.ShapeDtypeStruct(bias.shape, F32)],
        scratch_shapes=[pltpu.VMEM((A_PAD + S, HEAD_DIM), BF16), pltpu.VMEM((A_PAD + S, HEAD_DIM), BF16),
                        pltpu.VMEM((A_PAD + S, HEAD_DIM), F32), pltpu.VMEM((A_PAD + S, HEAD_DIM), F32)],
        semantics=("arbitrary", "arbitrary", "arbitrary"), args=[proj, bias, o, dgated], ride=ride)


def band_bias(rel_bias):
    length = TQ + A_KW - 1
    first = REL_CLIP + 1 - TQ
    gen = jnp.concatenate([rel_bias[:, first:],
                           jnp.broadcast_to(rel_bias[:, 2 * REL_CLIP:], (HEADS, length - (N_REL - first)))], axis=1)
    rev = jnp.concatenate([gen[:, ::-1], jnp.zeros((HEADS, 1), rel_bias.dtype)], axis=1)
    sheared = jnp.tile(rev, (1, TQ))[:, :TQ * length].reshape(HEADS, TQ, length)
    i = lax.broadcasted_iota(jnp.int32, (TQ, A_KW), 0)
    j = lax.broadcasted_iota(jnp.int32, (TQ, A_KW), 1)
    first_key = (i // CHUNK) * CHUNK
    in_band = (j >= first_key) & (j < first_key + (LEFT_CHUNKS + 1) * CHUNK)
    return jnp.where(in_band, sheared[:, :, TQ - 1:], NEG_INF)


def _group_scan(a, u, carry, reverse=False):
    row = lax.broadcasted_iota(jnp.int32, u.shape, 0)
    for k in (1, 2, 4):
        shift = 8 - k if reverse else k
        valid = (row < 8 - k) if reverse else (row >= k)
        u_sh = pltpu.roll(u, shift, 0)
        if a is None:
            u = jnp.where(valid, u + u_sh, u)
        else:
            a_sh = pltpu.roll(a, shift, 0)
            u = jnp.where(valid, a * u_sh + u, u)
            a = jnp.where(valid, a * a_sh, a)
    return (u + carry) if a is None else (a * carry + u)


def _scan_rows(n_rows, step, carry0, reverse=False):
    groups = n_rows // 8

    def loop(i, carry):
        gi = (groups - 1 - i) if reverse else i
        return step(pl.multiple_of(gi * 8, 8), carry)

    return lax.fori_loop(0, groups, loop, carry0)


def fox_cum_fwd(f_logit, f_bias, name):
    B, S, L = f_logit.shape

    def body(f_ref, b_ref, c_ref):
        z = f_ref[0] + b_ref[...]
        c_ref[0] = jnp.minimum(z, 0.0) - _log1p(jnp.exp(-jnp.abs(z)))

        def step(r0, carry):
            h = _group_scan(None, c_ref[0, pl.ds(r0, 8), :], carry)
            c_ref[0, pl.ds(r0, 8), :] = h
            return h[7:8, :]

        _scan_rows(S, step, jnp.zeros((1, L), F32))

    return pl.pallas_call(
        body, name=name, grid=(B,),
        in_specs=[pl.BlockSpec((1, S, L), lambda b: (b, 0, 0)), pl.BlockSpec((1, L), lambda b: (0, 0))],
        out_specs=pl.BlockSpec((1, S, L), lambda b: (b, 0, 0)),
        out_shape=jax.ShapeDtypeStruct((B, S, L), F32), compiler_params=_params(("parallel",)),
    )(f_logit, f_bias)


def fox_cum_bwd(f_logit, f_bias, dcum, name):
    B, S, L = f_logit.shape

    def body(f_ref, b_ref, d_ref, df_ref, db_ref):
        @pl.when(pl.program_id(0) == 0)
        def _():
            db_ref[...] = jnp.zeros_like(db_ref)

        def step(r0, carry):
            h = _group_scan(None, d_ref[0, pl.ds(r0, 8), :], carry, reverse=True)
            df_ref[0, pl.ds(r0, 8), :] = h
            return h[0:1, :]

        _scan_rows(S, step, jnp.zeros((1, L), F32), reverse=True)
        df = df_ref[0] * _sigmoid(-(f_ref[0] + b_ref[...]))
        df_ref[0] = df
        db_ref[...] += jnp.sum(df, axis=0, keepdims=True)

    seq = pl.BlockSpec((1, S, L), lambda b: (b, 0, 0))
    vec = pl.BlockSpec((1, L), lambda b: (0, 0))
    return pl.pallas_call(
        body, name=name, grid=(B,), in_specs=[seq, vec, seq], out_specs=[seq, vec],
        out_shape=[jax.ShapeDtypeStruct((B, S, L), F32), jax.ShapeDtypeStruct((1, L), F32)],
        compiler_params=_params(("arbitrary",)),
    )(f_logit, f_bias, dcum)


def _fox_scores(q, k, cc, cr, h, r0):
    lane = lax.broadcasted_iota(jnp.int32, cc.shape, 1)
    cq = jnp.sum(jnp.where(lane == h, cc, 0.0), axis=1, keepdims=True)
    sub = lax.broadcasted_iota(jnp.int32, cr.shape, 0)
    ck = jnp.sum(jnp.where(sub == h, cr, 0.0), axis=0, keepdims=True)
    s = _dot_nt(q, k) * (HEAD_DIM ** -0.5) + (cq - ck)
    qpos = r0 + lax.broadcasted_iota(jnp.int32, s.shape, 0)
    kpos = lax.broadcasted_iota(jnp.int32, s.shape, 1)
    return jnp.where(kpos <= qpos, s, NEG_INF)


KEY_STEP = 512


def _by_causal_width(t, S, fn):
    per = KEY_STEP // TQ
    for c in range(S // KEY_STEP):
        pl.when(t // per == c)(functools.partial(fn, (c + 1) * KEY_STEP))


def fox_fwd(proj, cum_col, cum_row, name, ride=None):
    _, B, S, W = proj.shape
    nt = S // TQ

    def body(p_ref, cc_ref, cr_ref, o_ref, gt_ref, k_ref, v_ref):
        h, t = pl.program_id(1), pl.program_id(2)

        @pl.when(t == 0)
        def _():
            k_ref[...] = p_ref[1, 0].astype(BF16)
            v_ref[...] = p_ref[2, 0].astype(BF16)

        r0 = pl.multiple_of(t * TQ, TQ)
        q = p_ref[0, 0, pl.ds(r0, TQ), :].astype(BF16)
        g = p_ref[3, 0, pl.ds(r0, TQ), :]

        def tile_out(width):
            p = _softmax_rows(_fox_scores(q, k_ref[0:width, :], cc_ref[0], cr_ref[0, :, 0:width], h, r0))
            o = _dot_nn(p.astype(BF16), v_ref[0:width, :])
            o_ref[0] = o
            gt_ref[0] = (o * _silu(g)).astype(BF16)

        _by_causal_width(t, S, tile_out)

    seq, tile = _head_specs(S, "bht")
    return _call(
        body, name=name, grid=(B, HEADS, nt),
        in_specs=[seq, pl.BlockSpec((1, TQ, cum_col.shape[2]), lambda b, h, t: (b, t, 0)),
                  pl.BlockSpec((1, HEADS, S), lambda b, h, t: (b, 0, 0))],
        out_specs=[tile, tile],
        out_shape=[jax.ShapeDtypeStruct((B, S, W), F32), jax.ShapeDtypeStruct((B, S, W), BF16)],
        scratch_shapes=[pltpu.VMEM((S, HEAD_DIM), BF16), pltpu.VMEM((S, HEAD_DIM), BF16)],
        semantics=("parallel", "parallel", "arbitrary"), args=[proj, cum_col, cum_row], ride=ride)


def fox_bwd(proj, cum_col, cum_row, o, dgated, name, ride=None):
    _, B, S, W = proj.shape
    nt = S // TQ

    def body(p_ref, cc_ref, cr_ref, o_ref, dgt_ref, dp_ref, dc_ref, k_ref, v_ref, dk_ref, dv_ref):
        h, t = pl.program_id(1), pl.program_id(2)

        @pl.when(t == 0)
        def _():
            k_ref[...] = p_ref[1, 0].astype(BF16)
            v_ref[...] = p_ref[2, 0].astype(BF16)
            dk_ref[...] = jnp.zeros_like(dk_ref)
            dv_ref[...] = jnp.zeros_like(dv_ref)
            dc_ref[...] = jnp.zeros_like(dc_ref)

        r0 = pl.multiple_of(t * TQ, TQ)
        rows = pl.ds(r0, TQ)
        q = p_ref[0, 0, rows, :].astype(BF16)
        g = p_ref[3, 0, rows, :]
        do, dg = _gate_bwd(dgt_ref[0], o_ref[0], g)
        do = do.astype(BF16)
        dp_ref[3, 0, rows, :] = dg.astype(BF16)

        def tile_grads(width):
            k, v = k_ref[0:width, :], v_ref[0:width, :]
            p = _softmax_rows(_fox_scores(q, k, cc_ref[0], cr_ref[0, :, 0:width], h, r0))
            dv_ref[0:width, :] += _dot_tn(p.astype(BF16), do)
            dpr = _dot_nt(do, v)
            ds = p * (dpr - jnp.sum(p * dpr, axis=-1, keepdims=True))
            dc_ref[0, 0, :, 0:width] += jnp.sum(ds, axis=0, keepdims=True)
            ds = (ds * (HEAD_DIM ** -0.5)).astype(BF16)
            dk_ref[0:width, :] += _dot_tn(ds, q)
            dp_ref[0, 0, rows, :] = _dot_nn(ds, k).astype(BF16)

        _by_causal_width(t, S, tile_grads)

        @pl.when(t == nt - 1)
        def _():
            dp_ref[1, 0] = dk_ref[...].astype(BF16)
            dp_ref[2, 0] = dv_ref[...].astype(BF16)

    seq, tile = _head_specs(S, "bht")
    return _call(
        body, name=name, grid=(B, HEADS, nt),
        in_specs=[seq, pl.BlockSpec((1, TQ, cum_col.shape[2]), lambda b, h, t: (b, t, 0)),
                  pl.BlockSpec((1, HEADS, S), lambda b, h, t: (b, 0, 0)), tile, tile],
        out_specs=[seq, pl.BlockSpec((1, 1, 1, S), lambda b, h, t: (b, h, 0, 0))],
        out_shape=[jax.ShapeDtypeStruct(proj.shape, BF16), jax.ShapeDtypeStruct((B, HEADS, 1, S), F32)],
        scratch_shapes=[pltpu.VMEM((S, HEAD_DIM), BF16), pltpu.VMEM((S, HEAD_DIM), BF16),
                        pltpu.VMEM((S, HEAD_DIM), F32), pltpu.VMEM((S, HEAD_DIM), F32)],
        semantics=("parallel", "parallel", "arbitrary"), args=[proj, cum_col, cum_row, o, dgated], ride=ride)


RG_ROWS = 512


def _rg_gates(xc, wa_ref, ba_ref, wx_ref, bx_ref, lam_ref):
    xcb = xc.astype(BF16)
    r = _sigmoid(_dot_nn(xcb, wa_ref[0]) + ba_ref[...])
    i = _sigmoid(_dot_nn(xcb, wx_ref[0]) + bx_ref[...])
    sp = _softplus(-lam_ref[...])
    log_a = (-RG_C * sp) * r
    a = jnp.exp(log_a)
    m = jnp.sqrt(-jnp.tanh(log_a) * (a * a + 1.0))
    return xcb, r, i, sp, a, m


def _rg_specs(B, S, rows, order):
    nc = S // rows

    def idx(fn):
        def index_map(*ids):
            v = dict(zip(order.lower(), ids))
            c = (nc - 1 - v["c"]) if "C" in order else v["c"]
            return fn(v["b"], v["d"], c)
        return index_map

    return dict(
        proj=pl.BlockSpec((2, 1, rows, RG_COLS), idx(lambda b, d, c: (0, b, c, d))),
        act=pl.BlockSpec((1, rows, RG_COLS), idx(lambda b, d, c: (b, c, d))),
        taps=pl.BlockSpec((CONV_WIDTH, RG_COLS), idx(lambda b, d, c: (0, d))),
        vec=pl.BlockSpec((1, RG_COLS), idx(lambda b, d, c: (0, d))),
        gate=pl.BlockSpec((1, RG_COLS, RG_COLS), idx(lambda b, d, c: (d, 0, 0))),
    )


def rglru_fwd(proj, conv_w, conv_b, wa, ba, wx, bx, lam, name, rows=RG_ROWS, ride=None):
    _, B, S, _ = proj.shape
    rows = min(rows, S)
    sp_ = _rg_specs(B, S, rows, "bdc")

    def body(p_ref, cw_ref, cb_ref, wa_ref, ba_ref, wx_ref, bx_ref, lam_ref,
             xc_ref, hs_ref, hp_ref, gt_ref, ext_ref, a_ref, u_ref, xcar_ref, hcar_ref):
        @pl.when(pl.program_id(2) == 0)
        def _():
            xcar_ref[...] = jnp.zeros_like(xcar_ref)
            hcar_ref[...] = jnp.zeros_like(hcar_ref)

        xr = p_ref[0, 0]
        ext_ref[0:8, :] = xcar_ref[...]
        ext_ref[8:, :] = xr
        xcar_ref[...] = xr[rows - 8:, :]
        xc = ext_ref[pl.ds(5, rows), :] * cw_ref[0:1, :]
        xc = xc + ext_ref[pl.ds(6, rows), :] * cw_ref[1:2, :]
        xc = xc + ext_ref[pl.ds(7, rows), :] * cw_ref[2:3, :]
        xc = xc + xr * cw_ref[3:4, :] + cb_ref[...]
        xc_ref[0] = xc
        _, _, i, _, a, m = _rg_gates(xc, wa_ref, ba_ref, wx_ref, bx_ref, lam_ref)
        a_ref[...] = a
        u_ref[...] = m * (i * xc)

        def step(r0, carry):
            h = _group_scan(a_ref[pl.ds(r0, 8), :], u_ref[pl.ds(r0, 8), :], carry)
            row = lax.broadcasted_iota(jnp.int32, h.shape, 0)
            hs_ref[0, pl.ds(r0, 8), :] = h
            hp_ref[0, pl.ds(r0, 8), :] = jnp.where(row == 0, carry, pltpu.roll(h, 1, 0))
            return h[7:8, :]

        hcar_ref[0:1, :] = _scan_rows(rows, step, hcar_ref[0:1, :])
        gt_ref[0] = (hs_ref[0] * _silu(p_ref[1, 0])).astype(BF16)

    act = jax.ShapeDtypeStruct((B, S, RG_WIDTH), F32)
    return _call(
        body, name=name, grid=(B, RG_GROUPS, S // rows),
        in_specs=[sp_["proj"], sp_["taps"], sp_["vec"], sp_["gate"], sp_["vec"], sp_["gate"], sp_["vec"], sp_["vec"]],
        out_specs=[sp_["act"]] * 4,
        out_shape=[act, act, act, jax.ShapeDtypeStruct((B, S, RG_WIDTH), BF16)],
        scratch_shapes=[pltpu.VMEM((rows + 8, RG_COLS), F32), pltpu.VMEM((rows, RG_COLS), F32),
                        pltpu.VMEM((rows, RG_COLS), F32), pltpu.VMEM((8, RG_COLS), F32), pltpu.VMEM((8, RG_COLS), F32)],
        semantics=("parallel", "parallel", "arbitrary"), args=[proj, conv_w, conv_b, wa, ba, wx, bx, lam], ride=ride)


def rglru_bwd(proj, xc, hs, hprev, dgated, conv_w, wa, ba, wx, bx, lam, name, rows=RG_ROWS, ride=None):
    _, B, S, _ = proj.shape
    rows = min(rows, S)
    sp_ = _rg_specs(B, S, rows, "dbC")

    def body(p_ref, xc_ref, hs_ref, hp_ref, dgt_ref, cw_ref, wa_ref, ba_ref, wx_ref, bx_ref, lam_ref,
             dp_ref, dcw_ref, dcb_ref, dwa_ref, dba_ref, dwx_ref, dbx_ref, dlam_ref,
             ext_ref, c_ref, l_ref, acar_ref, lcar_ref, dcar_ref):
        b_, c_ = pl.program_id(1), pl.program_id(2)

        @pl.when(c_ == 0)
        def _():
            acar_ref[...] = jnp.zeros_like(acar_ref)
            lcar_ref[...] = jnp.zeros_like(lcar_ref)
            dcar_ref[...] = jnp.zeros_like(dcar_ref)

        @pl.when((c_ == 0) & (b_ == 0))
        def _():
            for ref in (dcw_ref, dcb_ref, dwa_ref, dba_ref, dwx_ref, dbx_ref, dlam_ref):
                ref[...] = jnp.zeros_like(ref)

        xr, g = p_ref[0, 0], p_ref[1, 0]
        xc_v = xc_ref[0]
        xcb, r, i, sp, a, m = _rg_gates(xc_v, wa_ref, ba_ref, wx_ref, bx_ref, lam_ref)
        dhs, dg = _gate_bwd(dgt_ref[0], hs_ref[0], g)
        dp_ref[1, 0] = dg.astype(BF16)

        ext_ref[0:rows, :] = a
        ext_ref[rows:, :] = acar_ref[...]
        acar_ref[...] = a[0:8, :]
        c_ref[...] = ext_ref[pl.ds(1, rows), :]
        l_ref[...] = dhs

        def step(r0, carry):
            lam_g = _group_scan(c_ref[pl.ds(r0, 8), :], l_ref[pl.ds(r0, 8), :], carry, reverse=True)
            l_ref[pl.ds(r0, 8), :] = lam_g
            return lam_g[0:1, :]

        lcar_ref[0:1, :] = _scan_rows(rows, step, lcar_ref[0:1, :], reverse=True)
        du = l_ref[...]
        da = du * hp_ref[0]
        dlog_a = da * a - (du * (i * xc_v)) * (a * a / m)
        dr = dlog_a * (-RG_C * sp)
        dsp = jnp.sum(dlog_a * (-RG_C * r), axis=0, keepdims=True)
        dlam_ref[...] += dsp * (-_sigmoid(-lam_ref[...]))
        dpa = dr * (r * (1.0 - r))
        dpx = (du * (m * xc_v)) * (i * (1.0 - i))
        dba_ref[...] += jnp.sum(dpa, axis=0, keepdims=True)
        dbx_ref[...] += jnp.sum(dpx, axis=0, keepdims=True)
        dpa, dpx = dpa.astype(BF16), dpx.astype(BF16)
        dwa_ref[0] += _dot_tn(xcb, dpa)
        dwx_ref[0] += _dot_tn(xcb, dpx)
        dxc = du * (m * i) + _dot_nt(dpa, wa_ref[0]) + _dot_nt(dpx, wx_ref[0])

        dcb_ref[...] += jnp.sum(dxc, axis=0, keepdims=True)
        ext_ref[0:rows, :] = dxc
        ext_ref[rows:, :] = dcar_ref[...]
        dcar_ref[...] = dxc[0:8, :]
        dxr = jnp.zeros_like(dxc)
        for k in range(CONV_WIDTH):
            tap = CONV_WIDTH - 1 - k
            ahead = dxc if k == 0 else ext_ref[pl.ds(k, rows), :]
            dxr = dxr + ahead * cw_ref[tap:tap + 1, :]
            dcw_ref[tap:tap + 1, :] += jnp.sum(xr * ahead, axis=0, keepdims=True)
        dp_ref[0, 0] = dxr.astype(BF16)

    vec = jax.ShapeDtypeStruct((1, RG_WIDTH), F32)
    gate = jax.ShapeDtypeStruct((RG_GROUPS, RG_COLS, RG_COLS), F32)
    return _call(
        body, name=name, grid=(RG_GROUPS, B, S // rows),
        in_specs=[sp_["proj"], sp_["act"], sp_["act"], sp_["act"], sp_["act"], sp_["taps"],
                  sp_["gate"], sp_["vec"], sp_["gate"], sp_["vec"], sp_["vec"]],
        out_specs=[sp_["proj"], sp_["taps"], sp_["vec"], sp_["gate"], sp_["vec"], sp_["gate"], sp_["vec"], sp_["vec"]],
        out_shape=[jax.ShapeDtypeStruct(proj.shape, BF16), jax.ShapeDtypeStruct((CONV_WIDTH, RG_WIDTH), F32), vec,
                   gate, vec, gate, vec, vec],
        scratch_shapes=[pltpu.VMEM((rows + 8, RG_COLS), F32), pltpu.VMEM((rows, RG_COLS), F32),
                        pltpu.VMEM((rows, RG_COLS), F32), pltpu.VMEM((8, RG_COLS), F32),
                        pltpu.VMEM((8, RG_COLS), F32), pltpu.VMEM((8, RG_COLS), F32)],
        semantics=("arbitrary", "arbitrary", "arbitrary"),
        args=[proj, xc, hs, hprev, dgated, conv_w, wa, ba, wx, bx, lam], ride=ride)


def block_diag_gates(w):
    per = RG_COLS // RG_BLOCK
    w4 = w.reshape(RG_GROUPS, per, RG_BLOCK, RG_BLOCK)
    return jnp.einsum("dipq,ij->dipjq", w4, jnp.eye(per, dtype=w.dtype)).reshape(RG_GROUPS, RG_COLS, RG_COLS)


def block_diag_gates_t(dw):
    per = RG_COLS // RG_BLOCK
    dw6 = dw.reshape(RG_GROUPS, per, RG_BLOCK, per, RG_BLOCK)
    return jnp.stack([dw6[:, i, :, i, :] for i in range(per)], axis=1).reshape(RG_BLOCKS, RG_BLOCK, RG_BLOCK)


def adamw(w, parts, m, v, name, layer=0, prev=None, row_tile=ROW_TILE):
    L, R, C = w.shape
    n_parts = parts.shape[0]
    br = row_tile if R % row_tile == 0 else R

    def body(w_ref, p_ref, m_ref, v_ref, *refs):
        g_ref, d_ref, nm_ref, nv_ref = refs[-4:]
        g = p_ref[0].astype(F32)
        for k in range(1, n_parts):
            g = g + p_ref[k].astype(F32)
        nm = ADAM_B1 * m_ref[0] + (1.0 - ADAM_B1) * g
        nv = ADAM_B2 * v_ref[0] + (1.0 - ADAM_B2) * (g * g)
        m_hat = nm / (1.0 - ADAM_B1 ** ADAM_STEP)
        v_hat = nv / (1.0 - ADAM_B2 ** ADAM_STEP)
        g_ref[0] = g
        d_ref[0] = -ADAM_LR * (m_hat / (jnp.sqrt(v_hat) + ADAM_EPS) + ADAM_WD * w_ref[0])
        nm_ref[0] = nm
        nv_ref[0] = nv

    slab = pl.BlockSpec((1, br, C), lambda i: (layer, i, 0))
    out = jax.ShapeDtypeStruct((L, R, C), F32)
    carried = [] if prev is None else list(prev)
    return _call(
        body, name=name, grid=(R // br,),
        in_specs=[slab, pl.BlockSpec((n_parts, br, C), lambda i: (0, i, 0)), slab, slab]
        + [pl.BlockSpec(memory_space=pl.ANY)] * len(carried),
        out_specs=[slab] * 4, out_shape=[out] * 4, semantics=("parallel",), args=[w, parts, m, v] + carried,
        aliases={4 + k: k for k in range(len(carried))})


def _seq(a, B):
    return a.reshape(a.shape[:-2] + (B, a.shape[-2] // B, a.shape[-1]))


def _flat(a):
    return a.reshape(a.shape[:-3] + (a.shape[-3] * a.shape[-2], a.shape[-1]))


def _tiles(w, which, **default):
    return dict(default, **w.get("tiles", {}).get(which, {}))


def mixer_a_fwd(h, w, B, tag, rides):
    proj = matmul(h, w["w_in"], mode="nn", out_dtype=F32, name=f"{tag}_proj", out_slabs=4,
                  ride=rides.pop(f"{tag}_proj", None), **_tiles(w, "proj"))
    o, gated = attn_a_fwd(_seq(proj, B), w["bias"], f"{tag}_attn", ride=rides.pop(f"{tag}_attn", None))
    return _flat(gated), dict(proj=proj, o=o)


def mixer_a_bwd(dgated, w, saved, B, tag, rides):
    dproj, dbias = attn_a_bwd(_seq(saved["proj"], B), w["bias"], saved["o"], _seq(dgated, B), f"{tag}_attn_bwd",
                              ride=rides.pop(f"{tag}_attn_bwd", None))
    return _flat(dproj), dict(bias=dbias)


def mixer_b_fwd(h, w, B, tag, rides):
    proj = matmul(h, w["w_in"], mode="nn", out_dtype=F32, name=f"{tag}_proj", out_slabs=2, bn=RG_COLS,
                  ride=rides.pop(f"{tag}_proj", None))
    xc, hs, hprev, gated = rglru_fwd(_seq(proj, B), w["conv_w"], w["conv_b"], w["wa"], w["ba"], w["wx"], w["bx"],
                                     w["lam"], f"{tag}_rglru", ride=rides.pop(f"{tag}_rglru", None))
    return _flat(gated), dict(proj=proj, xc=xc, hs=hs, hprev=hprev)


def mixer_b_bwd(dgated, w, saved, B, tag, rides):
    dproj, dcw, dcb, dwa, dba, dwx, dbx, dlam = rglru_bwd(
        _seq(saved["proj"], B), saved["xc"], saved["hs"], saved["hprev"], _seq(dgated, B),
        w["conv_w"], w["wa"], w["ba"], w["wx"], w["bx"], w["lam"], f"{tag}_rglru_bwd",
        ride=rides.pop(f"{tag}_rglru_bwd", None))
    return _flat(dproj), dict(conv_w=dcw, conv_b=dcb, wa=dwa, ba=dba, wx=dwx, bx=dbx, lam=dlam)


def mixer_c_fwd(h, w, B, tag, rides):
    proj = matmul(h, w["w_in"], mode="nn", out_dtype=F32, name=f"{tag}_proj", out_slabs=4,
                  ride=rides.pop(f"{tag}_proj", None), **_tiles(w, "proj"))
    f_logit = matmul(h, w["w_f"], mode="nn", out_dtype=F32, name=f"{tag}_fproj")
    cum = fox_cum_fwd(_seq(f_logit, B), w["f_bias"], f"{tag}_cum")
    cum_row = cum[:, :, :HEADS].transpose(0, 2, 1)
    o, gated = fox_fwd(_seq(proj, B), cum, cum_row, f"{tag}_attn", ride=rides.pop(f"{tag}_attn", None))
    return _flat(gated), dict(proj=proj, o=o, f_logit=f_logit, cum=cum, cum_row=cum_row)


def mixer_c_bwd(dgated, w, saved, B, tag, rides):
    dproj, dck = fox_bwd(_seq(saved["proj"], B), saved["cum"], saved["cum_row"], saved["o"], _seq(dgated, B),
                         f"{tag}_attn_bwd", ride=rides.pop(f"{tag}_attn_bwd", None))
    S = dck.shape[-1]
    dcum = jnp.pad(-dck.reshape(B, HEADS, S).transpose(0, 2, 1), ((0, 0), (0, 0), (0, HEAD_DIM - HEADS)))
    df, dfb = fox_cum_bwd(_seq(saved["f_logit"], B), w["f_bias"], dcum, f"{tag}_cum_bwd")
    return _flat(dproj), dict(f_bias=dfb, df=_flat(df).astype(BF16))


MIXERS = {"a": (mixer_a_fwd, mixer_a_bwd), "b": (mixer_b_fwd, mixer_b_bwd), "c": (mixer_c_fwd, mixer_c_bwd)}
LAYER_KINDS = "abca"


def local_step(x, target, norm_pre, norm_post, get_layer, rides, on_grads):
    B, S, Dm = x.shape
    n_layers = len(LAYER_KINDS)
    xs = [x.reshape(B * S, Dm)]
    saved, layers = [], []
    for li, kind in enumerate(LAYER_KINDS):
        tag = f"l{li}{kind}"
        h = prenorm_fwd(xs[-1], norm_pre[li:li + 1], f"{tag}_prenorm")
        w = get_layer(li)
        gated, sv = MIXERS[kind][0](h, w, B, tag, rides)
        y = matmul(gated, w["w_out"], mode="nn", out_dtype=F32, name=f"{tag}_out", ride=rides.pop(f"{tag}_out", None))
        xs.append(postnorm_fwd(xs[-1], y, norm_post[li:li + 1], f"{tag}_postnorm"))
        saved.append(dict(sv, h=h, gated=gated, y=y))
        layers.append(w)
    loss, dx = loss_fwd_bwd(xs[-1], target.reshape(B * S, Dm), "loss")

    for li in reversed(range(n_layers)):
        kind, w, sv = LAYER_KINDS[li], layers[li], saved[li]
        tag = f"l{li}{kind}"
        dy, dg_post = postnorm_bwd(sv["y"], norm_post[li:li + 1], dx, f"{tag}_postnorm_bwd")
        on_grads(li, "norm_post", dg_post)
        on_grads(li, "w_out", matmul(sv["gated"], dy, mode="tn", out_dtype=BF16, name=f"{tag}_dwout",
                                     ride=rides.pop(f"{tag}_dwout", None)))
        dgated = matmul(dy, w["w_out"], mode="nt", out_dtype=F32, name=f"{tag}_dgated",
                        ride=rides.pop(f"{tag}_dgated", None))
        dproj, gw = MIXERS[kind][1](dgated, w, sv, B, tag, rides)
        df = gw.pop("df", None)
        for name, value in gw.items():
            on_grads(li, name, value)
        on_grads(li, "w_in", matmul(sv["h"], dproj, mode="tn", out_dtype=BF16, name=f"{tag}_dwin",
                                    out_slabs=w["grad_slabs"], ride=rides.pop(f"{tag}_dwin", None),
                                    **_tiles(w, "dwin")))
        dhs = [matmul(dproj, w["w_in"], mode="nt", out_dtype=F32, name=f"{tag}_dh",
                      ride=rides.pop(f"{tag}_dh", None), **_tiles(w, "dh"))]
        if df is not None:
            on_grads(li, "w_f", matmul(sv["h"], df, mode="tn", out_dtype=BF16, name=f"{tag}_dwf"))
            dhs.append(matmul(df, w["w_f"], mode="nt", out_dtype=F32, name=f"{tag}_dhf"))
        dx, dg_pre = prenorm_bwd(xs[li], norm_pre[li:li + 1], dhs, dx, f"{tag}_prenorm_bwd")
        on_grads(li, "norm_pre", dg_pre)
    assert not rides, list(rides)
    return loss, dx.reshape(B, S, Dm)


WEIGHTS = ("norm_pre", "norm_post", "a_w_in", "a_rel_bias", "a_w_out", "b_w_in", "b_conv_w", "b_conv_b",
           "b_gate_a_w", "b_gate_a_b", "b_gate_x_w", "b_gate_x_b", "b_lambda", "b_w_out", "c_w_in", "c_f_bias",
           "c_w_out")
C_SHARD = (4 * D_MODEL + HEADS) // N_DEV


def _rows(gathered):
    return gathered.reshape(gathered.shape[0] * gathered.shape[1], gathered.shape[2])


def layer_a(w_in, w_out, rel_bias):
    return dict(w_in=w_in, w_out=_rows(w_out), bias=band_bias(rel_bias), grad_slabs=N_DEV)


def layer_b(w_in, w_out, conv_w, small):
    return dict(
        w_in=w_in, w_out=_rows(w_out), grad_slabs=N_DEV,
        conv_w=conv_w.transpose(1, 0, 2).reshape(CONV_WIDTH, RG_WIDTH),
        conv_b=small["b_conv_b"], lam=small["b_lambda"],
        wa=block_diag_gates(small["b_gate_a_w"][0]).astype(BF16), ba=small["b_gate_a_b"].reshape(1, RG_WIDTH),
        wx=block_diag_gates(small["b_gate_x_w"][0]).astype(BF16), bx=small["b_gate_x_b"].reshape(1, RG_WIDTH))


def layer_c(w_in, w_out, small):
    full = w_in.transpose(1, 0, 2).reshape(D_MODEL, N_DEV * C_SHARD)
    return dict(w_in=full[:, :4 * D_MODEL], w_f=jnp.pad(full[:, 4 * D_MODEL:], ((0, 0), (0, HEAD_DIM - HEADS))),
                w_out=_rows(w_out), grad_slabs=1,
                f_bias=jnp.pad(small["c_f_bias"], ((0, 0), (0, HEAD_DIM - HEADS))))


def c_w_in_blocks(dmain, df):
    full = jnp.concatenate([dmain, df[:, :HEADS].astype(dmain.dtype)], axis=1)
    return full.reshape(D_MODEL, N_DEV, C_SHARD).transpose(1, 0, 2)


def _row_blocks(g):
    return g.reshape(N_DEV, g.shape[0] // N_DEV, g.shape[1])


PACK_LANES = 128
PACK_ALIGN = 8 * PACK_LANES


def pack(parts):
    flat = []
    for p in parts:
        n = p.size
        flat.append(jnp.pad(p.reshape(n), (0, -n % PACK_ALIGN)).reshape(-1, PACK_LANES))
    rows = sum(f.shape[0] for f in flat)
    flat.append(jnp.zeros((-rows % ROW_TILE, PACK_LANES), F32))
    return jnp.concatenate(flat, axis=0)


def unpack(packed, shapes):
    out, row = [], 0
    for shape in shapes:
        n = 1
        for s in shape:
            n *= s
        n_rows = (n + PACK_ALIGN - 1) // PACK_ALIGN * 8
        out.append(packed[row:row + n_rows].reshape(-1)[:n].reshape(shape))
        row += n_rows
    return out


LATE = (("a_rel_bias", slice(0, 1)), ("norm_pre", slice(0, 1)), ("norm_post", slice(0, 1)))
EARLY = (("a_rel_bias", slice(1, 2)), ("norm_pre", slice(1, 4)), ("norm_post", slice(1, 4)),
         ("b_conv_b", slice(None)), ("b_gate_a_w", slice(None)), ("b_gate_a_b", slice(None)),
         ("b_gate_x_w", slice(None)), ("b_gate_x_b", slice(None)), ("b_lambda", slice(None)),
         ("c_f_bias", slice(None)))


def _pieces(tree, pieces):
    return [tree[name][sl] for name, sl in pieces]


def kernel(x, norm_pre, norm_post, a_w_in, a_rel_bias, a_w_out, b_w_in, b_conv_w, b_conv_b, b_gate_a_w, b_gate_a_b, b_gate_x_w, b_gate_x_b, b_lambda, b_w_out, c_w_in, c_f_bias, c_w_out, loss_target, m_norm_pre, m_norm_post, m_a_w_in, m_a_rel_bias, m_a_w_out, m_b_w_in, m_b_conv_w, m_b_conv_b, m_b_gate_a_w, m_b_gate_a_b, m_b_gate_x_w, m_b_gate_x_b, m_b_lambda, m_b_w_out, m_c_w_in, m_c_f_bias, m_c_w_out, v_norm_pre, v_norm_post, v_a_w_in, v_a_rel_bias, v_a_w_out, v_b_w_in, v_b_conv_w, v_b_conv_b, v_b_gate_a_w, v_b_gate_a_b, v_b_gate_x_w, v_b_gate_x_b, v_b_lambda, v_b_w_out, v_c_w_in, v_c_f_bias, v_c_w_out):
    args = dict(locals())
    w = {n: args[n] for n in WEIGHTS}
    m = {n: args["m_" + n] for n in WEIGHTS}
    v = {n: args["v_" + n] for n in WEIGHTS}

    a_in, a_out = a_w_in.astype(BF16), a_w_out.astype(BF16)
    gather_a0 = Ride([a_in[0], a_out[0]], scatter=False, via_sibling=True)
    gather_b = Ride([b_w_in[0].astype(BF16), b_w_out[0].astype(BF16), b_conv_w[0]], scatter=False, via_sibling=True)
    gather_c_in = Ride([c_w_in[0].astype(BF16)], scatter=False, via_sibling=True)
    gather_c_out = Ride([c_w_out[0].astype(BF16)], scatter=False)
    gather_a1 = Ride([a_in[1], a_out[1]], scatter=False, via_sibling=True)
    exchange(gather_a0, "gather_l0")
    rides = {"l0a_proj": gather_b, "l0a_attn": gather_c_in, "l1b_rglru": gather_c_out, "l2c_attn": gather_a1}

    def get_layer(li):
        if li == 0:
            return layer_a(*gather_a0.out, a_rel_bias[0])
        if li == 1:
            return layer_b(*gather_b.out, w)
        if li == 2:
            return dict(layer_c(gather_c_in.out[0], gather_c_out.out[0], w),
                        tiles=dict(proj=dict(bn=2048)))
        return layer_a(*gather_a1.out, a_rel_bias[1])

    grads = [dict() for _ in LAYER_KINDS]
    scatters = {}

    def rel_bias_grad(j, dbias):
        return jax.vjp(band_bias, a_rel_bias[j])[1](dbias)[0][None]

    def early_partial():
        gb, gc = grads[1], grads[2]
        tree = dict(
            a_rel_bias=jnp.concatenate([jnp.zeros((1, HEADS, N_REL), F32), rel_bias_grad(1, grads[3]["bias"])]),
            norm_pre=jnp.concatenate([jnp.zeros((1, D_MODEL), F32)] + [grads[li]["norm_pre"] for li in (1, 2, 3)]),
            norm_post=jnp.concatenate([jnp.zeros((1, D_MODEL), F32)] + [grads[li]["norm_post"] for li in (1, 2, 3)]),
            b_conv_b=gb["conv_b"], b_lambda=gb["lam"],
            b_gate_a_w=block_diag_gates_t(gb["wa"])[None], b_gate_a_b=gb["ba"].reshape(1, RG_BLOCKS, RG_BLOCK),
            b_gate_x_w=block_diag_gates_t(gb["wx"])[None], b_gate_x_b=gb["bx"].reshape(1, RG_BLOCKS, RG_BLOCK),
            c_f_bias=gc["f_bias"][:, :HEADS])
        return pack(_pieces(tree, EARLY))

    def send(key, host, blocks, scatter=True, via_sibling=False):
        ride = rides.setdefault(host, Ride([], scatter, via_sibling))
        assert (ride.scatter, ride.via_sibling) == (scatter, via_sibling)
        scatters[key] = (ride, len(ride.arrs))
        ride.arrs.append(blocks)

    def on_grads(li, name, value):
        g = grads[li]
        g[name] = value
        if (li, name) == (3, "w_out"):
            send("a1_out", "l3a_attn_bwd", _row_blocks(value))
        elif (li, name) == (3, "w_in"):
            send("a1_in", "l2c_attn_bwd", value)
        elif (li, name) == (2, "w_out"):
            send("c_out", "l2c_attn_bwd", _row_blocks(value))
        elif (li, name) == (2, "w_f"):
            send("c_in", "l1b_rglru_bwd", c_w_in_blocks(g["w_in"], value))
        elif (li, name) == (1, "w_out"):
            send("b_out", "l1b_dh", _row_blocks(value))
        elif (li, name) == (1, "w_in"):
            send("b_in", "l0a_attn_bwd", value)
            send("b_conv", "l0a_attn_bwd",
                 g["conv_w"].reshape(CONV_WIDTH, N_DEV, RG_WIDTH // N_DEV).transpose(1, 0, 2))
        elif (li, name) == (1, "norm_pre"):
            send("early", "l0a_dwin", early_partial(), scatter=False, via_sibling=True)
        elif (li, name) == (0, "w_out"):
            send("a0_out", "l0a_attn_bwd", _row_blocks(value))
        elif (li, name) == (0, "w_in"):
            send("a0_in", "l0a_dh", value)

    loss, grad_x = local_step(x, loss_target, norm_pre, norm_post, get_layer, rides, on_grads)
    late_tree = dict(a_rel_bias=rel_bias_grad(0, grads[0]["bias"]), norm_pre=grads[0]["norm_pre"],
                     norm_post=grads[0]["norm_post"])
    late_parts = exchange(Ride([pack([late_tree[n] for n, _ in LATE])], scatter=False), "gather_late_grads")[0]

    def sharded(name, layer_parts):
        outs = None
        for j, parts in enumerate(layer_parts):
            outs = adamw(w[name], parts, m[name], v[name], f"adamw_{name}_{j}", layer=j, prev=outs)
        return outs

    def received(key):
        ride, position = scatters[key]
        return ride.out[position]

    res = dict(
        a_w_in=sharded("a_w_in", [received("a0_in"), received("a1_in")]),
        a_w_out=sharded("a_w_out", [received("a0_out"), received("a1_out")]),
        b_w_in=sharded("b_w_in", [received("b_in")]),
        b_w_out=sharded("b_w_out", [received("b_out")]),
        b_conv_w=sharded("b_conv_w", [received("b_conv")]),
        c_w_in=sharded("c_w_in", [received("c_in")]),
        c_w_out=sharded("c_w_out", [received("c_out")]))

    packed = {}
    for label, pieces, parts in (("early", EARLY, received("early")), ("late", LATE, late_parts)):
        outs = adamw(pack(_pieces(w, pieces))[None], parts, pack(_pieces(m, pieces))[None],
                     pack(_pieces(v, pieces))[None], f"adamw_replicated_{label}")
        shapes = [w[n][sl].shape for n, sl in pieces]
        packed[label] = [dict(zip([n for n, _ in pieces], unpack(o[0], shapes))) for o in outs]
    for n in ("b_conv_b", "b_gate_a_w", "b_gate_a_b", "b_gate_x_w", "b_gate_x_b", "b_lambda", "c_f_bias"):
        res[n] = [packed["early"][k][n] for k in range(4)]
    for n in ("a_rel_bias", "norm_pre", "norm_post"):
        res[n] = [jnp.concatenate([packed["late"][k][n], packed["early"][k][n]]) for k in range(4)]

    total = lax.psum(loss[0, 0], ("x", "y", "c"))
    return (total, grad_x, *[res[n][0] for n in WEIGHTS], *[res[n][1] for n in WEIGHTS],
            *[res[n][2] for n in WEIGHTS], *[res[n][3] for n in WEIGHTS])
```

```python
import functools

import jax
import jax.numpy as jnp
from jax import lax
from jax.experimental import pallas as pl
from jax.experimental.pallas import tpu as pltpu

F32 = jnp.float32
BF16 = jnp.bfloat16

N_DEV = 8
D_MODEL = 2048
HEADS = 16
HEAD_DIM = 128
CHUNK = 64
LEFT_CHUNKS = 8
REL_CLIP = 256
N_REL = 2 * REL_CLIP + 1
TQ = 256
A_PAD = LEFT_CHUNKS * CHUNK
A_KW = A_PAD + TQ
RG_WIDTH = 2560
RG_BLOCKS = 16
RG_BLOCK = 160
RG_COLS = 640
RG_GROUPS = RG_WIDTH // RG_COLS
RG_C = 8.0
CONV_WIDTH = 4
RMS_EPS = 1e-6
NEG_INF = -1e30
ADAM_LR = 0.001
ADAM_B1 = 0.9
ADAM_B2 = 0.999
ADAM_EPS = 1e-08
ADAM_WD = 0.01
ADAM_STEP = 10
VMEM_LIMIT = 56 * 1024 * 1024
MESH = pl.DeviceIdType.MESH


def _params(sem, vmem=VMEM_LIMIT):
    return pltpu.CompilerParams(dimension_semantics=sem, vmem_limit_bytes=vmem)


def _sigmoid(x):
    return 1.0 / (1.0 + jnp.exp(-x))


def _log1p(y):
    u = 1.0 + y
    return jnp.where(u == 1.0, y, jnp.log(u) * (y / jnp.where(u == 1.0, 1.0, u - 1.0)))


def _softplus(x):
    return jnp.maximum(x, 0.0) + _log1p(jnp.exp(-jnp.abs(x)))


def _dot(a, b, dims):
    return lax.dot_general(a, b, (dims, ((), ())), preferred_element_type=F32)


def _dot_nn(a, b):
    return _dot(a, b, ((1,), (0,)))


def _dot_nt(a, b):
    return _dot(a, b, ((1,), (1,)))


def _dot_tn(a, b):
    return _dot(a, b, ((0,), (0,)))


def _peers():
    x, y, c = lax.axis_index("x"), lax.axis_index("y"), lax.axis_index("c")
    me = 4 * x + 2 * y + c
    peers = []
    for k in range(1, N_DEV):
        px = 1 - x if k & 4 else x
        py = 1 - y if k & 2 else y
        pc = 1 - c if k & 1 else c
        peers.append(((px, py, pc), 4 * px + 2 * py + pc))
    return me, peers


class Ride:
    def __init__(self, arrs, scatter, via_sibling=False):
        assert not (scatter and via_sibling)
        self.arrs, self.scatter, self.via_sibling, self.out = list(arrs), scatter, via_sibling, None

    def out_shapes(self):
        return [jax.ShapeDtypeStruct(a.shape if self.scatter else (N_DEV,) + a.shape, a.dtype) for a in self.arrs]

    def sem_shapes(self):
        n = len(self.arrs)
        return [pltpu.SemaphoreType.DMA((n, N_DEV - 1)), pltpu.SemaphoreType.DMA((n, N_DEV - 1)),
                pltpu.SemaphoreType.DMA((n,))]

    def _copies(self, ins, outs, sems, landing):
        send_sems, recv_sems, local_sems = sems
        me, peers = _peers()
        local, remote = [], []
        for a, (src, dst) in enumerate(zip(ins, outs)):
            local.append(pltpu.make_async_copy(src.at[me] if self.scatter else src, dst.at[me], local_sems.at[a]))
            for k, (peer, peer_idx) in enumerate(peers):
                remote.append(pltpu.make_async_remote_copy(
                    src_ref=src.at[peer_idx] if self.scatter else src, dst_ref=dst.at[peer_idx if landing else me],
                    send_sem=send_sems.at[a, k], recv_sem=recv_sems.at[a, k], device_id=peer, device_id_type=MESH))
        return local, remote

    def _direct(self, k):
        return not self.via_sibling or k == 0 or (k + 1) % 2 == 0

    def start(self, ins, outs, sems):
        local, remote = self._copies(ins, outs, sems, landing=False)
        n_peers = N_DEV - 1
        for cp in local + [cp for i, cp in enumerate(remote) if self._direct(i % n_peers)]:
            cp.start()

    def wait(self, ins, outs, sems):
        local, remote = self._copies(ins, outs, sems, landing=True)
        n_peers = N_DEV - 1
        for i, cp in enumerate(remote):
            if self._direct(i % n_peers):
                cp.wait()
        if self.via_sibling:
            send_sems, recv_sems, _ = sems
            me, peers = _peers()
            sibling = peers[0][0]
            passed = []
            for a, dst in enumerate(outs):
                for j in range(1, n_peers, 2):
                    came, lands = peers[j][1], peers[j + 1][1]
                    pltpu.make_async_remote_copy(
                        src_ref=dst.at[came], dst_ref=dst.at[came], send_sem=send_sems.at[a, j + 1],
                        recv_sem=recv_sems.at[a, j + 1], device_id=sibling, device_id_type=MESH).start()
                    passed.append(pltpu.make_async_remote_copy(
                        src_ref=dst.at[came], dst_ref=dst.at[lands], send_sem=send_sems.at[a, j + 1],
                        recv_sem=recv_sems.at[a, j + 1], device_id=sibling, device_id_type=MESH))
            for cp in passed:
                cp.wait()
        for cp in local:
            cp.wait()


def _call(body, *, name, grid, in_specs, out_specs, out_shape, args, scratch_shapes=(), semantics=None, ride=None,
          aliases=None):
    scratch_shapes = list(scratch_shapes)
    if ride is None:
        return pl.pallas_call(
            body, name=name, grid=grid, in_specs=in_specs, out_specs=out_specs, out_shape=out_shape,
            scratch_shapes=scratch_shapes, input_output_aliases=aliases or {},
            compiler_params=_params(semantics if grid else None))(*args)
    assert not aliases
    n_in, n_out, n_sc, n_r = len(in_specs), len(out_specs), len(scratch_shapes), len(ride.arrs)

    def riding(*refs):
        ins, r_ins = refs[:n_in], refs[n_in:n_in + n_r]
        outs, r_outs = refs[n_in + n_r:n_in + n_r + n_out], refs[n_in + n_r + n_out:n_in + 2 * n_r + n_out]
        rest = refs[n_in + 2 * n_r + n_out:]
        scratch, sems = rest[:n_sc], rest[n_sc:]
        first = last = None
        for axis, size in enumerate(grid):
            pid = pl.program_id(axis)
            first = (pid == 0) if first is None else first & (pid == 0)
            last = (pid == size - 1) if last is None else last & (pid == size - 1)
        if grid:
            pl.when(first)(lambda: ride.start(r_ins, r_outs, sems))
        else:
            ride.start(r_ins, r_outs, sems)
        body(*ins, *outs, *scratch)
        if grid:
            pl.when(last)(lambda: ride.wait(r_ins, r_outs, sems))
        else:
            ride.wait(r_ins, r_outs, sems)

    any_spec = pl.BlockSpec(memory_space=pl.ANY)
    res = pl.pallas_call(
        riding, name=name, grid=grid, in_specs=list(in_specs) + [any_spec] * n_r,
        out_specs=list(out_specs) + [any_spec] * n_r, out_shape=list(out_shape) + ride.out_shapes(),
        scratch_shapes=scratch_shapes + ride.sem_shapes(),
        compiler_params=_params(("arbitrary",) * len(grid) if grid else None))(*args, *ride.arrs)
    ride.out = list(res[n_out:])
    return list(res[:n_out])


def exchange(ride, name):
    _call(lambda: None, name=name, grid=(), in_specs=[], out_specs=[], out_shape=[], args=[], ride=ride)
    return ride.out


LANES = 128


def _fit(dims, want):
    dims = tuple(dims)
    if len(set(dims)) == 1 and dims[0] <= want:
        return dims[0]
    return max(t for t in range(LANES, want + 1, LANES) if all(d % t == 0 for d in dims))


def _cols(arr):
    return arr.shape[-1] * (arr.shape[0] if len(arr.shape) == 3 else 1)


def _tile_spec(shape, rblk, cblk, rc):
    if len(shape) == 2:
        return pl.BlockSpec((rblk, cblk), rc)
    per = shape[2] // cblk

    def index_map(*ids):
        r, c = rc(*ids)
        return (c // per, r, c % per)

    return pl.BlockSpec((1, rblk, cblk), index_map)


def matmul(a, b, *, mode, out_dtype, name, bm=1024, bn=1024, bk=2048, out_slabs=1, m_part=(0, 1), ride=None):
    a_rows, a_cols, b_rows, b_cols = a.shape[-2], _cols(a), b.shape[-2], _cols(b)
    (K, M) = (a_rows, a_cols) if mode == "tn" else (a_cols, a_rows)
    N = b_rows if mode == "nt" else b_cols
    assert K == (b_cols if mode == "nt" else b_rows), (name, a.shape, b.shape)
    part, n_parts = m_part
    M = M // n_parts
    out_shape = (M, N) if out_slabs == 1 else (out_slabs, M, N // out_slabs)
    widths = dict(m=[M], n=[N, out_shape[-1]], k=[K])
    widths["m" if mode == "tn" else "k"].append(a.shape[-1])
    widths["k" if mode == "nt" else "n"].append(b.shape[-1])
    bm, bn, bk = _fit(widths["m"], bm), _fit(widths["n"], bn), _fit(widths["k"], bk)
    nk = K // bk
    dims = {"nn": ((1,), (0,)), "nt": ((1,), (1,)), "tn": ((0,), (0,))}[mode]

    def val(ref):
        return ref[0] if len(ref.shape) == 3 else ref[...]

    def put(ref, x):
        if len(ref.shape) == 3:
            ref[0] = x.astype(ref.dtype)
        else:
            ref[...] = x.astype(ref.dtype)

    def body(a_ref, b_ref, o_ref, *scratch):
        if nk == 1:
            put(o_ref, _dot(val(a_ref), val(b_ref), dims))
            return
        acc_ref, = scratch
        k = pl.program_id(2)

        @pl.when(k == 0)
        def _():
            acc_ref[...] = jnp.zeros_like(acc_ref)

        acc_ref[...] += _dot(val(a_ref), val(b_ref), dims)

        @pl.when(k == nk - 1)
        def _():
            put(o_ref, acc_ref[...])

    m0 = part * (M // bm)
    if mode == "tn":
        a_spec = _tile_spec(a.shape, bk, bm, lambda j, i, k: (k, m0 + i))
    else:
        a_spec = _tile_spec(a.shape, bm, bk, lambda j, i, k: (m0 + i, k))
    if mode == "nt":
        b_spec = _tile_spec(b.shape, bn, bk, lambda j, i, k: (j, k))
    else:
        b_spec = _tile_spec(b.shape, bk, bn, lambda j, i, k: (k, j))
    return _call(
        body, name=name, grid=(N // bn, M // bm, nk), in_specs=[a_spec, b_spec],
        out_specs=[_tile_spec(out_shape, bm, bn, lambda j, i, k: (i, j))],
        out_shape=[jax.ShapeDtypeStruct(out_shape, out_dtype)],
        scratch_shapes=[] if nk == 1 else [pltpu.VMEM((bm, bn), F32)],
        semantics=("parallel", "parallel", "arbitrary"), args=[a, b], ride=ride)[0]


ROW_TILE = 256


def _rms_stats(z):
    r = lax.rsqrt(jnp.mean(z * z, axis=-1, keepdims=True) + RMS_EPS)
    return r, z * r


def _rms_bwd(n, r, g, dout):
    dn = dout * g
    return r * (dn - n * jnp.mean(dn * n, axis=-1, keepdims=True))


def _row_spec(T, Dm):
    bt = min(ROW_TILE, T)
    return bt, pl.BlockSpec((bt, Dm), lambda i: (i, 0)), pl.BlockSpec((1, Dm), lambda i: (0, 0))


def prenorm_fwd(x, g, name):
    T, Dm = x.shape
    bt, row, vec = _row_spec(T, Dm)

    def body(x_ref, g_ref, h_ref):
        _, n = _rms_stats(x_ref[...])
        h_ref[...] = (n * g_ref[...]).astype(BF16)

    return pl.pallas_call(
        body, name=name, grid=(T // bt,), in_specs=[row, vec], out_specs=row,
        out_shape=jax.ShapeDtypeStruct((T, Dm), BF16), compiler_params=_params(("parallel",)),
    )(x, g)


def postnorm_fwd(x, y, g, name):
    T, Dm = x.shape
    bt, row, vec = _row_spec(T, Dm)

    def body(x_ref, y_ref, g_ref, o_ref):
        _, n = _rms_stats(y_ref[...])
        o_ref[...] = x_ref[...] + n * g_ref[...]

    return pl.pallas_call(
        body, name=name, grid=(T // bt,), in_specs=[row, row, vec], out_specs=row,
        out_shape=jax.ShapeDtypeStruct((T, Dm), F32), compiler_params=_params(("parallel",)),
    )(x, y, g)


def loss_fwd_bwd(xf, target, name):
    T, Dm = xf.shape
    bt, row, _ = _row_spec(T, Dm)

    def body(x_ref, t_ref, l_ref, d_ref):
        @pl.when(pl.program_id(0) == 0)
        def _():
            l_ref[...] = jnp.zeros_like(l_ref)

        err = x_ref[...] - t_ref[...]
        per_tok = jnp.mean(err * err, axis=-1, keepdims=True)
        l_ref[...] += 0.5 * jnp.sum(per_tok, axis=0, keepdims=True)
        d_ref[...] = err * (1.0 / Dm)

    return pl.pallas_call(
        body, name=name, grid=(T // bt,), in_specs=[row, row],
        out_specs=[pl.BlockSpec((1, 1), lambda i: (0, 0)), row],
        out_shape=[jax.ShapeDtypeStruct((1, 1), F32), jax.ShapeDtypeStruct((T, Dm), F32)],
        compiler_params=_params(("arbitrary",)),
    )(xf, target)


def postnorm_bwd(y, g, dout, name):
    T, Dm = y.shape
    bt, row, vec = _row_spec(T, Dm)

    def body(y_ref, g_ref, d_ref, dy_ref, dg_ref):
        @pl.when(pl.program_id(0) == 0)
        def _():
            dg_ref[...] = jnp.zeros_like(dg_ref)

        r, n = _rms_stats(y_ref[...])
        dout_v = d_ref[...]
        dg_ref[...] += jnp.sum(dout_v * n, axis=0, keepdims=True)
        dy_ref[...] = _rms_bwd(n, r, g_ref[...], dout_v).astype(BF16)

    return pl.pallas_call(
        body, name=name, grid=(T // bt,), in_specs=[row, vec, row], out_specs=[row, vec],
        out_shape=[jax.ShapeDtypeStruct((T, Dm), BF16), jax.ShapeDtypeStruct((1, Dm), F32)],
        compiler_params=_params(("arbitrary",)),
    )(y, g, dout)


def prenorm_bwd(x, g, dhs, dres, name):
    T, Dm = x.shape
    bt, row, vec = _row_spec(T, Dm)
    n_dh = len(dhs)

    def body(x_ref, g_ref, *refs):
        dh_refs, (dr_ref, dx_ref, dg_ref) = refs[:n_dh], refs[n_dh:]

        @pl.when(pl.program_id(0) == 0)
        def _():
            dg_ref[...] = jnp.zeros_like(dg_ref)

        r, n = _rms_stats(x_ref[...])
        dh_v = dh_refs[0][...]
        for extra in dh_refs[1:]:
            dh_v = dh_v + extra[...]
        dg_ref[...] += jnp.sum(dh_v * n, axis=0, keepdims=True)
        dx_ref[...] = dr_ref[...] + _rms_bwd(n, r, g_ref[...], dh_v)

    return pl.pallas_call(
        body, name=name, grid=(T // bt,), in_specs=[row, vec] + [row] * (n_dh + 1), out_specs=[row, vec],
        out_shape=[jax.ShapeDtypeStruct((T, Dm), F32), jax.ShapeDtypeStruct((1, Dm), F32)],
        compiler_params=_params(("arbitrary",)),
    )(x, g, *dhs, dres)


def _silu(g):
    return g * _sigmoid(g)


def _gate_bwd(dgated, core, g):
    sg = _sigmoid(g)
    return dgated * (g * sg), dgated * core * (sg * (1.0 + g * (1.0 - sg)))


def _softmax_rows(s):
    e = jnp.exp(s - jnp.max(s, axis=-1, keepdims=True))
    return e * (1.0 / jnp.sum(e, axis=-1, keepdims=True))


def _band_scores(q, kw, bias, r0):
    s = _dot_nt(q, kw) * (HEAD_DIM ** -0.5) + bias
    j = lax.broadcasted_iota(jnp.int32, s.shape, 1)
    return jnp.where(j >= A_PAD - r0, s, NEG_INF)


def _fill_padded_kv(p_ref, kp_ref, vp_ref):
    zeros = jnp.zeros((A_PAD, HEAD_DIM), BF16)
    kp_ref[0:A_PAD, :] = zeros
    vp_ref[0:A_PAD, :] = zeros
    kp_ref[A_PAD:, :] = p_ref[1, 0].astype(BF16)
    vp_ref[A_PAD:, :] = p_ref[2, 0].astype(BF16)


def _head_specs(S, order):
    def idx(fn):
        return lambda *ids: fn(**dict(zip(order, ids)))

    return (pl.BlockSpec((4, 1, S, HEAD_DIM), idx(lambda b, h, t: (0, b, 0, h))),
            pl.BlockSpec((1, TQ, HEAD_DIM), idx(lambda b, h, t: (b, t, h))))


def attn_a_fwd(proj, bias, name, ride=None):
    _, B, S, W = proj.shape
    nt = S // TQ

    def body(p_ref, b_ref, o_ref, gt_ref, kp_ref, vp_ref):
        t = pl.program_id(2)

        @pl.when(t == 0)
        def _():
            _fill_padded_kv(p_ref, kp_ref, vp_ref)

        r0 = pl.multiple_of(t * TQ, TQ)
        q = p_ref[0, 0, pl.ds(r0, TQ), :].astype(BF16)
        g = p_ref[3, 0, pl.ds(r0, TQ), :]
        p = _softmax_rows(_band_scores(q, kp_ref[pl.ds(r0, A_KW), :], b_ref[0], r0))
        o = _dot_nn(p.astype(BF16), vp_ref[pl.ds(r0, A_KW), :])
        o_ref[0] = o
        gt_ref[0] = (o * _silu(g)).astype(BF16)

    seq, tile = _head_specs(S, "bht")
    return _call(
        body, name=name, grid=(B, HEADS, nt),
        in_specs=[seq, pl.BlockSpec((1, TQ, A_KW), lambda b, h, t: (h, 0, 0))], out_specs=[tile, tile],
        out_shape=[jax.ShapeDtypeStruct((B, S, W), F32), jax.ShapeDtypeStruct((B, S, W), BF16)],
        scratch_shapes=[pltpu.VMEM((A_PAD + S, HEAD_DIM), BF16), pltpu.VMEM((A_PAD + S, HEAD_DIM), BF16)],
        semantics=("parallel", "parallel", "arbitrary"), args=[proj, bias], ride=ride)


def attn_a_bwd(proj, bias, o, dgated, name, ride=None):
    _, B, S, W = proj.shape
    nt = S // TQ

    def body(p_ref, b_ref, o_ref, dgt_ref, dp_ref, db_ref, kp_ref, vp_ref, dk_ref, dv_ref):
        b_, t = pl.program_id(1), pl.program_id(2)

        @pl.when(t == 0)
        def _():
            _fill_padded_kv(p_ref, kp_ref, vp_ref)
            dk_ref[...] = jnp.zeros_like(dk_ref)
            dv_ref[...] = jnp.zeros_like(dv_ref)

        @pl.when((t == 0) & (b_ == 0))
        def _():
            db_ref[...] = jnp.zeros_like(db_ref)

        r0 = pl.multiple_of(t * TQ, TQ)
        rows, win = pl.ds(r0, TQ), pl.ds(r0, A_KW)
        q = p_ref[0, 0, rows, :].astype(BF16)
        g = p_ref[3, 0, rows, :]
        kw, vw = kp_ref[win, :], vp_ref[win, :]
        p = _softmax_rows(_band_scores(q, kw, b_ref[0], r0))
        do, dg = _gate_bwd(dgt_ref[0], o_ref[0], g)
        do = do.astype(BF16)
        dv_ref[win, :] += _dot_tn(p.astype(BF16), do)
        dpr = _dot_nt(do, vw)
        ds = p * (dpr - jnp.sum(p * dpr, axis=-1, keepdims=True))
        db_ref[0] += ds
        ds = (ds * (HEAD_DIM ** -0.5)).astype(BF16)
        dk_ref[win, :] += _dot_tn(ds, q)
        dp_ref[0, 0, rows, :] = _dot_nn(ds, kw).astype(BF16)
        dp_ref[3, 0, rows, :] = dg.astype(BF16)

        @pl.when(t == nt - 1)
        def _():
            dp_ref[1, 0] = dk_ref[A_PAD:, :].astype(BF16)
            dp_ref[2, 0] = dv_ref[A_PAD:, :].astype(BF16)

    seq, tile = _head_specs(S, "hbt")
    bias_spec = pl.BlockSpec((1, TQ, A_KW), lambda h, b, t: (h, 0, 0))
    return _call(
        body, name=name, grid=(HEADS, B, nt), in_specs=[seq, bias_spec, tile, tile], out_specs=[seq, bias_spec],
        out_shape=[jax.ShapeDtypeStruct(proj.shape, BF16), jax.ShapeDtypeStruct(bias.shape, F32)],
        scratch_shapes=[pltpu.VMEM((A_PAD + S, HEAD_DIM), BF16), pltpu.VMEM((A_PAD + S, HEAD_DIM), BF16),
                        pltpu.VMEM((A_PAD + S, HEAD_DIM), F32), pltpu.VMEM((A_PAD + S, HEAD_DIM), F32)],
        semantics=("arbitrary", "arbitrary", "arbitrary"), args=[proj, bias, o, dgated], ride=ride)


def band_bias(rel_bias):
    length = TQ + A_KW - 1
    first = REL_CLIP + 1 - TQ
    gen = jnp.concatenate([rel_bias[:, first:],
                           jnp.broadcast_to(rel_bias[:, 2 * REL_CLIP:], (HEADS, length - (N_REL - first)))], axis=1)
    rev = jnp.concatenate([gen[:, ::-1], jnp.zeros((HEADS, 1), rel_bias.dtype)], axis=1)
    sheared = jnp.tile(rev, (1, TQ))[:, :TQ * length].reshape(HEADS, TQ, length)
    i = lax.broadcasted_iota(jnp.int32, (TQ, A_KW), 0)
    j = lax.broadcasted_iota(jnp.int32, (TQ, A_KW), 1)
    first_key = (i // CHUNK) * CHUNK
    in_band = (j >= first_key) & (j < first_key + (LEFT_CHUNKS + 1) * CHUNK)
    return jnp.where(in_band, sheared[:, :, TQ - 1:], NEG_INF)


def _group_scan(a, u, carry, reverse=False):
    row = lax.broadcasted_iota(jnp.int32, u.shape, 0)
    for k in (1, 2, 4):
        shift = 8 - k if reverse else k
        valid = (row < 8 - k) if reverse else (row >= k)
        u_sh = pltpu.roll(u, shift, 0)
        if a is None:
            u = jnp.where(valid, u + u_sh, u)
        else:
            a_sh = pltpu.roll(a, shift, 0)
            u = jnp.where(valid, a * u_sh + u, u)
            a = jnp.where(valid, a * a_sh, a)
    return (u + carry) if a is None else (a * carry + u)


def _scan_rows(n_rows, step, carry0, reverse=False):
    groups = n_rows // 8

    def loop(i, carry):
        gi = (groups - 1 - i) if reverse else i
        return step(pl.multiple_of(gi * 8, 8), carry)

    return lax.fori_loop(0, groups, loop, carry0)


def fox_cum_fwd(f_logit, f_bias, name):
    B, S, L = f_logit.shape

    def body(f_ref, b_ref, c_ref):
        z = f_ref[0] + b_ref[...]
        c_ref[0] = jnp.minimum(z, 0.0) - _log1p(jnp.exp(-jnp.abs(z)))

        def step(r0, carry):
            h = _group_scan(None, c_ref[0, pl.ds(r0, 8), :], carry)
            c_ref[0, pl.ds(r0, 8), :] = h
            return h[7:8, :]

        _scan_rows(S, step, jnp.zeros((1, L), F32))

    return pl.pallas_call(
        body, name=name, grid=(B,),
        in_specs=[pl.BlockSpec((1, S, L), lambda b: (b, 0, 0)), pl.BlockSpec((1, L), lambda b: (0, 0))],
        out_specs=pl.BlockSpec((1, S, L), lambda b: (b, 0, 0)),
        out_shape=jax.ShapeDtypeStruct((B, S, L), F32), compiler_params=_params(("parallel",)),
    )(f_logit, f_bias)


def fox_cum_bwd(f_logit, f_bias, dcum, name):
    B, S, L = f_logit.shape

    def body(f_ref, b_ref, d_ref, df_ref, db_ref):
        @pl.when(pl.program_id(0) == 0)
        def _():
            db_ref[...] = jnp.zeros_like(db_ref)

        def step(r0, carry):
            h = _group_scan(None, d_ref[0, pl.ds(r0, 8), :], carry, reverse=True)
            df_ref[0, pl.ds(r0, 8), :] = h
            return h[0:1, :]

        _scan_rows(S, step, jnp.zeros((1, L), F32), reverse=True)
        df = df_ref[0] * _sigmoid(-(f_ref[0] + b_ref[...]))
        df_ref[0] = df
        db_ref[...] += jnp.sum(df, axis=0, keepdims=True)

    seq = pl.BlockSpec((1, S, L), lambda b: (b, 0, 0))
    vec = pl.BlockSpec((1, L), lambda b: (0, 0))
    return pl.pallas_call(
        body, name=name, grid=(B,), in_specs=[seq, vec, seq], out_specs=[seq, vec],
        out_shape=[jax.ShapeDtypeStruct((B, S, L), F32), jax.ShapeDtypeStruct((1, L), F32)],
        compiler_params=_params(("arbitrary",)),
    )(f_logit, f_bias, dcum)


def _fox_scores(q, k, cc, cr, h, r0):
    lane = lax.broadcasted_iota(jnp.int32, cc.shape, 1)
    cq = jnp.sum(jnp.where(lane == h, cc, 0.0), axis=1, keepdims=True)
    sub = lax.broadcasted_iota(jnp.int32, cr.shape, 0)
    ck = jnp.sum(jnp.where(sub == h, cr, 0.0), axis=0, keepdims=True)
    s = _dot_nt(q, k) * (HEAD_DIM ** -0.5) + (cq - ck)
    qpos = r0 + lax.broadcasted_iota(jnp.int32, s.shape, 0)
    kpos = lax.broadcasted_iota(jnp.int32, s.shape, 1)
    return jnp.where(kpos <= qpos, s, NEG_INF)


KEY_STEP = 512


def _by_causal_width(t, S, fn):
    per = KEY_STEP // TQ
    for c in range(S // KEY_STEP):
        pl.when(t // per == c)(functools.partial(fn, (c + 1) * KEY_STEP))


def fox_fwd(proj, cum_col, cum_row, name, ride=None):
    _, B, S, W = proj.shape
    nt = S // TQ

    def body(p_ref, cc_ref, cr_ref, o_ref, gt_ref, k_ref, v_ref):
        h, t = pl.program_id(1), pl.program_id(2)

        @pl.when(t == 0)
        def _():
            k_ref[...] = p_ref[1, 0].astype(BF16)
            v_ref[...] = p_ref[2, 0].astype(BF16)

        r0 = pl.multiple_of(t * TQ, TQ)
        q = p_ref[0, 0, pl.ds(r0, TQ), :].astype(BF16)
        g = p_ref[3, 0, pl.ds(r0, TQ), :]

        def tile_out(width):
            p = _softmax_rows(_fox_scores(q, k_ref[0:width, :], cc_ref[0], cr_ref[0, :, 0:width], h, r0))
            o = _dot_nn(p.astype(BF16), v_ref[0:width, :])
            o_ref[0] = o
            gt_ref[0] = (o * _silu(g)).astype(BF16)

        _by_causal_width(t, S, tile_out)

    seq, tile = _head_specs(S, "bht")
    return _call(
        body, name=name, grid=(B, HEADS, nt),
        in_specs=[seq, pl.BlockSpec((1, TQ, cum_col.shape[2]), lambda b, h, t: (b, t, 0)),
                  pl.BlockSpec((1, HEADS, S), lambda b, h, t: (b, 0, 0))],
        out_specs=[tile, tile],
        out_shape=[jax.ShapeDtypeStruct((B, S, W), F32), jax.ShapeDtypeStruct((B, S, W), BF16)],
        scratch_shapes=[pltpu.VMEM((S, HEAD_DIM), BF16), pltpu.VMEM((S, HEAD_DIM), BF16)],
        semantics=("parallel", "parallel", "arbitrary"), args=[proj, cum_col, cum_row], ride=ride)


def fox_bwd(proj, cum_col, cum_row, o, dgated, name, ride=None):
    _, B, S, W = proj.shape
    nt = S // TQ

    def body(p_ref, cc_ref, cr_ref, o_ref, dgt_ref, dp_ref, dc_ref, k_ref, v_ref, dk_ref, dv_ref):
        h, t = pl.program_id(1), pl.program_id(2)

        @pl.when(t == 0)
        def _():
            k_ref[...] = p_ref[1, 0].astype(BF16)
            v_ref[...] = p_ref[2, 0].astype(BF16)
            dk_ref[...] = jnp.zeros_like(dk_ref)
            dv_ref[...] = jnp.zeros_like(dv_ref)
            dc_ref[...] = jnp.zeros_like(dc_ref)

        r0 = pl.multiple_of(t * TQ, TQ)
        rows = pl.ds(r0, TQ)
        q = p_ref[0, 0, rows, :].astype(BF16)
        g = p_ref[3, 0, rows, :]
        do, dg = _gate_bwd(dgt_ref[0], o_ref[0], g)
        do = do.astype(BF16)
        dp_ref[3, 0, rows, :] = dg.astype(BF16)

        def tile_grads(width):
            k, v = k_ref[0:width, :], v_ref[0:width, :]
            p = _softmax_rows(_fox_scores(q, k, cc_ref[0], cr_ref[0, :, 0:width], h, r0))
            dv_ref[0:width, :] += _dot_tn(p.astype(BF16), do)
            dpr = _dot_nt(do, v)
            ds = p * (dpr - jnp.sum(p * dpr, axis=-1, keepdims=True))
            dc_ref[0, 0, :, 0:width] += jnp.sum(ds, axis=0, keepdims=True)
            ds = (ds * (HEAD_DIM ** -0.5)).astype(BF16)
            dk_ref[0:width, :] += _dot_tn(ds, q)
            dp_ref[0, 0, rows, :] = _dot_nn(ds, k).astype(BF16)

        _by_causal_width(t, S, tile_grads)

        @pl.when(t == nt - 1)
        def _():
            dp_ref[1, 0] = dk_ref[...].astype(BF16)
            dp_ref[2, 0] = dv_ref[...].astype(BF16)

    seq, tile = _head_specs(S, "bht")
    return _call(
        body, name=name, grid=(B, HEADS, nt),
        in_specs=[seq, pl.BlockSpec((1, TQ, cum_col.shape[2]), lambda b, h, t: (b, t, 0)),
                  pl.BlockSpec((1, HEADS, S), lambda b, h, t: (b, 0, 0)), tile, tile],
        out_specs=[seq, pl.BlockSpec((1, 1, 1, S), lambda b, h, t: (b, h, 0, 0))],
        out_shape=[jax.ShapeDtypeStruct(proj.shape, BF16), jax.ShapeDtypeStruct((B, HEADS, 1, S), F32)],
        scratch_shapes=[pltpu.VMEM((S, HEAD_DIM), BF16), pltpu.VMEM((S, HEAD_DIM), BF16),
                        pltpu.VMEM((S, HEAD_DIM), F32), pltpu.VMEM((S, HEAD_DIM), F32)],
        semantics=("parallel", "parallel", "arbitrary"), args=[proj, cum_col, cum_row, o, dgated], ride=ride)


RG_ROWS = 512


def _rg_gates(xc, wa_ref, ba_ref, wx_ref, bx_ref, lam_ref):
    xcb = xc.astype(BF16)
    r = _sigmoid(_dot_nn(xcb, wa_ref[0]) + ba_ref[...])
    i = _sigmoid(_dot_nn(xcb, wx_ref[0]) + bx_ref[...])
    sp = _softplus(-lam_ref[...])
    log_a = (-RG_C * sp) * r
    a = jnp.exp(log_a)
    m = jnp.sqrt(-jnp.tanh(log_a) * (a * a + 1.0))
    return xcb, r, i, sp, a, m


def _rg_specs(B, S, rows, order):
    nc = S // rows

    def idx(fn):
        def index_map(*ids):
            v = dict(zip(order.lower(), ids))
            c = (nc - 1 - v["c"]) if "C" in order else v["c"]
            return fn(v["b"], v["d"], c)
        return index_map

    return dict(
        proj=pl.BlockSpec((2, 1, rows, RG_COLS), idx(lambda b, d, c: (0, b, c, d))),
        act=pl.BlockSpec((1, rows, RG_COLS), idx(lambda b, d, c: (b, c, d))),
        taps=pl.BlockSpec((CONV_WIDTH, RG_COLS), idx(lambda b, d, c: (0, d))),
        vec=pl.BlockSpec((1, RG_COLS), idx(lambda b, d, c: (0, d))),
        gate=pl.BlockSpec((1, RG_COLS, RG_COLS), idx(lambda b, d, c: (d, 0, 0))),
    )


def rglru_fwd(proj, conv_w, conv_b, wa, ba, wx, bx, lam, name, rows=RG_ROWS, ride=None):
    _, B, S, _ = proj.shape
    rows = min(rows, S)
    sp_ = _rg_specs(B, S, rows, "bdc")

    def body(p_ref, cw_ref, cb_ref, wa_ref, ba_ref, wx_ref, bx_ref, lam_ref,
             xc_ref, hs_ref, hp_ref, gt_ref, ext_ref, a_ref, u_ref, xcar_ref, hcar_ref):
        @pl.when(pl.program_id(2) == 0)
        def _():
            xcar_ref[...] = jnp.zeros_like(xcar_ref)
            hcar_ref[...] = jnp.zeros_like(hcar_ref)

        xr = p_ref[0, 0]
        ext_ref[0:8, :] = xcar_ref[...]
        ext_ref[8:, :] = xr
        xcar_ref[...] = xr[rows - 8:, :]
        xc = ext_ref[pl.ds(5, rows), :] * cw_ref[0:1, :]
        xc = xc + ext_ref[pl.ds(6, rows), :] * cw_ref[1:2, :]
        xc = xc + ext_ref[pl.ds(7, rows), :] * cw_ref[2:3, :]
        xc = xc + xr * cw_ref[3:4, :] + cb_ref[...]
        xc_ref[0] = xc
        _, _, i, _, a, m = _rg_gates(xc, wa_ref, ba_ref, wx_ref, bx_ref, lam_ref)
        a_ref[...] = a
        u_ref[...] = m * (i * xc)

        def step(r0, carry):
            h = _group_scan(a_ref[pl.ds(r0, 8), :], u_ref[pl.ds(r0, 8), :], carry)
            row = lax.broadcasted_iota(jnp.int32, h.shape, 0)
            hs_ref[0, pl.ds(r0, 8), :] = h
            hp_ref[0, pl.ds(r0, 8), :] = jnp.where(row == 0, carry, pltpu.roll(h, 1, 0))
            return h[7:8, :]

        hcar_ref[0:1, :] = _scan_rows(rows, step, hcar_ref[0:1, :])
        gt_ref[0] = (hs_ref[0] * _silu(p_ref[1, 0])).astype(BF16)

    act = jax.ShapeDtypeStruct((B, S, RG_WIDTH), F32)
    return _call(
        body, name=name, grid=(B, RG_GROUPS, S // rows),
        in_specs=[sp_["proj"], sp_["taps"], sp_["vec"], sp_["gate"], sp_["vec"], sp_["gate"], sp_["vec"], sp_["vec"]],
        out_specs=[sp_["act"]] * 4,
        out_shape=[act, act, act, jax.ShapeDtypeStruct((B, S, RG_WIDTH), BF16)],
        scratch_shapes=[pltpu.VMEM((rows + 8, RG_COLS), F32), pltpu.VMEM((rows, RG_COLS), F32),
                        pltpu.VMEM((rows, RG_COLS), F32), pltpu.VMEM((8, RG_COLS), F32), pltpu.VMEM((8, RG_COLS), F32)],
        semantics=("parallel", "parallel", "arbitrary"), args=[proj, conv_w, conv_b, wa, ba, wx, bx, lam], ride=ride)


def rglru_bwd(proj, xc, hs, hprev, dgated, conv_w, wa, ba, wx, bx, lam, name, rows=RG_ROWS, ride=None):
    _, B, S, _ = proj.shape
    rows = min(rows, S)
    sp_ = _rg_specs(B, S, rows, "dbC")

    def body(p_ref, xc_ref, hs_ref, hp_ref, dgt_ref, cw_ref, wa_ref, ba_ref, wx_ref, bx_ref, lam_ref,
             dp_ref, dcw_ref, dcb_ref, dwa_ref, dba_ref, dwx_ref, dbx_ref, dlam_ref,
             ext_ref, c_ref, l_ref, acar_ref, lcar_ref, dcar_ref):
        b_, c_ = pl.program_id(1), pl.program_id(2)

        @pl.when(c_ == 0)
        def _():
            acar_ref[...] = jnp.zeros_like(acar_ref)
            lcar_ref[...] = jnp.zeros_like(lcar_ref)
            dcar_ref[...] = jnp.zeros_like(dcar_ref)

        @pl.when((c_ == 0) & (b_ == 0))
        def _():
            for ref in (dcw_ref, dcb_ref, dwa_ref, dba_ref, dwx_ref, dbx_ref, dlam_ref):
                ref[...] = jnp.zeros_like(ref)

        xr, g = p_ref[0, 0], p_ref[1, 0]
        xc_v = xc_ref[0]
        xcb, r, i, sp, a, m = _rg_gates(xc_v, wa_ref, ba_ref, wx_ref, bx_ref, lam_ref)
        dhs, dg = _gate_bwd(dgt_ref[0], hs_ref[0], g)
        dp_ref[1, 0] = dg.astype(BF16)

        ext_ref[0:rows, :] = a
        ext_ref[rows:, :] = acar_ref[...]
        acar_ref[...] = a[0:8, :]
        c_ref[...] = ext_ref[pl.ds(1, rows), :]
        l_ref[...] = dhs

        def step(r0, carry):
            lam_g = _group_scan(c_ref[pl.ds(r0, 8), :], l_ref[pl.ds(r0, 8), :], carry, reverse=True)
            l_ref[pl.ds(r0, 8), :] = lam_g
            return lam_g[0:1, :]

        lcar_ref[0:1, :] = _scan_rows(rows, step, lcar_ref[0:1, :], reverse=True)
        du = l_ref[...]
        da = du * hp_ref[0]
        dlog_a = da * a - (du * (i * xc_v)) * (a * a / m)
        dr = dlog_a * (-RG_C * sp)
        dsp = jnp.sum(dlog_a * (-RG_C * r), axis=0, keepdims=True)
        dlam_ref[...] += dsp * (-_sigmoid(-lam_ref[...]))
        dpa = dr * (r * (1.0 - r))
        dpx = (du * (m * xc_v)) * (i * (1.0 - i))
        dba_ref[...] += jnp.sum(dpa, axis=0, keepdims=True)
        dbx_ref[...] += jnp.sum(dpx, axis=0, keepdims=True)
        dpa, dpx = dpa.astype(BF16), dpx.astype(BF16)
        dwa_ref[0] += _dot_tn(xcb, dpa)
        dwx_ref[0] += _dot_tn(xcb, dpx)
        dxc = du * (m * i) + _dot_nt(dpa, wa_ref[0]) + _dot_nt(dpx, wx_ref[0])

        dcb_ref[...] += jnp.sum(dxc, axis=0, keepdims=True)
        ext_ref[0:rows, :] = dxc
        ext_ref[rows:, :] = dcar_ref[...]
        dcar_ref[...] = dxc[0:8, :]
        dxr = jnp.zeros_like(dxc)
        for k in range(CONV_WIDTH):
            tap = CONV_WIDTH - 1 - k
            ahead = dxc if k == 0 else ext_ref[pl.ds(k, rows), :]
            dxr = dxr + ahead * cw_ref[tap:tap + 1, :]
            dcw_ref[tap:tap + 1, :] += jnp.sum(xr * ahead, axis=0, keepdims=True)
        dp_ref[0, 0] = dxr.astype(BF16)

    vec = jax.ShapeDtypeStruct((1, RG_WIDTH), F32)
    gate = jax.ShapeDtypeStruct((RG_GROUPS, RG_COLS, RG_COLS), F32)
    return _call(
        body, name=name, grid=(RG_GROUPS, B, S // rows),
        in_specs=[sp_["proj"], sp_["act"], sp_["act"], sp_["act"], sp_["act"], sp_["taps"],
                  sp_["gate"], sp_["vec"], sp_["gate"], sp_["vec"], sp_["vec"]],
        out_specs=[sp_["proj"], sp_["taps"], sp_["vec"], sp_["gate"], sp_["vec"], sp_["gate"], sp_["vec"], sp_["vec"]],
        out_shape=[jax.ShapeDtypeStruct(proj.shape, BF16), jax.ShapeDtypeStruct((CONV_WIDTH, RG_WIDTH), F32), vec,
                   gate, vec, gate, vec, vec],
        scratch_shapes=[pltpu.VMEM((rows + 8, RG_COLS), F32), pltpu.VMEM((rows, RG_COLS), F32),
                        pltpu.VMEM((rows, RG_COLS), F32), pltpu.VMEM((8, RG_COLS), F32),
                        pltpu.VMEM((8, RG_COLS), F32), pltpu.VMEM((8, RG_COLS), F32)],
        semantics=("arbitrary", "arbitrary", "arbitrary"),
        args=[proj, xc, hs, hprev, dgated, conv_w, wa, ba, wx, bx, lam], ride=ride)


def block_diag_gates(w):
    per = RG_COLS // RG_BLOCK
    w4 = w.reshape(RG_GROUPS, per, RG_BLOCK, RG_BLOCK)
    return jnp.einsum("dipq,ij->dipjq", w4, jnp.eye(per, dtype=w.dtype)).reshape(RG_GROUPS, RG_COLS, RG_COLS)


def block_diag_gates_t(dw):
    per = RG_COLS // RG_BLOCK
    dw6 = dw.reshape(RG_GROUPS, per, RG_BLOCK, per, RG_BLOCK)
    return jnp.stack([dw6[:, i, :, i, :] for i in range(per)], axis=1).reshape(RG_BLOCKS, RG_BLOCK, RG_BLOCK)


def adamw(w, parts, m, v, name, layer=0, prev=None, part_row0=0, row_tile=ROW_TILE):
    L, R, C = w.shape
    n_parts = parts.shape[0]
    br = row_tile if R % row_tile == 0 else R

    def body(w_ref, p_ref, m_ref, v_ref, *refs):
        g_ref, d_ref, nm_ref, nv_ref = refs[-4:]
        g = p_ref[0].astype(F32)
        for k in range(1, n_parts):
            g = g + p_ref[k].astype(F32)
        nm = ADAM_B1 * m_ref[0] + (1.0 - ADAM_B1) * g
        nv = ADAM_B2 * v_ref[0] + (1.0 - ADAM_B2) * (g * g)
        m_hat = nm / (1.0 - ADAM_B1 ** ADAM_STEP)
        v_hat = nv / (1.0 - ADAM_B2 ** ADAM_STEP)
        g_ref[0] = g
        d_ref[0] = -ADAM_LR * (m_hat / (jnp.sqrt(v_hat) + ADAM_EPS) + ADAM_WD * w_ref[0])
        nm_ref[0] = nm
        nv_ref[0] = nv

    slab = pl.BlockSpec((1, br, C), lambda i: (layer, i, 0))
    out = jax.ShapeDtypeStruct((L, R, C), F32)
    carried = [] if prev is None else list(prev)
    return _call(
        body, name=name, grid=(R // br,),
        in_specs=[slab, pl.BlockSpec((n_parts, br, C), lambda i: (0, part_row0 // br + i, 0)), slab, slab]
        + [pl.BlockSpec(memory_space=pl.ANY)] * len(carried),
        out_specs=[slab] * 4, out_shape=[out] * 4, semantics=("parallel",), args=[w, parts, m, v] + carried,
        aliases={4 + k: k for k in range(len(carried))})


def _seq(a, B):
    return a.reshape(a.shape[:-2] + (B, a.shape[-2] // B, a.shape[-1]))


def _flat(a):
    return a.reshape(a.shape[:-3] + (a.shape[-3] * a.shape[-2], a.shape[-1]))


def _tiles(w, which, **default):
    return dict(default, **w.get("tiles", {}).get(which, {}))


def mixer_a_fwd(h, w, B, tag, rides):
    proj = matmul(h, w["w_in"], mode="nn", out_dtype=F32, name=f"{tag}_proj", out_slabs=4,
                  ride=rides.pop(f"{tag}_proj", None), **_tiles(w, "proj"))
    o, gated = attn_a_fwd(_seq(proj, B), w["bias"], f"{tag}_attn", ride=rides.pop(f"{tag}_attn", None))
    return _flat(gated), dict(proj=proj, o=o)


def mixer_a_bwd(dgated, w, saved, B, tag, rides):
    dproj, dbias = attn_a_bwd(_seq(saved["proj"], B), w["bias"], saved["o"], _seq(dgated, B), f"{tag}_attn_bwd",
                              ride=rides.pop(f"{tag}_attn_bwd", None))
    return _flat(dproj), dict(bias=dbias)


def mixer_b_fwd(h, w, B, tag, rides):
    proj = matmul(h, w["w_in"], mode="nn", out_dtype=F32, name=f"{tag}_proj", out_slabs=2, bn=RG_COLS,
                  ride=rides.pop(f"{tag}_proj", None))
    xc, hs, hprev, gated = rglru_fwd(_seq(proj, B), w["conv_w"], w["conv_b"], w["wa"], w["ba"], w["wx"], w["bx"],
                                     w["lam"], f"{tag}_rglru", ride=rides.pop(f"{tag}_rglru", None))
    return _flat(gated), dict(proj=proj, xc=xc, hs=hs, hprev=hprev)


def mixer_b_bwd(dgated, w, saved, B, tag, rides):
    dproj, dcw, dcb, dwa, dba, dwx, dbx, dlam = rglru_bwd(
        _seq(saved["proj"], B), saved["xc"], saved["hs"], saved["hprev"], _seq(dgated, B),
        w["conv_w"], w["wa"], w["ba"], w["wx"], w["bx"], w["lam"], f"{tag}_rglru_bwd",
        ride=rides.pop(f"{tag}_rglru_bwd", None))
    return _flat(dproj), dict(conv_w=dcw, conv_b=dcb, wa=dwa, ba=dba, wx=dwx, bx=dbx, lam=dlam)


def mixer_c_fwd(h, w, B, tag, rides):
    proj = matmul(h, w["w_in"], mode="nn", out_dtype=F32, name=f"{tag}_proj", out_slabs=4,
                  ride=rides.pop(f"{tag}_proj", None), **_tiles(w, "proj"))
    f_logit = matmul(h, w["w_f"], mode="nn", out_dtype=F32, name=f"{tag}_fproj")
    cum = fox_cum_fwd(_seq(f_logit, B), w["f_bias"], f"{tag}_cum")
    cum_row = cum[:, :, :HEADS].transpose(0, 2, 1)
    o, gated = fox_fwd(_seq(proj, B), cum, cum_row, f"{tag}_attn", ride=rides.pop(f"{tag}_attn", None))
    return _flat(gated), dict(proj=proj, o=o, f_logit=f_logit, cum=cum, cum_row=cum_row)


def mixer_c_bwd(dgated, w, saved, B, tag, rides):
    dproj, dck = fox_bwd(_seq(saved["proj"], B), saved["cum"], saved["cum_row"], saved["o"], _seq(dgated, B),
                         f"{tag}_attn_bwd", ride=rides.pop(f"{tag}_attn_bwd", None))
    S = dck.shape[-1]
    dcum = jnp.pad(-dck.reshape(B, HEADS, S).transpose(0, 2, 1), ((0, 0), (0, 0), (0, HEAD_DIM - HEADS)))
    df, dfb = fox_cum_bwd(_seq(saved["f_logit"], B), w["f_bias"], dcum, f"{tag}_cum_bwd")
    return _flat(dproj), dict(f_bias=dfb, df=_flat(df).astype(BF16))


MIXERS = {"a": (mixer_a_fwd, mixer_a_bwd), "b": (mixer_b_fwd, mixer_b_bwd), "c": (mixer_c_fwd, mixer_c_bwd)}
LAYER_KINDS = "abca"


def local_step(x, target, norm_pre, norm_post, get_layer, rides, on_grads):
    B, S, Dm = x.shape
    n_layers = len(LAYER_KINDS)
    xs = [x.reshape(B * S, Dm)]
    saved, layers = [], []
    for li, kind in enumerate(LAYER_KINDS):
        tag = f"l{li}{kind}"
        h = prenorm_fwd(xs[-1], norm_pre[li:li + 1], f"{tag}_prenorm")
        w = get_layer(li)
        gated, sv = MIXERS[kind][0](h, w, B, tag, rides)
        if callable(w["w_out"]):
            w["w_out"] = w["w_out"]()
        y = matmul(gated, w["w_out"], mode="nn", out_dtype=F32, name=f"{tag}_out", ride=rides.pop(f"{tag}_out", None))
        xs.append(postnorm_fwd(xs[-1], y, norm_post[li:li + 1], f"{tag}_postnorm"))
        saved.append(dict(sv, h=h, gated=gated, y=y))
        layers.append(w)
    loss, dx = loss_fwd_bwd(xs[-1], target.reshape(B * S, Dm), "loss")

    for li in reversed(range(n_layers)):
        kind, w, sv = LAYER_KINDS[li], layers[li], saved[li]
        tag = f"l{li}{kind}"
        dy, dg_post = postnorm_bwd(sv["y"], norm_post[li:li + 1], dx, f"{tag}_postnorm_bwd")
        on_grads(li, "norm_post", dg_post)
        on_grads(li, "w_out", matmul(sv["gated"], dy, mode="tn", out_dtype=BF16, name=f"{tag}_dwout",
                                     ride=rides.pop(f"{tag}_dwout", None)))
        dgated = matmul(dy, w["w_out"], mode="nt", out_dtype=F32, name=f"{tag}_dgated",
                        ride=rides.pop(f"{tag}_dgated", None))
        dproj, gw = MIXERS[kind][1](dgated, w, sv, B, tag, rides)
        df = gw.pop("df", None)
        for name, value in gw.items():
            on_grads(li, name, value)
        n_parts = w.get("dwin_parts", 1)
        for part in range(n_parts):
            name = f"{tag}_dwin" + (f"_{part}" if n_parts > 1 else "")
            on_grads(li, "w_in" + (f"_{part}" if n_parts > 1 else ""),
                     matmul(sv["h"], dproj, mode="tn", out_dtype=BF16, name=name, out_slabs=w["grad_slabs"],
                            m_part=(part, n_parts), ride=rides.pop(name, None), **_tiles(w, "dwin")))
        if df is not None:
            on_grads(li, "w_f", matmul(sv["h"], df, mode="tn", out_dtype=BF16, name=f"{tag}_dwf"))
        dhs = [matmul(dproj, w["w_in"], mode="nt", out_dtype=F32, name=f"{tag}_dh",
                      ride=rides.pop(f"{tag}_dh", None), **_tiles(w, "dh"))]
        if df is not None:
            dhs.append(matmul(df, w["w_f"], mode="nt", out_dtype=F32, name=f"{tag}_dhf"))
        dx, dg_pre = prenorm_bwd(xs[li], norm_pre[li:li + 1], dhs, dx, f"{tag}_prenorm_bwd")
        on_grads(li, "norm_pre", dg_pre)
    assert not rides, list(rides)
    return loss, dx.reshape(B, S, Dm)


WEIGHTS = ("norm_pre", "norm_post", "a_w_in", "a_rel_bias", "a_w_out", "b_w_in", "b_conv_w", "b_conv_b",
           "b_gate_a_w", "b_gate_a_b", "b_gate_x_w", "b_gate_x_b", "b_lambda", "b_w_out", "c_w_in", "c_f_bias",
           "c_w_out")
C_SHARD = (4 * D_MODEL + HEADS) // N_DEV


def _rows(gathered):
    return gathered.reshape(gathered.shape[0] * gathered.shape[1], gathered.shape[2])


def layer_a(w_in, w_out, rel_bias):
    return dict(w_in=w_in, w_out=w_out if callable(w_out) else _rows(w_out), bias=band_bias(rel_bias),
                grad_slabs=N_DEV)


def layer_b(w_in, w_out, conv_w, small):
    return dict(
        w_in=w_in, w_out=_rows(w_out), grad_slabs=N_DEV,
        conv_w=conv_w.transpose(1, 0, 2).reshape(CONV_WIDTH, RG_WIDTH),
        conv_b=small["b_conv_b"], lam=small["b_lambda"],
        wa=block_diag_gates(small["b_gate_a_w"][0]).astype(BF16), ba=small["b_gate_a_b"].reshape(1, RG_WIDTH),
        wx=block_diag_gates(small["b_gate_x_w"][0]).astype(BF16), bx=small["b_gate_x_b"].reshape(1, RG_WIDTH))


def layer_c(w_in, w_out, small):
    full = w_in.transpose(1, 0, 2).reshape(D_MODEL, N_DEV * C_SHARD)
    return dict(w_in=full[:, :4 * D_MODEL], w_f=jnp.pad(full[:, 4 * D_MODEL:], ((0, 0), (0, HEAD_DIM - HEADS))),
                w_out=_rows(w_out), grad_slabs=1,
                f_bias=jnp.pad(small["c_f_bias"], ((0, 0), (0, HEAD_DIM - HEADS))))


def c_w_in_blocks(dmain, df):
    full = jnp.concatenate([dmain, df[:, :HEADS].astype(dmain.dtype)], axis=1)
    return full.reshape(D_MODEL, N_DEV, C_SHARD).transpose(1, 0, 2)


def _row_blocks(g):
    return g.reshape(N_DEV, g.shape[0] // N_DEV, g.shape[1])


PACK_LANES = 128
PACK_ALIGN = 8 * PACK_LANES


def pack(parts):
    flat = []
    for p in parts:
        n = p.size
        flat.append(jnp.pad(p.reshape(n), (0, -n % PACK_ALIGN)).reshape(-1, PACK_LANES))
    rows = sum(f.shape[0] for f in flat)
    flat.append(jnp.zeros((-rows % ROW_TILE, PACK_LANES), F32))
    return jnp.concatenate(flat, axis=0)


def unpack(packed, shapes):
    out, row = [], 0
    for shape in shapes:
        n = 1
        for s in shape:
            n *= s
        n_rows = (n + PACK_ALIGN - 1) // PACK_ALIGN * 8
        out.append(packed[row:row + n_rows].reshape(-1)[:n].reshape(shape))
        row += n_rows
    return out


LATE = (("a_rel_bias", slice(0, 1)), ("norm_pre", slice(0, 2)), ("norm_post", slice(0, 1)))
EARLY = (("a_rel_bias", slice(1, 2)), ("norm_pre", slice(2, 4)), ("norm_post", slice(1, 4)),
         ("b_conv_b", slice(None)), ("b_gate_a_w", slice(None)), ("b_gate_a_b", slice(None)),
         ("b_gate_x_w", slice(None)), ("b_gate_x_b", slice(None)), ("b_lambda", slice(None)),
         ("c_f_bias", slice(None)))


def _pieces(tree, pieces):
    return [tree[name][sl] for name, sl in pieces]


def kernel(x, norm_pre, norm_post, a_w_in, a_rel_bias, a_w_out, b_w_in, b_conv_w, b_conv_b, b_gate_a_w, b_gate_a_b, b_gate_x_w, b_gate_x_b, b_lambda, b_w_out, c_w_in, c_f_bias, c_w_out, loss_target, m_norm_pre, m_norm_post, m_a_w_in, m_a_rel_bias, m_a_w_out, m_b_w_in, m_b_conv_w, m_b_conv_b, m_b_gate_a_w, m_b_gate_a_b, m_b_gate_x_w, m_b_gate_x_b, m_b_lambda, m_b_w_out, m_c_w_in, m_c_f_bias, m_c_w_out, v_norm_pre, v_norm_post, v_a_w_in, v_a_rel_bias, v_a_w_out, v_b_w_in, v_b_conv_w, v_b_conv_b, v_b_gate_a_w, v_b_gate_a_b, v_b_gate_x_w, v_b_gate_x_b, v_b_lambda, v_b_w_out, v_c_w_in, v_c_f_bias, v_c_w_out):
    args = dict(locals())
    w = {n: args[n] for n in WEIGHTS}
    m = {n: args["m_" + n] for n in WEIGHTS}
    v = {n: args["v_" + n] for n in WEIGHTS}

    a_in, a_out = a_w_in.astype(BF16), a_w_out.astype(BF16)
    gather_a0 = Ride([a_in[0]], scatter=False, via_sibling=True)
    gather_b = Ride([b_w_in[0].astype(BF16), b_w_out[0].astype(BF16), b_conv_w[0], a_out[0]], scatter=False,
                    via_sibling=True)
    gather_c_in = Ride([c_w_in[0].astype(BF16)], scatter=False, via_sibling=True)
    gather_c_out = Ride([c_w_out[0].astype(BF16)], scatter=False)
    gather_a1 = Ride([a_in[1], a_out[1]], scatter=False, via_sibling=True)
    exchange(gather_a0, "gather_l0")
    rides = {"l0a_proj": gather_b, "l0a_attn": gather_c_in, "l1b_rglru": gather_c_out, "l2c_attn": gather_a1}

    def get_layer(li):
        if li == 0:
            return dict(layer_a(gather_a0.out[0], lambda: _rows(gather_b.out[3]), a_rel_bias[0]), dwin_parts=2)
        if li == 1:
            return layer_b(*gather_b.out[:3], w)
        if li == 2:
            return dict(layer_c(gather_c_in.out[0], gather_c_out.out[0], w),
                        tiles=dict(proj=dict(bn=2048)))
        return layer_a(*gather_a1.out, a_rel_bias[1])

    grads = [dict() for _ in LAYER_KINDS]
    scatters = {}

    def rel_bias_grad(j, dbias):
        return jax.vjp(band_bias, a_rel_bias[j])[1](dbias)[0][None]

    def early_partial():
        gb, gc = grads[1], grads[2]
        tree = dict(
            a_rel_bias=jnp.concatenate([jnp.zeros((1, HEADS, N_REL), F32), rel_bias_grad(1, grads[3]["bias"])]),
            norm_pre=jnp.concatenate([jnp.zeros((2, D_MODEL), F32)] + [grads[li]["norm_pre"] for li in (2, 3)]),
            norm_post=jnp.concatenate([jnp.zeros((1, D_MODEL), F32)] + [grads[li]["norm_post"] for li in (1, 2, 3)]),
            b_conv_b=gb["conv_b"], b_lambda=gb["lam"],
            b_gate_a_w=block_diag_gates_t(gb["wa"])[None], b_gate_a_b=gb["ba"].reshape(1, RG_BLOCKS, RG_BLOCK),
            b_gate_x_w=block_diag_gates_t(gb["wx"])[None], b_gate_x_b=gb["bx"].reshape(1, RG_BLOCKS, RG_BLOCK),
            c_f_bias=gc["f_bias"][:, :HEADS])
        return pack(_pieces(tree, EARLY))

    def send(key, host, blocks, scatter=True, via_sibling=False):
        ride = rides.setdefault(host, Ride([], scatter, via_sibling))
        assert (ride.scatter, ride.via_sibling) == (scatter, via_sibling)
        scatters[key] = (ride, len(ride.arrs))
        ride.arrs.append(blocks)

    def on_grads(li, name, value):
        g = grads[li]
        g[name] = value
        if (li, name) == (3, "w_out"):
            send("a1_out", "l3a_attn_bwd", _row_blocks(value))
        elif (li, name) == (3, "w_in"):
            send("a1_in", "l2c_attn_bwd", value)
        elif (li, name) == (2, "w_out"):
            send("c_out", "l2c_attn_bwd", _row_blocks(value))
        elif (li, name) == (2, "w_f"):
            blocks = c_w_in_blocks(g["w_in"], value)
            send("c_in_0", "l2c_dh", blocks[:, :D_MODEL // 2])
            send("c_in_1", "l1b_rglru_bwd", blocks[:, D_MODEL // 2:])
        elif (li, name) == (1, "w_out"):
            send("b_out", "l1b_dh", _row_blocks(value))
        elif (li, name) == (1, "w_in"):
            send("b_in", "l0a_attn_bwd", value)
            send("b_conv", "l0a_attn_bwd",
                 g["conv_w"].reshape(CONV_WIDTH, N_DEV, RG_WIDTH // N_DEV).transpose(1, 0, 2))
        elif (li, name) == (1, "lam"):
            send("early", "l1b_dwin", early_partial(), scatter=False, via_sibling=True)
        elif (li, name) == (0, "w_out"):
            send("a0_out", "l0a_attn_bwd", _row_blocks(value))
        elif (li, name) == (0, "w_in_0"):
            send("a0_in_0", "l0a_dwin_1", value)
        elif (li, name) == (0, "w_in_1"):
            send("a0_in_1", "l0a_dh", value)

    loss, grad_x = local_step(x, loss_target, norm_pre, norm_post, get_layer, rides, on_grads)
    late_tree = dict(a_rel_bias=rel_bias_grad(0, grads[0]["bias"]), norm_post=grads[0]["norm_post"],
                     norm_pre=jnp.concatenate([grads[0]["norm_pre"], grads[1]["norm_pre"]]))
    late_parts = exchange(Ride([pack([late_tree[n] for n, _ in LATE])], scatter=False), "gather_late_grads")[0]

    def sharded(name, slab_parts):
        shape = w[name].shape
        slabs = (len(slab_parts), shape[0] * shape[1] // len(slab_parts), shape[2])
        outs = None
        for j, (parts, row0) in enumerate(slab_parts):
            outs = adamw(w[name].reshape(slabs), parts, m[name].reshape(slabs), v[name].reshape(slabs),
                         f"adamw_{name}_{j}", layer=j, prev=outs, part_row0=row0)
        return [o.reshape(shape) for o in outs]

    def received(key):
        ride, position = scatters[key]
        return ride.out[position]

    res = dict(
        a_w_in=sharded("a_w_in", [(received("a0_in_0"), 0), (received("a0_in_1"), 0),
                                  (received("a1_in"), 0), (received("a1_in"), D_MODEL // 2)]),
        a_w_out=sharded("a_w_out", [(received("a0_out"), 0), (received("a1_out"), 0)]),
        b_w_in=sharded("b_w_in", [(received("b_in"), 0)]),
        b_w_out=sharded("b_w_out", [(received("b_out"), 0)]),
        b_conv_w=sharded("b_conv_w", [(received("b_conv"), 0)]),
        c_w_in=sharded("c_w_in", [(received("c_in_0"), 0), (received("c_in_1"), 0)]),
        c_w_out=sharded("c_w_out", [(received("c_out"), 0)]))

    packed = {}
    for label, pieces, parts in (("early", EARLY, received("early")), ("late", LATE, late_parts)):
        outs = adamw(pack(_pieces(w, pieces))[None], parts, pack(_pieces(m, pieces))[None],
                     pack(_pieces(v, pieces))[None], f"adamw_replicated_{label}")
        shapes = [w[n][sl].shape for n, sl in pieces]
        packed[label] = [dict(zip([n for n, _ in pieces], unpack(o[0], shapes))) for o in outs]
    for n in ("b_conv_b", "b_gate_a_w", "b_gate_a_b", "b_gate_x_w", "b_gate_x_b", "b_lambda", "c_f_bias"):
        res[n] = [packed["early"][k][n] for k in range(4)]
    for n in ("a_rel_bias", "norm_pre", "norm_post"):
        res[n] = [jnp.concatenate([packed["late"][k][n], packed["early"][k][n]]) for k in range(4)]

    total = lax.psum(loss[0, 0], ("x", "y", "c"))
    return (total, grad_x, *[res[n][0] for n in WEIGHTS], *[res[n][1] for n in WEIGHTS],
            *[res[n][2] for n in WEIGHTS], *[res[n][3] for n in WEIGHTS])
```

```python
import functools

import jax
import jax.numpy as jnp
from jax import lax
from jax.experimental import pallas as pl
from jax.experimental.pallas import tpu as pltpu

F32 = jnp.float32
BF16 = jnp.bfloat16

N_DEV = 8
D_MODEL = 2048
HEADS = 16
HEAD_DIM = 128
CHUNK = 64
LEFT_CHUNKS = 8
REL_CLIP = 256
N_REL = 2 * REL_CLIP + 1
TQ = 256
A_PAD = LEFT_CHUNKS * CHUNK
A_KW = A_PAD + TQ
RG_WIDTH = 2560
RG_BLOCKS = 16
RG_BLOCK = 160
RG_COLS = 640
RG_GROUPS = RG_WIDTH // RG_COLS
RG_C = 8.0
CONV_WIDTH = 4
RMS_EPS = 1e-6
NEG_INF = -1e30
ADAM_LR = 0.001
ADAM_B1 = 0.9
ADAM_B2 = 0.999
ADAM_EPS = 1e-08
ADAM_WD = 0.01
ADAM_STEP = 10
VMEM_LIMIT = 56 * 1024 * 1024
MESH = pl.DeviceIdType.MESH


def _params(sem, vmem=VMEM_LIMIT):
    return pltpu.CompilerParams(dimension_semantics=sem, vmem_limit_bytes=vmem)


def _sigmoid(x):
    return 1.0 / (1.0 + jnp.exp(-x))


def _log1p(y):
    u = 1.0 + y
    return jnp.where(u == 1.0, y, jnp.log(u) * (y / jnp.where(u == 1.0, 1.0, u - 1.0)))


def _softplus(x):
    return jnp.maximum(x, 0.0) + _log1p(jnp.exp(-jnp.abs(x)))


def _dot(a, b, dims):
    return lax.dot_general(a, b, (dims, ((), ())), preferred_element_type=F32)


def _dot_nn(a, b):
    return _dot(a, b, ((1,), (0,)))


def _dot_nt(a, b):
    return _dot(a, b, ((1,), (1,)))


def _dot_tn(a, b):
    return _dot(a, b, ((0,), (0,)))


def _peers():
    x, y, c = lax.axis_index("x"), lax.axis_index("y"), lax.axis_index("c")
    me = 4 * x + 2 * y + c
    peers = []
    for k in range(1, N_DEV):
        px = 1 - x if k & 4 else x
        py = 1 - y if k & 2 else y
        pc = 1 - c if k & 1 else c
        peers.append(((px, py, pc), 4 * px + 2 * py + pc))
    return me, peers


class Ride:
    def __init__(self, arrs, scatter, via_sibling=False):
        assert not (scatter and via_sibling)
        self.arrs, self.scatter, self.via_sibling, self.out = list(arrs), scatter, via_sibling, None

    def out_shapes(self):
        return [jax.ShapeDtypeStruct(a.shape if self.scatter else (N_DEV,) + a.shape, a.dtype) for a in self.arrs]

    def sem_shapes(self):
        n = len(self.arrs)
        return [pltpu.SemaphoreType.DMA((n, N_DEV - 1)), pltpu.SemaphoreType.DMA((n, N_DEV - 1)),
                pltpu.SemaphoreType.DMA((n,))]

    def _copies(self, ins, outs, sems, landing):
        send_sems, recv_sems, local_sems = sems
        me, peers = _peers()
        local, remote = [], []
        for a, (src, dst) in enumerate(zip(ins, outs)):
            local.append(pltpu.make_async_copy(src.at[me] if self.scatter else src, dst.at[me], local_sems.at[a]))
            for k, (peer, peer_idx) in enumerate(peers):
                remote.append(pltpu.make_async_remote_copy(
                    src_ref=src.at[peer_idx] if self.scatter else src, dst_ref=dst.at[peer_idx if landing else me],
                    send_sem=send_sems.at[a, k], recv_sem=recv_sems.at[a, k], device_id=peer, device_id_type=MESH))
        return local, remote

    def _direct(self, k):
        return not self.via_sibling or k == 0 or (k + 1) % 2 == 0

    def start(self, ins, outs, sems):
        local, remote = self._copies(ins, outs, sems, landing=False)
        n_peers = N_DEV - 1
        for cp in local + [cp for i, cp in enumerate(remote) if self._direct(i % n_peers)]:
            cp.start()

    def wait(self, ins, outs, sems):
        local, remote = self._copies(ins, outs, sems, landing=True)
        n_peers = N_DEV - 1
        for i, cp in enumerate(remote):
            if self._direct(i % n_peers):
                cp.wait()
        if self.via_sibling:
            send_sems, recv_sems, _ = sems
            me, peers = _peers()
            sibling = peers[0][0]
            passed = []
            for a, dst in enumerate(outs):
                for j in range(1, n_peers, 2):
                    came, lands = peers[j][1], peers[j + 1][1]
                    pltpu.make_async_remote_copy(
                        src_ref=dst.at[came], dst_ref=dst.at[came], send_sem=send_sems.at[a, j + 1],
                        recv_sem=recv_sems.at[a, j + 1], device_id=sibling, device_id_type=MESH).start()
                    passed.append(pltpu.make_async_remote_copy(
                        src_ref=dst.at[came], dst_ref=dst.at[lands], send_sem=send_sems.at[a, j + 1],
                        recv_sem=recv_sems.at[a, j + 1], device_id=sibling, device_id_type=MESH))
            for cp in passed:
                cp.wait()
        for cp in local:
            cp.wait()


def _call(body, *, name, grid, in_specs, out_specs, out_shape, args, scratch_shapes=(), semantics=None, ride=None,
          aliases=None):
    scratch_shapes = list(scratch_shapes)
    if ride is None:
        return pl.pallas_call(
            body, name=name, grid=grid, in_specs=in_specs, out_specs=out_specs, out_shape=out_shape,
            scratch_shapes=scratch_shapes, input_output_aliases=aliases or {},
            compiler_params=_params(semantics if grid else None))(*args)
    assert not aliases
    n_in, n_out, n_sc, n_r = len(in_specs), len(out_specs), len(scratch_shapes), len(ride.arrs)

    def riding(*refs):
        ins, r_ins = refs[:n_in], refs[n_in:n_in + n_r]
        outs, r_outs = refs[n_in + n_r:n_in + n_r + n_out], refs[n_in + n_r + n_out:n_in + 2 * n_r + n_out]
        rest = refs[n_in + 2 * n_r + n_out:]
        scratch, sems = rest[:n_sc], rest[n_sc:]
        first = last = None
        for axis, size in enumerate(grid):
            pid = pl.program_id(axis)
            first = (pid == 0) if first is None else first & (pid == 0)
            last = (pid == size - 1) if last is None else last & (pid == size - 1)
        if grid:
            pl.when(first)(lambda: ride.start(r_ins, r_outs, sems))
        else:
            ride.start(r_ins, r_outs, sems)
        body(*ins, *outs, *scratch)
        if grid:
            pl.when(last)(lambda: ride.wait(r_ins, r_outs, sems))
        else:
            ride.wait(r_ins, r_outs, sems)

    any_spec = pl.BlockSpec(memory_space=pl.ANY)
    res = pl.pallas_call(
        riding, name=name, grid=grid, in_specs=list(in_specs) + [any_spec] * n_r,
        out_specs=list(out_specs) + [any_spec] * n_r, out_shape=list(out_shape) + ride.out_shapes(),
        scratch_shapes=scratch_shapes + ride.sem_shapes(),
        compiler_params=_params(("arbitrary",) * len(grid) if grid else None))(*args, *ride.arrs)
    ride.out = list(res[n_out:])
    return list(res[:n_out])


def exchange(ride, name):
    _call(lambda: None, name=name, grid=(), in_specs=[], out_specs=[], out_shape=[], args=[], ride=ride)
    return ride.out


LANES = 128


def _fit(dims, want):
    dims = tuple(dims)
    if len(set(dims)) == 1 and dims[0] <= want:
        return dims[0]
    return max(t for t in range(LANES, want + 1, LANES) if all(d % t == 0 for d in dims))


def _cols(arr):
    return arr.shape[-1] * (arr.shape[0] if len(arr.shape) == 3 else 1)


def _tile_spec(shape, rblk, cblk, rc):
    if len(shape) == 2:
        return pl.BlockSpec((rblk, cblk), rc)
    per = shape[2] // cblk

    def index_map(*ids):
        r, c = rc(*ids)
        return (c // per, r, c % per)

    return pl.BlockSpec((1, rblk, cblk), index_map)


def matmul(a, b, *, mode, out_dtype, name, bm=1024, bn=1024, bk=2048, out_slabs=1, m_part=(0, 1), ride=None):
    a_rows, a_cols, b_rows, b_cols = a.shape[-2], _cols(a), b.shape[-2], _cols(b)
    (K, M) = (a_rows, a_cols) if mode == "tn" else (a_cols, a_rows)
    N = b_rows if mode == "nt" else b_cols
    assert K == (b_cols if mode == "nt" else b_rows), (name, a.shape, b.shape)
    part, n_parts = m_part
    M = M // n_parts
    out_shape = (M, N) if out_slabs == 1 else (out_slabs, M, N // out_slabs)
    widths = dict(m=[M], n=[N, out_shape[-1]], k=[K])
    widths["m" if mode == "tn" else "k"].append(a.shape[-1])
    widths["k" if mode == "nt" else "n"].append(b.shape[-1])
    bm, bn, bk = _fit(widths["m"], bm), _fit(widths["n"], bn), _fit(widths["k"], bk)
    nk = K // bk
    dims = {"nn": ((1,), (0,)), "nt": ((1,), (1,)), "tn": ((0,), (0,))}[mode]

    def val(ref):
        return ref[0] if len(ref.shape) == 3 else ref[...]

    def put(ref, x):
        if len(ref.shape) == 3:
            ref[0] = x.astype(ref.dtype)
        else:
            ref[...] = x.astype(ref.dtype)

    def body(a_ref, b_ref, o_ref, *scratch):
        if nk == 1:
            put(o_ref, _dot(val(a_ref), val(b_ref), dims))
            return
        acc_ref, = scratch
        k = pl.program_id(2)

        @pl.when(k == 0)
        def _():
            acc_ref[...] = jnp.zeros_like(acc_ref)

        acc_ref[...] += _dot(val(a_ref), val(b_ref), dims)

        @pl.when(k == nk - 1)
        def _():
            put(o_ref, acc_ref[...])

    m0 = part * (M // bm)
    if mode == "tn":
        a_spec = _tile_spec(a.shape, bk, bm, lambda j, i, k: (k, m0 + i))
    else:
        a_spec = _tile_spec(a.shape, bm, bk, lambda j, i, k: (m0 + i, k))
    if mode == "nt":
        b_spec = _tile_spec(b.shape, bn, bk, lambda j, i, k: (j, k))
    else:
        b_spec = _tile_spec(b.shape, bk, bn, lambda j, i, k: (k, j))
    return _call(
        body, name=name, grid=(N // bn, M // bm, nk), in_specs=[a_spec, b_spec],
        out_specs=[_tile_spec(out_shape, bm, bn, lambda j, i, k: (i, j))],
        out_shape=[jax.ShapeDtypeStruct(out_shape, out_dtype)],
        scratch_shapes=[] if nk == 1 else [pltpu.VMEM((bm, bn), F32)],
        semantics=("parallel", "parallel", "arbitrary"), args=[a, b], ride=ride)[0]


ROW_TILE = 256


def _rms_stats(z):
    r = lax.rsqrt(jnp.mean(z * z, axis=-1, keepdims=True) + RMS_EPS)
    return r, z * r


def _rms_bwd(n, r, g, dout):
    dn = dout * g
    return r * (dn - n * jnp.mean(dn * n, axis=-1, keepdims=True))


def _row_spec(T, Dm):
    bt = min(ROW_TILE, T)
    return bt, pl.BlockSpec((bt, Dm), lambda i: (i, 0)), pl.BlockSpec((1, Dm), lambda i: (0, 0))


def prenorm_fwd(x, g, name):
    T, Dm = x.shape
    bt, row, vec = _row_spec(T, Dm)

    def body(x_ref, g_ref, h_ref):
        _, n = _rms_stats(x_ref[...])
        h_ref[...] = (n * g_ref[...]).astype(BF16)

    return pl.pallas_call(
        body, name=name, grid=(T // bt,), in_specs=[row, vec], out_specs=row,
        out_shape=jax.ShapeDtypeStruct((T, Dm), BF16), compiler_params=_params(("parallel",)),
    )(x, g)


def postnorm_fwd(x, y, g, name):
    T, Dm = x.shape
    bt, row, vec = _row_spec(T, Dm)

    def body(x_ref, y_ref, g_ref, o_ref):
        _, n = _rms_stats(y_ref[...])
        o_ref[...] = x_ref[...] + n * g_ref[...]

    return pl.pallas_call(
        body, name=name, grid=(T // bt,), in_specs=[row, row, vec], out_specs=row,
        out_shape=jax.ShapeDtypeStruct((T, Dm), F32), compiler_params=_params(("parallel",)),
    )(x, y, g)


def loss_fwd_bwd(xf, target, name):
    T, Dm = xf.shape
    bt, row, _ = _row_spec(T, Dm)

    def body(x_ref, t_ref, l_ref, d_ref):
        @pl.when(pl.program_id(0) == 0)
        def _():
            l_ref[...] = jnp.zeros_like(l_ref)

        err = x_ref[...] - t_ref[...]
        per_tok = jnp.mean(err * err, axis=-1, keepdims=True)
        l_ref[...] += 0.5 * jnp.sum(per_tok, axis=0, keepdims=True)
        d_ref[...] = err * (1.0 / Dm)

    return pl.pallas_call(
        body, name=name, grid=(T // bt,), in_specs=[row, row],
        out_specs=[pl.BlockSpec((1, 1), lambda i: (0, 0)), row],
        out_shape=[jax.ShapeDtypeStruct((1, 1), F32), jax.ShapeDtypeStruct((T, Dm), F32)],
        compiler_params=_params(("arbitrary",)),
    )(xf, target)


def postnorm_bwd(y, g, dout, name):
    T, Dm = y.shape
    bt, row, vec = _row_spec(T, Dm)

    def body(y_ref, g_ref, d_ref, dy_ref, dg_ref):
        @pl.when(pl.program_id(0) == 0)
        def _():
            dg_ref[...] = jnp.zeros_like(dg_ref)

        r, n = _rms_stats(y_ref[...])
        dout_v = d_ref[...]
        dg_ref[...] += jnp.sum(dout_v * n, axis=0, keepdims=True)
        dy_ref[...] = _rms_bwd(n, r, g_ref[...], dout_v).astype(BF16)

    return pl.pallas_call(
        body, name=name, grid=(T // bt,), in_specs=[row, vec, row], out_specs=[row, vec],
        out_shape=[jax.ShapeDtypeStruct((T, Dm), BF16), jax.ShapeDtypeStruct((1, Dm), F32)],
        compiler_params=_params(("arbitrary",)),
    )(y, g, dout)


def prenorm_bwd(x, g, dhs, dres, name):
    T, Dm = x.shape
    bt, row, vec = _row_spec(T, Dm)
    n_dh = len(dhs)

    def body(x_ref, g_ref, *refs):
        dh_refs, (dr_ref, dx_ref, dg_ref) = refs[:n_dh], refs[n_dh:]

        @pl.when(pl.program_id(0) == 0)
        def _():
            dg_ref[...] = jnp.zeros_like(dg_ref)

        r, n = _rms_stats(x_ref[...])
        dh_v = dh_refs[0][...]
        for extra in dh_refs[1:]:
            dh_v = dh_v + extra[...]
        dg_ref[...] += jnp.sum(dh_v * n, axis=0, keepdims=True)
        dx_ref[...] = dr_ref[...] + _rms_bwd(n, r, g_ref[...], dh_v)

    return pl.pallas_call(
        body, name=name, grid=(T // bt,), in_specs=[row, vec] + [row] * (n_dh + 1), out_specs=[row, vec],
        out_shape=[jax.ShapeDtypeStruct((T, Dm), F32), jax.ShapeDtypeStruct((1, Dm), F32)],
        compiler_params=_params(("arbitrary",)),
    )(x, g, *dhs, dres)


def _silu(g):
    return g * _sigmoid(g)


def _gate_bwd(dgated, core, g):
    sg = _sigmoid(g)
    return dgated * (g * sg), dgated * core * (sg * (1.0 + g * (1.0 - sg)))


def _softmax_rows(s):
    e = jnp.exp(s - jnp.max(s, axis=-1, keepdims=True))
    return e * (1.0 / jnp.sum(e, axis=-1, keepdims=True))


STRIP = 16


def _band_scores(qk, bias, r0):
    s = qk * (HEAD_DIM ** -0.5) + bias
    j = lax.broadcasted_iota(jnp.int32, s.shape, 1)
    return jnp.where(j >= A_PAD - r0, s, NEG_INF)


def _fill_padded_kv(p_ref, kp_ref, vp_ref):
    zeros = jnp.zeros((A_PAD, HEAD_DIM), BF16)
    kp_ref[0:A_PAD, :] = zeros
    vp_ref[0:A_PAD, :] = zeros
    kp_ref[A_PAD:, :] = p_ref[1, 0].astype(BF16)
    vp_ref[A_PAD:, :] = p_ref[2, 0].astype(BF16)


def _head_specs(S, order):
    def idx(fn):
        return lambda *ids: fn(**dict(zip(order, ids)))

    return (pl.BlockSpec((4, 1, S, HEAD_DIM), idx(lambda b, h, t: (0, b, 0, h))),
            pl.BlockSpec((1, TQ, HEAD_DIM), idx(lambda b, h, t: (b, t, h))))


def attn_a_fwd(proj, bias, name, ride=None):
    _, B, S, W = proj.shape
    nt = S // TQ

    def body(p_ref, b_ref, o_ref, gt_ref, kp_ref, vp_ref, s_ref, pb_ref):
        t = pl.program_id(2)

        @pl.when(t == 0)
        def _():
            _fill_padded_kv(p_ref, kp_ref, vp_ref)

        r0 = pl.multiple_of(t * TQ, TQ)
        q = p_ref[0, 0, pl.ds(r0, TQ), :].astype(BF16)
        g = p_ref[3, 0, pl.ds(r0, TQ), :]
        s_ref[...] = _dot_nt(q, kp_ref[pl.ds(r0, A_KW), :])

        def strip(i, carry):
            rows = pl.ds(pl.multiple_of(i * STRIP, STRIP), STRIP)
            pb_ref[rows, :] = _softmax_rows(_band_scores(s_ref[rows, :], b_ref[0, rows, :], r0)).astype(BF16)
            return carry

        lax.fori_loop(0, TQ // STRIP, strip, 0)
        o = _dot_nn(pb_ref[...], vp_ref[pl.ds(r0, A_KW), :])
        o_ref[0] = o
        gt_ref[0] = (o * _silu(g)).astype(BF16)

    seq, tile = _head_specs(S, "bht")
    return _call(
        body, name=name, grid=(B, HEADS, nt),
        in_specs=[seq, pl.BlockSpec((1, TQ, A_KW), lambda b, h, t: (h, 0, 0))], out_specs=[tile, tile],
        out_shape=[jax.ShapeDtypeStruct((B, S, W), F32), jax.ShapeDtypeStruct((B, S, W), BF16)],
        scratch_shapes=[pltpu.VMEM((A_PAD + S, HEAD_DIM), BF16), pltpu.VMEM((A_PAD + S, HEAD_DIM), BF16),
                        pltpu.VMEM((TQ, A_KW), F32), pltpu.VMEM((TQ, A_KW), BF16)],
        semantics=("parallel", "parallel", "arbitrary"), args=[proj, bias], ride=ride)


def attn_a_bwd(proj, bias, o, dgated, name, ride=None):
    _, B, S, W = proj.shape
    nt = S // TQ

    def body(p_ref, b_ref, o_ref, dgt_ref, dp_ref, db_ref, kp_ref, vp_ref, dk_ref, dv_ref,
             s_ref, d_ref, pb_ref, dsb_ref):
        b_, t = pl.program_id(1), pl.program_id(2)

        @pl.when(t == 0)
        def _():
            _fill_padded_kv(p_ref, kp_ref, vp_ref)
            dk_ref[...] = jnp.zeros_like(dk_ref)
            dv_ref[...] = jnp.zeros_like(dv_ref)

        @pl.when((t == 0) & (b_ == 0))
        def _():
            db_ref[...] = jnp.zeros_like(db_ref)

        r0 = pl.multiple_of(t * TQ, TQ)
        rows, win = pl.ds(r0, TQ), pl.ds(r0, A_KW)
        q = p_ref[0, 0, rows, :].astype(BF16)
        g = p_ref[3, 0, rows, :]
        kw, vw = kp_ref[win, :], vp_ref[win, :]
        do, dg = _gate_bwd(dgt_ref[0], o_ref[0], g)
        do = do.astype(BF16)
        s_ref[...] = _dot_nt(q, kw)
        d_ref[...] = _dot_nt(do, vw)

        def strip(i, carry):
            part = pl.ds(pl.multiple_of(i * STRIP, STRIP), STRIP)
            p = _softmax_rows(_band_scores(s_ref[part, :], b_ref[0, part, :], r0))
            dpr = d_ref[part, :]
            ds = p * (dpr - jnp.sum(p * dpr, axis=-1, keepdims=True))
            db_ref[0, part, :] += ds
            pb_ref[part, :] = p.astype(BF16)
            dsb_ref[part, :] = (ds * (HEAD_DIM ** -0.5)).astype(BF16)
            return carry

        lax.fori_loop(0, TQ // STRIP, strip, 0)
        ds = dsb_ref[...]
        dv_ref[win, :] += _dot_tn(pb_ref[...], do)
        dk_ref[win, :] += _dot_tn(ds, q)
        dp_ref[0, 0, rows, :] = _dot_nn(ds, kw).astype(BF16)
        dp_ref[3, 0, rows, :] = dg.astype(BF16)

        @pl.when(t == nt - 1)
        def _():
            dp_ref[1, 0] = dk_ref[A_PAD:, :].astype(BF16)
            dp_ref[2, 0] = dv_ref[A_PAD:, :].astype(BF16)

    seq, tile = _head_specs(S, "hbt")
    bias_spec = pl.BlockSpec((1, TQ, A_KW), lambda h, b, t: (h, 0, 0))
    return _call(
        body, name=name, grid=(HEADS, B, nt), in_specs=[seq, bias_spec, tile, tile], out_specs=[seq, bias_spec],
        out_shape=[jax.ShapeDtypeStruct(proj.shape, BF16), jax.ShapeDtypeStruct(bias.shape, F32)],
        scratch_shapes=[pltpu.VMEM((A_PAD + S, HEAD_DIM), BF16), pltpu.VMEM((A_PAD + S, HEAD_DIM), BF16),
                        pltpu.VMEM((A_PAD + S, HEAD_DIM), F32), pltpu.VMEM((A_PAD + S, HEAD_DIM), F32),
                        pltpu.VMEM((TQ, A_KW), F32), pltpu.VMEM((TQ, A_KW), F32),
                        pltpu.VMEM((TQ, A_KW), BF16), pltpu.VMEM((TQ, A_KW), BF16)],
        semantics=("arbitrary", "arbitrary", "arbitrary"), args=[proj, bias, o, dgated], ride=ride)


def band_bias(rel_bias):
    length = TQ + A_KW - 1
    first = REL_CLIP + 1 - TQ
    gen = jnp.concatenate([rel_bias[:, first:],
                           jnp.broadcast_to(rel_bias[:, 2 * REL_CLIP:], (HEADS, length - (N_REL - first)))], axis=1)
    rev = jnp.concatenate([gen[:, ::-1], jnp.zeros((HEADS, 1), rel_bias.dtype)], axis=1)
    sheared = jnp.tile(rev, (1, TQ))[:, :TQ * length].reshape(HEADS, TQ, length)
    i = lax.broadcasted_iota(jnp.int32, (TQ, A_KW), 0)
    j = lax.broadcasted_iota(jnp.int32, (TQ, A_KW), 1)
    first_key = (i // CHUNK) * CHUNK
    in_band = (j >= first_key) & (j < first_key + (LEFT_CHUNKS + 1) * CHUNK)
    return jnp.where(in_band, sheared[:, :, TQ - 1:], NEG_INF)


def _group_scan(a, u, carry, reverse=False):
    row = lax.broadcasted_iota(jnp.int32, u.shape, 0)
    for k in (1, 2, 4):
        shift = 8 - k if reverse else k
        valid = (row < 8 - k) if reverse else (row >= k)
        u_sh = pltpu.roll(u, shift, 0)
        if a is None:
            u = jnp.where(valid, u + u_sh, u)
        else:
            a_sh = pltpu.roll(a, shift, 0)
            u = jnp.where(valid, a * u_sh + u, u)
            a = jnp.where(valid, a * a_sh, a)
    return (u + carry) if a is None else (a * carry + u)


def _scan_rows(n_rows, step, carry0, reverse=False):
    groups = n_rows // 8

    def loop(i, carry):
        gi = (groups - 1 - i) if reverse else i
        return step(pl.multiple_of(gi * 8, 8), carry)

    return lax.fori_loop(0, groups, loop, carry0)


def fox_cum_fwd(f_logit, f_bias, name):
    B, S, L = f_logit.shape

    def body(f_ref, b_ref, c_ref):
        z = f_ref[0] + b_ref[...]
        c_ref[0] = jnp.minimum(z, 0.0) - _log1p(jnp.exp(-jnp.abs(z)))

        def step(r0, carry):
            h = _group_scan(None, c_ref[0, pl.ds(r0, 8), :], carry)
            c_ref[0, pl.ds(r0, 8), :] = h
            return h[7:8, :]

        _scan_rows(S, step, jnp.zeros((1, L), F32))

    return pl.pallas_call(
        body, name=name, grid=(B,),
        in_specs=[pl.BlockSpec((1, S, L), lambda b: (b, 0, 0)), pl.BlockSpec((1, L), lambda b: (0, 0))],
        out_specs=pl.BlockSpec((1, S, L), lambda b: (b, 0, 0)),
        out_shape=jax.ShapeDtypeStruct((B, S, L), F32), compiler_params=_params(("parallel",)),
    )(f_logit, f_bias)


def fox_cum_bwd(f_logit, f_bias, dcum, name):
    B, S, L = f_logit.shape

    def body(f_ref, b_ref, d_ref, df_ref, db_ref):
        @pl.when(pl.program_id(0) == 0)
        def _():
            db_ref[...] = jnp.zeros_like(db_ref)

        def step(r0, carry):
            h = _group_scan(None, d_ref[0, pl.ds(r0, 8), :], carry, reverse=True)
            df_ref[0, pl.ds(r0, 8), :] = h
            return h[0:1, :]

        _scan_rows(S, step, jnp.zeros((1, L), F32), reverse=True)
        df = df_ref[0] * _sigmoid(-(f_ref[0] + b_ref[...]))
        df_ref[0] = df
        db_ref[...] += jnp.sum(df, axis=0, keepdims=True)

    seq = pl.BlockSpec((1, S, L), lambda b: (b, 0, 0))
    vec = pl.BlockSpec((1, L), lambda b: (0, 0))
    return pl.pallas_call(
        body, name=name, grid=(B,), in_specs=[seq, vec, seq], out_specs=[seq, vec],
        out_shape=[jax.ShapeDtypeStruct((B, S, L), F32), jax.ShapeDtypeStruct((1, L), F32)],
        compiler_params=_params(("arbitrary",)),
    )(f_logit, f_bias, dcum)


def _head_row(cr, h):
    sub = lax.broadcasted_iota(jnp.int32, cr.shape, 0)
    return jnp.sum(jnp.where(sub == h, cr, 0.0), axis=0, keepdims=True)


def _fox_scores(qk, cc, ck, h, r0):
    lane = lax.broadcasted_iota(jnp.int32, cc.shape, 1)
    cq = jnp.sum(jnp.where(lane == h, cc, 0.0), axis=1, keepdims=True)
    s = qk * (HEAD_DIM ** -0.5) + (cq - ck)
    qpos = r0 + lax.broadcasted_iota(jnp.int32, s.shape, 0)
    kpos = lax.broadcasted_iota(jnp.int32, s.shape, 1)
    return jnp.where(kpos <= qpos, s, NEG_INF)


KEY_STEP = 512


def _by_causal_width(t, S, fn):
    per = KEY_STEP // TQ
    for c in range(S // KEY_STEP):
        pl.when(t // per == c)(functools.partial(fn, (c + 1) * KEY_STEP))


def fox_fwd(proj, cum_col, cum_row, name, ride=None):
    _, B, S, W = proj.shape
    nt = S // TQ

    def body(p_ref, cc_ref, cr_ref, o_ref, gt_ref, k_ref, v_ref, s_ref, pb_ref):
        h, t = pl.program_id(1), pl.program_id(2)

        @pl.when(t == 0)
        def _():
            k_ref[...] = p_ref[1, 0].astype(BF16)
            v_ref[...] = p_ref[2, 0].astype(BF16)

        r0 = pl.multiple_of(t * TQ, TQ)
        q = p_ref[0, 0, pl.ds(r0, TQ), :].astype(BF16)
        g = p_ref[3, 0, pl.ds(r0, TQ), :]

        def tile_out(width):
            s_ref[:, 0:width] = _dot_nt(q, k_ref[0:width, :])
            ck = _head_row(cr_ref[0, :, 0:width], h)

            def strip(i, carry):
                i0 = pl.multiple_of(i * STRIP, STRIP)
                part = pl.ds(i0, STRIP)
                s = _fox_scores(s_ref[part, 0:width], cc_ref[0, part, :], ck, h, r0 + i0)
                pb_ref[part, 0:width] = _softmax_rows(s).astype(BF16)
                return carry

            lax.fori_loop(0, TQ // STRIP, strip, 0)
            o = _dot_nn(pb_ref[:, 0:width], v_ref[0:width, :])
            o_ref[0] = o
            gt_ref[0] = (o * _silu(g)).astype(BF16)

        _by_causal_width(t, S, tile_out)

    seq, tile = _head_specs(S, "bht")
    return _call(
        body, name=name, grid=(B, HEADS, nt),
        in_specs=[seq, pl.BlockSpec((1, TQ, cum_col.shape[2]), lambda b, h, t: (b, t, 0)),
                  pl.BlockSpec((1, HEADS, S), lambda b, h, t: (b, 0, 0))],
        out_specs=[tile, tile],
        out_shape=[jax.ShapeDtypeStruct((B, S, W), F32), jax.ShapeDtypeStruct((B, S, W), BF16)],
        scratch_shapes=[pltpu.VMEM((S, HEAD_DIM), BF16), pltpu.VMEM((S, HEAD_DIM), BF16),
                        pltpu.VMEM((TQ, S), F32), pltpu.VMEM((TQ, S), BF16)],
        semantics=("parallel", "parallel", "arbitrary"), args=[proj, cum_col, cum_row], ride=ride)


def fox_bwd(proj, cum_col, cum_row, o, dgated, name, ride=None):
    _, B, S, W = proj.shape
    nt = S // TQ

    def body(p_ref, cc_ref, cr_ref, o_ref, dgt_ref, dp_ref, dc_ref, k_ref, v_ref, dk_ref, dv_ref,
             s_ref, d_ref, pb_ref, dsb_ref):
        h, t = pl.program_id(1), pl.program_id(2)

        @pl.when(t == 0)
        def _():
            k_ref[...] = p_ref[1, 0].astype(BF16)
            v_ref[...] = p_ref[2, 0].astype(BF16)
            dk_ref[...] = jnp.zeros_like(dk_ref)
            dv_ref[...] = jnp.zeros_like(dv_ref)
            dc_ref[...] = jnp.zeros_like(dc_ref)

        r0 = pl.multiple_of(t * TQ, TQ)
        rows = pl.ds(r0, TQ)
        q = p_ref[0, 0, rows, :].astype(BF16)
        g = p_ref[3, 0, rows, :]
        do, dg = _gate_bwd(dgt_ref[0], o_ref[0], g)
        do = do.astype(BF16)
        dp_ref[3, 0, rows, :] = dg.astype(BF16)

        def tile_grads(width):
            k, v = k_ref[0:width, :], v_ref[0:width, :]
            s_ref[:, 0:width] = _dot_nt(q, k)
            d_ref[:, 0:width] = _dot_nt(do, v)
            ck = _head_row(cr_ref[0, :, 0:width], h)

            def strip(i, col_sums):
                i0 = pl.multiple_of(i * STRIP, STRIP)
                part = pl.ds(i0, STRIP)
                p = _softmax_rows(_fox_scores(s_ref[part, 0:width], cc_ref[0, part, :], ck, h, r0 + i0))
                dpr = d_ref[part, 0:width]
                ds = p * (dpr - jnp.sum(p * dpr, axis=-1, keepdims=True))
                pb_ref[part, 0:width] = p.astype(BF16)
                dsb_ref[part, 0:width] = (ds * (HEAD_DIM ** -0.5)).astype(BF16)
                return col_sums + jnp.sum(ds, axis=0, keepdims=True)

            dc_ref[0, 0, :, 0:width] += lax.fori_loop(0, TQ // STRIP, strip, jnp.zeros((1, width), F32))
            ds = dsb_ref[:, 0:width]
            dv_ref[0:width, :] += _dot_tn(pb_ref[:, 0:width], do)
            dk_ref[0:width, :] += _dot_tn(ds, q)
            dp_ref[0, 0, rows, :] = _dot_nn(ds, k).astype(BF16)

        _by_causal_width(t, S, tile_grads)

        @pl.when(t == nt - 1)
        def _():
            dp_ref[1, 0] = dk_ref[...].astype(BF16)
            dp_ref[2, 0] = dv_ref[...].astype(BF16)

    seq, tile = _head_specs(S, "bht")
    return _call(
        body, name=name, grid=(B, HEADS, nt),
        in_specs=[seq, pl.BlockSpec((1, TQ, cum_col.shape[2]), lambda b, h, t: (b, t, 0)),
                  pl.BlockSpec((1, HEADS, S), lambda b, h, t: (b, 0, 0)), tile, tile],
        out_specs=[seq, pl.BlockSpec((1, 1, 1, S), lambda b, h, t: (b, h, 0, 0))],
        out_shape=[jax.ShapeDtypeStruct(proj.shape, BF16), jax.ShapeDtypeStruct((B, HEADS, 1, S), F32)],
        scratch_shapes=[pltpu.VMEM((S, HEAD_DIM), BF16), pltpu.VMEM((S, HEAD_DIM), BF16),
                        pltpu.VMEM((S, HEAD_DIM), F32), pltpu.VMEM((S, HEAD_DIM), F32),
                        pltpu.VMEM((TQ, S), F32), pltpu.VMEM((TQ, S), F32),
                        pltpu.VMEM((TQ, S), BF16), pltpu.VMEM((TQ, S), BF16)],
        semantics=("parallel", "parallel", "arbitrary"), args=[proj, cum_col, cum_row, o, dgated], ride=ride)


RG_ROWS = 512


def _rg_gates(xc, wa_ref, ba_ref, wx_ref, bx_ref, lam_ref):
    xcb = xc.astype(BF16)
    r = _sigmoid(_dot_nn(xcb, wa_ref[0]) + ba_ref[...])
    i = _sigmoid(_dot_nn(xcb, wx_ref[0]) + bx_ref[...])
    sp = _softplus(-lam_ref[...])
    log_a = (-RG_C * sp) * r
    a = jnp.exp(log_a)
    m = jnp.sqrt(-jnp.tanh(log_a) * (a * a + 1.0))
    return xcb, r, i, sp, a, m


def _rg_specs(B, S, rows, order):
    nc = S // rows

    def idx(fn):
        def index_map(*ids):
            v = dict(zip(order.lower(), ids))
            c = (nc - 1 - v["c"]) if "C" in order else v["c"]
            return fn(v["b"], v["d"], c)
        return index_map

    return dict(
        proj=pl.BlockSpec((2, 1, rows, RG_COLS), idx(lambda b, d, c: (0, b, c, d))),
        act=pl.BlockSpec((1, rows, RG_COLS), idx(lambda b, d, c: (b, c, d))),
        taps=pl.BlockSpec((CONV_WIDTH, RG_COLS), idx(lambda b, d, c: (0, d))),
        vec=pl.BlockSpec((1, RG_COLS), idx(lambda b, d, c: (0, d))),
        gate=pl.BlockSpec((1, RG_COLS, RG_COLS), idx(lambda b, d, c: (d, 0, 0))),
    )


def rglru_fwd(proj, conv_w, conv_b, wa, ba, wx, bx, lam, name, rows=RG_ROWS, ride=None):
    _, B, S, _ = proj.shape
    rows = min(rows, S)
    sp_ = _rg_specs(B, S, rows, "bdc")

    def body(p_ref, cw_ref, cb_ref, wa_ref, ba_ref, wx_ref, bx_ref, lam_ref,
             xc_ref, hs_ref, hp_ref, gt_ref, ext_ref, a_ref, u_ref, xcar_ref, hcar_ref):
        @pl.when(pl.program_id(2) == 0)
        def _():
            xcar_ref[...] = jnp.zeros_like(xcar_ref)
            hcar_ref[...] = jnp.zeros_like(hcar_ref)

        xr = p_ref[0, 0]
        ext_ref[0:8, :] = xcar_ref[...]
        ext_ref[8:, :] = xr
        xcar_ref[...] = xr[rows - 8:, :]
        xc = ext_ref[pl.ds(5, rows), :] * cw_ref[0:1, :]
        xc = xc + ext_ref[pl.ds(6, rows), :] * cw_ref[1:2, :]
        xc = xc + ext_ref[pl.ds(7, rows), :] * cw_ref[2:3, :]
        xc = xc + xr * cw_ref[3:4, :] + cb_ref[...]
        xc_ref[0] = xc
        _, _, i, _, a, m = _rg_gates(xc, wa_ref, ba_ref, wx_ref, bx_ref, lam_ref)
        a_ref[...] = a
        u_ref[...] = m * (i * xc)

        def step(r0, carry):
            h = _group_scan(a_ref[pl.ds(r0, 8), :], u_ref[pl.ds(r0, 8), :], carry)
            row = lax.broadcasted_iota(jnp.int32, h.shape, 0)
            hs_ref[0, pl.ds(r0, 8), :] = h
            hp_ref[0, pl.ds(r0, 8), :] = jnp.where(row == 0, carry, pltpu.roll(h, 1, 0))
            return h[7:8, :]

        hcar_ref[0:1, :] = _scan_rows(rows, step, hcar_ref[0:1, :])
        gt_ref[0] = (hs_ref[0] * _silu(p_ref[1, 0])).astype(BF16)

    act = jax.ShapeDtypeStruct((B, S, RG_WIDTH), F32)
    return _call(
        body, name=name, grid=(B, RG_GROUPS, S // rows),
        in_specs=[sp_["proj"], sp_["taps"], sp_["vec"], sp_["gate"], sp_["vec"], sp_["gate"], sp_["vec"], sp_["vec"]],
        out_specs=[sp_["act"]] * 4,
        out_shape=[act, act, act, jax.ShapeDtypeStruct((B, S, RG_WIDTH), BF16)],
        scratch_shapes=[pltpu.VMEM((rows + 8, RG_COLS), F32), pltpu.VMEM((rows, RG_COLS), F32),
                        pltpu.VMEM((rows, RG_COLS), F32), pltpu.VMEM((8, RG_COLS), F32), pltpu.VMEM((8, RG_COLS), F32)],
        semantics=("parallel", "parallel", "arbitrary"), args=[proj, conv_w, conv_b, wa, ba, wx, bx, lam], ride=ride)


def rglru_bwd(proj, xc, hs, hprev, dgated, conv_w, wa, ba, wx, bx, lam, name, rows=RG_ROWS, ride=None):
    _, B, S, _ = proj.shape
    rows = min(rows, S)
    sp_ = _rg_specs(B, S, rows, "dbC")

    def body(p_ref, xc_ref, hs_ref, hp_ref, dgt_ref, cw_ref, wa_ref, ba_ref, wx_ref, bx_ref, lam_ref,
             dp_ref, dcw_ref, dcb_ref, dwa_ref, dba_ref, dwx_ref, dbx_ref, dlam_ref,
             ext_ref, c_ref, l_ref, acar_ref, lcar_ref, dcar_ref):
        b_, c_ = pl.program_id(1), pl.program_id(2)

        @pl.when(c_ == 0)
        def _():
            acar_ref[...] = jnp.zeros_like(acar_ref)
            lcar_ref[...] = jnp.zeros_like(lcar_ref)
            dcar_ref[...] = jnp.zeros_like(dcar_ref)

        @pl.when((c_ == 0) & (b_ == 0))
        def _():
            for ref in (dcw_ref, dcb_ref, dwa_ref, dba_ref, dwx_ref, dbx_ref, dlam_ref):
                ref[...] = jnp.zeros_like(ref)

        xr, g = p_ref[0, 0], p_ref[1, 0]
        xc_v = xc_ref[0]
        xcb, r, i, sp, a, m = _rg_gates(xc_v, wa_ref, ba_ref, wx_ref, bx_ref, lam_ref)
        dhs, dg = _gate_bwd(dgt_ref[0], hs_ref[0], g)
        dp_ref[1, 0] = dg.astype(BF16)

        ext_ref[0:rows, :] = a
        ext_ref[rows:, :] = acar_ref[...]
        acar_ref[...] = a[0:8, :]
        c_ref[...] = ext_ref[pl.ds(1, rows), :]
        l_ref[...] = dhs

        def step(r0, carry):
            lam_g = _group_scan(c_ref[pl.ds(r0, 8), :], l_ref[pl.ds(r0, 8), :], carry, reverse=True)
            l_ref[pl.ds(r0, 8), :] = lam_g
            return lam_g[0:1, :]

        lcar_ref[0:1, :] = _scan_rows(rows, step, lcar_ref[0:1, :], reverse=True)
        du = l_ref[...]
        da = du * hp_ref[0]
        dlog_a = da * a - (du * (i * xc_v)) * (a * a / m)
        dr = dlog_a * (-RG_C * sp)
        dsp = jnp.sum(dlog_a * (-RG_C * r), axis=0, keepdims=True)
        dlam_ref[...] += dsp * (-_sigmoid(-lam_ref[...]))
        dpa = dr * (r * (1.0 - r))
        dpx = (du * (m * xc_v)) * (i * (1.0 - i))
        dba_ref[...] += jnp.sum(dpa, axis=0, keepdims=True)
        dbx_ref[...] += jnp.sum(dpx, axis=0, keepdims=True)
        dpa, dpx = dpa.astype(BF16), dpx.astype(BF16)
        dwa_ref[0] += _dot_tn(xcb, dpa)
        dwx_ref[0] += _dot_tn(xcb, dpx)
        dxc = du * (m * i) + _dot_nt(dpa, wa_ref[0]) + _dot_nt(dpx, wx_ref[0])

        dcb_ref[...] += jnp.sum(dxc, axis=0, keepdims=True)
        ext_ref[0:rows, :] = dxc
        ext_ref[rows:, :] = dcar_ref[...]
        dcar_ref[...] = dxc[0:8, :]
        dxr = jnp.zeros_like(dxc)
        for k in range(CONV_WIDTH):
            tap = CONV_WIDTH - 1 - k
            ahead = dxc if k == 0 else ext_ref[pl.ds(k, rows), :]
            dxr = dxr + ahead * cw_ref[tap:tap + 1, :]
            dcw_ref[tap:tap + 1, :] += jnp.sum(xr * ahead, axis=0, keepdims=True)
        dp_ref[0, 0] = dxr.astype(BF16)

    vec = jax.ShapeDtypeStruct((1, RG_WIDTH), F32)
    gate = jax.ShapeDtypeStruct((RG_GROUPS, RG_COLS, RG_COLS), F32)
    return _call(
        body, name=name, grid=(RG_GROUPS, B, S // rows),
        in_specs=[sp_["proj"], sp_["act"], sp_["act"], sp_["act"], sp_["act"], sp_["taps"],
                  sp_["gate"], sp_["vec"], sp_["gate"], sp_["vec"], sp_["vec"]],
        out_specs=[sp_["proj"], sp_["taps"], sp_["vec"], sp_["gate"], sp_["vec"], sp_["gate"], sp_["vec"], sp_["vec"]],
        out_shape=[jax.ShapeDtypeStruct(proj.shape, BF16), jax.ShapeDtypeStruct((CONV_WIDTH, RG_WIDTH), F32), vec,
                   gate, vec, gate, vec, vec],
        scratch_shapes=[pltpu.VMEM((rows + 8, RG_COLS), F32), pltpu.VMEM((rows, RG_COLS), F32),
                        pltpu.VMEM((rows, RG_COLS), F32), pltpu.VMEM((8, RG_COLS), F32),
                        pltpu.VMEM((8, RG_COLS), F32), pltpu.VMEM((8, RG_COLS), F32)],
        semantics=("arbitrary", "arbitrary", "arbitrary"),
        args=[proj, xc, hs, hprev, dgated, conv_w, wa, ba, wx, bx, lam], ride=ride)


def block_diag_gates(w):
    per = RG_COLS // RG_BLOCK
    w4 = w.reshape(RG_GROUPS, per, RG_BLOCK, RG_BLOCK)
    return jnp.einsum("dipq,ij->dipjq", w4, jnp.eye(per, dtype=w.dtype)).reshape(RG_GROUPS, RG_COLS, RG_COLS)


def block_diag_gates_t(dw):
    per = RG_COLS // RG_BLOCK
    dw6 = dw.reshape(RG_GROUPS, per, RG_BLOCK, per, RG_BLOCK)
    return jnp.stack([dw6[:, i, :, i, :] for i in range(per)], axis=1).reshape(RG_BLOCKS, RG_BLOCK, RG_BLOCK)


def adamw(w, parts, m, v, name, layer=0, prev=None, part_row0=0, row_tile=ROW_TILE):
    L, R, C = w.shape
    n_parts = parts.shape[0]
    br = row_tile if R % row_tile == 0 else R

    def body(w_ref, p_ref, m_ref, v_ref, *refs):
        g_ref, d_ref, nm_ref, nv_ref = refs[-4:]
        g = p_ref[0].astype(F32)
        for k in range(1, n_parts):
            g = g + p_ref[k].astype(F32)
        nm = ADAM_B1 * m_ref[0] + (1.0 - ADAM_B1) * g
        nv = ADAM_B2 * v_ref[0] + (1.0 - ADAM_B2) * (g * g)
        m_hat = nm / (1.0 - ADAM_B1 ** ADAM_STEP)
        v_hat = nv / (1.0 - ADAM_B2 ** ADAM_STEP)
        g_ref[0] = g
        d_ref[0] = -ADAM_LR * (m_hat / (jnp.sqrt(v_hat) + ADAM_EPS) + ADAM_WD * w_ref[0])
        nm_ref[0] = nm
        nv_ref[0] = nv

    slab = pl.BlockSpec((1, br, C), lambda i: (layer, i, 0))
    out = jax.ShapeDtypeStruct((L, R, C), F32)
    carried = [] if prev is None else list(prev)
    return _call(
        body, name=name, grid=(R // br,),
        in_specs=[slab, pl.BlockSpec((n_parts, br, C), lambda i: (0, part_row0 // br + i, 0)), slab, slab]
        + [pl.BlockSpec(memory_space=pl.ANY)] * len(carried),
        out_specs=[slab] * 4, out_shape=[out] * 4, semantics=("parallel",), args=[w, parts, m, v] + carried,
        aliases={4 + k: k for k in range(len(carried))})


def _seq(a, B):
    return a.reshape(a.shape[:-2] + (B, a.shape[-2] // B, a.shape[-1]))


def _flat(a):
    return a.reshape(a.shape[:-3] + (a.shape[-3] * a.shape[-2], a.shape[-1]))


def _tiles(w, which, **default):
    return dict(default, **w.get("tiles", {}).get(which, {}))


def mixer_a_fwd(h, w, B, tag, rides):
    proj = matmul(h, w["w_in"], mode="nn", out_dtype=F32, name=f"{tag}_proj", out_slabs=4,
                  ride=rides.pop(f"{tag}_proj", None), **_tiles(w, "proj"))
    o, gated = attn_a_fwd(_seq(proj, B), w["bias"], f"{tag}_attn", ride=rides.pop(f"{tag}_attn", None))
    return _flat(gated), dict(proj=proj, o=o)


def mixer_a_bwd(dgated, w, saved, B, tag, rides):
    dproj, dbias = attn_a_bwd(_seq(saved["proj"], B), w["bias"], saved["o"], _seq(dgated, B), f"{tag}_attn_bwd",
                              ride=rides.pop(f"{tag}_attn_bwd", None))
    return _flat(dproj), dict(bias=dbias)


def mixer_b_fwd(h, w, B, tag, rides):
    proj = matmul(h, w["w_in"], mode="nn", out_dtype=F32, name=f"{tag}_proj", out_slabs=2, bn=RG_COLS,
                  ride=rides.pop(f"{tag}_proj", None))
    xc, hs, hprev, gated = rglru_fwd(_seq(proj, B), w["conv_w"], w["conv_b"], w["wa"], w["ba"], w["wx"], w["bx"],
                                     w["lam"], f"{tag}_rglru", ride=rides.pop(f"{tag}_rglru", None))
    return _flat(gated), dict(proj=proj, xc=xc, hs=hs, hprev=hprev)


def mixer_b_bwd(dgated, w, saved, B, tag, rides):
    dproj, dcw, dcb, dwa, dba, dwx, dbx, dlam = rglru_bwd(
        _seq(saved["proj"], B), saved["xc"], saved["hs"], saved["hprev"], _seq(dgated, B),
        w["conv_w"], w["wa"], w["ba"], w["wx"], w["bx"], w["lam"], f"{tag}_rglru_bwd",
        ride=rides.pop(f"{tag}_rglru_bwd", None))
    return _flat(dproj), dict(conv_w=dcw, conv_b=dcb, wa=dwa, ba=dba, wx=dwx, bx=dbx, lam=dlam)


def mixer_c_fwd(h, w, B, tag, rides):
    proj = matmul(h, w["w_in"], mode="nn", out_dtype=F32, name=f"{tag}_proj", out_slabs=4,
                  ride=rides.pop(f"{tag}_proj", None), **_tiles(w, "proj"))
    f_logit = matmul(h, w["w_f"], mode="nn", out_dtype=F32, name=f"{tag}_fproj")
    cum = fox_cum_fwd(_seq(f_logit, B), w["f_bias"], f"{tag}_cum")
    cum_row = cum[:, :, :HEADS].transpose(0, 2, 1)
    o, gated = fox_fwd(_seq(proj, B), cum, cum_row, f"{tag}_attn", ride=rides.pop(f"{tag}_attn", None))
    return _flat(gated), dict(proj=proj, o=o, f_logit=f_logit, cum=cum, cum_row=cum_row)


def mixer_c_bwd(dgated, w, saved, B, tag, rides):
    dproj, dck = fox_bwd(_seq(saved["proj"], B), saved["cum"], saved["cum_row"], saved["o"], _seq(dgated, B),
                         f"{tag}_attn_bwd", ride=rides.pop(f"{tag}_attn_bwd", None))
    S = dck.shape[-1]
    dcum = jnp.pad(-dck.reshape(B, HEADS, S).transpose(0, 2, 1), ((0, 0), (0, 0), (0, HEAD_DIM - HEADS)))
    df, dfb = fox_cum_bwd(_seq(saved["f_logit"], B), w["f_bias"], dcum, f"{tag}_cum_bwd")
    return _flat(dproj), dict(f_bias=dfb, df=_flat(df).astype(BF16))


MIXERS = {"a": (mixer_a_fwd, mixer_a_bwd), "b": (mixer_b_fwd, mixer_b_bwd), "c": (mixer_c_fwd, mixer_c_bwd)}
LAYER_KINDS = "abca"


def local_step(x, target, norm_pre, norm_post, get_layer, rides, on_grads):
    B, S, Dm = x.shape
    n_layers = len(LAYER_KINDS)
    xs = [x.reshape(B * S, Dm)]
    saved, layers = [], []
    for li, kind in enumerate(LAYER_KINDS):
        tag = f"l{li}{kind}"
        h = prenorm_fwd(xs[-1], norm_pre[li:li + 1], f"{tag}_prenorm")
        w = get_layer(li)
        gated, sv = MIXERS[kind][0](h, w, B, tag, rides)
        if callable(w["w_out"]):
            w["w_out"] = w["w_out"]()
        y = matmul(gated, w["w_out"], mode="nn", out_dtype=F32, name=f"{tag}_out", ride=rides.pop(f"{tag}_out", None))
        xs.append(postnorm_fwd(xs[-1], y, norm_post[li:li + 1], f"{tag}_postnorm"))
        saved.append(dict(sv, h=h, gated=gated, y=y))
        layers.append(w)
    loss, dx = loss_fwd_bwd(xs[-1], target.reshape(B * S, Dm), "loss")

    for li in reversed(range(n_layers)):
        kind, w, sv = LAYER_KINDS[li], layers[li], saved[li]
        tag = f"l{li}{kind}"
        dy, dg_post = postnorm_bwd(sv["y"], norm_post[li:li + 1], dx, f"{tag}_postnorm_bwd")
        on_grads(li, "norm_post", dg_post)
        on_grads(li, "w_out", matmul(sv["gated"], dy, mode="tn", out_dtype=BF16, name=f"{tag}_dwout",
                                     ride=rides.pop(f"{tag}_dwout", None)))
        dgated = matmul(dy, w["w_out"], mode="nt", out_dtype=F32, name=f"{tag}_dgated",
                        ride=rides.pop(f"{tag}_dgated", None))
        dproj, gw = MIXERS[kind][1](dgated, w, sv, B, tag, rides)
        df = gw.pop("df", None)
        for name, value in gw.items():
            on_grads(li, name, value)
        n_parts = w.get("dwin_parts", 1)
        for part in range(n_parts):
            name = f"{tag}_dwin" + (f"_{part}" if n_parts > 1 else "")
            on_grads(li, "w_in" + (f"_{part}" if n_parts > 1 else ""),
                     matmul(sv["h"], dproj, mode="tn", out_dtype=BF16, name=name, out_slabs=w["grad_slabs"],
                            m_part=(part, n_parts), ride=rides.pop(name, None), **_tiles(w, "dwin")))
        if df is not None:
            on_grads(li, "w_f", matmul(sv["h"], df, mode="tn", out_dtype=BF16, name=f"{tag}_dwf"))
        dhs = [matmul(dproj, w["w_in"], mode="nt", out_dtype=F32, name=f"{tag}_dh",
                      ride=rides.pop(f"{tag}_dh", None), **_tiles(w, "dh"))]
        if df is not None:
            dhs.append(matmul(df, w["w_f"], mode="nt", out_dtype=F32, name=f"{tag}_dhf"))
        dx, dg_pre = prenorm_bwd(xs[li], norm_pre[li:li + 1], dhs, dx, f"{tag}_prenorm_bwd")
        on_grads(li, "norm_pre", dg_pre)
    assert not rides, list(rides)
    return loss, dx.reshape(B, S, Dm)


WEIGHTS = ("norm_pre", "norm_post", "a_w_in", "a_rel_bias", "a_w_out", "b_w_in", "b_conv_w", "b_conv_b",
           "b_gate_a_w", "b_gate_a_b", "b_gate_x_w", "b_gate_x_b", "b_lambda", "b_w_out", "c_w_in", "c_f_bias",
           "c_w_out")
C_SHARD = (4 * D_MODEL + HEADS) // N_DEV


def _rows(gathered):
    return gathered.reshape(gathered.shape[0] * gathered.shape[1], gathered.shape[2])


def layer_a(w_in, w_out, rel_bias):
    return dict(w_in=w_in, w_out=w_out if callable(w_out) else _rows(w_out), bias=band_bias(rel_bias),
                grad_slabs=N_DEV)


def layer_b(w_in, w_out, conv_w, small):
    return dict(
        w_in=w_in, w_out=_rows(w_out), grad_slabs=N_DEV,
        conv_w=conv_w.transpose(1, 0, 2).reshape(CONV_WIDTH, RG_WIDTH),
        conv_b=small["b_conv_b"], lam=small["b_lambda"],
        wa=block_diag_gates(small["b_gate_a_w"][0]).astype(BF16), ba=small["b_gate_a_b"].reshape(1, RG_WIDTH),
        wx=block_diag_gates(small["b_gate_x_w"][0]).astype(BF16), bx=small["b_gate_x_b"].reshape(1, RG_WIDTH))


def layer_c(w_in, w_out, small):
    full = w_in.transpose(1, 0, 2).reshape(D_MODEL, N_DEV * C_SHARD)
    return dict(w_in=full[:, :4 * D_MODEL], w_f=jnp.pad(full[:, 4 * D_MODEL:], ((0, 0), (0, HEAD_DIM - HEADS))),
                w_out=_rows(w_out), grad_slabs=1,
                f_bias=jnp.pad(small["c_f_bias"], ((0, 0), (0, HEAD_DIM - HEADS))))


def c_w_in_blocks(dmain, df):
    full = jnp.concatenate([dmain, df[:, :HEADS].astype(dmain.dtype)], axis=1)
    return full.reshape(D_MODEL, N_DEV, C_SHARD).transpose(1, 0, 2)


def _row_blocks(g):
    return g.reshape(N_DEV, g.shape[0] // N_DEV, g.shape[1])


PACK_LANES = 128
PACK_ALIGN = 8 * PACK_LANES


def pack(parts):
    flat = []
    for p in parts:
        n = p.size
        flat.append(jnp.pad(p.reshape(n), (0, -n % PACK_ALIGN)).reshape(-1, PACK_LANES))
    rows = sum(f.shape[0] for f in flat)
    flat.append(jnp.zeros((-rows % ROW_TILE, PACK_LANES), F32))
    return jnp.concatenate(flat, axis=0)


def unpack(packed, shapes):
    out, row = [], 0
    for shape in shapes:
        n = 1
        for s in shape:
            n *= s
        n_rows = (n + PACK_ALIGN - 1) // PACK_ALIGN * 8
        out.append(packed[row:row + n_rows].reshape(-1)[:n].reshape(shape))
        row += n_rows
    return out


LATE = (("a_rel_bias", slice(0, 1)), ("norm_pre", slice(0, 2)), ("norm_post", slice(0, 1)))
EARLY = (("a_rel_bias", slice(1, 2)), ("norm_pre", slice(2, 4)), ("norm_post", slice(1, 4)),
         ("b_conv_b", slice(None)), ("b_gate_a_w", slice(None)), ("b_gate_a_b", slice(None)),
         ("b_gate_x_w", slice(None)), ("b_gate_x_b", slice(None)), ("b_lambda", slice(None)),
         ("c_f_bias", slice(None)))


def _pieces(tree, pieces):
    return [tree[name][sl] for name, sl in pieces]


def kernel(x, norm_pre, norm_post, a_w_in, a_rel_bias, a_w_out, b_w_in, b_conv_w, b_conv_b, b_gate_a_w, b_gate_a_b, b_gate_x_w, b_gate_x_b, b_lambda, b_w_out, c_w_in, c_f_bias, c_w_out, loss_target, m_norm_pre, m_norm_post, m_a_w_in, m_a_rel_bias, m_a_w_out, m_b_w_in, m_b_conv_w, m_b_conv_b, m_b_gate_a_w, m_b_gate_a_b, m_b_gate_x_w, m_b_gate_x_b, m_b_lambda, m_b_w_out, m_c_w_in, m_c_f_bias, m_c_w_out, v_norm_pre, v_norm_post, v_a_w_in, v_a_rel_bias, v_a_w_out, v_b_w_in, v_b_conv_w, v_b_conv_b, v_b_gate_a_w, v_b_gate_a_b, v_b_gate_x_w, v_b_gate_x_b, v_b_lambda, v_b_w_out, v_c_w_in, v_c_f_bias, v_c_w_out):
    args = dict(locals())
    w = {n: args[n] for n in WEIGHTS}
    m = {n: args["m_" + n] for n in WEIGHTS}
    v = {n: args["v_" + n] for n in WEIGHTS}

    a_in, a_out = a_w_in.astype(BF16), a_w_out.astype(BF16)
    gather_a0 = Ride([a_in[0]], scatter=False, via_sibling=True)
    gather_b = Ride([b_w_in[0].astype(BF16), b_w_out[0].astype(BF16), b_conv_w[0], a_out[0]], scatter=False,
                    via_sibling=True)
    gather_c_in = Ride([c_w_in[0].astype(BF16)], scatter=False, via_sibling=True)
    gather_c_out = Ride([c_w_out[0].astype(BF16)], scatter=False)
    gather_a1 = Ride([a_in[1], a_out[1]], scatter=False, via_sibling=True)
    exchange(gather_a0, "gather_l0")
    rides = {"l0a_proj": gather_b, "l0a_attn": gather_c_in, "l1b_rglru": gather_c_out, "l2c_attn": gather_a1}

    def get_layer(li):
        if li == 0:
            return dict(layer_a(gather_a0.out[0], lambda: _rows(gather_b.out[3]), a_rel_bias[0]), dwin_parts=2)
        if li == 1:
            return layer_b(*gather_b.out[:3], w)
        if li == 2:
            return dict(layer_c(gather_c_in.out[0], gather_c_out.out[0], w),
                        tiles=dict(proj=dict(bn=2048)))
        return layer_a(*gather_a1.out, a_rel_bias[1])

    grads = [dict() for _ in LAYER_KINDS]
    scatters = {}

    def rel_bias_grad(j, dbias):
        return jax.vjp(band_bias, a_rel_bias[j])[1](dbias)[0][None]

    def early_partial():
        gb, gc = grads[1], grads[2]
        tree = dict(
            a_rel_bias=jnp.concatenate([jnp.zeros((1, HEADS, N_REL), F32), rel_bias_grad(1, grads[3]["bias"])]),
            norm_pre=jnp.concatenate([jnp.zeros((2, D_MODEL), F32)] + [grads[li]["norm_pre"] for li in (2, 3)]),
            norm_post=jnp.concatenate([jnp.zeros((1, D_MODEL), F32)] + [grads[li]["norm_post"] for li in (1, 2, 3)]),
            b_conv_b=gb["conv_b"], b_lambda=gb["lam"],
            b_gate_a_w=block_diag_gates_t(gb["wa"])[None], b_gate_a_b=gb["ba"].reshape(1, RG_BLOCKS, RG_BLOCK),
            b_gate_x_w=block_diag_gates_t(gb["wx"])[None], b_gate_x_b=gb["bx"].reshape(1, RG_BLOCKS, RG_BLOCK),
            c_f_bias=gc["f_bias"][:, :HEADS])
        return pack(_pieces(tree, EARLY))

    def send(key, host, blocks, scatter=True, via_sibling=False):
        ride = rides.setdefault(host, Ride([], scatter, via_sibling))
        assert (ride.scatter, ride.via_sibling) == (scatter, via_sibling)
        scatters[key] = (ride, len(ride.arrs))
        ride.arrs.append(blocks)

    def on_grads(li, name, value):
        g = grads[li]
        g[name] = value
        if (li, name) == (3, "w_out"):
            send("a1_out", "l3a_attn_bwd", _row_blocks(value))
        elif (li, name) == (3, "w_in"):
            send("a1_in", "l2c_attn_bwd", value)
        elif (li, name) == (2, "w_out"):
            send("c_out", "l2c_attn_bwd", _row_blocks(value))
        elif (li, name) == (2, "w_f"):
            blocks = c_w_in_blocks(g["w_in"], value)
            send("c_in_0", "l2c_dh", blocks[:, :D_MODEL // 2])
            send("c_in_1", "l1b_rglru_bwd", blocks[:, D_MODEL // 2:])
        elif (li, name) == (1, "w_out"):
            send("b_out", "l1b_dh", _row_blocks(value))
        elif (li, name) == (1, "w_in"):
            send("b_in", "l0a_attn_bwd", value)
            send("b_conv", "l0a_attn_bwd",
                 g["conv_w"].reshape(CONV_WIDTH, N_DEV, RG_WIDTH // N_DEV).transpose(1, 0, 2))
        elif (li, name) == (1, "lam"):
            send("early", "l1b_dwin", early_partial(), scatter=False, via_sibling=True)
        elif (li, name) == (0, "w_out"):
            send("a0_out", "l0a_attn_bwd", _row_blocks(value))
        elif (li, name) == (0, "w_in_0"):
            send("a0_in_0", "l0a_dwin_1", value)
        elif (li, name) == (0, "w_in_1"):
            send("a0_in_1", "l0a_dh", value)

    loss, grad_x = local_step(x, loss_target, norm_pre, norm_post, get_layer, rides, on_grads)
    late_tree = dict(a_rel_bias=rel_bias_grad(0, grads[0]["bias"]), norm_post=grads[0]["norm_post"],
                     norm_pre=jnp.concatenate([grads[0]["norm_pre"], grads[1]["norm_pre"]]))
    late_parts = exchange(Ride([pack([late_tree[n] for n, _ in LATE])], scatter=False), "gather_late_grads")[0]

    def sharded(name, slab_parts):
        shape = w[name].shape
        slabs = (len(slab_parts), shape[0] * shape[1] // len(slab_parts), shape[2])
        outs = None
        for j, (parts, row0) in enumerate(slab_parts):
            outs = adamw(w[name].reshape(slabs), parts, m[name].reshape(slabs), v[name].reshape(slabs),
                         f"adamw_{name}_{j}", layer=j, prev=outs, part_row0=row0)
        return [o.reshape(shape) for o in outs]

    def received(key):
        ride, position = scatters[key]
        return ride.out[position]

    res = dict(
        a_w_in=sharded("a_w_in", [(received("a0_in_0"), 0), (received("a0_in_1"), 0),
                                  (received("a1_in"), 0), (received("a1_in"), D_MODEL // 2)]),
        a_w_out=sharded("a_w_out", [(received("a0_out"), 0), (received("a1_out"), 0)]),
        b_w_in=sharded("b_w_in", [(received("b_in"), 0)]),
        b_w_out=sharded("b_w_out", [(received("b_out"), 0)]),
        b_conv_w=sharded("b_conv_w", [(received("b_conv"), 0)]),
        c_w_in=sharded("c_w_in", [(received("c_in_0"), 0), (received("c_in_1"), 0)]),
        c_w_out=sharded("c_w_out", [(received("c_out"), 0)]))

    packed = {}
    for label, pieces, parts in (("early", EARLY, received("early")), ("late", LATE, late_parts)):
        outs = adamw(pack(_pieces(w, pieces))[None], parts, pack(_pieces(m, pieces))[None],
                     pack(_pieces(v, pieces))[None], f"adamw_replicated_{label}")
        shapes = [w[n][sl].shape for n, sl in pieces]
        packed[label] = [dict(zip([n for n, _ in pieces], unpack(o[0], shapes))) for o in outs]
    for n in ("b_conv_b", "b_gate_a_w", "b_gate_a_b", "b_gate_x_w", "b_gate_x_b", "b_lambda", "c_f_bias"):
        res[n] = [packed["early"][k][n] for k in range(4)]
    for n in ("a_rel_bias", "norm_pre", "norm_post"):
        res[n] = [jnp.concatenate([packed["late"][k][n], packed["early"][k][n]]) for k in range(4)]

    total = lax.psum(loss[0, 0], ("x", "y", "c"))
    return (total, grad_x, *[res[n][0] for n in WEIGHTS], *[res[n][1] for n in WEIGHTS],
            *[res[n][2] for n in WEIGHTS], *[res[n][3] for n in WEIGHTS])
```

```python
import functools

import jax
import jax.numpy as jnp
from jax import lax
from jax.experimental import pallas as pl
from jax.experimental.pallas import tpu as pltpu

F32 = jnp.float32
BF16 = jnp.bfloat16

N_DEV = 8
D_MODEL = 2048
HEADS = 16
HEAD_DIM = 128
CHUNK = 64
LEFT_CHUNKS = 8
REL_CLIP = 256
N_REL = 2 * REL_CLIP + 1
TQ = 256
A_PAD = LEFT_CHUNKS * CHUNK
A_KW = A_PAD + TQ
RG_WIDTH = 2560
RG_BLOCKS = 16
RG_BLOCK = 160
RG_COLS = 640
RG_GROUPS = RG_WIDTH // RG_COLS
RG_C = 8.0
CONV_WIDTH = 4
RMS_EPS = 1e-6
NEG_INF = -1e30
ADAM_LR = 0.001
ADAM_B1 = 0.9
ADAM_B2 = 0.999
ADAM_EPS = 1e-08
ADAM_WD = 0.01
ADAM_STEP = 10
VMEM_LIMIT = 56 * 1024 * 1024
MESH = pl.DeviceIdType.MESH


def _params(sem, vmem=VMEM_LIMIT):
    return pltpu.CompilerParams(dimension_semantics=sem, vmem_limit_bytes=vmem)


def _sigmoid(x):
    return 1.0 / (1.0 + jnp.exp(-x))


def _log1p(y):
    u = 1.0 + y
    return jnp.where(u == 1.0, y, jnp.log(u) * (y / jnp.where(u == 1.0, 1.0, u - 1.0)))


def _softplus(x):
    return jnp.maximum(x, 0.0) + _log1p(jnp.exp(-jnp.abs(x)))


def _dot(a, b, dims):
    return lax.dot_general(a, b, (dims, ((), ())), preferred_element_type=F32)


def _dot_nn(a, b):
    return _dot(a, b, ((1,), (0,)))


def _dot_nt(a, b):
    return _dot(a, b, ((1,), (1,)))


def _dot_tn(a, b):
    return _dot(a, b, ((0,), (0,)))


def _peers():
    x, y, c = lax.axis_index("x"), lax.axis_index("y"), lax.axis_index("c")
    me = 4 * x + 2 * y + c
    peers = []
    for k in range(1, N_DEV):
        px = 1 - x if k & 4 else x
        py = 1 - y if k & 2 else y
        pc = 1 - c if k & 1 else c
        peers.append(((px, py, pc), 4 * px + 2 * py + pc))
    return me, peers


class Ride:
    def __init__(self, arrs, scatter, via_sibling=False):
        assert not (scatter and via_sibling)
        self.arrs, self.scatter, self.via_sibling, self.out = list(arrs), scatter, via_sibling, None

    def out_shapes(self):
        return [jax.ShapeDtypeStruct(a.shape if self.scatter else (N_DEV,) + a.shape, a.dtype) for a in self.arrs]

    def sem_shapes(self):
        n = len(self.arrs)
        return [pltpu.SemaphoreType.DMA((n, N_DEV - 1)), pltpu.SemaphoreType.DMA((n, N_DEV - 1)),
                pltpu.SemaphoreType.DMA((n,))]

    def _copies(self, ins, outs, sems, landing):
        send_sems, recv_sems, local_sems = sems
        me, peers = _peers()
        local, remote = [], []
        for a, (src, dst) in enumerate(zip(ins, outs)):
            local.append(pltpu.make_async_copy(src.at[me] if self.scatter else src, dst.at[me], local_sems.at[a]))
            for k, (peer, peer_idx) in enumerate(peers):
                remote.append(pltpu.make_async_remote_copy(
                    src_ref=src.at[peer_idx] if self.scatter else src, dst_ref=dst.at[peer_idx if landing else me],
                    send_sem=send_sems.at[a, k], recv_sem=recv_sems.at[a, k], device_id=peer, device_id_type=MESH))
        return local, remote

    def _direct(self, k):
        return not self.via_sibling or k == 0 or (k + 1) % 2 == 0

    def start(self, ins, outs, sems):
        local, remote = self._copies(ins, outs, sems, landing=False)
        n_peers = N_DEV - 1
        for cp in local + [cp for i, cp in enumerate(remote) if self._direct(i % n_peers)]:
            cp.start()

    def wait(self, ins, outs, sems):
        local, remote = self._copies(ins, outs, sems, landing=True)
        n_peers = N_DEV - 1
        for i, cp in enumerate(remote):
            if self._direct(i % n_peers):
                cp.wait()
        if self.via_sibling:
            send_sems, recv_sems, _ = sems
            me, peers = _peers()
            sibling = peers[0][0]
            passed = []
            for a, dst in enumerate(outs):
                for j in range(1, n_peers, 2):
                    came, lands = peers[j][1], peers[j + 1][1]
                    pltpu.make_async_remote_copy(
                        src_ref=dst.at[came], dst_ref=dst.at[came], send_sem=send_sems.at[a, j + 1],
                        recv_sem=recv_sems.at[a, j + 1], device_id=sibling, device_id_type=MESH).start()
                    passed.append(pltpu.make_async_remote_copy(
                        src_ref=dst.at[came], dst_ref=dst.at[lands], send_sem=send_sems.at[a, j + 1],
                        recv_sem=recv_sems.at[a, j + 1], device_id=sibling, device_id_type=MESH))
            for cp in passed:
                cp.wait()
        for cp in local:
            cp.wait()


def _call(body, *, name, grid, in_specs, out_specs, out_shape, args, scratch_shapes=(), semantics=None, ride=None,
          aliases=None):
    scratch_shapes = list(scratch_shapes)
    if ride is None:
        return pl.pallas_call(
            body, name=name, grid=grid, in_specs=in_specs, out_specs=out_specs, out_shape=out_shape,
            scratch_shapes=scratch_shapes, input_output_aliases=aliases or {},
            compiler_params=_params(semantics if grid else None))(*args)
    assert not aliases
    n_in, n_out, n_sc, n_r = len(in_specs), len(out_specs), len(scratch_shapes), len(ride.arrs)

    def riding(*refs):
        ins, r_ins = refs[:n_in], refs[n_in:n_in + n_r]
        outs, r_outs = refs[n_in + n_r:n_in + n_r + n_out], refs[n_in + n_r + n_out:n_in + 2 * n_r + n_out]
        rest = refs[n_in + 2 * n_r + n_out:]
        scratch, sems = rest[:n_sc], rest[n_sc:]
        first = last = None
        for axis, size in enumerate(grid):
            pid = pl.program_id(axis)
            first = (pid == 0) if first is None else first & (pid == 0)
            last = (pid == size - 1) if last is None else last & (pid == size - 1)
        if grid:
            pl.when(first)(lambda: ride.start(r_ins, r_outs, sems))
        else:
            ride.start(r_ins, r_outs, sems)
        body(*ins, *outs, *scratch)
        if grid:
            pl.when(last)(lambda: ride.wait(r_ins, r_outs, sems))
        else:
            ride.wait(r_ins, r_outs, sems)

    any_spec = pl.BlockSpec(memory_space=pl.ANY)
    res = pl.pallas_call(
        riding, name=name, grid=grid, in_specs=list(in_specs) + [any_spec] * n_r,
        out_specs=list(out_specs) + [any_spec] * n_r, out_shape=list(out_shape) + ride.out_shapes(),
        scratch_shapes=scratch_shapes + ride.sem_shapes(),
        compiler_params=_params(("arbitrary",) * len(grid) if grid else None))(*args, *ride.arrs)
    ride.out = list(res[n_out:])
    return list(res[:n_out])


def exchange(ride, name):
    _call(lambda: None, name=name, grid=(), in_specs=[], out_specs=[], out_shape=[], args=[], ride=ride)
    return ride.out


LANES = 128


def _fit(dims, want):
    dims = tuple(dims)
    if len(set(dims)) == 1 and dims[0] <= want:
        return dims[0]
    return max(t for t in range(LANES, want + 1, LANES) if all(d % t == 0 for d in dims))


def _cols(arr):
    return arr.shape[-1] * (arr.shape[0] if len(arr.shape) == 3 else 1)


def _tile_spec(shape, rblk, cblk, rc):
    if len(shape) == 2:
        return pl.BlockSpec((rblk, cblk), rc)
    per = shape[2] // cblk

    def index_map(*ids):
        r, c = rc(*ids)
        return (c // per, r, c % per)

    return pl.BlockSpec((1, rblk, cblk), index_map)


def matmul(a, b, *, mode, out_dtype, name, bm=1024, bn=1024, bk=2048, out_slabs=1, m_part=(0, 1, 1), ride=None):
    a_rows, a_cols, b_rows, b_cols = a.shape[-2], _cols(a), b.shape[-2], _cols(b)
    (K, M) = (a_rows, a_cols) if mode == "tn" else (a_cols, a_rows)
    N = b_rows if mode == "nt" else b_cols
    assert K == (b_cols if mode == "nt" else b_rows), (name, a.shape, b.shape)
    first_range, n_ranges, of_ranges = m_part
    row0, M = first_range * (M // of_ranges), n_ranges * (M // of_ranges)
    out_shape = (M, N) if out_slabs == 1 else (out_slabs, M, N // out_slabs)
    widths = dict(m=[M], n=[N, out_shape[-1]], k=[K])
    widths["m" if mode == "tn" else "k"].append(a.shape[-1])
    widths["k" if mode == "nt" else "n"].append(b.shape[-1])
    bm, bn, bk = _fit(widths["m"], bm), _fit(widths["n"], bn), _fit(widths["k"], bk)
    nk = K // bk
    dims = {"nn": ((1,), (0,)), "nt": ((1,), (1,)), "tn": ((0,), (0,))}[mode]

    def val(ref):
        return ref[0] if len(ref.shape) == 3 else ref[...]

    def put(ref, x):
        if len(ref.shape) == 3:
            ref[0] = x.astype(ref.dtype)
        else:
            ref[...] = x.astype(ref.dtype)

    def body(a_ref, b_ref, o_ref, *scratch):
        if nk == 1:
            put(o_ref, _dot(val(a_ref), val(b_ref), dims))
            return
        acc_ref, = scratch
        k = pl.program_id(2)

        @pl.when(k == 0)
        def _():
            acc_ref[...] = jnp.zeros_like(acc_ref)

        acc_ref[...] += _dot(val(a_ref), val(b_ref), dims)

        @pl.when(k == nk - 1)
        def _():
            put(o_ref, acc_ref[...])

    assert row0 % bm == 0
    m0 = row0 // bm
    if mode == "tn":
        a_spec = _tile_spec(a.shape, bk, bm, lambda j, i, k: (k, m0 + i))
    else:
        a_spec = _tile_spec(a.shape, bm, bk, lambda j, i, k: (m0 + i, k))
    if mode == "nt":
        b_spec = _tile_spec(b.shape, bn, bk, lambda j, i, k: (j, k))
    else:
        b_spec = _tile_spec(b.shape, bk, bn, lambda j, i, k: (k, j))
    return _call(
        body, name=name, grid=(N // bn, M // bm, nk), in_specs=[a_spec, b_spec],
        out_specs=[_tile_spec(out_shape, bm, bn, lambda j, i, k: (i, j))],
        out_shape=[jax.ShapeDtypeStruct(out_shape, out_dtype)],
        scratch_shapes=[] if nk == 1 else [pltpu.VMEM((bm, bn), F32)],
        semantics=("parallel", "parallel", "arbitrary"), args=[a, b], ride=ride)[0]


ROW_TILE = 256


def _rms_stats(z):
    r = lax.rsqrt(jnp.mean(z * z, axis=-1, keepdims=True) + RMS_EPS)
    return r, z * r


def _rms_bwd(n, r, g, dout):
    dn = dout * g
    return r * (dn - n * jnp.mean(dn * n, axis=-1, keepdims=True))


def _row_spec(T, Dm):
    bt = min(ROW_TILE, T)
    return bt, pl.BlockSpec((bt, Dm), lambda i: (i, 0)), pl.BlockSpec((1, Dm), lambda i: (0, 0))


def prenorm_fwd(x, g, name):
    T, Dm = x.shape
    bt, row, vec = _row_spec(T, Dm)

    def body(x_ref, g_ref, h_ref):
        _, n = _rms_stats(x_ref[...])
        h_ref[...] = (n * g_ref[...]).astype(BF16)

    return pl.pallas_call(
        body, name=name, grid=(T // bt,), in_specs=[row, vec], out_specs=row,
        out_shape=jax.ShapeDtypeStruct((T, Dm), BF16), compiler_params=_params(("parallel",)),
    )(x, g)


def postnorm_prenorm_fwd(x, y, g_post, g_next, name):
    T, Dm = x.shape
    bt, row, vec = _row_spec(T, Dm)

    def body(x_ref, y_ref, gp_ref, gn_ref, o_ref, h_ref):
        _, n = _rms_stats(y_ref[...])
        x_new = x_ref[...] + n * gp_ref[...]
        o_ref[...] = x_new
        _, n_new = _rms_stats(x_new)
        h_ref[...] = (n_new * gn_ref[...]).astype(BF16)

    return pl.pallas_call(
        body, name=name, grid=(T // bt,), in_specs=[row, row, vec, vec], out_specs=[row, row],
        out_shape=[jax.ShapeDtypeStruct((T, Dm), F32), jax.ShapeDtypeStruct((T, Dm), BF16)],
        compiler_params=_params(("parallel",)),
    )(x, y, g_post, g_next)


def postnorm_loss(x, y, g, target, name):
    T, Dm = x.shape
    bt, row, vec = _row_spec(T, Dm)

    def body(x_ref, y_ref, g_ref, t_ref, l_ref, d_ref):
        @pl.when(pl.program_id(0) == 0)
        def _():
            l_ref[...] = jnp.zeros_like(l_ref)

        _, n = _rms_stats(y_ref[...])
        err = (x_ref[...] + n * g_ref[...]) - t_ref[...]
        per_tok = jnp.mean(err * err, axis=-1, keepdims=True)
        l_ref[...] += 0.5 * jnp.sum(per_tok, axis=0, keepdims=True)
        d_ref[...] = err * (1.0 / Dm)

    return pl.pallas_call(
        body, name=name, grid=(T // bt,), in_specs=[row, row, vec, row],
        out_specs=[pl.BlockSpec((1, 1), lambda i: (0, 0)), row],
        out_shape=[jax.ShapeDtypeStruct((1, 1), F32), jax.ShapeDtypeStruct((T, Dm), F32)],
        compiler_params=_params(("arbitrary",)),
    )(x, y, g, target)


def postnorm_bwd(y, g, dout, name):
    T, Dm = y.shape
    bt, row, vec = _row_spec(T, Dm)

    def body(y_ref, g_ref, d_ref, dy_ref, dg_ref):
        @pl.when(pl.program_id(0) == 0)
        def _():
            dg_ref[...] = jnp.zeros_like(dg_ref)

        r, n = _rms_stats(y_ref[...])
        dout_v = d_ref[...]
        dg_ref[...] += jnp.sum(dout_v * n, axis=0, keepdims=True)
        dy_ref[...] = _rms_bwd(n, r, g_ref[...], dout_v).astype(BF16)

    return pl.pallas_call(
        body, name=name, grid=(T // bt,), in_specs=[row, vec, row], out_specs=[row, vec],
        out_shape=[jax.ShapeDtypeStruct((T, Dm), BF16), jax.ShapeDtypeStruct((1, Dm), F32)],
        compiler_params=_params(("arbitrary",)),
    )(y, g, dout)


def prenorm_bwd(x, g, dhs, dres, name):
    T, Dm = x.shape
    bt, row, vec = _row_spec(T, Dm)
    n_dh = len(dhs)

    def body(x_ref, g_ref, *refs):
        dh_refs, (dr_ref, dx_ref, dg_ref) = refs[:n_dh], refs[n_dh:]

        @pl.when(pl.program_id(0) == 0)
        def _():
            dg_ref[...] = jnp.zeros_like(dg_ref)

        r, n = _rms_stats(x_ref[...])
        dh_v = dh_refs[0][...]
        for extra in dh_refs[1:]:
            dh_v = dh_v + extra[...]
        dg_ref[...] += jnp.sum(dh_v * n, axis=0, keepdims=True)
        dx_ref[...] = dr_ref[...] + _rms_bwd(n, r, g_ref[...], dh_v)

    return pl.pallas_call(
        body, name=name, grid=(T // bt,), in_specs=[row, vec] + [row] * (n_dh + 1), out_specs=[row, vec],
        out_shape=[jax.ShapeDtypeStruct((T, Dm), F32), jax.ShapeDtypeStruct((1, Dm), F32)],
        compiler_params=_params(("arbitrary",)),
    )(x, g, *dhs, dres)


def _silu(g):
    return g * _sigmoid(g)


def _gate_bwd(dgated, core, g):
    sg = _sigmoid(g)
    return dgated * (g * sg), dgated * core * (sg * (1.0 + g * (1.0 - sg)))


def _softmax_rows(s):
    e = jnp.exp(s - jnp.max(s, axis=-1, keepdims=True))
    return e * (1.0 / jnp.sum(e, axis=-1, keepdims=True))


def _band_scores(qk, bias, r0):
    s = qk * (HEAD_DIM ** -0.5) + bias
    j = lax.broadcasted_iota(jnp.int32, s.shape, 1)
    return jnp.where(j >= A_PAD - r0, s, NEG_INF)


def _fill_padded_kv(p_ref, kp_ref, vp_ref):
    zeros = jnp.zeros((A_PAD, HEAD_DIM), BF16)
    kp_ref[0:A_PAD, :] = zeros
    vp_ref[0:A_PAD, :] = zeros
    kp_ref[A_PAD:, :] = p_ref[1, 0].astype(BF16)
    vp_ref[A_PAD:, :] = p_ref[2, 0].astype(BF16)


def _head_specs(S, order):
    def idx(fn):
        return lambda *ids: fn(**dict(zip(order, ids)))

    return (pl.BlockSpec((4, 1, S, HEAD_DIM), idx(lambda b, h, t: (0, b, 0, h))),
            pl.BlockSpec((1, TQ, HEAD_DIM), idx(lambda b, h, t: (b, t, h))))


def attn_a_fwd(proj, bias, name, ride=None):
    _, B, S, W = proj.shape
    nt = S // TQ

    def body(p_ref, b_ref, o_ref, gt_ref, kp_ref, vp_ref):
        t = pl.program_id(2)

        @pl.when(t == 0)
        def _():
            _fill_padded_kv(p_ref, kp_ref, vp_ref)

        r0 = pl.multiple_of(t * TQ, TQ)
        q = p_ref[0, 0, pl.ds(r0, TQ), :].astype(BF16)
        g = p_ref[3, 0, pl.ds(r0, TQ), :]
        p = _softmax_rows(_band_scores(_dot_nt(q, kp_ref[pl.ds(r0, A_KW), :]), b_ref[0], r0))
        o = _dot_nn(p.astype(BF16), vp_ref[pl.ds(r0, A_KW), :])
        o_ref[0] = o
        gt_ref[0] = (o * _silu(g)).astype(BF16)

    seq, tile = _head_specs(S, "bht")
    return _call(
        body, name=name, grid=(B, HEADS, nt),
        in_specs=[seq, pl.BlockSpec((1, TQ, A_KW), lambda b, h, t: (h, 0, 0))], out_specs=[tile, tile],
        out_shape=[jax.ShapeDtypeStruct((B, S, W), F32), jax.ShapeDtypeStruct((B, S, W), BF16)],
        scratch_shapes=[pltpu.VMEM((A_PAD + S, HEAD_DIM), BF16), pltpu.VMEM((A_PAD + S, HEAD_DIM), BF16)],
        semantics=("parallel", "parallel", "arbitrary"), args=[proj, bias], ride=ride)


def attn_a_bwd(proj, bias, o, dgated, name, ride=None):
    _, B, S, W = proj.shape
    nt = S // TQ

    def body(p_ref, b_ref, o_ref, dgt_ref, dp_ref, db_ref, kp_ref, vp_ref, dk_ref, dv_ref):
        b_, t = pl.program_id(1), pl.program_id(2)

        @pl.when(t == 0)
        def _():
            _fill_padded_kv(p_ref, kp_ref, vp_ref)
            dk_ref[...] = jnp.zeros_like(dk_ref)
            dv_ref[...] = jnp.zeros_like(dv_ref)

        @pl.when((t == 0) & (b_ == 0))
        def _():
            db_ref[...] = jnp.zeros_like(db_ref)

        r0 = pl.multiple_of(t * TQ, TQ)
        rows, win = pl.ds(r0, TQ), pl.ds(r0, A_KW)
        q = p_ref[0, 0, rows, :].astype(BF16)
        g = p_ref[3, 0, rows, :]
        kw, vw = kp_ref[win, :], vp_ref[win, :]
        p = _softmax_rows(_band_scores(_dot_nt(q, kw), b_ref[0], r0))
        do, dg = _gate_bwd(dgt_ref[0], o_ref[0], g)
        do = do.astype(BF16)
        dv_ref[win, :] += _dot_tn(p.astype(BF16), do)
        dpr = _dot_nt(do, vw)
        ds = p * (dpr - jnp.sum(p * dpr, axis=-1, keepdims=True))
        db_ref[0] += ds
        ds = (ds * (HEAD_DIM ** -0.5)).astype(BF16)
        dk_ref[win, :] += _dot_tn(ds, q)
        dp_ref[0, 0, rows, :] = _dot_nn(ds, kw).astype(BF16)
        dp_ref[3, 0, rows, :] = dg.astype(BF16)

        @pl.when(t == nt - 1)
        def _():
            dp_ref[1, 0] = dk_ref[A_PAD:, :].astype(BF16)
            dp_ref[2, 0] = dv_ref[A_PAD:, :].astype(BF16)

    seq, tile = _head_specs(S, "hbt")
    bias_spec = pl.BlockSpec((1, TQ, A_KW), lambda h, b, t: (h, 0, 0))
    return _call(
        body, name=name, grid=(HEADS, B, nt), in_specs=[seq, bias_spec, tile, tile], out_specs=[seq, bias_spec],
        out_shape=[jax.ShapeDtypeStruct(proj.shape, BF16), jax.ShapeDtypeStruct(bias.shape, F32)],
        scratch_shapes=[pltpu.VMEM((A_PAD + S, HEAD_DIM), BF16), pltpu.VMEM((A_PAD + S, HEAD_DIM), BF16),
                        pltpu.VMEM((A_PAD + S, HEAD_DIM), F32), pltpu.VMEM((A_PAD + S, HEAD_DIM), F32)],
        semantics=("arbitrary", "arbitrary", "arbitrary"), args=[proj, bias, o, dgated], ride=ride)


def band_bias(rel_bias):
    length = TQ + A_KW - 1
    first = REL_CLIP + 1 - TQ
    gen = jnp.concatenate([rel_bias[:, first:],
                           jnp.broadcast_to(rel_bias[:, 2 * REL_CLIP:], (HEADS, length - (N_REL - first)))], axis=1)
    rev = jnp.concatenate([gen[:, ::-1], jnp.zeros((HEADS, 1), rel_bias.dtype)], axis=1)
    sheared = jnp.tile(rev, (1, TQ))[:, :TQ * length].reshape(HEADS, TQ, length)
    i = lax.broadcasted_iota(jnp.int32, (TQ, A_KW), 0)
    j = lax.broadcasted_iota(jnp.int32, (TQ, A_KW), 1)
    first_key = (i // CHUNK) * CHUNK
    in_band = (j >= first_key) & (j < first_key + (LEFT_CHUNKS + 1) * CHUNK)
    return jnp.where(in_band, sheared[:, :, TQ - 1:], NEG_INF)


def _group_scan(a, u, carry, reverse=False):
    row = lax.broadcasted_iota(jnp.int32, u.shape, 0)
    for k in (1, 2, 4):
        shift = 8 - k if reverse else k
        valid = (row < 8 - k) if reverse else (row >= k)
        u_sh = pltpu.roll(u, shift, 0)
        if a is None:
            u = jnp.where(valid, u + u_sh, u)
        else:
            a_sh = pltpu.roll(a, shift, 0)
            u = jnp.where(valid, a * u_sh + u, u)
            a = jnp.where(valid, a * a_sh, a)
    return (u + carry) if a is None else (a * carry + u)


def _scan_rows(n_rows, step, carry0, reverse=False):
    groups = n_rows // 8

    def loop(i, carry):
        gi = (groups - 1 - i) if reverse else i
        return step(pl.multiple_of(gi * 8, 8), carry)

    return lax.fori_loop(0, groups, loop, carry0)


def fox_cum_fwd(f_logit, f_bias, name):
    B, S, L = f_logit.shape

    def body(f_ref, b_ref, c_ref):
        z = f_ref[0] + b_ref[...]
        c_ref[0] = jnp.minimum(z, 0.0) - _log1p(jnp.exp(-jnp.abs(z)))

        def step(r0, carry):
            h = _group_scan(None, c_ref[0, pl.ds(r0, 8), :], carry)
            c_ref[0, pl.ds(r0, 8), :] = h
            return h[7:8, :]

        _scan_rows(S, step, jnp.zeros((1, L), F32))

    return pl.pallas_call(
        body, name=name, grid=(B,),
        in_specs=[pl.BlockSpec((1, S, L), lambda b: (b, 0, 0)), pl.BlockSpec((1, L), lambda b: (0, 0))],
        out_specs=pl.BlockSpec((1, S, L), lambda b: (b, 0, 0)),
        out_shape=jax.ShapeDtypeStruct((B, S, L), F32), compiler_params=_params(("parallel",)),
    )(f_logit, f_bias)


def fox_cum_bwd(f_logit, f_bias, dcum, name):
    B, S, L = f_logit.shape

    def body(f_ref, b_ref, d_ref, df_ref, db_ref):
        @pl.when(pl.program_id(0) == 0)
        def _():
            db_ref[...] = jnp.zeros_like(db_ref)

        def step(r0, carry):
            h = _group_scan(None, d_ref[0, pl.ds(r0, 8), :], carry, reverse=True)
            df_ref[0, pl.ds(r0, 8), :] = h
            return h[0:1, :]

        _scan_rows(S, step, jnp.zeros((1, L), F32), reverse=True)
        df = df_ref[0] * _sigmoid(-(f_ref[0] + b_ref[...]))
        df_ref[0] = df
        db_ref[...] += jnp.sum(df, axis=0, keepdims=True)

    seq = pl.BlockSpec((1, S, L), lambda b: (b, 0, 0))
    vec = pl.BlockSpec((1, L), lambda b: (0, 0))
    return pl.pallas_call(
        body, name=name, grid=(B,), in_specs=[seq, vec, seq], out_specs=[seq, vec],
        out_shape=[jax.ShapeDtypeStruct((B, S, L), F32), jax.ShapeDtypeStruct((1, L), F32)],
        compiler_params=_params(("arbitrary",)),
    )(f_logit, f_bias, dcum)


def _head_row(cr, h):
    sub = lax.broadcasted_iota(jnp.int32, cr.shape, 0)
    return jnp.sum(jnp.where(sub == h, cr, 0.0), axis=0, keepdims=True)


def _fox_scores(qk, cc, ck, h, r0):
    lane = lax.broadcasted_iota(jnp.int32, cc.shape, 1)
    cq = jnp.sum(jnp.where(lane == h, cc, 0.0), axis=1, keepdims=True)
    s = qk * (HEAD_DIM ** -0.5) + (cq - ck)
    qpos = r0 + lax.broadcasted_iota(jnp.int32, s.shape, 0)
    kpos = lax.broadcasted_iota(jnp.int32, s.shape, 1)
    return jnp.where(kpos <= qpos, s, NEG_INF)


KEY_STEP = 512


def _by_causal_width(t, S, fn):
    per = KEY_STEP // TQ
    for c in range(S // KEY_STEP):
        pl.when(t // per == c)(functools.partial(fn, (c + 1) * KEY_STEP))


def fox_fwd(proj, cum_col, cum_row, name, ride=None):
    _, B, S, W = proj.shape
    nt = S // TQ

    def body(p_ref, cc_ref, cr_ref, o_ref, gt_ref, k_ref, v_ref):
        h, t = pl.program_id(1), pl.program_id(2)

        @pl.when(t == 0)
        def _():
            k_ref[...] = p_ref[1, 0].astype(BF16)
            v_ref[...] = p_ref[2, 0].astype(BF16)

        r0 = pl.multiple_of(t * TQ, TQ)
        q = p_ref[0, 0, pl.ds(r0, TQ), :].astype(BF16)
        g = p_ref[3, 0, pl.ds(r0, TQ), :]

        def tile_out(width):
            ck = _head_row(cr_ref[0, :, 0:width], h)
            p = _softmax_rows(_fox_scores(_dot_nt(q, k_ref[0:width, :]), cc_ref[0], ck, h, r0))
            o = _dot_nn(p.astype(BF16), v_ref[0:width, :])
            o_ref[0] = o
            gt_ref[0] = (o * _silu(g)).astype(BF16)

        _by_causal_width(t, S, tile_out)

    seq, tile = _head_specs(S, "bht")
    return _call(
        body, name=name, grid=(B, HEADS, nt),
        in_specs=[seq, pl.BlockSpec((1, TQ, cum_col.shape[2]), lambda b, h, t: (b, t, 0)),
                  pl.BlockSpec((1, HEADS, S), lambda b, h, t: (b, 0, 0))],
        out_specs=[tile, tile],
        out_shape=[jax.ShapeDtypeStruct((B, S, W), F32), jax.ShapeDtypeStruct((B, S, W), BF16)],
        scratch_shapes=[pltpu.VMEM((S, HEAD_DIM), BF16), pltpu.VMEM((S, HEAD_DIM), BF16)],
        semantics=("parallel", "parallel", "arbitrary"), args=[proj, cum_col, cum_row], ride=ride)


def fox_bwd(proj, cum_col, cum_row, o, dgated, name, ride=None):
    _, B, S, W = proj.shape
    nt = S // TQ

    def body(p_ref, cc_ref, cr_ref, o_ref, dgt_ref, dp_ref, dc_ref, k_ref, v_ref, dk_ref, dv_ref):
        h, t = pl.program_id(1), pl.program_id(2)

        @pl.when(t == 0)
        def _():
            k_ref[...] = p_ref[1, 0].astype(BF16)
            v_ref[...] = p_ref[2, 0].astype(BF16)
            dk_ref[...] = jnp.zeros_like(dk_ref)
            dv_ref[...] = jnp.zeros_like(dv_ref)
            dc_ref[...] = jnp.zeros_like(dc_ref)

        r0 = pl.multiple_of(t * TQ, TQ)
        rows = pl.ds(r0, TQ)
        q = p_ref[0, 0, rows, :].astype(BF16)
        g = p_ref[3, 0, rows, :]
        do, dg = _gate_bwd(dgt_ref[0], o_ref[0], g)
        do = do.astype(BF16)
        dp_ref[3, 0, rows, :] = dg.astype(BF16)

        def tile_grads(width):
            k, v = k_ref[0:width, :], v_ref[0:width, :]
            ck = _head_row(cr_ref[0, :, 0:width], h)
            p = _softmax_rows(_fox_scores(_dot_nt(q, k), cc_ref[0], ck, h, r0))
            dv_ref[0:width, :] += _dot_tn(p.astype(BF16), do)
            dpr = _dot_nt(do, v)
            ds = p * (dpr - jnp.sum(p * dpr, axis=-1, keepdims=True))
            dc_ref[0, 0, :, 0:width] += jnp.sum(ds, axis=0, keepdims=True)
            ds = (ds * (HEAD_DIM ** -0.5)).astype(BF16)
            dk_ref[0:width, :] += _dot_tn(ds, q)
            dp_ref[0, 0, rows, :] = _dot_nn(ds, k).astype(BF16)

        _by_causal_width(t, S, tile_grads)

        @pl.when(t == nt - 1)
        def _():
            dp_ref[1, 0] = dk_ref[...].astype(BF16)
            dp_ref[2, 0] = dv_ref[...].astype(BF16)

    seq, tile = _head_specs(S, "bht")
    return _call(
        body, name=name, grid=(B, HEADS, nt),
        in_specs=[seq, pl.BlockSpec((1, TQ, cum_col.shape[2]), lambda b, h, t: (b, t, 0)),
                  pl.BlockSpec((1, HEADS, S), lambda b, h, t: (b, 0, 0)), tile, tile],
        out_specs=[seq, pl.BlockSpec((1, 1, 1, S), lambda b, h, t: (b, h, 0, 0))],
        out_shape=[jax.ShapeDtypeStruct(proj.shape, BF16), jax.ShapeDtypeStruct((B, HEADS, 1, S), F32)],
        scratch_shapes=[pltpu.VMEM((S, HEAD_DIM), BF16), pltpu.VMEM((S, HEAD_DIM), BF16),
                        pltpu.VMEM((S, HEAD_DIM), F32), pltpu.VMEM((S, HEAD_DIM), F32)],
        semantics=("parallel", "parallel", "arbitrary"), args=[proj, cum_col, cum_row, o, dgated], ride=ride)


RG_ROWS = 512


def _rg_gates(xc, wa_ref, ba_ref, wx_ref, bx_ref, lam_ref):
    xcb = xc.astype(BF16)
    r = _sigmoid(_dot_nn(xcb, wa_ref[0]) + ba_ref[...])
    i = _sigmoid(_dot_nn(xcb, wx_ref[0]) + bx_ref[...])
    sp = _softplus(-lam_ref[...])
    log_a = (-RG_C * sp) * r
    a = jnp.exp(log_a)
    m = jnp.sqrt(-jnp.tanh(log_a) * (a * a + 1.0))
    return xcb, r, i, sp, a, m


def _rg_specs(B, S, rows, order):
    nc = S // rows

    def idx(fn):
        def index_map(*ids):
            v = dict(zip(order.lower(), ids))
            c = (nc - 1 - v["c"]) if "C" in order else v["c"]
            return fn(v["b"], v["d"], c)
        return index_map

    return dict(
        proj=pl.BlockSpec((2, 1, rows, RG_COLS), idx(lambda b, d, c: (0, b, c, d))),
        act=pl.BlockSpec((1, rows, RG_COLS), idx(lambda b, d, c: (b, c, d))),
        taps=pl.BlockSpec((CONV_WIDTH, RG_COLS), idx(lambda b, d, c: (0, d))),
        vec=pl.BlockSpec((1, RG_COLS), idx(lambda b, d, c: (0, d))),
        gate=pl.BlockSpec((1, RG_COLS, RG_COLS), idx(lambda b, d, c: (d, 0, 0))),
    )


def rglru_fwd(proj, conv_w, conv_b, wa, ba, wx, bx, lam, name, rows=RG_ROWS, ride=None):
    _, B, S, _ = proj.shape
    rows = min(rows, S)
    sp_ = _rg_specs(B, S, rows, "bdc")

    def body(p_ref, cw_ref, cb_ref, wa_ref, ba_ref, wx_ref, bx_ref, lam_ref,
             xc_ref, hs_ref, hp_ref, gt_ref, ext_ref, a_ref, u_ref, xcar_ref, hcar_ref):
        @pl.when(pl.program_id(2) == 0)
        def _():
            xcar_ref[...] = jnp.zeros_like(xcar_ref)
            hcar_ref[...] = jnp.zeros_like(hcar_ref)

        xr = p_ref[0, 0]
        ext_ref[0:8, :] = xcar_ref[...]
        ext_ref[8:, :] = xr
        xcar_ref[...] = xr[rows - 8:, :]
        xc = ext_ref[pl.ds(5, rows), :] * cw_ref[0:1, :]
        xc = xc + ext_ref[pl.ds(6, rows), :] * cw_ref[1:2, :]
        xc = xc + ext_ref[pl.ds(7, rows), :] * cw_ref[2:3, :]
        xc = xc + xr * cw_ref[3:4, :] + cb_ref[...]
        xc_ref[0] = xc
        _, _, i, _, a, m = _rg_gates(xc, wa_ref, ba_ref, wx_ref, bx_ref, lam_ref)
        a_ref[...] = a
        u_ref[...] = m * (i * xc)

        def step(r0, carry):
            h = _group_scan(a_ref[pl.ds(r0, 8), :], u_ref[pl.ds(r0, 8), :], carry)
            row = lax.broadcasted_iota(jnp.int32, h.shape, 0)
            hs_ref[0, pl.ds(r0, 8), :] = h
            hp_ref[0, pl.ds(r0, 8), :] = jnp.where(row == 0, carry, pltpu.roll(h, 1, 0))
            return h[7:8, :]

        hcar_ref[0:1, :] = _scan_rows(rows, step, hcar_ref[0:1, :])
        gt_ref[0] = (hs_ref[0] * _silu(p_ref[1, 0])).astype(BF16)

    act = jax.ShapeDtypeStruct((B, S, RG_WIDTH), F32)
    return _call(
        body, name=name, grid=(B, RG_GROUPS, S // rows),
        in_specs=[sp_["proj"], sp_["taps"], sp_["vec"], sp_["gate"], sp_["vec"], sp_["gate"], sp_["vec"], sp_["vec"]],
        out_specs=[sp_["act"]] * 4,
        out_shape=[act, act, act, jax.ShapeDtypeStruct((B, S, RG_WIDTH), BF16)],
        scratch_shapes=[pltpu.VMEM((rows + 8, RG_COLS), F32), pltpu.VMEM((rows, RG_COLS), F32),
                        pltpu.VMEM((rows, RG_COLS), F32), pltpu.VMEM((8, RG_COLS), F32), pltpu.VMEM((8, RG_COLS), F32)],
        semantics=("parallel", "parallel", "arbitrary"), args=[proj, conv_w, conv_b, wa, ba, wx, bx, lam], ride=ride)


def rglru_bwd(proj, xc, hs, hprev, dgated, conv_w, wa, ba, wx, bx, lam, name, rows=RG_ROWS, ride=None):
    _, B, S, _ = proj.shape
    rows = min(rows, S)
    sp_ = _rg_specs(B, S, rows, "dbC")

    def body(p_ref, xc_ref, hs_ref, hp_ref, dgt_ref, cw_ref, wa_ref, ba_ref, wx_ref, bx_ref, lam_ref,
             dp_ref, dcw_ref, dcb_ref, dwa_ref, dba_ref, dwx_ref, dbx_ref, dlam_ref,
             ext_ref, c_ref, l_ref, acar_ref, lcar_ref, dcar_ref):
        b_, c_ = pl.program_id(1), pl.program_id(2)

        @pl.when(c_ == 0)
        def _():
            acar_ref[...] = jnp.zeros_like(acar_ref)
            lcar_ref[...] = jnp.zeros_like(lcar_ref)
            dcar_ref[...] = jnp.zeros_like(dcar_ref)

        @pl.when((c_ == 0) & (b_ == 0))
        def _():
            for ref in (dcw_ref, dcb_ref, dwa_ref, dba_ref, dwx_ref, dbx_ref, dlam_ref):
                ref[...] = jnp.zeros_like(ref)

        xr, g = p_ref[0, 0], p_ref[1, 0]
        xc_v = xc_ref[0]
        xcb, r, i, sp, a, m = _rg_gates(xc_v, wa_ref, ba_ref, wx_ref, bx_ref, lam_ref)
        dhs, dg = _gate_bwd(dgt_ref[0], hs_ref[0], g)
        dp_ref[1, 0] = dg.astype(BF16)

        ext_ref[0:rows, :] = a
        ext_ref[rows:, :] = acar_ref[...]
        acar_ref[...] = a[0:8, :]
        c_ref[...] = ext_ref[pl.ds(1, rows), :]
        l_ref[...] = dhs

        def step(r0, carry):
            lam_g = _group_scan(c_ref[pl.ds(r0, 8), :], l_ref[pl.ds(r0, 8), :], carry, reverse=True)
            l_ref[pl.ds(r0, 8), :] = lam_g
            return lam_g[0:1, :]

        lcar_ref[0:1, :] = _scan_rows(rows, step, lcar_ref[0:1, :], reverse=True)
        du = l_ref[...]
        da = du * hp_ref[0]
        dlog_a = da * a - (du * (i * xc_v)) * (a * a / m)
        dr = dlog_a * (-RG_C * sp)
        dsp = jnp.sum(dlog_a * (-RG_C * r), axis=0, keepdims=True)
        dlam_ref[...] += dsp * (-_sigmoid(-lam_ref[...]))
        dpa = dr * (r * (1.0 - r))
        dpx = (du * (m * xc_v)) * (i * (1.0 - i))
        dba_ref[...] += jnp.sum(dpa, axis=0, keepdims=True)
        dbx_ref[...] += jnp.sum(dpx, axis=0, keepdims=True)
        dpa, dpx = dpa.astype(BF16), dpx.astype(BF16)
        dwa_ref[0] += _dot_tn(xcb, dpa)
        dwx_ref[0] += _dot_tn(xcb, dpx)
        dxc = du * (m * i) + _dot_nt(dpa, wa_ref[0]) + _dot_nt(dpx, wx_ref[0])

        dcb_ref[...] += jnp.sum(dxc, axis=0, keepdims=True)
        ext_ref[0:rows, :] = dxc
        ext_ref[rows:, :] = dcar_ref[...]
        dcar_ref[...] = dxc[0:8, :]
        dxr = jnp.zeros_like(dxc)
        for k in range(CONV_WIDTH):
            tap = CONV_WIDTH - 1 - k
            ahead = dxc if k == 0 else ext_ref[pl.ds(k, rows), :]
            dxr = dxr + ahead * cw_ref[tap:tap + 1, :]
            dcw_ref[tap:tap + 1, :] += jnp.sum(xr * ahead, axis=0, keepdims=True)
        dp_ref[0, 0] = dxr.astype(BF16)

    vec = jax.ShapeDtypeStruct((1, RG_WIDTH), F32)
    gate = jax.ShapeDtypeStruct((RG_GROUPS, RG_COLS, RG_COLS), F32)
    return _call(
        body, name=name, grid=(RG_GROUPS, B, S // rows),
        in_specs=[sp_["proj"], sp_["act"], sp_["act"], sp_["act"], sp_["act"], sp_["taps"],
                  sp_["gate"], sp_["vec"], sp_["gate"], sp_["vec"], sp_["vec"]],
        out_specs=[sp_["proj"], sp_["taps"], sp_["vec"], sp_["gate"], sp_["vec"], sp_["gate"], sp_["vec"], sp_["vec"]],
        out_shape=[jax.ShapeDtypeStruct(proj.shape, BF16), jax.ShapeDtypeStruct((CONV_WIDTH, RG_WIDTH), F32), vec,
                   gate, vec, gate, vec, vec],
        scratch_shapes=[pltpu.VMEM((rows + 8, RG_COLS), F32), pltpu.VMEM((rows, RG_COLS), F32),
                        pltpu.VMEM((rows, RG_COLS), F32), pltpu.VMEM((8, RG_COLS), F32),
                        pltpu.VMEM((8, RG_COLS), F32), pltpu.VMEM((8, RG_COLS), F32)],
        semantics=("arbitrary", "arbitrary", "arbitrary"),
        args=[proj, xc, hs, hprev, dgated, conv_w, wa, ba, wx, bx, lam], ride=ride)


def block_diag_gates(w):
    per = RG_COLS // RG_BLOCK
    w4 = w.reshape(RG_GROUPS, per, RG_BLOCK, RG_BLOCK)
    return jnp.einsum("dipq,ij->dipjq", w4, jnp.eye(per, dtype=w.dtype)).reshape(RG_GROUPS, RG_COLS, RG_COLS)


def block_diag_gates_t(dw):
    per = RG_COLS // RG_BLOCK
    dw6 = dw.reshape(RG_GROUPS, per, RG_BLOCK, per, RG_BLOCK)
    return jnp.stack([dw6[:, i, :, i, :] for i in range(per)], axis=1).reshape(RG_BLOCKS, RG_BLOCK, RG_BLOCK)


def adamw(w, parts, m, v, name, layer=0, prev=None, part_row0=0, row_tile=ROW_TILE):
    L, R, C = w.shape
    n_parts = parts.shape[0]
    br = row_tile if R % row_tile == 0 else R

    def body(w_ref, p_ref, m_ref, v_ref, *refs):
        g_ref, d_ref, nm_ref, nv_ref = refs[-4:]
        g = p_ref[0].astype(F32)
        for k in range(1, n_parts):
            g = g + p_ref[k].astype(F32)
        nm = ADAM_B1 * m_ref[0] + (1.0 - ADAM_B1) * g
        nv = ADAM_B2 * v_ref[0] + (1.0 - ADAM_B2) * (g * g)
        m_hat = nm / (1.0 - ADAM_B1 ** ADAM_STEP)
        v_hat = nv / (1.0 - ADAM_B2 ** ADAM_STEP)
        g_ref[0] = g
        d_ref[0] = -ADAM_LR * (m_hat / (jnp.sqrt(v_hat) + ADAM_EPS) + ADAM_WD * w_ref[0])
        nm_ref[0] = nm
        nv_ref[0] = nv

    slab = pl.BlockSpec((1, br, C), lambda i: (layer, i, 0))
    out = jax.ShapeDtypeStruct((L, R, C), F32)
    carried = [] if prev is None else list(prev)
    return _call(
        body, name=name, grid=(R // br,),
        in_specs=[slab, pl.BlockSpec((n_parts, br, C), lambda i: (0, part_row0 // br + i, 0)), slab, slab]
        + [pl.BlockSpec(memory_space=pl.ANY)] * len(carried),
        out_specs=[slab] * 4, out_shape=[out] * 4, semantics=("parallel",), args=[w, parts, m, v] + carried,
        aliases={4 + k: k for k in range(len(carried))})


def _seq(a, B):
    return a.reshape(a.shape[:-2] + (B, a.shape[-2] // B, a.shape[-1]))


def _flat(a):
    return a.reshape(a.shape[:-3] + (a.shape[-3] * a.shape[-2], a.shape[-1]))


def _tiles(w, which, **default):
    return dict(default, **w.get("tiles", {}).get(which, {}))


def mixer_a_fwd(h, w, B, tag, rides):
    proj = matmul(h, w["w_in"], mode="nn", out_dtype=F32, name=f"{tag}_proj", out_slabs=4,
                  ride=rides.pop(f"{tag}_proj", None), **_tiles(w, "proj"))
    o, gated = attn_a_fwd(_seq(proj, B), w["bias"], f"{tag}_attn", ride=rides.pop(f"{tag}_attn", None))
    return _flat(gated), dict(proj=proj, o=o)


def mixer_a_bwd(dgated, w, saved, B, tag, rides):
    dproj, dbias = attn_a_bwd(_seq(saved["proj"], B), w["bias"], saved["o"], _seq(dgated, B), f"{tag}_attn_bwd",
                              ride=rides.pop(f"{tag}_attn_bwd", None))
    return _flat(dproj), dict(bias=dbias)


def mixer_b_fwd(h, w, B, tag, rides):
    proj = matmul(h, w["w_in"], mode="nn", out_dtype=F32, name=f"{tag}_proj", out_slabs=2, bn=RG_COLS,
                  ride=rides.pop(f"{tag}_proj", None))
    xc, hs, hprev, gated = rglru_fwd(_seq(proj, B), w["conv_w"], w["conv_b"], w["wa"], w["ba"], w["wx"], w["bx"],
                                     w["lam"], f"{tag}_rglru", ride=rides.pop(f"{tag}_rglru", None))
    return _flat(gated), dict(proj=proj, xc=xc, hs=hs, hprev=hprev)


def mixer_b_bwd(dgated, w, saved, B, tag, rides):
    dproj, dcw, dcb, dwa, dba, dwx, dbx, dlam = rglru_bwd(
        _seq(saved["proj"], B), saved["xc"], saved["hs"], saved["hprev"], _seq(dgated, B),
        w["conv_w"], w["wa"], w["ba"], w["wx"], w["bx"], w["lam"], f"{tag}_rglru_bwd",
        ride=rides.pop(f"{tag}_rglru_bwd", None))
    return _flat(dproj), dict(conv_w=dcw, conv_b=dcb, wa=dwa, ba=dba, wx=dwx, bx=dbx, lam=dlam)


def mixer_c_fwd(h, w, B, tag, rides):
    proj = matmul(h, w["w_in"], mode="nn", out_dtype=F32, name=f"{tag}_proj", out_slabs=4,
                  ride=rides.pop(f"{tag}_proj", None), **_tiles(w, "proj"))
    f_logit = matmul(h, w["w_f"], mode="nn", out_dtype=F32, name=f"{tag}_fproj")
    cum = fox_cum_fwd(_seq(f_logit, B), w["f_bias"], f"{tag}_cum")
    cum_row = cum[:, :, :HEADS].transpose(0, 2, 1)
    o, gated = fox_fwd(_seq(proj, B), cum, cum_row, f"{tag}_attn", ride=rides.pop(f"{tag}_attn", None))
    return _flat(gated), dict(proj=proj, o=o, f_logit=f_logit, cum=cum, cum_row=cum_row)


def mixer_c_bwd(dgated, w, saved, B, tag, rides):
    dproj, dck = fox_bwd(_seq(saved["proj"], B), saved["cum"], saved["cum_row"], saved["o"], _seq(dgated, B),
                         f"{tag}_attn_bwd", ride=rides.pop(f"{tag}_attn_bwd", None))
    S = dck.shape[-1]
    dcum = jnp.pad(-dck.reshape(B, HEADS, S).transpose(0, 2, 1), ((0, 0), (0, 0), (0, HEAD_DIM - HEADS)))
    df, dfb = fox_cum_bwd(_seq(saved["f_logit"], B), w["f_bias"], dcum, f"{tag}_cum_bwd")
    return _flat(dproj), dict(f_bias=dfb, df=_flat(df).astype(BF16))


MIXERS = {"a": (mixer_a_fwd, mixer_a_bwd), "b": (mixer_b_fwd, mixer_b_bwd), "c": (mixer_c_fwd, mixer_c_bwd)}
LAYER_KINDS = "abca"


def local_step(x, target, norm_pre, norm_post, get_layer, rides, on_grads):
    B, S, Dm = x.shape
    n_layers = len(LAYER_KINDS)
    xs = [x.reshape(B * S, Dm)]
    saved, layers = [], []
    h = prenorm_fwd(xs[0], norm_pre[0:1], "l0a_prenorm")
    for li, kind in enumerate(LAYER_KINDS):
        tag = f"l{li}{kind}"
        w = get_layer(li)
        gated, sv = MIXERS[kind][0](h, w, B, tag, rides)
        if callable(w["w_out"]):
            w["w_out"] = w["w_out"]()
        y = matmul(gated, w["w_out"], mode="nn", out_dtype=F32, name=f"{tag}_out", ride=rides.pop(f"{tag}_out", None))
        saved.append(dict(sv, h=h, gated=gated, y=y))
        layers.append(w)
        if li + 1 < n_layers:
            x_new, h = postnorm_prenorm_fwd(xs[-1], y, norm_post[li:li + 1], norm_pre[li + 1:li + 2],
                                            f"{tag}_postnorm")
            xs.append(x_new)
    loss, dx = postnorm_loss(xs[-1], y, norm_post[n_layers - 1:], target.reshape(B * S, Dm), "loss")

    for li in reversed(range(n_layers)):
        kind, w, sv = LAYER_KINDS[li], layers[li], saved[li]
        tag = f"l{li}{kind}"
        dy, dg_post = postnorm_bwd(sv["y"], norm_post[li:li + 1], dx, f"{tag}_postnorm_bwd")
        on_grads(li, "norm_post", dg_post)
        on_grads(li, "w_out", matmul(sv["gated"], dy, mode="tn", out_dtype=BF16, name=f"{tag}_dwout",
                                     ride=rides.pop(f"{tag}_dwout", None)))
        dgated = matmul(dy, w["w_out"], mode="nt", out_dtype=F32, name=f"{tag}_dgated",
                        ride=rides.pop(f"{tag}_dgated", None))
        dproj, gw = MIXERS[kind][1](dgated, w, sv, B, tag, rides)
        df = gw.pop("df", None)
        for name, value in gw.items():
            on_grads(li, name, value)
        parts = w.get("dwin_parts", [(0, 1, 1)])
        for i, m_part in enumerate(parts):
            suffix = f"_{i}" if len(parts) > 1 else ""
            on_grads(li, "w_in" + suffix,
                     matmul(sv["h"], dproj, mode="tn", out_dtype=BF16, name=f"{tag}_dwin{suffix}", m_part=m_part,
                            out_slabs=w["grad_slabs"], ride=rides.pop(f"{tag}_dwin{suffix}", None),
                            **_tiles(w, "dwin")))
        if df is not None:
            on_grads(li, "w_f", matmul(sv["h"], df, mode="tn", out_dtype=BF16, name=f"{tag}_dwf"))
        dhs = [matmul(dproj, w["w_in"], mode="nt", out_dtype=F32, name=f"{tag}_dh",
                      ride=rides.pop(f"{tag}_dh", None), **_tiles(w, "dh"))]
        if df is not None:
            dhs.append(matmul(df, w["w_f"], mode="nt", out_dtype=F32, name=f"{tag}_dhf"))
        dx, dg_pre = prenorm_bwd(xs[li], norm_pre[li:li + 1], dhs, dx, f"{tag}_prenorm_bwd")
        on_grads(li, "norm_pre", dg_pre)
    assert not rides, list(rides)
    return loss, dx.reshape(B, S, Dm)


WEIGHTS = ("norm_pre", "norm_post", "a_w_in", "a_rel_bias", "a_w_out", "b_w_in", "b_conv_w", "b_conv_b",
           "b_gate_a_w", "b_gate_a_b", "b_gate_x_w", "b_gate_x_b", "b_lambda", "b_w_out", "c_w_in", "c_f_bias",
           "c_w_out")
C_SHARD = (4 * D_MODEL + HEADS) // N_DEV


def _rows(gathered):
    return gathered.reshape(gathered.shape[0] * gathered.shape[1], gathered.shape[2])


def layer_a(w_in, w_out, rel_bias):
    return dict(w_in=w_in, w_out=w_out if callable(w_out) else _rows(w_out), bias=band_bias(rel_bias),
                grad_slabs=N_DEV)


def layer_b(w_in, w_out, conv_w, small):
    return dict(
        w_in=w_in, w_out=_rows(w_out), grad_slabs=N_DEV,
        conv_w=conv_w.transpose(1, 0, 2).reshape(CONV_WIDTH, RG_WIDTH),
        conv_b=small["b_conv_b"], lam=small["b_lambda"],
        wa=block_diag_gates(small["b_gate_a_w"][0]).astype(BF16), ba=small["b_gate_a_b"].reshape(1, RG_WIDTH),
        wx=block_diag_gates(small["b_gate_x_w"][0]).astype(BF16), bx=small["b_gate_x_b"].reshape(1, RG_WIDTH))


def layer_c(w_in, w_out, small):
    full = w_in.transpose(1, 0, 2).reshape(D_MODEL, N_DEV * C_SHARD)
    return dict(w_in=full[:, :4 * D_MODEL], w_f=jnp.pad(full[:, 4 * D_MODEL:], ((0, 0), (0, HEAD_DIM - HEADS))),
                w_out=_rows(w_out), grad_slabs=1,
                f_bias=jnp.pad(small["c_f_bias"], ((0, 0), (0, HEAD_DIM - HEADS))))


def c_w_in_blocks(dmain, df):
    full = jnp.concatenate([dmain, df[:, :HEADS].astype(dmain.dtype)], axis=1)
    return full.reshape(D_MODEL, N_DEV, C_SHARD).transpose(1, 0, 2)


def _row_blocks(g):
    return g.reshape(N_DEV, g.shape[0] // N_DEV, g.shape[1])


PACK_LANES = 128
PACK_ALIGN = 8 * PACK_LANES


def pack(parts):
    flat = []
    for p in parts:
        n = p.size
        flat.append(jnp.pad(p.reshape(n), (0, -n % PACK_ALIGN)).reshape(-1, PACK_LANES))
    rows = sum(f.shape[0] for f in flat)
    flat.append(jnp.zeros((-rows % ROW_TILE, PACK_LANES), F32))
    return jnp.concatenate(flat, axis=0)


def unpack(packed, shapes):
    out, row = [], 0
    for shape in shapes:
        n = 1
        for s in shape:
            n *= s
        n_rows = (n + PACK_ALIGN - 1) // PACK_ALIGN * 8
        out.append(packed[row:row + n_rows].reshape(-1)[:n].reshape(shape))
        row += n_rows
    return out


LATE = (("a_rel_bias", slice(0, 1)), ("norm_pre", slice(0, 2)), ("norm_post", slice(0, 1)))
EARLY = (("a_rel_bias", slice(1, 2)), ("norm_pre", slice(2, 4)), ("norm_post", slice(1, 4)),
         ("b_conv_b", slice(None)), ("b_gate_a_w", slice(None)), ("b_gate_a_b", slice(None)),
         ("b_gate_x_w", slice(None)), ("b_gate_x_b", slice(None)), ("b_lambda", slice(None)),
         ("c_f_bias", slice(None)))


def _pieces(tree, pieces):
    return [tree[name][sl] for name, sl in pieces]


def kernel(x, norm_pre, norm_post, a_w_in, a_rel_bias, a_w_out, b_w_in, b_conv_w, b_conv_b, b_gate_a_w, b_gate_a_b, b_gate_x_w, b_gate_x_b, b_lambda, b_w_out, c_w_in, c_f_bias, c_w_out, loss_target, m_norm_pre, m_norm_post, m_a_w_in, m_a_rel_bias, m_a_w_out, m_b_w_in, m_b_conv_w, m_b_conv_b, m_b_gate_a_w, m_b_gate_a_b, m_b_gate_x_w, m_b_gate_x_b, m_b_lambda, m_b_w_out, m_c_w_in, m_c_f_bias, m_c_w_out, v_norm_pre, v_norm_post, v_a_w_in, v_a_rel_bias, v_a_w_out, v_b_w_in, v_b_conv_w, v_b_conv_b, v_b_gate_a_w, v_b_gate_a_b, v_b_gate_x_w, v_b_gate_x_b, v_b_lambda, v_b_w_out, v_c_w_in, v_c_f_bias, v_c_w_out):
    args = dict(locals())
    w = {n: args[n] for n in WEIGHTS}
    m = {n: args["m_" + n] for n in WEIGHTS}
    v = {n: args["v_" + n] for n in WEIGHTS}

    a_in, a_out = a_w_in.astype(BF16), a_w_out.astype(BF16)
    gather_a0 = Ride([a_in[0]], scatter=False, via_sibling=True)
    in_l0_proj = Ride([b_w_in[0].astype(BF16), b_conv_w[0], a_out[0]], scatter=False, via_sibling=True)
    in_l0_attn = Ride([c_w_in[0].astype(BF16), b_w_out[0].astype(BF16)], scatter=False, via_sibling=True)
    in_l1_proj = Ride([c_w_out[0].astype(BF16)], scatter=False)
    in_l2_proj = Ride([a_out[1]], scatter=False)
    in_l2_attn = Ride([a_in[1]], scatter=False, via_sibling=True)
    exchange(gather_a0, "gather_l0")
    rides = {"l0a_proj": in_l0_proj, "l0a_attn": in_l0_attn, "l1b_proj": in_l1_proj, "l2c_proj": in_l2_proj,
             "l2c_attn": in_l2_attn}

    def get_layer(li):
        if li == 0:
            return dict(layer_a(gather_a0.out[0], lambda: _rows(in_l0_proj.out[2]), a_rel_bias[0]),
                        dwin_parts=[(0, 1, 4), (1, 1, 4), (2, 2, 4)])
        if li == 1:
            return layer_b(in_l0_proj.out[0], in_l0_attn.out[1], in_l0_proj.out[1], w)
        if li == 2:
            return dict(layer_c(in_l0_attn.out[0], in_l1_proj.out[0], w), tiles=dict(proj=dict(bn=2048)))
        return layer_a(in_l2_attn.out[0], in_l2_proj.out[0], a_rel_bias[1])

    grads = [dict() for _ in LAYER_KINDS]
    scatters = {}

    def rel_bias_grad(j, dbias):
        return jax.vjp(band_bias, a_rel_bias[j])[1](dbias)[0][None]

    def early_partial():
        gb, gc = grads[1], grads[2]
        tree = dict(
            a_rel_bias=jnp.concatenate([jnp.zeros((1, HEADS, N_REL), F32), rel_bias_grad(1, grads[3]["bias"])]),
            norm_pre=jnp.concatenate([jnp.zeros((2, D_MODEL), F32)] + [grads[li]["norm_pre"] for li in (2, 3)]),
            norm_post=jnp.concatenate([jnp.zeros((1, D_MODEL), F32)] + [grads[li]["norm_post"] for li in (1, 2, 3)]),
            b_conv_b=gb["conv_b"], b_lambda=gb["lam"],
            b_gate_a_w=block_diag_gates_t(gb["wa"])[None], b_gate_a_b=gb["ba"].reshape(1, RG_BLOCKS, RG_BLOCK),
            b_gate_x_w=block_diag_gates_t(gb["wx"])[None], b_gate_x_b=gb["bx"].reshape(1, RG_BLOCKS, RG_BLOCK),
            c_f_bias=gc["f_bias"][:, :HEADS])
        return pack(_pieces(tree, EARLY))

    def send(key, host, blocks, scatter=True, via_sibling=False):
        ride = rides.setdefault(host, Ride([], scatter, via_sibling))
        assert (ride.scatter, ride.via_sibling) == (scatter, via_sibling)
        scatters[key] = (ride, len(ride.arrs))
        ride.arrs.append(blocks)

    def on_grads(li, name, value):
        g = grads[li]
        g[name] = value
        if (li, name) == (3, "w_out"):
            send("a1_out", "l3a_attn_bwd", _row_blocks(value))
        elif (li, name) == (3, "w_in"):
            send("a1_in", "l2c_attn_bwd", value)
        elif (li, name) == (2, "w_out"):
            send("c_out", "l2c_attn_bwd", _row_blocks(value))
        elif (li, name) == (2, "w_f"):
            blocks = c_w_in_blocks(g["w_in"], value)
            send("c_in_0", "l2c_dh", blocks[:, :D_MODEL // 2])
            send("c_in_1", "l1b_rglru_bwd", blocks[:, D_MODEL // 2:])
        elif (li, name) == (1, "w_out"):
            send("b_out", "l1b_dh", _row_blocks(value))
        elif (li, name) == (1, "w_in"):
            send("b_in", "l0a_attn_bwd", value)
            send("b_conv", "l0a_attn_bwd",
                 g["conv_w"].reshape(CONV_WIDTH, N_DEV, RG_WIDTH // N_DEV).transpose(1, 0, 2))
        elif (li, name) == (1, "lam"):
            send("early", "l1b_dwin", early_partial(), scatter=False, via_sibling=True)
        elif (li, name) == (0, "w_out"):
            send("a0_out", "l0a_attn_bwd", _row_blocks(value))
        elif (li, name) == (0, "w_in_0"):
            send("a0_in_0", "l0a_dwin_1", value)
        elif (li, name) == (0, "w_in_1"):
            send("a0_in_1", "l0a_dwin_2", value)
        elif (li, name) == (0, "w_in_2"):
            send("a0_in_2", "l0a_dh", value)

    loss, grad_x = local_step(x, loss_target, norm_pre, norm_post, get_layer, rides, on_grads)
    late_tree = dict(a_rel_bias=rel_bias_grad(0, grads[0]["bias"]), norm_post=grads[0]["norm_post"],
                     norm_pre=jnp.concatenate([grads[0]["norm_pre"], grads[1]["norm_pre"]]))
    late_parts = exchange(Ride([pack([late_tree[n] for n, _ in LATE])], scatter=False), "gather_late_grads")[0]

    def sharded(name, slab_parts):
        shape = w[name].shape
        slabs = (len(slab_parts), shape[0] * shape[1] // len(slab_parts), shape[2])
        outs = None
        for j, (parts, row0) in enumerate(slab_parts):
            outs = adamw(w[name].reshape(slabs), parts, m[name].reshape(slabs), v[name].reshape(slabs),
                         f"adamw_{name}_{j}", layer=j, prev=outs, part_row0=row0)
        return [o.reshape(shape) for o in outs]

    def received(key):
        ride, position = scatters[key]
        return ride.out[position]

    res = dict(
        a_w_in=sharded("a_w_in", [(received("a0_in_0"), 0), (received("a0_in_1"), 0), (received("a0_in_2"), 0),
                                  (received("a0_in_2"), D_MODEL // 4)]
                       + [(received("a1_in"), q * D_MODEL // 4) for q in range(4)]),
        a_w_out=sharded("a_w_out", [(received("a0_out"), 0), (received("a1_out"), 0)]),
        b_w_in=sharded("b_w_in", [(received("b_in"), 0)]),
        b_w_out=sharded("b_w_out", [(received("b_out"), 0)]),
        b_conv_w=sharded("b_conv_w", [(received("b_conv"), 0)]),
        c_w_in=sharded("c_w_in", [(received("c_in_0"), 0), (received("c_in_1"), 0)]),
        c_w_out=sharded("c_w_out", [(received("c_out"), 0)]))

    packed = {}
    for label, pieces, parts in (("early", EARLY, received("early")), ("late", LATE, late_parts)):
        outs = adamw(pack(_pieces(w, pieces))[None], parts, pack(_pieces(m, pieces))[None],
                     pack(_pieces(v, pieces))[None], f"adamw_replicated_{label}")
        shapes = [w[n][sl].shape for n, sl in pieces]
        packed[label] = [dict(zip([n for n, _ in pieces], unpack(o[0], shapes))) for o in outs]
    for n in ("b_conv_b", "b_gate_a_w", "b_gate_a_b", "b_gate_x_w", "b_gate_x_b", "b_lambda", "c_f_bias"):
        res[n] = [packed["early"][k][n] for k in range(4)]
    for n in ("a_rel_bias", "norm_pre", "norm_post"):
        res[n] = [jnp.concatenate([packed["late"][k][n], packed["early"][k][n]]) for k in range(4)]

    total = lax.psum(loss[0, 0], ("x", "y", "c"))
    return (total, grad_x, *[res[n][0] for n in WEIGHTS], *[res[n][1] for n in WEIGHTS],
            *[res[n][2] for n in WEIGHTS], *[res[n][3] for n in WEIGHTS])
```

```python
import functools

import jax
import jax.numpy as jnp
from jax import lax
from jax.experimental import pallas as pl
from jax.experimental.pallas import tpu as pltpu

F32 = jnp.float32
BF16 = jnp.bfloat16

N_DEV = 8
D_MODEL = 2048
HEADS = 16
HEAD_DIM = 128
CHUNK = 64
LEFT_CHUNKS = 8
REL_CLIP = 256
N_REL = 2 * REL_CLIP + 1
TQ = 256
A_PAD = LEFT_CHUNKS * CHUNK
A_KW = A_PAD + TQ
RG_WIDTH = 2560
RG_BLOCKS = 16
RG_BLOCK = 160
RG_COLS = 640
RG_GROUPS = RG_WIDTH // RG_COLS
RG_C = 8.0
CONV_WIDTH = 4
RMS_EPS = 1e-6
NEG_INF = -1e30
ADAM_LR = 0.001
ADAM_B1 = 0.9
ADAM_B2 = 0.999
ADAM_EPS = 1e-08
ADAM_WD = 0.01
ADAM_STEP = 10
VMEM_LIMIT = 56 * 1024 * 1024
MESH = pl.DeviceIdType.MESH


def _params(sem, vmem=VMEM_LIMIT):
    return pltpu.CompilerParams(dimension_semantics=sem, vmem_limit_bytes=vmem)


def _sigmoid(x):
    return 1.0 / (1.0 + jnp.exp(-x))


def _log1p(y):
    u = 1.0 + y
    return jnp.where(u == 1.0, y, jnp.log(u) * (y / jnp.where(u == 1.0, 1.0, u - 1.0)))


def _softplus(x):
    return jnp.maximum(x, 0.0) + _log1p(jnp.exp(-jnp.abs(x)))


def _dot(a, b, dims):
    return lax.dot_general(a, b, (dims, ((), ())), preferred_element_type=F32)


def _dot_nn(a, b):
    return _dot(a, b, ((1,), (0,)))


def _dot_nt(a, b):
    return _dot(a, b, ((1,), (1,)))


def _dot_tn(a, b):
    return _dot(a, b, ((0,), (0,)))


def _peers():
    x, y, c = lax.axis_index("x"), lax.axis_index("y"), lax.axis_index("c")
    me = 4 * x + 2 * y + c
    peers = []
    for k in range(1, N_DEV):
        px = 1 - x if k & 4 else x
        py = 1 - y if k & 2 else y
        pc = 1 - c if k & 1 else c
        peers.append(((px, py, pc), 4 * px + 2 * py + pc))
    return me, peers


class Ride:
    def __init__(self, arrs, scatter, via_sibling=False):
        assert not (scatter and via_sibling)
        self.arrs, self.scatter, self.via_sibling, self.out = list(arrs), scatter, via_sibling, None

    def out_shapes(self):
        return [jax.ShapeDtypeStruct(a.shape if self.scatter else (N_DEV,) + a.shape, a.dtype) for a in self.arrs]

    def sem_shapes(self):
        n = len(self.arrs)
        return [pltpu.SemaphoreType.DMA((n, N_DEV - 1)), pltpu.SemaphoreType.DMA((n, N_DEV - 1)),
                pltpu.SemaphoreType.DMA((n,))]

    def _copies(self, ins, outs, sems, landing):
        send_sems, recv_sems, local_sems = sems
        me, peers = _peers()
        local, remote = [], []
        for a, (src, dst) in enumerate(zip(ins, outs)):
            local.append(pltpu.make_async_copy(src.at[me] if self.scatter else src, dst.at[me], local_sems.at[a]))
            for k, (peer, peer_idx) in enumerate(peers):
                remote.append(pltpu.make_async_remote_copy(
                    src_ref=src.at[peer_idx] if self.scatter else src, dst_ref=dst.at[peer_idx if landing else me],
                    send_sem=send_sems.at[a, k], recv_sem=recv_sems.at[a, k], device_id=peer, device_id_type=MESH))
        return local, remote

    def _direct(self, k):
        return not self.via_sibling or k == 0 or (k + 1) % 2 == 0

    def start(self, ins, outs, sems):
        local, remote = self._copies(ins, outs, sems, landing=False)
        n_peers = N_DEV - 1
        for cp in local + [cp for i, cp in enumerate(remote) if self._direct(i % n_peers)]:
            cp.start()

    def wait(self, ins, outs, sems):
        local, remote = self._copies(ins, outs, sems, landing=True)
        n_peers = N_DEV - 1
        for i, cp in enumerate(remote):
            if self._direct(i % n_peers):
                cp.wait()
        if self.via_sibling:
            send_sems, recv_sems, _ = sems
            me, peers = _peers()
            sibling = peers[0][0]
            passed = []
            for a, dst in enumerate(outs):
                for j in range(1, n_peers, 2):
                    came, lands = peers[j][1], peers[j + 1][1]
                    pltpu.make_async_remote_copy(
                        src_ref=dst.at[came], dst_ref=dst.at[came], send_sem=send_sems.at[a, j + 1],
                        recv_sem=recv_sems.at[a, j + 1], device_id=sibling, device_id_type=MESH).start()
                    passed.append(pltpu.make_async_remote_copy(
                        src_ref=dst.at[came], dst_ref=dst.at[lands], send_sem=send_sems.at[a, j + 1],
                        recv_sem=recv_sems.at[a, j + 1], device_id=sibling, device_id_type=MESH))
            for cp in passed:
                cp.wait()
        for cp in local:
            cp.wait()


def _call(body, *, name, grid, in_specs, out_specs, out_shape, args, scratch_shapes=(), semantics=None, ride=None,
          aliases=None):
    scratch_shapes = list(scratch_shapes)
    if ride is None:
        return pl.pallas_call(
            body, name=name, grid=grid, in_specs=in_specs, out_specs=out_specs, out_shape=out_shape,
            scratch_shapes=scratch_shapes, input_output_aliases=aliases or {},
            compiler_params=_params(semantics if grid else None))(*args)
    assert not aliases
    n_in, n_out, n_sc, n_r = len(in_specs), len(out_specs), len(scratch_shapes), len(ride.arrs)

    def riding(*refs):
        ins, r_ins = refs[:n_in], refs[n_in:n_in + n_r]
        outs, r_outs = refs[n_in + n_r:n_in + n_r + n_out], refs[n_in + n_r + n_out:n_in + 2 * n_r + n_out]
        rest = refs[n_in + 2 * n_r + n_out:]
        scratch, sems = rest[:n_sc], rest[n_sc:]
        first = last = None
        for axis, size in enumerate(grid):
            pid = pl.program_id(axis)
            first = (pid == 0) if first is None else first & (pid == 0)
            last = (pid == size - 1) if last is None else last & (pid == size - 1)
        if grid:
            pl.when(first)(lambda: ride.start(r_ins, r_outs, sems))
        else:
            ride.start(r_ins, r_outs, sems)
        body(*ins, *outs, *scratch)
        if grid:
            pl.when(last)(lambda: ride.wait(r_ins, r_outs, sems))
        else:
            ride.wait(r_ins, r_outs, sems)

    any_spec = pl.BlockSpec(memory_space=pl.ANY)
    res = pl.pallas_call(
        riding, name=name, grid=grid, in_specs=list(in_specs) + [any_spec] * n_r,
        out_specs=list(out_specs) + [any_spec] * n_r, out_shape=list(out_shape) + ride.out_shapes(),
        scratch_shapes=scratch_shapes + ride.sem_shapes(),
        compiler_params=_params(("arbitrary",) * len(grid) if grid else None))(*args, *ride.arrs)
    ride.out = list(res[n_out:])
    return list(res[:n_out])


def exchange(ride, name):
    _call(lambda: None, name=name, grid=(), in_specs=[], out_specs=[], out_shape=[], args=[], ride=ride)
    return ride.out


LANES = 128


def _fit(dims, want):
    dims = tuple(dims)
    if len(set(dims)) == 1 and dims[0] <= want:
        return dims[0]
    return max(t for t in range(LANES, want + 1, LANES) if all(d % t == 0 for d in dims))


def _cols(arr):
    return arr.shape[-1] * (arr.shape[0] if len(arr.shape) == 3 else 1)


def _tile_spec(shape, rblk, cblk, rc):
    if len(shape) == 2:
        return pl.BlockSpec((rblk, cblk), rc)
    per = shape[2] // cblk

    def index_map(*ids):
        r, c = rc(*ids)
        return (c // per, r, c % per)

    return pl.BlockSpec((1, rblk, cblk), index_map)


def matmul(a, b, *, mode, out_dtype, name, bm=1024, bn=1024, bk=2048, out_slabs=1, m_part=(0, 1, 1), ride=None):
    a_rows, a_cols, b_rows, b_cols = a.shape[-2], _cols(a), b.shape[-2], _cols(b)
    (K, M) = (a_rows, a_cols) if mode == "tn" else (a_cols, a_rows)
    N = b_rows if mode == "nt" else b_cols
    assert K == (b_cols if mode == "nt" else b_rows), (name, a.shape, b.shape)
    first_range, n_ranges, of_ranges = m_part
    row0, M = first_range * (M // of_ranges), n_ranges * (M // of_ranges)
    out_shape = (M, N) if out_slabs == 1 else (out_slabs, M, N // out_slabs)
    widths = dict(m=[M], n=[N, out_shape[-1]], k=[K])
    widths["m" if mode == "tn" else "k"].append(a.shape[-1])
    widths["k" if mode == "nt" else "n"].append(b.shape[-1])
    bm, bn, bk = _fit(widths["m"], bm), _fit(widths["n"], bn), _fit(widths["k"], bk)
    nk = K // bk
    dims = {"nn": ((1,), (0,)), "nt": ((1,), (1,)), "tn": ((0,), (0,))}[mode]

    def val(ref):
        return ref[0] if len(ref.shape) == 3 else ref[...]

    def put(ref, x):
        if len(ref.shape) == 3:
            ref[0] = x.astype(ref.dtype)
        else:
            ref[...] = x.astype(ref.dtype)

    def body(a_ref, b_ref, o_ref, *scratch):
        if nk == 1:
            put(o_ref, _dot(val(a_ref), val(b_ref), dims))
            return
        acc_ref, = scratch
        k = pl.program_id(2)

        @pl.when(k == 0)
        def _():
            acc_ref[...] = jnp.zeros_like(acc_ref)

        acc_ref[...] += _dot(val(a_ref), val(b_ref), dims)

        @pl.when(k == nk - 1)
        def _():
            put(o_ref, acc_ref[...])

    assert row0 % bm == 0
    m0 = row0 // bm
    if mode == "tn":
        a_spec = _tile_spec(a.shape, bk, bm, lambda j, i, k: (k, m0 + i))
    else:
        a_spec = _tile_spec(a.shape, bm, bk, lambda j, i, k: (m0 + i, k))
    if mode == "nt":
        b_spec = _tile_spec(b.shape, bn, bk, lambda j, i, k: (j, k))
    else:
        b_spec = _tile_spec(b.shape, bk, bn, lambda j, i, k: (k, j))
    return _call(
        body, name=name, grid=(N // bn, M // bm, nk), in_specs=[a_spec, b_spec],
        out_specs=[_tile_spec(out_shape, bm, bn, lambda j, i, k: (i, j))],
        out_shape=[jax.ShapeDtypeStruct(out_shape, out_dtype)],
        scratch_shapes=[] if nk == 1 else [pltpu.VMEM((bm, bn), F32)],
        semantics=("parallel", "parallel", "arbitrary"), args=[a, b], ride=ride)[0]


ROW_TILE = 256


def _rms_stats(z):
    r = lax.rsqrt(jnp.mean(z * z, axis=-1, keepdims=True) + RMS_EPS)
    return r, z * r


def _rms_bwd(n, r, g, dout):
    dn = dout * g
    return r * (dn - n * jnp.mean(dn * n, axis=-1, keepdims=True))


def _row_spec(T, Dm):
    bt = min(ROW_TILE, T)
    return bt, pl.BlockSpec((bt, Dm), lambda i: (i, 0)), pl.BlockSpec((1, Dm), lambda i: (0, 0))


def prenorm_fwd(x, g, name):
    T, Dm = x.shape
    bt, row, vec = _row_spec(T, Dm)

    def body(x_ref, g_ref, h_ref):
        _, n = _rms_stats(x_ref[...])
        h_ref[...] = (n * g_ref[...]).astype(BF16)

    return pl.pallas_call(
        body, name=name, grid=(T // bt,), in_specs=[row, vec], out_specs=row,
        out_shape=jax.ShapeDtypeStruct((T, Dm), BF16), compiler_params=_params(("parallel",)),
    )(x, g)


def postnorm_prenorm_fwd(x, y, g_post, g_next, name):
    T, Dm = x.shape
    bt, row, vec = _row_spec(T, Dm)

    def body(x_ref, y_ref, gp_ref, gn_ref, o_ref, h_ref):
        _, n = _rms_stats(y_ref[...])
        x_new = x_ref[...] + n * gp_ref[...]
        o_ref[...] = x_new
        _, n_new = _rms_stats(x_new)
        h_ref[...] = (n_new * gn_ref[...]).astype(BF16)

    return pl.pallas_call(
        body, name=name, grid=(T // bt,), in_specs=[row, row, vec, vec], out_specs=[row, row],
        out_shape=[jax.ShapeDtypeStruct((T, Dm), F32), jax.ShapeDtypeStruct((T, Dm), BF16)],
        compiler_params=_params(("parallel",)),
    )(x, y, g_post, g_next)


def postnorm_loss(x, y, g, target, name):
    T, Dm = x.shape
    bt, row, vec = _row_spec(T, Dm)

    def body(x_ref, y_ref, g_ref, t_ref, l_ref, d_ref):
        @pl.when(pl.program_id(0) == 0)
        def _():
            l_ref[...] = jnp.zeros_like(l_ref)

        _, n = _rms_stats(y_ref[...])
        err = (x_ref[...] + n * g_ref[...]) - t_ref[...]
        per_tok = jnp.mean(err * err, axis=-1, keepdims=True)
        l_ref[...] += 0.5 * jnp.sum(per_tok, axis=0, keepdims=True)
        d_ref[...] = err * (1.0 / Dm)

    return pl.pallas_call(
        body, name=name, grid=(T // bt,), in_specs=[row, row, vec, row],
        out_specs=[pl.BlockSpec((1, 1), lambda i: (0, 0)), row],
        out_shape=[jax.ShapeDtypeStruct((1, 1), F32), jax.ShapeDtypeStruct((T, Dm), F32)],
        compiler_params=_params(("arbitrary",)),
    )(x, y, g, target)


def postnorm_bwd(y, g, dout, name):
    T, Dm = y.shape
    bt, row, vec = _row_spec(T, Dm)

    def body(y_ref, g_ref, d_ref, dy_ref, dg_ref):
        @pl.when(pl.program_id(0) == 0)
        def _():
            dg_ref[...] = jnp.zeros_like(dg_ref)

        r, n = _rms_stats(y_ref[...])
        dout_v = d_ref[...]
        dg_ref[...] += jnp.sum(dout_v * n, axis=0, keepdims=True)
        dy_ref[...] = _rms_bwd(n, r, g_ref[...], dout_v).astype(BF16)

    return pl.pallas_call(
        body, name=name, grid=(T // bt,), in_specs=[row, vec, row], out_specs=[row, vec],
        out_shape=[jax.ShapeDtypeStruct((T, Dm), BF16), jax.ShapeDtypeStruct((1, Dm), F32)],
        compiler_params=_params(("arbitrary",)),
    )(y, g, dout)


def prenorm_bwd(x, g, dhs, dres, name):
    T, Dm = x.shape
    bt, row, vec = _row_spec(T, Dm)
    n_dh = len(dhs)

    def body(x_ref, g_ref, *refs):
        dh_refs, (dr_ref, dx_ref, dg_ref) = refs[:n_dh], refs[n_dh:]

        @pl.when(pl.program_id(0) == 0)
        def _():
            dg_ref[...] = jnp.zeros_like(dg_ref)

        r, n = _rms_stats(x_ref[...])
        dh_v = dh_refs[0][...]
        for extra in dh_refs[1:]:
            dh_v = dh_v + extra[...]
        dg_ref[...] += jnp.sum(dh_v * n, axis=0, keepdims=True)
        dx_ref[...] = dr_ref[...] + _rms_bwd(n, r, g_ref[...], dh_v)

    return pl.pallas_call(
        body, name=name, grid=(T // bt,), in_specs=[row, vec] + [row] * (n_dh + 1), out_specs=[row, vec],
        out_shape=[jax.ShapeDtypeStruct((T, Dm), F32), jax.ShapeDtypeStruct((1, Dm), F32)],
        compiler_params=_params(("arbitrary",)),
    )(x, g, *dhs, dres)


def _silu(g):
    return g * _sigmoid(g)


def _gate_bwd(dgated, core, g):
    sg = _sigmoid(g)
    return dgated * (g * sg), dgated * core * (sg * (1.0 + g * (1.0 - sg)))


def _softmax_rows(s):
    e = jnp.exp(s - jnp.max(s, axis=-1, keepdims=True))
    return e * (1.0 / jnp.sum(e, axis=-1, keepdims=True))


def _band_scores(qk, bias, r0):
    s = qk * (HEAD_DIM ** -0.5) + bias
    if r0 is None:
        return s
    j = lax.broadcasted_iota(jnp.int32, s.shape, 1)
    return jnp.where(j >= A_PAD - r0, s, NEG_INF)


def _with_or_without_padding(t, r0, fn):
    n_padded = A_PAD // TQ
    pl.when(t < n_padded)(functools.partial(fn, r0))
    pl.when(t >= n_padded)(functools.partial(fn, None))


def _fill_padded_kv(p_ref, kp_ref, vp_ref):
    zeros = jnp.zeros((A_PAD, HEAD_DIM), BF16)
    kp_ref[0:A_PAD, :] = zeros
    vp_ref[0:A_PAD, :] = zeros
    kp_ref[A_PAD:, :] = p_ref[1, 0].astype(BF16)
    vp_ref[A_PAD:, :] = p_ref[2, 0].astype(BF16)


def _head_specs(S, order):
    def idx(fn):
        return lambda *ids: fn(**dict(zip(order, ids)))

    return (pl.BlockSpec((4, 1, S, HEAD_DIM), idx(lambda b, h, t: (0, b, 0, h))),
            pl.BlockSpec((1, TQ, HEAD_DIM), idx(lambda b, h, t: (b, t, h))))


def attn_a_fwd(proj, bias, name, ride=None):
    _, B, S, W = proj.shape
    nt = S // TQ

    def body(p_ref, b_ref, o_ref, gt_ref, kp_ref, vp_ref):
        t = pl.program_id(2)

        @pl.when(t == 0)
        def _():
            _fill_padded_kv(p_ref, kp_ref, vp_ref)

        r0 = pl.multiple_of(t * TQ, TQ)
        q = p_ref[0, 0, pl.ds(r0, TQ), :].astype(BF16)
        g = p_ref[3, 0, pl.ds(r0, TQ), :]

        def tile_out(pad_r0):
            p = _softmax_rows(_band_scores(_dot_nt(q, kp_ref[pl.ds(r0, A_KW), :]), b_ref[0], pad_r0))
            o = _dot_nn(p.astype(BF16), vp_ref[pl.ds(r0, A_KW), :])
            o_ref[0] = o
            gt_ref[0] = (o * _silu(g)).astype(BF16)

        _with_or_without_padding(t, r0, tile_out)

    seq, tile = _head_specs(S, "bht")
    return _call(
        body, name=name, grid=(B, HEADS, nt),
        in_specs=[seq, pl.BlockSpec((1, TQ, A_KW), lambda b, h, t: (h, 0, 0))], out_specs=[tile, tile],
        out_shape=[jax.ShapeDtypeStruct((B, S, W), F32), jax.ShapeDtypeStruct((B, S, W), BF16)],
        scratch_shapes=[pltpu.VMEM((A_PAD + S, HEAD_DIM), BF16), pltpu.VMEM((A_PAD + S, HEAD_DIM), BF16)],
        semantics=("parallel", "parallel", "arbitrary"), args=[proj, bias], ride=ride)


def attn_a_bwd(proj, bias, o, dgated, name, ride=None):
    _, B, S, W = proj.shape
    nt = S // TQ

    def body(p_ref, b_ref, o_ref, dgt_ref, dp_ref, db_ref, kp_ref, vp_ref, dk_ref, dv_ref):
        b_, t = pl.program_id(1), pl.program_id(2)

        @pl.when(t == 0)
        def _():
            _fill_padded_kv(p_ref, kp_ref, vp_ref)
            dk_ref[...] = jnp.zeros_like(dk_ref)
            dv_ref[...] = jnp.zeros_like(dv_ref)

        @pl.when((t == 0) & (b_ == 0))
        def _():
            db_ref[...] = jnp.zeros_like(db_ref)

        r0 = pl.multiple_of(t * TQ, TQ)
        rows, win = pl.ds(r0, TQ), pl.ds(r0, A_KW)
        q = p_ref[0, 0, rows, :].astype(BF16)
        g = p_ref[3, 0, rows, :]
        kw, vw = kp_ref[win, :], vp_ref[win, :]
        do, dg = _gate_bwd(dgt_ref[0], o_ref[0], g)
        do = do.astype(BF16)
        dp_ref[3, 0, rows, :] = dg.astype(BF16)

        def tile_grads(pad_r0):
            p = _softmax_rows(_band_scores(_dot_nt(q, kw), b_ref[0], pad_r0))
            dv_ref[win, :] += _dot_tn(p.astype(BF16), do)
            dpr = _dot_nt(do, vw)
            ds = p * (dpr - jnp.sum(p * dpr, axis=-1, keepdims=True))
            db_ref[0] += ds
            ds = (ds * (HEAD_DIM ** -0.5)).astype(BF16)
            dk_ref[win, :] += _dot_tn(ds, q)
            dp_ref[0, 0, rows, :] = _dot_nn(ds, kw).astype(BF16)

        _with_or_without_padding(t, r0, tile_grads)

        @pl.when(t == nt - 1)
        def _():
            dp_ref[1, 0] = dk_ref[A_PAD:, :].astype(BF16)
            dp_ref[2, 0] = dv_ref[A_PAD:, :].astype(BF16)

    seq, tile = _head_specs(S, "hbt")
    bias_spec = pl.BlockSpec((1, TQ, A_KW), lambda h, b, t: (h, 0, 0))
    return _call(
        body, name=name, grid=(HEADS, B, nt), in_specs=[seq, bias_spec, tile, tile], out_specs=[seq, bias_spec],
        out_shape=[jax.ShapeDtypeStruct(proj.shape, BF16), jax.ShapeDtypeStruct(bias.shape, F32)],
        scratch_shapes=[pltpu.VMEM((A_PAD + S, HEAD_DIM), BF16), pltpu.VMEM((A_PAD + S, HEAD_DIM), BF16),
                        pltpu.VMEM((A_PAD + S, HEAD_DIM), F32), pltpu.VMEM((A_PAD + S, HEAD_DIM), F32)],
        semantics=("arbitrary", "arbitrary", "arbitrary"), args=[proj, bias, o, dgated], ride=ride)


def band_bias(rel_bias):
    length = TQ + A_KW - 1
    first = REL_CLIP + 1 - TQ
    gen = jnp.concatenate([rel_bias[:, first:],
                           jnp.broadcast_to(rel_bias[:, 2 * REL_CLIP:], (HEADS, length - (N_REL - first)))], axis=1)
    rev = jnp.concatenate([gen[:, ::-1], jnp.zeros((HEADS, 1), rel_bias.dtype)], axis=1)
    sheared = jnp.tile(rev, (1, TQ))[:, :TQ * length].reshape(HEADS, TQ, length)
    i = lax.broadcasted_iota(jnp.int32, (TQ, A_KW), 0)
    j = lax.broadcasted_iota(jnp.int32, (TQ, A_KW), 1)
    first_key = (i // CHUNK) * CHUNK
    in_band = (j >= first_key) & (j < first_key + (LEFT_CHUNKS + 1) * CHUNK)
    return jnp.where(in_band, sheared[:, :, TQ - 1:], NEG_INF)


def _group_scan(a, u, carry, reverse=False):
    row = lax.broadcasted_iota(jnp.int32, u.shape, 0)
    for k in (1, 2, 4):
        shift = 8 - k if reverse else k
        valid = (row < 8 - k) if reverse else (row >= k)
        u_sh = pltpu.roll(u, shift, 0)
        if a is None:
            u = jnp.where(valid, u + u_sh, u)
        else:
            a_sh = pltpu.roll(a, shift, 0)
            u = jnp.where(valid, a * u_sh + u, u)
            a = jnp.where(valid, a * a_sh, a)
    return (u + carry) if a is None else (a * carry + u)


def _scan_rows(n_rows, step, carry0, reverse=False):
    groups = n_rows // 8

    def loop(i, carry):
        gi = (groups - 1 - i) if reverse else i
        return step(pl.multiple_of(gi * 8, 8), carry)

    return lax.fori_loop(0, groups, loop, carry0)


def fox_cum_fwd(f_logit, f_bias, name):
    B, S, L = f_logit.shape

    def body(f_ref, b_ref, c_ref):
        z = f_ref[0] + b_ref[...]
        c_ref[0] = jnp.minimum(z, 0.0) - _log1p(jnp.exp(-jnp.abs(z)))

        def step(r0, carry):
            h = _group_scan(None, c_ref[0, pl.ds(r0, 8), :], carry)
            c_ref[0, pl.ds(r0, 8), :] = h
            return h[7:8, :]

        _scan_rows(S, step, jnp.zeros((1, L), F32))

    return pl.pallas_call(
        body, name=name, grid=(B,),
        in_specs=[pl.BlockSpec((1, S, L), lambda b: (b, 0, 0)), pl.BlockSpec((1, L), lambda b: (0, 0))],
        out_specs=pl.BlockSpec((1, S, L), lambda b: (b, 0, 0)),
        out_shape=jax.ShapeDtypeStruct((B, S, L), F32), compiler_params=_params(("parallel",)),
    )(f_logit, f_bias)


def fox_cum_bwd(f_logit, f_bias, dcum, name):
    B, S, L = f_logit.shape

    def body(f_ref, b_ref, d_ref, df_ref, db_ref):
        @pl.when(pl.program_id(0) == 0)
        def _():
            db_ref[...] = jnp.zeros_like(db_ref)

        def step(r0, carry):
            h = _group_scan(None, d_ref[0, pl.ds(r0, 8), :], carry, reverse=True)
            df_ref[0, pl.ds(r0, 8), :] = h
            return h[0:1, :]

        _scan_rows(S, step, jnp.zeros((1, L), F32), reverse=True)
        df = df_ref[0] * _sigmoid(-(f_ref[0] + b_ref[...]))
        df_ref[0] = df
        db_ref[...] += jnp.sum(df, axis=0, keepdims=True)

    seq = pl.BlockSpec((1, S, L), lambda b: (b, 0, 0))
    vec = pl.BlockSpec((1, L), lambda b: (0, 0))
    return pl.pallas_call(
        body, name=name, grid=(B,), in_specs=[seq, vec, seq], out_specs=[seq, vec],
        out_shape=[jax.ShapeDtypeStruct((B, S, L), F32), jax.ShapeDtypeStruct((1, L), F32)],
        compiler_params=_params(("arbitrary",)),
    )(f_logit, f_bias, dcum)


def _head_row(cr, h):
    sub = lax.broadcasted_iota(jnp.int32, cr.shape, 0)
    return jnp.sum(jnp.where(sub == h, cr, 0.0), axis=0, keepdims=True)


def _fox_scores(qk, cc, ck, h, r0):
    lane = lax.broadcasted_iota(jnp.int32, cc.shape, 1)
    cq = jnp.sum(jnp.where(lane == h, cc, 0.0), axis=1, keepdims=True)
    s = qk * (HEAD_DIM ** -0.5) + (cq - ck)
    qpos = r0 + lax.broadcasted_iota(jnp.int32, s.shape, 0)
    kpos = lax.broadcasted_iota(jnp.int32, s.shape, 1)
    return jnp.where(kpos <= qpos, s, NEG_INF)


KEY_STEP = 256


def _by_causal_width(t, S, fn):
    per = KEY_STEP // TQ
    for c in range(S // KEY_STEP):
        pl.when(t // per == c)(functools.partial(fn, (c + 1) * KEY_STEP))


def fox_fwd(proj, cum_col, cum_row, name, ride=None):
    _, B, S, W = proj.shape
    nt = S // TQ

    def body(p_ref, cc_ref, cr_ref, o_ref, gt_ref, k_ref, v_ref):
        h, t = pl.program_id(1), pl.program_id(2)

        @pl.when(t == 0)
        def _():
            k_ref[...] = p_ref[1, 0].astype(BF16)
            v_ref[...] = p_ref[2, 0].astype(BF16)

        r0 = pl.multiple_of(t * TQ, TQ)
        q = p_ref[0, 0, pl.ds(r0, TQ), :].astype(BF16)
        g = p_ref[3, 0, pl.ds(r0, TQ), :]

        def tile_out(width):
            ck = _head_row(cr_ref[0, :, 0:width], h)
            p = _softmax_rows(_fox_scores(_dot_nt(q, k_ref[0:width, :]), cc_ref[0], ck, h, r0))
            o = _dot_nn(p.astype(BF16), v_ref[0:width, :])
            o_ref[0] = o
            gt_ref[0] = (o * _silu(g)).astype(BF16)

        _by_causal_width(t, S, tile_out)

    seq, tile = _head_specs(S, "bht")
    return _call(
        body, name=name, grid=(B, HEADS, nt),
        in_specs=[seq, pl.BlockSpec((1, TQ, cum_col.shape[2]), lambda b, h, t: (b, t, 0)),
                  pl.BlockSpec((1, HEADS, S), lambda b, h, t: (b, 0, 0))],
        out_specs=[tile, tile],
        out_shape=[jax.ShapeDtypeStruct((B, S, W), F32), jax.ShapeDtypeStruct((B, S, W), BF16)],
        scratch_shapes=[pltpu.VMEM((S, HEAD_DIM), BF16), pltpu.VMEM((S, HEAD_DIM), BF16)],
        semantics=("parallel", "parallel", "arbitrary"), args=[proj, cum_col, cum_row], ride=ride)


def fox_bwd(proj, cum_col, cum_row, o, dgated, name, ride=None):
    _, B, S, W = proj.shape
    nt = S // TQ

    def body(p_ref, cc_ref, cr_ref, o_ref, dgt_ref, dp_ref, dc_ref, k_ref, v_ref, dk_ref, dv_ref):
        h, t = pl.program_id(1), pl.program_id(2)

        @pl.when(t == 0)
        def _():
            k_ref[...] = p_ref[1, 0].astype(BF16)
            v_ref[...] = p_ref[2, 0].astype(BF16)
            dk_ref[...] = jnp.zeros_like(dk_ref)
            dv_ref[...] = jnp.zeros_like(dv_ref)
            dc_ref[...] = jnp.zeros_like(dc_ref)

        r0 = pl.multiple_of(t * TQ, TQ)
        rows = pl.ds(r0, TQ)
        q = p_ref[0, 0, rows, :].astype(BF16)
        g = p_ref[3, 0, rows, :]
        do, dg = _gate_bwd(dgt_ref[0], o_ref[0], g)
        do = do.astype(BF16)
        dp_ref[3, 0, rows, :] = dg.astype(BF16)

        def tile_grads(width):
            k, v = k_ref[0:width, :], v_ref[0:width, :]
            ck = _head_row(cr_ref[0, :, 0:width], h)
            p = _softmax_rows(_fox_scores(_dot_nt(q, k), cc_ref[0], ck, h, r0))
            dv_ref[0:width, :] += _dot_tn(p.astype(BF16), do)
            dpr = _dot_nt(do, v)
            ds = p * (dpr - jnp.sum(p * dpr, axis=-1, keepdims=True))
            dc_ref[0, 0, :, 0:width] += jnp.sum(ds, axis=0, keepdims=True)
            ds = (ds * (HEAD_DIM ** -0.5)).astype(BF16)
            dk_ref[0:width, :] += _dot_tn(ds, q)
            dp_ref[0, 0, rows, :] = _dot_nn(ds, k).astype(BF16)

        _by_causal_width(t, S, tile_grads)

        @pl.when(t == nt - 1)
        def _():
            dp_ref[1, 0] = dk_ref[...].astype(BF16)
            dp_ref[2, 0] = dv_ref[...].astype(BF16)

    seq, tile = _head_specs(S, "bht")
    return _call(
        body, name=name, grid=(B, HEADS, nt),
        in_specs=[seq, pl.BlockSpec((1, TQ, cum_col.shape[2]), lambda b, h, t: (b, t, 0)),
                  pl.BlockSpec((1, HEADS, S), lambda b, h, t: (b, 0, 0)), tile, tile],
        out_specs=[seq, pl.BlockSpec((1, 1, 1, S), lambda b, h, t: (b, h, 0, 0))],
        out_shape=[jax.ShapeDtypeStruct(proj.shape, BF16), jax.ShapeDtypeStruct((B, HEADS, 1, S), F32)],
        scratch_shapes=[pltpu.VMEM((S, HEAD_DIM), BF16), pltpu.VMEM((S, HEAD_DIM), BF16),
                        pltpu.VMEM((S, HEAD_DIM), F32), pltpu.VMEM((S, HEAD_DIM), F32)],
        semantics=("parallel", "parallel", "arbitrary"), args=[proj, cum_col, cum_row, o, dgated], ride=ride)


RG_ROWS = 512


def _rg_gates(xc, wa_ref, ba_ref, wx_ref, bx_ref, lam_ref):
    xcb = xc.astype(BF16)
    r = _sigmoid(_dot_nn(xcb, wa_ref[0]) + ba_ref[...])
    i = _sigmoid(_dot_nn(xcb, wx_ref[0]) + bx_ref[...])
    sp = _softplus(-lam_ref[...])
    log_a = (-RG_C * sp) * r
    a = jnp.exp(log_a)
    m = jnp.sqrt(-jnp.tanh(log_a) * (a * a + 1.0))
    return xcb, r, i, sp, a, m


def _rg_specs(B, S, rows, order):
    nc = S // rows

    def idx(fn):
        def index_map(*ids):
            v = dict(zip(order.lower(), ids))
            c = (nc - 1 - v["c"]) if "C" in order else v["c"]
            return fn(v["b"], v["d"], c)
        return index_map

    return dict(
        proj=pl.BlockSpec((2, 1, rows, RG_COLS), idx(lambda b, d, c: (0, b, c, d))),
        act=pl.BlockSpec((1, rows, RG_COLS), idx(lambda b, d, c: (b, c, d))),
        taps=pl.BlockSpec((CONV_WIDTH, RG_COLS), idx(lambda b, d, c: (0, d))),
        vec=pl.BlockSpec((1, RG_COLS), idx(lambda b, d, c: (0, d))),
        gate=pl.BlockSpec((1, RG_COLS, RG_COLS), idx(lambda b, d, c: (d, 0, 0))),
    )


def rglru_fwd(proj, conv_w, conv_b, wa, ba, wx, bx, lam, name, rows=RG_ROWS, ride=None):
    _, B, S, _ = proj.shape
    rows = min(rows, S)
    sp_ = _rg_specs(B, S, rows, "bdc")

    def body(p_ref, cw_ref, cb_ref, wa_ref, ba_ref, wx_ref, bx_ref, lam_ref,
             xc_ref, hs_ref, hp_ref, gt_ref, ext_ref, a_ref, u_ref, xcar_ref, hcar_ref):
        @pl.when(pl.program_id(2) == 0)
        def _():
            xcar_ref[...] = jnp.zeros_like(xcar_ref)
            hcar_ref[...] = jnp.zeros_like(hcar_ref)

        xr = p_ref[0, 0]
        ext_ref[0:8, :] = xcar_ref[...]
        ext_ref[8:, :] = xr
        xcar_ref[...] = xr[rows - 8:, :]
        xc = ext_ref[pl.ds(5, rows), :] * cw_ref[0:1, :]
        xc = xc + ext_ref[pl.ds(6, rows), :] * cw_ref[1:2, :]
        xc = xc + ext_ref[pl.ds(7, rows), :] * cw_ref[2:3, :]
        xc = xc + xr * cw_ref[3:4, :] + cb_ref[...]
        xc_ref[0] = xc
        _, _, i, _, a, m = _rg_gates(xc, wa_ref, ba_ref, wx_ref, bx_ref, lam_ref)
        a_ref[...] = a
        u_ref[...] = m * (i * xc)

        def step(r0, carry):
            h = _group_scan(a_ref[pl.ds(r0, 8), :], u_ref[pl.ds(r0, 8), :], carry)
            row = lax.broadcasted_iota(jnp.int32, h.shape, 0)
            hs_ref[0, pl.ds(r0, 8), :] = h
            hp_ref[0, pl.ds(r0, 8), :] = jnp.where(row == 0, carry, pltpu.roll(h, 1, 0))
            return h[7:8, :]

        hcar_ref[0:1, :] = _scan_rows(rows, step, hcar_ref[0:1, :])
        gt_ref[0] = (hs_ref[0] * _silu(p_ref[1, 0])).astype(BF16)

    act = jax.ShapeDtypeStruct((B, S, RG_WIDTH), F32)
    return _call(
        body, name=name, grid=(B, RG_GROUPS, S // rows),
        in_specs=[sp_["proj"], sp_["taps"], sp_["vec"], sp_["gate"], sp_["vec"], sp_["gate"], sp_["vec"], sp_["vec"]],
        out_specs=[sp_["act"]] * 4,
        out_shape=[act, act, act, jax.ShapeDtypeStruct((B, S, RG_WIDTH), BF16)],
        scratch_shapes=[pltpu.VMEM((rows + 8, RG_COLS), F32), pltpu.VMEM((rows, RG_COLS), F32),
                        pltpu.VMEM((rows, RG_COLS), F32), pltpu.VMEM((8, RG_COLS), F32), pltpu.VMEM((8, RG_COLS), F32)],
        semantics=("parallel", "parallel", "arbitrary"), args=[proj, conv_w, conv_b, wa, ba, wx, bx, lam], ride=ride)


def rglru_bwd(proj, xc, hs, hprev, dgated, conv_w, wa, ba, wx, bx, lam, name, rows=RG_ROWS, ride=None):
    _, B, S, _ = proj.shape
    rows = min(rows, S)
    sp_ = _rg_specs(B, S, rows, "dbC")

    def body(p_ref, xc_ref, hs_ref, hp_ref, dgt_ref, cw_ref, wa_ref, ba_ref, wx_ref, bx_ref, lam_ref,
             dp_ref, dcw_ref, dcb_ref, dwa_ref, dba_ref, dwx_ref, dbx_ref, dlam_ref,
             ext_ref, c_ref, l_ref, acar_ref, lcar_ref, dcar_ref):
        b_, c_ = pl.program_id(1), pl.program_id(2)

        @pl.when(c_ == 0)
        def _():
            acar_ref[...] = jnp.zeros_like(acar_ref)
            lcar_ref[...] = jnp.zeros_like(lcar_ref)
            dcar_ref[...] = jnp.zeros_like(dcar_ref)

        @pl.when((c_ == 0) & (b_ == 0))
        def _():
            for ref in (dcw_ref, dcb_ref, dwa_ref, dba_ref, dwx_ref, dbx_ref, dlam_ref):
                ref[...] = jnp.zeros_like(ref)

        xr, g = p_ref[0, 0], p_ref[1, 0]
        xc_v = xc_ref[0]
        xcb, r, i, sp, a, m = _rg_gates(xc_v, wa_ref, ba_ref, wx_ref, bx_ref, lam_ref)
        dhs, dg = _gate_bwd(dgt_ref[0], hs_ref[0], g)
        dp_ref[1, 0] = dg.astype(BF16)

        ext_ref[0:rows, :] = a
        ext_ref[rows:, :] = acar_ref[...]
        acar_ref[...] = a[0:8, :]
        c_ref[...] = ext_ref[pl.ds(1, rows), :]
        l_ref[...] = dhs

        def step(r0, carry):
            lam_g = _group_scan(c_ref[pl.ds(r0, 8), :], l_ref[pl.ds(r0, 8), :], carry, reverse=True)
            l_ref[pl.ds(r0, 8), :] = lam_g
            return lam_g[0:1, :]

        lcar_ref[0:1, :] = _scan_rows(rows, step, lcar_ref[0:1, :], reverse=True)
        du = l_ref[...]
        da = du * hp_ref[0]
        dlog_a = da * a - (du * (i * xc_v)) * (a * a / m)
        dr = dlog_a * (-RG_C * sp)
        dsp = jnp.sum(dlog_a * (-RG_C * r), axis=0, keepdims=True)
        dlam_ref[...] += dsp * (-_sigmoid(-lam_ref[...]))
        dpa = dr * (r * (1.0 - r))
        dpx = (du * (m * xc_v)) * (i * (1.0 - i))
        dba_ref[...] += jnp.sum(dpa, axis=0, keepdims=True)
        dbx_ref[...] += jnp.sum(dpx, axis=0, keepdims=True)
        dpa, dpx = dpa.astype(BF16), dpx.astype(BF16)
        dwa_ref[0] += _dot_tn(xcb, dpa)
        dwx_ref[0] += _dot_tn(xcb, dpx)
        dxc = du * (m * i) + _dot_nt(dpa, wa_ref[0]) + _dot_nt(dpx, wx_ref[0])

        dcb_ref[...] += jnp.sum(dxc, axis=0, keepdims=True)
        ext_ref[0:rows, :] = dxc
        ext_ref[rows:, :] = dcar_ref[...]
        dcar_ref[...] = dxc[0:8, :]
        dxr = jnp.zeros_like(dxc)
        for k in range(CONV_WIDTH):
            tap = CONV_WIDTH - 1 - k
            ahead = dxc if k == 0 else ext_ref[pl.ds(k, rows), :]
            dxr = dxr + ahead * cw_ref[tap:tap + 1, :]
            dcw_ref[tap:tap + 1, :] += jnp.sum(xr * ahead, axis=0, keepdims=True)
        dp_ref[0, 0] = dxr.astype(BF16)

    vec = jax.ShapeDtypeStruct((1, RG_WIDTH), F32)
    gate = jax.ShapeDtypeStruct((RG_GROUPS, RG_COLS, RG_COLS), F32)
    return _call(
        body, name=name, grid=(RG_GROUPS, B, S // rows),
        in_specs=[sp_["proj"], sp_["act"], sp_["act"], sp_["act"], sp_["act"], sp_["taps"],
                  sp_["gate"], sp_["vec"], sp_["gate"], sp_["vec"], sp_["vec"]],
        out_specs=[sp_["proj"], sp_["taps"], sp_["vec"], sp_["gate"], sp_["vec"], sp_["gate"], sp_["vec"], sp_["vec"]],
        out_shape=[jax.ShapeDtypeStruct(proj.shape, BF16), jax.ShapeDtypeStruct((CONV_WIDTH, RG_WIDTH), F32), vec,
                   gate, vec, gate, vec, vec],
        scratch_shapes=[pltpu.VMEM((rows + 8, RG_COLS), F32), pltpu.VMEM((rows, RG_COLS), F32),
                        pltpu.VMEM((rows, RG_COLS), F32), pltpu.VMEM((8, RG_COLS), F32),
                        pltpu.VMEM((8, RG_COLS), F32), pltpu.VMEM((8, RG_COLS), F32)],
        semantics=("arbitrary", "arbitrary", "arbitrary"),
        args=[proj, xc, hs, hprev, dgated, conv_w, wa, ba, wx, bx, lam], ride=ride)


def block_diag_gates(w):
    per = RG_COLS // RG_BLOCK
    w4 = w.reshape(RG_GROUPS, per, RG_BLOCK, RG_BLOCK)
    return jnp.einsum("dipq,ij->dipjq", w4, jnp.eye(per, dtype=w.dtype)).reshape(RG_GROUPS, RG_COLS, RG_COLS)


def block_diag_gates_t(dw):
    per = RG_COLS // RG_BLOCK
    dw6 = dw.reshape(RG_GROUPS, per, RG_BLOCK, per, RG_BLOCK)
    return jnp.stack([dw6[:, i, :, i, :] for i in range(per)], axis=1).reshape(RG_BLOCKS, RG_BLOCK, RG_BLOCK)


def adamw(w, parts, m, v, name, layer=0, prev=None, part_row0=0, row_tile=ROW_TILE):
    L, R, C = w.shape
    n_parts = parts.shape[0]
    br = row_tile if R % row_tile == 0 else R

    def body(w_ref, p_ref, m_ref, v_ref, *refs):
        g_ref, d_ref, nm_ref, nv_ref = refs[-4:]
        g = p_ref[0].astype(F32)
        for k in range(1, n_parts):
            g = g + p_ref[k].astype(F32)
        nm = ADAM_B1 * m_ref[0] + (1.0 - ADAM_B1) * g
        nv = ADAM_B2 * v_ref[0] + (1.0 - ADAM_B2) * (g * g)
        m_hat = nm / (1.0 - ADAM_B1 ** ADAM_STEP)
        v_hat = nv / (1.0 - ADAM_B2 ** ADAM_STEP)
        g_ref[0] = g
        d_ref[0] = -ADAM_LR * (m_hat / (jnp.sqrt(v_hat) + ADAM_EPS) + ADAM_WD * w_ref[0])
        nm_ref[0] = nm
        nv_ref[0] = nv

    slab = pl.BlockSpec((1, br, C), lambda i: (layer, i, 0))
    out = jax.ShapeDtypeStruct((L, R, C), F32)
    carried = [] if prev is None else list(prev)
    return _call(
        body, name=name, grid=(R // br,),
        in_specs=[slab, pl.BlockSpec((n_parts, br, C), lambda i: (0, part_row0 // br + i, 0)), slab, slab]
        + [pl.BlockSpec(memory_space=pl.ANY)] * len(carried),
        out_specs=[slab] * 4, out_shape=[out] * 4, semantics=("parallel",), args=[w, parts, m, v] + carried,
        aliases={4 + k: k for k in range(len(carried))})


def _seq(a, B):
    return a.reshape(a.shape[:-2] + (B, a.shape[-2] // B, a.shape[-1]))


def _flat(a):
    return a.reshape(a.shape[:-3] + (a.shape[-3] * a.shape[-2], a.shape[-1]))


def _tiles(w, which, **default):
    return dict(default, **w.get("tiles", {}).get(which, {}))


def mixer_a_fwd(h, w, B, tag, rides):
    proj = matmul(h, w["w_in"], mode="nn", out_dtype=F32, name=f"{tag}_proj", out_slabs=4,
                  ride=rides.pop(f"{tag}_proj", None), **_tiles(w, "proj"))
    o, gated = attn_a_fwd(_seq(proj, B), w["bias"], f"{tag}_attn", ride=rides.pop(f"{tag}_attn", None))
    return _flat(gated), dict(proj=proj, o=o)


def mixer_a_bwd(dgated, w, saved, B, tag, rides):
    dproj, dbias = attn_a_bwd(_seq(saved["proj"], B), w["bias"], saved["o"], _seq(dgated, B), f"{tag}_attn_bwd",
                              ride=rides.pop(f"{tag}_attn_bwd", None))
    return _flat(dproj), dict(bias=dbias)


def mixer_b_fwd(h, w, B, tag, rides):
    proj = matmul(h, w["w_in"], mode="nn", out_dtype=F32, name=f"{tag}_proj", out_slabs=2, bn=RG_COLS,
                  ride=rides.pop(f"{tag}_proj", None))
    xc, hs, hprev, gated = rglru_fwd(_seq(proj, B), w["conv_w"], w["conv_b"], w["wa"], w["ba"], w["wx"], w["bx"],
                                     w["lam"], f"{tag}_rglru", ride=rides.pop(f"{tag}_rglru", None))
    return _flat(gated), dict(proj=proj, xc=xc, hs=hs, hprev=hprev)


def mixer_b_bwd(dgated, w, saved, B, tag, rides):
    dproj, dcw, dcb, dwa, dba, dwx, dbx, dlam = rglru_bwd(
        _seq(saved["proj"], B), saved["xc"], saved["hs"], saved["hprev"], _seq(dgated, B),
        w["conv_w"], w["wa"], w["ba"], w["wx"], w["bx"], w["lam"], f"{tag}_rglru_bwd",
        ride=rides.pop(f"{tag}_rglru_bwd", None))
    return _flat(dproj), dict(conv_w=dcw, conv_b=dcb, wa=dwa, ba=dba, wx=dwx, bx=dbx, lam=dlam)


def mixer_c_fwd(h, w, B, tag, rides):
    proj = matmul(h, w["w_in"], mode="nn", out_dtype=F32, name=f"{tag}_proj", out_slabs=4,
                  ride=rides.pop(f"{tag}_proj", None), **_tiles(w, "proj"))
    f_logit = matmul(h, w["w_f"], mode="nn", out_dtype=F32, name=f"{tag}_fproj")
    cum = fox_cum_fwd(_seq(f_logit, B), w["f_bias"], f"{tag}_cum")
    cum_row = cum[:, :, :HEADS].transpose(0, 2, 1)
    o, gated = fox_fwd(_seq(proj, B), cum, cum_row, f"{tag}_attn", ride=rides.pop(f"{tag}_attn", None))
    return _flat(gated), dict(proj=proj, o=o, f_logit=f_logit, cum=cum, cum_row=cum_row)


def mixer_c_bwd(dgated, w, saved, B, tag, rides):
    dproj, dck = fox_bwd(_seq(saved["proj"], B), saved["cum"], saved["cum_row"], saved["o"], _seq(dgated, B),
                         f"{tag}_attn_bwd", ride=rides.pop(f"{tag}_attn_bwd", None))
    S = dck.shape[-1]
    dcum = jnp.pad(-dck.reshape(B, HEADS, S).transpose(0, 2, 1), ((0, 0), (0, 0), (0, HEAD_DIM - HEADS)))
    df, dfb = fox_cum_bwd(_seq(saved["f_logit"], B), w["f_bias"], dcum, f"{tag}_cum_bwd")
    return _flat(dproj), dict(f_bias=dfb, df=_flat(df).astype(BF16))


MIXERS = {"a": (mixer_a_fwd, mixer_a_bwd), "b": (mixer_b_fwd, mixer_b_bwd), "c": (mixer_c_fwd, mixer_c_bwd)}
LAYER_KINDS = "abca"


def local_step(x, target, norm_pre, norm_post, get_layer, rides, on_grads):
    B, S, Dm = x.shape
    n_layers = len(LAYER_KINDS)
    xs = [x.reshape(B * S, Dm)]
    saved, layers = [], []
    h = prenorm_fwd(xs[0], norm_pre[0:1], "l0a_prenorm")
    for li, kind in enumerate(LAYER_KINDS):
        tag = f"l{li}{kind}"
        w = get_layer(li)
        gated, sv = MIXERS[kind][0](h, w, B, tag, rides)
        if callable(w["w_out"]):
            w["w_out"] = w["w_out"]()
        y = matmul(gated, w["w_out"], mode="nn", out_dtype=F32, name=f"{tag}_out", ride=rides.pop(f"{tag}_out", None))
        saved.append(dict(sv, h=h, gated=gated, y=y))
        layers.append(w)
        if li + 1 < n_layers:
            x_new, h = postnorm_prenorm_fwd(xs[-1], y, norm_post[li:li + 1], norm_pre[li + 1:li + 2],
                                            f"{tag}_postnorm")
            xs.append(x_new)
    loss, dx = postnorm_loss(xs[-1], y, norm_post[n_layers - 1:], target.reshape(B * S, Dm), "loss")

    for li in reversed(range(n_layers)):
        kind, w, sv = LAYER_KINDS[li], layers[li], saved[li]
        tag = f"l{li}{kind}"
        dy, dg_post = postnorm_bwd(sv["y"], norm_post[li:li + 1], dx, f"{tag}_postnorm_bwd")
        on_grads(li, "norm_post", dg_post)
        on_grads(li, "w_out", matmul(sv["gated"], dy, mode="tn", out_dtype=BF16, name=f"{tag}_dwout",
                                     ride=rides.pop(f"{tag}_dwout", None)))
        dgated = matmul(dy, w["w_out"], mode="nt", out_dtype=F32, name=f"{tag}_dgated",
                        ride=rides.pop(f"{tag}_dgated", None))
        dproj, gw = MIXERS[kind][1](dgated, w, sv, B, tag, rides)
        df = gw.pop("df", None)
        for name, value in gw.items():
            on_grads(li, name, value)
        parts = w.get("dwin_parts", [(0, 1, 1)])
        for i, m_part in enumerate(parts):
            suffix = f"_{i}" if len(parts) > 1 else ""
            on_grads(li, "w_in" + suffix,
                     matmul(sv["h"], dproj, mode="tn", out_dtype=BF16, name=f"{tag}_dwin{suffix}", m_part=m_part,
                            out_slabs=w["grad_slabs"], ride=rides.pop(f"{tag}_dwin{suffix}", None),
                            **_tiles(w, "dwin")))
        if df is not None:
            on_grads(li, "w_f", matmul(sv["h"], df, mode="tn", out_dtype=BF16, name=f"{tag}_dwf"))
        dhs = [matmul(dproj, w["w_in"], mode="nt", out_dtype=F32, name=f"{tag}_dh",
                      ride=rides.pop(f"{tag}_dh", None), **_tiles(w, "dh"))]
        if df is not None:
            dhs.append(matmul(df, w["w_f"], mode="nt", out_dtype=F32, name=f"{tag}_dhf"))
        dx, dg_pre = prenorm_bwd(xs[li], norm_pre[li:li + 1], dhs, dx, f"{tag}_prenorm_bwd")
        on_grads(li, "norm_pre", dg_pre)
    assert not rides, list(rides)
    return loss, dx.reshape(B, S, Dm)


WEIGHTS = ("norm_pre", "norm_post", "a_w_in", "a_rel_bias", "a_w_out", "b_w_in", "b_conv_w", "b_conv_b",
           "b_gate_a_w", "b_gate_a_b", "b_gate_x_w", "b_gate_x_b", "b_lambda", "b_w_out", "c_w_in", "c_f_bias",
           "c_w_out")
C_SHARD = (4 * D_MODEL + HEADS) // N_DEV


def _rows(gathered):
    return gathered.reshape(gathered.shape[0] * gathered.shape[1], gathered.shape[2])


def layer_a(w_in, w_out, rel_bias):
    return dict(w_in=w_in, w_out=w_out if callable(w_out) else _rows(w_out), bias=band_bias(rel_bias),
                grad_slabs=N_DEV)


def layer_b(w_in, w_out, conv_w, small):
    return dict(
        w_in=w_in, w_out=_rows(w_out), grad_slabs=N_DEV,
        conv_w=conv_w.transpose(1, 0, 2).reshape(CONV_WIDTH, RG_WIDTH),
        conv_b=small["b_conv_b"], lam=small["b_lambda"],
        wa=block_diag_gates(small["b_gate_a_w"][0]).astype(BF16), ba=small["b_gate_a_b"].reshape(1, RG_WIDTH),
        wx=block_diag_gates(small["b_gate_x_w"][0]).astype(BF16), bx=small["b_gate_x_b"].reshape(1, RG_WIDTH))


def layer_c(w_in, w_out, small):
    full = w_in.transpose(1, 0, 2).reshape(D_MODEL, N_DEV * C_SHARD)
    return dict(w_in=full[:, :4 * D_MODEL], w_f=jnp.pad(full[:, 4 * D_MODEL:], ((0, 0), (0, HEAD_DIM - HEADS))),
                w_out=_rows(w_out), grad_slabs=1,
                f_bias=jnp.pad(small["c_f_bias"], ((0, 0), (0, HEAD_DIM - HEADS))))


def c_w_in_blocks(dmain, df):
    full = jnp.concatenate([dmain, df[:, :HEADS].astype(dmain.dtype)], axis=1)
    return full.reshape(D_MODEL, N_DEV, C_SHARD).transpose(1, 0, 2)


def _row_blocks(g):
    return g.reshape(N_DEV, g.shape[0] // N_DEV, g.shape[1])


PACK_LANES = 128
PACK_ALIGN = 8 * PACK_LANES


def pack(parts):
    flat = []
    for p in parts:
        n = p.size
        flat.append(jnp.pad(p.reshape(n), (0, -n % PACK_ALIGN)).reshape(-1, PACK_LANES))
    rows = sum(f.shape[0] for f in flat)
    flat.append(jnp.zeros((-rows % ROW_TILE, PACK_LANES), F32))
    return jnp.concatenate(flat, axis=0)


def unpack(packed, shapes):
    out, row = [], 0
    for shape in shapes:
        n = 1
        for s in shape:
            n *= s
        n_rows = (n + PACK_ALIGN - 1) // PACK_ALIGN * 8
        out.append(packed[row:row + n_rows].reshape(-1)[:n].reshape(shape))
        row += n_rows
    return out


LATE = (("a_rel_bias", slice(0, 1)), ("norm_pre", slice(0, 2)), ("norm_post", slice(0, 1)))
EARLY = (("a_rel_bias", slice(1, 2)), ("norm_pre", slice(2, 4)), ("norm_post", slice(1, 4)),
         ("b_conv_b", slice(None)), ("b_gate_a_w", slice(None)), ("b_gate_a_b", slice(None)),
         ("b_gate_x_w", slice(None)), ("b_gate_x_b", slice(None)), ("b_lambda", slice(None)),
         ("c_f_bias", slice(None)))


def _pieces(tree, pieces):
    return [tree[name][sl] for name, sl in pieces]


def kernel(x, norm_pre, norm_post, a_w_in, a_rel_bias, a_w_out, b_w_in, b_conv_w, b_conv_b, b_gate_a_w, b_gate_a_b, b_gate_x_w, b_gate_x_b, b_lambda, b_w_out, c_w_in, c_f_bias, c_w_out, loss_target, m_norm_pre, m_norm_post, m_a_w_in, m_a_rel_bias, m_a_w_out, m_b_w_in, m_b_conv_w, m_b_conv_b, m_b_gate_a_w, m_b_gate_a_b, m_b_gate_x_w, m_b_gate_x_b, m_b_lambda, m_b_w_out, m_c_w_in, m_c_f_bias, m_c_w_out, v_norm_pre, v_norm_post, v_a_w_in, v_a_rel_bias, v_a_w_out, v_b_w_in, v_b_conv_w, v_b_conv_b, v_b_gate_a_w, v_b_gate_a_b, v_b_gate_x_w, v_b_gate_x_b, v_b_lambda, v_b_w_out, v_c_w_in, v_c_f_bias, v_c_w_out):
    args = dict(locals())
    w = {n: args[n] for n in WEIGHTS}
    m = {n: args["m_" + n] for n in WEIGHTS}
    v = {n: args["v_" + n] for n in WEIGHTS}

    a_in, a_out = a_w_in.astype(BF16), a_w_out.astype(BF16)
    gather_a0 = Ride([a_in[0]], scatter=False, via_sibling=True)
    in_l0_proj = Ride([b_w_in[0].astype(BF16), b_conv_w[0], a_out[0]], scatter=False, via_sibling=True)
    in_l0_attn = Ride([c_w_in[0].astype(BF16), b_w_out[0].astype(BF16)], scatter=False, via_sibling=True)
    in_l1_proj = Ride([c_w_out[0].astype(BF16)], scatter=False)
    in_l2_proj = Ride([a_out[1]], scatter=False)
    in_l2_attn = Ride([a_in[1]], scatter=False, via_sibling=True)
    exchange(gather_a0, "gather_l0")
    rides = {"l0a_proj": in_l0_proj, "l0a_attn": in_l0_attn, "l1b_proj": in_l1_proj, "l2c_proj": in_l2_proj,
             "l2c_attn": in_l2_attn}

    def get_layer(li):
        if li == 0:
            return dict(layer_a(gather_a0.out[0], lambda: _rows(in_l0_proj.out[2]), a_rel_bias[0]),
                        dwin_parts=[(0, 1, 4), (1, 1, 4), (2, 2, 4)])
        if li == 1:
            return layer_b(in_l0_proj.out[0], in_l0_attn.out[1], in_l0_proj.out[1], w)
        if li == 2:
            return dict(layer_c(in_l0_attn.out[0], in_l1_proj.out[0], w), tiles=dict(proj=dict(bn=2048)))
        return layer_a(in_l2_attn.out[0], in_l2_proj.out[0], a_rel_bias[1])

    grads = [dict() for _ in LAYER_KINDS]
    scatters = {}

    def rel_bias_grad(j, dbias):
        return jax.vjp(band_bias, a_rel_bias[j])[1](dbias)[0][None]

    def early_partial():
        gb, gc = grads[1], grads[2]
        tree = dict(
            a_rel_bias=jnp.concatenate([jnp.zeros((1, HEADS, N_REL), F32), rel_bias_grad(1, grads[3]["bias"])]),
            norm_pre=jnp.concatenate([jnp.zeros((2, D_MODEL), F32)] + [grads[li]["norm_pre"] for li in (2, 3)]),
            norm_post=jnp.concatenate([jnp.zeros((1, D_MODEL), F32)] + [grads[li]["norm_post"] for li in (1, 2, 3)]),
            b_conv_b=gb["conv_b"], b_lambda=gb["lam"],
            b_gate_a_w=block_diag_gates_t(gb["wa"])[None], b_gate_a_b=gb["ba"].reshape(1, RG_BLOCKS, RG_BLOCK),
            b_gate_x_w=block_diag_gates_t(gb["wx"])[None], b_gate_x_b=gb["bx"].reshape(1, RG_BLOCKS, RG_BLOCK),
            c_f_bias=gc["f_bias"][:, :HEADS])
        return pack(_pieces(tree, EARLY))

    def send(key, host, blocks, scatter=True, via_sibling=False):
        ride = rides.setdefault(host, Ride([], scatter, via_sibling))
        assert (ride.scatter, ride.via_sibling) == (scatter, via_sibling)
        scatters[key] = (ride, len(ride.arrs))
        ride.arrs.append(blocks)

    def on_grads(li, name, value):
        g = grads[li]
        g[name] = value
        if (li, name) == (3, "w_out"):
            send("a1_out", "l3a_attn_bwd", _row_blocks(value))
        elif (li, name) == (3, "w_in"):
            send("a1_in", "l2c_attn_bwd", value)
        elif (li, name) == (2, "w_out"):
            send("c_out", "l2c_attn_bwd", _row_blocks(value))
        elif (li, name) == (2, "w_f"):
            blocks = c_w_in_blocks(g["w_in"], value)
            send("c_in_0", "l2c_dh", blocks[:, :D_MODEL // 2])
            send("c_in_1", "l1b_rglru_bwd", blocks[:, D_MODEL // 2:])
        elif (li, name) == (1, "w_out"):
            send("b_out", "l1b_dh", _row_blocks(value))
        elif (li, name) == (1, "w_in"):
            send("b_in", "l0a_attn_bwd", value)
            send("b_conv", "l0a_attn_bwd",
                 g["conv_w"].reshape(CONV_WIDTH, N_DEV, RG_WIDTH // N_DEV).transpose(1, 0, 2))
        elif (li, name) == (1, "lam"):
            send("early", "l1b_dwin", early_partial(), scatter=False, via_sibling=True)
        elif (li, name) == (0, "w_out"):
            send("a0_out", "l0a_attn_bwd", _row_blocks(value))
        elif (li, name) == (0, "w_in_0"):
            send("a0_in_0", "l0a_dwin_1", value)
        elif (li, name) == (0, "w_in_1"):
            send("a0_in_1", "l0a_dwin_2", value)
        elif (li, name) == (0, "w_in_2"):
            send("a0_in_2", "l0a_dh", value)

    loss, grad_x = local_step(x, loss_target, norm_pre, norm_post, get_layer, rides, on_grads)
    late_tree = dict(a_rel_bias=rel_bias_grad(0, grads[0]["bias"]), norm_post=grads[0]["norm_post"],
                     norm_pre=jnp.concatenate([grads[0]["norm_pre"], grads[1]["norm_pre"]]))
    late_parts = exchange(Ride([pack([late_tree[n] for n, _ in LATE])], scatter=False), "gather_late_grads")[0]

    def sharded(name, slab_parts):
        shape = w[name].shape
        slabs = (len(slab_parts), shape[0] * shape[1] // len(slab_parts), shape[2])
        outs = None
        for j, (parts, row0) in enumerate(slab_parts):
            outs = adamw(w[name].reshape(slabs), parts, m[name].reshape(slabs), v[name].reshape(slabs),
                         f"adamw_{name}_{j}", layer=j, prev=outs, part_row0=row0)
        return [o.reshape(shape) for o in outs]

    def received(key):
        ride, position = scatters[key]
        return ride.out[position]

    res = dict(
        a_w_in=sharded("a_w_in", [(received("a0_in_0"), 0), (received("a0_in_1"), 0), (received("a0_in_2"), 0),
                                  (received("a0_in_2"), D_MODEL // 4)]
                       + [(received("a1_in"), q * D_MODEL // 4) for q in range(4)]),
        a_w_out=sharded("a_w_out", [(received("a0_out"), 0), (received("a1_out"), 0)]),
        b_w_in=sharded("b_w_in", [(received("b_in"), 0)]),
        b_w_out=sharded("b_w_out", [(received("b_out"), 0)]),
        b_conv_w=sharded("b_conv_w", [(received("b_conv"), 0)]),
        c_w_in=sharded("c_w_in", [(received("c_in_0"), 0), (received("c_in_1"), 0)]),
        c_w_out=sharded("c_w_out", [(received("c_out"), 0)]))

    packed = {}
    for label, pieces, parts in (("early", EARLY, received("early")), ("late", LATE, late_parts)):
        outs = adamw(pack(_pieces(w, pieces))[None], parts, pack(_pieces(m, pieces))[None],
                     pack(_pieces(v, pieces))[None], f"adamw_replicated_{label}")
        shapes = [w[n][sl].shape for n, sl in pieces]
        packed[label] = [dict(zip([n for n, _ in pieces], unpack(o[0], shapes))) for o in outs]
    for n in ("b_conv_b", "b_gate_a_w", "b_gate_a_b", "b_gate_x_w", "b_gate_x_b", "b_lambda", "c_f_bias"):
        res[n] = [packed["early"][k][n] for k in range(4)]
    for n in ("a_rel_bias", "norm_pre", "norm_post"):
        res[n] = [jnp.concatenate([packed["late"][k][n], packed["early"][k][n]]) for k in range(4)]

    total = lax.psum(loss[0, 0], ("x", "y", "c"))
    return (total, grad_x, *[res[n][0] for n in WEIGHTS], *[res[n][1] for n in WEIGHTS],
            *[res[n][2] for n in WEIGHTS], *[res[n][3] for n in WEIGHTS])
```

```python
import functools

import jax
import jax.numpy as jnp
from jax import lax
from jax.experimental import pallas as pl
from jax.experimental.pallas import tpu as pltpu

F32 = jnp.float32
BF16 = jnp.bfloat16

N_DEV = 8
D_MODEL = 2048
HEADS = 16
HEAD_DIM = 128
CHUNK = 64
LEFT_CHUNKS = 8
REL_CLIP = 256
N_REL = 2 * REL_CLIP + 1
TQ = 256
A_PAD = LEFT_CHUNKS * CHUNK
A_KW = A_PAD + TQ
RG_WIDTH = 2560
RG_BLOCKS = 16
RG_BLOCK = 160
RG_COLS = 640
RG_GROUPS = RG_WIDTH // RG_COLS
RG_C = 8.0
CONV_WIDTH = 4
RMS_EPS = 1e-6
NEG_INF = -1e30
ADAM_LR = 0.001
ADAM_B1 = 0.9
ADAM_B2 = 0.999
ADAM_EPS = 1e-08
ADAM_WD = 0.01
ADAM_STEP = 10
VMEM_LIMIT = 56 * 1024 * 1024
MESH = pl.DeviceIdType.MESH


def _params(sem, vmem=VMEM_LIMIT):
    return pltpu.CompilerParams(dimension_semantics=sem, vmem_limit_bytes=vmem)


def _sigmoid(x):
    return 1.0 / (1.0 + jnp.exp(-x))


def _log1p(y):
    u = 1.0 + y
    return jnp.where(u == 1.0, y, jnp.log(u) * (y / jnp.where(u == 1.0, 1.0, u - 1.0)))


def _softplus(x):
    return jnp.maximum(x, 0.0) + _log1p(jnp.exp(-jnp.abs(x)))


def _dot(a, b, dims):
    return lax.dot_general(a, b, (dims, ((), ())), preferred_element_type=F32)


def _dot_nn(a, b):
    return _dot(a, b, ((1,), (0,)))


def _dot_nt(a, b):
    return _dot(a, b, ((1,), (1,)))


def _dot_tn(a, b):
    return _dot(a, b, ((0,), (0,)))


def _peers():
    x, y, c = lax.axis_index("x"), lax.axis_index("y"), lax.axis_index("c")
    me = 4 * x + 2 * y + c
    peers = []
    for k in range(1, N_DEV):
        px = 1 - x if k & 4 else x
        py = 1 - y if k & 2 else y
        pc = 1 - c if k & 1 else c
        peers.append(((px, py, pc), 4 * px + 2 * py + pc))
    return me, peers


class Ride:
    def __init__(self, arrs, scatter, via_sibling=False):
        assert not (scatter and via_sibling)
        self.arrs, self.scatter, self.via_sibling, self.out = list(arrs), scatter, via_sibling, None

    def out_shapes(self):
        return [jax.ShapeDtypeStruct(a.shape if self.scatter else (N_DEV,) + a.shape, a.dtype) for a in self.arrs]

    def sem_shapes(self):
        n = len(self.arrs)
        return [pltpu.SemaphoreType.DMA((n, N_DEV - 1)), pltpu.SemaphoreType.DMA((n, N_DEV - 1)),
                pltpu.SemaphoreType.DMA((n,))]

    def _copies(self, ins, outs, sems, landing):
        send_sems, recv_sems, local_sems = sems
        me, peers = _peers()
        local, remote = [], []
        for a, (src, dst) in enumerate(zip(ins, outs)):
            local.append(pltpu.make_async_copy(src.at[me] if self.scatter else src, dst.at[me], local_sems.at[a]))
            for k, (peer, peer_idx) in enumerate(peers):
                remote.append(pltpu.make_async_remote_copy(
                    src_ref=src.at[peer_idx] if self.scatter else src, dst_ref=dst.at[peer_idx if landing else me],
                    send_sem=send_sems.at[a, k], recv_sem=recv_sems.at[a, k], device_id=peer, device_id_type=MESH))
        return local, remote

    def _direct(self, k):
        return not self.via_sibling or k == 0 or (k + 1) % 2 == 0

    def start(self, ins, outs, sems):
        local, remote = self._copies(ins, outs, sems, landing=False)
        n_peers = N_DEV - 1
        for cp in local + [cp for i, cp in enumerate(remote) if self._direct(i % n_peers)]:
            cp.start()

    def wait(self, ins, outs, sems):
        local, remote = self._copies(ins, outs, sems, landing=True)
        n_peers = N_DEV - 1
        for i, cp in enumerate(remote):
            if self._direct(i % n_peers):
                cp.wait()
        if self.via_sibling:
            send_sems, recv_sems, _ = sems
            me, peers = _peers()
            sibling = peers[0][0]
            passed = []
            for a, dst in enumerate(outs):
                for j in range(1, n_peers, 2):
                    came, lands = peers[j][1], peers[j + 1][1]
                    pltpu.make_async_remote_copy(
                        src_ref=dst.at[came], dst_ref=dst.at[came], send_sem=send_sems.at[a, j + 1],
                        recv_sem=recv_sems.at[a, j + 1], device_id=sibling, device_id_type=MESH).start()
                    passed.append(pltpu.make_async_remote_copy(
                        src_ref=dst.at[came], dst_ref=dst.at[lands], send_sem=send_sems.at[a, j + 1],
                        recv_sem=recv_sems.at[a, j + 1], device_id=sibling, device_id_type=MESH))
            for cp in passed:
                cp.wait()
        for cp in local:
            cp.wait()


def _call(body, *, name, grid, in_specs, out_specs, out_shape, args, scratch_shapes=(), semantics=None, ride=None,
          aliases=None):
    scratch_shapes = list(scratch_shapes)
    if ride is None:
        return pl.pallas_call(
            body, name=name, grid=grid, in_specs=in_specs, out_specs=out_specs, out_shape=out_shape,
            scratch_shapes=scratch_shapes, input_output_aliases=aliases or {},
            compiler_params=_params(semantics if grid else None))(*args)
    assert not aliases
    n_in, n_out, n_sc, n_r = len(in_specs), len(out_specs), len(scratch_shapes), len(ride.arrs)

    def riding(*refs):
        ins, r_ins = refs[:n_in], refs[n_in:n_in + n_r]
        outs, r_outs = refs[n_in + n_r:n_in + n_r + n_out], refs[n_in + n_r + n_out:n_in + 2 * n_r + n_out]
        rest = refs[n_in + 2 * n_r + n_out:]
        scratch, sems = rest[:n_sc], rest[n_sc:]
        first = last = None
        for axis, size in enumerate(grid):
            pid = pl.program_id(axis)
            first = (pid == 0) if first is None else first & (pid == 0)
            last = (pid == size - 1) if last is None else last & (pid == size - 1)
        if grid:
            pl.when(first)(lambda: ride.start(r_ins, r_outs, sems))
        else:
            ride.start(r_ins, r_outs, sems)
        body(*ins, *outs, *scratch)
        if grid:
            pl.when(last)(lambda: ride.wait(r_ins, r_outs, sems))
        else:
            ride.wait(r_ins, r_outs, sems)

    any_spec = pl.BlockSpec(memory_space=pl.ANY)
    res = pl.pallas_call(
        riding, name=name, grid=grid, in_specs=list(in_specs) + [any_spec] * n_r,
        out_specs=list(out_specs) + [any_spec] * n_r, out_shape=list(out_shape) + ride.out_shapes(),
        scratch_shapes=scratch_shapes + ride.sem_shapes(),
        compiler_params=_params(("arbitrary",) * len(grid) if grid else None))(*args, *ride.arrs)
    ride.out = list(res[n_out:])
    return list(res[:n_out])


def exchange(ride, name):
    _call(lambda: None, name=name, grid=(), in_specs=[], out_specs=[], out_shape=[], args=[], ride=ride)
    return ride.out


LANES = 128


def _fit(dims, want):
    dims = tuple(dims)
    if len(set(dims)) == 1 and dims[0] <= want:
        return dims[0]
    return max(t for t in range(LANES, want + 1, LANES) if all(d % t == 0 for d in dims))


def _cols(arr):
    return arr.shape[-1] * (arr.shape[0] if len(arr.shape) == 3 else 1)


def _tile_spec(shape, rblk, cblk, rc):
    if len(shape) == 2:
        return pl.BlockSpec((rblk, cblk), rc)
    per = shape[2] // cblk

    def index_map(*ids):
        r, c = rc(*ids)
        return (c // per, r, c % per)

    return pl.BlockSpec((1, rblk, cblk), index_map)


def matmul(a, b, *, mode, out_dtype, name, bm=1024, bn=1024, bk=2048, out_slabs=1, m_part=(0, 1, 1), ride=None):
    a_rows, a_cols, b_rows, b_cols = a.shape[-2], _cols(a), b.shape[-2], _cols(b)
    (K, M) = (a_rows, a_cols) if mode == "tn" else (a_cols, a_rows)
    N = b_rows if mode == "nt" else b_cols
    assert K == (b_cols if mode == "nt" else b_rows), (name, a.shape, b.shape)
    first_range, n_ranges, of_ranges = m_part
    row0, M = first_range * (M // of_ranges), n_ranges * (M // of_ranges)
    out_shape = (M, N) if out_slabs == 1 else (out_slabs, M, N // out_slabs)
    widths = dict(m=[M], n=[N, out_shape[-1]], k=[K])
    widths["m" if mode == "tn" else "k"].append(a.shape[-1])
    widths["k" if mode == "nt" else "n"].append(b.shape[-1])
    bm, bn, bk = _fit(widths["m"], bm), _fit(widths["n"], bn), _fit(widths["k"], bk)
    nk = K // bk
    dims = {"nn": ((1,), (0,)), "nt": ((1,), (1,)), "tn": ((0,), (0,))}[mode]

    def val(ref):
        return ref[0] if len(ref.shape) == 3 else ref[...]

    def put(ref, x):
        if len(ref.shape) == 3:
            ref[0] = x.astype(ref.dtype)
        else:
            ref[...] = x.astype(ref.dtype)

    def body(a_ref, b_ref, o_ref, *scratch):
        if nk == 1:
            put(o_ref, _dot(val(a_ref), val(b_ref), dims))
            return
        acc_ref, = scratch
        k = pl.program_id(2)

        @pl.when(k == 0)
        def _():
            acc_ref[...] = jnp.zeros_like(acc_ref)

        acc_ref[...] += _dot(val(a_ref), val(b_ref), dims)

        @pl.when(k == nk - 1)
        def _():
            put(o_ref, acc_ref[...])

    assert row0 % bm == 0
    m0 = row0 // bm
    if mode == "tn":
        a_spec = _tile_spec(a.shape, bk, bm, lambda j, i, k: (k, m0 + i))
    else:
        a_spec = _tile_spec(a.shape, bm, bk, lambda j, i, k: (m0 + i, k))
    if mode == "nt":
        b_spec = _tile_spec(b.shape, bn, bk, lambda j, i, k: (j, k))
    else:
        b_spec = _tile_spec(b.shape, bk, bn, lambda j, i, k: (k, j))
    return _call(
        body, name=name, grid=(N // bn, M // bm, nk), in_specs=[a_spec, b_spec],
        out_specs=[_tile_spec(out_shape, bm, bn, lambda j, i, k: (i, j))],
        out_shape=[jax.ShapeDtypeStruct(out_shape, out_dtype)],
        scratch_shapes=[] if nk == 1 else [pltpu.VMEM((bm, bn), F32)],
        semantics=("parallel", "parallel", "arbitrary"), args=[a, b], ride=ride)[0]


ROW_TILE = 256


def _rms_stats(z):
    r = lax.rsqrt(jnp.mean(z * z, axis=-1, keepdims=True) + RMS_EPS)
    return r, z * r


def _rms_bwd(n, r, g, dout):
    dn = dout * g
    return r * (dn - n * jnp.mean(dn * n, axis=-1, keepdims=True))


def _row_spec(T, Dm):
    bt = min(ROW_TILE, T)
    return bt, pl.BlockSpec((bt, Dm), lambda i: (i, 0)), pl.BlockSpec((1, Dm), lambda i: (0, 0))


def prenorm_fwd(x, g, name):
    T, Dm = x.shape
    bt, row, vec = _row_spec(T, Dm)

    def body(x_ref, g_ref, h_ref):
        _, n = _rms_stats(x_ref[...])
        h_ref[...] = (n * g_ref[...]).astype(BF16)

    return pl.pallas_call(
        body, name=name, grid=(T // bt,), in_specs=[row, vec], out_specs=row,
        out_shape=jax.ShapeDtypeStruct((T, Dm), BF16), compiler_params=_params(("parallel",)),
    )(x, g)


def postnorm_prenorm_fwd(x, y, g_post, g_next, name):
    T, Dm = x.shape
    bt, row, vec = _row_spec(T, Dm)

    def body(x_ref, y_ref, gp_ref, gn_ref, o_ref, h_ref):
        _, n = _rms_stats(y_ref[...])
        x_new = x_ref[...] + n * gp_ref[...]
        o_ref[...] = x_new
        _, n_new = _rms_stats(x_new)
        h_ref[...] = (n_new * gn_ref[...]).astype(BF16)

    return pl.pallas_call(
        body, name=name, grid=(T // bt,), in_specs=[row, row, vec, vec], out_specs=[row, row],
        out_shape=[jax.ShapeDtypeStruct((T, Dm), F32), jax.ShapeDtypeStruct((T, Dm), BF16)],
        compiler_params=_params(("parallel",)),
    )(x, y, g_post, g_next)


def postnorm_loss(x, y, g, target, name):
    T, Dm = x.shape
    bt, row, vec = _row_spec(T, Dm)

    def body(x_ref, y_ref, g_ref, t_ref, l_ref, d_ref):
        @pl.when(pl.program_id(0) == 0)
        def _():
            l_ref[...] = jnp.zeros_like(l_ref)

        _, n = _rms_stats(y_ref[...])
        err = (x_ref[...] + n * g_ref[...]) - t_ref[...]
        per_tok = jnp.mean(err * err, axis=-1, keepdims=True)
        l_ref[...] += 0.5 * jnp.sum(per_tok, axis=0, keepdims=True)
        d_ref[...] = err * (1.0 / Dm)

    return pl.pallas_call(
        body, name=name, grid=(T // bt,), in_specs=[row, row, vec, row],
        out_specs=[pl.BlockSpec((1, 1), lambda i: (0, 0)), row],
        out_shape=[jax.ShapeDtypeStruct((1, 1), F32), jax.ShapeDtypeStruct((T, Dm), F32)],
        compiler_params=_params(("arbitrary",)),
    )(x, y, g, target)


def postnorm_bwd(y, g, dout, name):
    T, Dm = y.shape
    bt, row, vec = _row_spec(T, Dm)

    def body(y_ref, g_ref, d_ref, dy_ref, dg_ref):
        @pl.when(pl.program_id(0) == 0)
        def _():
            dg_ref[...] = jnp.zeros_like(dg_ref)

        r, n = _rms_stats(y_ref[...])
        dout_v = d_ref[...]
        dg_ref[...] += jnp.sum(dout_v * n, axis=0, keepdims=True)
        dy_ref[...] = _rms_bwd(n, r, g_ref[...], dout_v).astype(BF16)

    return pl.pallas_call(
        body, name=name, grid=(T // bt,), in_specs=[row, vec, row], out_specs=[row, vec],
        out_shape=[jax.ShapeDtypeStruct((T, Dm), BF16), jax.ShapeDtypeStruct((1, Dm), F32)],
        compiler_params=_params(("arbitrary",)),
    )(y, g, dout)


def prenorm_bwd(x, g, dhs, dres, name):
    T, Dm = x.shape
    bt, row, vec = _row_spec(T, Dm)
    n_dh = len(dhs)

    def body(x_ref, g_ref, *refs):
        dh_refs, (dr_ref, dx_ref, dg_ref) = refs[:n_dh], refs[n_dh:]

        @pl.when(pl.program_id(0) == 0)
        def _():
            dg_ref[...] = jnp.zeros_like(dg_ref)

        r, n = _rms_stats(x_ref[...])
        dh_v = dh_refs[0][...]
        for extra in dh_refs[1:]:
            dh_v = dh_v + extra[...]
        dg_ref[...] += jnp.sum(dh_v * n, axis=0, keepdims=True)
        dx_ref[...] = dr_ref[...] + _rms_bwd(n, r, g_ref[...], dh_v)

    return pl.pallas_call(
        body, name=name, grid=(T // bt,), in_specs=[row, vec] + [row] * (n_dh + 1), out_specs=[row, vec],
        out_shape=[jax.ShapeDtypeStruct((T, Dm), F32), jax.ShapeDtypeStruct((1, Dm), F32)],
        compiler_params=_params(("arbitrary",)),
    )(x, g, *dhs, dres)


def _silu(g):
    return g * _sigmoid(g)


def _gate_bwd(dgated, core, g):
    sg = _sigmoid(g)
    return dgated * (g * sg), dgated * core * (sg * (1.0 + g * (1.0 - sg)))


def _softmax_rows(s):
    e = jnp.exp(s - jnp.max(s, axis=-1, keepdims=True))
    return e * (1.0 / jnp.sum(e, axis=-1, keepdims=True))


def _band_scores(qk, bias, r0):
    s = qk * (HEAD_DIM ** -0.5) + bias
    j = lax.broadcasted_iota(jnp.int32, s.shape, 1)
    return jnp.where(j >= A_PAD - r0, s, NEG_INF)


def _fill_padded_kv(p_ref, kp_ref, vp_ref):
    zeros = jnp.zeros((A_PAD, HEAD_DIM), BF16)
    kp_ref[0:A_PAD, :] = zeros
    vp_ref[0:A_PAD, :] = zeros
    kp_ref[A_PAD:, :] = p_ref[1, 0].astype(BF16)
    vp_ref[A_PAD:, :] = p_ref[2, 0].astype(BF16)


def _head_specs(S, order):
    def idx(fn):
        return lambda *ids: fn(**dict(zip(order, ids)))

    return (pl.BlockSpec((4, 1, S, HEAD_DIM), idx(lambda b, h, t: (0, b, 0, h))),
            pl.BlockSpec((1, TQ, HEAD_DIM), idx(lambda b, h, t: (b, t, h))))


def attn_a_fwd(proj, bias, name, ride=None):
    _, B, S, W = proj.shape
    nt = S // TQ

    def body(p_ref, b_ref, o_ref, gt_ref, kp_ref, vp_ref):
        t = pl.program_id(2)

        @pl.when(t == 0)
        def _():
            _fill_padded_kv(p_ref, kp_ref, vp_ref)

        r0 = pl.multiple_of(t * TQ, TQ)
        q = p_ref[0, 0, pl.ds(r0, TQ), :].astype(BF16)
        g = p_ref[3, 0, pl.ds(r0, TQ), :]
        p = _softmax_rows(_band_scores(_dot_nt(q, kp_ref[pl.ds(r0, A_KW), :]), b_ref[0], r0))
        o = _dot_nn(p.astype(BF16), vp_ref[pl.ds(r0, A_KW), :])
        o_ref[0] = o
        gt_ref[0] = (o * _silu(g)).astype(BF16)

    seq, tile = _head_specs(S, "bht")
    return _call(
        body, name=name, grid=(B, HEADS, nt),
        in_specs=[seq, pl.BlockSpec((1, TQ, A_KW), lambda b, h, t: (h, 0, 0))], out_specs=[tile, tile],
        out_shape=[jax.ShapeDtypeStruct((B, S, W), F32), jax.ShapeDtypeStruct((B, S, W), BF16)],
        scratch_shapes=[pltpu.VMEM((A_PAD + S, HEAD_DIM), BF16), pltpu.VMEM((A_PAD + S, HEAD_DIM), BF16)],
        semantics=("parallel", "parallel", "arbitrary"), args=[proj, bias], ride=ride)


def attn_a_bwd(proj, bias, o, dgated, name, ride=None):
    _, B, S, W = proj.shape
    nt = S // TQ

    def body(p_ref, b_ref, o_ref, dgt_ref, dp_ref, db_ref, kp_ref, vp_ref, dk_ref, dv_ref):
        b_, t = pl.program_id(1), pl.program_id(2)

        @pl.when(t == 0)
        def _():
            _fill_padded_kv(p_ref, kp_ref, vp_ref)
            dk_ref[...] = jnp.zeros_like(dk_ref)
            dv_ref[...] = jnp.zeros_like(dv_ref)

        @pl.when((t == 0) & (b_ == 0))
        def _():
            db_ref[...] = jnp.zeros_like(db_ref)

        r0 = pl.multiple_of(t * TQ, TQ)
        rows, win = pl.ds(r0, TQ), pl.ds(r0, A_KW)
        q = p_ref[0, 0, rows, :].astype(BF16)
        g = p_ref[3, 0, rows, :]
        kw, vw = kp_ref[win, :], vp_ref[win, :]
        p = _softmax_rows(_band_scores(_dot_nt(q, kw), b_ref[0], r0))
        do, dg = _gate_bwd(dgt_ref[0], o_ref[0], g)
        do = do.astype(BF16)
        dv_ref[win, :] += _dot_tn(p.astype(BF16), do)
        dpr = _dot_nt(do, vw)
        ds = p * (dpr - jnp.sum(p * dpr, axis=-1, keepdims=True))
        db_ref[0] += ds
        ds = (ds * (HEAD_DIM ** -0.5)).astype(BF16)
        dk_ref[win, :] += _dot_tn(ds, q)
        dp_ref[0, 0, rows, :] = _dot_nn(ds, kw).astype(BF16)
        dp_ref[3, 0, rows, :] = dg.astype(BF16)

        @pl.when(t == nt - 1)
        def _():
            dp_ref[1, 0] = dk_ref[A_PAD:, :].astype(BF16)
            dp_ref[2, 0] = dv_ref[A_PAD:, :].astype(BF16)

    seq, tile = _head_specs(S, "hbt")
    bias_spec = pl.BlockSpec((1, TQ, A_KW), lambda h, b, t: (h, 0, 0))
    return _call(
        body, name=name, grid=(HEADS, B, nt), in_specs=[seq, bias_spec, tile, tile], out_specs=[seq, bias_spec],
        out_shape=[jax.ShapeDtypeStruct(proj.shape, BF16), jax.ShapeDtypeStruct(bias.shape, F32)],
        scratch_shapes=[pltpu.VMEM((A_PAD + S, HEAD_DIM), BF16), pltpu.VMEM((A_PAD + S, HEAD_DIM), BF16),
                        pltpu.VMEM((A_PAD + S, HEAD_DIM), F32), pltpu.VMEM((A_PAD + S, HEAD_DIM), F32)],
        semantics=("arbitrary", "arbitrary", "arbitrary"), args=[proj, bias, o, dgated], ride=ride)


def band_bias(rel_bias):
    length = TQ + A_KW - 1
    first = REL_CLIP + 1 - TQ
    gen = jnp.concatenate([rel_bias[:, first:],
                           jnp.broadcast_to(rel_bias[:, 2 * REL_CLIP:], (HEADS, length - (N_REL - first)))], axis=1)
    rev = jnp.concatenate([gen[:, ::-1], jnp.zeros((HEADS, 1), rel_bias.dtype)], axis=1)
    sheared = jnp.tile(rev, (1, TQ))[:, :TQ * length].reshape(HEADS, TQ, length)
    i = lax.broadcasted_iota(jnp.int32, (TQ, A_KW), 0)
    j = lax.broadcasted_iota(jnp.int32, (TQ, A_KW), 1)
    first_key = (i // CHUNK) * CHUNK
    in_band = (j >= first_key) & (j < first_key + (LEFT_CHUNKS + 1) * CHUNK)
    return jnp.where(in_band, sheared[:, :, TQ - 1:], NEG_INF)


def _group_scan(a, u, carry, reverse=False):
    row = lax.broadcasted_iota(jnp.int32, u.shape, 0)
    for k in (1, 2, 4):
        shift = 8 - k if reverse else k
        valid = (row < 8 - k) if reverse else (row >= k)
        u_sh = pltpu.roll(u, shift, 0)
        if a is None:
            u = jnp.where(valid, u + u_sh, u)
        else:
            a_sh = pltpu.roll(a, shift, 0)
            u = jnp.where(valid, a * u_sh + u, u)
            a = jnp.where(valid, a * a_sh, a)
    return (u + carry) if a is None else (a * carry + u)


SCAN_UNROLL = 4


def _scan_rows(n_rows, step, carry0, reverse=False):
    groups = n_rows // 8

    def loop(i, carry):
        gi = (groups - 1 - i) if reverse else i
        return step(pl.multiple_of(gi * 8, 8), carry)

    return lax.fori_loop(0, groups, loop, carry0, unroll=SCAN_UNROLL)


def fox_cum_fwd(f_logit, f_bias, name):
    B, S, L = f_logit.shape

    def body(f_ref, b_ref, c_ref):
        z = f_ref[0] + b_ref[...]
        c_ref[0] = jnp.minimum(z, 0.0) - _log1p(jnp.exp(-jnp.abs(z)))

        def step(r0, carry):
            h = _group_scan(None, c_ref[0, pl.ds(r0, 8), :], carry)
            c_ref[0, pl.ds(r0, 8), :] = h
            return h[7:8, :]

        _scan_rows(S, step, jnp.zeros((1, L), F32))

    return pl.pallas_call(
        body, name=name, grid=(B,),
        in_specs=[pl.BlockSpec((1, S, L), lambda b: (b, 0, 0)), pl.BlockSpec((1, L), lambda b: (0, 0))],
        out_specs=pl.BlockSpec((1, S, L), lambda b: (b, 0, 0)),
        out_shape=jax.ShapeDtypeStruct((B, S, L), F32), compiler_params=_params(("parallel",)),
    )(f_logit, f_bias)


def fox_cum_bwd(f_logit, f_bias, dcum, name):
    B, S, L = f_logit.shape

    def body(f_ref, b_ref, d_ref, df_ref, db_ref):
        @pl.when(pl.program_id(0) == 0)
        def _():
            db_ref[...] = jnp.zeros_like(db_ref)

        def step(r0, carry):
            h = _group_scan(None, d_ref[0, pl.ds(r0, 8), :], carry, reverse=True)
            df_ref[0, pl.ds(r0, 8), :] = h
            return h[0:1, :]

        _scan_rows(S, step, jnp.zeros((1, L), F32), reverse=True)
        df = df_ref[0] * _sigmoid(-(f_ref[0] + b_ref[...]))
        df_ref[0] = df
        db_ref[...] += jnp.sum(df, axis=0, keepdims=True)

    seq = pl.BlockSpec((1, S, L), lambda b: (b, 0, 0))
    vec = pl.BlockSpec((1, L), lambda b: (0, 0))
    return pl.pallas_call(
        body, name=name, grid=(B,), in_specs=[seq, vec, seq], out_specs=[seq, vec],
        out_shape=[jax.ShapeDtypeStruct((B, S, L), F32), jax.ShapeDtypeStruct((1, L), F32)],
        compiler_params=_params(("arbitrary",)),
    )(f_logit, f_bias, dcum)


def _head_row(cr, h):
    sub = lax.broadcasted_iota(jnp.int32, cr.shape, 0)
    return jnp.sum(jnp.where(sub == h, cr, 0.0), axis=0, keepdims=True)


def _fox_scores(qk, cc, ck, h, r0):
    lane = lax.broadcasted_iota(jnp.int32, cc.shape, 1)
    cq = jnp.sum(jnp.where(lane == h, cc, 0.0), axis=1, keepdims=True)
    s = qk * (HEAD_DIM ** -0.5) + (cq - ck)
    qpos = r0 + lax.broadcasted_iota(jnp.int32, s.shape, 0)
    kpos = lax.broadcasted_iota(jnp.int32, s.shape, 1)
    return jnp.where(kpos <= qpos, s, NEG_INF)


KEY_STEP = 256


def _by_causal_width(t, S, fn):
    per = KEY_STEP // TQ
    for c in range(S // KEY_STEP):
        pl.when(t // per == c)(functools.partial(fn, (c + 1) * KEY_STEP))


def fox_fwd(proj, cum_col, cum_row, name, ride=None):
    _, B, S, W = proj.shape
    nt = S // TQ

    def body(p_ref, cc_ref, cr_ref, o_ref, gt_ref, k_ref, v_ref):
        h, t = pl.program_id(1), pl.program_id(2)

        @pl.when(t == 0)
        def _():
            k_ref[...] = p_ref[1, 0].astype(BF16)
            v_ref[...] = p_ref[2, 0].astype(BF16)

        r0 = pl.multiple_of(t * TQ, TQ)
        q = p_ref[0, 0, pl.ds(r0, TQ), :].astype(BF16)
        g = p_ref[3, 0, pl.ds(r0, TQ), :]

        def tile_out(width):
            ck = _head_row(cr_ref[0, :, 0:width], h)
            p = _softmax_rows(_fox_scores(_dot_nt(q, k_ref[0:width, :]), cc_ref[0], ck, h, r0))
            o = _dot_nn(p.astype(BF16), v_ref[0:width, :])
            o_ref[0] = o
            gt_ref[0] = (o * _silu(g)).astype(BF16)

        _by_causal_width(t, S, tile_out)

    seq, tile = _head_specs(S, "bht")
    return _call(
        body, name=name, grid=(B, HEADS, nt),
        in_specs=[seq, pl.BlockSpec((1, TQ, cum_col.shape[2]), lambda b, h, t: (b, t, 0)),
                  pl.BlockSpec((1, HEADS, S), lambda b, h, t: (b, 0, 0))],
        out_specs=[tile, tile],
        out_shape=[jax.ShapeDtypeStruct((B, S, W), F32), jax.ShapeDtypeStruct((B, S, W), BF16)],
        scratch_shapes=[pltpu.VMEM((S, HEAD_DIM), BF16), pltpu.VMEM((S, HEAD_DIM), BF16)],
        semantics=("parallel", "parallel", "arbitrary"), args=[proj, cum_col, cum_row], ride=ride)


def fox_bwd(proj, cum_col, cum_row, o, dgated, name, ride=None):
    _, B, S, W = proj.shape
    nt = S // TQ

    def body(p_ref, cc_ref, cr_ref, o_ref, dgt_ref, dp_ref, dc_ref, k_ref, v_ref, dk_ref, dv_ref):
        h, t = pl.program_id(1), pl.program_id(2)

        @pl.when(t == 0)
        def _():
            k_ref[...] = p_ref[1, 0].astype(BF16)
            v_ref[...] = p_ref[2, 0].astype(BF16)
            dk_ref[...] = jnp.zeros_like(dk_ref)
            dv_ref[...] = jnp.zeros_like(dv_ref)
            dc_ref[...] = jnp.zeros_like(dc_ref)

        r0 = pl.multiple_of(t * TQ, TQ)
        rows = pl.ds(r0, TQ)
        q = p_ref[0, 0, rows, :].astype(BF16)
        g = p_ref[3, 0, rows, :]
        do, dg = _gate_bwd(dgt_ref[0], o_ref[0], g)
        do = do.astype(BF16)
        dp_ref[3, 0, rows, :] = dg.astype(BF16)

        def tile_grads(width):
            k, v = k_ref[0:width, :], v_ref[0:width, :]
            ck = _head_row(cr_ref[0, :, 0:width], h)
            p = _softmax_rows(_fox_scores(_dot_nt(q, k), cc_ref[0], ck, h, r0))
            dv_ref[0:width, :] += _dot_tn(p.astype(BF16), do)
            dpr = _dot_nt(do, v)
            ds = p * (dpr - jnp.sum(p * dpr, axis=-1, keepdims=True))
            dc_ref[0, 0, :, 0:width] += jnp.sum(ds, axis=0, keepdims=True)
            ds = (ds * (HEAD_DIM ** -0.5)).astype(BF16)
            dk_ref[0:width, :] += _dot_tn(ds, q)
            dp_ref[0, 0, rows, :] = _dot_nn(ds, k).astype(BF16)

        _by_causal_width(t, S, tile_grads)

        @pl.when(t == nt - 1)
        def _():
            dp_ref[1, 0] = dk_ref[...].astype(BF16)
            dp_ref[2, 0] = dv_ref[...].astype(BF16)

    seq, tile = _head_specs(S, "bht")
    return _call(
        body, name=name, grid=(B, HEADS, nt),
        in_specs=[seq, pl.BlockSpec((1, TQ, cum_col.shape[2]), lambda b, h, t: (b, t, 0)),
                  pl.BlockSpec((1, HEADS, S), lambda b, h, t: (b, 0, 0)), tile, tile],
        out_specs=[seq, pl.BlockSpec((1, 1, 1, S), lambda b, h, t: (b, h, 0, 0))],
        out_shape=[jax.ShapeDtypeStruct(proj.shape, BF16), jax.ShapeDtypeStruct((B, HEADS, 1, S), F32)],
        scratch_shapes=[pltpu.VMEM((S, HEAD_DIM), BF16), pltpu.VMEM((S, HEAD_DIM), BF16),
                        pltpu.VMEM((S, HEAD_DIM), F32), pltpu.VMEM((S, HEAD_DIM), F32)],
        semantics=("parallel", "parallel", "arbitrary"), args=[proj, cum_col, cum_row, o, dgated], ride=ride)


RG_ROWS = 512


def _rg_gates(xc, wa_ref, ba_ref, wx_ref, bx_ref, lam_ref):
    xcb = xc.astype(BF16)
    r = _sigmoid(_dot_nn(xcb, wa_ref[0]) + ba_ref[...])
    i = _sigmoid(_dot_nn(xcb, wx_ref[0]) + bx_ref[...])
    sp = _softplus(-lam_ref[...])
    log_a = (-RG_C * sp) * r
    a = jnp.exp(log_a)
    m = jnp.sqrt(-jnp.tanh(log_a) * (a * a + 1.0))
    return xcb, r, i, sp, a, m


def _rg_specs(B, S, rows, order):
    nc = S // rows

    def idx(fn):
        def index_map(*ids):
            v = dict(zip(order.lower(), ids))
            c = (nc - 1 - v["c"]) if "C" in order else v["c"]
            return fn(v["b"], v["d"], c)
        return index_map

    return dict(
        proj=pl.BlockSpec((2, 1, rows, RG_COLS), idx(lambda b, d, c: (0, b, c, d))),
        act=pl.BlockSpec((1, rows, RG_COLS), idx(lambda b, d, c: (b, c, d))),
        taps=pl.BlockSpec((CONV_WIDTH, RG_COLS), idx(lambda b, d, c: (0, d))),
        vec=pl.BlockSpec((1, RG_COLS), idx(lambda b, d, c: (0, d))),
        gate=pl.BlockSpec((1, RG_COLS, RG_COLS), idx(lambda b, d, c: (d, 0, 0))),
    )


def rglru_fwd(proj, conv_w, conv_b, wa, ba, wx, bx, lam, name, rows=RG_ROWS, ride=None):
    _, B, S, _ = proj.shape
    rows = min(rows, S)
    sp_ = _rg_specs(B, S, rows, "bdc")

    def body(p_ref, cw_ref, cb_ref, wa_ref, ba_ref, wx_ref, bx_ref, lam_ref,
             xc_ref, hs_ref, hp_ref, gt_ref, ext_ref, a_ref, u_ref, xcar_ref, hcar_ref):
        @pl.when(pl.program_id(2) == 0)
        def _():
            xcar_ref[...] = jnp.zeros_like(xcar_ref)
            hcar_ref[...] = jnp.zeros_like(hcar_ref)

        xr = p_ref[0, 0]
        ext_ref[0:8, :] = xcar_ref[...]
        ext_ref[8:, :] = xr
        xcar_ref[...] = xr[rows - 8:, :]
        xc = ext_ref[pl.ds(5, rows), :] * cw_ref[0:1, :]
        xc = xc + ext_ref[pl.ds(6, rows), :] * cw_ref[1:2, :]
        xc = xc + ext_ref[pl.ds(7, rows), :] * cw_ref[2:3, :]
        xc = xc + xr * cw_ref[3:4, :] + cb_ref[...]
        xc_ref[0] = xc
        _, _, i, _, a, m = _rg_gates(xc, wa_ref, ba_ref, wx_ref, bx_ref, lam_ref)
        a_ref[...] = a
        u_ref[...] = m * (i * xc)

        def step(r0, carry):
            h = _group_scan(a_ref[pl.ds(r0, 8), :], u_ref[pl.ds(r0, 8), :], carry)
            row = lax.broadcasted_iota(jnp.int32, h.shape, 0)
            hs_ref[0, pl.ds(r0, 8), :] = h
            hp_ref[0, pl.ds(r0, 8), :] = jnp.where(row == 0, carry, pltpu.roll(h, 1, 0))
            return h[7:8, :]

        hcar_ref[0:1, :] = _scan_rows(rows, step, hcar_ref[0:1, :])
        gt_ref[0] = (hs_ref[0] * _silu(p_ref[1, 0])).astype(BF16)

    act = jax.ShapeDtypeStruct((B, S, RG_WIDTH), F32)
    return _call(
        body, name=name, grid=(B, RG_GROUPS, S // rows),
        in_specs=[sp_["proj"], sp_["taps"], sp_["vec"], sp_["gate"], sp_["vec"], sp_["gate"], sp_["vec"], sp_["vec"]],
        out_specs=[sp_["act"]] * 4,
        out_shape=[act, act, act, jax.ShapeDtypeStruct((B, S, RG_WIDTH), BF16)],
        scratch_shapes=[pltpu.VMEM((rows + 8, RG_COLS), F32), pltpu.VMEM((rows, RG_COLS), F32),
                        pltpu.VMEM((rows, RG_COLS), F32), pltpu.VMEM((8, RG_COLS), F32), pltpu.VMEM((8, RG_COLS), F32)],
        semantics=("parallel", "parallel", "arbitrary"), args=[proj, conv_w, conv_b, wa, ba, wx, bx, lam], ride=ride)


def rglru_bwd(proj, xc, hs, hprev, dgated, conv_w, wa, ba, wx, bx, lam, name, rows=RG_ROWS, ride=None):
    _, B, S, _ = proj.shape
    rows = min(rows, S)
    sp_ = _rg_specs(B, S, rows, "dbC")

    def body(p_ref, xc_ref, hs_ref, hp_ref, dgt_ref, cw_ref, wa_ref, ba_ref, wx_ref, bx_ref, lam_ref,
             dp_ref, dcw_ref, dcb_ref, dwa_ref, dba_ref, dwx_ref, dbx_ref, dlam_ref,
             ext_ref, c_ref, l_ref, acar_ref, lcar_ref, dcar_ref):
        b_, c_ = pl.program_id(1), pl.program_id(2)

        @pl.when(c_ == 0)
        def _():
            acar_ref[...] = jnp.zeros_like(acar_ref)
            lcar_ref[...] = jnp.zeros_like(lcar_ref)
            dcar_ref[...] = jnp.zeros_like(dcar_ref)

        @pl.when((c_ == 0) & (b_ == 0))
        def _():
            for ref in (dcw_ref, dcb_ref, dwa_ref, dba_ref, dwx_ref, dbx_ref, dlam_ref):
                ref[...] = jnp.zeros_like(ref)

        xr, g = p_ref[0, 0], p_ref[1, 0]
        xc_v = xc_ref[0]
        xcb, r, i, sp, a, m = _rg_gates(xc_v, wa_ref, ba_ref, wx_ref, bx_ref, lam_ref)
        dhs, dg = _gate_bwd(dgt_ref[0], hs_ref[0], g)
        dp_ref[1, 0] = dg.astype(BF16)

        ext_ref[0:rows, :] = a
        ext_ref[rows:, :] = acar_ref[...]
        acar_ref[...] = a[0:8, :]
        c_ref[...] = ext_ref[pl.ds(1, rows), :]
        l_ref[...] = dhs

        def step(r0, carry):
            lam_g = _group_scan(c_ref[pl.ds(r0, 8), :], l_ref[pl.ds(r0, 8), :], carry, reverse=True)
            l_ref[pl.ds(r0, 8), :] = lam_g
            return lam_g[0:1, :]

        lcar_ref[0:1, :] = _scan_rows(rows, step, lcar_ref[0:1, :], reverse=True)
        du = l_ref[...]
        da = du * hp_ref[0]
        dlog_a = da * a - (du * (i * xc_v)) * (a * a / m)
        dr = dlog_a * (-RG_C * sp)
        dsp = jnp.sum(dlog_a * (-RG_C * r), axis=0, keepdims=True)
        dlam_ref[...] += dsp * (-_sigmoid(-lam_ref[...]))
        dpa = dr * (r * (1.0 - r))
        dpx = (du * (m * xc_v)) * (i * (1.0 - i))
        dba_ref[...] += jnp.sum(dpa, axis=0, keepdims=True)
        dbx_ref[...] += jnp.sum(dpx, axis=0, keepdims=True)
        dpa, dpx = dpa.astype(BF16), dpx.astype(BF16)
        dwa_ref[0] += _dot_tn(xcb, dpa)
        dwx_ref[0] += _dot_tn(xcb, dpx)
        dxc = du * (m * i) + _dot_nt(dpa, wa_ref[0]) + _dot_nt(dpx, wx_ref[0])

        dcb_ref[...] += jnp.sum(dxc, axis=0, keepdims=True)
        ext_ref[0:rows, :] = dxc
        ext_ref[rows:, :] = dcar_ref[...]
        dcar_ref[...] = dxc[0:8, :]
        dxr = jnp.zeros_like(dxc)
        for k in range(CONV_WIDTH):
            tap = CONV_WIDTH - 1 - k
            ahead = dxc if k == 0 else ext_ref[pl.ds(k, rows), :]
            dxr = dxr + ahead * cw_ref[tap:tap + 1, :]
            dcw_ref[tap:tap + 1, :] += jnp.sum(xr * ahead, axis=0, keepdims=True)
        dp_ref[0, 0] = dxr.astype(BF16)

    vec = jax.ShapeDtypeStruct((1, RG_WIDTH), F32)
    gate = jax.ShapeDtypeStruct((RG_GROUPS, RG_COLS, RG_COLS), F32)
    return _call(
        body, name=name, grid=(RG_GROUPS, B, S // rows),
        in_specs=[sp_["proj"], sp_["act"], sp_["act"], sp_["act"], sp_["act"], sp_["taps"],
                  sp_["gate"], sp_["vec"], sp_["gate"], sp_["vec"], sp_["vec"]],
        out_specs=[sp_["proj"], sp_["taps"], sp_["vec"], sp_["gate"], sp_["vec"], sp_["gate"], sp_["vec"], sp_["vec"]],
        out_shape=[jax.ShapeDtypeStruct(proj.shape, BF16), jax.ShapeDtypeStruct((CONV_WIDTH, RG_WIDTH), F32), vec,
                   gate, vec, gate, vec, vec],
        scratch_shapes=[pltpu.VMEM((rows + 8, RG_COLS), F32), pltpu.VMEM((rows, RG_COLS), F32),
                        pltpu.VMEM((rows, RG_COLS), F32), pltpu.VMEM((8, RG_COLS), F32),
                        pltpu.VMEM((8, RG_COLS), F32), pltpu.VMEM((8, RG_COLS), F32)],
        semantics=("arbitrary", "arbitrary", "arbitrary"),
        args=[proj, xc, hs, hprev, dgated, conv_w, wa, ba, wx, bx, lam], ride=ride)


def block_diag_gates(w):
    per = RG_COLS // RG_BLOCK
    w4 = w.reshape(RG_GROUPS, per, RG_BLOCK, RG_BLOCK)
    return jnp.einsum("dipq,ij->dipjq", w4, jnp.eye(per, dtype=w.dtype)).reshape(RG_GROUPS, RG_COLS, RG_COLS)


def block_diag_gates_t(dw):
    per = RG_COLS // RG_BLOCK
    dw6 = dw.reshape(RG_GROUPS, per, RG_BLOCK, per, RG_BLOCK)
    return jnp.stack([dw6[:, i, :, i, :] for i in range(per)], axis=1).reshape(RG_BLOCKS, RG_BLOCK, RG_BLOCK)


def adamw(w, parts, m, v, name, layer=0, prev=None, part_row0=0, row_tile=ROW_TILE):
    L, R, C = w.shape
    n_parts = parts.shape[0]
    br = row_tile if R % row_tile == 0 else R

    def body(w_ref, p_ref, m_ref, v_ref, *refs):
        g_ref, d_ref, nm_ref, nv_ref = refs[-4:]
        g = p_ref[0].astype(F32)
        for k in range(1, n_parts):
            g = g + p_ref[k].astype(F32)
        nm = ADAM_B1 * m_ref[0] + (1.0 - ADAM_B1) * g
        nv = ADAM_B2 * v_ref[0] + (1.0 - ADAM_B2) * (g * g)
        m_hat = nm / (1.0 - ADAM_B1 ** ADAM_STEP)
        v_hat = nv / (1.0 - ADAM_B2 ** ADAM_STEP)
        g_ref[0] = g
        d_ref[0] = -ADAM_LR * (m_hat / (jnp.sqrt(v_hat) + ADAM_EPS) + ADAM_WD * w_ref[0])
        nm_ref[0] = nm
        nv_ref[0] = nv

    slab = pl.BlockSpec((1, br, C), lambda i: (layer, i, 0))
    out = jax.ShapeDtypeStruct((L, R, C), F32)
    carried = [] if prev is None else list(prev)
    return _call(
        body, name=name, grid=(R // br,),
        in_specs=[slab, pl.BlockSpec((n_parts, br, C), lambda i: (0, part_row0 // br + i, 0)), slab, slab]
        + [pl.BlockSpec(memory_space=pl.ANY)] * len(carried),
        out_specs=[slab] * 4, out_shape=[out] * 4, semantics=("parallel",), args=[w, parts, m, v] + carried,
        aliases={4 + k: k for k in range(len(carried))})


def _seq(a, B):
    return a.reshape(a.shape[:-2] + (B, a.shape[-2] // B, a.shape[-1]))


def _flat(a):
    return a.reshape(a.shape[:-3] + (a.shape[-3] * a.shape[-2], a.shape[-1]))


def _tiles(w, which, **default):
    return dict(default, **w.get("tiles", {}).get(which, {}))


def mixer_a_fwd(h, w, B, tag, rides):
    proj = matmul(h, w["w_in"], mode="nn", out_dtype=F32, name=f"{tag}_proj", out_slabs=4,
                  ride=rides.pop(f"{tag}_proj", None), **_tiles(w, "proj"))
    o, gated = attn_a_fwd(_seq(proj, B), w["bias"], f"{tag}_attn", ride=rides.pop(f"{tag}_attn", None))
    return _flat(gated), dict(proj=proj, o=o)


def mixer_a_bwd(dgated, w, saved, B, tag, rides):
    dproj, dbias = attn_a_bwd(_seq(saved["proj"], B), w["bias"], saved["o"], _seq(dgated, B), f"{tag}_attn_bwd",
                              ride=rides.pop(f"{tag}_attn_bwd", None))
    return _flat(dproj), dict(bias=dbias)


def mixer_b_fwd(h, w, B, tag, rides):
    proj = matmul(h, w["w_in"], mode="nn", out_dtype=F32, name=f"{tag}_proj", out_slabs=2, bn=RG_COLS,
                  ride=rides.pop(f"{tag}_proj", None))
    xc, hs, hprev, gated = rglru_fwd(_seq(proj, B), w["conv_w"], w["conv_b"], w["wa"], w["ba"], w["wx"], w["bx"],
                                     w["lam"], f"{tag}_rglru", ride=rides.pop(f"{tag}_rglru", None))
    return _flat(gated), dict(proj=proj, xc=xc, hs=hs, hprev=hprev)


def mixer_b_bwd(dgated, w, saved, B, tag, rides):
    dproj, dcw, dcb, dwa, dba, dwx, dbx, dlam = rglru_bwd(
        _seq(saved["proj"], B), saved["xc"], saved["hs"], saved["hprev"], _seq(dgated, B),
        w["conv_w"], w["wa"], w["ba"], w["wx"], w["bx"], w["lam"], f"{tag}_rglru_bwd",
        ride=rides.pop(f"{tag}_rglru_bwd", None))
    return _flat(dproj), dict(conv_w=dcw, conv_b=dcb, wa=dwa, ba=dba, wx=dwx, bx=dbx, lam=dlam)


def mixer_c_fwd(h, w, B, tag, rides):
    proj = matmul(h, w["w_in"], mode="nn", out_dtype=F32, name=f"{tag}_proj", out_slabs=4,
                  ride=rides.pop(f"{tag}_proj", None), **_tiles(w, "proj"))
    f_logit = matmul(h, w["w_f"], mode="nn", out_dtype=F32, name=f"{tag}_fproj")
    cum = fox_cum_fwd(_seq(f_logit, B), w["f_bias"], f"{tag}_cum")
    cum_row = cum[:, :, :HEADS].transpose(0, 2, 1)
    o, gated = fox_fwd(_seq(proj, B), cum, cum_row, f"{tag}_attn", ride=rides.pop(f"{tag}_attn", None))
    return _flat(gated), dict(proj=proj, o=o, f_logit=f_logit, cum=cum, cum_row=cum_row)


def mixer_c_bwd(dgated, w, saved, B, tag, rides):
    dproj, dck = fox_bwd(_seq(saved["proj"], B), saved["cum"], saved["cum_row"], saved["o"], _seq(dgated, B),
                         f"{tag}_attn_bwd", ride=rides.pop(f"{tag}_attn_bwd", None))
    S = dck.shape[-1]
    dcum = jnp.pad(-dck.reshape(B, HEADS, S).transpose(0, 2, 1), ((0, 0), (0, 0), (0, HEAD_DIM - HEADS)))
    df, dfb = fox_cum_bwd(_seq(saved["f_logit"], B), w["f_bias"], dcum, f"{tag}_cum_bwd")
    return _flat(dproj), dict(f_bias=dfb, df=_flat(df).astype(BF16))


MIXERS = {"a": (mixer_a_fwd, mixer_a_bwd), "b": (mixer_b_fwd, mixer_b_bwd), "c": (mixer_c_fwd, mixer_c_bwd)}
LAYER_KINDS = "abca"


def local_step(x, target, norm_pre, norm_post, get_layer, rides, on_grads):
    B, S, Dm = x.shape
    n_layers = len(LAYER_KINDS)
    xs = [x.reshape(B * S, Dm)]
    saved, layers = [], []
    h = prenorm_fwd(xs[0], norm_pre[0:1], "l0a_prenorm")
    for li, kind in enumerate(LAYER_KINDS):
        tag = f"l{li}{kind}"
        w = get_layer(li)
        gated, sv = MIXERS[kind][0](h, w, B, tag, rides)
        if callable(w["w_out"]):
            w["w_out"] = w["w_out"]()
        y = matmul(gated, w["w_out"], mode="nn", out_dtype=F32, name=f"{tag}_out", ride=rides.pop(f"{tag}_out", None))
        saved.append(dict(sv, h=h, gated=gated, y=y))
        layers.append(w)
        if li + 1 < n_layers:
            x_new, h = postnorm_prenorm_fwd(xs[-1], y, norm_post[li:li + 1], norm_pre[li + 1:li + 2],
                                            f"{tag}_postnorm")
            xs.append(x_new)
    loss, dx = postnorm_loss(xs[-1], y, norm_post[n_layers - 1:], target.reshape(B * S, Dm), "loss")

    for li in reversed(range(n_layers)):
        kind, w, sv = LAYER_KINDS[li], layers[li], saved[li]
        tag = f"l{li}{kind}"
        dy, dg_post = postnorm_bwd(sv["y"], norm_post[li:li + 1], dx, f"{tag}_postnorm_bwd")
        on_grads(li, "norm_post", dg_post)
        on_grads(li, "w_out", matmul(sv["gated"], dy, mode="tn", out_dtype=BF16, name=f"{tag}_dwout",
                                     ride=rides.pop(f"{tag}_dwout", None)))
        dgated = matmul(dy, w["w_out"], mode="nt", out_dtype=F32, name=f"{tag}_dgated",
                        ride=rides.pop(f"{tag}_dgated", None))
        dproj, gw = MIXERS[kind][1](dgated, w, sv, B, tag, rides)
        df = gw.pop("df", None)
        for name, value in gw.items():
            on_grads(li, name, value)
        parts = w.get("dwin_parts", [(0, 1, 1)])
        for i, m_part in enumerate(parts):
            suffix = f"_{i}" if len(parts) > 1 else ""
            on_grads(li, "w_in" + suffix,
                     matmul(sv["h"], dproj, mode="tn", out_dtype=BF16, name=f"{tag}_dwin{suffix}", m_part=m_part,
                            out_slabs=w["grad_slabs"], ride=rides.pop(f"{tag}_dwin{suffix}", None),
                            **_tiles(w, "dwin")))
        if df is not None:
            on_grads(li, "w_f", matmul(sv["h"], df, mode="tn", out_dtype=BF16, name=f"{tag}_dwf"))
        dhs = [matmul(dproj, w["w_in"], mode="nt", out_dtype=F32, name=f"{tag}_dh",
                      ride=rides.pop(f"{tag}_dh", None), **_tiles(w, "dh"))]
        if df is not None:
            dhs.append(matmul(df, w["w_f"], mode="nt", out_dtype=F32, name=f"{tag}_dhf"))
        dx, dg_pre = prenorm_bwd(xs[li], norm_pre[li:li + 1], dhs, dx, f"{tag}_prenorm_bwd")
        on_grads(li, "norm_pre", dg_pre)
    assert not rides, list(rides)
    return loss, dx.reshape(B, S, Dm)


WEIGHTS = ("norm_pre", "norm_post", "a_w_in", "a_rel_bias", "a_w_out", "b_w_in", "b_conv_w", "b_conv_b",
           "b_gate_a_w", "b_gate_a_b", "b_gate_x_w", "b_gate_x_b", "b_lambda", "b_w_out", "c_w_in", "c_f_bias",
           "c_w_out")
C_SHARD = (4 * D_MODEL + HEADS) // N_DEV


def _rows(gathered):
    return gathered.reshape(gathered.shape[0] * gathered.shape[1], gathered.shape[2])


def layer_a(w_in, w_out, rel_bias):
    return dict(w_in=w_in, w_out=w_out if callable(w_out) else _rows(w_out), bias=band_bias(rel_bias),
                grad_slabs=N_DEV)


def layer_b(w_in, w_out, conv_w, small):
    return dict(
        w_in=w_in, w_out=_rows(w_out), grad_slabs=N_DEV,
        conv_w=conv_w.transpose(1, 0, 2).reshape(CONV_WIDTH, RG_WIDTH),
        conv_b=small["b_conv_b"], lam=small["b_lambda"],
        wa=block_diag_gates(small["b_gate_a_w"][0]).astype(BF16), ba=small["b_gate_a_b"].reshape(1, RG_WIDTH),
        wx=block_diag_gates(small["b_gate_x_w"][0]).astype(BF16), bx=small["b_gate_x_b"].reshape(1, RG_WIDTH))


def layer_c(w_in, w_out, small):
    full = w_in.transpose(1, 0, 2).reshape(D_MODEL, N_DEV * C_SHARD)
    return dict(w_in=full[:, :4 * D_MODEL], w_f=jnp.pad(full[:, 4 * D_MODEL:], ((0, 0), (0, HEAD_DIM - HEADS))),
                w_out=_rows(w_out), grad_slabs=1,
                f_bias=jnp.pad(small["c_f_bias"], ((0, 0), (0, HEAD_DIM - HEADS))))


def c_w_in_blocks(dmain, df):
    full = jnp.concatenate([dmain, df[:, :HEADS].astype(dmain.dtype)], axis=1)
    return full.reshape(D_MODEL, N_DEV, C_SHARD).transpose(1, 0, 2)


def _row_blocks(g):
    return g.reshape(N_DEV, g.shape[0] // N_DEV, g.shape[1])


PACK_LANES = 128
PACK_ALIGN = 8 * PACK_LANES


def pack(parts):
    flat = []
    for p in parts:
        n = p.size
        flat.append(jnp.pad(p.reshape(n), (0, -n % PACK_ALIGN)).reshape(-1, PACK_LANES))
    rows = sum(f.shape[0] for f in flat)
    flat.append(jnp.zeros((-rows % ROW_TILE, PACK_LANES), F32))
    return jnp.concatenate(flat, axis=0)


def unpack(packed, shapes):
    out, row = [], 0
    for shape in shapes:
        n = 1
        for s in shape:
            n *= s
        n_rows = (n + PACK_ALIGN - 1) // PACK_ALIGN * 8
        out.append(packed[row:row + n_rows].reshape(-1)[:n].reshape(shape))
        row += n_rows
    return out


LATE = (("a_rel_bias", slice(0, 1)), ("norm_pre", slice(0, 2)), ("norm_post", slice(0, 1)))
EARLY = (("a_rel_bias", slice(1, 2)), ("norm_pre", slice(2, 4)), ("norm_post", slice(1, 4)),
         ("b_conv_b", slice(None)), ("b_gate_a_w", slice(None)), ("b_gate_a_b", slice(None)),
         ("b_gate_x_w", slice(None)), ("b_gate_x_b", slice(None)), ("b_lambda", slice(None)),
         ("c_f_bias", slice(None)))


def _pieces(tree, pieces):
    return [tree[name][sl] for name, sl in pieces]


def kernel(x, norm_pre, norm_post, a_w_in, a_rel_bias, a_w_out, b_w_in, b_conv_w, b_conv_b, b_gate_a_w, b_gate_a_b, b_gate_x_w, b_gate_x_b, b_lambda, b_w_out, c_w_in, c_f_bias, c_w_out, loss_target, m_norm_pre, m_norm_post, m_a_w_in, m_a_rel_bias, m_a_w_out, m_b_w_in, m_b_conv_w, m_b_conv_b, m_b_gate_a_w, m_b_gate_a_b, m_b_gate_x_w, m_b_gate_x_b, m_b_lambda, m_b_w_out, m_c_w_in, m_c_f_bias, m_c_w_out, v_norm_pre, v_norm_post, v_a_w_in, v_a_rel_bias, v_a_w_out, v_b_w_in, v_b_conv_w, v_b_conv_b, v_b_gate_a_w, v_b_gate_a_b, v_b_gate_x_w, v_b_gate_x_b, v_b_lambda, v_b_w_out, v_c_w_in, v_c_f_bias, v_c_w_out):
    args = dict(locals())
    w = {n: args[n] for n in WEIGHTS}
    m = {n: args["m_" + n] for n in WEIGHTS}
    v = {n: args["v_" + n] for n in WEIGHTS}

    a_in, a_out = a_w_in.astype(BF16), a_w_out.astype(BF16)
    gather_a0 = Ride([a_in[0]], scatter=False, via_sibling=True)
    in_l0_proj = Ride([b_w_in[0].astype(BF16), b_conv_w[0], a_out[0]], scatter=False, via_sibling=True)
    in_l0_attn = Ride([c_w_in[0].astype(BF16), b_w_out[0].astype(BF16)], scatter=False, via_sibling=True)
    in_l1_proj = Ride([c_w_out[0].astype(BF16)], scatter=False)
    in_l2_proj = Ride([a_out[1]], scatter=False)
    in_l2_attn = Ride([a_in[1]], scatter=False, via_sibling=True)
    exchange(gather_a0, "gather_l0")
    rides = {"l0a_proj": in_l0_proj, "l0a_attn": in_l0_attn, "l1b_proj": in_l1_proj, "l2c_proj": in_l2_proj,
             "l2c_attn": in_l2_attn}

    def get_layer(li):
        if li == 0:
            return dict(layer_a(gather_a0.out[0], lambda: _rows(in_l0_proj.out[2]), a_rel_bias[0]),
                        dwin_parts=[(0, 1, 4), (1, 1, 4), (2, 2, 4)])
        if li == 1:
            return layer_b(in_l0_proj.out[0], in_l0_attn.out[1], in_l0_proj.out[1], w)
        if li == 2:
            return dict(layer_c(in_l0_attn.out[0], in_l1_proj.out[0], w), tiles=dict(proj=dict(bn=2048)))
        return layer_a(in_l2_attn.out[0], in_l2_proj.out[0], a_rel_bias[1])

    grads = [dict() for _ in LAYER_KINDS]
    scatters = {}

    def rel_bias_grad(j, dbias):
        return jax.vjp(band_bias, a_rel_bias[j])[1](dbias)[0][None]

    def early_partial():
        gb, gc = grads[1], grads[2]
        tree = dict(
            a_rel_bias=jnp.concatenate([jnp.zeros((1, HEADS, N_REL), F32), rel_bias_grad(1, grads[3]["bias"])]),
            norm_pre=jnp.concatenate([jnp.zeros((2, D_MODEL), F32)] + [grads[li]["norm_pre"] for li in (2, 3)]),
            norm_post=jnp.concatenate([jnp.zeros((1, D_MODEL), F32)] + [grads[li]["norm_post"] for li in (1, 2, 3)]),
            b_conv_b=gb["conv_b"], b_lambda=gb["lam"],
            b_gate_a_w=block_diag_gates_t(gb["wa"])[None], b_gate_a_b=gb["ba"].reshape(1, RG_BLOCKS, RG_BLOCK),
            b_gate_x_w=block_diag_gates_t(gb["wx"])[None], b_gate_x_b=gb["bx"].reshape(1, RG_BLOCKS, RG_BLOCK),
            c_f_bias=gc["f_bias"][:, :HEADS])
        return pack(_pieces(tree, EARLY))

    def send(key, host, blocks, scatter=True, via_sibling=False):
        ride = rides.setdefault(host, Ride([], scatter, via_sibling))
        assert (ride.scatter, ride.via_sibling) == (scatter, via_sibling)
        scatters[key] = (ride, len(ride.arrs))
        ride.arrs.append(blocks)

    def on_grads(li, name, value):
        g = grads[li]
        g[name] = value
        if (li, name) == (3, "w_out"):
            send("a1_out", "l3a_attn_bwd", _row_blocks(value))
        elif (li, name) == (3, "w_in"):
            send("a1_in", "l2c_attn_bwd", value)
        elif (li, name) == (2, "w_out"):
            send("c_out", "l2c_attn_bwd", _row_blocks(value))
        elif (li, name) == (2, "w_f"):
            blocks = c_w_in_blocks(g["w_in"], value)
            send("c_in_0", "l2c_dh", blocks[:, :D_MODEL // 2])
            send("c_in_1", "l1b_rglru_bwd", blocks[:, D_MODEL // 2:])
        elif (li, name) == (1, "w_out"):
            send("b_out", "l1b_dh", _row_blocks(value))
        elif (li, name) == (1, "w_in"):
            send("b_in", "l0a_attn_bwd", value)
            send("b_conv", "l0a_attn_bwd",
                 g["conv_w"].reshape(CONV_WIDTH, N_DEV, RG_WIDTH // N_DEV).transpose(1, 0, 2))
        elif (li, name) == (1, "lam"):
            send("early", "l1b_dwin", early_partial(), scatter=False, via_sibling=True)
        elif (li, name) == (0, "w_out"):
            send("a0_out", "l0a_attn_bwd", _row_blocks(value))
        elif (li, name) == (0, "w_in_0"):
            send("a0_in_0", "l0a_dwin_1", value)
        elif (li, name) == (0, "w_in_1"):
            send("a0_in_1", "l0a_dwin_2", value)
        elif (li, name) == (0, "w_in_2"):
            send("a0_in_2", "l0a_dh", value)

    loss, grad_x = local_step(x, loss_target, norm_pre, norm_post, get_layer, rides, on_grads)
    late_tree = dict(a_rel_bias=rel_bias_grad(0, grads[0]["bias"]), norm_post=grads[0]["norm_post"],
                     norm_pre=jnp.concatenate([grads[0]["norm_pre"], grads[1]["norm_pre"]]))
    late_parts = exchange(Ride([pack([late_tree[n] for n, _ in LATE])], scatter=False), "gather_late_grads")[0]

    def sharded(name, slab_parts):
        shape = w[name].shape
        slabs = (len(slab_parts), shape[0] * shape[1] // len(slab_parts), shape[2])
        outs = None
        for j, (parts, row0) in enumerate(slab_parts):
            outs = adamw(w[name].reshape(slabs), parts, m[name].reshape(slabs), v[name].reshape(slabs),
                         f"adamw_{name}_{j}", layer=j, prev=outs, part_row0=row0)
        return [o.reshape(shape) for o in outs]

    def received(key):
        ride, position = scatters[key]
        return ride.out[position]

    res = dict(
        a_w_in=sharded("a_w_in", [(received("a0_in_0"), 0), (received("a0_in_1"), 0), (received("a0_in_2"), 0),
                                  (received("a0_in_2"), D_MODEL // 4)]
                       + [(received("a1_in"), q * D_MODEL // 4) for q in range(4)]),
        a_w_out=sharded("a_w_out", [(received("a0_out"), 0), (received("a1_out"), 0)]),
        b_w_in=sharded("b_w_in", [(received("b_in"), 0)]),
        b_w_out=sharded("b_w_out", [(received("b_out"), 0)]),
        b_conv_w=sharded("b_conv_w", [(received("b_conv"), 0)]),
        c_w_in=sharded("c_w_in", [(received("c_in_0"), 0), (received("c_in_1"), 0)]),
        c_w_out=sharded("c_w_out", [(received("c_out"), 0)]))

    packed = {}
    for label, pieces, parts in (("early", EARLY, received("early")), ("late", LATE, late_parts)):
        outs = adamw(pack(_pieces(w, pieces))[None], parts, pack(_pieces(m, pieces))[None],
                     pack(_pieces(v, pieces))[None], f"adamw_replicated_{label}")
        shapes = [w[n][sl].shape for n, sl in pieces]
        packed[label] = [dict(zip([n for n, _ in pieces], unpack(o[0], shapes))) for o in outs]
    for n in ("b_conv_b", "b_gate_a_w", "b_gate_a_b", "b_gate_x_w", "b_gate_x_b", "b_lambda", "c_f_bias"):
        res[n] = [packed["early"][k][n] for k in range(4)]
    for n in ("a_rel_bias", "norm_pre", "norm_post"):
        res[n] = [jnp.concatenate([packed["late"][k][n], packed["early"][k][n]]) for k in range(4)]

    total = lax.psum(loss[0, 0], ("x", "y", "c"))
    return (total, grad_x, *[res[n][0] for n in WEIGHTS], *[res[n][1] for n in WEIGHTS],
            *[res[n][2] for n in WEIGHTS], *[res[n][3] for n in WEIGHTS])
```

```python
import functools

import jax
import jax.numpy as jnp
from jax import lax
from jax.experimental import pallas as pl
from jax.experimental.pallas import tpu as pltpu

F32 = jnp.float32
BF16 = jnp.bfloat16

N_DEV = 8
D_MODEL = 2048
HEADS = 16
HEAD_DIM = 128
CHUNK = 64
LEFT_CHUNKS = 8
REL_CLIP = 256
N_REL = 2 * REL_CLIP + 1
TQ = 256
A_PAD = LEFT_CHUNKS * CHUNK
A_KW = A_PAD + TQ
RG_WIDTH = 2560
RG_BLOCKS = 16
RG_BLOCK = 160
RG_COLS = 640
RG_GROUPS = RG_WIDTH // RG_COLS
RG_C = 8.0
CONV_WIDTH = 4
RMS_EPS = 1e-6
NEG_INF = -1e30
ADAM_LR = 0.001
ADAM_B1 = 0.9
ADAM_B2 = 0.999
ADAM_EPS = 1e-08
ADAM_WD = 0.01
ADAM_STEP = 10
VMEM_LIMIT = 56 * 1024 * 1024
MESH = pl.DeviceIdType.MESH


def _params(sem, vmem=VMEM_LIMIT):
    return pltpu.CompilerParams(dimension_semantics=sem, vmem_limit_bytes=vmem)


def _sigmoid(x):
    return 1.0 / (1.0 + jnp.exp(-x))


def _log1p(y):
    u = 1.0 + y
    return jnp.where(u == 1.0, y, jnp.log(u) * (y / jnp.where(u == 1.0, 1.0, u - 1.0)))


def _softplus(x):
    return jnp.maximum(x, 0.0) + _log1p(jnp.exp(-jnp.abs(x)))


def _dot(a, b, dims):
    return lax.dot_general(a, b, (dims, ((), ())), preferred_element_type=F32)


def _dot_nn(a, b):
    return _dot(a, b, ((1,), (0,)))


def _dot_nt(a, b):
    return _dot(a, b, ((1,), (1,)))


def _dot_tn(a, b):
    return _dot(a, b, ((0,), (0,)))


def _peers():
    x, y, c = lax.axis_index("x"), lax.axis_index("y"), lax.axis_index("c")
    me = 4 * x + 2 * y + c
    peers = []
    for k in range(1, N_DEV):
        px = 1 - x if k & 4 else x
        py = 1 - y if k & 2 else y
        pc = 1 - c if k & 1 else c
        peers.append(((px, py, pc), 4 * px + 2 * py + pc))
    return me, peers


class Ride:
    def __init__(self, arrs, scatter, via_sibling=False):
        assert not (scatter and via_sibling)
        self.arrs, self.scatter, self.via_sibling, self.out = list(arrs), scatter, via_sibling, None

    def out_shapes(self):
        return [jax.ShapeDtypeStruct(a.shape if self.scatter else (N_DEV,) + a.shape, a.dtype) for a in self.arrs]

    def sem_shapes(self):
        n = len(self.arrs)
        return [pltpu.SemaphoreType.DMA((n, N_DEV - 1)), pltpu.SemaphoreType.DMA((n, N_DEV - 1)),
                pltpu.SemaphoreType.DMA((n,))]

    def _copies(self, ins, outs, sems, landing):
        send_sems, recv_sems, local_sems = sems
        me, peers = _peers()
        local, remote = [], []
        for a, (src, dst) in enumerate(zip(ins, outs)):
            local.append(pltpu.make_async_copy(src.at[me] if self.scatter else src, dst.at[me], local_sems.at[a]))
            for k, (peer, peer_idx) in enumerate(peers):
                remote.append(pltpu.make_async_remote_copy(
                    src_ref=src.at[peer_idx] if self.scatter else src, dst_ref=dst.at[peer_idx if landing else me],
                    send_sem=send_sems.at[a, k], recv_sem=recv_sems.at[a, k], device_id=peer, device_id_type=MESH))
        return local, remote

    def _direct(self, k):
        return not self.via_sibling or k == 0 or (k + 1) % 2 == 0

    def start(self, ins, outs, sems):
        local, remote = self._copies(ins, outs, sems, landing=False)
        n_peers = N_DEV - 1
        for cp in local + [cp for i, cp in enumerate(remote) if self._direct(i % n_peers)]:
            cp.start()

    def wait(self, ins, outs, sems):
        local, remote = self._copies(ins, outs, sems, landing=True)
        n_peers = N_DEV - 1
        for i, cp in enumerate(remote):
            if self._direct(i % n_peers):
                cp.wait()
        if self.via_sibling:
            send_sems, recv_sems, _ = sems
            me, peers = _peers()
            sibling = peers[0][0]
            passed = []
            for a, dst in enumerate(outs):
                for j in range(1, n_peers, 2):
                    came, lands = peers[j][1], peers[j + 1][1]
                    pltpu.make_async_remote_copy(
                        src_ref=dst.at[came], dst_ref=dst.at[came], send_sem=send_sems.at[a, j + 1],
                        recv_sem=recv_sems.at[a, j + 1], device_id=sibling, device_id_type=MESH).start()
                    passed.append(pltpu.make_async_remote_copy(
                        src_ref=dst.at[came], dst_ref=dst.at[lands], send_sem=send_sems.at[a, j + 1],
                        recv_sem=recv_sems.at[a, j + 1], device_id=sibling, device_id_type=MESH))
            for cp in passed:
                cp.wait()
        for cp in local:
            cp.wait()


def _call(body, *, name, grid, in_specs, out_specs, out_shape, args, scratch_shapes=(), semantics=None, ride=None,
          aliases=None):
    scratch_shapes = list(scratch_shapes)
    if ride is None:
        return pl.pallas_call(
            body, name=name, grid=grid, in_specs=in_specs, out_specs=out_specs, out_shape=out_shape,
            scratch_shapes=scratch_shapes, input_output_aliases=aliases or {},
            compiler_params=_params(semantics if grid else None))(*args)
    assert not aliases
    n_in, n_out, n_sc, n_r = len(in_specs), len(out_specs), len(scratch_shapes), len(ride.arrs)

    def riding(*refs):
        ins, r_ins = refs[:n_in], refs[n_in:n_in + n_r]
        outs, r_outs = refs[n_in + n_r:n_in + n_r + n_out], refs[n_in + n_r + n_out:n_in + 2 * n_r + n_out]
        rest = refs[n_in + 2 * n_r + n_out:]
        scratch, sems = rest[:n_sc], rest[n_sc:]
        first = last = None
        for axis, size in enumerate(grid):
            pid = pl.program_id(axis)
            first = (pid == 0) if first is None else first & (pid == 0)
            last = (pid == size - 1) if last is None else last & (pid == size - 1)
        if grid:
            pl.when(first)(lambda: ride.start(r_ins, r_outs, sems))
        else:
            ride.start(r_ins, r_outs, sems)
        body(*ins, *outs, *scratch)
        if grid:
            pl.when(last)(lambda: ride.wait(r_ins, r_outs, sems))
        else:
            ride.wait(r_ins, r_outs, sems)

    any_spec = pl.BlockSpec(memory_space=pl.ANY)
    res = pl.pallas_call(
        riding, name=name, grid=grid, in_specs=list(in_specs) + [any_spec] * n_r,
        out_specs=list(out_specs) + [any_spec] * n_r, out_shape=list(out_shape) + ride.out_shapes(),
        scratch_shapes=scratch_shapes + ride.sem_shapes(),
        compiler_params=_params(("arbitrary",) * len(grid) if grid else None))(*args, *ride.arrs)
    ride.out = list(res[n_out:])
    return list(res[:n_out])


def exchange(ride, name):
    _call(lambda: None, name=name, grid=(), in_specs=[], out_specs=[], out_shape=[], args=[], ride=ride)
    return ride.out


LANES = 128


def _fit(dims, want):
    dims = tuple(dims)
    if len(set(dims)) == 1 and dims[0] <= want:
        return dims[0]
    return max(t for t in range(LANES, want + 1, LANES) if all(d % t == 0 for d in dims))


def _cols(arr):
    return arr.shape[-1] * (arr.shape[0] if len(arr.shape) == 3 else 1)


def _tile_spec(shape, rblk, cblk, rc):
    if len(shape) == 2:
        return pl.BlockSpec((rblk, cblk), rc)
    per = shape[2] // cblk

    def index_map(*ids):
        r, c = rc(*ids)
        return (c // per, r, c % per)

    return pl.BlockSpec((1, rblk, cblk), index_map)


def matmul(a, b, *, mode, out_dtype, name, bm=1024, bn=1024, bk=2048, out_slabs=1, m_part=(0, 1, 1), ride=None):
    a_rows, a_cols, b_rows, b_cols = a.shape[-2], _cols(a), b.shape[-2], _cols(b)
    (K, M) = (a_rows, a_cols) if mode == "tn" else (a_cols, a_rows)
    N = b_rows if mode == "nt" else b_cols
    assert K == (b_cols if mode == "nt" else b_rows), (name, a.shape, b.shape)
    first_range, n_ranges, of_ranges = m_part
    row0, M = first_range * (M // of_ranges), n_ranges * (M // of_ranges)
    out_shape = (M, N) if out_slabs == 1 else (out_slabs, M, N // out_slabs)
    widths = dict(m=[M], n=[N, out_shape[-1]], k=[K])
    widths["m" if mode == "tn" else "k"].append(a.shape[-1])
    widths["k" if mode == "nt" else "n"].append(b.shape[-1])
    bm, bn, bk = _fit(widths["m"], bm), _fit(widths["n"], bn), _fit(widths["k"], bk)
    nk = K // bk
    dims = {"nn": ((1,), (0,)), "nt": ((1,), (1,)), "tn": ((0,), (0,))}[mode]

    def val(ref):
        return ref[0] if len(ref.shape) == 3 else ref[...]

    def put(ref, x):
        if len(ref.shape) == 3:
            ref[0] = x.astype(ref.dtype)
        else:
            ref[...] = x.astype(ref.dtype)

    def body(a_ref, b_ref, o_ref, *scratch):
        if nk == 1:
            put(o_ref, _dot(val(a_ref), val(b_ref), dims))
            return
        acc_ref, = scratch
        k = pl.program_id(2)

        @pl.when(k == 0)
        def _():
            acc_ref[...] = jnp.zeros_like(acc_ref)

        acc_ref[...] += _dot(val(a_ref), val(b_ref), dims)

        @pl.when(k == nk - 1)
        def _():
            put(o_ref, acc_ref[...])

    assert row0 % bm == 0
    m0 = row0 // bm
    if mode == "tn":
        a_spec = _tile_spec(a.shape, bk, bm, lambda j, i, k: (k, m0 + i))
    else:
        a_spec = _tile_spec(a.shape, bm, bk, lambda j, i, k: (m0 + i, k))
    if mode == "nt":
        b_spec = _tile_spec(b.shape, bn, bk, lambda j, i, k: (j, k))
    else:
        b_spec = _tile_spec(b.shape, bk, bn, lambda j, i, k: (k, j))
    return _call(
        body, name=name, grid=(N // bn, M // bm, nk), in_specs=[a_spec, b_spec],
        out_specs=[_tile_spec(out_shape, bm, bn, lambda j, i, k: (i, j))],
        out_shape=[jax.ShapeDtypeStruct(out_shape, out_dtype)],
        scratch_shapes=[] if nk == 1 else [pltpu.VMEM((bm, bn), F32)],
        semantics=("parallel", "parallel", "arbitrary"), args=[a, b], ride=ride)[0]


ROW_TILE = 256


def _rms_stats(z):
    r = lax.rsqrt(jnp.mean(z * z, axis=-1, keepdims=True) + RMS_EPS)
    return r, z * r


def _rms_bwd(n, r, g, dout):
    dn = dout * g
    return r * (dn - n * jnp.mean(dn * n, axis=-1, keepdims=True))


def _row_spec(T, Dm):
    bt = min(ROW_TILE, T)
    return bt, pl.BlockSpec((bt, Dm), lambda i: (i, 0)), pl.BlockSpec((1, Dm), lambda i: (0, 0))


def prenorm_fwd(x, g, name):
    T, Dm = x.shape
    bt, row, vec = _row_spec(T, Dm)

    def body(x_ref, g_ref, h_ref):
        _, n = _rms_stats(x_ref[...])
        h_ref[...] = (n * g_ref[...]).astype(BF16)

    return pl.pallas_call(
        body, name=name, grid=(T // bt,), in_specs=[row, vec], out_specs=row,
        out_shape=jax.ShapeDtypeStruct((T, Dm), BF16), compiler_params=_params(("parallel",)),
    )(x, g)


def postnorm_prenorm_fwd(x, y, g_post, g_next, name):
    T, Dm = x.shape
    bt, row, vec = _row_spec(T, Dm)

    def body(x_ref, y_ref, gp_ref, gn_ref, o_ref, h_ref):
        _, n = _rms_stats(y_ref[...])
        x_new = x_ref[...] + n * gp_ref[...]
        o_ref[...] = x_new
        _, n_new = _rms_stats(x_new)
        h_ref[...] = (n_new * gn_ref[...]).astype(BF16)

    return pl.pallas_call(
        body, name=name, grid=(T // bt,), in_specs=[row, row, vec, vec], out_specs=[row, row],
        out_shape=[jax.ShapeDtypeStruct((T, Dm), F32), jax.ShapeDtypeStruct((T, Dm), BF16)],
        compiler_params=_params(("parallel",)),
    )(x, y, g_post, g_next)


def postnorm_loss(x, y, g, target, name):
    T, Dm = x.shape
    bt, row, vec = _row_spec(T, Dm)

    def body(x_ref, y_ref, g_ref, t_ref, l_ref, d_ref):
        @pl.when(pl.program_id(0) == 0)
        def _():
            l_ref[...] = jnp.zeros_like(l_ref)

        _, n = _rms_stats(y_ref[...])
        err = (x_ref[...] + n * g_ref[...]) - t_ref[...]
        per_tok = jnp.mean(err * err, axis=-1, keepdims=True)
        l_ref[...] += 0.5 * jnp.sum(per_tok, axis=0, keepdims=True)
        d_ref[...] = err * (1.0 / Dm)

    return pl.pallas_call(
        body, name=name, grid=(T // bt,), in_specs=[row, row, vec, row],
        out_specs=[pl.BlockSpec((1, 1), lambda i: (0, 0)), row],
        out_shape=[jax.ShapeDtypeStruct((1, 1), F32), jax.ShapeDtypeStruct((T, Dm), F32)],
        compiler_params=_params(("arbitrary",)),
    )(x, y, g, target)


def postnorm_bwd(y, g, dout, name):
    T, Dm = y.shape
    bt, row, vec = _row_spec(T, Dm)

    def body(y_ref, g_ref, d_ref, dy_ref, dg_ref):
        @pl.when(pl.program_id(0) == 0)
        def _():
            dg_ref[...] = jnp.zeros_like(dg_ref)

        r, n = _rms_stats(y_ref[...])
        dout_v = d_ref[...]
        dg_ref[...] += jnp.sum(dout_v * n, axis=0, keepdims=True)
        dy_ref[...] = _rms_bwd(n, r, g_ref[...], dout_v).astype(BF16)

    return pl.pallas_call(
        body, name=name, grid=(T // bt,), in_specs=[row, vec, row], out_specs=[row, vec],
        out_shape=[jax.ShapeDtypeStruct((T, Dm), BF16), jax.ShapeDtypeStruct((1, Dm), F32)],
        compiler_params=_params(("arbitrary",)),
    )(y, g, dout)


def prenorm_bwd(x, g, dhs, dres, name):
    T, Dm = x.shape
    bt, row, vec = _row_spec(T, Dm)
    n_dh = len(dhs)

    def body(x_ref, g_ref, *refs):
        dh_refs, (dr_ref, dx_ref, dg_ref) = refs[:n_dh], refs[n_dh:]

        @pl.when(pl.program_id(0) == 0)
        def _():
            dg_ref[...] = jnp.zeros_like(dg_ref)

        r, n = _rms_stats(x_ref[...])
        dh_v = dh_refs[0][...]
        for extra in dh_refs[1:]:
            dh_v = dh_v + extra[...]
        dg_ref[...] += jnp.sum(dh_v * n, axis=0, keepdims=True)
        dx_ref[...] = dr_ref[...] + _rms_bwd(n, r, g_ref[...], dh_v)

    return pl.pallas_call(
        body, name=name, grid=(T // bt,), in_specs=[row, vec] + [row] * (n_dh + 1), out_specs=[row, vec],
        out_shape=[jax.ShapeDtypeStruct((T, Dm), F32), jax.ShapeDtypeStruct((1, Dm), F32)],
        compiler_params=_params(("arbitrary",)),
    )(x, g, *dhs, dres)


def _silu(g):
    return g * _sigmoid(g)


def _gate_bwd(dgated, core, g):
    sg = _sigmoid(g)
    return dgated * (g * sg), dgated * core * (sg * (1.0 + g * (1.0 - sg)))


def _softmax_rows(s):
    e = jnp.exp(s - jnp.max(s, axis=-1, keepdims=True))
    return e * (1.0 / jnp.sum(e, axis=-1, keepdims=True))


def _band_scores(qk, bias, r0):
    s = qk * (HEAD_DIM ** -0.5) + bias
    j = lax.broadcasted_iota(jnp.int32, (1, s.shape[1]), 1)
    return jnp.where(j >= A_PAD - r0, s, NEG_INF)


def _fill_padded_kv(p_ref, kp_ref, vp_ref):
    zeros = jnp.zeros((A_PAD, HEAD_DIM), BF16)
    kp_ref[0:A_PAD, :] = zeros
    vp_ref[0:A_PAD, :] = zeros
    kp_ref[A_PAD:, :] = p_ref[1, 0].astype(BF16)
    vp_ref[A_PAD:, :] = p_ref[2, 0].astype(BF16)


def _head_specs(S, order):
    def idx(fn):
        return lambda *ids: fn(**dict(zip(order, ids)))

    return (pl.BlockSpec((4, 1, S, HEAD_DIM), idx(lambda b, h, t: (0, b, 0, h))),
            pl.BlockSpec((1, TQ, HEAD_DIM), idx(lambda b, h, t: (b, t, h))))


def attn_a_fwd(proj, bias, name, ride=None):
    _, B, S, W = proj.shape
    nt = S // TQ

    def body(p_ref, b_ref, o_ref, gt_ref, kp_ref, vp_ref):
        t = pl.program_id(2)

        @pl.when(t == 0)
        def _():
            _fill_padded_kv(p_ref, kp_ref, vp_ref)

        r0 = pl.multiple_of(t * TQ, TQ)
        q = p_ref[0, 0, pl.ds(r0, TQ), :].astype(BF16)
        g = p_ref[3, 0, pl.ds(r0, TQ), :]
        p = _softmax_rows(_band_scores(_dot_nt(q, kp_ref[pl.ds(r0, A_KW), :]), b_ref[0], r0))
        o = _dot_nn(p.astype(BF16), vp_ref[pl.ds(r0, A_KW), :])
        o_ref[0] = o
        gt_ref[0] = (o * _silu(g)).astype(BF16)

    seq, tile = _head_specs(S, "bht")
    return _call(
        body, name=name, grid=(B, HEADS, nt),
        in_specs=[seq, pl.BlockSpec((1, TQ, A_KW), lambda b, h, t: (h, 0, 0))], out_specs=[tile, tile],
        out_shape=[jax.ShapeDtypeStruct((B, S, W), F32), jax.ShapeDtypeStruct((B, S, W), BF16)],
        scratch_shapes=[pltpu.VMEM((A_PAD + S, HEAD_DIM), BF16), pltpu.VMEM((A_PAD + S, HEAD_DIM), BF16)],
        semantics=("parallel", "parallel", "arbitrary"), args=[proj, bias], ride=ride)


def attn_a_bwd(proj, bias, o, dgated, name, ride=None):
    _, B, S, W = proj.shape
    nt = S // TQ

    def body(p_ref, b_ref, o_ref, dgt_ref, dp_ref, db_ref, kp_ref, vp_ref, dk_ref, dv_ref):
        b_, t = pl.program_id(1), pl.program_id(2)

        @pl.when(t == 0)
        def _():
            _fill_padded_kv(p_ref, kp_ref, vp_ref)
            dk_ref[...] = jnp.zeros_like(dk_ref)
            dv_ref[...] = jnp.zeros_like(dv_ref)

        @pl.when((t == 0) & (b_ == 0))
        def _():
            db_ref[...] = jnp.zeros_like(db_ref)

        r0 = pl.multiple_of(t * TQ, TQ)
        rows, win = pl.ds(r0, TQ), pl.ds(r0, A_KW)
        q = p_ref[0, 0, rows, :].astype(BF16)
        g = p_ref[3, 0, rows, :]
        kw, vw = kp_ref[win, :], vp_ref[win, :]
        p = _softmax_rows(_band_scores(_dot_nt(q, kw), b_ref[0], r0))
        do, dg = _gate_bwd(dgt_ref[0], o_ref[0], g)
        do = do.astype(BF16)
        dv_ref[win, :] += _dot_tn(p.astype(BF16), do)
        dpr = _dot_nt(do, vw)
        ds = p * (dpr - jnp.sum(p * dpr, axis=-1, keepdims=True))
        db_ref[0] += ds
        ds = (ds * (HEAD_DIM ** -0.5)).astype(BF16)
        dk_ref[win, :] += _dot_tn(ds, q)
        dp_ref[0, 0, rows, :] = _dot_nn(ds, kw).astype(BF16)
        dp_ref[3, 0, rows, :] = dg.astype(BF16)

        @pl.when(t == nt - 1)
        def _():
            dp_ref[1, 0] = dk_ref[A_PAD:, :].astype(BF16)
            dp_ref[2, 0] = dv_ref[A_PAD:, :].astype(BF16)

    seq, tile = _head_specs(S, "hbt")
    bias_spec = pl.BlockSpec((1, TQ, A_KW), lambda h, b, t: (h, 0, 0))
    return _call(
        body, name=name, grid=(HEADS, B, nt), in_specs=[seq, bias_spec, tile, tile], out_specs=[seq, bias_spec],
        out_shape=[jax.ShapeDtypeStruct(proj.shape, BF16), jax.ShapeDtypeStruct(bias.shape, F32)],
        scratch_shapes=[pltpu.VMEM((A_PAD + S, HEAD_DIM), BF16), pltpu.VMEM((A_PAD + S, HEAD_DIM), BF16),
                        pltpu.VMEM((A_PAD + S, HEAD_DIM), F32), pltpu.VMEM((A_PAD + S, HEAD_DIM), F32)],
        semantics=("arbitrary", "arbitrary", "arbitrary"), args=[proj, bias, o, dgated], ride=ride)


def band_bias(rel_bias):
    length = TQ + A_KW - 1
    first = REL_CLIP + 1 - TQ
    gen = jnp.concatenate([rel_bias[:, first:],
                           jnp.broadcast_to(rel_bias[:, 2 * REL_CLIP:], (HEADS, length - (N_REL - first)))], axis=1)
    rev = jnp.concatenate([gen[:, ::-1], jnp.zeros((HEADS, 1), rel_bias.dtype)], axis=1)
    sheared = jnp.tile(rev, (1, TQ))[:, :TQ * length].reshape(HEADS, TQ, length)
    i = lax.broadcasted_iota(jnp.int32, (TQ, A_KW), 0)
    j = lax.broadcasted_iota(jnp.int32, (TQ, A_KW), 1)
    first_key = (i // CHUNK) * CHUNK
    in_band = (j >= first_key) & (j < first_key + (LEFT_CHUNKS + 1) * CHUNK)
    return jnp.where(in_band, sheared[:, :, TQ - 1:], NEG_INF)


def _group_scan(a, u, carry, reverse=False):
    row = lax.broadcasted_iota(jnp.int32, u.shape, 0)
    for k in (1, 2, 4):
        shift = 8 - k if reverse else k
        valid = (row < 8 - k) if reverse else (row >= k)
        u_sh = pltpu.roll(u, shift, 0)
        if a is None:
            u = jnp.where(valid, u + u_sh, u)
        else:
            a_sh = pltpu.roll(a, shift, 0)
            u = jnp.where(valid, a * u_sh + u, u)
            a = jnp.where(valid, a * a_sh, a)
    return (u + carry) if a is None else (a * carry + u)


SCAN_UNROLL = 4


def _scan_rows(n_rows, step, carry0, reverse=False):
    groups = n_rows // 8

    def loop(i, carry):
        gi = (groups - 1 - i) if reverse else i
        return step(pl.multiple_of(gi * 8, 8), carry)

    return lax.fori_loop(0, groups, loop, carry0, unroll=SCAN_UNROLL)


def fox_cum_fwd(f_logit, f_bias, name):
    B, S, L = f_logit.shape

    def body(f_ref, b_ref, c_ref):
        z = f_ref[0] + b_ref[...]
        c_ref[0] = jnp.minimum(z, 0.0) - _log1p(jnp.exp(-jnp.abs(z)))

        def step(r0, carry):
            h = _group_scan(None, c_ref[0, pl.ds(r0, 8), :], carry)
            c_ref[0, pl.ds(r0, 8), :] = h
            return h[7:8, :]

        _scan_rows(S, step, jnp.zeros((1, L), F32))

    return pl.pallas_call(
        body, name=name, grid=(B,),
        in_specs=[pl.BlockSpec((1, S, L), lambda b: (b, 0, 0)), pl.BlockSpec((1, L), lambda b: (0, 0))],
        out_specs=pl.BlockSpec((1, S, L), lambda b: (b, 0, 0)),
        out_shape=jax.ShapeDtypeStruct((B, S, L), F32), compiler_params=_params(("parallel",)),
    )(f_logit, f_bias)


def fox_cum_bwd(f_logit, f_bias, dcum, name):
    B, S, L = f_logit.shape

    def body(f_ref, b_ref, d_ref, df_ref, db_ref):
        @pl.when(pl.program_id(0) == 0)
        def _():
            db_ref[...] = jnp.zeros_like(db_ref)

        def step(r0, carry):
            h = _group_scan(None, d_ref[0, pl.ds(r0, 8), :], carry, reverse=True)
            df_ref[0, pl.ds(r0, 8), :] = h
            return h[0:1, :]

        _scan_rows(S, step, jnp.zeros((1, L), F32), reverse=True)
        df = df_ref[0] * _sigmoid(-(f_ref[0] + b_ref[...]))
        df_ref[0] = df
        db_ref[...] += jnp.sum(df, axis=0, keepdims=True)

    seq = pl.BlockSpec((1, S, L), lambda b: (b, 0, 0))
    vec = pl.BlockSpec((1, L), lambda b: (0, 0))
    return pl.pallas_call(
        body, name=name, grid=(B,), in_specs=[seq, vec, seq], out_specs=[seq, vec],
        out_shape=[jax.ShapeDtypeStruct((B, S, L), F32), jax.ShapeDtypeStruct((1, L), F32)],
        compiler_params=_params(("arbitrary",)),
    )(f_logit, f_bias, dcum)


def _head_row(cr, h):
    sub = lax.broadcasted_iota(jnp.int32, cr.shape, 0)
    return jnp.sum(jnp.where(sub == h, cr, 0.0), axis=0, keepdims=True)


def _fox_scores(qk, cc, ck, h, r0):
    lane = lax.broadcasted_iota(jnp.int32, cc.shape, 1)
    cq = jnp.sum(jnp.where(lane == h, cc, 0.0), axis=1, keepdims=True)
    s = qk * (HEAD_DIM ** -0.5) + (cq - ck)
    qpos = r0 + lax.broadcasted_iota(jnp.int32, (s.shape[0], 1), 0)
    kpos = lax.broadcasted_iota(jnp.int32, (1, s.shape[1]), 1)
    return jnp.where(kpos <= qpos, s, NEG_INF)


KEY_STEP = 256


def _by_causal_width(t, S, fn):
    per = KEY_STEP // TQ
    for c in range(S // KEY_STEP):
        pl.when(t // per == c)(functools.partial(fn, (c + 1) * KEY_STEP))


def fox_fwd(proj, cum_col, cum_row, name, ride=None):
    _, B, S, W = proj.shape
    nt = S // TQ

    def body(p_ref, cc_ref, cr_ref, o_ref, gt_ref, k_ref, v_ref):
        h, t = pl.program_id(1), pl.program_id(2)

        @pl.when(t == 0)
        def _():
            k_ref[...] = p_ref[1, 0].astype(BF16)
            v_ref[...] = p_ref[2, 0].astype(BF16)

        r0 = pl.multiple_of(t * TQ, TQ)
        q = p_ref[0, 0, pl.ds(r0, TQ), :].astype(BF16)
        g = p_ref[3, 0, pl.ds(r0, TQ), :]

        def tile_out(width):
            ck = _head_row(cr_ref[0, :, 0:width], h)
            p = _softmax_rows(_fox_scores(_dot_nt(q, k_ref[0:width, :]), cc_ref[0], ck, h, r0))
            o = _dot_nn(p.astype(BF16), v_ref[0:width, :])
            o_ref[0] = o
            gt_ref[0] = (o * _silu(g)).astype(BF16)

        _by_causal_width(t, S, tile_out)

    seq, tile = _head_specs(S, "bht")
    return _call(
        body, name=name, grid=(B, HEADS, nt),
        in_specs=[seq, pl.BlockSpec((1, TQ, cum_col.shape[2]), lambda b, h, t: (b, t, 0)),
                  pl.BlockSpec((1, HEADS, S), lambda b, h, t: (b, 0, 0))],
        out_specs=[tile, tile],
        out_shape=[jax.ShapeDtypeStruct((B, S, W), F32), jax.ShapeDtypeStruct((B, S, W), BF16)],
        scratch_shapes=[pltpu.VMEM((S, HEAD_DIM), BF16), pltpu.VMEM((S, HEAD_DIM), BF16)],
        semantics=("parallel", "parallel", "arbitrary"), args=[proj, cum_col, cum_row], ride=ride)


def fox_bwd(proj, cum_col, cum_row, o, dgated, name, ride=None):
    _, B, S, W = proj.shape
    nt = S // TQ

    def body(p_ref, cc_ref, cr_ref, o_ref, dgt_ref, dp_ref, dc_ref, k_ref, v_ref, dk_ref, dv_ref):
        h, t = pl.program_id(1), pl.program_id(2)

        @pl.when(t == 0)
        def _():
            k_ref[...] = p_ref[1, 0].astype(BF16)
            v_ref[...] = p_ref[2, 0].astype(BF16)
            dk_ref[...] = jnp.zeros_like(dk_ref)
            dv_ref[...] = jnp.zeros_like(dv_ref)
            dc_ref[...] = jnp.zeros_like(dc_ref)

        r0 = pl.multiple_of(t * TQ, TQ)
        rows = pl.ds(r0, TQ)
        q = p_ref[0, 0, rows, :].astype(BF16)
        g = p_ref[3, 0, rows, :]
        do, dg = _gate_bwd(dgt_ref[0], o_ref[0], g)
        do = do.astype(BF16)
        dp_ref[3, 0, rows, :] = dg.astype(BF16)

        def tile_grads(width):
            k, v = k_ref[0:width, :], v_ref[0:width, :]
            ck = _head_row(cr_ref[0, :, 0:width], h)
            p = _softmax_rows(_fox_scores(_dot_nt(q, k), cc_ref[0], ck, h, r0))
            dv_ref[0:width, :] += _dot_tn(p.astype(BF16), do)
            dpr = _dot_nt(do, v)
            ds = p * (dpr - jnp.sum(p * dpr, axis=-1, keepdims=True))
            dc_ref[0, 0, :, 0:width] += jnp.sum(ds, axis=0, keepdims=True)
            ds = (ds * (HEAD_DIM ** -0.5)).astype(BF16)
            dk_ref[0:width, :] += _dot_tn(ds, q)
            dp_ref[0, 0, rows, :] = _dot_nn(ds, k).astype(BF16)

        _by_causal_width(t, S, tile_grads)

        @pl.when(t == nt - 1)
        def _():
            dp_ref[1, 0] = dk_ref[...].astype(BF16)
            dp_ref[2, 0] = dv_ref[...].astype(BF16)

    seq, tile = _head_specs(S, "bht")
    return _call(
        body, name=name, grid=(B, HEADS, nt),
        in_specs=[seq, pl.BlockSpec((1, TQ, cum_col.shape[2]), lambda b, h, t: (b, t, 0)),
                  pl.BlockSpec((1, HEADS, S), lambda b, h, t: (b, 0, 0)), tile, tile],
        out_specs=[seq, pl.BlockSpec((1, 1, 1, S), lambda b, h, t: (b, h, 0, 0))],
        out_shape=[jax.ShapeDtypeStruct(proj.shape, BF16), jax.ShapeDtypeStruct((B, HEADS, 1, S), F32)],
        scratch_shapes=[pltpu.VMEM((S, HEAD_DIM), BF16), pltpu.VMEM((S, HEAD_DIM), BF16),
                        pltpu.VMEM((S, HEAD_DIM), F32), pltpu.VMEM((S, HEAD_DIM), F32)],
        semantics=("parallel", "parallel", "arbitrary"), args=[proj, cum_col, cum_row, o, dgated], ride=ride)


RG_ROWS = 512


def _rg_gates(xc, wa_ref, ba_ref, wx_ref, bx_ref, lam_ref):
    xcb = xc.astype(BF16)
    r = _sigmoid(_dot_nn(xcb, wa_ref[0]) + ba_ref[...])
    i = _sigmoid(_dot_nn(xcb, wx_ref[0]) + bx_ref[...])
    sp = _softplus(-lam_ref[...])
    log_a = (-RG_C * sp) * r
    a = jnp.exp(log_a)
    m = jnp.sqrt(-jnp.tanh(log_a) * (a * a + 1.0))
    return xcb, r, i, sp, a, m


def _rg_specs(B, S, rows, order):
    nc = S // rows

    def idx(fn):
        def index_map(*ids):
            v = dict(zip(order.lower(), ids))
            c = (nc - 1 - v["c"]) if "C" in order else v["c"]
            return fn(v["b"], v["d"], c)
        return index_map

    return dict(
        proj=pl.BlockSpec((2, 1, rows, RG_COLS), idx(lambda b, d, c: (0, b, c, d))),
        act=pl.BlockSpec((1, rows, RG_COLS), idx(lambda b, d, c: (b, c, d))),
        taps=pl.BlockSpec((CONV_WIDTH, RG_COLS), idx(lambda b, d, c: (0, d))),
        vec=pl.BlockSpec((1, RG_COLS), idx(lambda b, d, c: (0, d))),
        gate=pl.BlockSpec((1, RG_COLS, RG_COLS), idx(lambda b, d, c: (d, 0, 0))),
    )


def rglru_fwd(proj, conv_w, conv_b, wa, ba, wx, bx, lam, name, rows=RG_ROWS, ride=None):
    _, B, S, _ = proj.shape
    rows = min(rows, S)
    sp_ = _rg_specs(B, S, rows, "bdc")

    def body(p_ref, cw_ref, cb_ref, wa_ref, ba_ref, wx_ref, bx_ref, lam_ref,
             xc_ref, hs_ref, hp_ref, gt_ref, ext_ref, a_ref, u_ref, xcar_ref, hcar_ref):
        @pl.when(pl.program_id(2) == 0)
        def _():
            xcar_ref[...] = jnp.zeros_like(xcar_ref)
            hcar_ref[...] = jnp.zeros_like(hcar_ref)

        xr = p_ref[0, 0]
        ext_ref[0:8, :] = xcar_ref[...]
        ext_ref[8:, :] = xr
        xcar_ref[...] = xr[rows - 8:, :]
        xc = ext_ref[pl.ds(5, rows), :] * cw_ref[0:1, :]
        xc = xc + ext_ref[pl.ds(6, rows), :] * cw_ref[1:2, :]
        xc = xc + ext_ref[pl.ds(7, rows), :] * cw_ref[2:3, :]
        xc = xc + xr * cw_ref[3:4, :] + cb_ref[...]
        xc_ref[0] = xc
        _, _, i, _, a, m = _rg_gates(xc, wa_ref, ba_ref, wx_ref, bx_ref, lam_ref)
        a_ref[...] = a
        u_ref[...] = m * (i * xc)

        def step(r0, carry):
            h = _group_scan(a_ref[pl.ds(r0, 8), :], u_ref[pl.ds(r0, 8), :], carry)
            row = lax.broadcasted_iota(jnp.int32, h.shape, 0)
            hs_ref[0, pl.ds(r0, 8), :] = h
            hp_ref[0, pl.ds(r0, 8), :] = jnp.where(row == 0, carry, pltpu.roll(h, 1, 0))
            return h[7:8, :]

        hcar_ref[0:1, :] = _scan_rows(rows, step, hcar_ref[0:1, :])
        gt_ref[0] = (hs_ref[0] * _silu(p_ref[1, 0])).astype(BF16)

    act = jax.ShapeDtypeStruct((B, S, RG_WIDTH), F32)
    return _call(
        body, name=name, grid=(B, RG_GROUPS, S // rows),
        in_specs=[sp_["proj"], sp_["taps"], sp_["vec"], sp_["gate"], sp_["vec"], sp_["gate"], sp_["vec"], sp_["vec"]],
        out_specs=[sp_["act"]] * 4,
        out_shape=[act, act, act, jax.ShapeDtypeStruct((B, S, RG_WIDTH), BF16)],
        scratch_shapes=[pltpu.VMEM((rows + 8, RG_COLS), F32), pltpu.VMEM((rows, RG_COLS), F32),
                        pltpu.VMEM((rows, RG_COLS), F32), pltpu.VMEM((8, RG_COLS), F32), pltpu.VMEM((8, RG_COLS), F32)],
        semantics=("parallel", "parallel", "arbitrary"), args=[proj, conv_w, conv_b, wa, ba, wx, bx, lam], ride=ride)


def rglru_bwd(proj, xc, hs, hprev, dgated, conv_w, wa, ba, wx, bx, lam, name, rows=RG_ROWS, ride=None):
    _, B, S, _ = proj.shape
    rows = min(rows, S)
    sp_ = _rg_specs(B, S, rows, "dbC")

    def body(p_ref, xc_ref, hs_ref, hp_ref, dgt_ref, cw_ref, wa_ref, ba_ref, wx_ref, bx_ref, lam_ref,
             dp_ref, dcw_ref, dcb_ref, dwa_ref, dba_ref, dwx_ref, dbx_ref, dlam_ref,
             ext_ref, c_ref, l_ref, acar_ref, lcar_ref, dcar_ref):
        b_, c_ = pl.program_id(1), pl.program_id(2)

        @pl.when(c_ == 0)
        def _():
            acar_ref[...] = jnp.zeros_like(acar_ref)
            lcar_ref[...] = jnp.zeros_like(lcar_ref)
            dcar_ref[...] = jnp.zeros_like(dcar_ref)

        @pl.when((c_ == 0) & (b_ == 0))
        def _():
            for ref in (dcw_ref, dcb_ref, dwa_ref, dba_ref, dwx_ref, dbx_ref, dlam_ref):
                ref[...] = jnp.zeros_like(ref)

        xr, g = p_ref[0, 0], p_ref[1, 0]
        xc_v = xc_ref[0]
        xcb, r, i, sp, a, m = _rg_gates(xc_v, wa_ref, ba_ref, wx_ref, bx_ref, lam_ref)
        dhs, dg = _gate_bwd(dgt_ref[0], hs_ref[0], g)
        dp_ref[1, 0] = dg.astype(BF16)

        ext_ref[0:rows, :] = a
        ext_ref[rows:, :] = acar_ref[...]
        acar_ref[...] = a[0:8, :]
        c_ref[...] = ext_ref[pl.ds(1, rows), :]
        l_ref[...] = dhs

        def step(r0, carry):
            lam_g = _group_scan(c_ref[pl.ds(r0, 8), :], l_ref[pl.ds(r0, 8), :], carry, reverse=True)
            l_ref[pl.ds(r0, 8), :] = lam_g
            return lam_g[0:1, :]

        lcar_ref[0:1, :] = _scan_rows(rows, step, lcar_ref[0:1, :], reverse=True)
        du = l_ref[...]
        da = du * hp_ref[0]
        dlog_a = da * a - (du * (i * xc_v)) * (a * a / m)
        dr = dlog_a * (-RG_C * sp)
        dsp = jnp.sum(dlog_a * (-RG_C * r), axis=0, keepdims=True)
        dlam_ref[...] += dsp * (-_sigmoid(-lam_ref[...]))
        dpa = dr * (r * (1.0 - r))
        dpx = (du * (m * xc_v)) * (i * (1.0 - i))
        dba_ref[...] += jnp.sum(dpa, axis=0, keepdims=True)
        dbx_ref[...] += jnp.sum(dpx, axis=0, keepdims=True)
        dpa, dpx = dpa.astype(BF16), dpx.astype(BF16)
        dwa_ref[0] += _dot_tn(xcb, dpa)
        dwx_ref[0] += _dot_tn(xcb, dpx)
        dxc = du * (m * i) + _dot_nt(dpa, wa_ref[0]) + _dot_nt(dpx, wx_ref[0])

        dcb_ref[...] += jnp.sum(dxc, axis=0, keepdims=True)
        ext_ref[0:rows, :] = dxc
        ext_ref[rows:, :] = dcar_ref[...]
        dcar_ref[...] = dxc[0:8, :]
        dxr = jnp.zeros_like(dxc)
        for k in range(CONV_WIDTH):
            tap = CONV_WIDTH - 1 - k
            ahead = dxc if k == 0 else ext_ref[pl.ds(k, rows), :]
            dxr = dxr + ahead * cw_ref[tap:tap + 1, :]
            dcw_ref[tap:tap + 1, :] += jnp.sum(xr * ahead, axis=0, keepdims=True)
        dp_ref[0, 0] = dxr.astype(BF16)

    vec = jax.ShapeDtypeStruct((1, RG_WIDTH), F32)
    gate = jax.ShapeDtypeStruct((RG_GROUPS, RG_COLS, RG_COLS), F32)
    return _call(
        body, name=name, grid=(RG_GROUPS, B, S // rows),
        in_specs=[sp_["proj"], sp_["act"], sp_["act"], sp_["act"], sp_["act"], sp_["taps"],
                  sp_["gate"], sp_["vec"], sp_["gate"], sp_["vec"], sp_["vec"]],
        out_specs=[sp_["proj"], sp_["taps"], sp_["vec"], sp_["gate"], sp_["vec"], sp_["gate"], sp_["vec"], sp_["vec"]],
        out_shape=[jax.ShapeDtypeStruct(proj.shape, BF16), jax.ShapeDtypeStruct((CONV_WIDTH, RG_WIDTH), F32), vec,
                   gate, vec, gate, vec, vec],
        scratch_shapes=[pltpu.VMEM((rows + 8, RG_COLS), F32), pltpu.VMEM((rows, RG_COLS), F32),
                        pltpu.VMEM((rows, RG_COLS), F32), pltpu.VMEM((8, RG_COLS), F32),
                        pltpu.VMEM((8, RG_COLS), F32), pltpu.VMEM((8, RG_COLS), F32)],
        semantics=("arbitrary", "arbitrary", "arbitrary"),
        args=[proj, xc, hs, hprev, dgated, conv_w, wa, ba, wx, bx, lam], ride=ride)


def block_diag_gates(w):
    per = RG_COLS // RG_BLOCK
    w4 = w.reshape(RG_GROUPS, per, RG_BLOCK, RG_BLOCK)
    return jnp.einsum("dipq,ij->dipjq", w4, jnp.eye(per, dtype=w.dtype)).reshape(RG_GROUPS, RG_COLS, RG_COLS)


def block_diag_gates_t(dw):
    per = RG_COLS // RG_BLOCK
    dw6 = dw.reshape(RG_GROUPS, per, RG_BLOCK, per, RG_BLOCK)
    return jnp.stack([dw6[:, i, :, i, :] for i in range(per)], axis=1).reshape(RG_BLOCKS, RG_BLOCK, RG_BLOCK)


def adamw(w, parts, m, v, name, layer=0, prev=None, part_row0=0, row_tile=ROW_TILE):
    L, R, C = w.shape
    n_parts = parts.shape[0]
    br = row_tile if R % row_tile == 0 else R

    def body(w_ref, p_ref, m_ref, v_ref, *refs):
        g_ref, d_ref, nm_ref, nv_ref = refs[-4:]
        g = p_ref[0].astype(F32)
        for k in range(1, n_parts):
            g = g + p_ref[k].astype(F32)
        nm = ADAM_B1 * m_ref[0] + (1.0 - ADAM_B1) * g
        nv = ADAM_B2 * v_ref[0] + (1.0 - ADAM_B2) * (g * g)
        m_hat = nm / (1.0 - ADAM_B1 ** ADAM_STEP)
        v_hat = nv / (1.0 - ADAM_B2 ** ADAM_STEP)
        g_ref[0] = g
        d_ref[0] = -ADAM_LR * (m_hat / (jnp.sqrt(v_hat) + ADAM_EPS) + ADAM_WD * w_ref[0])
        nm_ref[0] = nm
        nv_ref[0] = nv

    slab = pl.BlockSpec((1, br, C), lambda i: (layer, i, 0))
    out = jax.ShapeDtypeStruct((L, R, C), F32)
    carried = [] if prev is None else list(prev)
    return _call(
        body, name=name, grid=(R // br,),
        in_specs=[slab, pl.BlockSpec((n_parts, br, C), lambda i: (0, part_row0 // br + i, 0)), slab, slab]
        + [pl.BlockSpec(memory_space=pl.ANY)] * len(carried),
        out_specs=[slab] * 4, out_shape=[out] * 4, semantics=("parallel",), args=[w, parts, m, v] + carried,
        aliases={4 + k: k for k in range(len(carried))})


def _seq(a, B):
    return a.reshape(a.shape[:-2] + (B, a.shape[-2] // B, a.shape[-1]))


def _flat(a):
    return a.reshape(a.shape[:-3] + (a.shape[-3] * a.shape[-2], a.shape[-1]))


def _tiles(w, which, **default):
    return dict(default, **w.get("tiles", {}).get(which, {}))


def mixer_a_fwd(h, w, B, tag, rides):
    proj = matmul(h, w["w_in"], mode="nn", out_dtype=F32, name=f"{tag}_proj", out_slabs=4,
                  ride=rides.pop(f"{tag}_proj", None), **_tiles(w, "proj"))
    o, gated = attn_a_fwd(_seq(proj, B), w["bias"], f"{tag}_attn", ride=rides.pop(f"{tag}_attn", None))
    return _flat(gated), dict(proj=proj, o=o)


def mixer_a_bwd(dgated, w, saved, B, tag, rides):
    dproj, dbias = attn_a_bwd(_seq(saved["proj"], B), w["bias"], saved["o"], _seq(dgated, B), f"{tag}_attn_bwd",
                              ride=rides.pop(f"{tag}_attn_bwd", None))
    return _flat(dproj), dict(bias=dbias)


def mixer_b_fwd(h, w, B, tag, rides):
    proj = matmul(h, w["w_in"], mode="nn", out_dtype=F32, name=f"{tag}_proj", out_slabs=2, bn=RG_COLS,
                  ride=rides.pop(f"{tag}_proj", None))
    xc, hs, hprev, gated = rglru_fwd(_seq(proj, B), w["conv_w"], w["conv_b"], w["wa"], w["ba"], w["wx"], w["bx"],
                                     w["lam"], f"{tag}_rglru", ride=rides.pop(f"{tag}_rglru", None))
    return _flat(gated), dict(proj=proj, xc=xc, hs=hs, hprev=hprev)


def mixer_b_bwd(dgated, w, saved, B, tag, rides):
    dproj, dcw, dcb, dwa, dba, dwx, dbx, dlam = rglru_bwd(
        _seq(saved["proj"], B), saved["xc"], saved["hs"], saved["hprev"], _seq(dgated, B),
        w["conv_w"], w["wa"], w["ba"], w["wx"], w["bx"], w["lam"], f"{tag}_rglru_bwd",
        ride=rides.pop(f"{tag}_rglru_bwd", None))
    return _flat(dproj), dict(conv_w=dcw, conv_b=dcb, wa=dwa, ba=dba, wx=dwx, bx=dbx, lam=dlam)


def mixer_c_fwd(h, w, B, tag, rides):
    proj = matmul(h, w["w_in"], mode="nn", out_dtype=F32, name=f"{tag}_proj", out_slabs=4,
                  ride=rides.pop(f"{tag}_proj", None), **_tiles(w, "proj"))
    f_logit = matmul(h, w["w_f"], mode="nn", out_dtype=F32, name=f"{tag}_fproj")
    cum = fox_cum_fwd(_seq(f_logit, B), w["f_bias"], f"{tag}_cum")
    cum_row = cum[:, :, :HEADS].transpose(0, 2, 1)
    o, gated = fox_fwd(_seq(proj, B), cum, cum_row, f"{tag}_attn", ride=rides.pop(f"{tag}_attn", None))
    return _flat(gated), dict(proj=proj, o=o, f_logit=f_logit, cum=cum, cum_row=cum_row)


def mixer_c_bwd(dgated, w, saved, B, tag, rides):
    dproj, dck = fox_bwd(_seq(saved["proj"], B), saved["cum"], saved["cum_row"], saved["o"], _seq(dgated, B),
                         f"{tag}_attn_bwd", ride=rides.pop(f"{tag}_attn_bwd", None))
    S = dck.shape[-1]
    dcum = jnp.pad(-dck.reshape(B, HEADS, S).transpose(0, 2, 1), ((0, 0), (0, 0), (0, HEAD_DIM - HEADS)))
    df, dfb = fox_cum_bwd(_seq(saved["f_logit"], B), w["f_bias"], dcum, f"{tag}_cum_bwd")
    return _flat(dproj), dict(f_bias=dfb, df=_flat(df).astype(BF16))


MIXERS = {"a": (mixer_a_fwd, mixer_a_bwd), "b": (mixer_b_fwd, mixer_b_bwd), "c": (mixer_c_fwd, mixer_c_bwd)}
LAYER_KINDS = "abca"


def local_step(x, target, norm_pre, norm_post, get_layer, rides, on_grads):
    B, S, Dm = x.shape
    n_layers = len(LAYER_KINDS)
    xs = [x.reshape(B * S, Dm)]
    saved, layers = [], []
    h = prenorm_fwd(xs[0], norm_pre[0:1], "l0a_prenorm")
    for li, kind in enumerate(LAYER_KINDS):
        tag = f"l{li}{kind}"
        w = get_layer(li)
        gated, sv = MIXERS[kind][0](h, w, B, tag, rides)
        if callable(w["w_out"]):
            w["w_out"] = w["w_out"]()
        y = matmul(gated, w["w_out"], mode="nn", out_dtype=F32, name=f"{tag}_out", ride=rides.pop(f"{tag}_out", None))
        saved.append(dict(sv, h=h, gated=gated, y=y))
        layers.append(w)
        if li + 1 < n_layers:
            x_new, h = postnorm_prenorm_fwd(xs[-1], y, norm_post[li:li + 1], norm_pre[li + 1:li + 2],
                                            f"{tag}_postnorm")
            xs.append(x_new)
    loss, dx = postnorm_loss(xs[-1], y, norm_post[n_layers - 1:], target.reshape(B * S, Dm), "loss")

    for li in reversed(range(n_layers)):
        kind, w, sv = LAYER_KINDS[li], layers[li], saved[li]
        tag = f"l{li}{kind}"
        dy, dg_post = postnorm_bwd(sv["y"], norm_post[li:li + 1], dx, f"{tag}_postnorm_bwd")
        on_grads(li, "norm_post", dg_post)
        on_grads(li, "w_out", matmul(sv["gated"], dy, mode="tn", out_dtype=BF16, name=f"{tag}_dwout",
                                     ride=rides.pop(f"{tag}_dwout", None)))
        dgated = matmul(dy, w["w_out"], mode="nt", out_dtype=F32, name=f"{tag}_dgated",
                        ride=rides.pop(f"{tag}_dgated", None))
        dproj, gw = MIXERS[kind][1](dgated, w, sv, B, tag, rides)
        df = gw.pop("df", None)
        for name, value in gw.items():
            on_grads(li, name, value)
        parts = w.get("dwin_parts", [(0, 1, 1)])
        for i, m_part in enumerate(parts):
            suffix = f"_{i}" if len(parts) > 1 else ""
            on_grads(li, "w_in" + suffix,
                     matmul(sv["h"], dproj, mode="tn", out_dtype=BF16, name=f"{tag}_dwin{suffix}", m_part=m_part,
                            out_slabs=w["grad_slabs"], ride=rides.pop(f"{tag}_dwin{suffix}", None),
                            **_tiles(w, "dwin")))
        if df is not None:
            on_grads(li, "w_f", matmul(sv["h"], df, mode="tn", out_dtype=BF16, name=f"{tag}_dwf"))
        dhs = [matmul(dproj, w["w_in"], mode="nt", out_dtype=F32, name=f"{tag}_dh",
                      ride=rides.pop(f"{tag}_dh", None), **_tiles(w, "dh"))]
        if df is not None:
            dhs.append(matmul(df, w["w_f"], mode="nt", out_dtype=F32, name=f"{tag}_dhf"))
        dx, dg_pre = prenorm_bwd(xs[li], norm_pre[li:li + 1], dhs, dx, f"{tag}_prenorm_bwd")
        on_grads(li, "norm_pre", dg_pre)
    assert not rides, list(rides)
    return loss, dx.reshape(B, S, Dm)


WEIGHTS = ("norm_pre", "norm_post", "a_w_in", "a_rel_bias", "a_w_out", "b_w_in", "b_conv_w", "b_conv_b",
           "b_gate_a_w", "b_gate_a_b", "b_gate_x_w", "b_gate_x_b", "b_lambda", "b_w_out", "c_w_in", "c_f_bias",
           "c_w_out")
C_SHARD = (4 * D_MODEL + HEADS) // N_DEV


def _rows(gathered):
    return gathered.reshape(gathered.shape[0] * gathered.shape[1], gathered.shape[2])


def layer_a(w_in, w_out, rel_bias):
    return dict(w_in=w_in, w_out=w_out if callable(w_out) else _rows(w_out), bias=band_bias(rel_bias),
                grad_slabs=N_DEV)


def layer_b(w_in, w_out, conv_w, small):
    return dict(
        w_in=w_in, w_out=_rows(w_out), grad_slabs=N_DEV,
        conv_w=conv_w.transpose(1, 0, 2).reshape(CONV_WIDTH, RG_WIDTH),
        conv_b=small["b_conv_b"], lam=small["b_lambda"],
        wa=block_diag_gates(small["b_gate_a_w"][0]).astype(BF16), ba=small["b_gate_a_b"].reshape(1, RG_WIDTH),
        wx=block_diag_gates(small["b_gate_x_w"][0]).astype(BF16), bx=small["b_gate_x_b"].reshape(1, RG_WIDTH))


def layer_c(w_in, w_out, small):
    full = w_in.transpose(1, 0, 2).reshape(D_MODEL, N_DEV * C_SHARD)
    return dict(w_in=full[:, :4 * D_MODEL], w_f=jnp.pad(full[:, 4 * D_MODEL:], ((0, 0), (0, HEAD_DIM - HEADS))),
                w_out=_rows(w_out), grad_slabs=1,
                f_bias=jnp.pad(small["c_f_bias"], ((0, 0), (0, HEAD_DIM - HEADS))))


def c_w_in_blocks(dmain, df):
    full = jnp.concatenate([dmain, df[:, :HEADS].astype(dmain.dtype)], axis=1)
    return full.reshape(D_MODEL, N_DEV, C_SHARD).transpose(1, 0, 2)


def _row_blocks(g):
    return g.reshape(N_DEV, g.shape[0] // N_DEV, g.shape[1])


PACK_LANES = 128
PACK_ALIGN = 8 * PACK_LANES


def pack(parts):
    flat = []
    for p in parts:
        n = p.size
        flat.append(jnp.pad(p.reshape(n), (0, -n % PACK_ALIGN)).reshape(-1, PACK_LANES))
    rows = sum(f.shape[0] for f in flat)
    flat.append(jnp.zeros((-rows % ROW_TILE, PACK_LANES), F32))
    return jnp.concatenate(flat, axis=0)


def unpack(packed, shapes):
    out, row = [], 0
    for shape in shapes:
        n = 1
        for s in shape:
            n *= s
        n_rows = (n + PACK_ALIGN - 1) // PACK_ALIGN * 8
        out.append(packed[row:row + n_rows].reshape(-1)[:n].reshape(shape))
        row += n_rows
    return out


LATE = (("a_rel_bias", slice(0, 1)), ("norm_pre", slice(0, 2)), ("norm_post", slice(0, 1)))
EARLY = (("a_rel_bias", slice(1, 2)), ("norm_pre", slice(2, 4)), ("norm_post", slice(1, 4)),
         ("b_conv_b", slice(None)), ("b_gate_a_w", slice(None)), ("b_gate_a_b", slice(None)),
         ("b_gate_x_w", slice(None)), ("b_gate_x_b", slice(None)), ("b_lambda", slice(None)),
         ("c_f_bias", slice(None)))


def _pieces(tree, pieces):
    return [tree[name][sl] for name, sl in pieces]


def kernel(x, norm_pre, norm_post, a_w_in, a_rel_bias, a_w_out, b_w_in, b_conv_w, b_conv_b, b_gate_a_w, b_gate_a_b, b_gate_x_w, b_gate_x_b, b_lambda, b_w_out, c_w_in, c_f_bias, c_w_out, loss_target, m_norm_pre, m_norm_post, m_a_w_in, m_a_rel_bias, m_a_w_out, m_b_w_in, m_b_conv_w, m_b_conv_b, m_b_gate_a_w, m_b_gate_a_b, m_b_gate_x_w, m_b_gate_x_b, m_b_lambda, m_b_w_out, m_c_w_in, m_c_f_bias, m_c_w_out, v_norm_pre, v_norm_post, v_a_w_in, v_a_rel_bias, v_a_w_out, v_b_w_in, v_b_conv_w, v_b_conv_b, v_b_gate_a_w, v_b_gate_a_b, v_b_gate_x_w, v_b_gate_x_b, v_b_lambda, v_b_w_out, v_c_w_in, v_c_f_bias, v_c_w_out):
    args = dict(locals())
    w = {n: args[n] for n in WEIGHTS}
    m = {n: args["m_" + n] for n in WEIGHTS}
    v = {n: args["v_" + n] for n in WEIGHTS}

    a_in, a_out = a_w_in.astype(BF16), a_w_out.astype(BF16)
    gather_a0 = Ride([a_in[0]], scatter=False, via_sibling=True)
    in_l0_proj = Ride([b_w_in[0].astype(BF16), b_conv_w[0], a_out[0]], scatter=False, via_sibling=True)
    in_l0_attn = Ride([c_w_in[0].astype(BF16), b_w_out[0].astype(BF16)], scatter=False, via_sibling=True)
    in_l1_proj = Ride([c_w_out[0].astype(BF16)], scatter=False)
    in_l2_proj = Ride([a_out[1]], scatter=False)
    in_l2_attn = Ride([a_in[1]], scatter=False, via_sibling=True)
    exchange(gather_a0, "gather_l0")
    rides = {"l0a_proj": in_l0_proj, "l0a_attn": in_l0_attn, "l1b_proj": in_l1_proj, "l2c_proj": in_l2_proj,
             "l2c_attn": in_l2_attn}

    def get_layer(li):
        if li == 0:
            return dict(layer_a(gather_a0.out[0], lambda: _rows(in_l0_proj.out[2]), a_rel_bias[0]),
                        dwin_parts=[(0, 1, 4), (1, 1, 4), (2, 2, 4)])
        if li == 1:
            return layer_b(in_l0_proj.out[0], in_l0_attn.out[1], in_l0_proj.out[1], w)
        if li == 2:
            return dict(layer_c(in_l0_attn.out[0], in_l1_proj.out[0], w), tiles=dict(proj=dict(bn=2048)))
        return layer_a(in_l2_attn.out[0], in_l2_proj.out[0], a_rel_bias[1])

    grads = [dict() for _ in LAYER_KINDS]
    scatters = {}

    def rel_bias_grad(j, dbias):
        return jax.vjp(band_bias, a_rel_bias[j])[1](dbias)[0][None]

    def early_partial():
        gb, gc = grads[1], grads[2]
        tree = dict(
            a_rel_bias=jnp.concatenate([jnp.zeros((1, HEADS, N_REL), F32), rel_bias_grad(1, grads[3]["bias"])]),
            norm_pre=jnp.concatenate([jnp.zeros((2, D_MODEL), F32)] + [grads[li]["norm_pre"] for li in (2, 3)]),
            norm_post=jnp.concatenate([jnp.zeros((1, D_MODEL), F32)] + [grads[li]["norm_post"] for li in (1, 2, 3)]),
            b_conv_b=gb["conv_b"], b_lambda=gb["lam"],
            b_gate_a_w=block_diag_gates_t(gb["wa"])[None], b_gate_a_b=gb["ba"].reshape(1, RG_BLOCKS, RG_BLOCK),
            b_gate_x_w=block_diag_gates_t(gb["wx"])[None], b_gate_x_b=gb["bx"].reshape(1, RG_BLOCKS, RG_BLOCK),
            c_f_bias=gc["f_bias"][:, :HEADS])
        return pack(_pieces(tree, EARLY))

    def send(key, host, blocks, scatter=True, via_sibling=False):
        ride = rides.setdefault(host, Ride([], scatter, via_sibling))
        assert (ride.scatter, ride.via_sibling) == (scatter, via_sibling)
        scatters[key] = (ride, len(ride.arrs))
        ride.arrs.append(blocks)

    def on_grads(li, name, value):
        g = grads[li]
        g[name] = value
        if (li, name) == (3, "w_out"):
            send("a1_out", "l3a_attn_bwd", _row_blocks(value))
        elif (li, name) == (3, "w_in"):
            send("a1_in", "l2c_attn_bwd", value)
        elif (li, name) == (2, "w_out"):
            send("c_out", "l2c_attn_bwd", _row_blocks(value))
        elif (li, name) == (2, "w_f"):
            blocks = c_w_in_blocks(g["w_in"], value)
            send("c_in_0", "l2c_dh", blocks[:, :D_MODEL // 2])
            send("c_in_1", "l1b_rglru_bwd", blocks[:, D_MODEL // 2:])
        elif (li, name) == (1, "w_out"):
            send("b_out", "l1b_dh", _row_blocks(value))
        elif (li, name) == (1, "w_in"):
            send("b_in", "l0a_attn_bwd", value)
            send("b_conv", "l0a_attn_bwd",
                 g["conv_w"].reshape(CONV_WIDTH, N_DEV, RG_WIDTH // N_DEV).transpose(1, 0, 2))
        elif (li, name) == (1, "lam"):
            send("early", "l1b_dwin", early_partial(), scatter=False, via_sibling=True)
        elif (li, name) == (0, "w_out"):
            send("a0_out", "l0a_attn_bwd", _row_blocks(value))
        elif (li, name) == (0, "w_in_0"):
            send("a0_in_0", "l0a_dwin_1", value)
        elif (li, name) == (0, "w_in_1"):
            send("a0_in_1", "l0a_dwin_2", value)
        elif (li, name) == (0, "w_in_2"):
            send("a0_in_2", "l0a_dh", value)

    loss, grad_x = local_step(x, loss_target, norm_pre, norm_post, get_layer, rides, on_grads)
    late_tree = dict(a_rel_bias=rel_bias_grad(0, grads[0]["bias"]), norm_post=grads[0]["norm_post"],
                     norm_pre=jnp.concatenate([grads[0]["norm_pre"], grads[1]["norm_pre"]]))
    late_parts = exchange(Ride([pack([late_tree[n] for n, _ in LATE])], scatter=False), "gather_late_grads")[0]

    def sharded(name, slab_parts):
        shape = w[name].shape
        slabs = (len(slab_parts), shape[0] * shape[1] // len(slab_parts), shape[2])
        outs = None
        for j, (parts, row0) in enumerate(slab_parts):
            outs = adamw(w[name].reshape(slabs), parts, m[name].reshape(slabs), v[name].reshape(slabs),
                         f"adamw_{name}_{j}", layer=j, prev=outs, part_row0=row0)
        return [o.reshape(shape) for o in outs]

    def received(key):
        ride, position = scatters[key]
        return ride.out[position]

    res = dict(
        a_w_in=sharded("a_w_in", [(received("a0_in_0"), 0), (received("a0_in_1"), 0), (received("a0_in_2"), 0),
                                  (received("a0_in_2"), D_MODEL // 4)]
                       + [(received("a1_in"), q * D_MODEL // 4) for q in range(4)]),
        a_w_out=sharded("a_w_out", [(received("a0_out"), 0), (received("a1_out"), 0)]),
        b_w_in=sharded("b_w_in", [(received("b_in"), 0)]),
        b_w_out=sharded("b_w_out", [(received("b_out"), 0)]),
        b_conv_w=sharded("b_conv_w", [(received("b_conv"), 0)]),
        c_w_in=sharded("c_w_in", [(received("c_in_0"), 0), (received("c_in_1"), 0)]),
        c_w_out=sharded("c_w_out", [(received("c_out"), 0)]))

    packed = {}
    for label, pieces, parts in (("early", EARLY, received("early")), ("late", LATE, late_parts)):
        outs = adamw(pack(_pieces(w, pieces))[None], parts, pack(_pieces(m, pieces))[None],
                     pack(_pieces(v, pieces))[None], f"adamw_replicated_{label}")
        shapes = [w[n][sl].shape for n, sl in pieces]
        packed[label] = [dict(zip([n for n, _ in pieces], unpack(o[0], shapes))) for o in outs]
    for n in ("b_conv_b", "b_gate_a_w", "b_gate_a_b", "b_gate_x_w", "b_gate_x_b", "b_lambda", "c_f_bias"):
        res[n] = [packed["early"][k][n] for k in range(4)]
    for n in ("a_rel_bias", "norm_pre", "norm_post"):
        res[n] = [jnp.concatenate([packed["late"][k][n], packed["early"][k][n]]) for k in range(4)]

    total = lax.psum(loss[0, 0], ("x", "y", "c"))
    return (total, grad_x, *[res[n][0] for n in WEIGHTS], *[res[n][1] for n in WEIGHTS],
            *[res[n][2] for n in WEIGHTS], *[res[n][3] for n in WEIGHTS])
```

```python
import functools

import jax
import jax.numpy as jnp
from jax import lax
from jax.experimental import pallas as pl
from jax.experimental.pallas import tpu as pltpu

F32 = jnp.float32
BF16 = jnp.bfloat16

N_DEV = 8
D_MODEL = 2048
HEADS = 16
HEAD_DIM = 128
CHUNK = 64
LEFT_CHUNKS = 8
REL_CLIP = 256
N_REL = 2 * REL_CLIP + 1
TQ = 256
A_PAD = LEFT_CHUNKS * CHUNK
A_KW = A_PAD + TQ
RG_WIDTH = 2560
RG_BLOCKS = 16
RG_BLOCK = 160
RG_COLS = 640
RG_GROUPS = RG_WIDTH // RG_COLS
RG_C = 8.0
CONV_WIDTH = 4
RMS_EPS = 1e-6
NEG_INF = -1e30
ADAM_LR = 0.001
ADAM_B1 = 0.9
ADAM_B2 = 0.999
ADAM_EPS = 1e-08
ADAM_WD = 0.01
ADAM_STEP = 10
VMEM_LIMIT = 56 * 1024 * 1024
MESH = pl.DeviceIdType.MESH


def _params(sem, vmem=VMEM_LIMIT):
    return pltpu.CompilerParams(dimension_semantics=sem, vmem_limit_bytes=vmem)


def _sigmoid(x):
    return 1.0 / (1.0 + jnp.exp(-x))


def _log1p(y):
    u = 1.0 + y
    return jnp.where(u == 1.0, y, jnp.log(u) * (y / jnp.where(u == 1.0, 1.0, u - 1.0)))


def _softplus(x):
    return jnp.maximum(x, 0.0) + _log1p(jnp.exp(-jnp.abs(x)))


def _dot(a, b, dims):
    return lax.dot_general(a, b, (dims, ((), ())), preferred_element_type=F32)


def _dot_nn(a, b):
    return _dot(a, b, ((1,), (0,)))


def _dot_nt(a, b):
    return _dot(a, b, ((1,), (1,)))


def _dot_tn(a, b):
    return _dot(a, b, ((0,), (0,)))


def _peers():
    x, y, c = lax.axis_index("x"), lax.axis_index("y"), lax.axis_index("c")
    me = 4 * x + 2 * y + c
    peers = []
    for k in range(1, N_DEV):
        px = 1 - x if k & 4 else x
        py = 1 - y if k & 2 else y
        pc = 1 - c if k & 1 else c
        peers.append(((px, py, pc), 4 * px + 2 * py + pc))
    return me, peers


class Ride:
    def __init__(self, arrs, scatter, via_sibling=False):
        assert not (scatter and via_sibling)
        self.arrs, self.scatter, self.via_sibling, self.out = list(arrs), scatter, via_sibling, None

    def out_shapes(self):
        return [jax.ShapeDtypeStruct(a.shape if self.scatter else (N_DEV,) + a.shape, a.dtype) for a in self.arrs]

    def sem_shapes(self):
        n = len(self.arrs)
        return [pltpu.SemaphoreType.DMA((n, N_DEV - 1)), pltpu.SemaphoreType.DMA((n, N_DEV - 1)),
                pltpu.SemaphoreType.DMA((n,))]

    def _copies(self, ins, outs, sems, landing):
        send_sems, recv_sems, local_sems = sems
        me, peers = _peers()
        local, remote = [], []
        for a, (src, dst) in enumerate(zip(ins, outs)):
            local.append(pltpu.make_async_copy(src.at[me] if self.scatter else src, dst.at[me], local_sems.at[a]))
            for k, (peer, peer_idx) in enumerate(peers):
                remote.append(pltpu.make_async_remote_copy(
                    src_ref=src.at[peer_idx] if self.scatter else src, dst_ref=dst.at[peer_idx if landing else me],
                    send_sem=send_sems.at[a, k], recv_sem=recv_sems.at[a, k], device_id=peer, device_id_type=MESH))
        return local, remote

    def _direct(self, k):
        return not self.via_sibling or k == 0 or (k + 1) % 2 == 0

    def start(self, ins, outs, sems):
        local, remote = self._copies(ins, outs, sems, landing=False)
        n_peers = N_DEV - 1
        for cp in local + [cp for i, cp in enumerate(remote) if self._direct(i % n_peers)]:
            cp.start()

    def wait(self, ins, outs, sems):
        local, remote = self._copies(ins, outs, sems, landing=True)
        n_peers = N_DEV - 1
        for i, cp in enumerate(remote):
            if self._direct(i % n_peers):
                cp.wait()
        if self.via_sibling:
            send_sems, recv_sems, _ = sems
            me, peers = _peers()
            sibling = peers[0][0]
            passed = []
            for a, dst in enumerate(outs):
                for j in range(1, n_peers, 2):
                    came, lands = peers[j][1], peers[j + 1][1]
                    pltpu.make_async_remote_copy(
                        src_ref=dst.at[came], dst_ref=dst.at[came], send_sem=send_sems.at[a, j + 1],
                        recv_sem=recv_sems.at[a, j + 1], device_id=sibling, device_id_type=MESH).start()
                    passed.append(pltpu.make_async_remote_copy(
                        src_ref=dst.at[came], dst_ref=dst.at[lands], send_sem=send_sems.at[a, j + 1],
                        recv_sem=recv_sems.at[a, j + 1], device_id=sibling, device_id_type=MESH))
            for cp in passed:
                cp.wait()
        for cp in local:
            cp.wait()


def _call(body, *, name, grid, in_specs, out_specs, out_shape, args, scratch_shapes=(), semantics=None, ride=None,
          aliases=None):
    scratch_shapes = list(scratch_shapes)
    if ride is None:
        return pl.pallas_call(
            body, name=name, grid=grid, in_specs=in_specs, out_specs=out_specs, out_shape=out_shape,
            scratch_shapes=scratch_shapes, input_output_aliases=aliases or {},
            compiler_params=_params(semantics if grid else None))(*args)
    assert not aliases
    n_in, n_out, n_sc, n_r = len(in_specs), len(out_specs), len(scratch_shapes), len(ride.arrs)

    def riding(*refs):
        ins, r_ins = refs[:n_in], refs[n_in:n_in + n_r]
        outs, r_outs = refs[n_in + n_r:n_in + n_r + n_out], refs[n_in + n_r + n_out:n_in + 2 * n_r + n_out]
        rest = refs[n_in + 2 * n_r + n_out:]
        scratch, sems = rest[:n_sc], rest[n_sc:]
        first = last = None
        for axis, size in enumerate(grid):
            pid = pl.program_id(axis)
            first = (pid == 0) if first is None else first & (pid == 0)
            last = (pid == size - 1) if last is None else last & (pid == size - 1)
        if grid:
            pl.when(first)(lambda: ride.start(r_ins, r_outs, sems))
        else:
            ride.start(r_ins, r_outs, sems)
        body(*ins, *outs, *scratch)
        if grid:
            pl.when(last)(lambda: ride.wait(r_ins, r_outs, sems))
        else:
            ride.wait(r_ins, r_outs, sems)

    any_spec = pl.BlockSpec(memory_space=pl.ANY)
    res = pl.pallas_call(
        riding, name=name, grid=grid, in_specs=list(in_specs) + [any_spec] * n_r,
        out_specs=list(out_specs) + [any_spec] * n_r, out_shape=list(out_shape) + ride.out_shapes(),
        scratch_shapes=scratch_shapes + ride.sem_shapes(),
        compiler_params=_params(("arbitrary",) * len(grid) if grid else None))(*args, *ride.arrs)
    ride.out = list(res[n_out:])
    return list(res[:n_out])


def exchange(ride, name):
    _call(lambda: None, name=name, grid=(), in_specs=[], out_specs=[], out_shape=[], args=[], ride=ride)
    return ride.out


LANES = 128


def _fit(dims, want):
    dims = tuple(dims)
    if len(set(dims)) == 1 and dims[0] <= want:
        return dims[0]
    return max(t for t in range(LANES, want + 1, LANES) if all(d % t == 0 for d in dims))


def _cols(arr):
    return arr.shape[-1] * (arr.shape[0] if len(arr.shape) == 3 else 1)


def _tile_spec(shape, rblk, cblk, rc):
    if len(shape) == 2:
        return pl.BlockSpec((rblk, cblk), rc)
    per = shape[2] // cblk

    def index_map(*ids):
        r, c = rc(*ids)
        return (c // per, r, c % per)

    return pl.BlockSpec((1, rblk, cblk), index_map)


def matmul(a, b, *, mode, out_dtype, name, bm=1024, bn=1024, bk=2048, out_slabs=1, m_part=(0, 1, 1), ride=None):
    a_rows, a_cols, b_rows, b_cols = a.shape[-2], _cols(a), b.shape[-2], _cols(b)
    (K, M) = (a_rows, a_cols) if mode == "tn" else (a_cols, a_rows)
    N = b_rows if mode == "nt" else b_cols
    assert K == (b_cols if mode == "nt" else b_rows), (name, a.shape, b.shape)
    first_range, n_ranges, of_ranges = m_part
    row0, M = first_range * (M // of_ranges), n_ranges * (M // of_ranges)
    out_shape = (M, N) if out_slabs == 1 else (out_slabs, M, N // out_slabs)
    widths = dict(m=[M], n=[N, out_shape[-1]], k=[K])
    widths["m" if mode == "tn" else "k"].append(a.shape[-1])
    widths["k" if mode == "nt" else "n"].append(b.shape[-1])
    bm, bn, bk = _fit(widths["m"], bm), _fit(widths["n"], bn), _fit(widths["k"], bk)
    nk = K // bk
    dims = {"nn": ((1,), (0,)), "nt": ((1,), (1,)), "tn": ((0,), (0,))}[mode]

    def val(ref):
        return ref[0] if len(ref.shape) == 3 else ref[...]

    def put(ref, x):
        if len(ref.shape) == 3:
            ref[0] = x.astype(ref.dtype)
        else:
            ref[...] = x.astype(ref.dtype)

    def body(a_ref, b_ref, o_ref, *scratch):
        if nk == 1:
            put(o_ref, _dot(val(a_ref), val(b_ref), dims))
            return
        acc_ref, = scratch
        k = pl.program_id(2)

        @pl.when(k == 0)
        def _():
            acc_ref[...] = jnp.zeros_like(acc_ref)

        acc_ref[...] += _dot(val(a_ref), val(b_ref), dims)

        @pl.when(k == nk - 1)
        def _():
            put(o_ref, acc_ref[...])

    assert row0 % bm == 0
    m0 = row0 // bm
    if mode == "tn":
        a_spec = _tile_spec(a.shape, bk, bm, lambda j, i, k: (k, m0 + i))
    else:
        a_spec = _tile_spec(a.shape, bm, bk, lambda j, i, k: (m0 + i, k))
    if mode == "nt":
        b_spec = _tile_spec(b.shape, bn, bk, lambda j, i, k: (j, k))
    else:
        b_spec = _tile_spec(b.shape, bk, bn, lambda j, i, k: (k, j))
    return _call(
        body, name=name, grid=(N // bn, M // bm, nk), in_specs=[a_spec, b_spec],
        out_specs=[_tile_spec(out_shape, bm, bn, lambda j, i, k: (i, j))],
        out_shape=[jax.ShapeDtypeStruct(out_shape, out_dtype)],
        scratch_shapes=[] if nk == 1 else [pltpu.VMEM((bm, bn), F32)],
        semantics=("parallel", "parallel", "arbitrary"), args=[a, b], ride=ride)[0]


ROW_TILE = 256


def _rms_stats(z):
    r = lax.rsqrt(jnp.mean(z * z, axis=-1, keepdims=True) + RMS_EPS)
    return r, z * r


def _rms_bwd(n, r, g, dout):
    dn = dout * g
    return r * (dn - n * jnp.mean(dn * n, axis=-1, keepdims=True))


def _row_spec(T, Dm):
    bt = min(ROW_TILE, T)
    return bt, pl.BlockSpec((bt, Dm), lambda i: (i, 0)), pl.BlockSpec((1, Dm), lambda i: (0, 0))


def prenorm_fwd(x, g, name):
    T, Dm = x.shape
    bt, row, vec = _row_spec(T, Dm)

    def body(x_ref, g_ref, h_ref):
        _, n = _rms_stats(x_ref[...])
        h_ref[...] = (n * g_ref[...]).astype(BF16)

    return pl.pallas_call(
        body, name=name, grid=(T // bt,), in_specs=[row, vec], out_specs=row,
        out_shape=jax.ShapeDtypeStruct((T, Dm), BF16), compiler_params=_params(("parallel",)),
    )(x, g)


def postnorm_prenorm_fwd(x, y, g_post, g_next, name):
    T, Dm = x.shape
    bt, row, vec = _row_spec(T, Dm)

    def body(x_ref, y_ref, gp_ref, gn_ref, o_ref, h_ref):
        _, n = _rms_stats(y_ref[...])
        x_new = x_ref[...] + n * gp_ref[...]
        o_ref[...] = x_new
        _, n_new = _rms_stats(x_new)
        h_ref[...] = (n_new * gn_ref[...]).astype(BF16)

    return pl.pallas_call(
        body, name=name, grid=(T // bt,), in_specs=[row, row, vec, vec], out_specs=[row, row],
        out_shape=[jax.ShapeDtypeStruct((T, Dm), F32), jax.ShapeDtypeStruct((T, Dm), BF16)],
        compiler_params=_params(("parallel",)),
    )(x, y, g_post, g_next)


def postnorm_loss(x, y, g, target, name):
    T, Dm = x.shape
    bt, row, vec = _row_spec(T, Dm)

    def body(x_ref, y_ref, g_ref, t_ref, l_ref, d_ref):
        @pl.when(pl.program_id(0) == 0)
        def _():
            l_ref[...] = jnp.zeros_like(l_ref)

        _, n = _rms_stats(y_ref[...])
        err = (x_ref[...] + n * g_ref[...]) - t_ref[...]
        per_tok = jnp.mean(err * err, axis=-1, keepdims=True)
        l_ref[...] += 0.5 * jnp.sum(per_tok, axis=0, keepdims=True)
        d_ref[...] = err * (1.0 / Dm)

    return pl.pallas_call(
        body, name=name, grid=(T // bt,), in_specs=[row, row, vec, row],
        out_specs=[pl.BlockSpec((1, 1), lambda i: (0, 0)), row],
        out_shape=[jax.ShapeDtypeStruct((1, 1), F32), jax.ShapeDtypeStruct((T, Dm), F32)],
        compiler_params=_params(("arbitrary",)),
    )(x, y, g, target)


def postnorm_bwd(y, g, dout, name):
    T, Dm = y.shape
    bt, row, vec = _row_spec(T, Dm)

    def body(y_ref, g_ref, d_ref, dy_ref, dg_ref):
        @pl.when(pl.program_id(0) == 0)
        def _():
            dg_ref[...] = jnp.zeros_like(dg_ref)

        r, n = _rms_stats(y_ref[...])
        dout_v = d_ref[...]
        dg_ref[...] += jnp.sum(dout_v * n, axis=0, keepdims=True)
        dy_ref[...] = _rms_bwd(n, r, g_ref[...], dout_v).astype(BF16)

    return pl.pallas_call(
        body, name=name, grid=(T // bt,), in_specs=[row, vec, row], out_specs=[row, vec],
        out_shape=[jax.ShapeDtypeStruct((T, Dm), BF16), jax.ShapeDtypeStruct((1, Dm), F32)],
        compiler_params=_params(("arbitrary",)),
    )(y, g, dout)


def prenorm_bwd(x, g, dhs, dres, name):
    T, Dm = x.shape
    bt, row, vec = _row_spec(T, Dm)
    n_dh = len(dhs)

    def body(x_ref, g_ref, *refs):
        dh_refs, (dr_ref, dx_ref, dg_ref) = refs[:n_dh], refs[n_dh:]

        @pl.when(pl.program_id(0) == 0)
        def _():
            dg_ref[...] = jnp.zeros_like(dg_ref)

        r, n = _rms_stats(x_ref[...])
        dh_v = dh_refs[0][...]
        for extra in dh_refs[1:]:
            dh_v = dh_v + extra[...]
        dg_ref[...] += jnp.sum(dh_v * n, axis=0, keepdims=True)
        dx_ref[...] = dr_ref[...] + _rms_bwd(n, r, g_ref[...], dh_v)

    return pl.pallas_call(
        body, name=name, grid=(T // bt,), in_specs=[row, vec] + [row] * (n_dh + 1), out_specs=[row, vec],
        out_shape=[jax.ShapeDtypeStruct((T, Dm), F32), jax.ShapeDtypeStruct((1, Dm), F32)],
        compiler_params=_params(("arbitrary",)),
    )(x, g, *dhs, dres)


def _silu(g):
    return g * _sigmoid(g)


def _gate_bwd(dgated, core, g):
    sg = _sigmoid(g)
    return dgated * (g * sg), dgated * core * (sg * (1.0 + g * (1.0 - sg)))


def _softmax_rows(s):
    e = jnp.exp(s - jnp.max(s, axis=-1, keepdims=True))
    return e * (1.0 / jnp.sum(e, axis=-1, keepdims=True))


def _band_scores(qk, bias, r0):
    s = qk * (HEAD_DIM ** -0.5) + bias
    j = lax.broadcasted_iota(jnp.int32, s.shape, 1)
    return jnp.where(j >= A_PAD - r0, s, NEG_INF)


def _fill_padded_kv(p_ref, kp_ref, vp_ref):
    zeros = jnp.zeros((A_PAD, kp_ref.shape[1]), BF16)
    kp_ref[0:A_PAD, :] = zeros
    vp_ref[0:A_PAD, :] = zeros
    kp_ref[A_PAD:, :] = p_ref[1, 0].astype(BF16)
    vp_ref[A_PAD:, :] = p_ref[2, 0].astype(BF16)


def _head_specs(S, order, heads=1):
    def idx(fn):
        return lambda *ids: fn(**dict(zip(order, ids)))

    return (pl.BlockSpec((4, 1, S, heads * HEAD_DIM), idx(lambda b, h, t: (0, b, 0, h))),
            pl.BlockSpec((1, TQ, heads * HEAD_DIM), idx(lambda b, h, t: (b, t, h))))


def attn_a_fwd(proj, bias, name, heads=1, ride=None):
    _, B, S, W = proj.shape
    nt = S // TQ

    def body(p_ref, b_ref, o_ref, gt_ref, kp_ref, vp_ref):
        t = pl.program_id(2)

        @pl.when(t == 0)
        def _():
            _fill_padded_kv(p_ref, kp_ref, vp_ref)

        r0 = pl.multiple_of(t * TQ, TQ)
        for e in range(heads):
            lanes = slice(e * HEAD_DIM, (e + 1) * HEAD_DIM)
            q = p_ref[0, 0, pl.ds(r0, TQ), lanes].astype(BF16)
            g = p_ref[3, 0, pl.ds(r0, TQ), lanes]
            p = _softmax_rows(_band_scores(_dot_nt(q, kp_ref[pl.ds(r0, A_KW), lanes]), b_ref[e], r0))
            o = _dot_nn(p.astype(BF16), vp_ref[pl.ds(r0, A_KW), lanes])
            o_ref[0, :, lanes] = o
            gt_ref[0, :, lanes] = (o * _silu(g)).astype(BF16)

    seq, tile = _head_specs(S, "bht", heads)
    kv = pltpu.VMEM((A_PAD + S, heads * HEAD_DIM), BF16)
    return _call(
        body, name=name, grid=(B, HEADS // heads, nt),
        in_specs=[seq, pl.BlockSpec((heads, TQ, A_KW), lambda b, h, t: (h, 0, 0))], out_specs=[tile, tile],
        out_shape=[jax.ShapeDtypeStruct((B, S, W), F32), jax.ShapeDtypeStruct((B, S, W), BF16)],
        scratch_shapes=[kv, kv],
        semantics=("parallel", "parallel", "arbitrary"), args=[proj, bias], ride=ride)


def attn_a_bwd(proj, bias, o, dgated, name, heads=1, ride=None):
    _, B, S, W = proj.shape
    nt = S // TQ

    def body(p_ref, b_ref, o_ref, dgt_ref, dp_ref, db_ref, kp_ref, vp_ref, dk_ref, dv_ref):
        b_, t = pl.program_id(1), pl.program_id(2)

        @pl.when(t == 0)
        def _():
            _fill_padded_kv(p_ref, kp_ref, vp_ref)
            dk_ref[...] = jnp.zeros_like(dk_ref)
            dv_ref[...] = jnp.zeros_like(dv_ref)

        @pl.when((t == 0) & (b_ == 0))
        def _():
            db_ref[...] = jnp.zeros_like(db_ref)

        r0 = pl.multiple_of(t * TQ, TQ)
        rows, win = pl.ds(r0, TQ), pl.ds(r0, A_KW)
        for e in range(heads):
            lanes = slice(e * HEAD_DIM, (e + 1) * HEAD_DIM)
            q = p_ref[0, 0, rows, lanes].astype(BF16)
            g = p_ref[3, 0, rows, lanes]
            kw, vw = kp_ref[win, lanes], vp_ref[win, lanes]
            p = _softmax_rows(_band_scores(_dot_nt(q, kw), b_ref[e], r0))
            do, dg = _gate_bwd(dgt_ref[0, :, lanes], o_ref[0, :, lanes], g)
            do = do.astype(BF16)
            dv_ref[win, lanes] += _dot_tn(p.astype(BF16), do)
            dpr = _dot_nt(do, vw)
            ds = p * (dpr - jnp.sum(p * dpr, axis=-1, keepdims=True))
            db_ref[e] += ds
            ds = (ds * (HEAD_DIM ** -0.5)).astype(BF16)
            dk_ref[win, lanes] += _dot_tn(ds, q)
            dp_ref[0, 0, rows, lanes] = _dot_nn(ds, kw).astype(BF16)
            dp_ref[3, 0, rows, lanes] = dg.astype(BF16)

        @pl.when(t == nt - 1)
        def _():
            dp_ref[1, 0] = dk_ref[A_PAD:, :].astype(BF16)
            dp_ref[2, 0] = dv_ref[A_PAD:, :].astype(BF16)

    seq, tile = _head_specs(S, "hbt", heads)
    bias_spec = pl.BlockSpec((heads, TQ, A_KW), lambda h, b, t: (h, 0, 0))
    kv = pltpu.VMEM((A_PAD + S, heads * HEAD_DIM), BF16)
    acc = pltpu.VMEM((A_PAD + S, heads * HEAD_DIM), F32)
    return _call(
        body, name=name, grid=(HEADS // heads, B, nt), in_specs=[seq, bias_spec, tile, tile],
        out_specs=[seq, bias_spec],
        out_shape=[jax.ShapeDtypeStruct(proj.shape, BF16), jax.ShapeDtypeStruct(bias.shape, F32)],
        scratch_shapes=[kv, kv, acc, acc],
        semantics=("arbitrary", "arbitrary", "arbitrary"), args=[proj, bias, o, dgated], ride=ride)


def band_bias(rel_bias):
    length = TQ + A_KW - 1
    first = REL_CLIP + 1 - TQ
    gen = jnp.concatenate([rel_bias[:, first:],
                           jnp.broadcast_to(rel_bias[:, 2 * REL_CLIP:], (HEADS, length - (N_REL - first)))], axis=1)
    rev = jnp.concatenate([gen[:, ::-1], jnp.zeros((HEADS, 1), rel_bias.dtype)], axis=1)
    sheared = jnp.tile(rev, (1, TQ))[:, :TQ * length].reshape(HEADS, TQ, length)
    i = lax.broadcasted_iota(jnp.int32, (TQ, A_KW), 0)
    j = lax.broadcasted_iota(jnp.int32, (TQ, A_KW), 1)
    first_key = (i // CHUNK) * CHUNK
    in_band = (j >= first_key) & (j < first_key + (LEFT_CHUNKS + 1) * CHUNK)
    return jnp.where(in_band, sheared[:, :, TQ - 1:], NEG_INF)


def _group_scan(a, u, carry, reverse=False):
    row = lax.broadcasted_iota(jnp.int32, u.shape, 0)
    for k in (1, 2, 4):
        shift = 8 - k if reverse else k
        valid = (row < 8 - k) if reverse else (row >= k)
        u_sh = pltpu.roll(u, shift, 0)
        if a is None:
            u = jnp.where(valid, u + u_sh, u)
        else:
            a_sh = pltpu.roll(a, shift, 0)
            u = jnp.where(valid, a * u_sh + u, u)
            a = jnp.where(valid, a * a_sh, a)
    return (u + carry) if a is None else (a * carry + u)


SCAN_UNROLL = 4


def _scan_rows(n_rows, step, carry0, reverse=False):
    groups = n_rows // 8

    def loop(i, carry):
        gi = (groups - 1 - i) if reverse else i
        return step(pl.multiple_of(gi * 8, 8), carry)

    return lax.fori_loop(0, groups, loop, carry0, unroll=SCAN_UNROLL)


def fox_cum_fwd(f_logit, f_bias, name):
    B, S, L = f_logit.shape

    def body(f_ref, b_ref, c_ref):
        z = f_ref[0] + b_ref[...]
        c_ref[0] = jnp.minimum(z, 0.0) - _log1p(jnp.exp(-jnp.abs(z)))

        def step(r0, carry):
            h = _group_scan(None, c_ref[0, pl.ds(r0, 8), :], carry)
            c_ref[0, pl.ds(r0, 8), :] = h
            return h[7:8, :]

        _scan_rows(S, step, jnp.zeros((1, L), F32))

    return pl.pallas_call(
        body, name=name, grid=(B,),
        in_specs=[pl.BlockSpec((1, S, L), lambda b: (b, 0, 0)), pl.BlockSpec((1, L), lambda b: (0, 0))],
        out_specs=pl.BlockSpec((1, S, L), lambda b: (b, 0, 0)),
        out_shape=jax.ShapeDtypeStruct((B, S, L), F32), compiler_params=_params(("parallel",)),
    )(f_logit, f_bias)


def fox_cum_bwd(f_logit, f_bias, dcum, name):
    B, S, L = f_logit.shape

    def body(f_ref, b_ref, d_ref, df_ref, db_ref):
        @pl.when(pl.program_id(0) == 0)
        def _():
            db_ref[...] = jnp.zeros_like(db_ref)

        def step(r0, carry):
            h = _group_scan(None, d_ref[0, pl.ds(r0, 8), :], carry, reverse=True)
            df_ref[0, pl.ds(r0, 8), :] = h
            return h[0:1, :]

        _scan_rows(S, step, jnp.zeros((1, L), F32), reverse=True)
        df = df_ref[0] * _sigmoid(-(f_ref[0] + b_ref[...]))
        df_ref[0] = df
        db_ref[...] += jnp.sum(df, axis=0, keepdims=True)

    seq = pl.BlockSpec((1, S, L), lambda b: (b, 0, 0))
    vec = pl.BlockSpec((1, L), lambda b: (0, 0))
    return pl.pallas_call(
        body, name=name, grid=(B,), in_specs=[seq, vec, seq], out_specs=[seq, vec],
        out_shape=[jax.ShapeDtypeStruct((B, S, L), F32), jax.ShapeDtypeStruct((1, L), F32)],
        compiler_params=_params(("arbitrary",)),
    )(f_logit, f_bias, dcum)


def _head_row(cr, h):
    sub = lax.broadcasted_iota(jnp.int32, cr.shape, 0)
    return jnp.sum(jnp.where(sub == h, cr, 0.0), axis=0, keepdims=True)


def _fox_scores(qk, cc, ck, h, r0):
    lane = lax.broadcasted_iota(jnp.int32, cc.shape, 1)
    cq = jnp.sum(jnp.where(lane == h, cc, 0.0), axis=1, keepdims=True)
    s = qk * (HEAD_DIM ** -0.5) + (cq - ck)
    qpos = r0 + lax.broadcasted_iota(jnp.int32, s.shape, 0)
    kpos = lax.broadcasted_iota(jnp.int32, s.shape, 1)
    return jnp.where(kpos <= qpos, s, NEG_INF)


KEY_STEP = 256


def _by_causal_width(t, S, fn):
    per = KEY_STEP // TQ
    for c in range(S // KEY_STEP):
        pl.when(t // per == c)(functools.partial(fn, (c + 1) * KEY_STEP))


def fox_fwd(proj, cum_col, cum_row, name, ride=None):
    _, B, S, W = proj.shape
    nt = S // TQ

    def body(p_ref, cc_ref, cr_ref, o_ref, gt_ref, k_ref, v_ref):
        h, t = pl.program_id(1), pl.program_id(2)

        @pl.when(t == 0)
        def _():
            k_ref[...] = p_ref[1, 0].astype(BF16)
            v_ref[...] = p_ref[2, 0].astype(BF16)

        r0 = pl.multiple_of(t * TQ, TQ)
        q = p_ref[0, 0, pl.ds(r0, TQ), :].astype(BF16)
        g = p_ref[3, 0, pl.ds(r0, TQ), :]

        def tile_out(width):
            ck = _head_row(cr_ref[0, :, 0:width], h)
            p = _softmax_rows(_fox_scores(_dot_nt(q, k_ref[0:width, :]), cc_ref[0], ck, h, r0))
            o = _dot_nn(p.astype(BF16), v_ref[0:width, :])
            o_ref[0] = o
            gt_ref[0] = (o * _silu(g)).astype(BF16)

        _by_causal_width(t, S, tile_out)

    seq, tile = _head_specs(S, "bht")
    return _call(
        body, name=name, grid=(B, HEADS, nt),
        in_specs=[seq, pl.BlockSpec((1, TQ, cum_col.shape[2]), lambda b, h, t: (b, t, 0)),
                  pl.BlockSpec((1, HEADS, S), lambda b, h, t: (b, 0, 0))],
        out_specs=[tile, tile],
        out_shape=[jax.ShapeDtypeStruct((B, S, W), F32), jax.ShapeDtypeStruct((B, S, W), BF16)],
        scratch_shapes=[pltpu.VMEM((S, HEAD_DIM), BF16), pltpu.VMEM((S, HEAD_DIM), BF16)],
        semantics=("parallel", "parallel", "arbitrary"), args=[proj, cum_col, cum_row], ride=ride)


def fox_bwd(proj, cum_col, cum_row, o, dgated, name, ride=None):
    _, B, S, W = proj.shape
    nt = S // TQ

    def body(p_ref, cc_ref, cr_ref, o_ref, dgt_ref, dp_ref, dc_ref, k_ref, v_ref, dk_ref, dv_ref):
        h, t = pl.program_id(1), pl.program_id(2)

        @pl.when(t == 0)
        def _():
            k_ref[...] = p_ref[1, 0].astype(BF16)
            v_ref[...] = p_ref[2, 0].astype(BF16)
            dk_ref[...] = jnp.zeros_like(dk_ref)
            dv_ref[...] = jnp.zeros_like(dv_ref)
            dc_ref[...] = jnp.zeros_like(dc_ref)

        r0 = pl.multiple_of(t * TQ, TQ)
        rows = pl.ds(r0, TQ)
        q = p_ref[0, 0, rows, :].astype(BF16)
        g = p_ref[3, 0, rows, :]
        do, dg = _gate_bwd(dgt_ref[0], o_ref[0], g)
        do = do.astype(BF16)
        dp_ref[3, 0, rows, :] = dg.astype(BF16)

        def tile_grads(width):
            k, v = k_ref[0:width, :], v_ref[0:width, :]
            ck = _head_row(cr_ref[0, :, 0:width], h)
            p = _softmax_rows(_fox_scores(_dot_nt(q, k), cc_ref[0], ck, h, r0))
            dv_ref[0:width, :] += _dot_tn(p.astype(BF16), do)
            dpr = _dot_nt(do, v)
            ds = p * (dpr - jnp.sum(p * dpr, axis=-1, keepdims=True))
            dc_ref[0, 0, :, 0:width] += jnp.sum(ds, axis=0, keepdims=True)
            ds = (ds * (HEAD_DIM ** -0.5)).astype(BF16)
            dk_ref[0:width, :] += _dot_tn(ds, q)
            dp_ref[0, 0, rows, :] = _dot_nn(ds, k).astype(BF16)

        _by_causal_width(t, S, tile_grads)

        @pl.when(t == nt - 1)
        def _():
            dp_ref[1, 0] = dk_ref[...].astype(BF16)
            dp_ref[2, 0] = dv_ref[...].astype(BF16)

    seq, tile = _head_specs(S, "bht")
    return _call(
        body, name=name, grid=(B, HEADS, nt),
        in_specs=[seq, pl.BlockSpec((1, TQ, cum_col.shape[2]), lambda b, h, t: (b, t, 0)),
                  pl.BlockSpec((1, HEADS, S), lambda b, h, t: (b, 0, 0)), tile, tile],
        out_specs=[seq, pl.BlockSpec((1, 1, 1, S), lambda b, h, t: (b, h, 0, 0))],
        out_shape=[jax.ShapeDtypeStruct(proj.shape, BF16), jax.ShapeDtypeStruct((B, HEADS, 1, S), F32)],
        scratch_shapes=[pltpu.VMEM((S, HEAD_DIM), BF16), pltpu.VMEM((S, HEAD_DIM), BF16),
                        pltpu.VMEM((S, HEAD_DIM), F32), pltpu.VMEM((S, HEAD_DIM), F32)],
        semantics=("parallel", "parallel", "arbitrary"), args=[proj, cum_col, cum_row, o, dgated], ride=ride)


RG_ROWS = 512


def _rg_gates(xc, wa_ref, ba_ref, wx_ref, bx_ref, lam_ref):
    xcb = xc.astype(BF16)
    r = _sigmoid(_dot_nn(xcb, wa_ref[0]) + ba_ref[...])
    i = _sigmoid(_dot_nn(xcb, wx_ref[0]) + bx_ref[...])
    sp = _softplus(-lam_ref[...])
    log_a = (-RG_C * sp) * r
    a = jnp.exp(log_a)
    m = jnp.sqrt(-jnp.tanh(log_a) * (a * a + 1.0))
    return xcb, r, i, sp, a, m


def _rg_specs(B, S, rows, order):
    nc = S // rows

    def idx(fn):
        def index_map(*ids):
            v = dict(zip(order.lower(), ids))
            c = (nc - 1 - v["c"]) if "C" in order else v["c"]
            return fn(v["b"], v["d"], c)
        return index_map

    return dict(
        proj=pl.BlockSpec((2, 1, rows, RG_COLS), idx(lambda b, d, c: (0, b, c, d))),
        act=pl.BlockSpec((1, rows, RG_COLS), idx(lambda b, d, c: (b, c, d))),
        taps=pl.BlockSpec((CONV_WIDTH, RG_COLS), idx(lambda b, d, c: (0, d))),
        vec=pl.BlockSpec((1, RG_COLS), idx(lambda b, d, c: (0, d))),
        gate=pl.BlockSpec((1, RG_COLS, RG_COLS), idx(lambda b, d, c: (d, 0, 0))),
    )


def rglru_fwd(proj, conv_w, conv_b, wa, ba, wx, bx, lam, name, rows=RG_ROWS, ride=None):
    _, B, S, _ = proj.shape
    rows = min(rows, S)
    sp_ = _rg_specs(B, S, rows, "bdc")

    def body(p_ref, cw_ref, cb_ref, wa_ref, ba_ref, wx_ref, bx_ref, lam_ref,
             xc_ref, hs_ref, hp_ref, gt_ref, ext_ref, a_ref, u_ref, xcar_ref, hcar_ref):
        @pl.when(pl.program_id(2) == 0)
        def _():
            xcar_ref[...] = jnp.zeros_like(xcar_ref)
            hcar_ref[...] = jnp.zeros_like(hcar_ref)

        xr = p_ref[0, 0]
        ext_ref[0:8, :] = xcar_ref[...]
        ext_ref[8:, :] = xr
        xcar_ref[...] = xr[rows - 8:, :]
        xc = ext_ref[pl.ds(5, rows), :] * cw_ref[0:1, :]
        xc = xc + ext_ref[pl.ds(6, rows), :] * cw_ref[1:2, :]
        xc = xc + ext_ref[pl.ds(7, rows), :] * cw_ref[2:3, :]
        xc = xc + xr * cw_ref[3:4, :] + cb_ref[...]
        xc_ref[0] = xc
        _, _, i, _, a, m = _rg_gates(xc, wa_ref, ba_ref, wx_ref, bx_ref, lam_ref)
        a_ref[...] = a
        u_ref[...] = m * (i * xc)

        def step(r0, carry):
            h = _group_scan(a_ref[pl.ds(r0, 8), :], u_ref[pl.ds(r0, 8), :], carry)
            row = lax.broadcasted_iota(jnp.int32, h.shape, 0)
            hs_ref[0, pl.ds(r0, 8), :] = h
            hp_ref[0, pl.ds(r0, 8), :] = jnp.where(row == 0, carry, pltpu.roll(h, 1, 0))
            return h[7:8, :]

        hcar_ref[0:1, :] = _scan_rows(rows, step, hcar_ref[0:1, :])
        gt_ref[0] = (hs_ref[0] * _silu(p_ref[1, 0])).astype(BF16)

    act = jax.ShapeDtypeStruct((B, S, RG_WIDTH), F32)
    return _call(
        body, name=name, grid=(B, RG_GROUPS, S // rows),
        in_specs=[sp_["proj"], sp_["taps"], sp_["vec"], sp_["gate"], sp_["vec"], sp_["gate"], sp_["vec"], sp_["vec"]],
        out_specs=[sp_["act"]] * 4,
        out_shape=[act, act, act, jax.ShapeDtypeStruct((B, S, RG_WIDTH), BF16)],
        scratch_shapes=[pltpu.VMEM((rows + 8, RG_COLS), F32), pltpu.VMEM((rows, RG_COLS), F32),
                        pltpu.VMEM((rows, RG_COLS), F32), pltpu.VMEM((8, RG_COLS), F32), pltpu.VMEM((8, RG_COLS), F32)],
        semantics=("parallel", "parallel", "arbitrary"), args=[proj, conv_w, conv_b, wa, ba, wx, bx, lam], ride=ride)


def rglru_bwd(proj, xc, hs, hprev, dgated, conv_w, wa, ba, wx, bx, lam, name, rows=RG_ROWS, ride=None):
    _, B, S, _ = proj.shape
    rows = min(rows, S)
    sp_ = _rg_specs(B, S, rows, "dbC")

    def body(p_ref, xc_ref, hs_ref, hp_ref, dgt_ref, cw_ref, wa_ref, ba_ref, wx_ref, bx_ref, lam_ref,
             dp_ref, dcw_ref, dcb_ref, dwa_ref, dba_ref, dwx_ref, dbx_ref, dlam_ref,
             ext_ref, c_ref, l_ref, acar_ref, lcar_ref, dcar_ref):
        b_, c_ = pl.program_id(1), pl.program_id(2)

        @pl.when(c_ == 0)
        def _():
            acar_ref[...] = jnp.zeros_like(acar_ref)
            lcar_ref[...] = jnp.zeros_like(lcar_ref)
            dcar_ref[...] = jnp.zeros_like(dcar_ref)

        @pl.when((c_ == 0) & (b_ == 0))
        def _():
            for ref in (dcw_ref, dcb_ref, dwa_ref, dba_ref, dwx_ref, dbx_ref, dlam_ref):
                ref[...] = jnp.zeros_like(ref)

        xr, g = p_ref[0, 0], p_ref[1, 0]
        xc_v = xc_ref[0]
        xcb, r, i, sp, a, m = _rg_gates(xc_v, wa_ref, ba_ref, wx_ref, bx_ref, lam_ref)
        dhs, dg = _gate_bwd(dgt_ref[0], hs_ref[0], g)
        dp_ref[1, 0] = dg.astype(BF16)

        ext_ref[0:rows, :] = a
        ext_ref[rows:, :] = acar_ref[...]
        acar_ref[...] = a[0:8, :]
        c_ref[...] = ext_ref[pl.ds(1, rows), :]
        l_ref[...] = dhs

        def step(r0, carry):
            lam_g = _group_scan(c_ref[pl.ds(r0, 8), :], l_ref[pl.ds(r0, 8), :], carry, reverse=True)
            l_ref[pl.ds(r0, 8), :] = lam_g
            return lam_g[0:1, :]

        lcar_ref[0:1, :] = _scan_rows(rows, step, lcar_ref[0:1, :], reverse=True)
        du = l_ref[...]
        da = du * hp_ref[0]
        dlog_a = da * a - (du * (i * xc_v)) * (a * a / m)
        dr = dlog_a * (-RG_C * sp)
        dsp = jnp.sum(dlog_a * (-RG_C * r), axis=0, keepdims=True)
        dlam_ref[...] += dsp * (-_sigmoid(-lam_ref[...]))
        dpa = dr * (r * (1.0 - r))
        dpx = (du * (m * xc_v)) * (i * (1.0 - i))
        dba_ref[...] += jnp.sum(dpa, axis=0, keepdims=True)
        dbx_ref[...] += jnp.sum(dpx, axis=0, keepdims=True)
        dpa, dpx = dpa.astype(BF16), dpx.astype(BF16)
        dwa_ref[0] += _dot_tn(xcb, dpa)
        dwx_ref[0] += _dot_tn(xcb, dpx)
        dxc = du * (m * i) + _dot_nt(dpa, wa_ref[0]) + _dot_nt(dpx, wx_ref[0])

        dcb_ref[...] += jnp.sum(dxc, axis=0, keepdims=True)
        ext_ref[0:rows, :] = dxc
        ext_ref[rows:, :] = dcar_ref[...]
        dcar_ref[...] = dxc[0:8, :]
        dxr = jnp.zeros_like(dxc)
        for k in range(CONV_WIDTH):
            tap = CONV_WIDTH - 1 - k
            ahead = dxc if k == 0 else ext_ref[pl.ds(k, rows), :]
            dxr = dxr + ahead * cw_ref[tap:tap + 1, :]
            dcw_ref[tap:tap + 1, :] += jnp.sum(xr * ahead, axis=0, keepdims=True)
        dp_ref[0, 0] = dxr.astype(BF16)

    vec = jax.ShapeDtypeStruct((1, RG_WIDTH), F32)
    gate = jax.ShapeDtypeStruct((RG_GROUPS, RG_COLS, RG_COLS), F32)
    return _call(
        body, name=name, grid=(RG_GROUPS, B, S // rows),
        in_specs=[sp_["proj"], sp_["act"], sp_["act"], sp_["act"], sp_["act"], sp_["taps"],
                  sp_["gate"], sp_["vec"], sp_["gate"], sp_["vec"], sp_["vec"]],
        out_specs=[sp_["proj"], sp_["taps"], sp_["vec"], sp_["gate"], sp_["vec"], sp_["gate"], sp_["vec"], sp_["vec"]],
        out_shape=[jax.ShapeDtypeStruct(proj.shape, BF16), jax.ShapeDtypeStruct((CONV_WIDTH, RG_WIDTH), F32), vec,
                   gate, vec, gate, vec, vec],
        scratch_shapes=[pltpu.VMEM((rows + 8, RG_COLS), F32), pltpu.VMEM((rows, RG_COLS), F32),
                        pltpu.VMEM((rows, RG_COLS), F32), pltpu.VMEM((8, RG_COLS), F32),
                        pltpu.VMEM((8, RG_COLS), F32), pltpu.VMEM((8, RG_COLS), F32)],
        semantics=("arbitrary", "arbitrary", "arbitrary"),
        args=[proj, xc, hs, hprev, dgated, conv_w, wa, ba, wx, bx, lam], ride=ride)


def block_diag_gates(w):
    per = RG_COLS // RG_BLOCK
    w4 = w.reshape(RG_GROUPS, per, RG_BLOCK, RG_BLOCK)
    return jnp.einsum("dipq,ij->dipjq", w4, jnp.eye(per, dtype=w.dtype)).reshape(RG_GROUPS, RG_COLS, RG_COLS)


def block_diag_gates_t(dw):
    per = RG_COLS // RG_BLOCK
    dw6 = dw.reshape(RG_GROUPS, per, RG_BLOCK, per, RG_BLOCK)
    return jnp.stack([dw6[:, i, :, i, :] for i in range(per)], axis=1).reshape(RG_BLOCKS, RG_BLOCK, RG_BLOCK)


def adamw(w, parts, m, v, name, layer=0, prev=None, part_row0=0, row_tile=ROW_TILE):
    L, R, C = w.shape
    n_parts = parts.shape[0]
    br = row_tile if R % row_tile == 0 else R

    def body(w_ref, p_ref, m_ref, v_ref, *refs):
        g_ref, d_ref, nm_ref, nv_ref = refs[-4:]
        g = p_ref[0].astype(F32)
        for k in range(1, n_parts):
            g = g + p_ref[k].astype(F32)
        nm = ADAM_B1 * m_ref[0] + (1.0 - ADAM_B1) * g
        nv = ADAM_B2 * v_ref[0] + (1.0 - ADAM_B2) * (g * g)
        m_hat = nm / (1.0 - ADAM_B1 ** ADAM_STEP)
        v_hat = nv / (1.0 - ADAM_B2 ** ADAM_STEP)
        g_ref[0] = g
        d_ref[0] = -ADAM_LR * (m_hat / (jnp.sqrt(v_hat) + ADAM_EPS) + ADAM_WD * w_ref[0])
        nm_ref[0] = nm
        nv_ref[0] = nv

    slab = pl.BlockSpec((1, br, C), lambda i: (layer, i, 0))
    out = jax.ShapeDtypeStruct((L, R, C), F32)
    carried = [] if prev is None else list(prev)
    return _call(
        body, name=name, grid=(R // br,),
        in_specs=[slab, pl.BlockSpec((n_parts, br, C), lambda i: (0, part_row0 // br + i, 0)), slab, slab]
        + [pl.BlockSpec(memory_space=pl.ANY)] * len(carried),
        out_specs=[slab] * 4, out_shape=[out] * 4, semantics=("parallel",), args=[w, parts, m, v] + carried,
        aliases={4 + k: k for k in range(len(carried))})


def _seq(a, B):
    return a.reshape(a.shape[:-2] + (B, a.shape[-2] // B, a.shape[-1]))


def _flat(a):
    return a.reshape(a.shape[:-3] + (a.shape[-3] * a.shape[-2], a.shape[-1]))


def _tiles(w, which, **default):
    return dict(default, **w.get("tiles", {}).get(which, {}))


def mixer_a_fwd(h, w, B, tag, rides):
    proj = matmul(h, w["w_in"], mode="nn", out_dtype=F32, name=f"{tag}_proj", out_slabs=4,
                  ride=rides.pop(f"{tag}_proj", None), **_tiles(w, "proj"))
    o, gated = attn_a_fwd(_seq(proj, B), w["bias"], f"{tag}_attn", heads=w.get("attn_heads", 1),
                          ride=rides.pop(f"{tag}_attn", None))
    return _flat(gated), dict(proj=proj, o=o)


def mixer_a_bwd(dgated, w, saved, B, tag, rides):
    dproj, dbias = attn_a_bwd(_seq(saved["proj"], B), w["bias"], saved["o"], _seq(dgated, B), f"{tag}_attn_bwd",
                              heads=w.get("attn_heads", 1), ride=rides.pop(f"{tag}_attn_bwd", None))
    return _flat(dproj), dict(bias=dbias)


def mixer_b_fwd(h, w, B, tag, rides):
    proj = matmul(h, w["w_in"], mode="nn", out_dtype=F32, name=f"{tag}_proj", out_slabs=2, bn=RG_COLS,
                  ride=rides.pop(f"{tag}_proj", None))
    xc, hs, hprev, gated = rglru_fwd(_seq(proj, B), w["conv_w"], w["conv_b"], w["wa"], w["ba"], w["wx"], w["bx"],
                                     w["lam"], f"{tag}_rglru", ride=rides.pop(f"{tag}_rglru", None))
    return _flat(gated), dict(proj=proj, xc=xc, hs=hs, hprev=hprev)


def mixer_b_bwd(dgated, w, saved, B, tag, rides):
    dproj, dcw, dcb, dwa, dba, dwx, dbx, dlam = rglru_bwd(
        _seq(saved["proj"], B), saved["xc"], saved["hs"], saved["hprev"], _seq(dgated, B),
        w["conv_w"], w["wa"], w["ba"], w["wx"], w["bx"], w["lam"], f"{tag}_rglru_bwd",
        ride=rides.pop(f"{tag}_rglru_bwd", None))
    return _flat(dproj), dict(conv_w=dcw, conv_b=dcb, wa=dwa, ba=dba, wx=dwx, bx=dbx, lam=dlam)


def mixer_c_fwd(h, w, B, tag, rides):
    proj = matmul(h, w["w_in"], mode="nn", out_dtype=F32, name=f"{tag}_proj", out_slabs=4,
                  ride=rides.pop(f"{tag}_proj", None), **_tiles(w, "proj"))
    f_logit = matmul(h, w["w_f"], mode="nn", out_dtype=F32, name=f"{tag}_fproj")
    cum = fox_cum_fwd(_seq(f_logit, B), w["f_bias"], f"{tag}_cum")
    cum_row = cum[:, :, :HEADS].transpose(0, 2, 1)
    o, gated = fox_fwd(_seq(proj, B), cum, cum_row, f"{tag}_attn", ride=rides.pop(f"{tag}_attn", None))
    return _flat(gated), dict(proj=proj, o=o, f_logit=f_logit, cum=cum, cum_row=cum_row)


def mixer_c_bwd(dgated, w, saved, B, tag, rides):
    dproj, dck = fox_bwd(_seq(saved["proj"], B), saved["cum"], saved["cum_row"], saved["o"], _seq(dgated, B),
                         f"{tag}_attn_bwd", ride=rides.pop(f"{tag}_attn_bwd", None))
    S = dck.shape[-1]
    dcum = jnp.pad(-dck.reshape(B, HEADS, S).transpose(0, 2, 1), ((0, 0), (0, 0), (0, HEAD_DIM - HEADS)))
    df, dfb = fox_cum_bwd(_seq(saved["f_logit"], B), w["f_bias"], dcum, f"{tag}_cum_bwd")
    return _flat(dproj), dict(f_bias=dfb, df=_flat(df).astype(BF16))


MIXERS = {"a": (mixer_a_fwd, mixer_a_bwd), "b": (mixer_b_fwd, mixer_b_bwd), "c": (mixer_c_fwd, mixer_c_bwd)}
LAYER_KINDS = "abca"


def local_step(x, target, norm_pre, norm_post, get_layer, rides, on_grads):
    B, S, Dm = x.shape
    n_layers = len(LAYER_KINDS)
    xs = [x.reshape(B * S, Dm)]
    saved, layers = [], []
    h = prenorm_fwd(xs[0], norm_pre[0:1], "l0a_prenorm")
    for li, kind in enumerate(LAYER_KINDS):
        tag = f"l{li}{kind}"
        w = get_layer(li)
        gated, sv = MIXERS[kind][0](h, w, B, tag, rides)
        if callable(w["w_out"]):
            w["w_out"] = w["w_out"]()
        y = matmul(gated, w["w_out"], mode="nn", out_dtype=F32, name=f"{tag}_out", ride=rides.pop(f"{tag}_out", None))
        saved.append(dict(sv, h=h, gated=gated, y=y))
        layers.append(w)
        if li + 1 < n_layers:
            x_new, h = postnorm_prenorm_fwd(xs[-1], y, norm_post[li:li + 1], norm_pre[li + 1:li + 2],
                                            f"{tag}_postnorm")
            xs.append(x_new)
    loss, dx = postnorm_loss(xs[-1], y, norm_post[n_layers - 1:], target.reshape(B * S, Dm), "loss")

    for li in reversed(range(n_layers)):
        kind, w, sv = LAYER_KINDS[li], layers[li], saved[li]
        tag = f"l{li}{kind}"
        dy, dg_post = postnorm_bwd(sv["y"], norm_post[li:li + 1], dx, f"{tag}_postnorm_bwd")
        on_grads(li, "norm_post", dg_post)
        on_grads(li, "w_out", matmul(sv["gated"], dy, mode="tn", out_dtype=BF16, name=f"{tag}_dwout",
                                     ride=rides.pop(f"{tag}_dwout", None)))
        dgated = matmul(dy, w["w_out"], mode="nt", out_dtype=F32, name=f"{tag}_dgated",
                        ride=rides.pop(f"{tag}_dgated", None))
        dproj, gw = MIXERS[kind][1](dgated, w, sv, B, tag, rides)
        df = gw.pop("df", None)
        for name, value in gw.items():
            on_grads(li, name, value)
        parts = w.get("dwin_parts", [(0, 1, 1)])
        for i, m_part in enumerate(parts):
            suffix = f"_{i}" if len(parts) > 1 else ""
            on_grads(li, "w_in" + suffix,
                     matmul(sv["h"], dproj, mode="tn", out_dtype=BF16, name=f"{tag}_dwin{suffix}", m_part=m_part,
                            out_slabs=w["grad_slabs"], ride=rides.pop(f"{tag}_dwin{suffix}", None),
                            **_tiles(w, "dwin")))
        if df is not None:
            on_grads(li, "w_f", matmul(sv["h"], df, mode="tn", out_dtype=BF16, name=f"{tag}_dwf"))
        dhs = [matmul(dproj, w["w_in"], mode="nt", out_dtype=F32, name=f"{tag}_dh",
                      ride=rides.pop(f"{tag}_dh", None), **_tiles(w, "dh"))]
        if df is not None:
            dhs.append(matmul(df, w["w_f"], mode="nt", out_dtype=F32, name=f"{tag}_dhf"))
        dx, dg_pre = prenorm_bwd(xs[li], norm_pre[li:li + 1], dhs, dx, f"{tag}_prenorm_bwd")
        on_grads(li, "norm_pre", dg_pre)
    assert not rides, list(rides)
    return loss, dx.reshape(B, S, Dm)


WEIGHTS = ("norm_pre", "norm_post", "a_w_in", "a_rel_bias", "a_w_out", "b_w_in", "b_conv_w", "b_conv_b",
           "b_gate_a_w", "b_gate_a_b", "b_gate_x_w", "b_gate_x_b", "b_lambda", "b_w_out", "c_w_in", "c_f_bias",
           "c_w_out")
C_SHARD = (4 * D_MODEL + HEADS) // N_DEV


def _rows(gathered):
    return gathered.reshape(gathered.shape[0] * gathered.shape[1], gathered.shape[2])


def layer_a(w_in, w_out, rel_bias):
    return dict(w_in=w_in, w_out=w_out if callable(w_out) else _rows(w_out), bias=band_bias(rel_bias),
                grad_slabs=N_DEV)


def layer_b(w_in, w_out, conv_w, small):
    return dict(
        w_in=w_in, w_out=_rows(w_out), grad_slabs=N_DEV,
        conv_w=conv_w.transpose(1, 0, 2).reshape(CONV_WIDTH, RG_WIDTH),
        conv_b=small["b_conv_b"], lam=small["b_lambda"],
        wa=block_diag_gates(small["b_gate_a_w"][0]).astype(BF16), ba=small["b_gate_a_b"].reshape(1, RG_WIDTH),
        wx=block_diag_gates(small["b_gate_x_w"][0]).astype(BF16), bx=small["b_gate_x_b"].reshape(1, RG_WIDTH))


def layer_c(w_in, w_out, small):
    full = w_in.transpose(1, 0, 2).reshape(D_MODEL, N_DEV * C_SHARD)
    return dict(w_in=full[:, :4 * D_MODEL], w_f=jnp.pad(full[:, 4 * D_MODEL:], ((0, 0), (0, HEAD_DIM - HEADS))),
                w_out=_rows(w_out), grad_slabs=1,
                f_bias=jnp.pad(small["c_f_bias"], ((0, 0), (0, HEAD_DIM - HEADS))))


def c_w_in_blocks(dmain, df):
    full = jnp.concatenate([dmain, df[:, :HEADS].astype(dmain.dtype)], axis=1)
    return full.reshape(D_MODEL, N_DEV, C_SHARD).transpose(1, 0, 2)


def _row_blocks(g):
    return g.reshape(N_DEV, g.shape[0] // N_DEV, g.shape[1])


PACK_LANES = 128
PACK_ALIGN = 8 * PACK_LANES


def pack(parts):
    flat = []
    for p in parts:
        n = p.size
        flat.append(jnp.pad(p.reshape(n), (0, -n % PACK_ALIGN)).reshape(-1, PACK_LANES))
    rows = sum(f.shape[0] for f in flat)
    flat.append(jnp.zeros((-rows % ROW_TILE, PACK_LANES), F32))
    return jnp.concatenate(flat, axis=0)


def unpack(packed, shapes):
    out, row = [], 0
    for shape in shapes:
        n = 1
        for s in shape:
            n *= s
        n_rows = (n + PACK_ALIGN - 1) // PACK_ALIGN * 8
        out.append(packed[row:row + n_rows].reshape(-1)[:n].reshape(shape))
        row += n_rows
    return out


LATE = (("a_rel_bias", slice(0, 1)), ("norm_pre", slice(0, 2)), ("norm_post", slice(0, 1)))
EARLY = (("a_rel_bias", slice(1, 2)), ("norm_pre", slice(2, 4)), ("norm_post", slice(1, 4)),
         ("b_conv_b", slice(None)), ("b_gate_a_w", slice(None)), ("b_gate_a_b", slice(None)),
         ("b_gate_x_w", slice(None)), ("b_gate_x_b", slice(None)), ("b_lambda", slice(None)),
         ("c_f_bias", slice(None)))


def _pieces(tree, pieces):
    return [tree[name][sl] for name, sl in pieces]


def kernel(x, norm_pre, norm_post, a_w_in, a_rel_bias, a_w_out, b_w_in, b_conv_w, b_conv_b, b_gate_a_w, b_gate_a_b, b_gate_x_w, b_gate_x_b, b_lambda, b_w_out, c_w_in, c_f_bias, c_w_out, loss_target, m_norm_pre, m_norm_post, m_a_w_in, m_a_rel_bias, m_a_w_out, m_b_w_in, m_b_conv_w, m_b_conv_b, m_b_gate_a_w, m_b_gate_a_b, m_b_gate_x_w, m_b_gate_x_b, m_b_lambda, m_b_w_out, m_c_w_in, m_c_f_bias, m_c_w_out, v_norm_pre, v_norm_post, v_a_w_in, v_a_rel_bias, v_a_w_out, v_b_w_in, v_b_conv_w, v_b_conv_b, v_b_gate_a_w, v_b_gate_a_b, v_b_gate_x_w, v_b_gate_x_b, v_b_lambda, v_b_w_out, v_c_w_in, v_c_f_bias, v_c_w_out):
    args = dict(locals())
    w = {n: args[n] for n in WEIGHTS}
    m = {n: args["m_" + n] for n in WEIGHTS}
    v = {n: args["v_" + n] for n in WEIGHTS}

    a_in, a_out = a_w_in.astype(BF16), a_w_out.astype(BF16)
    gather_a0 = Ride([a_in[0]], scatter=False, via_sibling=True)
    in_l0_proj = Ride([b_w_in[0].astype(BF16), b_conv_w[0], a_out[0]], scatter=False, via_sibling=True)
    in_l0_attn = Ride([c_w_in[0].astype(BF16), b_w_out[0].astype(BF16)], scatter=False, via_sibling=True)
    in_l1_proj = Ride([c_w_out[0].astype(BF16)], scatter=False)
    in_l2_proj = Ride([a_out[1]], scatter=False)
    in_l2_attn = Ride([a_in[1]], scatter=False, via_sibling=True)
    exchange(gather_a0, "gather_l0")
    rides = {"l0a_proj": in_l0_proj, "l0a_attn": in_l0_attn, "l1b_proj": in_l1_proj, "l2c_proj": in_l2_proj,
             "l2c_attn": in_l2_attn}

    def get_layer(li):
        if li == 0:
            return dict(layer_a(gather_a0.out[0], lambda: _rows(in_l0_proj.out[2]), a_rel_bias[0]),
                        dwin_parts=[(0, 1, 4), (1, 1, 4), (2, 2, 4)])
        if li == 1:
            return layer_b(in_l0_proj.out[0], in_l0_attn.out[1], in_l0_proj.out[1], w)
        if li == 2:
            return dict(layer_c(in_l0_attn.out[0], in_l1_proj.out[0], w), tiles=dict(proj=dict(bn=2048)))
        return dict(layer_a(in_l2_attn.out[0], in_l2_proj.out[0], a_rel_bias[1]), attn_heads=2)

    grads = [dict() for _ in LAYER_KINDS]
    scatters = {}

    def rel_bias_grad(j, dbias):
        return jax.vjp(band_bias, a_rel_bias[j])[1](dbias)[0][None]

    def early_partial():
        gb, gc = grads[1], grads[2]
        tree = dict(
            a_rel_bias=jnp.concatenate([jnp.zeros((1, HEADS, N_REL), F32), rel_bias_grad(1, grads[3]["bias"])]),
            norm_pre=jnp.concatenate([jnp.zeros((2, D_MODEL), F32)] + [grads[li]["norm_pre"] for li in (2, 3)]),
            norm_post=jnp.concatenate([jnp.zeros((1, D_MODEL), F32)] + [grads[li]["norm_post"] for li in (1, 2, 3)]),
            b_conv_b=gb["conv_b"], b_lambda=gb["lam"],
            b_gate_a_w=block_diag_gates_t(gb["wa"])[None], b_gate_a_b=gb["ba"].reshape(1, RG_BLOCKS, RG_BLOCK),
            b_gate_x_w=block_diag_gates_t(gb["wx"])[None], b_gate_x_b=gb["bx"].reshape(1, RG_BLOCKS, RG_BLOCK),
            c_f_bias=gc["f_bias"][:, :HEADS])
        return pack(_pieces(tree, EARLY))

    def send(key, host, blocks, scatter=True, via_sibling=False):
        ride = rides.setdefault(host, Ride([], scatter, via_sibling))
        assert (ride.scatter, ride.via_sibling) == (scatter, via_sibling)
        scatters[key] = (ride, len(ride.arrs))
        ride.arrs.append(blocks)

    def on_grads(li, name, value):
        g = grads[li]
        g[name] = value
        if (li, name) == (3, "w_out"):
            send("a1_out", "l3a_attn_bwd", _row_blocks(value))
        elif (li, name) == (3, "w_in"):
            send("a1_in", "l2c_attn_bwd", value)
        elif (li, name) == (2, "w_out"):
            send("c_out", "l2c_attn_bwd", _row_blocks(value))
        elif (li, name) == (2, "w_f"):
            blocks = c_w_in_blocks(g["w_in"], value)
            send("c_in_0", "l2c_dh", blocks[:, :D_MODEL // 2])
            send("c_in_1", "l1b_rglru_bwd", blocks[:, D_MODEL // 2:])
        elif (li, name) == (1, "w_out"):
            send("b_out", "l1b_dh", _row_blocks(value))
        elif (li, name) == (1, "w_in"):
            send("b_in", "l0a_attn_bwd", value)
            send("b_conv", "l0a_attn_bwd",
                 g["conv_w"].reshape(CONV_WIDTH, N_DEV, RG_WIDTH // N_DEV).transpose(1, 0, 2))
        elif (li, name) == (1, "lam"):
            send("early", "l1b_dwin", early_partial(), scatter=False, via_sibling=True)
        elif (li, name) == (0, "w_out"):
            send("a0_out", "l0a_attn_bwd", _row_blocks(value))
        elif (li, name) == (0, "w_in_0"):
            send("a0_in_0", "l0a_dwin_1", value)
        elif (li, name) == (0, "w_in_1"):
            send("a0_in_1", "l0a_dwin_2", value)
        elif (li, name) == (0, "w_in_2"):
            send("a0_in_2", "l0a_dh", value)

    loss, grad_x = local_step(x, loss_target, norm_pre, norm_post, get_layer, rides, on_grads)
    late_tree = dict(a_rel_bias=rel_bias_grad(0, grads[0]["bias"]), norm_post=grads[0]["norm_post"],
                     norm_pre=jnp.concatenate([grads[0]["norm_pre"], grads[1]["norm_pre"]]))
    late_parts = exchange(Ride([pack([late_tree[n] for n, _ in LATE])], scatter=False), "gather_late_grads")[0]

    def sharded(name, slab_parts):
        shape = w[name].shape
        slabs = (len(slab_parts), shape[0] * shape[1] // len(slab_parts), shape[2])
        outs = None
        for j, (parts, row0) in enumerate(slab_parts):
            outs = adamw(w[name].reshape(slabs), parts, m[name].reshape(slabs), v[name].reshape(slabs),
                         f"adamw_{name}_{j}", layer=j, prev=outs, part_row0=row0)
        return [o.reshape(shape) for o in outs]

    def received(key):
        ride, position = scatters[key]
        return ride.out[position]

    res = dict(
        a_w_in=sharded("a_w_in", [(received("a0_in_0"), 0), (received("a0_in_1"), 0), (received("a0_in_2"), 0),
                                  (received("a0_in_2"), D_MODEL // 4)]
                       + [(received("a1_in"), q * D_MODEL // 4) for q in range(4)]),
        a_w_out=sharded("a_w_out", [(received("a0_out"), 0), (received("a1_out"), 0)]),
        b_w_in=sharded("b_w_in", [(received("b_in"), 0)]),
        b_w_out=sharded("b_w_out", [(received("b_out"), 0)]),
        b_conv_w=sharded("b_conv_w", [(received("b_conv"), 0)]),
        c_w_in=sharded("c_w_in", [(received("c_in_0"), 0), (received("c_in_1"), 0)]),
        c_w_out=sharded("c_w_out", [(received("c_out"), 0)]))

    packed = {}
    for label, pieces, parts in (("early", EARLY, received("early")), ("late", LATE, late_parts)):
        outs = adamw(pack(_pieces(w, pieces))[None], parts, pack(_pieces(m, pieces))[None],
                     pack(_pieces(v, pieces))[None], f"adamw_replicated_{label}")
        shapes = [w[n][sl].shape for n, sl in pieces]
        packed[label] = [dict(zip([n for n, _ in pieces], unpack(o[0], shapes))) for o in outs]
    for n in ("b_conv_b", "b_gate_a_w", "b_gate_a_b", "b_gate_x_w", "b_gate_x_b", "b_lambda", "c_f_bias"):
        res[n] = [packed["early"][k][n] for k in range(4)]
    for n in ("a_rel_bias", "norm_pre", "norm_post"):
        res[n] = [jnp.concatenate([packed["late"][k][n], packed["early"][k][n]]) for k in range(4)]

    total = lax.psum(loss[0, 0], ("x", "y", "c"))
    return (total, grad_x, *[res[n][0] for n in WEIGHTS], *[res[n][1] for n in WEIGHTS],
            *[res[n][2] for n in WEIGHTS], *[res[n][3] for n in WEIGHTS])
```

```python
import functools

import jax
import jax.numpy as jnp
from jax import lax
from jax.experimental import pallas as pl
from jax.experimental.pallas import tpu as pltpu

F32 = jnp.float32
BF16 = jnp.bfloat16

N_DEV = 8
D_MODEL = 2048
HEADS = 16
HEAD_DIM = 128
CHUNK = 64
LEFT_CHUNKS = 8
REL_CLIP = 256
N_REL = 2 * REL_CLIP + 1
TQ = 256
A_PAD = LEFT_CHUNKS * CHUNK
A_KW = A_PAD + TQ
RG_WIDTH = 2560
RG_BLOCKS = 16
RG_BLOCK = 160
RG_COLS = 640
RG_GROUPS = RG_WIDTH // RG_COLS
RG_C = 8.0
CONV_WIDTH = 4
RMS_EPS = 1e-6
NEG_INF = -1e30
ADAM_LR = 0.001
ADAM_B1 = 0.9
ADAM_B2 = 0.999
ADAM_EPS = 1e-08
ADAM_WD = 0.01
ADAM_STEP = 10
VMEM_LIMIT = 56 * 1024 * 1024
MESH = pl.DeviceIdType.MESH


def _params(sem, vmem=VMEM_LIMIT):
    return pltpu.CompilerParams(dimension_semantics=sem, vmem_limit_bytes=vmem)


def _sigmoid(x):
    return 1.0 / (1.0 + jnp.exp(-x))


def _log1p(y):
    u = 1.0 + y
    return jnp.where(u == 1.0, y, jnp.log(u) * (y / jnp.where(u == 1.0, 1.0, u - 1.0)))


def _softplus(x):
    return jnp.maximum(x, 0.0) + _log1p(jnp.exp(-jnp.abs(x)))


def _dot(a, b, dims):
    return lax.dot_general(a, b, (dims, ((), ())), preferred_element_type=F32)


def _dot_nn(a, b):
    return _dot(a, b, ((1,), (0,)))


def _dot_nt(a, b):
    return _dot(a, b, ((1,), (1,)))


def _dot_tn(a, b):
    return _dot(a, b, ((0,), (0,)))


def _peers():
    x, y, c = lax.axis_index("x"), lax.axis_index("y"), lax.axis_index("c")
    me = 4 * x + 2 * y + c
    peers = []
    for k in range(1, N_DEV):
        px = 1 - x if k & 4 else x
        py = 1 - y if k & 2 else y
        pc = 1 - c if k & 1 else c
        peers.append(((px, py, pc), 4 * px + 2 * py + pc))
    return me, peers


class Ride:
    def __init__(self, arrs, scatter, via_sibling=False):
        assert not (scatter and via_sibling)
        self.arrs, self.scatter, self.via_sibling, self.out = list(arrs), scatter, via_sibling, None

    def out_shapes(self):
        return [jax.ShapeDtypeStruct(a.shape if self.scatter else (N_DEV,) + a.shape, a.dtype) for a in self.arrs]

    def sem_shapes(self):
        n = len(self.arrs)
        return [pltpu.SemaphoreType.DMA((n, N_DEV - 1)), pltpu.SemaphoreType.DMA((n, N_DEV - 1)),
                pltpu.SemaphoreType.DMA((n,))]

    def _copies(self, ins, outs, sems, landing):
        send_sems, recv_sems, local_sems = sems
        me, peers = _peers()
        local, remote = [], []
        for a, (src, dst) in enumerate(zip(ins, outs)):
            local.append(pltpu.make_async_copy(src.at[me] if self.scatter else src, dst.at[me], local_sems.at[a]))
            for k, (peer, peer_idx) in enumerate(peers):
                remote.append(pltpu.make_async_remote_copy(
                    src_ref=src.at[peer_idx] if self.scatter else src, dst_ref=dst.at[peer_idx if landing else me],
                    send_sem=send_sems.at[a, k], recv_sem=recv_sems.at[a, k], device_id=peer, device_id_type=MESH))
        return local, remote

    def _direct(self, k):
        return not self.via_sibling or k == 0 or (k + 1) % 2 == 0

    def start(self, ins, outs, sems):
        local, remote = self._copies(ins, outs, sems, landing=False)
        n_peers = N_DEV - 1
        for cp in local + [cp for i, cp in enumerate(remote) if self._direct(i % n_peers)]:
            cp.start()

    def wait(self, ins, outs, sems):
        local, remote = self._copies(ins, outs, sems, landing=True)
        n_peers = N_DEV - 1
        for i, cp in enumerate(remote):
            if self._direct(i % n_peers):
                cp.wait()
        if self.via_sibling:
            send_sems, recv_sems, _ = sems
            me, peers = _peers()
            sibling = peers[0][0]
            passed = []
            for a, dst in enumerate(outs):
                for j in range(1, n_peers, 2):
                    came, lands = peers[j][1], peers[j + 1][1]
                    pltpu.make_async_remote_copy(
                        src_ref=dst.at[came], dst_ref=dst.at[came], send_sem=send_sems.at[a, j + 1],
                        recv_sem=recv_sems.at[a, j + 1], device_id=sibling, device_id_type=MESH).start()
                    passed.append(pltpu.make_async_remote_copy(
                        src_ref=dst.at[came], dst_ref=dst.at[lands], send_sem=send_sems.at[a, j + 1],
                        recv_sem=recv_sems.at[a, j + 1], device_id=sibling, device_id_type=MESH))
            for cp in passed:
                cp.wait()
        for cp in local:
            cp.wait()


def _call(body, *, name, grid, in_specs, out_specs, out_shape, args, scratch_shapes=(), semantics=None, ride=None,
          aliases=None):
    scratch_shapes = list(scratch_shapes)
    if ride is None:
        return pl.pallas_call(
            body, name=name, grid=grid, in_specs=in_specs, out_specs=out_specs, out_shape=out_shape,
            scratch_shapes=scratch_shapes, input_output_aliases=aliases or {},
            compiler_params=_params(semantics if grid else None))(*args)
    assert not aliases
    n_in, n_out, n_sc, n_r = len(in_specs), len(out_specs), len(scratch_shapes), len(ride.arrs)

    def riding(*refs):
        ins, r_ins = refs[:n_in], refs[n_in:n_in + n_r]
        outs, r_outs = refs[n_in + n_r:n_in + n_r + n_out], refs[n_in + n_r + n_out:n_in + 2 * n_r + n_out]
        rest = refs[n_in + 2 * n_r + n_out:]
        scratch, sems = rest[:n_sc], rest[n_sc:]
        first = last = None
        for axis, size in enumerate(grid):
            pid = pl.program_id(axis)
            first = (pid == 0) if first is None else first & (pid == 0)
            last = (pid == size - 1) if last is None else last & (pid == size - 1)
        if grid:
            pl.when(first)(lambda: ride.start(r_ins, r_outs, sems))
        else:
            ride.start(r_ins, r_outs, sems)
        body(*ins, *outs, *scratch)
        if grid:
            pl.when(last)(lambda: ride.wait(r_ins, r_outs, sems))
        else:
            ride.wait(r_ins, r_outs, sems)

    any_spec = pl.BlockSpec(memory_space=pl.ANY)
    res = pl.pallas_call(
        riding, name=name, grid=grid, in_specs=list(in_specs) + [any_spec] * n_r,
        out_specs=list(out_specs) + [any_spec] * n_r, out_shape=list(out_shape) + ride.out_shapes(),
        scratch_shapes=scratch_shapes + ride.sem_shapes(),
        compiler_params=_params(("arbitrary",) * len(grid) if grid else None))(*args, *ride.arrs)
    ride.out = list(res[n_out:])
    return list(res[:n_out])


def exchange(ride, name):
    _call(lambda: None, name=name, grid=(), in_specs=[], out_specs=[], out_shape=[], args=[], ride=ride)
    return ride.out


LANES = 128


def _fit(dims, want):
    dims = tuple(dims)
    if len(set(dims)) == 1 and dims[0] <= want:
        return dims[0]
    return max(t for t in range(LANES, want + 1, LANES) if all(d % t == 0 for d in dims))


def _cols(arr):
    return arr.shape[-1] * (arr.shape[0] if len(arr.shape) == 3 else 1)


def _tile_spec(shape, rblk, cblk, rc):
    if len(shape) == 2:
        return pl.BlockSpec((rblk, cblk), rc)
    per = shape[2] // cblk

    def index_map(*ids):
        r, c = rc(*ids)
        return (c // per, r, c % per)

    return pl.BlockSpec((1, rblk, cblk), index_map)


def matmul(a, b, *, mode, out_dtype, name, bm=1024, bn=1024, bk=2048, out_slabs=1, m_part=(0, 1, 1), ride=None):
    a_rows, a_cols, b_rows, b_cols = a.shape[-2], _cols(a), b.shape[-2], _cols(b)
    (K, M) = (a_rows, a_cols) if mode == "tn" else (a_cols, a_rows)
    N = b_rows if mode == "nt" else b_cols
    assert K == (b_cols if mode == "nt" else b_rows), (name, a.shape, b.shape)
    first_range, n_ranges, of_ranges = m_part
    row0, M = first_range * (M // of_ranges), n_ranges * (M // of_ranges)
    out_shape = (M, N) if out_slabs == 1 else (out_slabs, M, N // out_slabs)
    widths = dict(m=[M], n=[N, out_shape[-1]], k=[K])
    widths["m" if mode == "tn" else "k"].append(a.shape[-1])
    widths["k" if mode == "nt" else "n"].append(b.shape[-1])
    bm, bn, bk = _fit(widths["m"], bm), _fit(widths["n"], bn), _fit(widths["k"], bk)
    nk = K // bk
    dims = {"nn": ((1,), (0,)), "nt": ((1,), (1,)), "tn": ((0,), (0,))}[mode]

    def val(ref):
        return ref[0] if len(ref.shape) == 3 else ref[...]

    def put(ref, x):
        if len(ref.shape) == 3:
            ref[0] = x.astype(ref.dtype)
        else:
            ref[...] = x.astype(ref.dtype)

    def body(a_ref, b_ref, o_ref, *scratch):
        if nk == 1:
            put(o_ref, _dot(val(a_ref), val(b_ref), dims))
            return
        acc_ref, = scratch
        k = pl.program_id(2)

        @pl.when(k == 0)
        def _():
            acc_ref[...] = jnp.zeros_like(acc_ref)

        acc_ref[...] += _dot(val(a_ref), val(b_ref), dims)

        @pl.when(k == nk - 1)
        def _():
            put(o_ref, acc_ref[...])

    assert row0 % bm == 0
    m0 = row0 // bm
    if mode == "tn":
        a_spec = _tile_spec(a.shape, bk, bm, lambda j, i, k: (k, m0 + i))
    else:
        a_spec = _tile_spec(a.shape, bm, bk, lambda j, i, k: (m0 + i, k))
    if mode == "nt":
        b_spec = _tile_spec(b.shape, bn, bk, lambda j, i, k: (j, k))
    else:
        b_spec = _tile_spec(b.shape, bk, bn, lambda j, i, k: (k, j))
    return _call(
        body, name=name, grid=(N // bn, M // bm, nk), in_specs=[a_spec, b_spec],
        out_specs=[_tile_spec(out_shape, bm, bn, lambda j, i, k: (i, j))],
        out_shape=[jax.ShapeDtypeStruct(out_shape, out_dtype)],
        scratch_shapes=[] if nk == 1 else [pltpu.VMEM((bm, bn), F32)],
        semantics=("parallel", "parallel", "arbitrary"), args=[a, b], ride=ride)[0]


ROW_TILE = 256


def _rms_stats(z):
    r = lax.rsqrt(jnp.mean(z * z, axis=-1, keepdims=True) + RMS_EPS)
    return r, z * r


def _rms_bwd(n, r, g, dout):
    dn = dout * g
    return r * (dn - n * jnp.mean(dn * n, axis=-1, keepdims=True))


def _row_spec(T, Dm):
    bt = min(ROW_TILE, T)
    return bt, pl.BlockSpec((bt, Dm), lambda i: (i, 0)), pl.BlockSpec((1, Dm), lambda i: (0, 0))


def prenorm_fwd(x, g, name):
    T, Dm = x.shape
    bt, row, vec = _row_spec(T, Dm)

    def body(x_ref, g_ref, h_ref):
        _, n = _rms_stats(x_ref[...])
        h_ref[...] = (n * g_ref[...]).astype(BF16)

    return pl.pallas_call(
        body, name=name, grid=(T // bt,), in_specs=[row, vec], out_specs=row,
        out_shape=jax.ShapeDtypeStruct((T, Dm), BF16), compiler_params=_params(("parallel",)),
    )(x, g)


def postnorm_prenorm_fwd(x, y, g_post, g_next, name):
    T, Dm = x.shape
    bt, row, vec = _row_spec(T, Dm)

    def body(x_ref, y_ref, gp_ref, gn_ref, o_ref, h_ref):
        _, n = _rms_stats(y_ref[...])
        x_new = x_ref[...] + n * gp_ref[...]
        o_ref[...] = x_new
        _, n_new = _rms_stats(x_new)
        h_ref[...] = (n_new * gn_ref[...]).astype(BF16)

    return pl.pallas_call(
        body, name=name, grid=(T // bt,), in_specs=[row, row, vec, vec], out_specs=[row, row],
        out_shape=[jax.ShapeDtypeStruct((T, Dm), F32), jax.ShapeDtypeStruct((T, Dm), BF16)],
        compiler_params=_params(("parallel",)),
    )(x, y, g_post, g_next)


def postnorm_loss(x, y, g, target, name):
    T, Dm = x.shape
    bt, row, vec = _row_spec(T, Dm)

    def body(x_ref, y_ref, g_ref, t_ref, l_ref, d_ref):
        @pl.when(pl.program_id(0) == 0)
        def _():
            l_ref[...] = jnp.zeros_like(l_ref)

        _, n = _rms_stats(y_ref[...])
        err = (x_ref[...] + n * g_ref[...]) - t_ref[...]
        per_tok = jnp.mean(err * err, axis=-1, keepdims=True)
        l_ref[...] += 0.5 * jnp.sum(per_tok, axis=0, keepdims=True)
        d_ref[...] = err * (1.0 / Dm)

    return pl.pallas_call(
        body, name=name, grid=(T // bt,), in_specs=[row, row, vec, row],
        out_specs=[pl.BlockSpec((1, 1), lambda i: (0, 0)), row],
        out_shape=[jax.ShapeDtypeStruct((1, 1), F32), jax.ShapeDtypeStruct((T, Dm), F32)],
        compiler_params=_params(("arbitrary",)),
    )(x, y, g, target)


def postnorm_bwd(y, g, dout, name):
    T, Dm = y.shape
    bt, row, vec = _row_spec(T, Dm)

    def body(y_ref, g_ref, d_ref, dy_ref, dg_ref):
        @pl.when(pl.program_id(0) == 0)
        def _():
            dg_ref[...] = jnp.zeros_like(dg_ref)

        r, n = _rms_stats(y_ref[...])
        dout_v = d_ref[...]
        dg_ref[...] += jnp.sum(dout_v * n, axis=0, keepdims=True)
        dy_ref[...] = _rms_bwd(n, r, g_ref[...], dout_v).astype(BF16)

    return pl.pallas_call(
        body, name=name, grid=(T // bt,), in_specs=[row, vec, row], out_specs=[row, vec],
        out_shape=[jax.ShapeDtypeStruct((T, Dm), BF16), jax.ShapeDtypeStruct((1, Dm), F32)],
        compiler_params=_params(("arbitrary",)),
    )(y, g, dout)


def prenorm_bwd(x, g, dhs, dres, name):
    T, Dm = x.shape
    bt, row, vec = _row_spec(T, Dm)
    n_dh = len(dhs)

    def body(x_ref, g_ref, *refs):
        dh_refs, (dr_ref, dx_ref, dg_ref) = refs[:n_dh], refs[n_dh:]

        @pl.when(pl.program_id(0) == 0)
        def _():
            dg_ref[...] = jnp.zeros_like(dg_ref)

        r, n = _rms_stats(x_ref[...])
        dh_v = dh_refs[0][...]
        for extra in dh_refs[1:]:
            dh_v = dh_v + extra[...]
        dg_ref[...] += jnp.sum(dh_v * n, axis=0, keepdims=True)
        dx_ref[...] = dr_ref[...] + _rms_bwd(n, r, g_ref[...], dh_v)

    return pl.pallas_call(
        body, name=name, grid=(T // bt,), in_specs=[row, vec] + [row] * (n_dh + 1), out_specs=[row, vec],
        out_shape=[jax.ShapeDtypeStruct((T, Dm), F32), jax.ShapeDtypeStruct((1, Dm), F32)],
        compiler_params=_params(("arbitrary",)),
    )(x, g, *dhs, dres)


def _silu(g):
    return g * _sigmoid(g)


def _gate_bwd(dgated, core, g):
    sg = _sigmoid(g)
    return dgated * (g * sg), dgated * core * (sg * (1.0 + g * (1.0 - sg)))


def _softmax_rows(s):
    e = jnp.exp(s - jnp.max(s, axis=-1, keepdims=True))
    return e * (1.0 / jnp.sum(e, axis=-1, keepdims=True))


def _band_scores(qk, bias, r0):
    s = qk * (HEAD_DIM ** -0.5) + bias
    j = lax.broadcasted_iota(jnp.int32, s.shape, 1)
    return jnp.where(j >= A_PAD - r0, s, NEG_INF)


def _fill_padded_kv(p_ref, kp_ref, vp_ref):
    zeros = jnp.zeros((A_PAD, kp_ref.shape[1]), BF16)
    kp_ref[0:A_PAD, :] = zeros
    vp_ref[0:A_PAD, :] = zeros
    kp_ref[A_PAD:, :] = p_ref[1, 0].astype(BF16)
    vp_ref[A_PAD:, :] = p_ref[2, 0].astype(BF16)


def _head_specs(S, order, heads=1):
    def idx(fn):
        return lambda *ids: fn(**dict(zip(order, ids)))

    return (pl.BlockSpec((4, 1, S, heads * HEAD_DIM), idx(lambda b, h, t: (0, b, 0, h))),
            pl.BlockSpec((1, TQ, heads * HEAD_DIM), idx(lambda b, h, t: (b, t, h))))


def attn_a_fwd(proj, bias, name, heads=1, ride=None):
    _, B, S, W = proj.shape
    nt = S // TQ

    def body(p_ref, b_ref, o_ref, gt_ref, kp_ref, vp_ref):
        t = pl.program_id(2)

        @pl.when(t == 0)
        def _():
            _fill_padded_kv(p_ref, kp_ref, vp_ref)

        r0 = pl.multiple_of(t * TQ, TQ)
        for e in range(heads):
            lanes = slice(e * HEAD_DIM, (e + 1) * HEAD_DIM)
            q = p_ref[0, 0, pl.ds(r0, TQ), lanes].astype(BF16)
            g = p_ref[3, 0, pl.ds(r0, TQ), lanes]
            p = _softmax_rows(_band_scores(_dot_nt(q, kp_ref[pl.ds(r0, A_KW), lanes]), b_ref[e], r0))
            o = _dot_nn(p.astype(BF16), vp_ref[pl.ds(r0, A_KW), lanes])
            o_ref[0, :, lanes] = o
            gt_ref[0, :, lanes] = (o * _silu(g)).astype(BF16)

    seq, tile = _head_specs(S, "bht", heads)
    kv = pltpu.VMEM((A_PAD + S, heads * HEAD_DIM), BF16)
    return _call(
        body, name=name, grid=(B, HEADS // heads, nt),
        in_specs=[seq, pl.BlockSpec((heads, TQ, A_KW), lambda b, h, t: (h, 0, 0))], out_specs=[tile, tile],
        out_shape=[jax.ShapeDtypeStruct((B, S, W), F32), jax.ShapeDtypeStruct((B, S, W), BF16)],
        scratch_shapes=[kv, kv],
        semantics=("parallel", "parallel", "arbitrary"), args=[proj, bias], ride=ride)


def attn_a_bwd(proj, bias, o, dgated, name, heads=1, ride=None):
    _, B, S, W = proj.shape
    nt = S // TQ

    def body(p_ref, b_ref, o_ref, dgt_ref, dp_ref, db_ref, kp_ref, vp_ref, dk_ref, dv_ref):
        b_, t = pl.program_id(1), pl.program_id(2)

        @pl.when(t == 0)
        def _():
            _fill_padded_kv(p_ref, kp_ref, vp_ref)
            dk_ref[...] = jnp.zeros_like(dk_ref)
            dv_ref[...] = jnp.zeros_like(dv_ref)

        @pl.when((t == 0) & (b_ == 0))
        def _():
            db_ref[...] = jnp.zeros_like(db_ref)

        r0 = pl.multiple_of(t * TQ, TQ)
        rows, win = pl.ds(r0, TQ), pl.ds(r0, A_KW)
        for e in range(heads):
            lanes = slice(e * HEAD_DIM, (e + 1) * HEAD_DIM)
            q = p_ref[0, 0, rows, lanes].astype(BF16)
            g = p_ref[3, 0, rows, lanes]
            kw, vw = kp_ref[win, lanes], vp_ref[win, lanes]
            p = _softmax_rows(_band_scores(_dot_nt(q, kw), b_ref[e], r0))
            do, dg = _gate_bwd(dgt_ref[0, :, lanes], o_ref[0, :, lanes], g)
            do = do.astype(BF16)
            dv_ref[win, lanes] += _dot_tn(p.astype(BF16), do)
            dpr = _dot_nt(do, vw)
            ds = p * (dpr - jnp.sum(p * dpr, axis=-1, keepdims=True))
            db_ref[e] += ds
            ds = (ds * (HEAD_DIM ** -0.5)).astype(BF16)
            dk_ref[win, lanes] += _dot_tn(ds, q)
            dp_ref[0, 0, rows, lanes] = _dot_nn(ds, kw).astype(BF16)
            dp_ref[3, 0, rows, lanes] = dg.astype(BF16)

        @pl.when(t == nt - 1)
        def _():
            dp_ref[1, 0] = dk_ref[A_PAD:, :].astype(BF16)
            dp_ref[2, 0] = dv_ref[A_PAD:, :].astype(BF16)

    seq, tile = _head_specs(S, "hbt", heads)
    bias_spec = pl.BlockSpec((heads, TQ, A_KW), lambda h, b, t: (h, 0, 0))
    kv = pltpu.VMEM((A_PAD + S, heads * HEAD_DIM), BF16)
    acc = pltpu.VMEM((A_PAD + S, heads * HEAD_DIM), F32)
    return _call(
        body, name=name, grid=(HEADS // heads, B, nt), in_specs=[seq, bias_spec, tile, tile],
        out_specs=[seq, bias_spec],
        out_shape=[jax.ShapeDtypeStruct(proj.shape, BF16), jax.ShapeDtypeStruct(bias.shape, F32)],
        scratch_shapes=[kv, kv, acc, acc],
        semantics=("arbitrary", "arbitrary", "arbitrary"), args=[proj, bias, o, dgated], ride=ride)


def band_bias(rel_bias):
    length = TQ + A_KW - 1
    first = REL_CLIP + 1 - TQ
    gen = jnp.concatenate([rel_bias[:, first:],
                           jnp.broadcast_to(rel_bias[:, 2 * REL_CLIP:], (HEADS, length - (N_REL - first)))], axis=1)
    rev = jnp.concatenate([gen[:, ::-1], jnp.zeros((HEADS, 1), rel_bias.dtype)], axis=1)
    sheared = jnp.tile(rev, (1, TQ))[:, :TQ * length].reshape(HEADS, TQ, length)
    i = lax.broadcasted_iota(jnp.int32, (TQ, A_KW), 0)
    j = lax.broadcasted_iota(jnp.int32, (TQ, A_KW), 1)
    first_key = (i // CHUNK) * CHUNK
    in_band = (j >= first_key) & (j < first_key + (LEFT_CHUNKS + 1) * CHUNK)
    return jnp.where(in_band, sheared[:, :, TQ - 1:], NEG_INF)


def _group_scan(a, u, carry, reverse=False):
    row = lax.broadcasted_iota(jnp.int32, u.shape, 0)
    for k in (1, 2, 4):
        shift = 8 - k if reverse else k
        valid = (row < 8 - k) if reverse else (row >= k)
        u_sh = pltpu.roll(u, shift, 0)
        if a is None:
            u = jnp.where(valid, u + u_sh, u)
        else:
            a_sh = pltpu.roll(a, shift, 0)
            u = jnp.where(valid, a * u_sh + u, u)
            a = jnp.where(valid, a * a_sh, a)
    return (u + carry) if a is None else (a * carry + u)


SCAN_UNROLL = 4


def _scan_rows(n_rows, step, carry0, reverse=False):
    groups = n_rows // 8

    def loop(i, carry):
        gi = (groups - 1 - i) if reverse else i
        return step(pl.multiple_of(gi * 8, 8), carry)

    return lax.fori_loop(0, groups, loop, carry0, unroll=SCAN_UNROLL)


def fox_cum_fwd(f_logit, f_bias, name):
    B, S, L = f_logit.shape

    def body(f_ref, b_ref, c_ref):
        z = f_ref[0] + b_ref[...]
        c_ref[0] = jnp.minimum(z, 0.0) - _log1p(jnp.exp(-jnp.abs(z)))

        def step(r0, carry):
            h = _group_scan(None, c_ref[0, pl.ds(r0, 8), :], carry)
            c_ref[0, pl.ds(r0, 8), :] = h
            return h[7:8, :]

        _scan_rows(S, step, jnp.zeros((1, L), F32))

    return pl.pallas_call(
        body, name=name, grid=(B,),
        in_specs=[pl.BlockSpec((1, S, L), lambda b: (b, 0, 0)), pl.BlockSpec((1, L), lambda b: (0, 0))],
        out_specs=pl.BlockSpec((1, S, L), lambda b: (b, 0, 0)),
        out_shape=jax.ShapeDtypeStruct((B, S, L), F32), compiler_params=_params(("parallel",)),
    )(f_logit, f_bias)


def fox_cum_bwd(f_logit, f_bias, dcum, name):
    B, S, L = f_logit.shape

    def body(f_ref, b_ref, d_ref, df_ref, db_ref):
        @pl.when(pl.program_id(0) == 0)
        def _():
            db_ref[...] = jnp.zeros_like(db_ref)

        def step(r0, carry):
            h = _group_scan(None, d_ref[0, pl.ds(r0, 8), :], carry, reverse=True)
            df_ref[0, pl.ds(r0, 8), :] = h
            return h[0:1, :]

        _scan_rows(S, step, jnp.zeros((1, L), F32), reverse=True)
        df = df_ref[0] * _sigmoid(-(f_ref[0] + b_ref[...]))
        df_ref[0] = df
        db_ref[...] += jnp.sum(df, axis=0, keepdims=True)

    seq = pl.BlockSpec((1, S, L), lambda b: (b, 0, 0))
    vec = pl.BlockSpec((1, L), lambda b: (0, 0))
    return pl.pallas_call(
        body, name=name, grid=(B,), in_specs=[seq, vec, seq], out_specs=[seq, vec],
        out_shape=[jax.ShapeDtypeStruct((B, S, L), F32), jax.ShapeDtypeStruct((1, L), F32)],
        compiler_params=_params(("arbitrary",)),
    )(f_logit, f_bias, dcum)


def _head_row(cr, h):
    sub = lax.broadcasted_iota(jnp.int32, cr.shape, 0)
    return jnp.sum(jnp.where(sub == h, cr, 0.0), axis=0, keepdims=True)


def _fox_scores(qk, cc, ck, h, r0):
    lane = lax.broadcasted_iota(jnp.int32, cc.shape, 1)
    cq = jnp.sum(jnp.where(lane == h, cc, 0.0), axis=1, keepdims=True)
    s = qk * (HEAD_DIM ** -0.5) + (cq - ck)
    qpos = r0 + lax.broadcasted_iota(jnp.int32, s.shape, 0)
    kpos = lax.broadcasted_iota(jnp.int32, s.shape, 1)
    return jnp.where(kpos <= qpos, s, NEG_INF)


KEY_STEP = 256


def _by_causal_width(t, S, fn):
    per = KEY_STEP // TQ
    for c in range(S // KEY_STEP):
        pl.when(t // per == c)(functools.partial(fn, (c + 1) * KEY_STEP))


def fox_fwd(proj, cum_col, cum_row, name, heads=1, ride=None):
    _, B, S, W = proj.shape
    nt = S // TQ

    def body(p_ref, cc_ref, cr_ref, o_ref, gt_ref, k_ref, v_ref):
        head0, t = pl.program_id(1) * heads, pl.program_id(2)

        @pl.when(t == 0)
        def _():
            k_ref[...] = p_ref[1, 0].astype(BF16)
            v_ref[...] = p_ref[2, 0].astype(BF16)

        r0 = pl.multiple_of(t * TQ, TQ)

        def tile_out(width):
            for e in range(heads):
                h, lanes = head0 + e, slice(e * HEAD_DIM, (e + 1) * HEAD_DIM)
                q = p_ref[0, 0, pl.ds(r0, TQ), lanes].astype(BF16)
                ck = _head_row(cr_ref[0, :, 0:width], h)
                p = _softmax_rows(_fox_scores(_dot_nt(q, k_ref[0:width, lanes]), cc_ref[0], ck, h, r0))
                o = _dot_nn(p.astype(BF16), v_ref[0:width, lanes])
                o_ref[0, :, lanes] = o
                gt_ref[0, :, lanes] = (o * _silu(p_ref[3, 0, pl.ds(r0, TQ), lanes])).astype(BF16)

        _by_causal_width(t, S, tile_out)

    seq, tile = _head_specs(S, "bht", heads)
    kv = pltpu.VMEM((S, heads * HEAD_DIM), BF16)
    return _call(
        body, name=name, grid=(B, HEADS // heads, nt),
        in_specs=[seq, pl.BlockSpec((1, TQ, cum_col.shape[2]), lambda b, h, t: (b, t, 0)),
                  pl.BlockSpec((1, HEADS, S), lambda b, h, t: (b, 0, 0))],
        out_specs=[tile, tile],
        out_shape=[jax.ShapeDtypeStruct((B, S, W), F32), jax.ShapeDtypeStruct((B, S, W), BF16)],
        scratch_shapes=[kv, kv],
        semantics=("parallel", "parallel", "arbitrary"), args=[proj, cum_col, cum_row], ride=ride)


def fox_bwd(proj, cum_col, cum_row, o, dgated, name, heads=1, ride=None):
    _, B, S, W = proj.shape
    nt = S // TQ

    def body(p_ref, cc_ref, cr_ref, o_ref, dgt_ref, dp_ref, dc_ref, k_ref, v_ref, dk_ref, dv_ref):
        head0, t = pl.program_id(1) * heads, pl.program_id(2)

        @pl.when(t == 0)
        def _():
            k_ref[...] = p_ref[1, 0].astype(BF16)
            v_ref[...] = p_ref[2, 0].astype(BF16)
            dk_ref[...] = jnp.zeros_like(dk_ref)
            dv_ref[...] = jnp.zeros_like(dv_ref)
            dc_ref[...] = jnp.zeros_like(dc_ref)

        r0 = pl.multiple_of(t * TQ, TQ)
        rows = pl.ds(r0, TQ)

        def tile_grads(width):
            for e in range(heads):
                h, lanes = head0 + e, slice(e * HEAD_DIM, (e + 1) * HEAD_DIM)
                q = p_ref[0, 0, rows, lanes].astype(BF16)
                do, dg = _gate_bwd(dgt_ref[0, :, lanes], o_ref[0, :, lanes], p_ref[3, 0, rows, lanes])
                do = do.astype(BF16)
                dp_ref[3, 0, rows, lanes] = dg.astype(BF16)
                k, v = k_ref[0:width, lanes], v_ref[0:width, lanes]
                ck = _head_row(cr_ref[0, :, 0:width], h)
                p = _softmax_rows(_fox_scores(_dot_nt(q, k), cc_ref[0], ck, h, r0))
                dv_ref[0:width, lanes] += _dot_tn(p.astype(BF16), do)
                dpr = _dot_nt(do, v)
                ds = p * (dpr - jnp.sum(p * dpr, axis=-1, keepdims=True))
                dc_ref[0, e, :, 0:width] += jnp.sum(ds, axis=0, keepdims=True)
                ds = (ds * (HEAD_DIM ** -0.5)).astype(BF16)
                dk_ref[0:width, lanes] += _dot_tn(ds, q)
                dp_ref[0, 0, rows, lanes] = _dot_nn(ds, k).astype(BF16)

        _by_causal_width(t, S, tile_grads)

        @pl.when(t == nt - 1)
        def _():
            dp_ref[1, 0] = dk_ref[...].astype(BF16)
            dp_ref[2, 0] = dv_ref[...].astype(BF16)

    seq, tile = _head_specs(S, "bht", heads)
    kv = pltpu.VMEM((S, heads * HEAD_DIM), BF16)
    acc = pltpu.VMEM((S, heads * HEAD_DIM), F32)
    return _call(
        body, name=name, grid=(B, HEADS // heads, nt),
        in_specs=[seq, pl.BlockSpec((1, TQ, cum_col.shape[2]), lambda b, h, t: (b, t, 0)),
                  pl.BlockSpec((1, HEADS, S), lambda b, h, t: (b, 0, 0)), tile, tile],
        out_specs=[seq, pl.BlockSpec((1, heads, 1, S), lambda b, h, t: (b, h, 0, 0))],
        out_shape=[jax.ShapeDtypeStruct(proj.shape, BF16), jax.ShapeDtypeStruct((B, HEADS, 1, S), F32)],
        scratch_shapes=[kv, kv, acc, acc],
        semantics=("parallel", "parallel", "arbitrary"), args=[proj, cum_col, cum_row, o, dgated], ride=ride)


RG_ROWS = 512


def _rg_gates(xc, wa_ref, ba_ref, wx_ref, bx_ref, lam_ref):
    xcb = xc.astype(BF16)
    r = _sigmoid(_dot_nn(xcb, wa_ref[0]) + ba_ref[...])
    i = _sigmoid(_dot_nn(xcb, wx_ref[0]) + bx_ref[...])
    sp = _softplus(-lam_ref[...])
    log_a = (-RG_C * sp) * r
    a = jnp.exp(log_a)
    m = jnp.sqrt(-jnp.tanh(log_a) * (a * a + 1.0))
    return xcb, r, i, sp, a, m


def _rg_specs(B, S, rows, order):
    nc = S // rows

    def idx(fn):
        def index_map(*ids):
            v = dict(zip(order.lower(), ids))
            c = (nc - 1 - v["c"]) if "C" in order else v["c"]
            return fn(v["b"], v["d"], c)
        return index_map

    return dict(
        proj=pl.BlockSpec((2, 1, rows, RG_COLS), idx(lambda b, d, c: (0, b, c, d))),
        act=pl.BlockSpec((1, rows, RG_COLS), idx(lambda b, d, c: (b, c, d))),
        taps=pl.BlockSpec((CONV_WIDTH, RG_COLS), idx(lambda b, d, c: (0, d))),
        vec=pl.BlockSpec((1, RG_COLS), idx(lambda b, d, c: (0, d))),
        gate=pl.BlockSpec((1, RG_COLS, RG_COLS), idx(lambda b, d, c: (d, 0, 0))),
    )


def rglru_fwd(proj, conv_w, conv_b, wa, ba, wx, bx, lam, name, rows=RG_ROWS, ride=None):
    _, B, S, _ = proj.shape
    rows = min(rows, S)
    sp_ = _rg_specs(B, S, rows, "bdc")

    def body(p_ref, cw_ref, cb_ref, wa_ref, ba_ref, wx_ref, bx_ref, lam_ref,
             xc_ref, hs_ref, hp_ref, gt_ref, ext_ref, a_ref, u_ref, xcar_ref, hcar_ref):
        @pl.when(pl.program_id(2) == 0)
        def _():
            xcar_ref[...] = jnp.zeros_like(xcar_ref)
            hcar_ref[...] = jnp.zeros_like(hcar_ref)

        xr = p_ref[0, 0]
        ext_ref[0:8, :] = xcar_ref[...]
        ext_ref[8:, :] = xr
        xcar_ref[...] = xr[rows - 8:, :]
        xc = ext_ref[pl.ds(5, rows), :] * cw_ref[0:1, :]
        xc = xc + ext_ref[pl.ds(6, rows), :] * cw_ref[1:2, :]
        xc = xc + ext_ref[pl.ds(7, rows), :] * cw_ref[2:3, :]
        xc = xc + xr * cw_ref[3:4, :] + cb_ref[...]
        xc_ref[0] = xc
        _, _, i, _, a, m = _rg_gates(xc, wa_ref, ba_ref, wx_ref, bx_ref, lam_ref)
        a_ref[...] = a
        u_ref[...] = m * (i * xc)

        def step(r0, carry):
            h = _group_scan(a_ref[pl.ds(r0, 8), :], u_ref[pl.ds(r0, 8), :], carry)
            row = lax.broadcasted_iota(jnp.int32, h.shape, 0)
            hs_ref[0, pl.ds(r0, 8), :] = h
            hp_ref[0, pl.ds(r0, 8), :] = jnp.where(row == 0, carry, pltpu.roll(h, 1, 0))
            return h[7:8, :]

        hcar_ref[0:1, :] = _scan_rows(rows, step, hcar_ref[0:1, :])
        gt_ref[0] = (hs_ref[0] * _silu(p_ref[1, 0])).astype(BF16)

    act = jax.ShapeDtypeStruct((B, S, RG_WIDTH), F32)
    return _call(
        body, name=name, grid=(B, RG_GROUPS, S // rows),
        in_specs=[sp_["proj"], sp_["taps"], sp_["vec"], sp_["gate"], sp_["vec"], sp_["gate"], sp_["vec"], sp_["vec"]],
        out_specs=[sp_["act"]] * 4,
        out_shape=[act, act, act, jax.ShapeDtypeStruct((B, S, RG_WIDTH), BF16)],
        scratch_shapes=[pltpu.VMEM((rows + 8, RG_COLS), F32), pltpu.VMEM((rows, RG_COLS), F32),
                        pltpu.VMEM((rows, RG_COLS), F32), pltpu.VMEM((8, RG_COLS), F32), pltpu.VMEM((8, RG_COLS), F32)],
        semantics=("parallel", "parallel", "arbitrary"), args=[proj, conv_w, conv_b, wa, ba, wx, bx, lam], ride=ride)


def rglru_bwd(proj, xc, hs, hprev, dgated, conv_w, wa, ba, wx, bx, lam, name, rows=RG_ROWS, ride=None):
    _, B, S, _ = proj.shape
    rows = min(rows, S)
    sp_ = _rg_specs(B, S, rows, "dbC")

    def body(p_ref, xc_ref, hs_ref, hp_ref, dgt_ref, cw_ref, wa_ref, ba_ref, wx_ref, bx_ref, lam_ref,
             dp_ref, dcw_ref, dcb_ref, dwa_ref, dba_ref, dwx_ref, dbx_ref, dlam_ref,
             ext_ref, c_ref, l_ref, acar_ref, lcar_ref, dcar_ref):
        b_, c_ = pl.program_id(1), pl.program_id(2)

        @pl.when(c_ == 0)
        def _():
            acar_ref[...] = jnp.zeros_like(acar_ref)
            lcar_ref[...] = jnp.zeros_like(lcar_ref)
            dcar_ref[...] = jnp.zeros_like(dcar_ref)

        @pl.when((c_ == 0) & (b_ == 0))
        def _():
            for ref in (dcw_ref, dcb_ref, dwa_ref, dba_ref, dwx_ref, dbx_ref, dlam_ref):
                ref[...] = jnp.zeros_like(ref)

        xr, g = p_ref[0, 0], p_ref[1, 0]
        xc_v = xc_ref[0]
        xcb, r, i, sp, a, m = _rg_gates(xc_v, wa_ref, ba_ref, wx_ref, bx_ref, lam_ref)
        dhs, dg = _gate_bwd(dgt_ref[0], hs_ref[0], g)
        dp_ref[1, 0] = dg.astype(BF16)

        ext_ref[0:rows, :] = a
        ext_ref[rows:, :] = acar_ref[...]
        acar_ref[...] = a[0:8, :]
        c_ref[...] = ext_ref[pl.ds(1, rows), :]
        l_ref[...] = dhs

        def step(r0, carry):
            lam_g = _group_scan(c_ref[pl.ds(r0, 8), :], l_ref[pl.ds(r0, 8), :], carry, reverse=True)
            l_ref[pl.ds(r0, 8), :] = lam_g
            return lam_g[0:1, :]

        lcar_ref[0:1, :] = _scan_rows(rows, step, lcar_ref[0:1, :], reverse=True)
        du = l_ref[...]
        da = du * hp_ref[0]
        dlog_a = da * a - (du * (i * xc_v)) * (a * a / m)
        dr = dlog_a * (-RG_C * sp)
        dsp = jnp.sum(dlog_a * (-RG_C * r), axis=0, keepdims=True)
        dlam_ref[...] += dsp * (-_sigmoid(-lam_ref[...]))
        dpa = dr * (r * (1.0 - r))
        dpx = (du * (m * xc_v)) * (i * (1.0 - i))
        dba_ref[...] += jnp.sum(dpa, axis=0, keepdims=True)
        dbx_ref[...] += jnp.sum(dpx, axis=0, keepdims=True)
        dpa, dpx = dpa.astype(BF16), dpx.astype(BF16)
        dwa_ref[0] += _dot_tn(xcb, dpa)
        dwx_ref[0] += _dot_tn(xcb, dpx)
        dxc = du * (m * i) + _dot_nt(dpa, wa_ref[0]) + _dot_nt(dpx, wx_ref[0])

        dcb_ref[...] += jnp.sum(dxc, axis=0, keepdims=True)
        ext_ref[0:rows, :] = dxc
        ext_ref[rows:, :] = dcar_ref[...]
        dcar_ref[...] = dxc[0:8, :]
        dxr = jnp.zeros_like(dxc)
        for k in range(CONV_WIDTH):
            tap = CONV_WIDTH - 1 - k
            ahead = dxc if k == 0 else ext_ref[pl.ds(k, rows), :]
            dxr = dxr + ahead * cw_ref[tap:tap + 1, :]
            dcw_ref[tap:tap + 1, :] += jnp.sum(xr * ahead, axis=0, keepdims=True)
        dp_ref[0, 0] = dxr.astype(BF16)

    vec = jax.ShapeDtypeStruct((1, RG_WIDTH), F32)
    gate = jax.ShapeDtypeStruct((RG_GROUPS, RG_COLS, RG_COLS), F32)
    return _call(
        body, name=name, grid=(RG_GROUPS, B, S // rows),
        in_specs=[sp_["proj"], sp_["act"], sp_["act"], sp_["act"], sp_["act"], sp_["taps"],
                  sp_["gate"], sp_["vec"], sp_["gate"], sp_["vec"], sp_["vec"]],
        out_specs=[sp_["proj"], sp_["taps"], sp_["vec"], sp_["gate"], sp_["vec"], sp_["gate"], sp_["vec"], sp_["vec"]],
        out_shape=[jax.ShapeDtypeStruct(proj.shape, BF16), jax.ShapeDtypeStruct((CONV_WIDTH, RG_WIDTH), F32), vec,
                   gate, vec, gate, vec, vec],
        scratch_shapes=[pltpu.VMEM((rows + 8, RG_COLS), F32), pltpu.VMEM((rows, RG_COLS), F32),
                        pltpu.VMEM((rows, RG_COLS), F32), pltpu.VMEM((8, RG_COLS), F32),
                        pltpu.VMEM((8, RG_COLS), F32), pltpu.VMEM((8, RG_COLS), F32)],
        semantics=("arbitrary", "arbitrary", "arbitrary"),
        args=[proj, xc, hs, hprev, dgated, conv_w, wa, ba, wx, bx, lam], ride=ride)


def block_diag_gates(w):
    per = RG_COLS // RG_BLOCK
    w4 = w.reshape(RG_GROUPS, per, RG_BLOCK, RG_BLOCK)
    return jnp.einsum("dipq,ij->dipjq", w4, jnp.eye(per, dtype=w.dtype)).reshape(RG_GROUPS, RG_COLS, RG_COLS)


def block_diag_gates_t(dw):
    per = RG_COLS // RG_BLOCK
    dw6 = dw.reshape(RG_GROUPS, per, RG_BLOCK, per, RG_BLOCK)
    return jnp.stack([dw6[:, i, :, i, :] for i in range(per)], axis=1).reshape(RG_BLOCKS, RG_BLOCK, RG_BLOCK)


def adamw(w, parts, m, v, name, layer=0, prev=None, part_row0=0, row_tile=ROW_TILE):
    L, R, C = w.shape
    n_parts = parts.shape[0]
    br = row_tile if R % row_tile == 0 else R

    def body(w_ref, p_ref, m_ref, v_ref, *refs):
        g_ref, d_ref, nm_ref, nv_ref = refs[-4:]
        g = p_ref[0].astype(F32)
        for k in range(1, n_parts):
            g = g + p_ref[k].astype(F32)
        nm = ADAM_B1 * m_ref[0] + (1.0 - ADAM_B1) * g
        nv = ADAM_B2 * v_ref[0] + (1.0 - ADAM_B2) * (g * g)
        m_hat = nm / (1.0 - ADAM_B1 ** ADAM_STEP)
        v_hat = nv / (1.0 - ADAM_B2 ** ADAM_STEP)
        g_ref[0] = g
        d_ref[0] = -ADAM_LR * (m_hat / (jnp.sqrt(v_hat) + ADAM_EPS) + ADAM_WD * w_ref[0])
        nm_ref[0] = nm
        nv_ref[0] = nv

    slab = pl.BlockSpec((1, br, C), lambda i: (layer, i, 0))
    out = jax.ShapeDtypeStruct((L, R, C), F32)
    carried = [] if prev is None else list(prev)
    return _call(
        body, name=name, grid=(R // br,),
        in_specs=[slab, pl.BlockSpec((n_parts, br, C), lambda i: (0, part_row0 // br + i, 0)), slab, slab]
        + [pl.BlockSpec(memory_space=pl.ANY)] * len(carried),
        out_specs=[slab] * 4, out_shape=[out] * 4, semantics=("parallel",), args=[w, parts, m, v] + carried,
        aliases={4 + k: k for k in range(len(carried))})


def _seq(a, B):
    return a.reshape(a.shape[:-2] + (B, a.shape[-2] // B, a.shape[-1]))


def _flat(a):
    return a.reshape(a.shape[:-3] + (a.shape[-3] * a.shape[-2], a.shape[-1]))


def _tiles(w, which, **default):
    return dict(default, **w.get("tiles", {}).get(which, {}))


def mixer_a_fwd(h, w, B, tag, rides):
    proj = matmul(h, w["w_in"], mode="nn", out_dtype=F32, name=f"{tag}_proj", out_slabs=4,
                  ride=rides.pop(f"{tag}_proj", None), **_tiles(w, "proj"))
    o, gated = attn_a_fwd(_seq(proj, B), w["bias"], f"{tag}_attn", heads=w.get("attn_heads", (1, 1))[0],
                          ride=rides.pop(f"{tag}_attn", None))
    return _flat(gated), dict(proj=proj, o=o)


def mixer_a_bwd(dgated, w, saved, B, tag, rides):
    dproj, dbias = attn_a_bwd(_seq(saved["proj"], B), w["bias"], saved["o"], _seq(dgated, B), f"{tag}_attn_bwd",
                              heads=w.get("attn_heads", (1, 1))[1], ride=rides.pop(f"{tag}_attn_bwd", None))
    return _flat(dproj), dict(bias=dbias)


def mixer_b_fwd(h, w, B, tag, rides):
    proj = matmul(h, w["w_in"], mode="nn", out_dtype=F32, name=f"{tag}_proj", out_slabs=2, bn=RG_COLS,
                  ride=rides.pop(f"{tag}_proj", None))
    xc, hs, hprev, gated = rglru_fwd(_seq(proj, B), w["conv_w"], w["conv_b"], w["wa"], w["ba"], w["wx"], w["bx"],
                                     w["lam"], f"{tag}_rglru", ride=rides.pop(f"{tag}_rglru", None))
    return _flat(gated), dict(proj=proj, xc=xc, hs=hs, hprev=hprev)


def mixer_b_bwd(dgated, w, saved, B, tag, rides):
    dproj, dcw, dcb, dwa, dba, dwx, dbx, dlam = rglru_bwd(
        _seq(saved["proj"], B), saved["xc"], saved["hs"], saved["hprev"], _seq(dgated, B),
        w["conv_w"], w["wa"], w["ba"], w["wx"], w["bx"], w["lam"], f"{tag}_rglru_bwd",
        ride=rides.pop(f"{tag}_rglru_bwd", None))
    return _flat(dproj), dict(conv_w=dcw, conv_b=dcb, wa=dwa, ba=dba, wx=dwx, bx=dbx, lam=dlam)


def mixer_c_fwd(h, w, B, tag, rides):
    proj = matmul(h, w["w_in"], mode="nn", out_dtype=F32, name=f"{tag}_proj", out_slabs=4,
                  ride=rides.pop(f"{tag}_proj", None), **_tiles(w, "proj"))
    f_logit = matmul(h, w["w_f"], mode="nn", out_dtype=F32, name=f"{tag}_fproj")
    cum = fox_cum_fwd(_seq(f_logit, B), w["f_bias"], f"{tag}_cum")
    cum_row = cum[:, :, :HEADS].transpose(0, 2, 1)
    o, gated = fox_fwd(_seq(proj, B), cum, cum_row, f"{tag}_attn", heads=w.get("attn_heads", (1, 1))[0],
                       ride=rides.pop(f"{tag}_attn", None))
    return _flat(gated), dict(proj=proj, o=o, f_logit=f_logit, cum=cum, cum_row=cum_row)


def mixer_c_bwd(dgated, w, saved, B, tag, rides):
    dproj, dck = fox_bwd(_seq(saved["proj"], B), saved["cum"], saved["cum_row"], saved["o"], _seq(dgated, B),
                         f"{tag}_attn_bwd", heads=w.get("attn_heads", (1, 1))[1],
                         ride=rides.pop(f"{tag}_attn_bwd", None))
    S = dck.shape[-1]
    dcum = jnp.pad(-dck.reshape(B, HEADS, S).transpose(0, 2, 1), ((0, 0), (0, 0), (0, HEAD_DIM - HEADS)))
    df, dfb = fox_cum_bwd(_seq(saved["f_logit"], B), w["f_bias"], dcum, f"{tag}_cum_bwd")
    return _flat(dproj), dict(f_bias=dfb, df=_flat(df).astype(BF16))


MIXERS = {"a": (mixer_a_fwd, mixer_a_bwd), "b": (mixer_b_fwd, mixer_b_bwd), "c": (mixer_c_fwd, mixer_c_bwd)}
LAYER_KINDS = "abca"


def local_step(x, target, norm_pre, norm_post, get_layer, rides, on_grads):
    B, S, Dm = x.shape
    n_layers = len(LAYER_KINDS)
    xs = [x.reshape(B * S, Dm)]
    saved, layers = [], []
    h = prenorm_fwd(xs[0], norm_pre[0:1], "l0a_prenorm")
    for li, kind in enumerate(LAYER_KINDS):
        tag = f"l{li}{kind}"
        w = get_layer(li)
        gated, sv = MIXERS[kind][0](h, w, B, tag, rides)
        if callable(w["w_out"]):
            w["w_out"] = w["w_out"]()
        y = matmul(gated, w["w_out"], mode="nn", out_dtype=F32, name=f"{tag}_out", ride=rides.pop(f"{tag}_out", None))
        saved.append(dict(sv, h=h, gated=gated, y=y))
        layers.append(w)
        if li + 1 < n_layers:
            x_new, h = postnorm_prenorm_fwd(xs[-1], y, norm_post[li:li + 1], norm_pre[li + 1:li + 2],
                                            f"{tag}_postnorm")
            xs.append(x_new)
    loss, dx = postnorm_loss(xs[-1], y, norm_post[n_layers - 1:], target.reshape(B * S, Dm), "loss")

    for li in reversed(range(n_layers)):
        kind, w, sv = LAYER_KINDS[li], layers[li], saved[li]
        tag = f"l{li}{kind}"
        dy, dg_post = postnorm_bwd(sv["y"], norm_post[li:li + 1], dx, f"{tag}_postnorm_bwd")
        on_grads(li, "norm_post", dg_post)
        on_grads(li, "w_out", matmul(sv["gated"], dy, mode="tn", out_dtype=BF16, name=f"{tag}_dwout",
                                     ride=rides.pop(f"{tag}_dwout", None)))
        dgated = matmul(dy, w["w_out"], mode="nt", out_dtype=F32, name=f"{tag}_dgated",
                        ride=rides.pop(f"{tag}_dgated", None))
        dproj, gw = MIXERS[kind][1](dgated, w, sv, B, tag, rides)
        df = gw.pop("df", None)
        for name, value in gw.items():
            on_grads(li, name, value)
        parts = w.get("dwin_parts", [(0, 1, 1)])
        for i, m_part in enumerate(parts):
            suffix = f"_{i}" if len(parts) > 1 else ""
            on_grads(li, "w_in" + suffix,
                     matmul(sv["h"], dproj, mode="tn", out_dtype=BF16, name=f"{tag}_dwin{suffix}", m_part=m_part,
                            out_slabs=w["grad_slabs"], ride=rides.pop(f"{tag}_dwin{suffix}", None),
                            **_tiles(w, "dwin")))
        if df is not None:
            on_grads(li, "w_f", matmul(sv["h"], df, mode="tn", out_dtype=BF16, name=f"{tag}_dwf"))
        dhs = [matmul(dproj, w["w_in"], mode="nt", out_dtype=F32, name=f"{tag}_dh",
                      ride=rides.pop(f"{tag}_dh", None), **_tiles(w, "dh"))]
        if df is not None:
            dhs.append(matmul(df, w["w_f"], mode="nt", out_dtype=F32, name=f"{tag}_dhf"))
        dx, dg_pre = prenorm_bwd(xs[li], norm_pre[li:li + 1], dhs, dx, f"{tag}_prenorm_bwd")
        on_grads(li, "norm_pre", dg_pre)
    assert not rides, list(rides)
    return loss, dx.reshape(B, S, Dm)


WEIGHTS = ("norm_pre", "norm_post", "a_w_in", "a_rel_bias", "a_w_out", "b_w_in", "b_conv_w", "b_conv_b",
           "b_gate_a_w", "b_gate_a_b", "b_gate_x_w", "b_gate_x_b", "b_lambda", "b_w_out", "c_w_in", "c_f_bias",
           "c_w_out")
C_SHARD = (4 * D_MODEL + HEADS) // N_DEV


def _rows(gathered):
    return gathered.reshape(gathered.shape[0] * gathered.shape[1], gathered.shape[2])


def layer_a(w_in, w_out, rel_bias):
    return dict(w_in=w_in, w_out=w_out if callable(w_out) else _rows(w_out), bias=band_bias(rel_bias),
                grad_slabs=N_DEV)


def layer_b(w_in, w_out, conv_w, small):
    return dict(
        w_in=w_in, w_out=_rows(w_out), grad_slabs=N_DEV,
        conv_w=conv_w.transpose(1, 0, 2).reshape(CONV_WIDTH, RG_WIDTH),
        conv_b=small["b_conv_b"], lam=small["b_lambda"],
        wa=block_diag_gates(small["b_gate_a_w"][0]).astype(BF16), ba=small["b_gate_a_b"].reshape(1, RG_WIDTH),
        wx=block_diag_gates(small["b_gate_x_w"][0]).astype(BF16), bx=small["b_gate_x_b"].reshape(1, RG_WIDTH))


def layer_c(w_in, w_out, small):
    full = w_in.transpose(1, 0, 2).reshape(D_MODEL, N_DEV * C_SHARD)
    return dict(w_in=full[:, :4 * D_MODEL], w_f=jnp.pad(full[:, 4 * D_MODEL:], ((0, 0), (0, HEAD_DIM - HEADS))),
                w_out=_rows(w_out), grad_slabs=1,
                f_bias=jnp.pad(small["c_f_bias"], ((0, 0), (0, HEAD_DIM - HEADS))))


def c_w_in_blocks(dmain, df):
    full = jnp.concatenate([dmain, df[:, :HEADS].astype(dmain.dtype)], axis=1)
    return full.reshape(D_MODEL, N_DEV, C_SHARD).transpose(1, 0, 2)


def _row_blocks(g):
    return g.reshape(N_DEV, g.shape[0] // N_DEV, g.shape[1])


PACK_LANES = 128
PACK_ALIGN = 8 * PACK_LANES


def pack(parts):
    flat = []
    for p in parts:
        n = p.size
        flat.append(jnp.pad(p.reshape(n), (0, -n % PACK_ALIGN)).reshape(-1, PACK_LANES))
    rows = sum(f.shape[0] for f in flat)
    flat.append(jnp.zeros((-rows % ROW_TILE, PACK_LANES), F32))
    return jnp.concatenate(flat, axis=0)


def unpack(packed, shapes):
    out, row = [], 0
    for shape in shapes:
        n = 1
        for s in shape:
            n *= s
        n_rows = (n + PACK_ALIGN - 1) // PACK_ALIGN * 8
        out.append(packed[row:row + n_rows].reshape(-1)[:n].reshape(shape))
        row += n_rows
    return out


LATE = (("a_rel_bias", slice(0, 1)), ("norm_pre", slice(0, 2)), ("norm_post", slice(0, 1)))
EARLY = (("a_rel_bias", slice(1, 2)), ("norm_pre", slice(2, 4)), ("norm_post", slice(1, 4)),
         ("b_conv_b", slice(None)), ("b_gate_a_w", slice(None)), ("b_gate_a_b", slice(None)),
         ("b_gate_x_w", slice(None)), ("b_gate_x_b", slice(None)), ("b_lambda", slice(None)),
         ("c_f_bias", slice(None)))


def _pieces(tree, pieces):
    return [tree[name][sl] for name, sl in pieces]


def kernel(x, norm_pre, norm_post, a_w_in, a_rel_bias, a_w_out, b_w_in, b_conv_w, b_conv_b, b_gate_a_w, b_gate_a_b, b_gate_x_w, b_gate_x_b, b_lambda, b_w_out, c_w_in, c_f_bias, c_w_out, loss_target, m_norm_pre, m_norm_post, m_a_w_in, m_a_rel_bias, m_a_w_out, m_b_w_in, m_b_conv_w, m_b_conv_b, m_b_gate_a_w, m_b_gate_a_b, m_b_gate_x_w, m_b_gate_x_b, m_b_lambda, m_b_w_out, m_c_w_in, m_c_f_bias, m_c_w_out, v_norm_pre, v_norm_post, v_a_w_in, v_a_rel_bias, v_a_w_out, v_b_w_in, v_b_conv_w, v_b_conv_b, v_b_gate_a_w, v_b_gate_a_b, v_b_gate_x_w, v_b_gate_x_b, v_b_lambda, v_b_w_out, v_c_w_in, v_c_f_bias, v_c_w_out):
    args = dict(locals())
    w = {n: args[n] for n in WEIGHTS}
    m = {n: args["m_" + n] for n in WEIGHTS}
    v = {n: args["v_" + n] for n in WEIGHTS}

    a_in, a_out = a_w_in.astype(BF16), a_w_out.astype(BF16)
    gather_a0 = Ride([a_in[0]], scatter=False, via_sibling=True)
    in_l0_proj = Ride([b_w_in[0].astype(BF16), b_conv_w[0], a_out[0]], scatter=False, via_sibling=True)
    in_l0_attn = Ride([c_w_in[0].astype(BF16), b_w_out[0].astype(BF16)], scatter=False, via_sibling=True)
    in_l1_proj = Ride([c_w_out[0].astype(BF16)], scatter=False)
    in_l2_proj = Ride([a_out[1]], scatter=False)
    in_l2_attn = Ride([a_in[1]], scatter=False, via_sibling=True)
    exchange(gather_a0, "gather_l0")
    rides = {"l0a_proj": in_l0_proj, "l0a_attn": in_l0_attn, "l1b_proj": in_l1_proj, "l2c_proj": in_l2_proj,
             "l2c_attn": in_l2_attn}

    def get_layer(li):
        if li == 0:
            return dict(layer_a(gather_a0.out[0], lambda: _rows(in_l0_proj.out[2]), a_rel_bias[0]),
                        dwin_parts=[(0, 1, 4), (1, 1, 4), (2, 2, 4)], attn_heads=(2, 2))
        if li == 1:
            return layer_b(in_l0_proj.out[0], in_l0_attn.out[1], in_l0_proj.out[1], w)
        if li == 2:
            return dict(layer_c(in_l0_attn.out[0], in_l1_proj.out[0], w), tiles=dict(proj=dict(bn=2048)),
                        attn_heads=(2, 2))
        return dict(layer_a(in_l2_attn.out[0], in_l2_proj.out[0], a_rel_bias[1]), attn_heads=(4, 2))

    grads = [dict() for _ in LAYER_KINDS]
    scatters = {}

    def rel_bias_grad(j, dbias):
        return jax.vjp(band_bias, a_rel_bias[j])[1](dbias)[0][None]

    def early_partial():
        gb, gc = grads[1], grads[2]
        tree = dict(
            a_rel_bias=jnp.concatenate([jnp.zeros((1, HEADS, N_REL), F32), rel_bias_grad(1, grads[3]["bias"])]),
            norm_pre=jnp.concatenate([jnp.zeros((2, D_MODEL), F32)] + [grads[li]["norm_pre"] for li in (2, 3)]),
            norm_post=jnp.concatenate([jnp.zeros((1, D_MODEL), F32)] + [grads[li]["norm_post"] for li in (1, 2, 3)]),
            b_conv_b=gb["conv_b"], b_lambda=gb["lam"],
            b_gate_a_w=block_diag_gates_t(gb["wa"])[None], b_gate_a_b=gb["ba"].reshape(1, RG_BLOCKS, RG_BLOCK),
            b_gate_x_w=block_diag_gates_t(gb["wx"])[None], b_gate_x_b=gb["bx"].reshape(1, RG_BLOCKS, RG_BLOCK),
            c_f_bias=gc["f_bias"][:, :HEADS])
        return pack(_pieces(tree, EARLY))

    def send(key, host, blocks, scatter=True, via_sibling=False):
        ride = rides.setdefault(host, Ride([], scatter, via_sibling))
        assert (ride.scatter, ride.via_sibling) == (scatter, via_sibling)
        scatters[key] = (ride, len(ride.arrs))
        ride.arrs.append(blocks)

    def on_grads(li, name, value):
        g = grads[li]
        g[name] = value
        if (li, name) == (3, "w_out"):
            send("a1_out", "l3a_attn_bwd", _row_blocks(value))
        elif (li, name) == (3, "w_in"):
            send("a1_in", "l2c_attn_bwd", value)
        elif (li, name) == (2, "w_out"):
            send("c_out", "l2c_attn_bwd", _row_blocks(value))
        elif (li, name) == (2, "w_f"):
            blocks = c_w_in_blocks(g["w_in"], value)
            send("c_in_0", "l2c_dh", blocks[:, :D_MODEL // 2])
            send("c_in_1", "l1b_rglru_bwd", blocks[:, D_MODEL // 2:])
        elif (li, name) == (1, "w_out"):
            send("b_out", "l1b_dh", _row_blocks(value))
        elif (li, name) == (1, "w_in"):
            send("b_in", "l0a_attn_bwd", value)
            send("b_conv", "l0a_attn_bwd",
                 g["conv_w"].reshape(CONV_WIDTH, N_DEV, RG_WIDTH // N_DEV).transpose(1, 0, 2))
        elif (li, name) == (1, "lam"):
            send("early", "l1b_dwin", early_partial(), scatter=False, via_sibling=True)
        elif (li, name) == (0, "w_out"):
            send("a0_out", "l0a_attn_bwd", _row_blocks(value))
        elif (li, name) == (0, "w_in_0"):
            send("a0_in_0", "l0a_dwin_1", value)
        elif (li, name) == (0, "w_in_1"):
            send("a0_in_1", "l0a_dwin_2", value)
        elif (li, name) == (0, "w_in_2"):
            send("a0_in_2", "l0a_dh", value)

    loss, grad_x = local_step(x, loss_target, norm_pre, norm_post, get_layer, rides, on_grads)
    late_tree = dict(a_rel_bias=rel_bias_grad(0, grads[0]["bias"]), norm_post=grads[0]["norm_post"],
                     norm_pre=jnp.concatenate([grads[0]["norm_pre"], grads[1]["norm_pre"]]))
    late_parts = exchange(Ride([pack([late_tree[n] for n, _ in LATE])], scatter=False), "gather_late_grads")[0]

    def sharded(name, slab_parts):
        shape = w[name].shape
        slabs = (len(slab_parts), shape[0] * shape[1] // len(slab_parts), shape[2])
        outs = None
        for j, (parts, row0) in enumerate(slab_parts):
            outs = adamw(w[name].reshape(slabs), parts, m[name].reshape(slabs), v[name].reshape(slabs),
                         f"adamw_{name}_{j}", layer=j, prev=outs, part_row0=row0)
        return [o.reshape(shape) for o in outs]

    def received(key):
        ride, position = scatters[key]
        return ride.out[position]

    res = dict(
        a_w_in=sharded("a_w_in", [(received("a0_in_0"), 0), (received("a0_in_1"), 0), (received("a0_in_2"), 0),
                                  (received("a0_in_2"), D_MODEL // 4)]
                       + [(received("a1_in"), q * D_MODEL // 4) for q in range(4)]),
        a_w_out=sharded("a_w_out", [(received("a0_out"), 0), (received("a1_out"), 0)]),
        b_w_in=sharded("b_w_in", [(received("b_in"), 0)]),
        b_w_out=sharded("b_w_out", [(received("b_out"), 0)]),
        b_conv_w=sharded("b_conv_w", [(received("b_conv"), 0)]),
        c_w_in=sharded("c_w_in", [(received("c_in_0"), 0), (received("c_in_1"), 0)]),
        c_w_out=sharded("c_w_out", [(received("c_out"), 0)]))

    packed = {}
    for label, pieces, parts in (("early", EARLY, received("early")), ("late", LATE, late_parts)):
        outs = adamw(pack(_pieces(w, pieces))[None], parts, pack(_pieces(m, pieces))[None],
                     pack(_pieces(v, pieces))[None], f"adamw_replicated_{label}")
        shapes = [w[n][sl].shape for n, sl in pieces]
        packed[label] = [dict(zip([n for n, _ in pieces], unpack(o[0], shapes))) for o in outs]
    for n in ("b_conv_b", "b_gate_a_w", "b_gate_a_b", "b_gate_x_w", "b_gate_x_b", "b_lambda", "c_f_bias"):
        res[n] = [packed["early"][k][n] for k in range(4)]
    for n in ("a_rel_bias", "norm_pre", "norm_post"):
        res[n] = [jnp.concatenate([packed["late"][k][n], packed["early"][k][n]]) for k in range(4)]

    total = lax.psum(loss[0, 0], ("x", "y", "c"))
    return (total, grad_x, *[res[n][0] for n in WEIGHTS], *[res[n][1] for n in WEIGHTS],
            *[res[n][2] for n in WEIGHTS], *[res[n][3] for n in WEIGHTS])
```

```python
import functools

import jax
import jax.numpy as jnp
from jax import lax
from jax.experimental import pallas as pl
from jax.experimental.pallas import tpu as pltpu

F32 = jnp.float32
BF16 = jnp.bfloat16

N_DEV = 8
D_MODEL = 2048
HEADS = 16
HEAD_DIM = 128
CHUNK = 64
LEFT_CHUNKS = 8
REL_CLIP = 256
N_REL = 2 * REL_CLIP + 1
TQ = 256
A_PAD = LEFT_CHUNKS * CHUNK
A_KW = A_PAD + TQ
RG_WIDTH = 2560
RG_BLOCKS = 16
RG_BLOCK = 160
RG_COLS = 640
RG_GROUPS = RG_WIDTH // RG_COLS
RG_C = 8.0
CONV_WIDTH = 4
RMS_EPS = 1e-6
NEG_INF = -1e30
ADAM_LR = 0.001
ADAM_B1 = 0.9
ADAM_B2 = 0.999
ADAM_EPS = 1e-08
ADAM_WD = 0.01
ADAM_STEP = 10
VMEM_LIMIT = 56 * 1024 * 1024
MESH = pl.DeviceIdType.MESH


def _params(sem, vmem=VMEM_LIMIT):
    return pltpu.CompilerParams(dimension_semantics=sem, vmem_limit_bytes=vmem)


def _sigmoid(x):
    return 1.0 / (1.0 + jnp.exp(-x))


def _log1p(y):
    u = 1.0 + y
    return jnp.where(u == 1.0, y, jnp.log(u) * (y / jnp.where(u == 1.0, 1.0, u - 1.0)))


def _softplus(x):
    return jnp.maximum(x, 0.0) + _log1p(jnp.exp(-jnp.abs(x)))


def _dot(a, b, dims):
    return lax.dot_general(a, b, (dims, ((), ())), preferred_element_type=F32)


def _dot_nn(a, b):
    return _dot(a, b, ((1,), (0,)))


def _dot_nt(a, b):
    return _dot(a, b, ((1,), (1,)))


def _dot_tn(a, b):
    return _dot(a, b, ((0,), (0,)))


def _peers():
    x, y, c = lax.axis_index("x"), lax.axis_index("y"), lax.axis_index("c")
    me = 4 * x + 2 * y + c
    peers = []
    for k in range(1, N_DEV):
        px = 1 - x if k & 4 else x
        py = 1 - y if k & 2 else y
        pc = 1 - c if k & 1 else c
        peers.append(((px, py, pc), 4 * px + 2 * py + pc))
    return me, peers


class Ride:
    def __init__(self, arrs, scatter, via_sibling=False):
        assert not (scatter and via_sibling)
        self.arrs, self.scatter, self.via_sibling, self.out = list(arrs), scatter, via_sibling, None

    def out_shapes(self):
        return [jax.ShapeDtypeStruct(a.shape if self.scatter else (N_DEV,) + a.shape, a.dtype) for a in self.arrs]

    def sem_shapes(self):
        n = len(self.arrs)
        return [pltpu.SemaphoreType.DMA((n, N_DEV - 1)), pltpu.SemaphoreType.DMA((n, N_DEV - 1)),
                pltpu.SemaphoreType.DMA((n,))]

    def _copies(self, ins, outs, sems, landing):
        send_sems, recv_sems, local_sems = sems
        me, peers = _peers()
        local, remote = [], []
        for a, (src, dst) in enumerate(zip(ins, outs)):
            local.append(pltpu.make_async_copy(src.at[me] if self.scatter else src, dst.at[me], local_sems.at[a]))
            for k, (peer, peer_idx) in enumerate(peers):
                remote.append(pltpu.make_async_remote_copy(
                    src_ref=src.at[peer_idx] if self.scatter else src, dst_ref=dst.at[peer_idx if landing else me],
                    send_sem=send_sems.at[a, k], recv_sem=recv_sems.at[a, k], device_id=peer, device_id_type=MESH))
        return local, remote

    def _direct(self, k):
        return not self.via_sibling or k == 0 or (k + 1) % 2 == 0

    def start(self, ins, outs, sems):
        local, remote = self._copies(ins, outs, sems, landing=False)
        n_peers = N_DEV - 1
        for cp in local + [cp for i, cp in enumerate(remote) if self._direct(i % n_peers)]:
            cp.start()

    def wait(self, ins, outs, sems):
        local, remote = self._copies(ins, outs, sems, landing=True)
        n_peers = N_DEV - 1
        for i, cp in enumerate(remote):
            if self._direct(i % n_peers):
                cp.wait()
        if self.via_sibling:
            send_sems, recv_sems, _ = sems
            me, peers = _peers()
            sibling = peers[0][0]
            passed = []
            for a, dst in enumerate(outs):
                for j in range(1, n_peers, 2):
                    came, lands = peers[j][1], peers[j + 1][1]
                    pltpu.make_async_remote_copy(
                        src_ref=dst.at[came], dst_ref=dst.at[came], send_sem=send_sems.at[a, j + 1],
                        recv_sem=recv_sems.at[a, j + 1], device_id=sibling, device_id_type=MESH).start()
                    passed.append(pltpu.make_async_remote_copy(
                        src_ref=dst.at[came], dst_ref=dst.at[lands], send_sem=send_sems.at[a, j + 1],
                        recv_sem=recv_sems.at[a, j + 1], device_id=sibling, device_id_type=MESH))
            for cp in passed:
                cp.wait()
        for cp in local:
            cp.wait()


def _call(body, *, name, grid, in_specs, out_specs, out_shape, args, scratch_shapes=(), semantics=None, ride=None,
          aliases=None):
    scratch_shapes = list(scratch_shapes)
    if ride is None:
        return pl.pallas_call(
            body, name=name, grid=grid, in_specs=in_specs, out_specs=out_specs, out_shape=out_shape,
            scratch_shapes=scratch_shapes, input_output_aliases=aliases or {},
            compiler_params=_params(semantics if grid else None))(*args)
    assert not aliases
    n_in, n_out, n_sc, n_r = len(in_specs), len(out_specs), len(scratch_shapes), len(ride.arrs)

    def riding(*refs):
        ins, r_ins = refs[:n_in], refs[n_in:n_in + n_r]
        outs, r_outs = refs[n_in + n_r:n_in + n_r + n_out], refs[n_in + n_r + n_out:n_in + 2 * n_r + n_out]
        rest = refs[n_in + 2 * n_r + n_out:]
        scratch, sems = rest[:n_sc], rest[n_sc:]
        first = last = None
        for axis, size in enumerate(grid):
            pid = pl.program_id(axis)
            first = (pid == 0) if first is None else first & (pid == 0)
            last = (pid == size - 1) if last is None else last & (pid == size - 1)
        if grid:
            pl.when(first)(lambda: ride.start(r_ins, r_outs, sems))
        else:
            ride.start(r_ins, r_outs, sems)
        body(*ins, *outs, *scratch)
        if grid:
            pl.when(last)(lambda: ride.wait(r_ins, r_outs, sems))
        else:
            ride.wait(r_ins, r_outs, sems)

    any_spec = pl.BlockSpec(memory_space=pl.ANY)
    res = pl.pallas_call(
        riding, name=name, grid=grid, in_specs=list(in_specs) + [any_spec] * n_r,
        out_specs=list(out_specs) + [any_spec] * n_r, out_shape=list(out_shape) + ride.out_shapes(),
        scratch_shapes=scratch_shapes + ride.sem_shapes(),
        compiler_params=_params(("arbitrary",) * len(grid) if grid else None))(*args, *ride.arrs)
    ride.out = list(res[n_out:])
    return list(res[:n_out])


def exchange(ride, name):
    _call(lambda: None, name=name, grid=(), in_specs=[], out_specs=[], out_shape=[], args=[], ride=ride)
    return ride.out


LANES = 128


def _fit(dims, want):
    dims = tuple(dims)
    if len(set(dims)) == 1 and dims[0] <= want:
        return dims[0]
    return max(t for t in range(LANES, want + 1, LANES) if all(d % t == 0 for d in dims))


def _cols(arr):
    return arr.shape[-1] * (arr.shape[0] if len(arr.shape) == 3 else 1)


def _tile_spec(shape, rblk, cblk, rc):
    if len(shape) == 2:
        return pl.BlockSpec((rblk, cblk), rc)
    per = shape[2] // cblk

    def index_map(*ids):
        r, c = rc(*ids)
        return (c // per, r, c % per)

    return pl.BlockSpec((1, rblk, cblk), index_map)


def matmul(a, b, *, mode, out_dtype, name, bm=1024, bn=1024, bk=2048, out_slabs=1, m_part=(0, 1, 1), ride=None):
    a_rows, a_cols, b_rows, b_cols = a.shape[-2], _cols(a), b.shape[-2], _cols(b)
    (K, M) = (a_rows, a_cols) if mode == "tn" else (a_cols, a_rows)
    N = b_rows if mode == "nt" else b_cols
    assert K == (b_cols if mode == "nt" else b_rows), (name, a.shape, b.shape)
    first_range, n_ranges, of_ranges = m_part
    row0, M = first_range * (M // of_ranges), n_ranges * (M // of_ranges)
    out_shape = (M, N) if out_slabs == 1 else (out_slabs, M, N // out_slabs)
    widths = dict(m=[M], n=[N, out_shape[-1]], k=[K])
    widths["m" if mode == "tn" else "k"].append(a.shape[-1])
    widths["k" if mode == "nt" else "n"].append(b.shape[-1])
    bm, bn, bk = _fit(widths["m"], bm), _fit(widths["n"], bn), _fit(widths["k"], bk)
    nk = K // bk
    dims = {"nn": ((1,), (0,)), "nt": ((1,), (1,)), "tn": ((0,), (0,))}[mode]

    def val(ref):
        return ref[0] if len(ref.shape) == 3 else ref[...]

    def put(ref, x):
        if len(ref.shape) == 3:
            ref[0] = x.astype(ref.dtype)
        else:
            ref[...] = x.astype(ref.dtype)

    def body(a_ref, b_ref, o_ref, *scratch):
        if nk == 1:
            put(o_ref, _dot(val(a_ref), val(b_ref), dims))
            return
        acc_ref, = scratch
        k = pl.program_id(2)

        @pl.when(k == 0)
        def _():
            acc_ref[...] = jnp.zeros_like(acc_ref)

        acc_ref[...] += _dot(val(a_ref), val(b_ref), dims)

        @pl.when(k == nk - 1)
        def _():
            put(o_ref, acc_ref[...])

    assert row0 % bm == 0
    m0 = row0 // bm
    if mode == "tn":
        a_spec = _tile_spec(a.shape, bk, bm, lambda j, i, k: (k, m0 + i))
    else:
        a_spec = _tile_spec(a.shape, bm, bk, lambda j, i, k: (m0 + i, k))
    if mode == "nt":
        b_spec = _tile_spec(b.shape, bn, bk, lambda j, i, k: (j, k))
    else:
        b_spec = _tile_spec(b.shape, bk, bn, lambda j, i, k: (k, j))
    return _call(
        body, name=name, grid=(N // bn, M // bm, nk), in_specs=[a_spec, b_spec],
        out_specs=[_tile_spec(out_shape, bm, bn, lambda j, i, k: (i, j))],
        out_shape=[jax.ShapeDtypeStruct(out_shape, out_dtype)],
        scratch_shapes=[] if nk == 1 else [pltpu.VMEM((bm, bn), F32)],
        semantics=("parallel", "parallel", "arbitrary"), args=[a, b], ride=ride)[0]


ROW_TILE = 256


def _rms_stats(z):
    r = lax.rsqrt(jnp.mean(z * z, axis=-1, keepdims=True) + RMS_EPS)
    return r, z * r


def _rms_bwd(n, r, g, dout):
    dn = dout * g
    return r * (dn - n * jnp.mean(dn * n, axis=-1, keepdims=True))


def _row_spec(T, Dm):
    bt = min(ROW_TILE, T)
    return bt, pl.BlockSpec((bt, Dm), lambda i: (i, 0)), pl.BlockSpec((1, Dm), lambda i: (0, 0))


def prenorm_fwd(x, g, name):
    T, Dm = x.shape
    bt, row, vec = _row_spec(T, Dm)

    def body(x_ref, g_ref, h_ref):
        _, n = _rms_stats(x_ref[...])
        h_ref[...] = (n * g_ref[...]).astype(BF16)

    return pl.pallas_call(
        body, name=name, grid=(T // bt,), in_specs=[row, vec], out_specs=row,
        out_shape=jax.ShapeDtypeStruct((T, Dm), BF16), compiler_params=_params(("parallel",)),
    )(x, g)


def postnorm_prenorm_fwd(x, y, g_post, g_next, name):
    T, Dm = x.shape
    bt, row, vec = _row_spec(T, Dm)

    def body(x_ref, y_ref, gp_ref, gn_ref, o_ref, h_ref):
        _, n = _rms_stats(y_ref[...])
        x_new = x_ref[...] + n * gp_ref[...]
        o_ref[...] = x_new
        _, n_new = _rms_stats(x_new)
        h_ref[...] = (n_new * gn_ref[...]).astype(BF16)

    return pl.pallas_call(
        body, name=name, grid=(T // bt,), in_specs=[row, row, vec, vec], out_specs=[row, row],
        out_shape=[jax.ShapeDtypeStruct((T, Dm), F32), jax.ShapeDtypeStruct((T, Dm), BF16)],
        compiler_params=_params(("parallel",)),
    )(x, y, g_post, g_next)


def postnorm_loss(x, y, g, target, name):
    T, Dm = x.shape
    bt, row, vec = _row_spec(T, Dm)

    def body(x_ref, y_ref, g_ref, t_ref, l_ref, d_ref):
        @pl.when(pl.program_id(0) == 0)
        def _():
            l_ref[...] = jnp.zeros_like(l_ref)

        _, n = _rms_stats(y_ref[...])
        err = (x_ref[...] + n * g_ref[...]) - t_ref[...]
        per_tok = jnp.mean(err * err, axis=-1, keepdims=True)
        l_ref[...] += 0.5 * jnp.sum(per_tok, axis=0, keepdims=True)
        d_ref[...] = err * (1.0 / Dm)

    return pl.pallas_call(
        body, name=name, grid=(T // bt,), in_specs=[row, row, vec, row],
        out_specs=[pl.BlockSpec((1, 1), lambda i: (0, 0)), row],
        out_shape=[jax.ShapeDtypeStruct((1, 1), F32), jax.ShapeDtypeStruct((T, Dm), F32)],
        compiler_params=_params(("arbitrary",)),
    )(x, y, g, target)


def postnorm_bwd(y, g, dout, name):
    T, Dm = y.shape
    bt, row, vec = _row_spec(T, Dm)

    def body(y_ref, g_ref, d_ref, dy_ref, dg_ref):
        @pl.when(pl.program_id(0) == 0)
        def _():
            dg_ref[...] = jnp.zeros_like(dg_ref)

        r, n = _rms_stats(y_ref[...])
        dout_v = d_ref[...]
        dg_ref[...] += jnp.sum(dout_v * n, axis=0, keepdims=True)
        dy_ref[...] = _rms_bwd(n, r, g_ref[...], dout_v).astype(BF16)

    return pl.pallas_call(
        body, name=name, grid=(T // bt,), in_specs=[row, vec, row], out_specs=[row, vec],
        out_shape=[jax.ShapeDtypeStruct((T, Dm), BF16), jax.ShapeDtypeStruct((1, Dm), F32)],
        compiler_params=_params(("arbitrary",)),
    )(y, g, dout)


def prenorm_bwd(x, g, dhs, dres, name):
    T, Dm = x.shape
    bt, row, vec = _row_spec(T, Dm)
    n_dh = len(dhs)

    def body(x_ref, g_ref, *refs):
        dh_refs, (dr_ref, dx_ref, dg_ref) = refs[:n_dh], refs[n_dh:]

        @pl.when(pl.program_id(0) == 0)
        def _():
            dg_ref[...] = jnp.zeros_like(dg_ref)

        r, n = _rms_stats(x_ref[...])
        dh_v = dh_refs[0][...]
        for extra in dh_refs[1:]:
            dh_v = dh_v + extra[...]
        dg_ref[...] += jnp.sum(dh_v * n, axis=0, keepdims=True)
        dx_ref[...] = dr_ref[...] + _rms_bwd(n, r, g_ref[...], dh_v)

    return pl.pallas_call(
        body, name=name, grid=(T // bt,), in_specs=[row, vec] + [row] * (n_dh + 1), out_specs=[row, vec],
        out_shape=[jax.ShapeDtypeStruct((T, Dm), F32), jax.ShapeDtypeStruct((1, Dm), F32)],
        compiler_params=_params(("arbitrary",)),
    )(x, g, *dhs, dres)


def _silu(g):
    return g * _sigmoid(g)


def _gate_bwd(dgated, core, g):
    sg = _sigmoid(g)
    return dgated * (g * sg), dgated * core * (sg * (1.0 + g * (1.0 - sg)))


def _softmax_rows(s):
    e = jnp.exp(s - jnp.max(s, axis=-1, keepdims=True))
    return e * (1.0 / jnp.sum(e, axis=-1, keepdims=True))


def _band_scores(qk, bias, r0):
    s = qk * (HEAD_DIM ** -0.5) + bias
    j = lax.broadcasted_iota(jnp.int32, s.shape, 1)
    return jnp.where(j >= A_PAD - r0, s, NEG_INF)


def _fill_padded_kv(p_ref, kp_ref, vp_ref):
    zeros = jnp.zeros((A_PAD, kp_ref.shape[1]), BF16)
    kp_ref[0:A_PAD, :] = zeros
    vp_ref[0:A_PAD, :] = zeros
    kp_ref[A_PAD:, :] = p_ref[1, 0].astype(BF16)
    vp_ref[A_PAD:, :] = p_ref[2, 0].astype(BF16)


def _head_specs(S, order, heads=1):
    def idx(fn):
        return lambda *ids: fn(**dict(zip(order, ids)))

    return (pl.BlockSpec((4, 1, S, heads * HEAD_DIM), idx(lambda b, h, t: (0, b, 0, h))),
            pl.BlockSpec((1, TQ, heads * HEAD_DIM), idx(lambda b, h, t: (b, t, h))))


def attn_a_fwd(proj, bias, name, heads=1, ride=None):
    _, B, S, W = proj.shape
    nt = S // TQ

    def body(p_ref, b_ref, o_ref, gt_ref, kp_ref, vp_ref):
        t = pl.program_id(2)

        @pl.when(t == 0)
        def _():
            _fill_padded_kv(p_ref, kp_ref, vp_ref)

        r0 = pl.multiple_of(t * TQ, TQ)
        for e in range(heads):
            lanes = slice(e * HEAD_DIM, (e + 1) * HEAD_DIM)
            q = p_ref[0, 0, pl.ds(r0, TQ), lanes].astype(BF16)
            g = p_ref[3, 0, pl.ds(r0, TQ), lanes]
            p = _softmax_rows(_band_scores(_dot_nt(q, kp_ref[pl.ds(r0, A_KW), lanes]), b_ref[e], r0))
            o = _dot_nn(p.astype(BF16), vp_ref[pl.ds(r0, A_KW), lanes])
            o_ref[0, :, lanes] = o
            gt_ref[0, :, lanes] = (o * _silu(g)).astype(BF16)

    seq, tile = _head_specs(S, "bht", heads)
    kv = pltpu.VMEM((A_PAD + S, heads * HEAD_DIM), BF16)
    return _call(
        body, name=name, grid=(B, HEADS // heads, nt),
        in_specs=[seq, pl.BlockSpec((heads, TQ, A_KW), lambda b, h, t: (h, 0, 0))], out_specs=[tile, tile],
        out_shape=[jax.ShapeDtypeStruct((B, S, W), F32), jax.ShapeDtypeStruct((B, S, W), BF16)],
        scratch_shapes=[kv, kv],
        semantics=("parallel", "parallel", "arbitrary"), args=[proj, bias], ride=ride)


def attn_a_bwd(proj, bias, o, dgated, name, heads=1, ride=None):
    _, B, S, W = proj.shape
    nt = S // TQ

    def body(p_ref, b_ref, o_ref, dgt_ref, dp_ref, db_ref, kp_ref, vp_ref, dk_ref, dv_ref):
        b_, t = pl.program_id(1), pl.program_id(2)

        @pl.when(t == 0)
        def _():
            _fill_padded_kv(p_ref, kp_ref, vp_ref)
            dk_ref[...] = jnp.zeros_like(dk_ref)
            dv_ref[...] = jnp.zeros_like(dv_ref)

        @pl.when((t == 0) & (b_ == 0))
        def _():
            db_ref[...] = jnp.zeros_like(db_ref)

        r0 = pl.multiple_of(t * TQ, TQ)
        rows, win = pl.ds(r0, TQ), pl.ds(r0, A_KW)
        for e in range(heads):
            lanes = slice(e * HEAD_DIM, (e + 1) * HEAD_DIM)
            q = p_ref[0, 0, rows, lanes].astype(BF16)
            g = p_ref[3, 0, rows, lanes]
            kw, vw = kp_ref[win, lanes], vp_ref[win, lanes]
            p = _softmax_rows(_band_scores(_dot_nt(q, kw), b_ref[e], r0))
            do, dg = _gate_bwd(dgt_ref[0, :, lanes], o_ref[0, :, lanes], g)
            do = do.astype(BF16)
            dv_ref[win, lanes] += _dot_tn(p.astype(BF16), do)
            dpr = _dot_nt(do, vw)
            ds = p * (dpr - jnp.sum(p * dpr, axis=-1, keepdims=True))
            db_ref[e] += ds
            ds = (ds * (HEAD_DIM ** -0.5)).astype(BF16)
            dk_ref[win, lanes] += _dot_tn(ds, q)
            dp_ref[0, 0, rows, lanes] = _dot_nn(ds, kw).astype(BF16)
            dp_ref[3, 0, rows, lanes] = dg.astype(BF16)

        @pl.when(t == nt - 1)
        def _():
            dp_ref[1, 0] = dk_ref[A_PAD:, :].astype(BF16)
            dp_ref[2, 0] = dv_ref[A_PAD:, :].astype(BF16)

    seq, tile = _head_specs(S, "hbt", heads)
    bias_spec = pl.BlockSpec((heads, TQ, A_KW), lambda h, b, t: (h, 0, 0))
    kv = pltpu.VMEM((A_PAD + S, heads * HEAD_DIM), BF16)
    acc = pltpu.VMEM((A_PAD + S, heads * HEAD_DIM), F32)
    return _call(
        body, name=name, grid=(HEADS // heads, B, nt), in_specs=[seq, bias_spec, tile, tile],
        out_specs=[seq, bias_spec],
        out_shape=[jax.ShapeDtypeStruct(proj.shape, BF16), jax.ShapeDtypeStruct(bias.shape, F32)],
        scratch_shapes=[kv, kv, acc, acc],
        semantics=("arbitrary", "arbitrary", "arbitrary"), args=[proj, bias, o, dgated], ride=ride)


def band_bias(rel_bias):
    length = TQ + A_KW - 1
    first = REL_CLIP + 1 - TQ
    gen = jnp.concatenate([rel_bias[:, first:],
                           jnp.broadcast_to(rel_bias[:, 2 * REL_CLIP:], (HEADS, length - (N_REL - first)))], axis=1)
    rev = jnp.concatenate([gen[:, ::-1], jnp.zeros((HEADS, 1), rel_bias.dtype)], axis=1)
    sheared = jnp.tile(rev, (1, TQ))[:, :TQ * length].reshape(HEADS, TQ, length)
    i = lax.broadcasted_iota(jnp.int32, (TQ, A_KW), 0)
    j = lax.broadcasted_iota(jnp.int32, (TQ, A_KW), 1)
    first_key = (i // CHUNK) * CHUNK
    in_band = (j >= first_key) & (j < first_key + (LEFT_CHUNKS + 1) * CHUNK)
    return jnp.where(in_band, sheared[:, :, TQ - 1:], NEG_INF)


def _group_scan(a, u, carry, reverse=False):
    row = lax.broadcasted_iota(jnp.int32, u.shape, 0)
    for k in (1, 2, 4):
        shift = 8 - k if reverse else k
        valid = (row < 8 - k) if reverse else (row >= k)
        u_sh = pltpu.roll(u, shift, 0)
        if a is None:
            u = jnp.where(valid, u + u_sh, u)
        else:
            a_sh = pltpu.roll(a, shift, 0)
            u = jnp.where(valid, a * u_sh + u, u)
            a = jnp.where(valid, a * a_sh, a)
    return (u + carry) if a is None else (a * carry + u)


SCAN_UNROLL = 4


def _scan_rows(n_rows, step, carry0, reverse=False):
    groups = n_rows // 8

    def loop(i, carry):
        gi = (groups - 1 - i) if reverse else i
        return step(pl.multiple_of(gi * 8, 8), carry)

    return lax.fori_loop(0, groups, loop, carry0, unroll=SCAN_UNROLL)


def fox_cum_fwd(f_logit, f_bias, name):
    B, S, L = f_logit.shape

    def body(f_ref, b_ref, c_ref):
        z = f_ref[0] + b_ref[...]
        c_ref[0] = jnp.minimum(z, 0.0) - _log1p(jnp.exp(-jnp.abs(z)))

        def step(r0, carry):
            h = _group_scan(None, c_ref[0, pl.ds(r0, 8), :], carry)
            c_ref[0, pl.ds(r0, 8), :] = h
            return h[7:8, :]

        _scan_rows(S, step, jnp.zeros((1, L), F32))

    return pl.pallas_call(
        body, name=name, grid=(B,),
        in_specs=[pl.BlockSpec((1, S, L), lambda b: (b, 0, 0)), pl.BlockSpec((1, L), lambda b: (0, 0))],
        out_specs=pl.BlockSpec((1, S, L), lambda b: (b, 0, 0)),
        out_shape=jax.ShapeDtypeStruct((B, S, L), F32), compiler_params=_params(("parallel",)),
    )(f_logit, f_bias)


def fox_cum_bwd(f_logit, f_bias, dcum, name):
    B, S, L = f_logit.shape

    def body(f_ref, b_ref, d_ref, df_ref, db_ref):
        @pl.when(pl.program_id(0) == 0)
        def _():
            db_ref[...] = jnp.zeros_like(db_ref)

        def step(r0, carry):
            h = _group_scan(None, d_ref[0, pl.ds(r0, 8), :], carry, reverse=True)
            df_ref[0, pl.ds(r0, 8), :] = h
            return h[0:1, :]

        _scan_rows(S, step, jnp.zeros((1, L), F32), reverse=True)
        df = df_ref[0] * _sigmoid(-(f_ref[0] + b_ref[...]))
        df_ref[0] = df
        db_ref[...] += jnp.sum(df, axis=0, keepdims=True)

    seq = pl.BlockSpec((1, S, L), lambda b: (b, 0, 0))
    vec = pl.BlockSpec((1, L), lambda b: (0, 0))
    return pl.pallas_call(
        body, name=name, grid=(B,), in_specs=[seq, vec, seq], out_specs=[seq, vec],
        out_shape=[jax.ShapeDtypeStruct((B, S, L), F32), jax.ShapeDtypeStruct((1, L), F32)],
        compiler_params=_params(("arbitrary",)),
    )(f_logit, f_bias, dcum)


def _head_row(cr, h):
    sub = lax.broadcasted_iota(jnp.int32, cr.shape, 0)
    return jnp.sum(jnp.where(sub == h, cr, 0.0), axis=0, keepdims=True)


def _fox_scores(qk, cc, ck, h, r0):
    lane = lax.broadcasted_iota(jnp.int32, cc.shape, 1)
    cq = jnp.sum(jnp.where(lane == h, cc, 0.0), axis=1, keepdims=True)
    s = qk * (HEAD_DIM ** -0.5) + (cq - ck)
    qpos = r0 + lax.broadcasted_iota(jnp.int32, s.shape, 0)
    kpos = lax.broadcasted_iota(jnp.int32, s.shape, 1)
    return jnp.where(kpos <= qpos, s, NEG_INF)


KEY_STEP = 256


def _by_causal_width(t, S, fn):
    per = KEY_STEP // TQ
    for c in range(S // KEY_STEP):
        pl.when(t // per == c)(functools.partial(fn, (c + 1) * KEY_STEP))


def fox_fwd(proj, cum_col, cum_row, name, heads=1, ride=None):
    _, B, S, W = proj.shape
    nt = S // TQ

    def body(p_ref, cc_ref, cr_ref, o_ref, gt_ref, k_ref, v_ref):
        head0, t = pl.program_id(1) * heads, pl.program_id(2)

        @pl.when(t == 0)
        def _():
            k_ref[...] = p_ref[1, 0].astype(BF16)
            v_ref[...] = p_ref[2, 0].astype(BF16)

        r0 = pl.multiple_of(t * TQ, TQ)

        def tile_out(width):
            for e in range(heads):
                h, lanes = head0 + e, slice(e * HEAD_DIM, (e + 1) * HEAD_DIM)
                q = p_ref[0, 0, pl.ds(r0, TQ), lanes].astype(BF16)
                ck = _head_row(cr_ref[0, :, 0:width], h)
                p = _softmax_rows(_fox_scores(_dot_nt(q, k_ref[0:width, lanes]), cc_ref[0], ck, h, r0))
                o = _dot_nn(p.astype(BF16), v_ref[0:width, lanes])
                o_ref[0, :, lanes] = o
                gt_ref[0, :, lanes] = (o * _silu(p_ref[3, 0, pl.ds(r0, TQ), lanes])).astype(BF16)

        _by_causal_width(t, S, tile_out)

    seq, tile = _head_specs(S, "bht", heads)
    kv = pltpu.VMEM((S, heads * HEAD_DIM), BF16)
    return _call(
        body, name=name, grid=(B, HEADS // heads, nt),
        in_specs=[seq, pl.BlockSpec((1, TQ, cum_col.shape[2]), lambda b, h, t: (b, t, 0)),
                  pl.BlockSpec((1, HEADS, S), lambda b, h, t: (b, 0, 0))],
        out_specs=[tile, tile],
        out_shape=[jax.ShapeDtypeStruct((B, S, W), F32), jax.ShapeDtypeStruct((B, S, W), BF16)],
        scratch_shapes=[kv, kv],
        semantics=("parallel", "parallel", "arbitrary"), args=[proj, cum_col, cum_row], ride=ride)


def fox_bwd(proj, cum_col, cum_row, o, dgated, name, heads=1, ride=None):
    _, B, S, W = proj.shape
    nt = S // TQ

    def body(p_ref, cc_ref, cr_ref, o_ref, dgt_ref, dp_ref, dc_ref, k_ref, v_ref, dk_ref, dv_ref):
        head0, t = pl.program_id(1) * heads, pl.program_id(2)

        @pl.when(t == 0)
        def _():
            k_ref[...] = p_ref[1, 0].astype(BF16)
            v_ref[...] = p_ref[2, 0].astype(BF16)
            dk_ref[...] = jnp.zeros_like(dk_ref)
            dv_ref[...] = jnp.zeros_like(dv_ref)
            dc_ref[...] = jnp.zeros_like(dc_ref)

        r0 = pl.multiple_of(t * TQ, TQ)
        rows = pl.ds(r0, TQ)

        def tile_grads(width):
            for e in range(heads):
                h, lanes = head0 + e, slice(e * HEAD_DIM, (e + 1) * HEAD_DIM)
                q = p_ref[0, 0, rows, lanes].astype(BF16)
                do, dg = _gate_bwd(dgt_ref[0, :, lanes], o_ref[0, :, lanes], p_ref[3, 0, rows, lanes])
                do = do.astype(BF16)
                dp_ref[3, 0, rows, lanes] = dg.astype(BF16)
                k, v = k_ref[0:width, lanes], v_ref[0:width, lanes]
                ck = _head_row(cr_ref[0, :, 0:width], h)
                p = _softmax_rows(_fox_scores(_dot_nt(q, k), cc_ref[0], ck, h, r0))
                dv_ref[0:width, lanes] += _dot_tn(p.astype(BF16), do)
                dpr = _dot_nt(do, v)
                ds = p * (dpr - jnp.sum(p * dpr, axis=-1, keepdims=True))
                dc_ref[0, e, :, 0:width] += jnp.sum(ds, axis=0, keepdims=True)
                ds = (ds * (HEAD_DIM ** -0.5)).astype(BF16)
                dk_ref[0:width, lanes] += _dot_tn(ds, q)
                dp_ref[0, 0, rows, lanes] = _dot_nn(ds, k).astype(BF16)

        _by_causal_width(t, S, tile_grads)

        @pl.when(t == nt - 1)
        def _():
            dp_ref[1, 0] = dk_ref[...].astype(BF16)
            dp_ref[2, 0] = dv_ref[...].astype(BF16)

    seq, tile = _head_specs(S, "bht", heads)
    kv = pltpu.VMEM((S, heads * HEAD_DIM), BF16)
    acc = pltpu.VMEM((S, heads * HEAD_DIM), F32)
    return _call(
        body, name=name, grid=(B, HEADS // heads, nt),
        in_specs=[seq, pl.BlockSpec((1, TQ, cum_col.shape[2]), lambda b, h, t: (b, t, 0)),
                  pl.BlockSpec((1, HEADS, S), lambda b, h, t: (b, 0, 0)), tile, tile],
        out_specs=[seq, pl.BlockSpec((1, heads, 1, S), lambda b, h, t: (b, h, 0, 0))],
        out_shape=[jax.ShapeDtypeStruct(proj.shape, BF16), jax.ShapeDtypeStruct((B, HEADS, 1, S), F32)],
        scratch_shapes=[kv, kv, acc, acc],
        semantics=("parallel", "parallel", "arbitrary"), args=[proj, cum_col, cum_row, o, dgated], ride=ride)


RG_ROWS = 512


def _rg_gates(xc, wa_ref, ba_ref, wx_ref, bx_ref, lam_ref):
    xcb = xc.astype(BF16)
    r = _sigmoid(_dot_nn(xcb, wa_ref[0]) + ba_ref[...])
    i = _sigmoid(_dot_nn(xcb, wx_ref[0]) + bx_ref[...])
    sp = _softplus(-lam_ref[...])
    log_a = (-RG_C * sp) * r
    a = jnp.exp(log_a)
    m = jnp.sqrt(-jnp.tanh(log_a) * (a * a + 1.0))
    return xcb, r, i, sp, a, m


def _rg_specs(B, S, rows, order):
    nc = S // rows

    def idx(fn):
        def index_map(*ids):
            v = dict(zip(order.lower(), ids))
            c = (nc - 1 - v["c"]) if "C" in order else v["c"]
            return fn(v["b"], v["d"], c)
        return index_map

    return dict(
        proj=pl.BlockSpec((2, 1, rows, RG_COLS), idx(lambda b, d, c: (0, b, c, d))),
        act=pl.BlockSpec((1, rows, RG_COLS), idx(lambda b, d, c: (b, c, d))),
        taps=pl.BlockSpec((CONV_WIDTH, RG_COLS), idx(lambda b, d, c: (0, d))),
        vec=pl.BlockSpec((1, RG_COLS), idx(lambda b, d, c: (0, d))),
        gate=pl.BlockSpec((1, RG_COLS, RG_COLS), idx(lambda b, d, c: (d, 0, 0))),
    )


def rglru_fwd(proj, conv_w, conv_b, wa, ba, wx, bx, lam, name, rows=RG_ROWS, ride=None):
    _, B, S, _ = proj.shape
    rows = min(rows, S)
    sp_ = _rg_specs(B, S, rows, "bdc")

    def body(p_ref, cw_ref, cb_ref, wa_ref, ba_ref, wx_ref, bx_ref, lam_ref,
             xc_ref, hs_ref, hp_ref, gt_ref, ext_ref, a_ref, u_ref, xcar_ref, hcar_ref):
        @pl.when(pl.program_id(2) == 0)
        def _():
            xcar_ref[...] = jnp.zeros_like(xcar_ref)
            hcar_ref[...] = jnp.zeros_like(hcar_ref)

        xr = p_ref[0, 0]
        ext_ref[0:8, :] = xcar_ref[...]
        ext_ref[8:, :] = xr
        xcar_ref[...] = xr[rows - 8:, :]
        xc = ext_ref[pl.ds(5, rows), :] * cw_ref[0:1, :]
        xc = xc + ext_ref[pl.ds(6, rows), :] * cw_ref[1:2, :]
        xc = xc + ext_ref[pl.ds(7, rows), :] * cw_ref[2:3, :]
        xc = xc + xr * cw_ref[3:4, :] + cb_ref[...]
        xc_ref[0] = xc
        _, _, i, _, a, m = _rg_gates(xc, wa_ref, ba_ref, wx_ref, bx_ref, lam_ref)
        a_ref[...] = a
        u_ref[...] = m * (i * xc)

        def step(r0, carry):
            h = _group_scan(a_ref[pl.ds(r0, 8), :], u_ref[pl.ds(r0, 8), :], carry)
            row = lax.broadcasted_iota(jnp.int32, h.shape, 0)
            hs_ref[0, pl.ds(r0, 8), :] = h
            hp_ref[0, pl.ds(r0, 8), :] = jnp.where(row == 0, carry, pltpu.roll(h, 1, 0))
            return h[7:8, :]

        hcar_ref[0:1, :] = _scan_rows(rows, step, hcar_ref[0:1, :])
        gt_ref[0] = (hs_ref[0] * _silu(p_ref[1, 0])).astype(BF16)

    act = jax.ShapeDtypeStruct((B, S, RG_WIDTH), F32)
    return _call(
        body, name=name, grid=(B, RG_GROUPS, S // rows),
        in_specs=[sp_["proj"], sp_["taps"], sp_["vec"], sp_["gate"], sp_["vec"], sp_["gate"], sp_["vec"], sp_["vec"]],
        out_specs=[sp_["act"]] * 4,
        out_shape=[act, act, act, jax.ShapeDtypeStruct((B, S, RG_WIDTH), BF16)],
        scratch_shapes=[pltpu.VMEM((rows + 8, RG_COLS), F32), pltpu.VMEM((rows, RG_COLS), F32),
                        pltpu.VMEM((rows, RG_COLS), F32), pltpu.VMEM((8, RG_COLS), F32), pltpu.VMEM((8, RG_COLS), F32)],
        semantics=("parallel", "parallel", "arbitrary"), args=[proj, conv_w, conv_b, wa, ba, wx, bx, lam], ride=ride)


def rglru_bwd(proj, xc, hs, hprev, dgated, conv_w, wa, ba, wx, bx, lam, name, rows=RG_ROWS, ride=None):
    _, B, S, _ = proj.shape
    rows = min(rows, S)
    sp_ = _rg_specs(B, S, rows, "dbC")

    def body(p_ref, xc_ref, hs_ref, hp_ref, dgt_ref, cw_ref, wa_ref, ba_ref, wx_ref, bx_ref, lam_ref,
             dp_ref, dcw_ref, dcb_ref, dwa_ref, dba_ref, dwx_ref, dbx_ref, dlam_ref,
             ext_ref, c_ref, l_ref, acar_ref, lcar_ref, dcar_ref):
        b_, c_ = pl.program_id(1), pl.program_id(2)

        @pl.when(c_ == 0)
        def _():
            acar_ref[...] = jnp.zeros_like(acar_ref)
            lcar_ref[...] = jnp.zeros_like(lcar_ref)
            dcar_ref[...] = jnp.zeros_like(dcar_ref)

        @pl.when((c_ == 0) & (b_ == 0))
        def _():
            for ref in (dcw_ref, dcb_ref, dwa_ref, dba_ref, dwx_ref, dbx_ref, dlam_ref):
                ref[...] = jnp.zeros_like(ref)

        xr, g = p_ref[0, 0], p_ref[1, 0]
        xc_v = xc_ref[0]
        xcb, r, i, sp, a, m = _rg_gates(xc_v, wa_ref, ba_ref, wx_ref, bx_ref, lam_ref)
        dhs, dg = _gate_bwd(dgt_ref[0], hs_ref[0], g)
        dp_ref[1, 0] = dg.astype(BF16)

        ext_ref[0:rows, :] = a
        ext_ref[rows:, :] = acar_ref[...]
        acar_ref[...] = a[0:8, :]
        c_ref[...] = ext_ref[pl.ds(1, rows), :]
        l_ref[...] = dhs

        def step(r0, carry):
            lam_g = _group_scan(c_ref[pl.ds(r0, 8), :], l_ref[pl.ds(r0, 8), :], carry, reverse=True)
            l_ref[pl.ds(r0, 8), :] = lam_g
            return lam_g[0:1, :]

        lcar_ref[0:1, :] = _scan_rows(rows, step, lcar_ref[0:1, :], reverse=True)
        du = l_ref[...]
        da = du * hp_ref[0]
        dlog_a = da * a - (du * (i * xc_v)) * (a * a / m)
        dr = dlog_a * (-RG_C * sp)
        dsp = jnp.sum(dlog_a * (-RG_C * r), axis=0, keepdims=True)
        dlam_ref[...] += dsp * (-_sigmoid(-lam_ref[...]))
        dpa = dr * (r * (1.0 - r))
        dpx = (du * (m * xc_v)) * (i * (1.0 - i))
        dba_ref[...] += jnp.sum(dpa, axis=0, keepdims=True)
        dbx_ref[...] += jnp.sum(dpx, axis=0, keepdims=True)
        dpa, dpx = dpa.astype(BF16), dpx.astype(BF16)
        dwa_ref[0] += _dot_tn(xcb, dpa)
        dwx_ref[0] += _dot_tn(xcb, dpx)
        dxc = du * (m * i) + _dot_nt(dpa, wa_ref[0]) + _dot_nt(dpx, wx_ref[0])

        dcb_ref[...] += jnp.sum(dxc, axis=0, keepdims=True)
        ext_ref[0:rows, :] = dxc
        ext_ref[rows:, :] = dcar_ref[...]
        dcar_ref[...] = dxc[0:8, :]
        dxr = jnp.zeros_like(dxc)
        for k in range(CONV_WIDTH):
            tap = CONV_WIDTH - 1 - k
            ahead = dxc if k == 0 else ext_ref[pl.ds(k, rows), :]
            dxr = dxr + ahead * cw_ref[tap:tap + 1, :]
            dcw_ref[tap:tap + 1, :] += jnp.sum(xr * ahead, axis=0, keepdims=True)
        dp_ref[0, 0] = dxr.astype(BF16)

    vec = jax.ShapeDtypeStruct((1, RG_WIDTH), F32)
    gate = jax.ShapeDtypeStruct((RG_GROUPS, RG_COLS, RG_COLS), F32)
    return _call(
        body, name=name, grid=(RG_GROUPS, B, S // rows),
        in_specs=[sp_["proj"], sp_["act"], sp_["act"], sp_["act"], sp_["act"], sp_["taps"],
                  sp_["gate"], sp_["vec"], sp_["gate"], sp_["vec"], sp_["vec"]],
        out_specs=[sp_["proj"], sp_["taps"], sp_["vec"], sp_["gate"], sp_["vec"], sp_["gate"], sp_["vec"], sp_["vec"]],
        out_shape=[jax.ShapeDtypeStruct(proj.shape, BF16), jax.ShapeDtypeStruct((CONV_WIDTH, RG_WIDTH), F32), vec,
                   gate, vec, gate, vec, vec],
        scratch_shapes=[pltpu.VMEM((rows + 8, RG_COLS), F32), pltpu.VMEM((rows, RG_COLS), F32),
                        pltpu.VMEM((rows, RG_COLS), F32), pltpu.VMEM((8, RG_COLS), F32),
                        pltpu.VMEM((8, RG_COLS), F32), pltpu.VMEM((8, RG_COLS), F32)],
        semantics=("arbitrary", "arbitrary", "arbitrary"),
        args=[proj, xc, hs, hprev, dgated, conv_w, wa, ba, wx, bx, lam], ride=ride)


def block_diag_gates(w):
    per = RG_COLS // RG_BLOCK
    w4 = w.reshape(RG_GROUPS, per, RG_BLOCK, RG_BLOCK)
    return jnp.einsum("dipq,ij->dipjq", w4, jnp.eye(per, dtype=w.dtype)).reshape(RG_GROUPS, RG_COLS, RG_COLS)


def block_diag_gates_t(dw):
    per = RG_COLS // RG_BLOCK
    dw6 = dw.reshape(RG_GROUPS, per, RG_BLOCK, per, RG_BLOCK)
    return jnp.stack([dw6[:, i, :, i, :] for i in range(per)], axis=1).reshape(RG_BLOCKS, RG_BLOCK, RG_BLOCK)


def adamw(w, parts, m, v, name, layer=0, prev=None, part_row0=0, row_tile=ROW_TILE):
    L, R, C = w.shape
    n_parts = parts.shape[0]
    br = row_tile if R % row_tile == 0 else R

    def body(w_ref, p_ref, m_ref, v_ref, *refs):
        g_ref, d_ref, nm_ref, nv_ref = refs[-4:]
        g = p_ref[0].astype(F32)
        for k in range(1, n_parts):
            g = g + p_ref[k].astype(F32)
        nm = ADAM_B1 * m_ref[0] + (1.0 - ADAM_B1) * g
        nv = ADAM_B2 * v_ref[0] + (1.0 - ADAM_B2) * (g * g)
        m_hat = nm / (1.0 - ADAM_B1 ** ADAM_STEP)
        v_hat = nv / (1.0 - ADAM_B2 ** ADAM_STEP)
        g_ref[0] = g
        d_ref[0] = -ADAM_LR * (m_hat / (jnp.sqrt(v_hat) + ADAM_EPS) + ADAM_WD * w_ref[0])
        nm_ref[0] = nm
        nv_ref[0] = nv

    slab = pl.BlockSpec((1, br, C), lambda i: (layer, i, 0))
    out = jax.ShapeDtypeStruct((L, R, C), F32)
    carried = [] if prev is None else list(prev)
    return _call(
        body, name=name, grid=(R // br,),
        in_specs=[slab, pl.BlockSpec((n_parts, br, C), lambda i: (0, part_row0 // br + i, 0)), slab, slab]
        + [pl.BlockSpec(memory_space=pl.ANY)] * len(carried),
        out_specs=[slab] * 4, out_shape=[out] * 4, semantics=("parallel",), args=[w, parts, m, v] + carried,
        aliases={4 + k: k for k in range(len(carried))})


def _seq(a, B):
    return a.reshape(a.shape[:-2] + (B, a.shape[-2] // B, a.shape[-1]))


def _flat(a):
    return a.reshape(a.shape[:-3] + (a.shape[-3] * a.shape[-2], a.shape[-1]))


def _tiles(w, which, **default):
    return dict(default, **w.get("tiles", {}).get(which, {}))


def mixer_a_fwd(h, w, B, tag, rides):
    proj = matmul(h, w["w_in"], mode="nn", out_dtype=F32, name=f"{tag}_proj", out_slabs=4,
                  ride=rides.pop(f"{tag}_proj", None), **_tiles(w, "proj"))
    o, gated = attn_a_fwd(_seq(proj, B), w["bias"], f"{tag}_attn", heads=w.get("attn_heads", (1, 1))[0],
                          ride=rides.pop(f"{tag}_attn", None))
    return _flat(gated), dict(proj=proj, o=o)


def mixer_a_bwd(dgated, w, saved, B, tag, rides):
    dproj, dbias = attn_a_bwd(_seq(saved["proj"], B), w["bias"], saved["o"], _seq(dgated, B), f"{tag}_attn_bwd",
                              heads=w.get("attn_heads", (1, 1))[1], ride=rides.pop(f"{tag}_attn_bwd", None))
    return _flat(dproj), dict(bias=dbias)


def mixer_b_fwd(h, w, B, tag, rides):
    proj = matmul(h, w["w_in"], mode="nn", out_dtype=F32, name=f"{tag}_proj", out_slabs=2, bn=RG_COLS,
                  ride=rides.pop(f"{tag}_proj", None))
    xc, hs, hprev, gated = rglru_fwd(_seq(proj, B), w["conv_w"], w["conv_b"], w["wa"], w["ba"], w["wx"], w["bx"],
                                     w["lam"], f"{tag}_rglru", ride=rides.pop(f"{tag}_rglru", None))
    return _flat(gated), dict(proj=proj, xc=xc, hs=hs, hprev=hprev)


def mixer_b_bwd(dgated, w, saved, B, tag, rides):
    dproj, dcw, dcb, dwa, dba, dwx, dbx, dlam = rglru_bwd(
        _seq(saved["proj"], B), saved["xc"], saved["hs"], saved["hprev"], _seq(dgated, B),
        w["conv_w"], w["wa"], w["ba"], w["wx"], w["bx"], w["lam"], f"{tag}_rglru_bwd",
        ride=rides.pop(f"{tag}_rglru_bwd", None))
    return _flat(dproj), dict(conv_w=dcw, conv_b=dcb, wa=dwa, ba=dba, wx=dwx, bx=dbx, lam=dlam)


def mixer_c_fwd(h, w, B, tag, rides):
    proj = matmul(h, w["w_in"], mode="nn", out_dtype=F32, name=f"{tag}_proj", out_slabs=4,
                  ride=rides.pop(f"{tag}_proj", None), **_tiles(w, "proj"))
    f_logit = matmul(h, w["w_f"], mode="nn", out_dtype=F32, name=f"{tag}_fproj")
    cum = fox_cum_fwd(_seq(f_logit, B), w["f_bias"], f"{tag}_cum")
    cum_row = cum[:, :, :HEADS].transpose(0, 2, 1)
    o, gated = fox_fwd(_seq(proj, B), cum, cum_row, f"{tag}_attn", heads=w.get("attn_heads", (1, 1))[0],
                       ride=rides.pop(f"{tag}_attn", None))
    return _flat(gated), dict(proj=proj, o=o, f_logit=f_logit, cum=cum, cum_row=cum_row)


def mixer_c_bwd(dgated, w, saved, B, tag, rides):
    dproj, dck = fox_bwd(_seq(saved["proj"], B), saved["cum"], saved["cum_row"], saved["o"], _seq(dgated, B),
                         f"{tag}_attn_bwd", heads=w.get("attn_heads", (1, 1))[1],
                         ride=rides.pop(f"{tag}_attn_bwd", None))
    S = dck.shape[-1]
    dcum = jnp.pad(-dck.reshape(B, HEADS, S).transpose(0, 2, 1), ((0, 0), (0, 0), (0, HEAD_DIM - HEADS)))
    df, dfb = fox_cum_bwd(_seq(saved["f_logit"], B), w["f_bias"], dcum, f"{tag}_cum_bwd")
    return _flat(dproj), dict(f_bias=dfb, df=_flat(df).astype(BF16))


MIXERS = {"a": (mixer_a_fwd, mixer_a_bwd), "b": (mixer_b_fwd, mixer_b_bwd), "c": (mixer_c_fwd, mixer_c_bwd)}
LAYER_KINDS = "abca"


def local_step(x, target, norm_pre, norm_post, get_layer, rides, on_grads):
    B, S, Dm = x.shape
    n_layers = len(LAYER_KINDS)
    xs = [x.reshape(B * S, Dm)]
    saved, layers = [], []
    h = prenorm_fwd(xs[0], norm_pre[0:1], "l0a_prenorm")
    for li, kind in enumerate(LAYER_KINDS):
        tag = f"l{li}{kind}"
        w = get_layer(li)
        gated, sv = MIXERS[kind][0](h, w, B, tag, rides)
        if callable(w["w_out"]):
            w["w_out"] = w["w_out"]()
        y = matmul(gated, w["w_out"], mode="nn", out_dtype=F32, name=f"{tag}_out", ride=rides.pop(f"{tag}_out", None))
        saved.append(dict(sv, h=h, gated=gated, y=y))
        layers.append(w)
        if li + 1 < n_layers:
            x_new, h = postnorm_prenorm_fwd(xs[-1], y, norm_post[li:li + 1], norm_pre[li + 1:li + 2],
                                            f"{tag}_postnorm")
            xs.append(x_new)
    loss, dx = postnorm_loss(xs[-1], y, norm_post[n_layers - 1:], target.reshape(B * S, Dm), "loss")

    for li in reversed(range(n_layers)):
        kind, w, sv = LAYER_KINDS[li], layers[li], saved[li]
        tag = f"l{li}{kind}"
        dy, dg_post = postnorm_bwd(sv["y"], norm_post[li:li + 1], dx, f"{tag}_postnorm_bwd")
        on_grads(li, "norm_post", dg_post)
        on_grads(li, "w_out", matmul(sv["gated"], dy, mode="tn", out_dtype=BF16, name=f"{tag}_dwout",
                                     ride=rides.pop(f"{tag}_dwout", None)))
        dgated = matmul(dy, w["w_out"], mode="nt", out_dtype=F32, name=f"{tag}_dgated",
                        ride=rides.pop(f"{tag}_dgated", None))
        dproj, gw = MIXERS[kind][1](dgated, w, sv, B, tag, rides)
        df = gw.pop("df", None)
        for name, value in gw.items():
            on_grads(li, name, value)
        parts = w.get("dwin_parts", [(0, 1, 1)])
        for i, m_part in enumerate(parts):
            suffix = f"_{i}" if len(parts) > 1 else ""
            on_grads(li, "w_in" + suffix,
                     matmul(sv["h"], dproj, mode="tn", out_dtype=BF16, name=f"{tag}_dwin{suffix}", m_part=m_part,
                            out_slabs=w["grad_slabs"], ride=rides.pop(f"{tag}_dwin{suffix}", None),
                            **_tiles(w, "dwin")))
        if df is not None:
            on_grads(li, "w_f", matmul(sv["h"], df, mode="tn", out_dtype=BF16, name=f"{tag}_dwf"))
        dhs = [matmul(dproj, w["w_in"], mode="nt", out_dtype=F32, name=f"{tag}_dh",
                      ride=rides.pop(f"{tag}_dh", None), **_tiles(w, "dh"))]
        if df is not None:
            dhs.append(matmul(df, w["w_f"], mode="nt", out_dtype=F32, name=f"{tag}_dhf"))
        dx, dg_pre = prenorm_bwd(xs[li], norm_pre[li:li + 1], dhs, dx, f"{tag}_prenorm_bwd")
        on_grads(li, "norm_pre", dg_pre)
    assert not rides, list(rides)
    return loss, dx.reshape(B, S, Dm)


WEIGHTS = ("norm_pre", "norm_post", "a_w_in", "a_rel_bias", "a_w_out", "b_w_in", "b_conv_w", "b_conv_b",
           "b_gate_a_w", "b_gate_a_b", "b_gate_x_w", "b_gate_x_b", "b_lambda", "b_w_out", "c_w_in", "c_f_bias",
           "c_w_out")
C_SHARD = (4 * D_MODEL + HEADS) // N_DEV


def _rows(gathered):
    return gathered.reshape(gathered.shape[0] * gathered.shape[1], gathered.shape[2])


def layer_a(w_in, w_out, rel_bias):
    return dict(w_in=w_in, w_out=w_out if callable(w_out) else _rows(w_out), bias=band_bias(rel_bias),
                grad_slabs=N_DEV)


def layer_b(w_in, w_out, conv_w, small):
    return dict(
        w_in=w_in, w_out=_rows(w_out), grad_slabs=N_DEV,
        conv_w=conv_w.transpose(1, 0, 2).reshape(CONV_WIDTH, RG_WIDTH),
        conv_b=small["b_conv_b"], lam=small["b_lambda"],
        wa=block_diag_gates(small["b_gate_a_w"][0]).astype(BF16), ba=small["b_gate_a_b"].reshape(1, RG_WIDTH),
        wx=block_diag_gates(small["b_gate_x_w"][0]).astype(BF16), bx=small["b_gate_x_b"].reshape(1, RG_WIDTH))


def layer_c(w_in, w_out, small):
    full = w_in.transpose(1, 0, 2).reshape(D_MODEL, N_DEV * C_SHARD)
    return dict(w_in=full[:, :4 * D_MODEL], w_f=jnp.pad(full[:, 4 * D_MODEL:], ((0, 0), (0, HEAD_DIM - HEADS))),
                w_out=_rows(w_out), grad_slabs=1,
                f_bias=jnp.pad(small["c_f_bias"], ((0, 0), (0, HEAD_DIM - HEADS))))


def c_w_in_blocks(dmain, df):
    full = jnp.concatenate([dmain, df[:, :HEADS].astype(dmain.dtype)], axis=1)
    return full.reshape(D_MODEL, N_DEV, C_SHARD).transpose(1, 0, 2)


def _row_blocks(g):
    return g.reshape(N_DEV, g.shape[0] // N_DEV, g.shape[1])


PACK_LANES = 128
PACK_ALIGN = 8 * PACK_LANES


def pack(parts):
    flat = []
    for p in parts:
        n = p.size
        flat.append(jnp.pad(p.reshape(n), (0, -n % PACK_ALIGN)).reshape(-1, PACK_LANES))
    rows = sum(f.shape[0] for f in flat)
    flat.append(jnp.zeros((-rows % ROW_TILE, PACK_LANES), F32))
    return jnp.concatenate(flat, axis=0)


def unpack(packed, shapes):
    out, row = [], 0
    for shape in shapes:
        n = 1
        for s in shape:
            n *= s
        n_rows = (n + PACK_ALIGN - 1) // PACK_ALIGN * 8
        out.append(packed[row:row + n_rows].reshape(-1)[:n].reshape(shape))
        row += n_rows
    return out


LATE = (("a_rel_bias", slice(0, 1)), ("norm_pre", slice(0, 2)), ("norm_post", slice(0, 1)))
EARLY = (("a_rel_bias", slice(1, 2)), ("norm_pre", slice(2, 4)), ("norm_post", slice(1, 4)),
         ("b_conv_b", slice(None)), ("b_gate_a_w", slice(None)), ("b_gate_a_b", slice(None)),
         ("b_gate_x_w", slice(None)), ("b_gate_x_b", slice(None)), ("b_lambda", slice(None)),
         ("c_f_bias", slice(None)))


def _pieces(tree, pieces):
    return [tree[name][sl] for name, sl in pieces]


def kernel(x, norm_pre, norm_post, a_w_in, a_rel_bias, a_w_out, b_w_in, b_conv_w, b_conv_b, b_gate_a_w, b_gate_a_b, b_gate_x_w, b_gate_x_b, b_lambda, b_w_out, c_w_in, c_f_bias, c_w_out, loss_target, m_norm_pre, m_norm_post, m_a_w_in, m_a_rel_bias, m_a_w_out, m_b_w_in, m_b_conv_w, m_b_conv_b, m_b_gate_a_w, m_b_gate_a_b, m_b_gate_x_w, m_b_gate_x_b, m_b_lambda, m_b_w_out, m_c_w_in, m_c_f_bias, m_c_w_out, v_norm_pre, v_norm_post, v_a_w_in, v_a_rel_bias, v_a_w_out, v_b_w_in, v_b_conv_w, v_b_conv_b, v_b_gate_a_w, v_b_gate_a_b, v_b_gate_x_w, v_b_gate_x_b, v_b_lambda, v_b_w_out, v_c_w_in, v_c_f_bias, v_c_w_out):
    args = dict(locals())
    w = {n: args[n] for n in WEIGHTS}
    m = {n: args["m_" + n] for n in WEIGHTS}
    v = {n: args["v_" + n] for n in WEIGHTS}

    a_in, a_out = a_w_in.astype(BF16), a_w_out.astype(BF16)
    gather_a0 = Ride([a_in[0]], scatter=False, via_sibling=True)
    in_l0_proj = Ride([b_w_in[0].astype(BF16), b_conv_w[0], a_out[0]], scatter=False, via_sibling=True)
    in_l0_attn = Ride([c_w_in[0].astype(BF16), b_w_out[0].astype(BF16)], scatter=False, via_sibling=True)
    in_l1_proj = Ride([c_w_out[0].astype(BF16)], scatter=False)
    in_l2_proj = Ride([a_out[1]], scatter=False)
    in_l2_attn = Ride([a_in[1]], scatter=False, via_sibling=True)
    exchange(gather_a0, "gather_l0")
    rides = {"l0a_proj": in_l0_proj, "l0a_attn": in_l0_attn, "l1b_proj": in_l1_proj, "l2c_proj": in_l2_proj,
             "l2c_attn": in_l2_attn}

    def get_layer(li):
        if li == 0:
            return dict(layer_a(gather_a0.out[0], lambda: _rows(in_l0_proj.out[2]), a_rel_bias[0]),
                        dwin_parts=[(0, 1, 4), (1, 1, 4), (2, 2, 4)], attn_heads=(4, 2))
        if li == 1:
            return layer_b(in_l0_proj.out[0], in_l0_attn.out[1], in_l0_proj.out[1], w)
        if li == 2:
            return dict(layer_c(in_l0_attn.out[0], in_l1_proj.out[0], w), tiles=dict(proj=dict(bn=2048)),
                        attn_heads=(4, 2))
        return dict(layer_a(in_l2_attn.out[0], in_l2_proj.out[0], a_rel_bias[1]), attn_heads=(4, 2))

    grads = [dict() for _ in LAYER_KINDS]
    scatters = {}

    def rel_bias_grad(j, dbias):
        return jax.vjp(band_bias, a_rel_bias[j])[1](dbias)[0][None]

    def early_partial():
        gb, gc = grads[1], grads[2]
        tree = dict(
            a_rel_bias=jnp.concatenate([jnp.zeros((1, HEADS, N_REL), F32), rel_bias_grad(1, grads[3]["bias"])]),
            norm_pre=jnp.concatenate([jnp.zeros((2, D_MODEL), F32)] + [grads[li]["norm_pre"] for li in (2, 3)]),
            norm_post=jnp.concatenate([jnp.zeros((1, D_MODEL), F32)] + [grads[li]["norm_post"] for li in (1, 2, 3)]),
            b_conv_b=gb["conv_b"], b_lambda=gb["lam"],
            b_gate_a_w=block_diag_gates_t(gb["wa"])[None], b_gate_a_b=gb["ba"].reshape(1, RG_BLOCKS, RG_BLOCK),
            b_gate_x_w=block_diag_gates_t(gb["wx"])[None], b_gate_x_b=gb["bx"].reshape(1, RG_BLOCKS, RG_BLOCK),
            c_f_bias=gc["f_bias"][:, :HEADS])
        return pack(_pieces(tree, EARLY))

    def send(key, host, blocks, scatter=True, via_sibling=False):
        ride = rides.setdefault(host, Ride([], scatter, via_sibling))
        assert (ride.scatter, ride.via_sibling) == (scatter, via_sibling)
        scatters[key] = (ride, len(ride.arrs))
        ride.arrs.append(blocks)

    def on_grads(li, name, value):
        g = grads[li]
        g[name] = value
        if (li, name) == (3, "w_out"):
            send("a1_out", "l3a_attn_bwd", _row_blocks(value))
        elif (li, name) == (3, "w_in"):
            send("a1_in", "l2c_attn_bwd", value)
        elif (li, name) == (2, "w_out"):
            send("c_out", "l2c_attn_bwd", _row_blocks(value))
        elif (li, name) == (2, "w_f"):
            blocks = c_w_in_blocks(g["w_in"], value)
            send("c_in_0", "l2c_dh", blocks[:, :D_MODEL // 2])
            send("c_in_1", "l1b_rglru_bwd", blocks[:, D_MODEL // 2:])
        elif (li, name) == (1, "w_out"):
            send("b_out", "l1b_dh", _row_blocks(value))
        elif (li, name) == (1, "w_in"):
            send("b_in", "l0a_attn_bwd", value)
            send("b_conv", "l0a_attn_bwd",
                 g["conv_w"].reshape(CONV_WIDTH, N_DEV, RG_WIDTH // N_DEV).transpose(1, 0, 2))
        elif (li, name) == (1, "lam"):
            send("early", "l1b_dwin", early_partial(), scatter=False, via_sibling=True)
        elif (li, name) == (0, "w_out"):
            send("a0_out", "l0a_attn_bwd", _row_blocks(value))
        elif (li, name) == (0, "w_in_0"):
            send("a0_in_0", "l0a_dwin_1", value)
        elif (li, name) == (0, "w_in_1"):
            send("a0_in_1", "l0a_dwin_2", value)
        elif (li, name) == (0, "w_in_2"):
            send("a0_in_2", "l0a_dh", value)

    loss, grad_x = local_step(x, loss_target, norm_pre, norm_post, get_layer, rides, on_grads)
    late_tree = dict(a_rel_bias=rel_bias_grad(0, grads[0]["bias"]), norm_post=grads[0]["norm_post"],
                     norm_pre=jnp.concatenate([grads[0]["norm_pre"], grads[1]["norm_pre"]]))
    late_parts = exchange(Ride([pack([late_tree[n] for n, _ in LATE])], scatter=False), "gather_late_grads")[0]

    def sharded(name, slab_parts):
        shape = w[name].shape
        slabs = (len(slab_parts), shape[0] * shape[1] // len(slab_parts), shape[2])
        outs = None
        for j, (parts, row0) in enumerate(slab_parts):
            outs = adamw(w[name].reshape(slabs), parts, m[name].reshape(slabs), v[name].reshape(slabs),
                         f"adamw_{name}_{j}", layer=j, prev=outs, part_row0=row0)
        return [o.reshape(shape) for o in outs]

    def received(key):
        ride, position = scatters[key]
        return ride.out[position]

    res = dict(
        a_w_in=sharded("a_w_in", [(received("a0_in_0"), 0), (received("a0_in_1"), 0), (received("a0_in_2"), 0),
                                  (received("a0_in_2"), D_MODEL // 4)]
                       + [(received("a1_in"), q * D_MODEL // 4) for q in range(4)]),
        a_w_out=sharded("a_w_out", [(received("a0_out"), 0), (received("a1_out"), 0)]),
        b_w_in=sharded("b_w_in", [(received("b_in"), 0)]),
        b_w_out=sharded("b_w_out", [(received("b_out"), 0)]),
        b_conv_w=sharded("b_conv_w", [(received("b_conv"), 0)]),
        c_w_in=sharded("c_w_in", [(received("c_in_0"), 0), (received("c_in_1"), 0)]),
        c_w_out=sharded("c_w_out", [(received("c_out"), 0)]))

    packed = {}
    for label, pieces, parts in (("early", EARLY, received("early")), ("late", LATE, late_parts)):
        outs = adamw(pack(_pieces(w, pieces))[None], parts, pack(_pieces(m, pieces))[None],
                     pack(_pieces(v, pieces))[None], f"adamw_replicated_{label}")
        shapes = [w[n][sl].shape for n, sl in pieces]
        packed[label] = [dict(zip([n for n, _ in pieces], unpack(o[0], shapes))) for o in outs]
    for n in ("b_conv_b", "b_gate_a_w", "b_gate_a_b", "b_gate_x_w", "b_gate_x_b", "b_lambda", "c_f_bias"):
        res[n] = [packed["early"][k][n] for k in range(4)]
    for n in ("a_rel_bias", "norm_pre", "norm_post"):
        res[n] = [jnp.concatenate([packed["late"][k][n], packed["early"][k][n]]) for k in range(4)]

    total = lax.psum(loss[0, 0], ("x", "y", "c"))
    return (total, grad_x, *[res[n][0] for n in WEIGHTS], *[res[n][1] for n in WEIGHTS],
            *[res[n][2] for n in WEIGHTS], *[res[n][3] for n in WEIGHTS])
```

```python
import functools

import jax
import jax.numpy as jnp
from jax import lax
from jax.experimental import pallas as pl
from jax.experimental.pallas import tpu as pltpu

F32 = jnp.float32
BF16 = jnp.bfloat16

N_DEV = 8
D_MODEL = 2048
HEADS = 16
HEAD_DIM = 128
CHUNK = 64
LEFT_CHUNKS = 8
REL_CLIP = 256
N_REL = 2 * REL_CLIP + 1
TQ = 256
A_PAD = LEFT_CHUNKS * CHUNK
A_KW = A_PAD + TQ
RG_WIDTH = 2560
RG_BLOCKS = 16
RG_BLOCK = 160
RG_COLS = 640
RG_GROUPS = RG_WIDTH // RG_COLS
RG_C = 8.0
CONV_WIDTH = 4
RMS_EPS = 1e-6
NEG_INF = -1e30
ADAM_LR = 0.001
ADAM_B1 = 0.9
ADAM_B2 = 0.999
ADAM_EPS = 1e-08
ADAM_WD = 0.01
ADAM_STEP = 10
VMEM_LIMIT = 56 * 1024 * 1024
MESH = pl.DeviceIdType.MESH


def _params(sem, vmem=VMEM_LIMIT):
    return pltpu.CompilerParams(dimension_semantics=sem, vmem_limit_bytes=vmem)


def _sigmoid(x):
    return 1.0 / (1.0 + jnp.exp(-x))


def _log1p(y):
    u = 1.0 + y
    return jnp.where(u == 1.0, y, jnp.log(u) * (y / jnp.where(u == 1.0, 1.0, u - 1.0)))


def _softplus(x):
    return jnp.maximum(x, 0.0) + _log1p(jnp.exp(-jnp.abs(x)))


def _dot(a, b, dims):
    return lax.dot_general(a, b, (dims, ((), ())), preferred_element_type=F32)


def _dot_nn(a, b):
    return _dot(a, b, ((1,), (0,)))


def _dot_nt(a, b):
    return _dot(a, b, ((1,), (1,)))


def _dot_tn(a, b):
    return _dot(a, b, ((0,), (0,)))


def _peers():
    x, y, c = lax.axis_index("x"), lax.axis_index("y"), lax.axis_index("c")
    me = 4 * x + 2 * y + c
    peers = []
    for k in range(1, N_DEV):
        px = 1 - x if k & 4 else x
        py = 1 - y if k & 2 else y
        pc = 1 - c if k & 1 else c
        peers.append(((px, py, pc), 4 * px + 2 * py + pc))
    return me, peers


class Ride:
    def __init__(self, arrs, scatter, via_sibling=False):
        assert not (scatter and via_sibling)
        self.arrs, self.scatter, self.via_sibling, self.out = list(arrs), scatter, via_sibling, None

    def out_shapes(self):
        return [jax.ShapeDtypeStruct(a.shape if self.scatter else (N_DEV,) + a.shape, a.dtype) for a in self.arrs]

    def sem_shapes(self):
        n = len(self.arrs)
        return [pltpu.SemaphoreType.DMA((n, N_DEV - 1)), pltpu.SemaphoreType.DMA((n, N_DEV - 1)),
                pltpu.SemaphoreType.DMA((n,))]

    def _copies(self, ins, outs, sems, landing):
        send_sems, recv_sems, local_sems = sems
        me, peers = _peers()
        local, remote = [], []
        for a, (src, dst) in enumerate(zip(ins, outs)):
            local.append(pltpu.make_async_copy(src.at[me] if self.scatter else src, dst.at[me], local_sems.at[a]))
            for k, (peer, peer_idx) in enumerate(peers):
                remote.append(pltpu.make_async_remote_copy(
                    src_ref=src.at[peer_idx] if self.scatter else src, dst_ref=dst.at[peer_idx if landing else me],
                    send_sem=send_sems.at[a, k], recv_sem=recv_sems.at[a, k], device_id=peer, device_id_type=MESH))
        return local, remote

    def _direct(self, k):
        return not self.via_sibling or k == 0 or (k + 1) % 2 == 0

    def start(self, ins, outs, sems):
        local, remote = self._copies(ins, outs, sems, landing=False)
        n_peers = N_DEV - 1
        for cp in local + [cp for i, cp in enumerate(remote) if self._direct(i % n_peers)]:
            cp.start()

    def wait(self, ins, outs, sems):
        local, remote = self._copies(ins, outs, sems, landing=True)
        n_peers = N_DEV - 1
        for i, cp in enumerate(remote):
            if self._direct(i % n_peers):
                cp.wait()
        if self.via_sibling:
            send_sems, recv_sems, _ = sems
            me, peers = _peers()
            sibling = peers[0][0]
            passed = []
            for a, dst in enumerate(outs):
                for j in range(1, n_peers, 2):
                    came, lands = peers[j][1], peers[j + 1][1]
                    pltpu.make_async_remote_copy(
                        src_ref=dst.at[came], dst_ref=dst.at[came], send_sem=send_sems.at[a, j + 1],
                        recv_sem=recv_sems.at[a, j + 1], device_id=sibling, device_id_type=MESH).start()
                    passed.append(pltpu.make_async_remote_copy(
                        src_ref=dst.at[came], dst_ref=dst.at[lands], send_sem=send_sems.at[a, j + 1],
                        recv_sem=recv_sems.at[a, j + 1], device_id=sibling, device_id_type=MESH))
            for cp in passed:
                cp.wait()
        for cp in local:
            cp.wait()


def _call(body, *, name, grid, in_specs, out_specs, out_shape, args, scratch_shapes=(), semantics=None, ride=None,
          aliases=None):
    scratch_shapes = list(scratch_shapes)
    if ride is None:
        return pl.pallas_call(
            body, name=name, grid=grid, in_specs=in_specs, out_specs=out_specs, out_shape=out_shape,
            scratch_shapes=scratch_shapes, input_output_aliases=aliases or {},
            compiler_params=_params(semantics if grid else None))(*args)
    assert not aliases
    n_in, n_out, n_sc, n_r = len(in_specs), len(out_specs), len(scratch_shapes), len(ride.arrs)

    def riding(*refs):
        ins, r_ins = refs[:n_in], refs[n_in:n_in + n_r]
        outs, r_outs = refs[n_in + n_r:n_in + n_r + n_out], refs[n_in + n_r + n_out:n_in + 2 * n_r + n_out]
        rest = refs[n_in + 2 * n_r + n_out:]
        scratch, sems = rest[:n_sc], rest[n_sc:]
        first = last = None
        for axis, size in enumerate(grid):
            pid = pl.program_id(axis)
            first = (pid == 0) if first is None else first & (pid == 0)
            last = (pid == size - 1) if last is None else last & (pid == size - 1)
        if grid:
            pl.when(first)(lambda: ride.start(r_ins, r_outs, sems))
        else:
            ride.start(r_ins, r_outs, sems)
        body(*ins, *outs, *scratch)
        if grid:
            pl.when(last)(lambda: ride.wait(r_ins, r_outs, sems))
        else:
            ride.wait(r_ins, r_outs, sems)

    any_spec = pl.BlockSpec(memory_space=pl.ANY)
    res = pl.pallas_call(
        riding, name=name, grid=grid, in_specs=list(in_specs) + [any_spec] * n_r,
        out_specs=list(out_specs) + [any_spec] * n_r, out_shape=list(out_shape) + ride.out_shapes(),
        scratch_shapes=scratch_shapes + ride.sem_shapes(),
        compiler_params=_params(("arbitrary",) * len(grid) if grid else None))(*args, *ride.arrs)
    ride.out = list(res[n_out:])
    return list(res[:n_out])


def exchange(ride, name):
    _call(lambda: None, name=name, grid=(), in_specs=[], out_specs=[], out_shape=[], args=[], ride=ride)
    return ride.out


LANES = 128


def _fit(dims, want):
    dims = tuple(dims)
    if len(set(dims)) == 1 and dims[0] <= want:
        return dims[0]
    return max(t for t in range(LANES, want + 1, LANES) if all(d % t == 0 for d in dims))


def _cols(arr):
    return arr.shape[-1] * (arr.shape[0] if len(arr.shape) == 3 else 1)


def _tile_spec(shape, rblk, cblk, rc):
    if len(shape) == 2:
        return pl.BlockSpec((rblk, cblk), rc)
    per = shape[2] // cblk

    def index_map(*ids):
        r, c = rc(*ids)
        return (c // per, r, c % per)

    return pl.BlockSpec((1, rblk, cblk), index_map)


def matmul(a, b, *, mode, out_dtype, name, bm=1024, bn=1024, bk=2048, out_slabs=1, m_part=(0, 1, 1), ride=None):
    a_rows, a_cols, b_rows, b_cols = a.shape[-2], _cols(a), b.shape[-2], _cols(b)
    (K, M) = (a_rows, a_cols) if mode == "tn" else (a_cols, a_rows)
    N = b_rows if mode == "nt" else b_cols
    assert K == (b_cols if mode == "nt" else b_rows), (name, a.shape, b.shape)
    first_range, n_ranges, of_ranges = m_part
    row0, M = first_range * (M // of_ranges), n_ranges * (M // of_ranges)
    out_shape = (M, N) if out_slabs == 1 else (out_slabs, M, N // out_slabs)
    widths = dict(m=[M], n=[N, out_shape[-1]], k=[K])
    widths["m" if mode == "tn" else "k"].append(a.shape[-1])
    widths["k" if mode == "nt" else "n"].append(b.shape[-1])
    bm, bn, bk = _fit(widths["m"], bm), _fit(widths["n"], bn), _fit(widths["k"], bk)
    nk = K // bk
    dims = {"nn": ((1,), (0,)), "nt": ((1,), (1,)), "tn": ((0,), (0,))}[mode]

    def val(ref):
        return ref[0] if len(ref.shape) == 3 else ref[...]

    def put(ref, x):
        if len(ref.shape) == 3:
            ref[0] = x.astype(ref.dtype)
        else:
            ref[...] = x.astype(ref.dtype)

    def body(a_ref, b_ref, o_ref, *scratch):
        if nk == 1:
            put(o_ref, _dot(val(a_ref), val(b_ref), dims))
            return
        acc_ref, = scratch
        k = pl.program_id(2)

        @pl.when(k == 0)
        def _():
            acc_ref[...] = jnp.zeros_like(acc_ref)

        acc_ref[...] += _dot(val(a_ref), val(b_ref), dims)

        @pl.when(k == nk - 1)
        def _():
            put(o_ref, acc_ref[...])

    assert row0 % bm == 0
    m0 = row0 // bm
    if mode == "tn":
        a_spec = _tile_spec(a.shape, bk, bm, lambda j, i, k: (k, m0 + i))
    else:
        a_spec = _tile_spec(a.shape, bm, bk, lambda j, i, k: (m0 + i, k))
    if mode == "nt":
        b_spec = _tile_spec(b.shape, bn, bk, lambda j, i, k: (j, k))
    else:
        b_spec = _tile_spec(b.shape, bk, bn, lambda j, i, k: (k, j))
    return _call(
        body, name=name, grid=(N // bn, M // bm, nk), in_specs=[a_spec, b_spec],
        out_specs=[_tile_spec(out_shape, bm, bn, lambda j, i, k: (i, j))],
        out_shape=[jax.ShapeDtypeStruct(out_shape, out_dtype)],
        scratch_shapes=[] if nk == 1 else [pltpu.VMEM((bm, bn), F32)],
        semantics=("parallel", "parallel", "arbitrary"), args=[a, b], ride=ride)[0]


ROW_TILE = 256


def _rms_stats(z):
    r = lax.rsqrt(jnp.mean(z * z, axis=-1, keepdims=True) + RMS_EPS)
    return r, z * r


def _rms_bwd(n, r, g, dout):
    dn = dout * g
    return r * (dn - n * jnp.mean(dn * n, axis=-1, keepdims=True))


def _row_spec(T, Dm):
    bt = min(ROW_TILE, T)
    return bt, pl.BlockSpec((bt, Dm), lambda i: (i, 0)), pl.BlockSpec((1, Dm), lambda i: (0, 0))


def prenorm_fwd(x, g, name):
    T, Dm = x.shape
    bt, row, vec = _row_spec(T, Dm)

    def body(x_ref, g_ref, h_ref):
        _, n = _rms_stats(x_ref[...])
        h_ref[...] = (n * g_ref[...]).astype(BF16)

    return pl.pallas_call(
        body, name=name, grid=(T // bt,), in_specs=[row, vec], out_specs=row,
        out_shape=jax.ShapeDtypeStruct((T, Dm), BF16), compiler_params=_params(("parallel",)),
    )(x, g)


def postnorm_prenorm_fwd(x, y, g_post, g_next, name):
    T, Dm = x.shape
    bt, row, vec = _row_spec(T, Dm)

    def body(x_ref, y_ref, gp_ref, gn_ref, o_ref, h_ref):
        _, n = _rms_stats(y_ref[...])
        x_new = x_ref[...] + n * gp_ref[...]
        o_ref[...] = x_new
        _, n_new = _rms_stats(x_new)
        h_ref[...] = (n_new * gn_ref[...]).astype(BF16)

    return pl.pallas_call(
        body, name=name, grid=(T // bt,), in_specs=[row, row, vec, vec], out_specs=[row, row],
        out_shape=[jax.ShapeDtypeStruct((T, Dm), F32), jax.ShapeDtypeStruct((T, Dm), BF16)],
        compiler_params=_params(("parallel",)),
    )(x, y, g_post, g_next)


def postnorm_loss(x, y, g, target, name):
    T, Dm = x.shape
    bt, row, vec = _row_spec(T, Dm)

    def body(x_ref, y_ref, g_ref, t_ref, l_ref, d_ref):
        @pl.when(pl.program_id(0) == 0)
        def _():
            l_ref[...] = jnp.zeros_like(l_ref)

        _, n = _rms_stats(y_ref[...])
        err = (x_ref[...] + n * g_ref[...]) - t_ref[...]
        per_tok = jnp.mean(err * err, axis=-1, keepdims=True)
        l_ref[...] += 0.5 * jnp.sum(per_tok, axis=0, keepdims=True)
        d_ref[...] = err * (1.0 / Dm)

    return pl.pallas_call(
        body, name=name, grid=(T // bt,), in_specs=[row, row, vec, row],
        out_specs=[pl.BlockSpec((1, 1), lambda i: (0, 0)), row],
        out_shape=[jax.ShapeDtypeStruct((1, 1), F32), jax.ShapeDtypeStruct((T, Dm), F32)],
        compiler_params=_params(("arbitrary",)),
    )(x, y, g, target)


def postnorm_bwd(y, g, dout, name):
    T, Dm = y.shape
    bt, row, vec = _row_spec(T, Dm)

    def body(y_ref, g_ref, d_ref, dy_ref, dg_ref):
        @pl.when(pl.program_id(0) == 0)
        def _():
            dg_ref[...] = jnp.zeros_like(dg_ref)

        r, n = _rms_stats(y_ref[...])
        dout_v = d_ref[...]
        dg_ref[...] += jnp.sum(dout_v * n, axis=0, keepdims=True)
        dy_ref[...] = _rms_bwd(n, r, g_ref[...], dout_v).astype(BF16)

    return pl.pallas_call(
        body, name=name, grid=(T // bt,), in_specs=[row, vec, row], out_specs=[row, vec],
        out_shape=[jax.ShapeDtypeStruct((T, Dm), BF16), jax.ShapeDtypeStruct((1, Dm), F32)],
        compiler_params=_params(("arbitrary",)),
    )(y, g, dout)


def prenorm_bwd(x, g, dhs, dres, name):
    T, Dm = x.shape
    bt, row, vec = _row_spec(T, Dm)
    n_dh = len(dhs)

    def body(x_ref, g_ref, *refs):
        dh_refs, (dr_ref, dx_ref, dg_ref) = refs[:n_dh], refs[n_dh:]

        @pl.when(pl.program_id(0) == 0)
        def _():
            dg_ref[...] = jnp.zeros_like(dg_ref)

        r, n = _rms_stats(x_ref[...])
        dh_v = dh_refs[0][...]
        for extra in dh_refs[1:]:
            dh_v = dh_v + extra[...]
        dg_ref[...] += jnp.sum(dh_v * n, axis=0, keepdims=True)
        dx_ref[...] = dr_ref[...] + _rms_bwd(n, r, g_ref[...], dh_v)

    return pl.pallas_call(
        body, name=name, grid=(T // bt,), in_specs=[row, vec] + [row] * (n_dh + 1), out_specs=[row, vec],
        out_shape=[jax.ShapeDtypeStruct((T, Dm), F32), jax.ShapeDtypeStruct((1, Dm), F32)],
        compiler_params=_params(("arbitrary",)),
    )(x, g, *dhs, dres)


def _silu(g):
    return g * _sigmoid(g)


def _gate_bwd(dgated, core, g):
    sg = _sigmoid(g)
    return dgated * (g * sg), dgated * core * (sg * (1.0 + g * (1.0 - sg)))


def _softmax_rows(s):
    e = jnp.exp(s - jnp.max(s, axis=-1, keepdims=True))
    return e * (1.0 / jnp.sum(e, axis=-1, keepdims=True))


def _band_scores(qk, bias, r0):
    s = qk * (HEAD_DIM ** -0.5) + bias
    j = lax.broadcasted_iota(jnp.int32, s.shape, 1)
    return jnp.where(j >= A_PAD - r0, s, NEG_INF)


def _fill_padded_kv(p_ref, kp_ref, vp_ref):
    zeros = jnp.zeros((A_PAD, kp_ref.shape[1]), BF16)
    kp_ref[0:A_PAD, :] = zeros
    vp_ref[0:A_PAD, :] = zeros
    kp_ref[A_PAD:, :] = p_ref[1, 0].astype(BF16)
    vp_ref[A_PAD:, :] = p_ref[2, 0].astype(BF16)


def _head_specs(S, order, heads=1):
    def idx(fn):
        return lambda *ids: fn(**dict(zip(order, ids)))

    return (pl.BlockSpec((4, 1, S, heads * HEAD_DIM), idx(lambda b, h, t: (0, b, 0, h))),
            pl.BlockSpec((1, TQ, heads * HEAD_DIM), idx(lambda b, h, t: (b, t, h))))


def attn_a_fwd(proj, bias, name, heads=1, ride=None):
    _, B, S, W = proj.shape
    nt = S // TQ

    def body(p_ref, b_ref, o_ref, gt_ref, kp_ref, vp_ref):
        t = pl.program_id(2)

        @pl.when(t == 0)
        def _():
            _fill_padded_kv(p_ref, kp_ref, vp_ref)

        r0 = pl.multiple_of(t * TQ, TQ)
        for e in range(heads):
            lanes = slice(e * HEAD_DIM, (e + 1) * HEAD_DIM)
            q = p_ref[0, 0, pl.ds(r0, TQ), lanes].astype(BF16)
            g = p_ref[3, 0, pl.ds(r0, TQ), lanes]
            p = _softmax_rows(_band_scores(_dot_nt(q, kp_ref[pl.ds(r0, A_KW), lanes]), b_ref[e], r0))
            o = _dot_nn(p.astype(BF16), vp_ref[pl.ds(r0, A_KW), lanes])
            o_ref[0, :, lanes] = o
            gt_ref[0, :, lanes] = (o * _silu(g)).astype(BF16)

    seq, tile = _head_specs(S, "bht", heads)
    kv = pltpu.VMEM((A_PAD + S, heads * HEAD_DIM), BF16)
    return _call(
        body, name=name, grid=(B, HEADS // heads, nt),
        in_specs=[seq, pl.BlockSpec((heads, TQ, A_KW), lambda b, h, t: (h, 0, 0))], out_specs=[tile, tile],
        out_shape=[jax.ShapeDtypeStruct((B, S, W), F32), jax.ShapeDtypeStruct((B, S, W), BF16)],
        scratch_shapes=[kv, kv],
        semantics=("parallel", "parallel", "arbitrary"), args=[proj, bias], ride=ride)


def attn_a_bwd(proj, bias, o, dgated, name, heads=1, ride=None):
    _, B, S, W = proj.shape
    nt = S // TQ

    def body(p_ref, b_ref, o_ref, dgt_ref, dp_ref, db_ref, kp_ref, vp_ref, dk_ref, dv_ref):
        b_, t = pl.program_id(1), pl.program_id(2)

        @pl.when(t == 0)
        def _():
            _fill_padded_kv(p_ref, kp_ref, vp_ref)
            dk_ref[...] = jnp.zeros_like(dk_ref)
            dv_ref[...] = jnp.zeros_like(dv_ref)

        @pl.when((t == 0) & (b_ == 0))
        def _():
            db_ref[...] = jnp.zeros_like(db_ref)

        r0 = pl.multiple_of(t * TQ, TQ)
        rows, win = pl.ds(r0, TQ), pl.ds(r0, A_KW)
        for e in range(heads):
            lanes = slice(e * HEAD_DIM, (e + 1) * HEAD_DIM)
            q = p_ref[0, 0, rows, lanes].astype(BF16)
            g = p_ref[3, 0, rows, lanes]
            kw, vw = kp_ref[win, lanes], vp_ref[win, lanes]
            p = _softmax_rows(_band_scores(_dot_nt(q, kw), b_ref[e], r0))
            do, dg = _gate_bwd(dgt_ref[0, :, lanes], o_ref[0, :, lanes], g)
            do = do.astype(BF16)
            dv_ref[win, lanes] += _dot_tn(p.astype(BF16), do)
            dpr = _dot_nt(do, vw)
            ds = p * (dpr - jnp.sum(p * dpr, axis=-1, keepdims=True))
            db_ref[e] += ds
            ds = (ds * (HEAD_DIM ** -0.5)).astype(BF16)
            dk_ref[win, lanes] += _dot_tn(ds, q)
            dp_ref[0, 0, rows, lanes] = _dot_nn(ds, kw).astype(BF16)
            dp_ref[3, 0, rows, lanes] = dg.astype(BF16)

        @pl.when(t == nt - 1)
        def _():
            dp_ref[1, 0] = dk_ref[A_PAD:, :].astype(BF16)
            dp_ref[2, 0] = dv_ref[A_PAD:, :].astype(BF16)

    seq, tile = _head_specs(S, "hbt", heads)
    bias_spec = pl.BlockSpec((heads, TQ, A_KW), lambda h, b, t: (h, 0, 0))
    kv = pltpu.VMEM((A_PAD + S, heads * HEAD_DIM), BF16)
    acc = pltpu.VMEM((A_PAD + S, heads * HEAD_DIM), F32)
    return _call(
        body, name=name, grid=(HEADS // heads, B, nt), in_specs=[seq, bias_spec, tile, tile],
        out_specs=[seq, bias_spec],
        out_shape=[jax.ShapeDtypeStruct(proj.shape, BF16), jax.ShapeDtypeStruct(bias.shape, F32)],
        scratch_shapes=[kv, kv, acc, acc],
        semantics=("arbitrary", "arbitrary", "arbitrary"), args=[proj, bias, o, dgated], ride=ride)


def band_bias(rel_bias):
    length = TQ + A_KW - 1
    first = REL_CLIP + 1 - TQ
    gen = jnp.concatenate([rel_bias[:, first:],
                           jnp.broadcast_to(rel_bias[:, 2 * REL_CLIP:], (HEADS, length - (N_REL - first)))], axis=1)
    rev = jnp.concatenate([gen[:, ::-1], jnp.zeros((HEADS, 1), rel_bias.dtype)], axis=1)
    sheared = jnp.tile(rev, (1, TQ))[:, :TQ * length].reshape(HEADS, TQ, length)
    i = lax.broadcasted_iota(jnp.int32, (TQ, A_KW), 0)
    j = lax.broadcasted_iota(jnp.int32, (TQ, A_KW), 1)
    first_key = (i // CHUNK) * CHUNK
    in_band = (j >= first_key) & (j < first_key + (LEFT_CHUNKS + 1) * CHUNK)
    return jnp.where(in_band, sheared[:, :, TQ - 1:], NEG_INF)


def _group_scan(a, u, carry, reverse=False):
    row = lax.broadcasted_iota(jnp.int32, u.shape, 0)
    for k in (1, 2, 4):
        shift = 8 - k if reverse else k
        valid = (row < 8 - k) if reverse else (row >= k)
        u_sh = pltpu.roll(u, shift, 0)
        if a is None:
            u = jnp.where(valid, u + u_sh, u)
        else:
            a_sh = pltpu.roll(a, shift, 0)
            u = jnp.where(valid, a * u_sh + u, u)
            a = jnp.where(valid, a * a_sh, a)
    return (u + carry) if a is None else (a * carry + u)


SCAN_UNROLL = 4


def _scan_rows(n_rows, step, carry0, reverse=False):
    groups = n_rows // 8

    def loop(i, carry):
        gi = (groups - 1 - i) if reverse else i
        return step(pl.multiple_of(gi * 8, 8), carry)

    return lax.fori_loop(0, groups, loop, carry0, unroll=SCAN_UNROLL)


def fox_cum_fwd(f_logit, f_bias, name):
    B, S, L = f_logit.shape

    def body(f_ref, b_ref, c_ref):
        z = f_ref[0] + b_ref[...]
        c_ref[0] = jnp.minimum(z, 0.0) - _log1p(jnp.exp(-jnp.abs(z)))

        def step(r0, carry):
            h = _group_scan(None, c_ref[0, pl.ds(r0, 8), :], carry)
            c_ref[0, pl.ds(r0, 8), :] = h
            return h[7:8, :]

        _scan_rows(S, step, jnp.zeros((1, L), F32))

    return pl.pallas_call(
        body, name=name, grid=(B,),
        in_specs=[pl.BlockSpec((1, S, L), lambda b: (b, 0, 0)), pl.BlockSpec((1, L), lambda b: (0, 0))],
        out_specs=pl.BlockSpec((1, S, L), lambda b: (b, 0, 0)),
        out_shape=jax.ShapeDtypeStruct((B, S, L), F32), compiler_params=_params(("parallel",)),
    )(f_logit, f_bias)


def fox_cum_bwd(f_logit, f_bias, dcum, name):
    B, S, L = f_logit.shape

    def body(f_ref, b_ref, d_ref, df_ref, db_ref):
        @pl.when(pl.program_id(0) == 0)
        def _():
            db_ref[...] = jnp.zeros_like(db_ref)

        def step(r0, carry):
            h = _group_scan(None, d_ref[0, pl.ds(r0, 8), :], carry, reverse=True)
            df_ref[0, pl.ds(r0, 8), :] = h
            return h[0:1, :]

        _scan_rows(S, step, jnp.zeros((1, L), F32), reverse=True)
        df = df_ref[0] * _sigmoid(-(f_ref[0] + b_ref[...]))
        df_ref[0] = df
        db_ref[...] += jnp.sum(df, axis=0, keepdims=True)

    seq = pl.BlockSpec((1, S, L), lambda b: (b, 0, 0))
    vec = pl.BlockSpec((1, L), lambda b: (0, 0))
    return pl.pallas_call(
        body, name=name, grid=(B,), in_specs=[seq, vec, seq], out_specs=[seq, vec],
        out_shape=[jax.ShapeDtypeStruct((B, S, L), F32), jax.ShapeDtypeStruct((1, L), F32)],
        compiler_params=_params(("arbitrary",)),
    )(f_logit, f_bias, dcum)


def _head_row(cr, h):
    sub = lax.broadcasted_iota(jnp.int32, cr.shape, 0)
    return jnp.sum(jnp.where(sub == h, cr, 0.0), axis=0, keepdims=True)


def _fox_scores(qk, cc, ck, h, r0):
    lane = lax.broadcasted_iota(jnp.int32, cc.shape, 1)
    cq = jnp.sum(jnp.where(lane == h, cc, 0.0), axis=1, keepdims=True)
    s = qk * (HEAD_DIM ** -0.5) + (cq - ck)
    qpos = r0 + lax.broadcasted_iota(jnp.int32, s.shape, 0)
    kpos = lax.broadcasted_iota(jnp.int32, s.shape, 1)
    return jnp.where(kpos <= qpos, s, NEG_INF)


KEY_STEP = 256


def _by_causal_width(t, S, fn):
    per = KEY_STEP // TQ
    for c in range(S // KEY_STEP):
        pl.when(t // per == c)(functools.partial(fn, (c + 1) * KEY_STEP))


def fox_fwd(proj, cum_col, cum_row, name, heads=1, ride=None):
    _, B, S, W = proj.shape
    nt = S // TQ

    def body(p_ref, cc_ref, cr_ref, o_ref, gt_ref, k_ref, v_ref):
        head0, t = pl.program_id(1) * heads, pl.program_id(2)

        @pl.when(t == 0)
        def _():
            k_ref[...] = p_ref[1, 0].astype(BF16)
            v_ref[...] = p_ref[2, 0].astype(BF16)

        r0 = pl.multiple_of(t * TQ, TQ)

        def tile_out(width):
            for e in range(heads):
                h, lanes = head0 + e, slice(e * HEAD_DIM, (e + 1) * HEAD_DIM)
                q = p_ref[0, 0, pl.ds(r0, TQ), lanes].astype(BF16)
                ck = _head_row(cr_ref[0, :, 0:width], h)
                p = _softmax_rows(_fox_scores(_dot_nt(q, k_ref[0:width, lanes]), cc_ref[0], ck, h, r0))
                o = _dot_nn(p.astype(BF16), v_ref[0:width, lanes])
                o_ref[0, :, lanes] = o
                gt_ref[0, :, lanes] = (o * _silu(p_ref[3, 0, pl.ds(r0, TQ), lanes])).astype(BF16)

        _by_causal_width(t, S, tile_out)

    seq, tile = _head_specs(S, "bht", heads)
    kv = pltpu.VMEM((S, heads * HEAD_DIM), BF16)
    return _call(
        body, name=name, grid=(B, HEADS // heads, nt),
        in_specs=[seq, pl.BlockSpec((1, TQ, cum_col.shape[2]), lambda b, h, t: (b, t, 0)),
                  pl.BlockSpec((1, HEADS, S), lambda b, h, t: (b, 0, 0))],
        out_specs=[tile, tile],
        out_shape=[jax.ShapeDtypeStruct((B, S, W), F32), jax.ShapeDtypeStruct((B, S, W), BF16)],
        scratch_shapes=[kv, kv],
        semantics=("parallel", "parallel", "arbitrary"), args=[proj, cum_col, cum_row], ride=ride)


def fox_bwd(proj, cum_col, cum_row, o, dgated, name, heads=1, ride=None):
    _, B, S, W = proj.shape
    nt = S // TQ

    def body(p_ref, cc_ref, cr_ref, o_ref, dgt_ref, dp_ref, dc_ref, k_ref, v_ref, dk_ref, dv_ref):
        head0, t = pl.program_id(1) * heads, pl.program_id(2)

        @pl.when(t == 0)
        def _():
            k_ref[...] = p_ref[1, 0].astype(BF16)
            v_ref[...] = p_ref[2, 0].astype(BF16)
            dk_ref[...] = jnp.zeros_like(dk_ref)
            dv_ref[...] = jnp.zeros_like(dv_ref)
            dc_ref[...] = jnp.zeros_like(dc_ref)

        r0 = pl.multiple_of(t * TQ, TQ)
        rows = pl.ds(r0, TQ)

        def tile_grads(width):
            for e in range(heads):
                h, lanes = head0 + e, slice(e * HEAD_DIM, (e + 1) * HEAD_DIM)
                q = p_ref[0, 0, rows, lanes].astype(BF16)
                do, dg = _gate_bwd(dgt_ref[0, :, lanes], o_ref[0, :, lanes], p_ref[3, 0, rows, lanes])
                do = do.astype(BF16)
                dp_ref[3, 0, rows, lanes] = dg.astype(BF16)
                k, v = k_ref[0:width, lanes], v_ref[0:width, lanes]
                ck = _head_row(cr_ref[0, :, 0:width], h)
                p = _softmax_rows(_fox_scores(_dot_nt(q, k), cc_ref[0], ck, h, r0))
                dv_ref[0:width, lanes] += _dot_tn(p.astype(BF16), do)
                dpr = _dot_nt(do, v)
                ds = p * (dpr - jnp.sum(p * dpr, axis=-1, keepdims=True))
                dc_ref[0, e, :, 0:width] += jnp.sum(ds, axis=0, keepdims=True)
                ds = (ds * (HEAD_DIM ** -0.5)).astype(BF16)
                dk_ref[0:width, lanes] += _dot_tn(ds, q)
                dp_ref[0, 0, rows, lanes] = _dot_nn(ds, k).astype(BF16)

        _by_causal_width(t, S, tile_grads)

        @pl.when(t == nt - 1)
        def _():
            dp_ref[1, 0] = dk_ref[...].astype(BF16)
            dp_ref[2, 0] = dv_ref[...].astype(BF16)

    seq, tile = _head_specs(S, "bht", heads)
    kv = pltpu.VMEM((S, heads * HEAD_DIM), BF16)
    acc = pltpu.VMEM((S, heads * HEAD_DIM), F32)
    return _call(
        body, name=name, grid=(B, HEADS // heads, nt),
        in_specs=[seq, pl.BlockSpec((1, TQ, cum_col.shape[2]), lambda b, h, t: (b, t, 0)),
                  pl.BlockSpec((1, HEADS, S), lambda b, h, t: (b, 0, 0)), tile, tile],
        out_specs=[seq, pl.BlockSpec((1, heads, 1, S), lambda b, h, t: (b, h, 0, 0))],
        out_shape=[jax.ShapeDtypeStruct(proj.shape, BF16), jax.ShapeDtypeStruct((B, HEADS, 1, S), F32)],
        scratch_shapes=[kv, kv, acc, acc],
        semantics=("parallel", "parallel", "arbitrary"), args=[proj, cum_col, cum_row, o, dgated], ride=ride)


RG_ROWS = 512


def _rg_gates(xc, wa_ref, ba_ref, wx_ref, bx_ref, lam_ref):
    xcb = xc.astype(BF16)
    r = _sigmoid(_dot_nn(xcb, wa_ref[0]) + ba_ref[...])
    i = _sigmoid(_dot_nn(xcb, wx_ref[0]) + bx_ref[...])
    sp = _softplus(-lam_ref[...])
    log_a = (-RG_C * sp) * r
    a = jnp.exp(log_a)
    m = jnp.sqrt(-jnp.tanh(log_a) * (a * a + 1.0))
    return xcb, r, i, sp, a, m


def _rg_specs(B, S, rows, order):
    nc = S // rows

    def idx(fn):
        def index_map(*ids):
            v = dict(zip(order.lower(), ids))
            c = (nc - 1 - v["c"]) if "C" in order else v["c"]
            return fn(v["b"], v["d"], c)
        return index_map

    return dict(
        proj=pl.BlockSpec((2, 1, rows, RG_COLS), idx(lambda b, d, c: (0, b, c, d))),
        act=pl.BlockSpec((1, rows, RG_COLS), idx(lambda b, d, c: (b, c, d))),
        taps=pl.BlockSpec((CONV_WIDTH, RG_COLS), idx(lambda b, d, c: (0, d))),
        vec=pl.BlockSpec((1, RG_COLS), idx(lambda b, d, c: (0, d))),
        gate=pl.BlockSpec((1, RG_COLS, RG_COLS), idx(lambda b, d, c: (d, 0, 0))),
    )


def rglru_fwd(proj, conv_w, conv_b, wa, ba, wx, bx, lam, name, rows=RG_ROWS, ride=None):
    _, B, S, _ = proj.shape
    rows = min(rows, S)
    sp_ = _rg_specs(B, S, rows, "bdc")

    def body(p_ref, cw_ref, cb_ref, wa_ref, ba_ref, wx_ref, bx_ref, lam_ref,
             xc_ref, hs_ref, hp_ref, gt_ref, ext_ref, a_ref, u_ref, xcar_ref, hcar_ref):
        @pl.when(pl.program_id(2) == 0)
        def _():
            xcar_ref[...] = jnp.zeros_like(xcar_ref)
            hcar_ref[...] = jnp.zeros_like(hcar_ref)

        xr = p_ref[0, 0]
        ext_ref[0:8, :] = xcar_ref[...]
        ext_ref[8:, :] = xr
        xcar_ref[...] = xr[rows - 8:, :]
        xc = ext_ref[pl.ds(5, rows), :] * cw_ref[0:1, :]
        xc = xc + ext_ref[pl.ds(6, rows), :] * cw_ref[1:2, :]
        xc = xc + ext_ref[pl.ds(7, rows), :] * cw_ref[2:3, :]
        xc = xc + xr * cw_ref[3:4, :] + cb_ref[...]
        xc_ref[0] = xc
        _, _, i, _, a, m = _rg_gates(xc, wa_ref, ba_ref, wx_ref, bx_ref, lam_ref)
        a_ref[...] = a
        u_ref[...] = m * (i * xc)

        def step(r0, carry):
            h = _group_scan(a_ref[pl.ds(r0, 8), :], u_ref[pl.ds(r0, 8), :], carry)
            row = lax.broadcasted_iota(jnp.int32, h.shape, 0)
            hs_ref[0, pl.ds(r0, 8), :] = h
            hp_ref[0, pl.ds(r0, 8), :] = jnp.where(row == 0, carry, pltpu.roll(h, 1, 0))
            return h[7:8, :]

        hcar_ref[0:1, :] = _scan_rows(rows, step, hcar_ref[0:1, :])
        gt_ref[0] = (hs_ref[0] * _silu(p_ref[1, 0])).astype(BF16)

    act = jax.ShapeDtypeStruct((B, S, RG_WIDTH), F32)
    return _call(
        body, name=name, grid=(B, RG_GROUPS, S // rows),
        in_specs=[sp_["proj"], sp_["taps"], sp_["vec"], sp_["gate"], sp_["vec"], sp_["gate"], sp_["vec"], sp_["vec"]],
        out_specs=[sp_["act"]] * 4,
        out_shape=[act, act, act, jax.ShapeDtypeStruct((B, S, RG_WIDTH), BF16)],
        scratch_shapes=[pltpu.VMEM((rows + 8, RG_COLS), F32), pltpu.VMEM((rows, RG_COLS), F32),
                        pltpu.VMEM((rows, RG_COLS), F32), pltpu.VMEM((8, RG_COLS), F32), pltpu.VMEM((8, RG_COLS), F32)],
        semantics=("parallel", "parallel", "arbitrary"), args=[proj, conv_w, conv_b, wa, ba, wx, bx, lam], ride=ride)


def rglru_bwd(proj, xc, hs, hprev, dgated, conv_w, wa, ba, wx, bx, lam, name, rows=RG_ROWS, ride=None):
    _, B, S, _ = proj.shape
    rows = min(rows, S)
    sp_ = _rg_specs(B, S, rows, "dbC")

    def body(p_ref, xc_ref, hs_ref, hp_ref, dgt_ref, cw_ref, wa_ref, ba_ref, wx_ref, bx_ref, lam_ref,
             dp_ref, dcw_ref, dcb_ref, dwa_ref, dba_ref, dwx_ref, dbx_ref, dlam_ref,
             ext_ref, c_ref, l_ref, acar_ref, lcar_ref, dcar_ref):
        b_, c_ = pl.program_id(1), pl.program_id(2)

        @pl.when(c_ == 0)
        def _():
            acar_ref[...] = jnp.zeros_like(acar_ref)
            lcar_ref[...] = jnp.zeros_like(lcar_ref)
            dcar_ref[...] = jnp.zeros_like(dcar_ref)

        @pl.when((c_ == 0) & (b_ == 0))
        def _():
            for ref in (dcw_ref, dcb_ref, dwa_ref, dba_ref, dwx_ref, dbx_ref, dlam_ref):
                ref[...] = jnp.zeros_like(ref)

        xr, g = p_ref[0, 0], p_ref[1, 0]
        xc_v = xc_ref[0]
        xcb, r, i, sp, a, m = _rg_gates(xc_v, wa_ref, ba_ref, wx_ref, bx_ref, lam_ref)
        dhs, dg = _gate_bwd(dgt_ref[0], hs_ref[0], g)
        dp_ref[1, 0] = dg.astype(BF16)

        ext_ref[0:rows, :] = a
        ext_ref[rows:, :] = acar_ref[...]
        acar_ref[...] = a[0:8, :]
        c_ref[...] = ext_ref[pl.ds(1, rows), :]
        l_ref[...] = dhs

        def step(r0, carry):
            lam_g = _group_scan(c_ref[pl.ds(r0, 8), :], l_ref[pl.ds(r0, 8), :], carry, reverse=True)
            l_ref[pl.ds(r0, 8), :] = lam_g
            return lam_g[0:1, :]

        lcar_ref[0:1, :] = _scan_rows(rows, step, lcar_ref[0:1, :], reverse=True)
        du = l_ref[...]
        da = du * hp_ref[0]
        dlog_a = da * a - (du * (i * xc_v)) * (a * a / m)
        dr = dlog_a * (-RG_C * sp)
        dsp = jnp.sum(dlog_a * (-RG_C * r), axis=0, keepdims=True)
        dlam_ref[...] += dsp * (-_sigmoid(-lam_ref[...]))
        dpa = dr * (r * (1.0 - r))
        dpx = (du * (m * xc_v)) * (i * (1.0 - i))
        dba_ref[...] += jnp.sum(dpa, axis=0, keepdims=True)
        dbx_ref[...] += jnp.sum(dpx, axis=0, keepdims=True)
        dpa, dpx = dpa.astype(BF16), dpx.astype(BF16)
        dwa_ref[0] += _dot_tn(xcb, dpa)
        dwx_ref[0] += _dot_tn(xcb, dpx)
        dxc = du * (m * i) + _dot_nt(dpa, wa_ref[0]) + _dot_nt(dpx, wx_ref[0])

        dcb_ref[...] += jnp.sum(dxc, axis=0, keepdims=True)
        ext_ref[0:rows, :] = dxc
        ext_ref[rows:, :] = dcar_ref[...]
        dcar_ref[...] = dxc[0:8, :]
        dxr = jnp.zeros_like(dxc)
        for k in range(CONV_WIDTH):
            tap = CONV_WIDTH - 1 - k
            ahead = dxc if k == 0 else ext_ref[pl.ds(k, rows), :]
            dxr = dxr + ahead * cw_ref[tap:tap + 1, :]
            dcw_ref[tap:tap + 1, :] += jnp.sum(xr * ahead, axis=0, keepdims=True)
        dp_ref[0, 0] = dxr.astype(BF16)

    vec = jax.ShapeDtypeStruct((1, RG_WIDTH), F32)
    gate = jax.ShapeDtypeStruct((RG_GROUPS, RG_COLS, RG_COLS), F32)
    return _call(
        body, name=name, grid=(RG_GROUPS, B, S // rows),
        in_specs=[sp_["proj"], sp_["act"], sp_["act"], sp_["act"], sp_["act"], sp_["taps"],
                  sp_["gate"], sp_["vec"], sp_["gate"], sp_["vec"], sp_["vec"]],
        out_specs=[sp_["proj"], sp_["taps"], sp_["vec"], sp_["gate"], sp_["vec"], sp_["gate"], sp_["vec"], sp_["vec"]],
        out_shape=[jax.ShapeDtypeStruct(proj.shape, BF16), jax.ShapeDtypeStruct((CONV_WIDTH, RG_WIDTH), F32), vec,
                   gate, vec, gate, vec, vec],
        scratch_shapes=[pltpu.VMEM((rows + 8, RG_COLS), F32), pltpu.VMEM((rows, RG_COLS), F32),
                        pltpu.VMEM((rows, RG_COLS), F32), pltpu.VMEM((8, RG_COLS), F32),
                        pltpu.VMEM((8, RG_COLS), F32), pltpu.VMEM((8, RG_COLS), F32)],
        semantics=("arbitrary", "arbitrary", "arbitrary"),
        args=[proj, xc, hs, hprev, dgated, conv_w, wa, ba, wx, bx, lam], ride=ride)


def block_diag_gates(w):
    per = RG_COLS // RG_BLOCK
    w4 = w.reshape(RG_GROUPS, per, RG_BLOCK, RG_BLOCK)
    return jnp.einsum("dipq,ij->dipjq", w4, jnp.eye(per, dtype=w.dtype)).reshape(RG_GROUPS, RG_COLS, RG_COLS)


def block_diag_gates_t(dw):
    per = RG_COLS // RG_BLOCK
    dw6 = dw.reshape(RG_GROUPS, per, RG_BLOCK, per, RG_BLOCK)
    return jnp.stack([dw6[:, i, :, i, :] for i in range(per)], axis=1).reshape(RG_BLOCKS, RG_BLOCK, RG_BLOCK)


def adamw(w, parts, m, v, name, layer=0, prev=None, part_row0=0, row_tile=ROW_TILE):
    L, R, C = w.shape
    n_parts = parts.shape[0]
    br = row_tile if R % row_tile == 0 else R

    def body(w_ref, p_ref, m_ref, v_ref, *refs):
        g_ref, d_ref, nm_ref, nv_ref = refs[-4:]
        g = p_ref[0].astype(F32)
        for k in range(1, n_parts):
            g = g + p_ref[k].astype(F32)
        nm = ADAM_B1 * m_ref[0] + (1.0 - ADAM_B1) * g
        nv = ADAM_B2 * v_ref[0] + (1.0 - ADAM_B2) * (g * g)
        m_hat = nm / (1.0 - ADAM_B1 ** ADAM_STEP)
        v_hat = nv / (1.0 - ADAM_B2 ** ADAM_STEP)
        g_ref[0] = g
        d_ref[0] = -ADAM_LR * (m_hat / (jnp.sqrt(v_hat) + ADAM_EPS) + ADAM_WD * w_ref[0])
        nm_ref[0] = nm
        nv_ref[0] = nv

    slab = pl.BlockSpec((1, br, C), lambda i: (layer, i, 0))
    out = jax.ShapeDtypeStruct((L, R, C), F32)
    carried = [] if prev is None else list(prev)
    return _call(
        body, name=name, grid=(R // br,),
        in_specs=[slab, pl.BlockSpec((n_parts, br, C), lambda i: (0, part_row0 // br + i, 0)), slab, slab]
        + [pl.BlockSpec(memory_space=pl.ANY)] * len(carried),
        out_specs=[slab] * 4, out_shape=[out] * 4, semantics=("parallel",), args=[w, parts, m, v] + carried,
        aliases={4 + k: k for k in range(len(carried))})


def _seq(a, B):
    return a.reshape(a.shape[:-2] + (B, a.shape[-2] // B, a.shape[-1]))


def _flat(a):
    return a.reshape(a.shape[:-3] + (a.shape[-3] * a.shape[-2], a.shape[-1]))


def _tiles(w, which, **default):
    return dict(default, **w.get("tiles", {}).get(which, {}))


def mixer_a_fwd(h, w, B, tag, rides):
    proj = matmul(h, w["w_in"], mode="nn", out_dtype=F32, name=f"{tag}_proj", out_slabs=4,
                  ride=rides.pop(f"{tag}_proj", None), **_tiles(w, "proj"))
    o, gated = attn_a_fwd(_seq(proj, B), w["bias"], f"{tag}_attn", heads=w.get("attn_heads", (1, 1))[0],
                          ride=rides.pop(f"{tag}_attn", None))
    return _flat(gated), dict(proj=proj, o=o)


def mixer_a_bwd(dgated, w, saved, B, tag, rides):
    dproj, dbias = attn_a_bwd(_seq(saved["proj"], B), w["bias"], saved["o"], _seq(dgated, B), f"{tag}_attn_bwd",
                              heads=w.get("attn_heads", (1, 1))[1], ride=rides.pop(f"{tag}_attn_bwd", None))
    return _flat(dproj), dict(bias=dbias)


def mixer_b_fwd(h, w, B, tag, rides):
    proj = matmul(h, w["w_in"], mode="nn", out_dtype=F32, name=f"{tag}_proj", out_slabs=2, bn=RG_COLS,
                  ride=rides.pop(f"{tag}_proj", None))
    xc, hs, hprev, gated = rglru_fwd(_seq(proj, B), w["conv_w"], w["conv_b"], w["wa"], w["ba"], w["wx"], w["bx"],
                                     w["lam"], f"{tag}_rglru", ride=rides.pop(f"{tag}_rglru", None))
    return _flat(gated), dict(proj=proj, xc=xc, hs=hs, hprev=hprev)


def mixer_b_bwd(dgated, w, saved, B, tag, rides):
    dproj, dcw, dcb, dwa, dba, dwx, dbx, dlam = rglru_bwd(
        _seq(saved["proj"], B), saved["xc"], saved["hs"], saved["hprev"], _seq(dgated, B),
        w["conv_w"], w["wa"], w["ba"], w["wx"], w["bx"], w["lam"], f"{tag}_rglru_bwd",
        ride=rides.pop(f"{tag}_rglru_bwd", None))
    return _flat(dproj), dict(conv_w=dcw, conv_b=dcb, wa=dwa, ba=dba, wx=dwx, bx=dbx, lam=dlam)


def mixer_c_fwd(h, w, B, tag, rides):
    proj = matmul(h, w["w_in"], mode="nn", out_dtype=F32, name=f"{tag}_proj", out_slabs=4,
                  ride=rides.pop(f"{tag}_proj", None), **_tiles(w, "proj"))
    f_logit = matmul(h, w["w_f"], mode="nn", out_dtype=F32, name=f"{tag}_fproj")
    cum = fox_cum_fwd(_seq(f_logit, B), w["f_bias"], f"{tag}_cum")
    cum_row = cum[:, :, :HEADS].transpose(0, 2, 1)
    o, gated = fox_fwd(_seq(proj, B), cum, cum_row, f"{tag}_attn", heads=w.get("attn_heads", (1, 1))[0],
                       ride=rides.pop(f"{tag}_attn", None))
    return _flat(gated), dict(proj=proj, o=o, f_logit=f_logit, cum=cum, cum_row=cum_row)


def mixer_c_bwd(dgated, w, saved, B, tag, rides):
    dproj, dck = fox_bwd(_seq(saved["proj"], B), saved["cum"], saved["cum_row"], saved["o"], _seq(dgated, B),
                         f"{tag}_attn_bwd", heads=w.get("attn_heads", (1, 1))[1],
                         ride=rides.pop(f"{tag}_attn_bwd", None))
    S = dck.shape[-1]
    dcum = jnp.pad(-dck.reshape(B, HEADS, S).transpose(0, 2, 1), ((0, 0), (0, 0), (0, HEAD_DIM - HEADS)))
    df, dfb = fox_cum_bwd(_seq(saved["f_logit"], B), w["f_bias"], dcum, f"{tag}_cum_bwd")
    return _flat(dproj), dict(f_bias=dfb, df=_flat(df).astype(BF16))


MIXERS = {"a": (mixer_a_fwd, mixer_a_bwd), "b": (mixer_b_fwd, mixer_b_bwd), "c": (mixer_c_fwd, mixer_c_bwd)}
LAYER_KINDS = "abca"


def local_step(x, target, norm_pre, norm_post, get_layer, rides, on_grads):
    B, S, Dm = x.shape
    n_layers = len(LAYER_KINDS)
    xs = [x.reshape(B * S, Dm)]
    saved, layers = [], []
    h = prenorm_fwd(xs[0], norm_pre[0:1], "l0a_prenorm")
    for li, kind in enumerate(LAYER_KINDS):
        tag = f"l{li}{kind}"
        w = get_layer(li)
        gated, sv = MIXERS[kind][0](h, w, B, tag, rides)
        if callable(w["w_out"]):
            w["w_out"] = w["w_out"]()
        y = matmul(gated, w["w_out"], mode="nn", out_dtype=F32, name=f"{tag}_out", ride=rides.pop(f"{tag}_out", None))
        saved.append(dict(sv, h=h, gated=gated, y=y))
        layers.append(w)
        if li + 1 < n_layers:
            x_new, h = postnorm_prenorm_fwd(xs[-1], y, norm_post[li:li + 1], norm_pre[li + 1:li + 2],
                                            f"{tag}_postnorm")
            xs.append(x_new)
    loss, dx = postnorm_loss(xs[-1], y, norm_post[n_layers - 1:], target.reshape(B * S, Dm), "loss")

    for li in reversed(range(n_layers)):
        kind, w, sv = LAYER_KINDS[li], layers[li], saved[li]
        tag = f"l{li}{kind}"
        dy, dg_post = postnorm_bwd(sv["y"], norm_post[li:li + 1], dx, f"{tag}_postnorm_bwd")
        on_grads(li, "norm_post", dg_post)
        on_grads(li, "w_out", matmul(sv["gated"], dy, mode="tn", out_dtype=BF16, name=f"{tag}_dwout",
                                     ride=rides.pop(f"{tag}_dwout", None)))
        dgated = matmul(dy, w["w_out"], mode="nt", out_dtype=F32, name=f"{tag}_dgated",
                        ride=rides.pop(f"{tag}_dgated", None))
        dproj, gw = MIXERS[kind][1](dgated, w, sv, B, tag, rides)
        df = gw.pop("df", None)
        for name, value in gw.items():
            on_grads(li, name, value)
        parts = w.get("dwin_parts", [(0, 1, 1)])
        for i, m_part in enumerate(parts):
            suffix = f"_{i}" if len(parts) > 1 else ""
            on_grads(li, "w_in" + suffix,
                     matmul(sv["h"], dproj, mode="tn", out_dtype=BF16, name=f"{tag}_dwin{suffix}", m_part=m_part,
                            out_slabs=w["grad_slabs"], ride=rides.pop(f"{tag}_dwin{suffix}", None),
                            **_tiles(w, "dwin")))
        if df is not None:
            on_grads(li, "w_f", matmul(sv["h"], df, mode="tn", out_dtype=BF16, name=f"{tag}_dwf"))
        dhs = [matmul(dproj, w["w_in"], mode="nt", out_dtype=F32, name=f"{tag}_dh",
                      ride=rides.pop(f"{tag}_dh", None), **_tiles(w, "dh"))]
        if df is not None:
            dhs.append(matmul(df, w["w_f"], mode="nt", out_dtype=F32, name=f"{tag}_dhf"))
        dx, dg_pre = prenorm_bwd(xs[li], norm_pre[li:li + 1], dhs, dx, f"{tag}_prenorm_bwd")
        on_grads(li, "norm_pre", dg_pre)
    assert not rides, list(rides)
    return loss, dx.reshape(B, S, Dm)


WEIGHTS = ("norm_pre", "norm_post", "a_w_in", "a_rel_bias", "a_w_out", "b_w_in", "b_conv_w", "b_conv_b",
           "b_gate_a_w", "b_gate_a_b", "b_gate_x_w", "b_gate_x_b", "b_lambda", "b_w_out", "c_w_in", "c_f_bias",
           "c_w_out")
C_SHARD = (4 * D_MODEL + HEADS) // N_DEV


def _rows(gathered):
    return gathered.reshape(gathered.shape[0] * gathered.shape[1], gathered.shape[2])


def layer_a(w_in, w_out, rel_bias):
    return dict(w_in=w_in, w_out=w_out if callable(w_out) else _rows(w_out), bias=band_bias(rel_bias),
                grad_slabs=N_DEV)


def layer_b(w_in, w_out, conv_w, small):
    return dict(
        w_in=w_in, w_out=w_out if callable(w_out) else _rows(w_out), grad_slabs=N_DEV,
        conv_w=conv_w.transpose(1, 0, 2).reshape(CONV_WIDTH, RG_WIDTH),
        conv_b=small["b_conv_b"], lam=small["b_lambda"],
        wa=block_diag_gates(small["b_gate_a_w"][0]).astype(BF16), ba=small["b_gate_a_b"].reshape(1, RG_WIDTH),
        wx=block_diag_gates(small["b_gate_x_w"][0]).astype(BF16), bx=small["b_gate_x_b"].reshape(1, RG_WIDTH))


def layer_c(w_in, w_out, small):
    full = w_in.transpose(1, 0, 2).reshape(D_MODEL, N_DEV * C_SHARD)
    return dict(w_in=full[:, :4 * D_MODEL], w_f=jnp.pad(full[:, 4 * D_MODEL:], ((0, 0), (0, HEAD_DIM - HEADS))),
                w_out=_rows(w_out), grad_slabs=1,
                f_bias=jnp.pad(small["c_f_bias"], ((0, 0), (0, HEAD_DIM - HEADS))))


def c_w_in_blocks(dmain, df):
    full = jnp.concatenate([dmain, df[:, :HEADS].astype(dmain.dtype)], axis=1)
    return full.reshape(D_MODEL, N_DEV, C_SHARD).transpose(1, 0, 2)


def _row_blocks(g):
    return g.reshape(N_DEV, g.shape[0] // N_DEV, g.shape[1])


PACK_LANES = 128
PACK_ALIGN = 8 * PACK_LANES


def pack(parts):
    flat = []
    for p in parts:
        n = p.size
        flat.append(jnp.pad(p.reshape(n), (0, -n % PACK_ALIGN)).reshape(-1, PACK_LANES))
    rows = sum(f.shape[0] for f in flat)
    flat.append(jnp.zeros((-rows % ROW_TILE, PACK_LANES), F32))
    return jnp.concatenate(flat, axis=0)


def unpack(packed, shapes):
    out, row = [], 0
    for shape in shapes:
        n = 1
        for s in shape:
            n *= s
        n_rows = (n + PACK_ALIGN - 1) // PACK_ALIGN * 8
        out.append(packed[row:row + n_rows].reshape(-1)[:n].reshape(shape))
        row += n_rows
    return out


LATE = (("a_rel_bias", slice(0, 1)), ("norm_pre", slice(0, 2)), ("norm_post", slice(0, 1)))
EARLY = (("a_rel_bias", slice(1, 2)), ("norm_pre", slice(2, 4)), ("norm_post", slice(1, 4)),
         ("b_conv_b", slice(None)), ("b_gate_a_w", slice(None)), ("b_gate_a_b", slice(None)),
         ("b_gate_x_w", slice(None)), ("b_gate_x_b", slice(None)), ("b_lambda", slice(None)),
         ("c_f_bias", slice(None)))


def _pieces(tree, pieces):
    return [tree[name][sl] for name, sl in pieces]


def kernel(x, norm_pre, norm_post, a_w_in, a_rel_bias, a_w_out, b_w_in, b_conv_w, b_conv_b, b_gate_a_w, b_gate_a_b, b_gate_x_w, b_gate_x_b, b_lambda, b_w_out, c_w_in, c_f_bias, c_w_out, loss_target, m_norm_pre, m_norm_post, m_a_w_in, m_a_rel_bias, m_a_w_out, m_b_w_in, m_b_conv_w, m_b_conv_b, m_b_gate_a_w, m_b_gate_a_b, m_b_gate_x_w, m_b_gate_x_b, m_b_lambda, m_b_w_out, m_c_w_in, m_c_f_bias, m_c_w_out, v_norm_pre, v_norm_post, v_a_w_in, v_a_rel_bias, v_a_w_out, v_b_w_in, v_b_conv_w, v_b_conv_b, v_b_gate_a_w, v_b_gate_a_b, v_b_gate_x_w, v_b_gate_x_b, v_b_lambda, v_b_w_out, v_c_w_in, v_c_f_bias, v_c_w_out):
    args = dict(locals())
    w = {n: args[n] for n in WEIGHTS}
    m = {n: args["m_" + n] for n in WEIGHTS}
    v = {n: args["v_" + n] for n in WEIGHTS}

    a_in, a_out = a_w_in.astype(BF16), a_w_out.astype(BF16)
    gather_a0 = Ride([a_in[0]], scatter=False, via_sibling=True)
    in_l0_proj = Ride([b_w_in[0].astype(BF16), b_conv_w[0], a_out[0]], scatter=False, via_sibling=True)
    in_l0_attn = Ride([c_w_in[0].astype(BF16)], scatter=False, via_sibling=True)
    in_l1_proj = Ride([b_w_out[0].astype(BF16)], scatter=False)
    in_l1_rglru = Ride([c_w_out[0].astype(BF16)], scatter=False)
    in_l2_proj = Ride([a_out[1]], scatter=False)
    in_l2_attn = Ride([a_in[1]], scatter=False, via_sibling=True)
    exchange(gather_a0, "gather_l0")
    rides = {"l0a_proj": in_l0_proj, "l0a_attn": in_l0_attn, "l1b_proj": in_l1_proj, "l1b_rglru": in_l1_rglru,
             "l2c_proj": in_l2_proj, "l2c_attn": in_l2_attn}

    def get_layer(li):
        if li == 0:
            return dict(layer_a(gather_a0.out[0], lambda: _rows(in_l0_proj.out[2]), a_rel_bias[0]),
                        dwin_parts=[(0, 1, 4), (1, 1, 4), (2, 2, 4)], attn_heads=(4, 2))
        if li == 1:
            return layer_b(in_l0_proj.out[0], lambda: _rows(in_l1_proj.out[0]), in_l0_proj.out[1], w)
        if li == 2:
            return dict(layer_c(in_l0_attn.out[0], in_l1_rglru.out[0], w), tiles=dict(proj=dict(bn=2048)),
                        attn_heads=(4, 2))
        return dict(layer_a(in_l2_attn.out[0], in_l2_proj.out[0], a_rel_bias[1]), attn_heads=(4, 2))

    grads = [dict() for _ in LAYER_KINDS]
    scatters = {}

    def rel_bias_grad(j, dbias):
        return jax.vjp(band_bias, a_rel_bias[j])[1](dbias)[0][None]

    def early_partial():
        gb, gc = grads[1], grads[2]
        tree = dict(
            a_rel_bias=jnp.concatenate([jnp.zeros((1, HEADS, N_REL), F32), rel_bias_grad(1, grads[3]["bias"])]),
            norm_pre=jnp.concatenate([jnp.zeros((2, D_MODEL), F32)] + [grads[li]["norm_pre"] for li in (2, 3)]),
            norm_post=jnp.concatenate([jnp.zeros((1, D_MODEL), F32)] + [grads[li]["norm_post"] for li in (1, 2, 3)]),
            b_conv_b=gb["conv_b"], b_lambda=gb["lam"],
            b_gate_a_w=block_diag_gates_t(gb["wa"])[None], b_gate_a_b=gb["ba"].reshape(1, RG_BLOCKS, RG_BLOCK),
            b_gate_x_w=block_diag_gates_t(gb["wx"])[None], b_gate_x_b=gb["bx"].reshape(1, RG_BLOCKS, RG_BLOCK),
            c_f_bias=gc["f_bias"][:, :HEADS])
        return pack(_pieces(tree, EARLY))

    def send(key, host, blocks, scatter=True, via_sibling=False):
        ride = rides.setdefault(host, Ride([], scatter, via_sibling))
        assert (ride.scatter, ride.via_sibling) == (scatter, via_sibling)
        scatters[key] = (ride, len(ride.arrs))
        ride.arrs.append(blocks)

    def on_grads(li, name, value):
        g = grads[li]
        g[name] = value
        if (li, name) == (3, "w_out"):
            send("a1_out", "l3a_attn_bwd", _row_blocks(value))
        elif (li, name) == (3, "w_in"):
            send("a1_in", "l2c_attn_bwd", value)
        elif (li, name) == (2, "w_out"):
            send("c_out", "l2c_attn_bwd", _row_blocks(value))
        elif (li, name) == (2, "w_f"):
            blocks = c_w_in_blocks(g["w_in"], value)
            send("c_in_0", "l2c_dh", blocks[:, :D_MODEL // 2])
            send("c_in_1", "l1b_rglru_bwd", blocks[:, D_MODEL // 2:])
        elif (li, name) == (1, "w_out"):
            send("b_out", "l1b_dh", _row_blocks(value))
        elif (li, name) == (1, "w_in"):
            send("b_in", "l0a_attn_bwd", value)
            send("b_conv", "l0a_attn_bwd",
                 g["conv_w"].reshape(CONV_WIDTH, N_DEV, RG_WIDTH // N_DEV).transpose(1, 0, 2))
        elif (li, name) == (1, "lam"):
            send("early", "l1b_dwin", early_partial(), scatter=False, via_sibling=True)
        elif (li, name) == (0, "w_out"):
            send("a0_out", "l0a_attn_bwd", _row_blocks(value))
        elif (li, name) == (0, "w_in_0"):
            send("a0_in_0", "l0a_dwin_1", value)
        elif (li, name) == (0, "w_in_1"):
            send("a0_in_1", "l0a_dwin_2", value)
        elif (li, name) == (0, "w_in_2"):
            send("a0_in_2", "l0a_dh", value)

    loss, grad_x = local_step(x, loss_target, norm_pre, norm_post, get_layer, rides, on_grads)
    late_tree = dict(a_rel_bias=rel_bias_grad(0, grads[0]["bias"]), norm_post=grads[0]["norm_post"],
                     norm_pre=jnp.concatenate([grads[0]["norm_pre"], grads[1]["norm_pre"]]))
    late_parts = exchange(Ride([pack([late_tree[n] for n, _ in LATE])], scatter=False), "gather_late_grads")[0]

    def sharded(name, slab_parts):
        shape = w[name].shape
        slabs = (len(slab_parts), shape[0] * shape[1] // len(slab_parts), shape[2])
        outs = None
        for j, (parts, row0) in enumerate(slab_parts):
            outs = adamw(w[name].reshape(slabs), parts, m[name].reshape(slabs), v[name].reshape(slabs),
                         f"adamw_{name}_{j}", layer=j, prev=outs, part_row0=row0)
        return [o.reshape(shape) for o in outs]

    def received(key):
        ride, position = scatters[key]
        return ride.out[position]

    res = dict(
        a_w_in=sharded("a_w_in", [(received("a0_in_0"), 0), (received("a0_in_1"), 0), (received("a0_in_2"), 0),
                                  (received("a0_in_2"), D_MODEL // 4)]
                       + [(received("a1_in"), q * D_MODEL // 4) for q in range(4)]),
        a_w_out=sharded("a_w_out", [(received("a0_out"), 0), (received("a1_out"), 0)]),
        b_w_in=sharded("b_w_in", [(received("b_in"), 0)]),
        b_w_out=sharded("b_w_out", [(received("b_out"), 0)]),
        b_conv_w=sharded("b_conv_w", [(received("b_conv"), 0)]),
        c_w_in=sharded("c_w_in", [(received("c_in_0"), 0), (received("c_in_1"), 0)]),
        c_w_out=sharded("c_w_out", [(received("c_out"), 0)]))

    packed = {}
    for label, pieces, parts in (("early", EARLY, received("early")), ("late", LATE, late_parts)):
        outs = adamw(pack(_pieces(w, pieces))[None], parts, pack(_pieces(m, pieces))[None],
                     pack(_pieces(v, pieces))[None], f"adamw_replicated_{label}")
        shapes = [w[n][sl].shape for n, sl in pieces]
        packed[label] = [dict(zip([n for n, _ in pieces], unpack(o[0], shapes))) for o in outs]
    for n in ("b_conv_b", "b_gate_a_w", "b_gate_a_b", "b_gate_x_w", "b_gate_x_b", "b_lambda", "c_f_bias"):
        res[n] = [packed["early"][k][n] for k in range(4)]
    for n in ("a_rel_bias", "norm_pre", "norm_post"):
        res[n] = [jnp.concatenate([packed["late"][k][n], packed["early"][k][n]]) for k in range(4)]

    total = lax.psum(loss[0, 0], ("x", "y", "c"))
    return (total, grad_x, *[res[n][0] for n in WEIGHTS], *[res[n][1] for n in WEIGHTS],
            *[res[n][2] for n in WEIGHTS], *[res[n][3] for n in WEIGHTS])
```

```python
import functools

import jax
import jax.numpy as jnp
from jax import lax
from jax.experimental import pallas as pl
from jax.experimental.pallas import tpu as pltpu

F32 = jnp.float32
BF16 = jnp.bfloat16

N_DEV = 8
D_MODEL = 2048
HEADS = 16
HEAD_DIM = 128
CHUNK = 64
LEFT_CHUNKS = 8
REL_CLIP = 256
N_REL = 2 * REL_CLIP + 1
TQ = 256
A_PAD = LEFT_CHUNKS * CHUNK
A_KW = A_PAD + TQ
RG_WIDTH = 2560
RG_BLOCKS = 16
RG_BLOCK = 160
RG_COLS = 640
RG_GROUPS = RG_WIDTH // RG_COLS
RG_C = 8.0
CONV_WIDTH = 4
RMS_EPS = 1e-6
NEG_INF = -1e30
ADAM_LR = 0.001
ADAM_B1 = 0.9
ADAM_B2 = 0.999
ADAM_EPS = 1e-08
ADAM_WD = 0.01
ADAM_STEP = 10
VMEM_LIMIT = 56 * 1024 * 1024
MESH = pl.DeviceIdType.MESH


def _params(sem, vmem=VMEM_LIMIT):
    return pltpu.CompilerParams(dimension_semantics=sem, vmem_limit_bytes=vmem)


def _sigmoid(x):
    return 1.0 / (1.0 + jnp.exp(-x))


def _log1p(y):
    u = 1.0 + y
    return jnp.where(u == 1.0, y, jnp.log(u) * (y / jnp.where(u == 1.0, 1.0, u - 1.0)))


def _softplus(x):
    return jnp.maximum(x, 0.0) + _log1p(jnp.exp(-jnp.abs(x)))


def _dot(a, b, dims):
    return lax.dot_general(a, b, (dims, ((), ())), preferred_element_type=F32)


def _dot_nn(a, b):
    return _dot(a, b, ((1,), (0,)))


def _dot_nt(a, b):
    return _dot(a, b, ((1,), (1,)))


def _dot_tn(a, b):
    return _dot(a, b, ((0,), (0,)))


def _peers():
    x, y, c = lax.axis_index("x"), lax.axis_index("y"), lax.axis_index("c")
    me = 4 * x + 2 * y + c
    peers = []
    for k in range(1, N_DEV):
        px = 1 - x if k & 4 else x
        py = 1 - y if k & 2 else y
        pc = 1 - c if k & 1 else c
        peers.append(((px, py, pc), 4 * px + 2 * py + pc))
    return me, peers


class Ride:
    def __init__(self, arrs, scatter, via_sibling=False):
        assert not (scatter and via_sibling)
        self.arrs, self.scatter, self.via_sibling, self.out = list(arrs), scatter, via_sibling, None

    def out_shapes(self):
        return [jax.ShapeDtypeStruct(a.shape if self.scatter else (N_DEV,) + a.shape, a.dtype) for a in self.arrs]

    def sem_shapes(self):
        n = len(self.arrs)
        return [pltpu.SemaphoreType.DMA((n, N_DEV - 1)), pltpu.SemaphoreType.DMA((n, N_DEV - 1)),
                pltpu.SemaphoreType.DMA((n,))]

    def _copies(self, ins, outs, sems, landing):
        send_sems, recv_sems, local_sems = sems
        me, peers = _peers()
        local, remote = [], []
        for a, (src, dst) in enumerate(zip(ins, outs)):
            local.append(pltpu.make_async_copy(src.at[me] if self.scatter else src, dst.at[me], local_sems.at[a]))
            for k, (peer, peer_idx) in enumerate(peers):
                remote.append(pltpu.make_async_remote_copy(
                    src_ref=src.at[peer_idx] if self.scatter else src, dst_ref=dst.at[peer_idx if landing else me],
                    send_sem=send_sems.at[a, k], recv_sem=recv_sems.at[a, k], device_id=peer, device_id_type=MESH))
        return local, remote

    def _direct(self, k):
        return not self.via_sibling or k == 0 or (k + 1) % 2 == 0

    def start(self, ins, outs, sems):
        local, remote = self._copies(ins, outs, sems, landing=False)
        n_peers = N_DEV - 1
        for cp in local + [cp for i, cp in enumerate(remote) if self._direct(i % n_peers)]:
            cp.start()

    def wait(self, ins, outs, sems):
        local, remote = self._copies(ins, outs, sems, landing=True)
        n_peers = N_DEV - 1
        for i, cp in enumerate(remote):
            if self._direct(i % n_peers):
                cp.wait()
        if self.via_sibling:
            send_sems, recv_sems, _ = sems
            me, peers = _peers()
            sibling = peers[0][0]
            passed = []
            for a, dst in enumerate(outs):
                for j in range(1, n_peers, 2):
                    came, lands = peers[j][1], peers[j + 1][1]
                    pltpu.make_async_remote_copy(
                        src_ref=dst.at[came], dst_ref=dst.at[came], send_sem=send_sems.at[a, j + 1],
                        recv_sem=recv_sems.at[a, j + 1], device_id=sibling, device_id_type=MESH).start()
                    passed.append(pltpu.make_async_remote_copy(
                        src_ref=dst.at[came], dst_ref=dst.at[lands], send_sem=send_sems.at[a, j + 1],
                        recv_sem=recv_sems.at[a, j + 1], device_id=sibling, device_id_type=MESH))
            for cp in passed:
                cp.wait()
        for cp in local:
            cp.wait()


def _call(body, *, name, grid, in_specs, out_specs, out_shape, args, scratch_shapes=(), semantics=None, ride=None,
          aliases=None):
    scratch_shapes = list(scratch_shapes)
    if ride is None:
        return pl.pallas_call(
            body, name=name, grid=grid, in_specs=in_specs, out_specs=out_specs, out_shape=out_shape,
            scratch_shapes=scratch_shapes, input_output_aliases=aliases or {},
            compiler_params=_params(semantics if grid else None))(*args)
    assert not aliases
    n_in, n_out, n_sc, n_r = len(in_specs), len(out_specs), len(scratch_shapes), len(ride.arrs)

    def riding(*refs):
        ins, r_ins = refs[:n_in], refs[n_in:n_in + n_r]
        outs, r_outs = refs[n_in + n_r:n_in + n_r + n_out], refs[n_in + n_r + n_out:n_in + 2 * n_r + n_out]
        rest = refs[n_in + 2 * n_r + n_out:]
        scratch, sems = rest[:n_sc], rest[n_sc:]
        first = last = None
        for axis, size in enumerate(grid):
            pid = pl.program_id(axis)
            first = (pid == 0) if first is None else first & (pid == 0)
            last = (pid == size - 1) if last is None else last & (pid == size - 1)
        if grid:
            pl.when(first)(lambda: ride.start(r_ins, r_outs, sems))
        else:
            ride.start(r_ins, r_outs, sems)
        body(*ins, *outs, *scratch)
        if grid:
            pl.when(last)(lambda: ride.wait(r_ins, r_outs, sems))
        else:
            ride.wait(r_ins, r_outs, sems)

    any_spec = pl.BlockSpec(memory_space=pl.ANY)
    res = pl.pallas_call(
        riding, name=name, grid=grid, in_specs=list(in_specs) + [any_spec] * n_r,
        out_specs=list(out_specs) + [any_spec] * n_r, out_shape=list(out_shape) + ride.out_shapes(),
        scratch_shapes=scratch_shapes + ride.sem_shapes(),
        compiler_params=_params(("arbitrary",) * len(grid) if grid else None))(*args, *ride.arrs)
    ride.out = list(res[n_out:])
    return list(res[:n_out])


def exchange(ride, name):
    _call(lambda: None, name=name, grid=(), in_specs=[], out_specs=[], out_shape=[], args=[], ride=ride)
    return ride.out


LANES = 128


def _fit(dims, want):
    dims = tuple(dims)
    if len(set(dims)) == 1 and dims[0] <= want:
        return dims[0]
    return max(t for t in range(LANES, want + 1, LANES) if all(d % t == 0 for d in dims))


def _cols(arr):
    return arr.shape[-1] * (arr.shape[0] if len(arr.shape) == 3 else 1)


def _tile_spec(shape, rblk, cblk, rc):
    if len(shape) == 2:
        return pl.BlockSpec((rblk, cblk), rc)
    per = shape[2] // cblk

    def index_map(*ids):
        r, c = rc(*ids)
        return (c // per, r, c % per)

    return pl.BlockSpec((1, rblk, cblk), index_map)


def matmul(a, b, *, mode, out_dtype, name, bm=1024, bn=1024, bk=2048, out_slabs=1, m_part=(0, 1, 1), ride=None):
    a_rows, a_cols, b_rows, b_cols = a.shape[-2], _cols(a), b.shape[-2], _cols(b)
    (K, M) = (a_rows, a_cols) if mode == "tn" else (a_cols, a_rows)
    N = b_rows if mode == "nt" else b_cols
    assert K == (b_cols if mode == "nt" else b_rows), (name, a.shape, b.shape)
    first_range, n_ranges, of_ranges = m_part
    row0, M = first_range * (M // of_ranges), n_ranges * (M // of_ranges)
    out_shape = (M, N) if out_slabs == 1 else (out_slabs, M, N // out_slabs)
    widths = dict(m=[M], n=[N, out_shape[-1]], k=[K])
    widths["m" if mode == "tn" else "k"].append(a.shape[-1])
    widths["k" if mode == "nt" else "n"].append(b.shape[-1])
    bm, bn, bk = _fit(widths["m"], bm), _fit(widths["n"], bn), _fit(widths["k"], bk)
    nk = K // bk
    dims = {"nn": ((1,), (0,)), "nt": ((1,), (1,)), "tn": ((0,), (0,))}[mode]

    def val(ref):
        return ref[0] if len(ref.shape) == 3 else ref[...]

    def put(ref, x):
        if len(ref.shape) == 3:
            ref[0] = x.astype(ref.dtype)
        else:
            ref[...] = x.astype(ref.dtype)

    def body(a_ref, b_ref, o_ref, *scratch):
        if nk == 1:
            put(o_ref, _dot(val(a_ref), val(b_ref), dims))
            return
        acc_ref, = scratch
        k = pl.program_id(2)

        @pl.when(k == 0)
        def _():
            acc_ref[...] = jnp.zeros_like(acc_ref)

        acc_ref[...] += _dot(val(a_ref), val(b_ref), dims)

        @pl.when(k == nk - 1)
        def _():
            put(o_ref, acc_ref[...])

    assert row0 % bm == 0
    m0 = row0 // bm
    if mode == "tn":
        a_spec = _tile_spec(a.shape, bk, bm, lambda j, i, k: (k, m0 + i))
    else:
        a_spec = _tile_spec(a.shape, bm, bk, lambda j, i, k: (m0 + i, k))
    if mode == "nt":
        b_spec = _tile_spec(b.shape, bn, bk, lambda j, i, k: (j, k))
    else:
        b_spec = _tile_spec(b.shape, bk, bn, lambda j, i, k: (k, j))
    return _call(
        body, name=name, grid=(N // bn, M // bm, nk), in_specs=[a_spec, b_spec],
        out_specs=[_tile_spec(out_shape, bm, bn, lambda j, i, k: (i, j))],
        out_shape=[jax.ShapeDtypeStruct(out_shape, out_dtype)],
        scratch_shapes=[] if nk == 1 else [pltpu.VMEM((bm, bn), F32)],
        semantics=("parallel", "parallel", "arbitrary"), args=[a, b], ride=ride)[0]


ROW_TILE = 256


def _rms_stats(z):
    r = lax.rsqrt(jnp.mean(z * z, axis=-1, keepdims=True) + RMS_EPS)
    return r, z * r


def _rms_bwd(n, r, g, dout):
    dn = dout * g
    return r * (dn - n * jnp.mean(dn * n, axis=-1, keepdims=True))


def _row_spec(T, Dm):
    bt = min(ROW_TILE, T)
    return bt, pl.BlockSpec((bt, Dm), lambda i: (i, 0)), pl.BlockSpec((1, Dm), lambda i: (0, 0))


def prenorm_fwd(x, g, name):
    T, Dm = x.shape
    bt, row, vec = _row_spec(T, Dm)

    def body(x_ref, g_ref, h_ref):
        _, n = _rms_stats(x_ref[...])
        h_ref[...] = (n * g_ref[...]).astype(BF16)

    return pl.pallas_call(
        body, name=name, grid=(T // bt,), in_specs=[row, vec], out_specs=row,
        out_shape=jax.ShapeDtypeStruct((T, Dm), BF16), compiler_params=_params(("parallel",)),
    )(x, g)


def postnorm_prenorm_fwd(x, y, g_post, g_next, name):
    T, Dm = x.shape
    bt, row, vec = _row_spec(T, Dm)

    def body(x_ref, y_ref, gp_ref, gn_ref, o_ref, h_ref):
        _, n = _rms_stats(y_ref[...])
        x_new = x_ref[...] + n * gp_ref[...]
        o_ref[...] = x_new
        _, n_new = _rms_stats(x_new)
        h_ref[...] = (n_new * gn_ref[...]).astype(BF16)

    return pl.pallas_call(
        body, name=name, grid=(T // bt,), in_specs=[row, row, vec, vec], out_specs=[row, row],
        out_shape=[jax.ShapeDtypeStruct((T, Dm), F32), jax.ShapeDtypeStruct((T, Dm), BF16)],
        compiler_params=_params(("parallel",)),
    )(x, y, g_post, g_next)


def postnorm_loss(x, y, g, target, name):
    T, Dm = x.shape
    bt, row, vec = _row_spec(T, Dm)

    def body(x_ref, y_ref, g_ref, t_ref, l_ref, d_ref):
        @pl.when(pl.program_id(0) == 0)
        def _():
            l_ref[...] = jnp.zeros_like(l_ref)

        _, n = _rms_stats(y_ref[...])
        err = (x_ref[...] + n * g_ref[...]) - t_ref[...]
        per_tok = jnp.mean(err * err, axis=-1, keepdims=True)
        l_ref[...] += 0.5 * jnp.sum(per_tok, axis=0, keepdims=True)
        d_ref[...] = err * (1.0 / Dm)

    return pl.pallas_call(
        body, name=name, grid=(T // bt,), in_specs=[row, row, vec, row],
        out_specs=[pl.BlockSpec((1, 1), lambda i: (0, 0)), row],
        out_shape=[jax.ShapeDtypeStruct((1, 1), F32), jax.ShapeDtypeStruct((T, Dm), F32)],
        compiler_params=_params(("arbitrary",)),
    )(x, y, g, target)


def postnorm_bwd(y, g, dout, name):
    T, Dm = y.shape
    bt, row, vec = _row_spec(T, Dm)

    def body(y_ref, g_ref, d_ref, dy_ref, dg_ref):
        @pl.when(pl.program_id(0) == 0)
        def _():
            dg_ref[...] = jnp.zeros_like(dg_ref)

        r, n = _rms_stats(y_ref[...])
        dout_v = d_ref[...]
        dg_ref[...] += jnp.sum(dout_v * n, axis=0, keepdims=True)
        dy_ref[...] = _rms_bwd(n, r, g_ref[...], dout_v).astype(BF16)

    return pl.pallas_call(
        body, name=name, grid=(T // bt,), in_specs=[row, vec, row], out_specs=[row, vec],
        out_shape=[jax.ShapeDtypeStruct((T, Dm), BF16), jax.ShapeDtypeStruct((1, Dm), F32)],
        compiler_params=_params(("arbitrary",)),
    )(y, g, dout)


def prenorm_bwd(x, g, dhs, dres, name):
    T, Dm = x.shape
    bt, row, vec = _row_spec(T, Dm)
    n_dh = len(dhs)

    def body(x_ref, g_ref, *refs):
        dh_refs, (dr_ref, dx_ref, dg_ref) = refs[:n_dh], refs[n_dh:]

        @pl.when(pl.program_id(0) == 0)
        def _():
            dg_ref[...] = jnp.zeros_like(dg_ref)

        r, n = _rms_stats(x_ref[...])
        dh_v = dh_refs[0][...]
        for extra in dh_refs[1:]:
            dh_v = dh_v + extra[...]
        dg_ref[...] += jnp.sum(dh_v * n, axis=0, keepdims=True)
        dx_ref[...] = dr_ref[...] + _rms_bwd(n, r, g_ref[...], dh_v)

    return pl.pallas_call(
        body, name=name, grid=(T // bt,), in_specs=[row, vec] + [row] * (n_dh + 1), out_specs=[row, vec],
        out_shape=[jax.ShapeDtypeStruct((T, Dm), F32), jax.ShapeDtypeStruct((1, Dm), F32)],
        compiler_params=_params(("arbitrary",)),
    )(x, g, *dhs, dres)


def _silu(g):
    return g * _sigmoid(g)


def _gate_bwd(dgated, core, g):
    sg = _sigmoid(g)
    return dgated * (g * sg), dgated * core * (sg * (1.0 + g * (1.0 - sg)))


def _softmax_rows(s):
    e = jnp.exp(s - jnp.max(s, axis=-1, keepdims=True))
    return e * (1.0 / jnp.sum(e, axis=-1, keepdims=True))


def _band_scores(qk, bias, r0):
    s = qk * (HEAD_DIM ** -0.5) + bias
    j = lax.broadcasted_iota(jnp.int32, s.shape, 1)
    return jnp.where(j >= A_PAD - r0, s, NEG_INF)


def _fill_padded_kv(p_ref, kp_ref, vp_ref):
    zeros = jnp.zeros((A_PAD, kp_ref.shape[1]), BF16)
    kp_ref[0:A_PAD, :] = zeros
    vp_ref[0:A_PAD, :] = zeros
    kp_ref[A_PAD:, :] = p_ref[1, 0].astype(BF16)
    vp_ref[A_PAD:, :] = p_ref[2, 0].astype(BF16)


def _head_specs(S, order, heads=1):
    def idx(fn):
        return lambda *ids: fn(**dict(zip(order, ids)))

    return (pl.BlockSpec((4, 1, S, heads * HEAD_DIM), idx(lambda b, h, t: (0, b, 0, h))),
            pl.BlockSpec((1, TQ, heads * HEAD_DIM), idx(lambda b, h, t: (b, t, h))))


def attn_a_fwd(proj, bias, name, heads=1, ride=None):
    _, B, S, W = proj.shape
    nt = S // TQ

    def body(p_ref, b_ref, o_ref, gt_ref, kp_ref, vp_ref):
        t = pl.program_id(2)

        @pl.when(t == 0)
        def _():
            _fill_padded_kv(p_ref, kp_ref, vp_ref)

        r0 = pl.multiple_of(t * TQ, TQ)
        for e in range(heads):
            lanes = slice(e * HEAD_DIM, (e + 1) * HEAD_DIM)
            q = p_ref[0, 0, pl.ds(r0, TQ), lanes].astype(BF16)
            g = p_ref[3, 0, pl.ds(r0, TQ), lanes]
            p = _softmax_rows(_band_scores(_dot_nt(q, kp_ref[pl.ds(r0, A_KW), lanes]), b_ref[e], r0))
            o = _dot_nn(p.astype(BF16), vp_ref[pl.ds(r0, A_KW), lanes])
            o_ref[0, :, lanes] = o
            gt_ref[0, :, lanes] = (o * _silu(g)).astype(BF16)

    seq, tile = _head_specs(S, "bht", heads)
    kv = pltpu.VMEM((A_PAD + S, heads * HEAD_DIM), BF16)
    return _call(
        body, name=name, grid=(B, HEADS // heads, nt),
        in_specs=[seq, pl.BlockSpec((heads, TQ, A_KW), lambda b, h, t: (h, 0, 0))], out_specs=[tile, tile],
        out_shape=[jax.ShapeDtypeStruct((B, S, W), F32), jax.ShapeDtypeStruct((B, S, W), BF16)],
        scratch_shapes=[kv, kv],
        semantics=("parallel", "parallel", "arbitrary"), args=[proj, bias], ride=ride)


def attn_a_bwd(proj, bias, o, dgated, name, heads=1, ride=None):
    _, B, S, W = proj.shape
    nt = S // TQ

    def body(p_ref, b_ref, o_ref, dgt_ref, dp_ref, db_ref, kp_ref, vp_ref, dk_ref, dv_ref):
        b_, t = pl.program_id(1), pl.program_id(2)

        @pl.when(t == 0)
        def _():
            _fill_padded_kv(p_ref, kp_ref, vp_ref)
            dk_ref[...] = jnp.zeros_like(dk_ref)
            dv_ref[...] = jnp.zeros_like(dv_ref)

        @pl.when((t == 0) & (b_ == 0))
        def _():
            db_ref[...] = jnp.zeros_like(db_ref)

        r0 = pl.multiple_of(t * TQ, TQ)
        rows, win = pl.ds(r0, TQ), pl.ds(r0, A_KW)
        for e in range(heads):
            lanes = slice(e * HEAD_DIM, (e + 1) * HEAD_DIM)
            q = p_ref[0, 0, rows, lanes].astype(BF16)
            g = p_ref[3, 0, rows, lanes]
            kw, vw = kp_ref[win, lanes], vp_ref[win, lanes]
            p = _softmax_rows(_band_scores(_dot_nt(q, kw), b_ref[e], r0))
            do, dg = _gate_bwd(dgt_ref[0, :, lanes], o_ref[0, :, lanes], g)
            do = do.astype(BF16)
            dv_ref[win, lanes] += _dot_tn(p.astype(BF16), do)
            dpr = _dot_nt(do, vw)
            ds = p * (dpr - jnp.sum(p * dpr, axis=-1, keepdims=True))
            db_ref[e] += ds
            ds = (ds * (HEAD_DIM ** -0.5)).astype(BF16)
            dk_ref[win, lanes] += _dot_tn(ds, q)
            dp_ref[0, 0, rows, lanes] = _dot_nn(ds, kw).astype(BF16)
            dp_ref[3, 0, rows, lanes] = dg.astype(BF16)

        @pl.when(t == nt - 1)
        def _():
            dp_ref[1, 0] = dk_ref[A_PAD:, :].astype(BF16)
            dp_ref[2, 0] = dv_ref[A_PAD:, :].astype(BF16)

    seq, tile = _head_specs(S, "hbt", heads)
    bias_spec = pl.BlockSpec((heads, TQ, A_KW), lambda h, b, t: (h, 0, 0))
    kv = pltpu.VMEM((A_PAD + S, heads * HEAD_DIM), BF16)
    acc = pltpu.VMEM((A_PAD + S, heads * HEAD_DIM), F32)
    return _call(
        body, name=name, grid=(HEADS // heads, B, nt), in_specs=[seq, bias_spec, tile, tile],
        out_specs=[seq, bias_spec],
        out_shape=[jax.ShapeDtypeStruct(proj.shape, BF16), jax.ShapeDtypeStruct(bias.shape, F32)],
        scratch_shapes=[kv, kv, acc, acc],
        semantics=("arbitrary", "arbitrary", "arbitrary"), args=[proj, bias, o, dgated], ride=ride)


def band_bias(rel_bias):
    length = TQ + A_KW - 1
    first = REL_CLIP + 1 - TQ
    gen = jnp.concatenate([rel_bias[:, first:],
                           jnp.broadcast_to(rel_bias[:, 2 * REL_CLIP:], (HEADS, length - (N_REL - first)))], axis=1)
    rev = jnp.concatenate([gen[:, ::-1], jnp.zeros((HEADS, 1), rel_bias.dtype)], axis=1)
    sheared = jnp.tile(rev, (1, TQ))[:, :TQ * length].reshape(HEADS, TQ, length)
    i = lax.broadcasted_iota(jnp.int32, (TQ, A_KW), 0)
    j = lax.broadcasted_iota(jnp.int32, (TQ, A_KW), 1)
    first_key = (i // CHUNK) * CHUNK
    in_band = (j >= first_key) & (j < first_key + (LEFT_CHUNKS + 1) * CHUNK)
    return jnp.where(in_band, sheared[:, :, TQ - 1:], NEG_INF)


def _group_scan(a, u, carry, reverse=False):
    row = lax.broadcasted_iota(jnp.int32, u.shape, 0)
    for k in (1, 2, 4):
        shift = 8 - k if reverse else k
        valid = (row < 8 - k) if reverse else (row >= k)
        u_sh = pltpu.roll(u, shift, 0)
        if a is None:
            u = jnp.where(valid, u + u_sh, u)
        else:
            a_sh = pltpu.roll(a, shift, 0)
            u = jnp.where(valid, a * u_sh + u, u)
            a = jnp.where(valid, a * a_sh, a)
    return (u + carry) if a is None else (a * carry + u)


SCAN_UNROLL = 4


def _scan_rows(n_rows, step, carry0, reverse=False):
    groups = n_rows // 8

    def loop(i, carry):
        gi = (groups - 1 - i) if reverse else i
        return step(pl.multiple_of(gi * 8, 8), carry)

    return lax.fori_loop(0, groups, loop, carry0, unroll=SCAN_UNROLL)


def fox_cum_fwd(f_logit, f_bias, name):
    B, S, L = f_logit.shape

    def body(f_ref, b_ref, c_ref):
        z = f_ref[0] + b_ref[...]
        c_ref[0] = jnp.minimum(z, 0.0) - _log1p(jnp.exp(-jnp.abs(z)))

        def step(r0, carry):
            h = _group_scan(None, c_ref[0, pl.ds(r0, 8), :], carry)
            c_ref[0, pl.ds(r0, 8), :] = h
            return h[7:8, :]

        _scan_rows(S, step, jnp.zeros((1, L), F32))

    return pl.pallas_call(
        body, name=name, grid=(B,),
        in_specs=[pl.BlockSpec((1, S, L), lambda b: (b, 0, 0)), pl.BlockSpec((1, L), lambda b: (0, 0))],
        out_specs=pl.BlockSpec((1, S, L), lambda b: (b, 0, 0)),
        out_shape=jax.ShapeDtypeStruct((B, S, L), F32), compiler_params=_params(("parallel",)),
    )(f_logit, f_bias)


def fox_cum_bwd(f_logit, f_bias, dcum, name):
    B, S, L = f_logit.shape

    def body(f_ref, b_ref, d_ref, df_ref, db_ref):
        @pl.when(pl.program_id(0) == 0)
        def _():
            db_ref[...] = jnp.zeros_like(db_ref)

        def step(r0, carry):
            h = _group_scan(None, d_ref[0, pl.ds(r0, 8), :], carry, reverse=True)
            df_ref[0, pl.ds(r0, 8), :] = h
            return h[0:1, :]

        _scan_rows(S, step, jnp.zeros((1, L), F32), reverse=True)
        df = df_ref[0] * _sigmoid(-(f_ref[0] + b_ref[...]))
        df_ref[0] = df
        db_ref[...] += jnp.sum(df, axis=0, keepdims=True)

    seq = pl.BlockSpec((1, S, L), lambda b: (b, 0, 0))
    vec = pl.BlockSpec((1, L), lambda b: (0, 0))
    return pl.pallas_call(
        body, name=name, grid=(B,), in_specs=[seq, vec, seq], out_specs=[seq, vec],
        out_shape=[jax.ShapeDtypeStruct((B, S, L), F32), jax.ShapeDtypeStruct((1, L), F32)],
        compiler_params=_params(("arbitrary",)),
    )(f_logit, f_bias, dcum)


def _head_row(cr, h):
    sub = lax.broadcasted_iota(jnp.int32, cr.shape, 0)
    return jnp.sum(jnp.where(sub == h, cr, 0.0), axis=0, keepdims=True)


def _fox_scores(qk, cc, ck, h, r0):
    lane = lax.broadcasted_iota(jnp.int32, cc.shape, 1)
    cq = jnp.sum(jnp.where(lane == h, cc, 0.0), axis=1, keepdims=True)
    s = qk * (HEAD_DIM ** -0.5) + (cq - ck)
    qpos = r0 + lax.broadcasted_iota(jnp.int32, s.shape, 0)
    kpos = lax.broadcasted_iota(jnp.int32, s.shape, 1)
    return jnp.where(kpos <= qpos, s, NEG_INF)


KEY_STEP = 256


def _by_causal_width(t, S, fn):
    per = KEY_STEP // TQ
    for c in range(S // KEY_STEP):
        pl.when(t // per == c)(functools.partial(fn, (c + 1) * KEY_STEP))


def fox_fwd(proj, cum_col, cum_row, name, heads=1, ride=None):
    _, B, S, W = proj.shape
    nt = S // TQ

    def body(p_ref, cc_ref, cr_ref, o_ref, gt_ref, k_ref, v_ref):
        head0, t = pl.program_id(1) * heads, pl.program_id(2)

        @pl.when(t == 0)
        def _():
            k_ref[...] = p_ref[1, 0].astype(BF16)
            v_ref[...] = p_ref[2, 0].astype(BF16)

        r0 = pl.multiple_of(t * TQ, TQ)

        def tile_out(width):
            for e in range(heads):
                h, lanes = head0 + e, slice(e * HEAD_DIM, (e + 1) * HEAD_DIM)
                q = p_ref[0, 0, pl.ds(r0, TQ), lanes].astype(BF16)
                ck = _head_row(cr_ref[0, :, 0:width], h)
                p = _softmax_rows(_fox_scores(_dot_nt(q, k_ref[0:width, lanes]), cc_ref[0], ck, h, r0))
                o = _dot_nn(p.astype(BF16), v_ref[0:width, lanes])
                o_ref[0, :, lanes] = o
                gt_ref[0, :, lanes] = (o * _silu(p_ref[3, 0, pl.ds(r0, TQ), lanes])).astype(BF16)

        _by_causal_width(t, S, tile_out)

    seq, tile = _head_specs(S, "bht", heads)
    kv = pltpu.VMEM((S, heads * HEAD_DIM), BF16)
    return _call(
        body, name=name, grid=(B, HEADS // heads, nt),
        in_specs=[seq, pl.BlockSpec((1, TQ, cum_col.shape[2]), lambda b, h, t: (b, t, 0)),
                  pl.BlockSpec((1, HEADS, S), lambda b, h, t: (b, 0, 0))],
        out_specs=[tile, tile],
        out_shape=[jax.ShapeDtypeStruct((B, S, W), F32), jax.ShapeDtypeStruct((B, S, W), BF16)],
        scratch_shapes=[kv, kv],
        semantics=("parallel", "parallel", "arbitrary"), args=[proj, cum_col, cum_row], ride=ride)


def fox_bwd(proj, cum_col, cum_row, o, dgated, name, heads=1, ride=None):
    _, B, S, W = proj.shape
    nt = S // TQ

    def body(p_ref, cc_ref, cr_ref, o_ref, dgt_ref, dp_ref, dc_ref, k_ref, v_ref, dk_ref, dv_ref):
        head0, t = pl.program_id(1) * heads, pl.program_id(2)

        @pl.when(t == 0)
        def _():
            k_ref[...] = p_ref[1, 0].astype(BF16)
            v_ref[...] = p_ref[2, 0].astype(BF16)
            dk_ref[...] = jnp.zeros_like(dk_ref)
            dv_ref[...] = jnp.zeros_like(dv_ref)
            dc_ref[...] = jnp.zeros_like(dc_ref)

        r0 = pl.multiple_of(t * TQ, TQ)
        rows = pl.ds(r0, TQ)

        def tile_grads(width):
            for e in range(heads):
                h, lanes = head0 + e, slice(e * HEAD_DIM, (e + 1) * HEAD_DIM)
                q = p_ref[0, 0, rows, lanes].astype(BF16)
                do, dg = _gate_bwd(dgt_ref[0, :, lanes], o_ref[0, :, lanes], p_ref[3, 0, rows, lanes])
                do = do.astype(BF16)
                dp_ref[3, 0, rows, lanes] = dg.astype(BF16)
                k, v = k_ref[0:width, lanes], v_ref[0:width, lanes]
                ck = _head_row(cr_ref[0, :, 0:width], h)
                p = _softmax_rows(_fox_scores(_dot_nt(q, k), cc_ref[0], ck, h, r0))
                dv_ref[0:width, lanes] += _dot_tn(p.astype(BF16), do)
                dpr = _dot_nt(do, v)
                ds = p * (dpr - jnp.sum(p * dpr, axis=-1, keepdims=True))
                dc_ref[0, e, :, 0:width] += jnp.sum(ds, axis=0, keepdims=True)
                ds = (ds * (HEAD_DIM ** -0.5)).astype(BF16)
                dk_ref[0:width, lanes] += _dot_tn(ds, q)
                dp_ref[0, 0, rows, lanes] = _dot_nn(ds, k).astype(BF16)

        _by_causal_width(t, S, tile_grads)

        @pl.when(t == nt - 1)
        def _():
            dp_ref[1, 0] = dk_ref[...].astype(BF16)
            dp_ref[2, 0] = dv_ref[...].astype(BF16)

    seq, tile = _head_specs(S, "bht", heads)
    kv = pltpu.VMEM((S, heads * HEAD_DIM), BF16)
    acc = pltpu.VMEM((S, heads * HEAD_DIM), F32)
    return _call(
        body, name=name, grid=(B, HEADS // heads, nt),
        in_specs=[seq, pl.BlockSpec((1, TQ, cum_col.shape[2]), lambda b, h, t: (b, t, 0)),
                  pl.BlockSpec((1, HEADS, S), lambda b, h, t: (b, 0, 0)), tile, tile],
        out_specs=[seq, pl.BlockSpec((1, heads, 1, S), lambda b, h, t: (b, h, 0, 0))],
        out_shape=[jax.ShapeDtypeStruct(proj.shape, BF16), jax.ShapeDtypeStruct((B, HEADS, 1, S), F32)],
        scratch_shapes=[kv, kv, acc, acc],
        semantics=("parallel", "parallel", "arbitrary"), args=[proj, cum_col, cum_row, o, dgated], ride=ride)


RG_ROWS = 512


def _rg_gates(xc, wa_ref, ba_ref, wx_ref, bx_ref, lam_ref):
    xcb = xc.astype(BF16)
    r = _sigmoid(_dot_nn(xcb, wa_ref[0]) + ba_ref[...])
    i = _sigmoid(_dot_nn(xcb, wx_ref[0]) + bx_ref[...])
    sp = _softplus(-lam_ref[...])
    log_a = (-RG_C * sp) * r
    a = jnp.exp(log_a)
    m = jnp.sqrt(-jnp.tanh(log_a) * (a * a + 1.0))
    return xcb, r, i, sp, a, m


def _rg_specs(B, S, rows, order):
    nc = S // rows

    def idx(fn):
        def index_map(*ids):
            v = dict(zip(order.lower(), ids))
            c = (nc - 1 - v["c"]) if "C" in order else v["c"]
            return fn(v["b"], v["d"], c)
        return index_map

    return dict(
        proj=pl.BlockSpec((2, 1, rows, RG_COLS), idx(lambda b, d, c: (0, b, c, d))),
        act=pl.BlockSpec((1, rows, RG_COLS), idx(lambda b, d, c: (b, c, d))),
        taps=pl.BlockSpec((CONV_WIDTH, RG_COLS), idx(lambda b, d, c: (0, d))),
        vec=pl.BlockSpec((1, RG_COLS), idx(lambda b, d, c: (0, d))),
        gate=pl.BlockSpec((1, RG_COLS, RG_COLS), idx(lambda b, d, c: (d, 0, 0))),
    )


def rglru_fwd(proj, conv_w, conv_b, wa, ba, wx, bx, lam, name, rows=RG_ROWS, ride=None):
    _, B, S, _ = proj.shape
    rows = min(rows, S)
    sp_ = _rg_specs(B, S, rows, "bdc")

    def body(p_ref, cw_ref, cb_ref, wa_ref, ba_ref, wx_ref, bx_ref, lam_ref,
             xc_ref, hs_ref, hp_ref, gt_ref, ext_ref, a_ref, u_ref, xcar_ref, hcar_ref):
        @pl.when(pl.program_id(2) == 0)
        def _():
            xcar_ref[...] = jnp.zeros_like(xcar_ref)
            hcar_ref[...] = jnp.zeros_like(hcar_ref)

        xr = p_ref[0, 0]
        ext_ref[0:8, :] = xcar_ref[...]
        ext_ref[8:, :] = xr
        xcar_ref[...] = xr[rows - 8:, :]
        xc = ext_ref[pl.ds(5, rows), :] * cw_ref[0:1, :]
        xc = xc + ext_ref[pl.ds(6, rows), :] * cw_ref[1:2, :]
        xc = xc + ext_ref[pl.ds(7, rows), :] * cw_ref[2:3, :]
        xc = xc + xr * cw_ref[3:4, :] + cb_ref[...]
        xc_ref[0] = xc
        _, _, i, _, a, m = _rg_gates(xc, wa_ref, ba_ref, wx_ref, bx_ref, lam_ref)
        a_ref[...] = a
        u_ref[...] = m * (i * xc)

        def step(r0, carry):
            h = _group_scan(a_ref[pl.ds(r0, 8), :], u_ref[pl.ds(r0, 8), :], carry)
            row = lax.broadcasted_iota(jnp.int32, h.shape, 0)
            hs_ref[0, pl.ds(r0, 8), :] = h
            hp_ref[0, pl.ds(r0, 8), :] = jnp.where(row == 0, carry, pltpu.roll(h, 1, 0))
            return h[7:8, :]

        hcar_ref[0:1, :] = _scan_rows(rows, step, hcar_ref[0:1, :])
        gt_ref[0] = (hs_ref[0] * _silu(p_ref[1, 0])).astype(BF16)

    act = jax.ShapeDtypeStruct((B, S, RG_WIDTH), F32)
    return _call(
        body, name=name, grid=(B, RG_GROUPS, S // rows),
        in_specs=[sp_["proj"], sp_["taps"], sp_["vec"], sp_["gate"], sp_["vec"], sp_["gate"], sp_["vec"], sp_["vec"]],
        out_specs=[sp_["act"]] * 4,
        out_shape=[act, act, act, jax.ShapeDtypeStruct((B, S, RG_WIDTH), BF16)],
        scratch_shapes=[pltpu.VMEM((rows + 8, RG_COLS), F32), pltpu.VMEM((rows, RG_COLS), F32),
                        pltpu.VMEM((rows, RG_COLS), F32), pltpu.VMEM((8, RG_COLS), F32), pltpu.VMEM((8, RG_COLS), F32)],
        semantics=("parallel", "parallel", "arbitrary"), args=[proj, conv_w, conv_b, wa, ba, wx, bx, lam], ride=ride)


def rglru_bwd(proj, xc, hs, hprev, dgated, conv_w, wa, ba, wx, bx, lam, name, rows=RG_ROWS, ride=None):
    _, B, S, _ = proj.shape
    rows = min(rows, S)
    sp_ = _rg_specs(B, S, rows, "dbC")

    def body(p_ref, xc_ref, hs_ref, hp_ref, dgt_ref, cw_ref, wa_ref, ba_ref, wx_ref, bx_ref, lam_ref,
             dp_ref, dcw_ref, dcb_ref, dwa_ref, dba_ref, dwx_ref, dbx_ref, dlam_ref,
             ext_ref, c_ref, l_ref, acar_ref, lcar_ref, dcar_ref):
        b_, c_ = pl.program_id(1), pl.program_id(2)

        @pl.when(c_ == 0)
        def _():
            acar_ref[...] = jnp.zeros_like(acar_ref)
            lcar_ref[...] = jnp.zeros_like(lcar_ref)
            dcar_ref[...] = jnp.zeros_like(dcar_ref)

        @pl.when((c_ == 0) & (b_ == 0))
        def _():
            for ref in (dcw_ref, dcb_ref, dwa_ref, dba_ref, dwx_ref, dbx_ref, dlam_ref):
                ref[...] = jnp.zeros_like(ref)

        xr, g = p_ref[0, 0], p_ref[1, 0]
        xc_v = xc_ref[0]
        xcb, r, i, sp, a, m = _rg_gates(xc_v, wa_ref, ba_ref, wx_ref, bx_ref, lam_ref)
        dhs, dg = _gate_bwd(dgt_ref[0], hs_ref[0], g)
        dp_ref[1, 0] = dg.astype(BF16)

        ext_ref[0:rows, :] = a
        ext_ref[rows:, :] = acar_ref[...]
        acar_ref[...] = a[0:8, :]
        c_ref[...] = ext_ref[pl.ds(1, rows), :]
        l_ref[...] = dhs

        def step(r0, carry):
            lam_g = _group_scan(c_ref[pl.ds(r0, 8), :], l_ref[pl.ds(r0, 8), :], carry, reverse=True)
            l_ref[pl.ds(r0, 8), :] = lam_g
            return lam_g[0:1, :]

        lcar_ref[0:1, :] = _scan_rows(rows, step, lcar_ref[0:1, :], reverse=True)
        du = l_ref[...]
        da = du * hp_ref[0]
        dlog_a = da * a - (du * (i * xc_v)) * (a * a / m)
        dr = dlog_a * (-RG_C * sp)
        dsp = jnp.sum(dlog_a * (-RG_C * r), axis=0, keepdims=True)
        dlam_ref[...] += dsp * (-_sigmoid(-lam_ref[...]))
        dpa = dr * (r * (1.0 - r))
        dpx = (du * (m * xc_v)) * (i * (1.0 - i))
        dba_ref[...] += jnp.sum(dpa, axis=0, keepdims=True)
        dbx_ref[...] += jnp.sum(dpx, axis=0, keepdims=True)
        dpa, dpx = dpa.astype(BF16), dpx.astype(BF16)
        dwa_ref[0] += _dot_tn(xcb, dpa)
        dwx_ref[0] += _dot_tn(xcb, dpx)
        dxc = du * (m * i) + _dot_nt(dpa, wa_ref[0]) + _dot_nt(dpx, wx_ref[0])

        dcb_ref[...] += jnp.sum(dxc, axis=0, keepdims=True)
        ext_ref[0:rows, :] = dxc
        ext_ref[rows:, :] = dcar_ref[...]
        dcar_ref[...] = dxc[0:8, :]
        dxr = jnp.zeros_like(dxc)
        for k in range(CONV_WIDTH):
            tap = CONV_WIDTH - 1 - k
            ahead = dxc if k == 0 else ext_ref[pl.ds(k, rows), :]
            dxr = dxr + ahead * cw_ref[tap:tap + 1, :]
            dcw_ref[tap:tap + 1, :] += jnp.sum(xr * ahead, axis=0, keepdims=True)
        dp_ref[0, 0] = dxr.astype(BF16)

    vec = jax.ShapeDtypeStruct((1, RG_WIDTH), F32)
    gate = jax.ShapeDtypeStruct((RG_GROUPS, RG_COLS, RG_COLS), F32)
    return _call(
        body, name=name, grid=(RG_GROUPS, B, S // rows),
        in_specs=[sp_["proj"], sp_["act"], sp_["act"], sp_["act"], sp_["act"], sp_["taps"],
                  sp_["gate"], sp_["vec"], sp_["gate"], sp_["vec"], sp_["vec"]],
        out_specs=[sp_["proj"], sp_["taps"], sp_["vec"], sp_["gate"], sp_["vec"], sp_["gate"], sp_["vec"], sp_["vec"]],
        out_shape=[jax.ShapeDtypeStruct(proj.shape, BF16), jax.ShapeDtypeStruct((CONV_WIDTH, RG_WIDTH), F32), vec,
                   gate, vec, gate, vec, vec],
        scratch_shapes=[pltpu.VMEM((rows + 8, RG_COLS), F32), pltpu.VMEM((rows, RG_COLS), F32),
                        pltpu.VMEM((rows, RG_COLS), F32), pltpu.VMEM((8, RG_COLS), F32),
                        pltpu.VMEM((8, RG_COLS), F32), pltpu.VMEM((8, RG_COLS), F32)],
        semantics=("arbitrary", "arbitrary", "arbitrary"),
        args=[proj, xc, hs, hprev, dgated, conv_w, wa, ba, wx, bx, lam], ride=ride)


def block_diag_gates(w):
    per = RG_COLS // RG_BLOCK
    w4 = w.reshape(RG_GROUPS, per, RG_BLOCK, RG_BLOCK)
    return jnp.einsum("dipq,ij->dipjq", w4, jnp.eye(per, dtype=w.dtype)).reshape(RG_GROUPS, RG_COLS, RG_COLS)


def block_diag_gates_t(dw):
    per = RG_COLS // RG_BLOCK
    dw6 = dw.reshape(RG_GROUPS, per, RG_BLOCK, per, RG_BLOCK)
    return jnp.stack([dw6[:, i, :, i, :] for i in range(per)], axis=1).reshape(RG_BLOCKS, RG_BLOCK, RG_BLOCK)


def adamw(w, parts, m, v, name, layer=0, prev=None, part_row0=0, row_tile=ROW_TILE):
    L, R, C = w.shape
    n_parts = parts.shape[0]
    br = row_tile if R % row_tile == 0 else R

    def body(w_ref, p_ref, m_ref, v_ref, *refs):
        g_ref, d_ref, nm_ref, nv_ref = refs[-4:]
        g = p_ref[0].astype(F32)
        for k in range(1, n_parts):
            g = g + p_ref[k].astype(F32)
        nm = ADAM_B1 * m_ref[0] + (1.0 - ADAM_B1) * g
        nv = ADAM_B2 * v_ref[0] + (1.0 - ADAM_B2) * (g * g)
        m_hat = nm / (1.0 - ADAM_B1 ** ADAM_STEP)
        v_hat = nv / (1.0 - ADAM_B2 ** ADAM_STEP)
        g_ref[0] = g
        d_ref[0] = -ADAM_LR * (m_hat / (jnp.sqrt(v_hat) + ADAM_EPS) + ADAM_WD * w_ref[0])
        nm_ref[0] = nm
        nv_ref[0] = nv

    slab = pl.BlockSpec((1, br, C), lambda i: (layer, i, 0))
    out = jax.ShapeDtypeStruct((L, R, C), F32)
    carried = [] if prev is None else list(prev)
    return _call(
        body, name=name, grid=(R // br,),
        in_specs=[slab, pl.BlockSpec((n_parts, br, C), lambda i: (0, part_row0 // br + i, 0)), slab, slab]
        + [pl.BlockSpec(memory_space=pl.ANY)] * len(carried),
        out_specs=[slab] * 4, out_shape=[out] * 4, semantics=("parallel",), args=[w, parts, m, v] + carried,
        aliases={4 + k: k for k in range(len(carried))})


def _seq(a, B):
    return a.reshape(a.shape[:-2] + (B, a.shape[-2] // B, a.shape[-1]))


def _flat(a):
    return a.reshape(a.shape[:-3] + (a.shape[-3] * a.shape[-2], a.shape[-1]))


def _tiles(w, which, **default):
    return dict(default, **w.get("tiles", {}).get(which, {}))


def mixer_a_fwd(h, w, B, tag, rides):
    proj = matmul(h, w["w_in"], mode="nn", out_dtype=F32, name=f"{tag}_proj", out_slabs=4,
                  ride=rides.pop(f"{tag}_proj", None), **_tiles(w, "proj"))
    o, gated = attn_a_fwd(_seq(proj, B), w["bias"], f"{tag}_attn", heads=w.get("attn_heads", (1, 1))[0],
                          ride=rides.pop(f"{tag}_attn", None))
    return _flat(gated), dict(proj=proj, o=o)


def mixer_a_bwd(dgated, w, saved, B, tag, rides):
    dproj, dbias = attn_a_bwd(_seq(saved["proj"], B), w["bias"], saved["o"], _seq(dgated, B), f"{tag}_attn_bwd",
                              heads=w.get("attn_heads", (1, 1))[1], ride=rides.pop(f"{tag}_attn_bwd", None))
    return _flat(dproj), dict(bias=dbias)


def mixer_b_fwd(h, w, B, tag, rides):
    proj = matmul(h, w["w_in"], mode="nn", out_dtype=F32, name=f"{tag}_proj", out_slabs=2, bn=RG_COLS,
                  ride=rides.pop(f"{tag}_proj", None))
    xc, hs, hprev, gated = rglru_fwd(_seq(proj, B), w["conv_w"], w["conv_b"], w["wa"], w["ba"], w["wx"], w["bx"],
                                     w["lam"], f"{tag}_rglru", ride=rides.pop(f"{tag}_rglru", None))
    return _flat(gated), dict(proj=proj, xc=xc, hs=hs, hprev=hprev)


def mixer_b_bwd(dgated, w, saved, B, tag, rides):
    dproj, dcw, dcb, dwa, dba, dwx, dbx, dlam = rglru_bwd(
        _seq(saved["proj"], B), saved["xc"], saved["hs"], saved["hprev"], _seq(dgated, B),
        w["conv_w"], w["wa"], w["ba"], w["wx"], w["bx"], w["lam"], f"{tag}_rglru_bwd",
        ride=rides.pop(f"{tag}_rglru_bwd", None))
    return _flat(dproj), dict(conv_w=dcw, conv_b=dcb, wa=dwa, ba=dba, wx=dwx, bx=dbx, lam=dlam)


def mixer_c_fwd(h, w, B, tag, rides):
    proj = matmul(h, w["w_in"], mode="nn", out_dtype=F32, name=f"{tag}_proj", out_slabs=4,
                  ride=rides.pop(f"{tag}_proj", None), **_tiles(w, "proj"))
    f_logit = matmul(h, w["w_f"], mode="nn", out_dtype=F32, name=f"{tag}_fproj")
    cum = fox_cum_fwd(_seq(f_logit, B), w["f_bias"], f"{tag}_cum")
    cum_row = cum[:, :, :HEADS].transpose(0, 2, 1)
    o, gated = fox_fwd(_seq(proj, B), cum, cum_row, f"{tag}_attn", heads=w.get("attn_heads", (1, 1))[0],
                       ride=rides.pop(f"{tag}_attn", None))
    return _flat(gated), dict(proj=proj, o=o, f_logit=f_logit, cum=cum, cum_row=cum_row)


def mixer_c_bwd(dgated, w, saved, B, tag, rides):
    dproj, dck = fox_bwd(_seq(saved["proj"], B), saved["cum"], saved["cum_row"], saved["o"], _seq(dgated, B),
                         f"{tag}_attn_bwd", heads=w.get("attn_heads", (1, 1))[1],
                         ride=rides.pop(f"{tag}_attn_bwd", None))
    S = dck.shape[-1]
    dcum = jnp.pad(-dck.reshape(B, HEADS, S).transpose(0, 2, 1), ((0, 0), (0, 0), (0, HEAD_DIM - HEADS)))
    df, dfb = fox_cum_bwd(_seq(saved["f_logit"], B), w["f_bias"], dcum, f"{tag}_cum_bwd")
    return _flat(dproj), dict(f_bias=dfb, df=_flat(df).astype(BF16))


MIXERS = {"a": (mixer_a_fwd, mixer_a_bwd), "b": (mixer_b_fwd, mixer_b_bwd), "c": (mixer_c_fwd, mixer_c_bwd)}
LAYER_KINDS = "abca"


def local_step(x, target, norm_pre, norm_post, get_layer, rides, on_grads):
    B, S, Dm = x.shape
    n_layers = len(LAYER_KINDS)
    xs = [x.reshape(B * S, Dm)]
    saved, layers = [], []
    h = prenorm_fwd(xs[0], norm_pre[0:1], "l0a_prenorm")
    for li, kind in enumerate(LAYER_KINDS):
        tag = f"l{li}{kind}"
        w = get_layer(li)
        gated, sv = MIXERS[kind][0](h, w, B, tag, rides)
        if callable(w["w_out"]):
            w["w_out"] = w["w_out"]()
        y = matmul(gated, w["w_out"], mode="nn", out_dtype=F32, name=f"{tag}_out", ride=rides.pop(f"{tag}_out", None))
        saved.append(dict(sv, h=h, gated=gated, y=y))
        layers.append(w)
        if li + 1 < n_layers:
            x_new, h = postnorm_prenorm_fwd(xs[-1], y, norm_post[li:li + 1], norm_pre[li + 1:li + 2],
                                            f"{tag}_postnorm")
            xs.append(x_new)
    loss, dx = postnorm_loss(xs[-1], y, norm_post[n_layers - 1:], target.reshape(B * S, Dm), "loss")

    for li in reversed(range(n_layers)):
        kind, w, sv = LAYER_KINDS[li], layers[li], saved[li]
        tag = f"l{li}{kind}"
        dy, dg_post = postnorm_bwd(sv["y"], norm_post[li:li + 1], dx, f"{tag}_postnorm_bwd")
        on_grads(li, "norm_post", dg_post)
        on_grads(li, "w_out", matmul(sv["gated"], dy, mode="tn", out_dtype=BF16, name=f"{tag}_dwout",
                                     ride=rides.pop(f"{tag}_dwout", None)))
        dgated = matmul(dy, w["w_out"], mode="nt", out_dtype=F32, name=f"{tag}_dgated",
                        ride=rides.pop(f"{tag}_dgated", None))
        dproj, gw = MIXERS[kind][1](dgated, w, sv, B, tag, rides)
        df = gw.pop("df", None)
        for name, value in gw.items():
            on_grads(li, name, value)
        parts = w.get("dwin_parts", [(0, 1, 1)])
        for i, m_part in enumerate(parts):
            suffix = f"_{i}" if len(parts) > 1 else ""
            on_grads(li, "w_in" + suffix,
                     matmul(sv["h"], dproj, mode="tn", out_dtype=BF16, name=f"{tag}_dwin{suffix}", m_part=m_part,
                            out_slabs=w["grad_slabs"], ride=rides.pop(f"{tag}_dwin{suffix}", None),
                            **_tiles(w, "dwin")))
        if df is not None:
            on_grads(li, "w_f", matmul(sv["h"], df, mode="tn", out_dtype=BF16, name=f"{tag}_dwf"))
        dhs = [matmul(dproj, w["w_in"], mode="nt", out_dtype=F32, name=f"{tag}_dh",
                      ride=rides.pop(f"{tag}_dh", None), **_tiles(w, "dh"))]
        if df is not None:
            dhs.append(matmul(df, w["w_f"], mode="nt", out_dtype=F32, name=f"{tag}_dhf"))
        dx, dg_pre = prenorm_bwd(xs[li], norm_pre[li:li + 1], dhs, dx, f"{tag}_prenorm_bwd")
        on_grads(li, "norm_pre", dg_pre)
    assert not rides, list(rides)
    return loss, dx.reshape(B, S, Dm)


WEIGHTS = ("norm_pre", "norm_post", "a_w_in", "a_rel_bias", "a_w_out", "b_w_in", "b_conv_w", "b_conv_b",
           "b_gate_a_w", "b_gate_a_b", "b_gate_x_w", "b_gate_x_b", "b_lambda", "b_w_out", "c_w_in", "c_f_bias",
           "c_w_out")
C_SHARD = (4 * D_MODEL + HEADS) // N_DEV


def _rows(gathered):
    return gathered.reshape(gathered.shape[0] * gathered.shape[1], gathered.shape[2])


def layer_a(w_in, w_out, rel_bias):
    return dict(w_in=w_in, w_out=w_out if callable(w_out) else _rows(w_out), bias=band_bias(rel_bias),
                grad_slabs=N_DEV)


def layer_b(w_in, w_out, conv_w, small):
    return dict(
        w_in=w_in, w_out=w_out if callable(w_out) else _rows(w_out), grad_slabs=N_DEV,
        conv_w=conv_w.transpose(1, 0, 2).reshape(CONV_WIDTH, RG_WIDTH),
        conv_b=small["b_conv_b"], lam=small["b_lambda"],
        wa=block_diag_gates(small["b_gate_a_w"][0]).astype(BF16), ba=small["b_gate_a_b"].reshape(1, RG_WIDTH),
        wx=block_diag_gates(small["b_gate_x_w"][0]).astype(BF16), bx=small["b_gate_x_b"].reshape(1, RG_WIDTH))


def layer_c(w_in, w_out, small):
    full = w_in.transpose(1, 0, 2).reshape(D_MODEL, N_DEV * C_SHARD)
    return dict(w_in=full[:, :4 * D_MODEL], w_f=jnp.pad(full[:, 4 * D_MODEL:], ((0, 0), (0, HEAD_DIM - HEADS))),
                w_out=_rows(w_out), grad_slabs=1,
                f_bias=jnp.pad(small["c_f_bias"], ((0, 0), (0, HEAD_DIM - HEADS))))


def c_w_in_blocks(dmain, df):
    full = jnp.concatenate([dmain, df[:, :HEADS].astype(dmain.dtype)], axis=1)
    return full.reshape(D_MODEL, N_DEV, C_SHARD).transpose(1, 0, 2)


def _row_blocks(g):
    return g.reshape(N_DEV, g.shape[0] // N_DEV, g.shape[1])


PACK_LANES = 128
PACK_ALIGN = 8 * PACK_LANES


def pack(parts):
    flat = []
    for p in parts:
        n = p.size
        flat.append(jnp.pad(p.reshape(n), (0, -n % PACK_ALIGN)).reshape(-1, PACK_LANES))
    rows = sum(f.shape[0] for f in flat)
    flat.append(jnp.zeros((-rows % ROW_TILE, PACK_LANES), F32))
    return jnp.concatenate(flat, axis=0)


def unpack(packed, shapes):
    out, row = [], 0
    for shape in shapes:
        n = 1
        for s in shape:
            n *= s
        n_rows = (n + PACK_ALIGN - 1) // PACK_ALIGN * 8
        out.append(packed[row:row + n_rows].reshape(-1)[:n].reshape(shape))
        row += n_rows
    return out


LATE = (("a_rel_bias", slice(0, 1)), ("norm_pre", slice(0, 2)), ("norm_post", slice(0, 1)))
EARLY = (("a_rel_bias", slice(1, 2)), ("norm_pre", slice(2, 4)), ("norm_post", slice(1, 4)),
         ("b_conv_b", slice(None)), ("b_gate_a_w", slice(None)), ("b_gate_a_b", slice(None)),
         ("b_gate_x_w", slice(None)), ("b_gate_x_b", slice(None)), ("b_lambda", slice(None)),
         ("c_f_bias", slice(None)))


def _pieces(tree, pieces):
    return [tree[name][sl] for name, sl in pieces]


def kernel(x, norm_pre, norm_post, a_w_in, a_rel_bias, a_w_out, b_w_in, b_conv_w, b_conv_b, b_gate_a_w, b_gate_a_b, b_gate_x_w, b_gate_x_b, b_lambda, b_w_out, c_w_in, c_f_bias, c_w_out, loss_target, m_norm_pre, m_norm_post, m_a_w_in, m_a_rel_bias, m_a_w_out, m_b_w_in, m_b_conv_w, m_b_conv_b, m_b_gate_a_w, m_b_gate_a_b, m_b_gate_x_w, m_b_gate_x_b, m_b_lambda, m_b_w_out, m_c_w_in, m_c_f_bias, m_c_w_out, v_norm_pre, v_norm_post, v_a_w_in, v_a_rel_bias, v_a_w_out, v_b_w_in, v_b_conv_w, v_b_conv_b, v_b_gate_a_w, v_b_gate_a_b, v_b_gate_x_w, v_b_gate_x_b, v_b_lambda, v_b_w_out, v_c_w_in, v_c_f_bias, v_c_w_out):
    args = dict(locals())
    w = {n: args[n] for n in WEIGHTS}
    m = {n: args["m_" + n] for n in WEIGHTS}
    v = {n: args["v_" + n] for n in WEIGHTS}

    a_in, a_out = a_w_in.astype(BF16), a_w_out.astype(BF16)
    gather_a0 = Ride([a_in[0]], scatter=False, via_sibling=True)
    in_l0_proj = Ride([b_w_in[0].astype(BF16), b_conv_w[0], a_out[0]], scatter=False, via_sibling=True)
    in_l0_attn = Ride([c_w_in[0].astype(BF16)], scatter=False, via_sibling=True)
    in_l1_proj = Ride([b_w_out[0].astype(BF16)], scatter=False)
    in_l1_rglru = Ride([c_w_out[0].astype(BF16)], scatter=False)
    in_l2_proj = Ride([a_out[1]], scatter=False)
    in_l2_attn = Ride([a_in[1]], scatter=False, via_sibling=True)
    exchange(gather_a0, "gather_l0")
    rides = {"l0a_proj": in_l0_proj, "l0a_attn": in_l0_attn, "l1b_proj": in_l1_proj, "l1b_rglru": in_l1_rglru,
             "l2c_proj": in_l2_proj, "l2c_attn": in_l2_attn}

    def get_layer(li):
        if li == 0:
            return dict(layer_a(gather_a0.out[0], lambda: _rows(in_l0_proj.out[2]), a_rel_bias[0]),
                        dwin_parts=[(0, 1, 4), (1, 1, 4), (2, 2, 4)], attn_heads=(4, 2))
        if li == 1:
            return layer_b(in_l0_proj.out[0], lambda: _rows(in_l1_proj.out[0]), in_l0_proj.out[1], w)
        if li == 2:
            return dict(layer_c(in_l0_attn.out[0], in_l1_rglru.out[0], w), tiles=dict(proj=dict(bn=2048)),
                        attn_heads=(4, 2))
        return dict(layer_a(in_l2_attn.out[0], in_l2_proj.out[0], a_rel_bias[1]), attn_heads=(4, 2))

    grads = [dict() for _ in LAYER_KINDS]
    scatters = {}

    def rel_bias_grad(j, dbias):
        return jax.vjp(band_bias, a_rel_bias[j])[1](dbias)[0][None]

    def early_partial():
        gb, gc = grads[1], grads[2]
        tree = dict(
            a_rel_bias=jnp.concatenate([jnp.zeros((1, HEADS, N_REL), F32), rel_bias_grad(1, grads[3]["bias"])]),
            norm_pre=jnp.concatenate([jnp.zeros((2, D_MODEL), F32)] + [grads[li]["norm_pre"] for li in (2, 3)]),
            norm_post=jnp.concatenate([jnp.zeros((1, D_MODEL), F32)] + [grads[li]["norm_post"] for li in (1, 2, 3)]),
            b_conv_b=gb["conv_b"], b_lambda=gb["lam"],
            b_gate_a_w=block_diag_gates_t(gb["wa"])[None], b_gate_a_b=gb["ba"].reshape(1, RG_BLOCKS, RG_BLOCK),
            b_gate_x_w=block_diag_gates_t(gb["wx"])[None], b_gate_x_b=gb["bx"].reshape(1, RG_BLOCKS, RG_BLOCK),
            c_f_bias=gc["f_bias"][:, :HEADS])
        return pack(_pieces(tree, EARLY))

    def send(key, host, blocks, scatter=True, via_sibling=False):
        ride = rides.setdefault(host, Ride([], scatter, via_sibling))
        assert (ride.scatter, ride.via_sibling) == (scatter, via_sibling)
        scatters[key] = (ride, len(ride.arrs))
        ride.arrs.append(blocks)

    def on_grads(li, name, value):
        g = grads[li]
        g[name] = value
        if (li, name) == (3, "w_out"):
            send("a1_out", "l3a_attn_bwd", _row_blocks(value))
        elif (li, name) == (3, "w_in"):
            send("a1_in", "l2c_attn_bwd", value)
        elif (li, name) == (2, "w_out"):
            send("c_out", "l1b_dgated", _row_blocks(value))
        elif (li, name) == (2, "w_f"):
            blocks = c_w_in_blocks(g["w_in"], value)
            send("c_in_0", "l2c_dh", blocks[:, :D_MODEL // 2])
            send("c_in_1", "l1b_rglru_bwd", blocks[:, D_MODEL // 2:])
        elif (li, name) == (1, "w_out"):
            send("b_out", "l1b_dh", _row_blocks(value))
        elif (li, name) == (1, "w_in"):
            send("b_in", "l0a_attn_bwd", value)
            send("b_conv", "l0a_attn_bwd",
                 g["conv_w"].reshape(CONV_WIDTH, N_DEV, RG_WIDTH // N_DEV).transpose(1, 0, 2))
        elif (li, name) == (1, "lam"):
            send("early", "l1b_dwin", early_partial(), scatter=False, via_sibling=True)
        elif (li, name) == (0, "w_out"):
            send("a0_out", "l0a_attn_bwd", _row_blocks(value))
        elif (li, name) == (0, "w_in_0"):
            send("a0_in_0", "l0a_dwin_1", value)
        elif (li, name) == (0, "w_in_1"):
            send("a0_in_1", "l0a_dwin_2", value)
        elif (li, name) == (0, "w_in_2"):
            send("a0_in_2", "l0a_dh", value)

    loss, grad_x = local_step(x, loss_target, norm_pre, norm_post, get_layer, rides, on_grads)
    late_tree = dict(a_rel_bias=rel_bias_grad(0, grads[0]["bias"]), norm_post=grads[0]["norm_post"],
                     norm_pre=jnp.concatenate([grads[0]["norm_pre"], grads[1]["norm_pre"]]))
    late_parts = exchange(Ride([pack([late_tree[n] for n, _ in LATE])], scatter=False), "gather_late_grads")[0]

    def sharded(name, slab_parts):
        shape = w[name].shape
        slabs = (len(slab_parts), shape[0] * shape[1] // len(slab_parts), shape[2])
        outs = None
        for j, (parts, row0) in enumerate(slab_parts):
            outs = adamw(w[name].reshape(slabs), parts, m[name].reshape(slabs), v[name].reshape(slabs),
                         f"adamw_{name}_{j}", layer=j, prev=outs, part_row0=row0)
        return [o.reshape(shape) for o in outs]

    def received(key):
        ride, position = scatters[key]
        return ride.out[position]

    res = dict(
        a_w_in=sharded("a_w_in", [(received("a0_in_0"), 0), (received("a0_in_1"), 0), (received("a0_in_2"), 0),
                                  (received("a0_in_2"), D_MODEL // 4)]
                       + [(received("a1_in"), q * D_MODEL // 4) for q in range(4)]),
        a_w_out=sharded("a_w_out", [(received("a0_out"), 0), (received("a1_out"), 0)]),
        b_w_in=sharded("b_w_in", [(received("b_in"), 0)]),
        b_w_out=sharded("b_w_out", [(received("b_out"), 0)]),
        b_conv_w=sharded("b_conv_w", [(received("b_conv"), 0)]),
        c_w_in=sharded("c_w_in", [(received("c_in_0"), 0), (received("c_in_1"), 0)]),
        c_w_out=sharded("c_w_out", [(received("c_out"), 0)]))

    packed = {}
    for label, pieces, parts in (("early", EARLY, received("early")), ("late", LATE, late_parts)):
        outs = adamw(pack(_pieces(w, pieces))[None], parts, pack(_pieces(m, pieces))[None],
                     pack(_pieces(v, pieces))[None], f"adamw_replicated_{label}")
        shapes = [w[n][sl].shape for n, sl in pieces]
        packed[label] = [dict(zip([n for n, _ in pieces], unpack(o[0], shapes))) for o in outs]
    for n in ("b_conv_b", "b_gate_a_w", "b_gate_a_b", "b_gate_x_w", "b_gate_x_b", "b_lambda", "c_f_bias"):
        res[n] = [packed["early"][k][n] for k in range(4)]
    for n in ("a_rel_bias", "norm_pre", "norm_post"):
        res[n] = [jnp.concatenate([packed["late"][k][n], packed["early"][k][n]]) for k in range(4)]

    total = lax.psum(loss[0, 0], ("x", "y", "c"))
    return (total, grad_x, *[res[n][0] for n in WEIGHTS], *[res[n][1] for n in WEIGHTS],
            *[res[n][2] for n in WEIGHTS], *[res[n][3] for n in WEIGHTS])
```

```python
import functools

import jax
import jax.numpy as jnp
from jax import lax
from jax.experimental import pallas as pl
from jax.experimental.pallas import tpu as pltpu

F32 = jnp.float32
BF16 = jnp.bfloat16

N_DEV = 8
D_MODEL = 2048
HEADS = 16
HEAD_DIM = 128
CHUNK = 64
LEFT_CHUNKS = 8
REL_CLIP = 256
N_REL = 2 * REL_CLIP + 1
TQ = 256
A_PAD = LEFT_CHUNKS * CHUNK
A_KW = A_PAD + TQ
RG_WIDTH = 2560
RG_BLOCKS = 16
RG_BLOCK = 160
RG_COLS = 640
RG_GROUPS = RG_WIDTH // RG_COLS
RG_C = 8.0
CONV_WIDTH = 4
RMS_EPS = 1e-6
NEG_INF = -1e30
ADAM_LR = 0.001
ADAM_B1 = 0.9
ADAM_B2 = 0.999
ADAM_EPS = 1e-08
ADAM_WD = 0.01
ADAM_STEP = 10
VMEM_LIMIT = 56 * 1024 * 1024
MESH = pl.DeviceIdType.MESH


def _params(sem, vmem=VMEM_LIMIT):
    return pltpu.CompilerParams(dimension_semantics=sem, vmem_limit_bytes=vmem)


def _sigmoid(x):
    return 1.0 / (1.0 + jnp.exp(-x))


def _log1p(y):
    u = 1.0 + y
    return jnp.where(u == 1.0, y, jnp.log(u) * (y / jnp.where(u == 1.0, 1.0, u - 1.0)))


def _softplus(x):
    return jnp.maximum(x, 0.0) + _log1p(jnp.exp(-jnp.abs(x)))


def _dot(a, b, dims):
    return lax.dot_general(a, b, (dims, ((), ())), preferred_element_type=F32)


def _dot_nn(a, b):
    return _dot(a, b, ((1,), (0,)))


def _dot_nt(a, b):
    return _dot(a, b, ((1,), (1,)))


def _dot_tn(a, b):
    return _dot(a, b, ((0,), (0,)))


def _peers():
    x, y, c = lax.axis_index("x"), lax.axis_index("y"), lax.axis_index("c")
    me = 4 * x + 2 * y + c
    peers = []
    for k in range(1, N_DEV):
        px = 1 - x if k & 4 else x
        py = 1 - y if k & 2 else y
        pc = 1 - c if k & 1 else c
        peers.append(((px, py, pc), 4 * px + 2 * py + pc))
    return me, peers


class Ride:
    def __init__(self, arrs, scatter, via_sibling=False):
        assert not (scatter and via_sibling)
        self.arrs, self.scatter, self.via_sibling, self.out = list(arrs), scatter, via_sibling, None

    def out_shapes(self):
        return [jax.ShapeDtypeStruct(a.shape if self.scatter else (N_DEV,) + a.shape, a.dtype) for a in self.arrs]

    def sem_shapes(self):
        n = len(self.arrs)
        return [pltpu.SemaphoreType.DMA((n, N_DEV - 1)), pltpu.SemaphoreType.DMA((n, N_DEV - 1)),
                pltpu.SemaphoreType.DMA((n,))]

    def _copies(self, ins, outs, sems, landing):
        send_sems, recv_sems, local_sems = sems
        me, peers = _peers()
        local, remote = [], []
        for a, (src, dst) in enumerate(zip(ins, outs)):
            local.append(pltpu.make_async_copy(src.at[me] if self.scatter else src, dst.at[me], local_sems.at[a]))
            for k, (peer, peer_idx) in enumerate(peers):
                remote.append(pltpu.make_async_remote_copy(
                    src_ref=src.at[peer_idx] if self.scatter else src, dst_ref=dst.at[peer_idx if landing else me],
                    send_sem=send_sems.at[a, k], recv_sem=recv_sems.at[a, k], device_id=peer, device_id_type=MESH))
        return local, remote

    def _direct(self, k):
        return not self.via_sibling or k == 0 or (k + 1) % 2 == 0

    def start(self, ins, outs, sems):
        local, remote = self._copies(ins, outs, sems, landing=False)
        n_peers = N_DEV - 1
        for cp in local + [cp for i, cp in enumerate(remote) if self._direct(i % n_peers)]:
            cp.start()

    def wait(self, ins, outs, sems):
        local, remote = self._copies(ins, outs, sems, landing=True)
        n_peers = N_DEV - 1
        for i, cp in enumerate(remote):
            if self._direct(i % n_peers):
                cp.wait()
        if self.via_sibling:
            send_sems, recv_sems, _ = sems
            me, peers = _peers()
            sibling = peers[0][0]
            passed = []
            for a, dst in enumerate(outs):
                for j in range(1, n_peers, 2):
                    came, lands = peers[j][1], peers[j + 1][1]
                    pltpu.make_async_remote_copy(
                        src_ref=dst.at[came], dst_ref=dst.at[came], send_sem=send_sems.at[a, j + 1],
                        recv_sem=recv_sems.at[a, j + 1], device_id=sibling, device_id_type=MESH).start()
                    passed.append(pltpu.make_async_remote_copy(
                        src_ref=dst.at[came], dst_ref=dst.at[lands], send_sem=send_sems.at[a, j + 1],
                        recv_sem=recv_sems.at[a, j + 1], device_id=sibling, device_id_type=MESH))
            for cp in passed:
                cp.wait()
        for cp in local:
            cp.wait()


def _call(body, *, name, grid, in_specs, out_specs, out_shape, args, scratch_shapes=(), semantics=None, ride=None,
          aliases=None):
    scratch_shapes = list(scratch_shapes)
    if ride is None:
        return pl.pallas_call(
            body, name=name, grid=grid, in_specs=in_specs, out_specs=out_specs, out_shape=out_shape,
            scratch_shapes=scratch_shapes, input_output_aliases=aliases or {},
            compiler_params=_params(semantics if grid else None))(*args)
    assert not aliases
    n_in, n_out, n_sc, n_r = len(in_specs), len(out_specs), len(scratch_shapes), len(ride.arrs)

    def riding(*refs):
        ins, r_ins = refs[:n_in], refs[n_in:n_in + n_r]
        outs, r_outs = refs[n_in + n_r:n_in + n_r + n_out], refs[n_in + n_r + n_out:n_in + 2 * n_r + n_out]
        rest = refs[n_in + 2 * n_r + n_out:]
        scratch, sems = rest[:n_sc], rest[n_sc:]
        first = last = None
        for axis, size in enumerate(grid):
            pid = pl.program_id(axis)
            first = (pid == 0) if first is None else first & (pid == 0)
            last = (pid == size - 1) if last is None else last & (pid == size - 1)
        if grid:
            pl.when(first)(lambda: ride.start(r_ins, r_outs, sems))
        else:
            ride.start(r_ins, r_outs, sems)
        body(*ins, *outs, *scratch)
        if grid:
            pl.when(last)(lambda: ride.wait(r_ins, r_outs, sems))
        else:
            ride.wait(r_ins, r_outs, sems)

    any_spec = pl.BlockSpec(memory_space=pl.ANY)
    res = pl.pallas_call(
        riding, name=name, grid=grid, in_specs=list(in_specs) + [any_spec] * n_r,
        out_specs=list(out_specs) + [any_spec] * n_r, out_shape=list(out_shape) + ride.out_shapes(),
        scratch_shapes=scratch_shapes + ride.sem_shapes(),
        compiler_params=_params(("arbitrary",) * len(grid) if grid else None))(*args, *ride.arrs)
    ride.out = list(res[n_out:])
    return list(res[:n_out])


def exchange(ride, name):
    _call(lambda: None, name=name, grid=(), in_specs=[], out_specs=[], out_shape=[], args=[], ride=ride)
    return ride.out


LANES = 128


def _fit(dims, want):
    dims = tuple(dims)
    if len(set(dims)) == 1 and dims[0] <= want:
        return dims[0]
    return max(t for t in range(LANES, want + 1, LANES) if all(d % t == 0 for d in dims))


def _cols(arr):
    return arr.shape[-1] * (arr.shape[0] if len(arr.shape) == 3 else 1)


def _tile_spec(shape, rblk, cblk, rc):
    if len(shape) == 2:
        return pl.BlockSpec((rblk, cblk), rc)
    per = shape[2] // cblk

    def index_map(*ids):
        r, c = rc(*ids)
        return (c // per, r, c % per)

    return pl.BlockSpec((1, rblk, cblk), index_map)


def matmul(a, b, *, mode, out_dtype, name, bm=1024, bn=1024, bk=2048, out_slabs=1, m_part=(0, 1, 1), ride=None):
    a_rows, a_cols, b_rows, b_cols = a.shape[-2], _cols(a), b.shape[-2], _cols(b)
    (K, M) = (a_rows, a_cols) if mode == "tn" else (a_cols, a_rows)
    N = b_rows if mode == "nt" else b_cols
    assert K == (b_cols if mode == "nt" else b_rows), (name, a.shape, b.shape)
    first_range, n_ranges, of_ranges = m_part
    row0, M = first_range * (M // of_ranges), n_ranges * (M // of_ranges)
    out_shape = (M, N) if out_slabs == 1 else (out_slabs, M, N // out_slabs)
    widths = dict(m=[M], n=[N, out_shape[-1]], k=[K])
    widths["m" if mode == "tn" else "k"].append(a.shape[-1])
    widths["k" if mode == "nt" else "n"].append(b.shape[-1])
    bm, bn, bk = _fit(widths["m"], bm), _fit(widths["n"], bn), _fit(widths["k"], bk)
    nk = K // bk
    dims = {"nn": ((1,), (0,)), "nt": ((1,), (1,)), "tn": ((0,), (0,))}[mode]

    def val(ref):
        return ref[0] if len(ref.shape) == 3 else ref[...]

    def put(ref, x):
        if len(ref.shape) == 3:
            ref[0] = x.astype(ref.dtype)
        else:
            ref[...] = x.astype(ref.dtype)

    def body(a_ref, b_ref, o_ref, *scratch):
        if nk == 1:
            put(o_ref, _dot(val(a_ref), val(b_ref), dims))
            return
        acc_ref, = scratch
        k = pl.program_id(2)

        @pl.when(k == 0)
        def _():
            acc_ref[...] = jnp.zeros_like(acc_ref)

        acc_ref[...] += _dot(val(a_ref), val(b_ref), dims)

        @pl.when(k == nk - 1)
        def _():
            put(o_ref, acc_ref[...])

    assert row0 % bm == 0
    m0 = row0 // bm
    if mode == "tn":
        a_spec = _tile_spec(a.shape, bk, bm, lambda j, i, k: (k, m0 + i))
    else:
        a_spec = _tile_spec(a.shape, bm, bk, lambda j, i, k: (m0 + i, k))
    if mode == "nt":
        b_spec = _tile_spec(b.shape, bn, bk, lambda j, i, k: (j, k))
    else:
        b_spec = _tile_spec(b.shape, bk, bn, lambda j, i, k: (k, j))
    return _call(
        body, name=name, grid=(N // bn, M // bm, nk), in_specs=[a_spec, b_spec],
        out_specs=[_tile_spec(out_shape, bm, bn, lambda j, i, k: (i, j))],
        out_shape=[jax.ShapeDtypeStruct(out_shape, out_dtype)],
        scratch_shapes=[] if nk == 1 else [pltpu.VMEM((bm, bn), F32)],
        semantics=("parallel", "parallel", "arbitrary"), args=[a, b], ride=ride)[0]


ROW_TILE = 256


def _rms_stats(z):
    r = lax.rsqrt(jnp.mean(z * z, axis=-1, keepdims=True) + RMS_EPS)
    return r, z * r


def _rms_bwd(n, r, g, dout):
    dn = dout * g
    return r * (dn - n * jnp.mean(dn * n, axis=-1, keepdims=True))


def _row_spec(T, Dm):
    bt = min(ROW_TILE, T)
    return bt, pl.BlockSpec((bt, Dm), lambda i: (i, 0)), pl.BlockSpec((1, Dm), lambda i: (0, 0))


def prenorm_fwd(x, g, name):
    T, Dm = x.shape
    bt, row, vec = _row_spec(T, Dm)

    def body(x_ref, g_ref, h_ref):
        _, n = _rms_stats(x_ref[...])
        h_ref[...] = (n * g_ref[...]).astype(BF16)

    return pl.pallas_call(
        body, name=name, grid=(T // bt,), in_specs=[row, vec], out_specs=row,
        out_shape=jax.ShapeDtypeStruct((T, Dm), BF16), compiler_params=_params(("parallel",)),
    )(x, g)


def postnorm_prenorm_fwd(x, y, g_post, g_next, name):
    T, Dm = x.shape
    bt, row, vec = _row_spec(T, Dm)

    def body(x_ref, y_ref, gp_ref, gn_ref, o_ref, h_ref):
        _, n = _rms_stats(y_ref[...])
        x_new = x_ref[...] + n * gp_ref[...]
        o_ref[...] = x_new
        _, n_new = _rms_stats(x_new)
        h_ref[...] = (n_new * gn_ref[...]).astype(BF16)

    return pl.pallas_call(
        body, name=name, grid=(T // bt,), in_specs=[row, row, vec, vec], out_specs=[row, row],
        out_shape=[jax.ShapeDtypeStruct((T, Dm), F32), jax.ShapeDtypeStruct((T, Dm), BF16)],
        compiler_params=_params(("parallel",)),
    )(x, y, g_post, g_next)


def postnorm_loss(x, y, g, target, name):
    T, Dm = x.shape
    bt, row, vec = _row_spec(T, Dm)

    def body(x_ref, y_ref, g_ref, t_ref, l_ref, d_ref):
        @pl.when(pl.program_id(0) == 0)
        def _():
            l_ref[...] = jnp.zeros_like(l_ref)

        _, n = _rms_stats(y_ref[...])
        err = (x_ref[...] + n * g_ref[...]) - t_ref[...]
        per_tok = jnp.mean(err * err, axis=-1, keepdims=True)
        l_ref[...] += 0.5 * jnp.sum(per_tok, axis=0, keepdims=True)
        d_ref[...] = err * (1.0 / Dm)

    return pl.pallas_call(
        body, name=name, grid=(T // bt,), in_specs=[row, row, vec, row],
        out_specs=[pl.BlockSpec((1, 1), lambda i: (0, 0)), row],
        out_shape=[jax.ShapeDtypeStruct((1, 1), F32), jax.ShapeDtypeStruct((T, Dm), F32)],
        compiler_params=_params(("arbitrary",)),
    )(x, y, g, target)


def postnorm_bwd(y, g, dout, name):
    T, Dm = y.shape
    bt, row, vec = _row_spec(T, Dm)

    def body(y_ref, g_ref, d_ref, dy_ref, dg_ref):
        @pl.when(pl.program_id(0) == 0)
        def _():
            dg_ref[...] = jnp.zeros_like(dg_ref)

        r, n = _rms_stats(y_ref[...])
        dout_v = d_ref[...]
        dg_ref[...] += jnp.sum(dout_v * n, axis=0, keepdims=True)
        dy_ref[...] = _rms_bwd(n, r, g_ref[...], dout_v).astype(BF16)

    return pl.pallas_call(
        body, name=name, grid=(T // bt,), in_specs=[row, vec, row], out_specs=[row, vec],
        out_shape=[jax.ShapeDtypeStruct((T, Dm), BF16), jax.ShapeDtypeStruct((1, Dm), F32)],
        compiler_params=_params(("arbitrary",)),
    )(y, g, dout)


def prenorm_bwd(x, g, dhs, dres, name):
    T, Dm = x.shape
    bt, row, vec = _row_spec(T, Dm)
    n_dh = len(dhs)

    def body(x_ref, g_ref, *refs):
        dh_refs, (dr_ref, dx_ref, dg_ref) = refs[:n_dh], refs[n_dh:]

        @pl.when(pl.program_id(0) == 0)
        def _():
            dg_ref[...] = jnp.zeros_like(dg_ref)

        r, n = _rms_stats(x_ref[...])
        dh_v = dh_refs[0][...]
        for extra in dh_refs[1:]:
            dh_v = dh_v + extra[...]
        dg_ref[...] += jnp.sum(dh_v * n, axis=0, keepdims=True)
        dx_ref[...] = dr_ref[...] + _rms_bwd(n, r, g_ref[...], dh_v)

    return pl.pallas_call(
        body, name=name, grid=(T // bt,), in_specs=[row, vec] + [row] * (n_dh + 1), out_specs=[row, vec],
        out_shape=[jax.ShapeDtypeStruct((T, Dm), F32), jax.ShapeDtypeStruct((1, Dm), F32)],
        compiler_params=_params(("arbitrary",)),
    )(x, g, *dhs, dres)


def _silu(g):
    return g * _sigmoid(g)


def _gate_bwd(dgated, core, g):
    sg = _sigmoid(g)
    return dgated * (g * sg), dgated * core * (sg * (1.0 + g * (1.0 - sg)))


def _softmax_rows(s):
    e = jnp.exp(s - jnp.max(s, axis=-1, keepdims=True))
    return e * (1.0 / jnp.sum(e, axis=-1, keepdims=True))


def _band_scores(qk, bias, r0):
    s = qk * (HEAD_DIM ** -0.5) + bias
    j = lax.broadcasted_iota(jnp.int32, s.shape, 1)
    return jnp.where(j >= A_PAD - r0, s, NEG_INF)


def _fill_padded_kv(p_ref, kp_ref, vp_ref):
    zeros = jnp.zeros((A_PAD, kp_ref.shape[1]), BF16)
    kp_ref[0:A_PAD, :] = zeros
    vp_ref[0:A_PAD, :] = zeros
    kp_ref[A_PAD:, :] = p_ref[1, 0].astype(BF16)
    vp_ref[A_PAD:, :] = p_ref[2, 0].astype(BF16)


def _head_specs(S, order, heads=1):
    def idx(fn):
        return lambda *ids: fn(**dict(zip(order, ids)))

    return (pl.BlockSpec((4, 1, S, heads * HEAD_DIM), idx(lambda b, h, t: (0, b, 0, h))),
            pl.BlockSpec((1, TQ, heads * HEAD_DIM), idx(lambda b, h, t: (b, t, h))))


def attn_a_fwd(proj, bias, name, heads=1, ride=None):
    _, B, S, W = proj.shape
    nt = S // TQ

    def body(p_ref, b_ref, o_ref, gt_ref, kp_ref, vp_ref):
        t = pl.program_id(2)

        @pl.when(t == 0)
        def _():
            _fill_padded_kv(p_ref, kp_ref, vp_ref)

        r0 = pl.multiple_of(t * TQ, TQ)
        for e in range(heads):
            lanes = slice(e * HEAD_DIM, (e + 1) * HEAD_DIM)
            q = p_ref[0, 0, pl.ds(r0, TQ), lanes].astype(BF16)
            g = p_ref[3, 0, pl.ds(r0, TQ), lanes]
            p = _softmax_rows(_band_scores(_dot_nt(q, kp_ref[pl.ds(r0, A_KW), lanes]), b_ref[e], r0))
            o = _dot_nn(p.astype(BF16), vp_ref[pl.ds(r0, A_KW), lanes])
            o_ref[0, :, lanes] = o
            gt_ref[0, :, lanes] = (o * _silu(g)).astype(BF16)

    seq, tile = _head_specs(S, "bht", heads)
    kv = pltpu.VMEM((A_PAD + S, heads * HEAD_DIM), BF16)
    return _call(
        body, name=name, grid=(B, HEADS // heads, nt),
        in_specs=[seq, pl.BlockSpec((heads, TQ, A_KW), lambda b, h, t: (h, 0, 0))], out_specs=[tile, tile],
        out_shape=[jax.ShapeDtypeStruct((B, S, W), F32), jax.ShapeDtypeStruct((B, S, W), BF16)],
        scratch_shapes=[kv, kv],
        semantics=("parallel", "parallel", "arbitrary"), args=[proj, bias], ride=ride)


def attn_a_bwd(proj, bias, o, dgated, name, heads=1, ride=None):
    _, B, S, W = proj.shape
    nt = S // TQ

    def body(p_ref, b_ref, o_ref, dgt_ref, dp_ref, db_ref, kp_ref, vp_ref, dk_ref, dv_ref):
        b_, t = pl.program_id(1), pl.program_id(2)

        @pl.when(t == 0)
        def _():
            _fill_padded_kv(p_ref, kp_ref, vp_ref)
            dk_ref[...] = jnp.zeros_like(dk_ref)
            dv_ref[...] = jnp.zeros_like(dv_ref)

        @pl.when((t == 0) & (b_ == 0))
        def _():
            db_ref[...] = jnp.zeros_like(db_ref)

        r0 = pl.multiple_of(t * TQ, TQ)
        rows, win = pl.ds(r0, TQ), pl.ds(r0, A_KW)
        for e in range(heads):
            lanes = slice(e * HEAD_DIM, (e + 1) * HEAD_DIM)
            q = p_ref[0, 0, rows, lanes].astype(BF16)
            g = p_ref[3, 0, rows, lanes]
            kw, vw = kp_ref[win, lanes], vp_ref[win, lanes]
            p = _softmax_rows(_band_scores(_dot_nt(q, kw), b_ref[e], r0))
            do, dg = _gate_bwd(dgt_ref[0, :, lanes], o_ref[0, :, lanes], g)
            do = do.astype(BF16)
            dv_ref[win, lanes] += _dot_tn(p.astype(BF16), do)
            dpr = _dot_nt(do, vw)
            ds = p * (dpr - jnp.sum(p * dpr, axis=-1, keepdims=True))
            db_ref[e] += ds
            ds = (ds * (HEAD_DIM ** -0.5)).astype(BF16)
            dk_ref[win, lanes] += _dot_tn(ds, q)
            dp_ref[0, 0, rows, lanes] = _dot_nn(ds, kw).astype(BF16)
            dp_ref[3, 0, rows, lanes] = dg.astype(BF16)

        @pl.when(t == nt - 1)
        def _():
            dp_ref[1, 0] = dk_ref[A_PAD:, :].astype(BF16)
            dp_ref[2, 0] = dv_ref[A_PAD:, :].astype(BF16)

    seq, tile = _head_specs(S, "hbt", heads)
    bias_spec = pl.BlockSpec((heads, TQ, A_KW), lambda h, b, t: (h, 0, 0))
    kv = pltpu.VMEM((A_PAD + S, heads * HEAD_DIM), BF16)
    acc = pltpu.VMEM((A_PAD + S, heads * HEAD_DIM), F32)
    return _call(
        body, name=name, grid=(HEADS // heads, B, nt), in_specs=[seq, bias_spec, tile, tile],
        out_specs=[seq, bias_spec],
        out_shape=[jax.ShapeDtypeStruct(proj.shape, BF16), jax.ShapeDtypeStruct(bias.shape, F32)],
        scratch_shapes=[kv, kv, acc, acc],
        semantics=("arbitrary", "arbitrary", "arbitrary"), args=[proj, bias, o, dgated], ride=ride)


def band_bias(rel_bias):
    length = TQ + A_KW - 1
    first = REL_CLIP + 1 - TQ
    gen = jnp.concatenate([rel_bias[:, first:],
                           jnp.broadcast_to(rel_bias[:, 2 * REL_CLIP:], (HEADS, length - (N_REL - first)))], axis=1)
    rev = jnp.concatenate([gen[:, ::-1], jnp.zeros((HEADS, 1), rel_bias.dtype)], axis=1)
    sheared = jnp.tile(rev, (1, TQ))[:, :TQ * length].reshape(HEADS, TQ, length)
    i = lax.broadcasted_iota(jnp.int32, (TQ, A_KW), 0)
    j = lax.broadcasted_iota(jnp.int32, (TQ, A_KW), 1)
    first_key = (i // CHUNK) * CHUNK
    in_band = (j >= first_key) & (j < first_key + (LEFT_CHUNKS + 1) * CHUNK)
    return jnp.where(in_band, sheared[:, :, TQ - 1:], NEG_INF)


def _group_scan(a, u, carry, reverse=False):
    row = lax.broadcasted_iota(jnp.int32, u.shape, 0)
    for k in (1, 2, 4):
        shift = 8 - k if reverse else k
        valid = (row < 8 - k) if reverse else (row >= k)
        u_sh = pltpu.roll(u, shift, 0)
        if a is None:
            u = jnp.where(valid, u + u_sh, u)
        else:
            a_sh = pltpu.roll(a, shift, 0)
            u = jnp.where(valid, a * u_sh + u, u)
            a = jnp.where(valid, a * a_sh, a)
    return (u + carry) if a is None else (a * carry + u)


SCAN_UNROLL = 4


def _scan_rows(n_rows, step, carry0, reverse=False):
    groups = n_rows // 8

    def loop(i, carry):
        gi = (groups - 1 - i) if reverse else i
        return step(pl.multiple_of(gi * 8, 8), carry)

    return lax.fori_loop(0, groups, loop, carry0, unroll=SCAN_UNROLL)


def fox_cum_fwd(f_logit, f_bias, name):
    B, S, L = f_logit.shape

    def body(f_ref, b_ref, c_ref):
        z = f_ref[0] + b_ref[...]
        c_ref[0] = jnp.minimum(z, 0.0) - _log1p(jnp.exp(-jnp.abs(z)))

        def step(r0, carry):
            h = _group_scan(None, c_ref[0, pl.ds(r0, 8), :], carry)
            c_ref[0, pl.ds(r0, 8), :] = h
            return h[7:8, :]

        _scan_rows(S, step, jnp.zeros((1, L), F32))

    return pl.pallas_call(
        body, name=name, grid=(B,),
        in_specs=[pl.BlockSpec((1, S, L), lambda b: (b, 0, 0)), pl.BlockSpec((1, L), lambda b: (0, 0))],
        out_specs=pl.BlockSpec((1, S, L), lambda b: (b, 0, 0)),
        out_shape=jax.ShapeDtypeStruct((B, S, L), F32), compiler_params=_params(("parallel",)),
    )(f_logit, f_bias)


def fox_cum_bwd(f_logit, f_bias, dcum, name):
    B, S, L = f_logit.shape

    def body(f_ref, b_ref, d_ref, df_ref, db_ref):
        @pl.when(pl.program_id(0) == 0)
        def _():
            db_ref[...] = jnp.zeros_like(db_ref)

        def step(r0, carry):
            h = _group_scan(None, d_ref[0, pl.ds(r0, 8), :], carry, reverse=True)
            df_ref[0, pl.ds(r0, 8), :] = h
            return h[0:1, :]

        _scan_rows(S, step, jnp.zeros((1, L), F32), reverse=True)
        df = df_ref[0] * _sigmoid(-(f_ref[0] + b_ref[...]))
        df_ref[0] = df
        db_ref[...] += jnp.sum(df, axis=0, keepdims=True)

    seq = pl.BlockSpec((1, S, L), lambda b: (b, 0, 0))
    vec = pl.BlockSpec((1, L), lambda b: (0, 0))
    return pl.pallas_call(
        body, name=name, grid=(B,), in_specs=[seq, vec, seq], out_specs=[seq, vec],
        out_shape=[jax.ShapeDtypeStruct((B, S, L), F32), jax.ShapeDtypeStruct((1, L), F32)],
        compiler_params=_params(("arbitrary",)),
    )(f_logit, f_bias, dcum)


def _head_row(cr, h):
    sub = lax.broadcasted_iota(jnp.int32, cr.shape, 0)
    return jnp.sum(jnp.where(sub == h, cr, 0.0), axis=0, keepdims=True)


def _fox_scores(qk, cc, ck, h, r0):
    lane = lax.broadcasted_iota(jnp.int32, cc.shape, 1)
    cq = jnp.sum(jnp.where(lane == h, cc, 0.0), axis=1, keepdims=True)
    s = qk * (HEAD_DIM ** -0.5) + (cq - ck)
    qpos = r0 + lax.broadcasted_iota(jnp.int32, s.shape, 0)
    kpos = lax.broadcasted_iota(jnp.int32, s.shape, 1)
    return jnp.where(kpos <= qpos, s, NEG_INF)


KEY_STEP = 256


def _by_causal_width(t, S, fn):
    per = KEY_STEP // TQ
    for c in range(S // KEY_STEP):
        pl.when(t // per == c)(functools.partial(fn, (c + 1) * KEY_STEP))


def fox_fwd(proj, cum_col, cum_row, name, heads=1, ride=None):
    _, B, S, W = proj.shape
    nt = S // TQ

    def body(p_ref, cc_ref, cr_ref, o_ref, gt_ref, k_ref, v_ref):
        head0, t = pl.program_id(1) * heads, pl.program_id(2)

        @pl.when(t == 0)
        def _():
            k_ref[...] = p_ref[1, 0].astype(BF16)
            v_ref[...] = p_ref[2, 0].astype(BF16)

        r0 = pl.multiple_of(t * TQ, TQ)

        def tile_out(width):
            for e in range(heads):
                h, lanes = head0 + e, slice(e * HEAD_DIM, (e + 1) * HEAD_DIM)
                q = p_ref[0, 0, pl.ds(r0, TQ), lanes].astype(BF16)
                ck = _head_row(cr_ref[0, :, 0:width], h)
                p = _softmax_rows(_fox_scores(_dot_nt(q, k_ref[0:width, lanes]), cc_ref[0], ck, h, r0))
                o = _dot_nn(p.astype(BF16), v_ref[0:width, lanes])
                o_ref[0, :, lanes] = o
                gt_ref[0, :, lanes] = (o * _silu(p_ref[3, 0, pl.ds(r0, TQ), lanes])).astype(BF16)

        _by_causal_width(t, S, tile_out)

    seq, tile = _head_specs(S, "bht", heads)
    kv = pltpu.VMEM((S, heads * HEAD_DIM), BF16)
    return _call(
        body, name=name, grid=(B, HEADS // heads, nt),
        in_specs=[seq, pl.BlockSpec((1, TQ, cum_col.shape[2]), lambda b, h, t: (b, t, 0)),
                  pl.BlockSpec((1, HEADS, S), lambda b, h, t: (b, 0, 0))],
        out_specs=[tile, tile],
        out_shape=[jax.ShapeDtypeStruct((B, S, W), F32), jax.ShapeDtypeStruct((B, S, W), BF16)],
        scratch_shapes=[kv, kv],
        semantics=("parallel", "parallel", "arbitrary"), args=[proj, cum_col, cum_row], ride=ride)


def fox_bwd(proj, cum_col, cum_row, o, dgated, name, heads=1, ride=None):
    _, B, S, W = proj.shape
    nt = S // TQ

    def body(p_ref, cc_ref, cr_ref, o_ref, dgt_ref, dp_ref, dc_ref, k_ref, v_ref, dk_ref, dv_ref):
        head0, t = pl.program_id(1) * heads, pl.program_id(2)

        @pl.when(t == 0)
        def _():
            k_ref[...] = p_ref[1, 0].astype(BF16)
            v_ref[...] = p_ref[2, 0].astype(BF16)
            dk_ref[...] = jnp.zeros_like(dk_ref)
            dv_ref[...] = jnp.zeros_like(dv_ref)
            dc_ref[...] = jnp.zeros_like(dc_ref)

        r0 = pl.multiple_of(t * TQ, TQ)
        rows = pl.ds(r0, TQ)

        def tile_grads(width):
            for e in range(heads):
                h, lanes = head0 + e, slice(e * HEAD_DIM, (e + 1) * HEAD_DIM)
                q = p_ref[0, 0, rows, lanes].astype(BF16)
                do, dg = _gate_bwd(dgt_ref[0, :, lanes], o_ref[0, :, lanes], p_ref[3, 0, rows, lanes])
                do = do.astype(BF16)
                dp_ref[3, 0, rows, lanes] = dg.astype(BF16)
                k, v = k_ref[0:width, lanes], v_ref[0:width, lanes]
                ck = _head_row(cr_ref[0, :, 0:width], h)
                p = _softmax_rows(_fox_scores(_dot_nt(q, k), cc_ref[0], ck, h, r0))
                dv_ref[0:width, lanes] += _dot_tn(p.astype(BF16), do)
                dpr = _dot_nt(do, v)
                ds = p * (dpr - jnp.sum(p * dpr, axis=-1, keepdims=True))
                dc_ref[0, e, :, 0:width] += jnp.sum(ds, axis=0, keepdims=True)
                ds = (ds * (HEAD_DIM ** -0.5)).astype(BF16)
                dk_ref[0:width, lanes] += _dot_tn(ds, q)
                dp_ref[0, 0, rows, lanes] = _dot_nn(ds, k).astype(BF16)

        _by_causal_width(t, S, tile_grads)

        @pl.when(t == nt - 1)
        def _():
            dp_ref[1, 0] = dk_ref[...].astype(BF16)
            dp_ref[2, 0] = dv_ref[...].astype(BF16)

    seq, tile = _head_specs(S, "bht", heads)
    kv = pltpu.VMEM((S, heads * HEAD_DIM), BF16)
    acc = pltpu.VMEM((S, heads * HEAD_DIM), F32)
    return _call(
        body, name=name, grid=(B, HEADS // heads, nt),
        in_specs=[seq, pl.BlockSpec((1, TQ, cum_col.shape[2]), lambda b, h, t: (b, t, 0)),
                  pl.BlockSpec((1, HEADS, S), lambda b, h, t: (b, 0, 0)), tile, tile],
        out_specs=[seq, pl.BlockSpec((1, heads, 1, S), lambda b, h, t: (b, h, 0, 0))],
        out_shape=[jax.ShapeDtypeStruct(proj.shape, BF16), jax.ShapeDtypeStruct((B, HEADS, 1, S), F32)],
        scratch_shapes=[kv, kv, acc, acc],
        semantics=("parallel", "parallel", "arbitrary"), args=[proj, cum_col, cum_row, o, dgated], ride=ride)


RG_ROWS = 512


def _rg_gates(xc, wa_ref, ba_ref, wx_ref, bx_ref, lam_ref):
    xcb = xc.astype(BF16)
    r = _sigmoid(_dot_nn(xcb, wa_ref[0]) + ba_ref[...])
    i = _sigmoid(_dot_nn(xcb, wx_ref[0]) + bx_ref[...])
    sp = _softplus(-lam_ref[...])
    log_a = (-RG_C * sp) * r
    a = jnp.exp(log_a)
    m = jnp.sqrt(-jnp.tanh(log_a) * (a * a + 1.0))
    return xcb, r, i, sp, a, m


def _rg_specs(B, S, rows, order):
    nc = S // rows

    def idx(fn):
        def index_map(*ids):
            v = dict(zip(order.lower(), ids))
            c = (nc - 1 - v["c"]) if "C" in order else v["c"]
            return fn(v["b"], v["d"], c)
        return index_map

    return dict(
        proj=pl.BlockSpec((2, 1, rows, RG_COLS), idx(lambda b, d, c: (0, b, c, d))),
        act=pl.BlockSpec((1, rows, RG_COLS), idx(lambda b, d, c: (b, c, d))),
        taps=pl.BlockSpec((CONV_WIDTH, RG_COLS), idx(lambda b, d, c: (0, d))),
        vec=pl.BlockSpec((1, RG_COLS), idx(lambda b, d, c: (0, d))),
        gate=pl.BlockSpec((1, RG_COLS, RG_COLS), idx(lambda b, d, c: (d, 0, 0))),
    )


def rglru_fwd(proj, conv_w, conv_b, wa, ba, wx, bx, lam, name, rows=RG_ROWS, ride=None):
    _, B, S, _ = proj.shape
    rows = min(rows, S)
    sp_ = _rg_specs(B, S, rows, "bdc")

    def body(p_ref, cw_ref, cb_ref, wa_ref, ba_ref, wx_ref, bx_ref, lam_ref,
             xc_ref, hs_ref, hp_ref, gt_ref, ext_ref, a_ref, u_ref, xcar_ref, hcar_ref):
        @pl.when(pl.program_id(2) == 0)
        def _():
            xcar_ref[...] = jnp.zeros_like(xcar_ref)
            hcar_ref[...] = jnp.zeros_like(hcar_ref)

        xr = p_ref[0, 0]
        ext_ref[0:8, :] = xcar_ref[...]
        ext_ref[8:, :] = xr
        xcar_ref[...] = xr[rows - 8:, :]
        xc = ext_ref[pl.ds(5, rows), :] * cw_ref[0:1, :]
        xc = xc + ext_ref[pl.ds(6, rows), :] * cw_ref[1:2, :]
        xc = xc + ext_ref[pl.ds(7, rows), :] * cw_ref[2:3, :]
        xc = xc + xr * cw_ref[3:4, :] + cb_ref[...]
        xc_ref[0] = xc
        _, _, i, _, a, m = _rg_gates(xc, wa_ref, ba_ref, wx_ref, bx_ref, lam_ref)
        a_ref[...] = a
        u_ref[...] = m * (i * xc)

        def step(r0, carry):
            h = _group_scan(a_ref[pl.ds(r0, 8), :], u_ref[pl.ds(r0, 8), :], carry)
            row = lax.broadcasted_iota(jnp.int32, h.shape, 0)
            hs_ref[0, pl.ds(r0, 8), :] = h
            hp_ref[0, pl.ds(r0, 8), :] = jnp.where(row == 0, carry, pltpu.roll(h, 1, 0))
            return h[7:8, :]

        hcar_ref[0:1, :] = _scan_rows(rows, step, hcar_ref[0:1, :])
        gt_ref[0] = (hs_ref[0] * _silu(p_ref[1, 0])).astype(BF16)

    act = jax.ShapeDtypeStruct((B, S, RG_WIDTH), F32)
    return _call(
        body, name=name, grid=(B, RG_GROUPS, S // rows),
        in_specs=[sp_["proj"], sp_["taps"], sp_["vec"], sp_["gate"], sp_["vec"], sp_["gate"], sp_["vec"], sp_["vec"]],
        out_specs=[sp_["act"]] * 4,
        out_shape=[act, act, act, jax.ShapeDtypeStruct((B, S, RG_WIDTH), BF16)],
        scratch_shapes=[pltpu.VMEM((rows + 8, RG_COLS), F32), pltpu.VMEM((rows, RG_COLS), F32),
                        pltpu.VMEM((rows, RG_COLS), F32), pltpu.VMEM((8, RG_COLS), F32), pltpu.VMEM((8, RG_COLS), F32)],
        semantics=("parallel", "parallel", "arbitrary"), args=[proj, conv_w, conv_b, wa, ba, wx, bx, lam], ride=ride)


def rglru_bwd(proj, xc, hs, hprev, dgated, conv_w, wa, ba, wx, bx, lam, name, rows=RG_ROWS, ride=None):
    _, B, S, _ = proj.shape
    rows = min(rows, S)
    sp_ = _rg_specs(B, S, rows, "dbC")

    def body(p_ref, xc_ref, hs_ref, hp_ref, dgt_ref, cw_ref, wa_ref, ba_ref, wx_ref, bx_ref, lam_ref,
             dp_ref, dcw_ref, dcb_ref, dwa_ref, dba_ref, dwx_ref, dbx_ref, dlam_ref,
             ext_ref, c_ref, l_ref, acar_ref, lcar_ref, dcar_ref):
        b_, c_ = pl.program_id(1), pl.program_id(2)

        @pl.when(c_ == 0)
        def _():
            acar_ref[...] = jnp.zeros_like(acar_ref)
            lcar_ref[...] = jnp.zeros_like(lcar_ref)
            dcar_ref[...] = jnp.zeros_like(dcar_ref)

        @pl.when((c_ == 0) & (b_ == 0))
        def _():
            for ref in (dcw_ref, dcb_ref, dwa_ref, dba_ref, dwx_ref, dbx_ref, dlam_ref):
                ref[...] = jnp.zeros_like(ref)

        xr, g = p_ref[0, 0], p_ref[1, 0]
        xc_v = xc_ref[0]
        xcb, r, i, sp, a, m = _rg_gates(xc_v, wa_ref, ba_ref, wx_ref, bx_ref, lam_ref)
        dhs, dg = _gate_bwd(dgt_ref[0], hs_ref[0], g)
        dp_ref[1, 0] = dg.astype(BF16)

        ext_ref[0:rows, :] = a
        ext_ref[rows:, :] = acar_ref[...]
        acar_ref[...] = a[0:8, :]
        c_ref[...] = ext_ref[pl.ds(1, rows), :]
        l_ref[...] = dhs

        def step(r0, carry):
            lam_g = _group_scan(c_ref[pl.ds(r0, 8), :], l_ref[pl.ds(r0, 8), :], carry, reverse=True)
            l_ref[pl.ds(r0, 8), :] = lam_g
            return lam_g[0:1, :]

        lcar_ref[0:1, :] = _scan_rows(rows, step, lcar_ref[0:1, :], reverse=True)
        du = l_ref[...]
        da = du * hp_ref[0]
        dlog_a = da * a - (du * (i * xc_v)) * (a * a / m)
        dr = dlog_a * (-RG_C * sp)
        dsp = jnp.sum(dlog_a * (-RG_C * r), axis=0, keepdims=True)
        dlam_ref[...] += dsp * (-_sigmoid(-lam_ref[...]))
        dpa = dr * (r * (1.0 - r))
        dpx = (du * (m * xc_v)) * (i * (1.0 - i))
        dba_ref[...] += jnp.sum(dpa, axis=0, keepdims=True)
        dbx_ref[...] += jnp.sum(dpx, axis=0, keepdims=True)
        dpa, dpx = dpa.astype(BF16), dpx.astype(BF16)
        dwa_ref[0] += _dot_tn(xcb, dpa)
        dwx_ref[0] += _dot_tn(xcb, dpx)
        dxc = du * (m * i) + _dot_nt(dpa, wa_ref[0]) + _dot_nt(dpx, wx_ref[0])

        dcb_ref[...] += jnp.sum(dxc, axis=0, keepdims=True)
        ext_ref[0:rows, :] = dxc
        ext_ref[rows:, :] = dcar_ref[...]
        dcar_ref[...] = dxc[0:8, :]
        dxr = jnp.zeros_like(dxc)
        for k in range(CONV_WIDTH):
            tap = CONV_WIDTH - 1 - k
            ahead = dxc if k == 0 else ext_ref[pl.ds(k, rows), :]
            dxr = dxr + ahead * cw_ref[tap:tap + 1, :]
            dcw_ref[tap:tap + 1, :] += jnp.sum(xr * ahead, axis=0, keepdims=True)
        dp_ref[0, 0] = dxr.astype(BF16)

    vec = jax.ShapeDtypeStruct((1, RG_WIDTH), F32)
    gate = jax.ShapeDtypeStruct((RG_GROUPS, RG_COLS, RG_COLS), F32)
    return _call(
        body, name=name, grid=(RG_GROUPS, B, S // rows),
        in_specs=[sp_["proj"], sp_["act"], sp_["act"], sp_["act"], sp_["act"], sp_["taps"],
                  sp_["gate"], sp_["vec"], sp_["gate"], sp_["vec"], sp_["vec"]],
        out_specs=[sp_["proj"], sp_["taps"], sp_["vec"], sp_["gate"], sp_["vec"], sp_["gate"], sp_["vec"], sp_["vec"]],
        out_shape=[jax.ShapeDtypeStruct(proj.shape, BF16), jax.ShapeDtypeStruct((CONV_WIDTH, RG_WIDTH), F32), vec,
                   gate, vec, gate, vec, vec],
        scratch_shapes=[pltpu.VMEM((rows + 8, RG_COLS), F32), pltpu.VMEM((rows, RG_COLS), F32),
                        pltpu.VMEM((rows, RG_COLS), F32), pltpu.VMEM((8, RG_COLS), F32),
                        pltpu.VMEM((8, RG_COLS), F32), pltpu.VMEM((8, RG_COLS), F32)],
        semantics=("arbitrary", "arbitrary", "arbitrary"),
        args=[proj, xc, hs, hprev, dgated, conv_w, wa, ba, wx, bx, lam], ride=ride)


def block_diag_gates(w):
    per = RG_COLS // RG_BLOCK
    w4 = w.reshape(RG_GROUPS, per, RG_BLOCK, RG_BLOCK)
    return jnp.einsum("dipq,ij->dipjq", w4, jnp.eye(per, dtype=w.dtype)).reshape(RG_GROUPS, RG_COLS, RG_COLS)


def block_diag_gates_t(dw):
    per = RG_COLS // RG_BLOCK
    dw6 = dw.reshape(RG_GROUPS, per, RG_BLOCK, per, RG_BLOCK)
    return jnp.stack([dw6[:, i, :, i, :] for i in range(per)], axis=1).reshape(RG_BLOCKS, RG_BLOCK, RG_BLOCK)


def adamw(w, parts, m, v, name, layer=0, prev=None, part_row0=0, row_tile=ROW_TILE):
    L, R, C = w.shape
    n_parts = parts.shape[0]
    br = row_tile if R % row_tile == 0 else R

    def body(w_ref, p_ref, m_ref, v_ref, *refs):
        g_ref, d_ref, nm_ref, nv_ref = refs[-4:]
        g = p_ref[0].astype(F32)
        for k in range(1, n_parts):
            g = g + p_ref[k].astype(F32)
        nm = ADAM_B1 * m_ref[0] + (1.0 - ADAM_B1) * g
        nv = ADAM_B2 * v_ref[0] + (1.0 - ADAM_B2) * (g * g)
        m_hat = nm / (1.0 - ADAM_B1 ** ADAM_STEP)
        v_hat = nv / (1.0 - ADAM_B2 ** ADAM_STEP)
        g_ref[0] = g
        d_ref[0] = -ADAM_LR * (m_hat / (jnp.sqrt(v_hat) + ADAM_EPS) + ADAM_WD * w_ref[0])
        nm_ref[0] = nm
        nv_ref[0] = nv

    slab = pl.BlockSpec((1, br, C), lambda i: (layer, i, 0))
    out = jax.ShapeDtypeStruct((L, R, C), F32)
    carried = [] if prev is None else list(prev)
    return _call(
        body, name=name, grid=(R // br,),
        in_specs=[slab, pl.BlockSpec((n_parts, br, C), lambda i: (0, part_row0 // br + i, 0)), slab, slab]
        + [pl.BlockSpec(memory_space=pl.ANY)] * len(carried),
        out_specs=[slab] * 4, out_shape=[out] * 4, semantics=("parallel",), args=[w, parts, m, v] + carried,
        aliases={4 + k: k for k in range(len(carried))})


def _seq(a, B):
    return a.reshape(a.shape[:-2] + (B, a.shape[-2] // B, a.shape[-1]))


def _flat(a):
    return a.reshape(a.shape[:-3] + (a.shape[-3] * a.shape[-2], a.shape[-1]))


def _tiles(w, which, **default):
    return dict(default, **w.get("tiles", {}).get(which, {}))


def mixer_a_fwd(h, w, B, tag, rides):
    proj = matmul(h, w["w_in"], mode="nn", out_dtype=F32, name=f"{tag}_proj", out_slabs=4,
                  ride=rides.pop(f"{tag}_proj", None), **_tiles(w, "proj"))
    o, gated = attn_a_fwd(_seq(proj, B), w["bias"], f"{tag}_attn", heads=w.get("attn_heads", (1, 1))[0],
                          ride=rides.pop(f"{tag}_attn", None))
    return _flat(gated), dict(proj=proj, o=o)


def mixer_a_bwd(dgated, w, saved, B, tag, rides):
    dproj, dbias = attn_a_bwd(_seq(saved["proj"], B), w["bias"], saved["o"], _seq(dgated, B), f"{tag}_attn_bwd",
                              heads=w.get("attn_heads", (1, 1))[1], ride=rides.pop(f"{tag}_attn_bwd", None))
    return _flat(dproj), dict(bias=dbias)


def mixer_b_fwd(h, w, B, tag, rides):
    proj = matmul(h, w["w_in"], mode="nn", out_dtype=F32, name=f"{tag}_proj", out_slabs=2, bn=RG_COLS,
                  ride=rides.pop(f"{tag}_proj", None))
    xc, hs, hprev, gated = rglru_fwd(_seq(proj, B), w["conv_w"], w["conv_b"], w["wa"], w["ba"], w["wx"], w["bx"],
                                     w["lam"], f"{tag}_rglru", rows=2 * RG_ROWS,
                                     ride=rides.pop(f"{tag}_rglru", None))
    return _flat(gated), dict(proj=proj, xc=xc, hs=hs, hprev=hprev)


def mixer_b_bwd(dgated, w, saved, B, tag, rides):
    dproj, dcw, dcb, dwa, dba, dwx, dbx, dlam = rglru_bwd(
        _seq(saved["proj"], B), saved["xc"], saved["hs"], saved["hprev"], _seq(dgated, B),
        w["conv_w"], w["wa"], w["ba"], w["wx"], w["bx"], w["lam"], f"{tag}_rglru_bwd",
        ride=rides.pop(f"{tag}_rglru_bwd", None))
    return _flat(dproj), dict(conv_w=dcw, conv_b=dcb, wa=dwa, ba=dba, wx=dwx, bx=dbx, lam=dlam)


def mixer_c_fwd(h, w, B, tag, rides):
    proj = matmul(h, w["w_in"], mode="nn", out_dtype=F32, name=f"{tag}_proj", out_slabs=4,
                  ride=rides.pop(f"{tag}_proj", None), **_tiles(w, "proj"))
    f_logit = matmul(h, w["w_f"], mode="nn", out_dtype=F32, name=f"{tag}_fproj")
    cum = fox_cum_fwd(_seq(f_logit, B), w["f_bias"], f"{tag}_cum")
    cum_row = cum[:, :, :HEADS].transpose(0, 2, 1)
    o, gated = fox_fwd(_seq(proj, B), cum, cum_row, f"{tag}_attn", heads=w.get("attn_heads", (1, 1))[0],
                       ride=rides.pop(f"{tag}_attn", None))
    return _flat(gated), dict(proj=proj, o=o, f_logit=f_logit, cum=cum, cum_row=cum_row)


def mixer_c_bwd(dgated, w, saved, B, tag, rides):
    dproj, dck = fox_bwd(_seq(saved["proj"], B), saved["cum"], saved["cum_row"], saved["o"], _seq(dgated, B),
                         f"{tag}_attn_bwd", heads=w.get("attn_heads", (1, 1))[1],
                         ride=rides.pop(f"{tag}_attn_bwd", None))
    S = dck.shape[-1]
    dcum = jnp.pad(-dck.reshape(B, HEADS, S).transpose(0, 2, 1), ((0, 0), (0, 0), (0, HEAD_DIM - HEADS)))
    df, dfb = fox_cum_bwd(_seq(saved["f_logit"], B), w["f_bias"], dcum, f"{tag}_cum_bwd")
    return _flat(dproj), dict(f_bias=dfb, df=_flat(df).astype(BF16))


MIXERS = {"a": (mixer_a_fwd, mixer_a_bwd), "b": (mixer_b_fwd, mixer_b_bwd), "c": (mixer_c_fwd, mixer_c_bwd)}
LAYER_KINDS = "abca"


def local_step(x, target, norm_pre, norm_post, get_layer, rides, on_grads):
    B, S, Dm = x.shape
    n_layers = len(LAYER_KINDS)
    xs = [x.reshape(B * S, Dm)]
    saved, layers = [], []
    h = prenorm_fwd(xs[0], norm_pre[0:1], "l0a_prenorm")
    for li, kind in enumerate(LAYER_KINDS):
        tag = f"l{li}{kind}"
        w = get_layer(li)
        gated, sv = MIXERS[kind][0](h, w, B, tag, rides)
        if callable(w["w_out"]):
            w["w_out"] = w["w_out"]()
        y = matmul(gated, w["w_out"], mode="nn", out_dtype=F32, name=f"{tag}_out", ride=rides.pop(f"{tag}_out", None))
        saved.append(dict(sv, h=h, gated=gated, y=y))
        layers.append(w)
        if li + 1 < n_layers:
            x_new, h = postnorm_prenorm_fwd(xs[-1], y, norm_post[li:li + 1], norm_pre[li + 1:li + 2],
                                            f"{tag}_postnorm")
            xs.append(x_new)
    loss, dx = postnorm_loss(xs[-1], y, norm_post[n_layers - 1:], target.reshape(B * S, Dm), "loss")

    for li in reversed(range(n_layers)):
        kind, w, sv = LAYER_KINDS[li], layers[li], saved[li]
        tag = f"l{li}{kind}"
        dy, dg_post = postnorm_bwd(sv["y"], norm_post[li:li + 1], dx, f"{tag}_postnorm_bwd")
        on_grads(li, "norm_post", dg_post)
        on_grads(li, "w_out", matmul(sv["gated"], dy, mode="tn", out_dtype=BF16, name=f"{tag}_dwout",
                                     ride=rides.pop(f"{tag}_dwout", None)))
        dgated = matmul(dy, w["w_out"], mode="nt", out_dtype=F32, name=f"{tag}_dgated",
                        ride=rides.pop(f"{tag}_dgated", None))
        dproj, gw = MIXERS[kind][1](dgated, w, sv, B, tag, rides)
        df = gw.pop("df", None)
        for name, value in gw.items():
            on_grads(li, name, value)
        parts = w.get("dwin_parts", [(0, 1, 1)])
        for i, m_part in enumerate(parts):
            suffix = f"_{i}" if len(parts) > 1 else ""
            on_grads(li, "w_in" + suffix,
                     matmul(sv["h"], dproj, mode="tn", out_dtype=BF16, name=f"{tag}_dwin{suffix}", m_part=m_part,
                            out_slabs=w["grad_slabs"], ride=rides.pop(f"{tag}_dwin{suffix}", None),
                            **_tiles(w, "dwin")))
        if df is not None:
            on_grads(li, "w_f", matmul(sv["h"], df, mode="tn", out_dtype=BF16, name=f"{tag}_dwf"))
        dhs = [matmul(dproj, w["w_in"], mode="nt", out_dtype=F32, name=f"{tag}_dh",
                      ride=rides.pop(f"{tag}_dh", None), **_tiles(w, "dh"))]
        if df is not None:
            dhs.append(matmul(df, w["w_f"], mode="nt", out_dtype=F32, name=f"{tag}_dhf"))
        dx, dg_pre = prenorm_bwd(xs[li], norm_pre[li:li + 1], dhs, dx, f"{tag}_prenorm_bwd")
        on_grads(li, "norm_pre", dg_pre)
    assert not rides, list(rides)
    return loss, dx.reshape(B, S, Dm)


WEIGHTS = ("norm_pre", "norm_post", "a_w_in", "a_rel_bias", "a_w_out", "b_w_in", "b_conv_w", "b_conv_b",
           "b_gate_a_w", "b_gate_a_b", "b_gate_x_w", "b_gate_x_b", "b_lambda", "b_w_out", "c_w_in", "c_f_bias",
           "c_w_out")
C_SHARD = (4 * D_MODEL + HEADS) // N_DEV


def _rows(gathered):
    return gathered.reshape(gathered.shape[0] * gathered.shape[1], gathered.shape[2])


def layer_a(w_in, w_out, rel_bias):
    return dict(w_in=w_in, w_out=w_out if callable(w_out) else _rows(w_out), bias=band_bias(rel_bias),
                grad_slabs=N_DEV)


def layer_b(w_in, w_out, conv_w, small):
    return dict(
        w_in=w_in, w_out=w_out if callable(w_out) else _rows(w_out), grad_slabs=N_DEV,
        conv_w=conv_w.transpose(1, 0, 2).reshape(CONV_WIDTH, RG_WIDTH),
        conv_b=small["b_conv_b"], lam=small["b_lambda"],
        wa=block_diag_gates(small["b_gate_a_w"][0]).astype(BF16), ba=small["b_gate_a_b"].reshape(1, RG_WIDTH),
        wx=block_diag_gates(small["b_gate_x_w"][0]).astype(BF16), bx=small["b_gate_x_b"].reshape(1, RG_WIDTH))


def layer_c(w_in, w_out, small):
    full = w_in.transpose(1, 0, 2).reshape(D_MODEL, N_DEV * C_SHARD)
    return dict(w_in=full[:, :4 * D_MODEL], w_f=jnp.pad(full[:, 4 * D_MODEL:], ((0, 0), (0, HEAD_DIM - HEADS))),
                w_out=_rows(w_out), grad_slabs=1,
                f_bias=jnp.pad(small["c_f_bias"], ((0, 0), (0, HEAD_DIM - HEADS))))


def c_w_in_blocks(dmain, df):
    full = jnp.concatenate([dmain, df[:, :HEADS].astype(dmain.dtype)], axis=1)
    return full.reshape(D_MODEL, N_DEV, C_SHARD).transpose(1, 0, 2)


def _row_blocks(g):
    return g.reshape(N_DEV, g.shape[0] // N_DEV, g.shape[1])


PACK_LANES = 128
PACK_ALIGN = 8 * PACK_LANES


def pack(parts):
    flat = []
    for p in parts:
        n = p.size
        flat.append(jnp.pad(p.reshape(n), (0, -n % PACK_ALIGN)).reshape(-1, PACK_LANES))
    rows = sum(f.shape[0] for f in flat)
    flat.append(jnp.zeros((-rows % ROW_TILE, PACK_LANES), F32))
    return jnp.concatenate(flat, axis=0)


def unpack(packed, shapes):
    out, row = [], 0
    for shape in shapes:
        n = 1
        for s in shape:
            n *= s
        n_rows = (n + PACK_ALIGN - 1) // PACK_ALIGN * 8
        out.append(packed[row:row + n_rows].reshape(-1)[:n].reshape(shape))
        row += n_rows
    return out


LATE = (("a_rel_bias", slice(0, 1)), ("norm_pre", slice(0, 2)), ("norm_post", slice(0, 1)))
EARLY = (("a_rel_bias", slice(1, 2)), ("norm_pre", slice(2, 4)), ("norm_post", slice(1, 4)),
         ("b_conv_b", slice(None)), ("b_gate_a_w", slice(None)), ("b_gate_a_b", slice(None)),
         ("b_gate_x_w", slice(None)), ("b_gate_x_b", slice(None)), ("b_lambda", slice(None)),
         ("c_f_bias", slice(None)))


def _pieces(tree, pieces):
    return [tree[name][sl] for name, sl in pieces]


def kernel(x, norm_pre, norm_post, a_w_in, a_rel_bias, a_w_out, b_w_in, b_conv_w, b_conv_b, b_gate_a_w, b_gate_a_b, b_gate_x_w, b_gate_x_b, b_lambda, b_w_out, c_w_in, c_f_bias, c_w_out, loss_target, m_norm_pre, m_norm_post, m_a_w_in, m_a_rel_bias, m_a_w_out, m_b_w_in, m_b_conv_w, m_b_conv_b, m_b_gate_a_w, m_b_gate_a_b, m_b_gate_x_w, m_b_gate_x_b, m_b_lambda, m_b_w_out, m_c_w_in, m_c_f_bias, m_c_w_out, v_norm_pre, v_norm_post, v_a_w_in, v_a_rel_bias, v_a_w_out, v_b_w_in, v_b_conv_w, v_b_conv_b, v_b_gate_a_w, v_b_gate_a_b, v_b_gate_x_w, v_b_gate_x_b, v_b_lambda, v_b_w_out, v_c_w_in, v_c_f_bias, v_c_w_out):
    args = dict(locals())
    w = {n: args[n] for n in WEIGHTS}
    m = {n: args["m_" + n] for n in WEIGHTS}
    v = {n: args["v_" + n] for n in WEIGHTS}

    a_in, a_out = a_w_in.astype(BF16), a_w_out.astype(BF16)
    gather_a0 = Ride([a_in[0]], scatter=False, via_sibling=True)
    in_l0_proj = Ride([b_w_in[0].astype(BF16), b_conv_w[0], a_out[0]], scatter=False, via_sibling=True)
    in_l0_attn = Ride([c_w_in[0].astype(BF16)], scatter=False, via_sibling=True)
    in_l1_proj = Ride([b_w_out[0].astype(BF16)], scatter=False)
    in_l1_rglru = Ride([c_w_out[0].astype(BF16)], scatter=False)
    in_l2_proj = Ride([a_out[1]], scatter=False)
    in_l2_attn = Ride([a_in[1]], scatter=False, via_sibling=True)
    exchange(gather_a0, "gather_l0")
    rides = {"l0a_proj": in_l0_proj, "l0a_attn": in_l0_attn, "l1b_proj": in_l1_proj, "l1b_rglru": in_l1_rglru,
             "l2c_proj": in_l2_proj, "l2c_attn": in_l2_attn}

    def get_layer(li):
        if li == 0:
            return dict(layer_a(gather_a0.out[0], lambda: _rows(in_l0_proj.out[2]), a_rel_bias[0]),
                        dwin_parts=[(0, 1, 4), (1, 1, 4), (2, 2, 4)], attn_heads=(4, 2))
        if li == 1:
            return layer_b(in_l0_proj.out[0], lambda: _rows(in_l1_proj.out[0]), in_l0_proj.out[1], w)
        if li == 2:
            return dict(layer_c(in_l0_attn.out[0], in_l1_rglru.out[0], w), tiles=dict(proj=dict(bn=2048)),
                        attn_heads=(4, 2))
        return dict(layer_a(in_l2_attn.out[0], in_l2_proj.out[0], a_rel_bias[1]), attn_heads=(4, 2))

    grads = [dict() for _ in LAYER_KINDS]
    scatters = {}

    def rel_bias_grad(j, dbias):
        return jax.vjp(band_bias, a_rel_bias[j])[1](dbias)[0][None]

    def early_partial():
        gb, gc = grads[1], grads[2]
        tree = dict(
            a_rel_bias=jnp.concatenate([jnp.zeros((1, HEADS, N_REL), F32), rel_bias_grad(1, grads[3]["bias"])]),
            norm_pre=jnp.concatenate([jnp.zeros((2, D_MODEL), F32)] + [grads[li]["norm_pre"] for li in (2, 3)]),
            norm_post=jnp.concatenate([jnp.zeros((1, D_MODEL), F32)] + [grads[li]["norm_post"] for li in (1, 2, 3)]),
            b_conv_b=gb["conv_b"], b_lambda=gb["lam"],
            b_gate_a_w=block_diag_gates_t(gb["wa"])[None], b_gate_a_b=gb["ba"].reshape(1, RG_BLOCKS, RG_BLOCK),
            b_gate_x_w=block_diag_gates_t(gb["wx"])[None], b_gate_x_b=gb["bx"].reshape(1, RG_BLOCKS, RG_BLOCK),
            c_f_bias=gc["f_bias"][:, :HEADS])
        return pack(_pieces(tree, EARLY))

    def send(key, host, blocks, scatter=True, via_sibling=False):
        ride = rides.setdefault(host, Ride([], scatter, via_sibling))
        assert (ride.scatter, ride.via_sibling) == (scatter, via_sibling)
        scatters[key] = (ride, len(ride.arrs))
        ride.arrs.append(blocks)

    def on_grads(li, name, value):
        g = grads[li]
        g[name] = value
        if (li, name) == (3, "w_out"):
            send("a1_out", "l3a_attn_bwd", _row_blocks(value))
        elif (li, name) == (3, "w_in"):
            send("a1_in", "l2c_attn_bwd", value)
        elif (li, name) == (2, "w_out"):
            send("c_out", "l2c_attn_bwd", _row_blocks(value))
        elif (li, name) == (2, "w_f"):
            blocks = c_w_in_blocks(g["w_in"], value)
            send("c_in_0", "l2c_dh", blocks[:, :D_MODEL // 2])
            send("c_in_1", "l1b_rglru_bwd", blocks[:, D_MODEL // 2:])
        elif (li, name) == (1, "w_out"):
            send("b_out", "l1b_dh", _row_blocks(value))
        elif (li, name) == (1, "w_in"):
            send("b_in", "l0a_attn_bwd", value)
            send("b_conv", "l0a_attn_bwd",
                 g["conv_w"].reshape(CONV_WIDTH, N_DEV, RG_WIDTH // N_DEV).transpose(1, 0, 2))
        elif (li, name) == (1, "lam"):
            send("early", "l1b_dwin", early_partial(), scatter=False, via_sibling=True)
        elif (li, name) == (0, "w_out"):
            send("a0_out", "l0a_attn_bwd", _row_blocks(value))
        elif (li, name) == (0, "w_in_0"):
            send("a0_in_0", "l0a_dwin_1", value)
        elif (li, name) == (0, "w_in_1"):
            send("a0_in_1", "l0a_dwin_2", value)
        elif (li, name) == (0, "w_in_2"):
            send("a0_in_2", "l0a_dh", value)

    loss, grad_x = local_step(x, loss_target, norm_pre, norm_post, get_layer, rides, on_grads)
    late_tree = dict(a_rel_bias=rel_bias_grad(0, grads[0]["bias"]), norm_post=grads[0]["norm_post"],
                     norm_pre=jnp.concatenate([grads[0]["norm_pre"], grads[1]["norm_pre"]]))
    late_parts = exchange(Ride([pack([late_tree[n] for n, _ in LATE])], scatter=False), "gather_late_grads")[0]

    def sharded(name, slab_parts):
        shape = w[name].shape
        slabs = (len(slab_parts), shape[0] * shape[1] // len(slab_parts), shape[2])
        outs = None
        for j, (parts, row0) in enumerate(slab_parts):
            outs = adamw(w[name].reshape(slabs), parts, m[name].reshape(slabs), v[name].reshape(slabs),
                         f"adamw_{name}_{j}", layer=j, prev=outs, part_row0=row0)
        return [o.reshape(shape) for o in outs]

    def received(key):
        ride, position = scatters[key]
        return ride.out[position]

    res = dict(
        a_w_in=sharded("a_w_in", [(received("a0_in_0"), 0), (received("a0_in_1"), 0), (received("a0_in_2"), 0),
                                  (received("a0_in_2"), D_MODEL // 4)]
                       + [(received("a1_in"), q * D_MODEL // 4) for q in range(4)]),
        a_w_out=sharded("a_w_out", [(received("a0_out"), 0), (received("a1_out"), 0)]),
        b_w_in=sharded("b_w_in", [(received("b_in"), 0)]),
        b_w_out=sharded("b_w_out", [(received("b_out"), 0)]),
        b_conv_w=sharded("b_conv_w", [(received("b_conv"), 0)]),
        c_w_in=sharded("c_w_in", [(received("c_in_0"), 0), (received("c_in_1"), 0)]),
        c_w_out=sharded("c_w_out", [(received("c_out"), 0)]))

    packed = {}
    for label, pieces, parts in (("early", EARLY, received("early")), ("late", LATE, late_parts)):
        outs = adamw(pack(_pieces(w, pieces))[None], parts, pack(_pieces(m, pieces))[None],
                     pack(_pieces(v, pieces))[None], f"adamw_replicated_{label}")
        shapes = [w[n][sl].shape for n, sl in pieces]
        packed[label] = [dict(zip([n for n, _ in pieces], unpack(o[0], shapes))) for o in outs]
    for n in ("b_conv_b", "b_gate_a_w", "b_gate_a_b", "b_gate_x_w", "b_gate_x_b", "b_lambda", "c_f_bias"):
        res[n] = [packed["early"][k][n] for k in range(4)]
    for n in ("a_rel_bias", "norm_pre", "norm_post"):
        res[n] = [jnp.concatenate([packed["late"][k][n], packed["early"][k][n]]) for k in range(4)]

    total = lax.psum(loss[0, 0], ("x", "y", "c"))
    return (total, grad_x, *[res[n][0] for n in WEIGHTS], *[res[n][1] for n in WEIGHTS],
            *[res[n][2] for n in WEIGHTS], *[res[n][3] for n in WEIGHTS])
```
